```python
import jax, jax.numpy as jnp
from jax import lax
import numpy as np

D_MODEL = 1024
BATCH = 8
SEQ = 4096
DEPTH = 4

CHUNK = 64
RET_HEADS = 8
RET_QK_DIM = 64
RET_V_DIM = 128
RET_QK_WIDTH = RET_HEADS * RET_QK_DIM
RET_V_WIDTH = RET_HEADS * RET_V_DIM
CONV_WIDTH = D_MODEL
CONV_KERNEL = 31
ROPE_BASE = 10000.0
NORM_EPS = 1e-6
IN_SIZES = (
    RET_QK_WIDTH,
    RET_QK_WIDTH,
    RET_V_WIDTH,
    RET_V_WIDTH,
    CONV_WIDTH,
    CONV_WIDTH,
    CONV_WIDTH,
    D_MODEL,
    D_MODEL,
)
IN_WIDTH = sum(IN_SIZES)

kernel_name = "hybrid_retention_conformer_conv_block"


def _split_points():
    pts, acc = [], 0
    for s in IN_SIZES[:-1]:
        acc += s
        pts.append(acc)
    return pts


def rms_norm(x, g):
    xf = x.astype(jnp.float32)
    y = xf * lax.rsqrt(jnp.mean(xf * xf, axis=-1, keepdims=True) + NORM_EPS)
    return (y * g.astype(jnp.float32)).astype(x.dtype)


def layer_norm(x, g, b):
    xf = x.astype(jnp.float32)
    mu = jnp.mean(xf, axis=-1, keepdims=True)
    var = jnp.mean(jnp.square(xf - mu), axis=-1, keepdims=True)
    y = (xf - mu) * lax.rsqrt(var + NORM_EPS)
    return (y * g.astype(jnp.float32) + b.astype(jnp.float32)).astype(x.dtype)


def rotary(t, pos):
    half = t.shape[-1] // 2
    inv = ROPE_BASE ** (-jnp.arange(half, dtype=jnp.float32) / half)
    ang = pos[:, None] * inv[None, :]
    cos = jnp.cos(ang)[None, :, None, :]
    sin = jnp.sin(ang)[None, :, None, :]
    t1, t2 = t[..., :half], t[..., half:]
    return jnp.concatenate([t1 * cos - t2 * sin, t1 * sin + t2 * cos], axis=-1)


def chunk_retention(q, k, v):
    B, S, H, dk = q.shape
    dv = v.shape[-1]
    n = S // CHUNK
    qc = q.reshape(B, n, CHUNK, H, dk)
    kc = k.reshape(B, n, CHUNK, H, dk)
    vc = v.reshape(B, n, CHUNK, H, dv)
    log_g = jnp.log1p(-jnp.exp2(-5.0 - jnp.arange(H, dtype=jnp.float32)))
    idx = jnp.arange(CHUNK, dtype=jnp.float32)
    intra_decay = jnp.exp(log_g[:, None, None] * jnp.abs(idx[:, None] - idx[None, :]))
    scores = jnp.einsum('bnahd,bnjhd->bnhaj', qc, kc) * intra_decay
    intra = jnp.einsum('bnhaj,bnjhe->bnahe', scores, vc)
    k_decay = jnp.exp(log_g[None, :] * (CHUNK - idx[:, None]))
    kv = jnp.einsum('bnjhd,jh,bnjhe->nbhde', kc, k_decay, vc)
    chunk_decay = jnp.exp(log_g * CHUNK)[None, :, None, None]

    def step(state, kv_i):
        return state * chunk_decay + kv_i, state

    _, states = lax.scan(step, jnp.zeros((B, H, dk, dv), jnp.float32), kv)
    q_decay = jnp.exp(log_g[None, :] * idx[:, None])
    cross = jnp.einsum('bnahd,ah,nbhde->bnahe', qc, q_decay, states)
    return (intra + cross).reshape(B, S, H, dv)


def causal_depthwise_conv(u, w, b):
    K = w.shape[0]
    up = jnp.pad(u, ((0, 0), (K - 1, 0), (0, 0)))
    y = lax.conv_general_dilated(up, w[:, None, :].astype(u.dtype), window_strides=(1,), padding='VALID',
                                 dimension_numbers=('NWC', 'WIO', 'NWC'),
                                 feature_group_count=u.shape[-1])
    return y + b.astype(u.dtype)


def hybrid_layer(x, pre_g, w_in, w_ret_out, conv_w, conv_b, ln_g, ln_b, w_conv_out, w_o, post_g):
    B, S, _ = x.shape
    h = rms_norm(x, pre_g)
    proj = h @ w_in
    q, k, v, g_ret, glu_a, glu_b, g_conv, m_ret, m_conv = jnp.split(proj, _split_points(), axis=-1)

    pos = jnp.arange(S, dtype=jnp.float32)
    qh = rotary(q.astype(jnp.float32).reshape(B, S, RET_HEADS, RET_QK_DIM), pos) * (RET_QK_DIM ** -0.5)
    kh = rotary(k.astype(jnp.float32).reshape(B, S, RET_HEADS, RET_QK_DIM), pos)
    vh = v.astype(jnp.float32).reshape(B, S, RET_HEADS, RET_V_DIM)
    r = chunk_retention(qh, kh, vh)
    mu = jnp.mean(r, axis=-1, keepdims=True)
    var = jnp.mean(jnp.square(r - mu), axis=-1, keepdims=True)
    r = ((r - mu) * lax.rsqrt(var + NORM_EPS)).reshape(B, S, RET_V_WIDTH).astype(x.dtype)
    ret_out = (r * jax.nn.silu(g_ret)) @ w_ret_out

    u = glu_a * jax.nn.sigmoid(glu_b)
    c = causal_depthwise_conv(u, conv_w, conv_b)
    c = jax.nn.silu(layer_norm(c, ln_g, ln_b))
    conv_out = (c * jax.nn.silu(g_conv)) @ w_conv_out

    y = jax.nn.sigmoid(m_ret) * ret_out + jax.nn.sigmoid(m_conv) * conv_out
    y = y @ w_o
    return x + rms_norm(y, post_g)


def _fwd_setup_inputs(seed: int = 0) -> dict:
    key = jax.random.key(seed)
    ks = jax.random.split(key, 12)
    f = jnp.float32
    x = jax.random.normal(ks[0], (BATCH, SEQ, D_MODEL), f)
    pre_norm_g = 1.0 + 0.05 * jax.random.normal(ks[1], (DEPTH, D_MODEL), f)
    w_in = jax.random.normal(ks[2], (DEPTH, D_MODEL, IN_WIDTH), f) * D_MODEL ** -0.5
    w_ret_out = jax.random.normal(ks[3], (DEPTH, RET_V_WIDTH, D_MODEL), f) * RET_V_WIDTH ** -0.5
    conv_w = jax.random.normal(ks[4], (DEPTH, CONV_KERNEL, CONV_WIDTH), f) * CONV_KERNEL ** -0.5
    conv_b = 0.02 * jax.random.normal(ks[5], (DEPTH, CONV_WIDTH), f)
    conv_ln_g = 1.0 + 0.05 * jax.random.normal(ks[6], (DEPTH, CONV_WIDTH), f)
    conv_ln_b = 0.02 * jax.random.normal(ks[7], (DEPTH, CONV_WIDTH), f)
    w_conv_out = jax.random.normal(ks[8], (DEPTH, CONV_WIDTH, D_MODEL), f) * CONV_WIDTH ** -0.5
    w_o = jax.random.normal(ks[9], (DEPTH, D_MODEL, D_MODEL), f) * D_MODEL ** -0.5
    post_norm_g = 1.0 + 0.05 * jax.random.normal(ks[10], (DEPTH, D_MODEL), f)
    return {"x": x, "pre_norm_g": pre_norm_g, "w_in": w_in, "w_ret_out": w_ret_out,
            "conv_w": conv_w, "conv_b": conv_b, "conv_ln_g": conv_ln_g, "conv_ln_b": conv_ln_b,
            "w_conv_out": w_conv_out, "w_o": w_o, "post_norm_g": post_norm_g}


def _fwd_reference(x, pre_norm_g, w_in, w_ret_out, conv_w, conv_b, conv_ln_g, conv_ln_b, w_conv_out, w_o, post_norm_g):
    h = x
    for l in range(DEPTH):
        h = hybrid_layer(h, pre_norm_g[l], w_in[l], w_ret_out[l], conv_w[l], conv_b[l],
                         conv_ln_g[l], conv_ln_b[l], w_conv_out[l], w_o[l], post_norm_g[l])
    return h


import jax as _jax
import jax.numpy as _jnp

TWIN_FORMAT = 'train_step'
FWD_PARAMS = ['x', 'pre_norm_g', 'w_in', 'w_ret_out', 'conv_w', 'conv_b', 'conv_ln_g', 'conv_ln_b', 'w_conv_out', 'w_o', 'post_norm_g']
TWIN_WEIGHTS = ['pre_norm_g', 'w_in', 'w_ret_out', 'conv_w', 'conv_b', 'conv_ln_g', 'conv_ln_b', 'w_conv_out', 'w_o', 'post_norm_g']
TWIN_DIFF_INPUT = 'x'
TWIN_INPUTS = ['x', 'pre_norm_g', 'w_in', 'w_ret_out', 'conv_w', 'conv_b', 'conv_ln_g', 'conv_ln_b', 'w_conv_out', 'w_o', 'post_norm_g', 'loss_target', 'm_pre_norm_g', 'm_w_in', 'm_w_ret_out', 'm_conv_w', 'm_conv_b', 'm_conv_ln_g', 'm_conv_ln_b', 'm_w_conv_out', 'm_w_o', 'm_post_norm_g', 'v_pre_norm_g', 'v_w_in', 'v_w_ret_out', 'v_conv_w', 'v_conv_b', 'v_conv_ln_g', 'v_conv_ln_b', 'v_w_conv_out', 'v_w_o', 'v_post_norm_g']
TWIN_OUTPUTS = ['loss', 'grad_x', 'grad_pre_norm_g', 'grad_w_in', 'grad_w_ret_out', 'grad_conv_w', 'grad_conv_b', 'grad_conv_ln_g', 'grad_conv_ln_b', 'grad_w_conv_out', 'grad_w_o', 'grad_post_norm_g', 'delta_pre_norm_g', 'delta_w_in', 'delta_w_ret_out', 'delta_conv_w', 'delta_conv_b', 'delta_conv_ln_g', 'delta_conv_ln_b', 'delta_w_conv_out', 'delta_w_o', 'delta_post_norm_g', 'new_m_pre_norm_g', 'new_m_w_in', 'new_m_w_ret_out', 'new_m_conv_w', 'new_m_conv_b', 'new_m_conv_ln_g', 'new_m_conv_ln_b', 'new_m_w_conv_out', 'new_m_w_o', 'new_m_post_norm_g', 'new_v_pre_norm_g', 'new_v_w_in', 'new_v_w_ret_out', 'new_v_conv_w', 'new_v_conv_b', 'new_v_conv_ln_g', 'new_v_conv_ln_b', 'new_v_w_conv_out', 'new_v_w_o', 'new_v_post_norm_g']
TWIN_LEAF_KINDS = {'loss': 'loss', 'grad_x': 'grad_x', 'grad_pre_norm_g': 'grad_w', 'grad_w_in': 'grad_w', 'grad_w_ret_out': 'grad_w', 'grad_conv_w': 'grad_w', 'grad_conv_b': 'grad_w', 'grad_conv_ln_g': 'grad_w', 'grad_conv_ln_b': 'grad_w', 'grad_w_conv_out': 'grad_w', 'grad_w_o': 'grad_w', 'grad_post_norm_g': 'grad_w', 'delta_pre_norm_g': 'delta_w', 'delta_w_in': 'delta_w', 'delta_w_ret_out': 'delta_w', 'delta_conv_w': 'delta_w', 'delta_conv_b': 'delta_w', 'delta_conv_ln_g': 'delta_w', 'delta_conv_ln_b': 'delta_w', 'delta_w_conv_out': 'delta_w', 'delta_w_o': 'delta_w', 'delta_post_norm_g': 'delta_w', 'new_m_pre_norm_g': 'new_m', 'new_m_w_in': 'new_m', 'new_m_w_ret_out': 'new_m', 'new_m_conv_w': 'new_m', 'new_m_conv_b': 'new_m', 'new_m_conv_ln_g': 'new_m', 'new_m_conv_ln_b': 'new_m', 'new_m_w_conv_out': 'new_m', 'new_m_w_o': 'new_m', 'new_m_post_norm_g': 'new_m', 'new_v_pre_norm_g': 'new_v', 'new_v_w_in': 'new_v', 'new_v_w_ret_out': 'new_v', 'new_v_conv_w': 'new_v', 'new_v_conv_b': 'new_v', 'new_v_conv_ln_g': 'new_v', 'new_v_conv_ln_b': 'new_v', 'new_v_w_conv_out': 'new_v', 'new_v_w_o': 'new_v', 'new_v_post_norm_g': 'new_v'}


def _forward(args):
    return _fwd_reference(*[args[k] for k in FWD_PARAMS])


def _output_shape():
    out = _jax.eval_shape(lambda: _forward(_fwd_setup_inputs(0)))
    return out.shape, out.dtype

N_MICROBATCH = 1
ADAM_LR = 0.001
ADAM_B1 = 0.9
ADAM_B2 = 0.999
ADAM_EPS = 1e-08
ADAM_WD = 0.01
ADAM_STEP = 10
PER_EXAMPLE_BATCH_AXIS = {'x': 0, 'loss_target': 0}
SHARED_INPUTS = []
_WEIGHT_DTYPES = {'pre_norm_g': _jnp.float32, 'w_in': _jnp.float32, 'w_ret_out': _jnp.float32, 'conv_w': _jnp.float32, 'conv_b': _jnp.float32, 'conv_ln_g': _jnp.float32, 'conv_ln_b': _jnp.float32, 'w_conv_out': _jnp.float32, 'w_o': _jnp.float32, 'post_norm_g': _jnp.float32}
MOMENT_SCALE = {'pre_norm_g': 1.714225e+00, 'w_in': 6.035562e-01, 'w_ret_out': 7.507740e-01, 'conv_w': 4.649142e-01, 'conv_b': 1.088840e+00, 'conv_ln_g': 6.148510e-01, 'conv_ln_b': 6.256016e-01, 'w_conv_out': 4.667588e-01, 'w_o': 8.919281e-01, 'post_norm_g': 3.173237e+01}


def _to_microbatches(a, axis):
    t = _jnp.moveaxis(a, axis, 0)
    t = t.reshape((N_MICROBATCH, t.shape[0] // N_MICROBATCH) + t.shape[1:])
    return _jnp.moveaxis(t, 1, axis + 1)


def setup_inputs(seed: int = 0) -> dict:
    inp = _fwd_setup_inputs(seed)
    key = _jax.random.fold_in(_jax.random.key(seed), 7919)
    shape, _ = _output_shape()
    out = dict(inp)
    out["loss_target"] = _jax.random.normal(_jax.random.fold_in(key, 0), shape, _jnp.float32)
    for i, name in enumerate(TWIN_WEIGHTS):
        w = inp[name].astype(_jnp.float32)
        if MOMENT_SCALE is None:
            s = _jnp.sqrt(_jnp.mean(_jnp.square(w)) + 1e-30)
        else:
            s = MOMENT_SCALE[name]
        km, kv = _jax.random.split(_jax.random.fold_in(key, i + 1))
        out[name] = w
        out["m_" + name] = s * _jax.random.normal(km, w.shape, _jnp.float32)
        out["v_" + name] = (s * s) * _jax.random.uniform(kv, w.shape, _jnp.float32, 0.5, 1.5)
    if N_MICROBATCH > 1:
        for name, axis in PER_EXAMPLE_BATCH_AXIS.items():
            out[name] = _to_microbatches(out[name], axis)
    return {'x': out['x'], 'pre_norm_g': out['pre_norm_g'], 'w_in': out['w_in'], 'w_ret_out': out['w_ret_out'], 'conv_w': out['conv_w'], 'conv_b': out['conv_b'], 'conv_ln_g': out['conv_ln_g'], 'conv_ln_b': out['conv_ln_b'], 'w_conv_out': out['w_conv_out'], 'w_o': out['w_o'], 'post_norm_g': out['post_norm_g'], 'loss_target': out['loss_target'], 'm_pre_norm_g': out['m_pre_norm_g'], 'm_w_in': out['m_w_in'], 'm_w_ret_out': out['m_w_ret_out'], 'm_conv_w': out['m_conv_w'], 'm_conv_b': out['m_conv_b'], 'm_conv_ln_g': out['m_conv_ln_g'], 'm_conv_ln_b': out['m_conv_ln_b'], 'm_w_conv_out': out['m_w_conv_out'], 'm_w_o': out['m_w_o'], 'm_post_norm_g': out['m_post_norm_g'], 'v_pre_norm_g': out['v_pre_norm_g'], 'v_w_in': out['v_w_in'], 'v_w_ret_out': out['v_w_ret_out'], 'v_conv_w': out['v_conv_w'], 'v_conv_b': out['v_conv_b'], 'v_conv_ln_g': out['v_conv_ln_g'], 'v_conv_ln_b': out['v_conv_ln_b'], 'v_w_conv_out': out['v_w_conv_out'], 'v_w_o': out['v_w_o'], 'v_post_norm_g': out['v_post_norm_g']}


def _loss(weights, diff, rest, loss_target):
    with _jax.named_scope("forward"):
        args = {**rest, TWIN_DIFF_INPUT: diff, **{k: w.astype(_WEIGHT_DTYPES[k]) for k, w in weights.items()}}
        y = _forward(args)
    with _jax.named_scope("loss_head"):
        err = _jnp.square(y.astype(_jnp.float32) - loss_target)
        return 0.5 * _jnp.sum(_jnp.mean(err, axis=-1)) if err.ndim else 0.5 * err


def _adamw(w, g, m, v):
    m = ADAM_B1 * m + (1.0 - ADAM_B1) * g
    v = ADAM_B2 * v + (1.0 - ADAM_B2) * _jnp.square(g)
    m_hat = m / (1.0 - ADAM_B1 ** ADAM_STEP)
    v_hat = v / (1.0 - ADAM_B2 ** ADAM_STEP)
    delta = -ADAM_LR * (m_hat / (_jnp.sqrt(v_hat) + ADAM_EPS) + ADAM_WD * w)
    return delta, m, v


def reference(x, pre_norm_g, w_in, w_ret_out, conv_w, conv_b, conv_ln_g, conv_ln_b, w_conv_out, w_o, post_norm_g, loss_target, m_pre_norm_g, m_w_in, m_w_ret_out, m_conv_w, m_conv_b, m_conv_ln_g, m_conv_ln_b, m_w_conv_out, m_w_o, m_post_norm_g, v_pre_norm_g, v_w_in, v_w_ret_out, v_conv_w, v_conv_b, v_conv_ln_g, v_conv_ln_b, v_w_conv_out, v_w_o, v_post_norm_g):
    given = dict(x=x, pre_norm_g=pre_norm_g, w_in=w_in, w_ret_out=w_ret_out, conv_w=conv_w, conv_b=conv_b, conv_ln_g=conv_ln_g, conv_ln_b=conv_ln_b, w_conv_out=w_conv_out, w_o=w_o, post_norm_g=post_norm_g, loss_target=loss_target, m_pre_norm_g=m_pre_norm_g, m_w_in=m_w_in, m_w_ret_out=m_w_ret_out, m_conv_w=m_conv_w, m_conv_b=m_conv_b, m_conv_ln_g=m_conv_ln_g, m_conv_ln_b=m_conv_ln_b, m_w_conv_out=m_w_conv_out, m_w_o=m_w_o, m_post_norm_g=m_post_norm_g, v_pre_norm_g=v_pre_norm_g, v_w_in=v_w_in, v_w_ret_out=v_w_ret_out, v_conv_w=v_conv_w, v_conv_b=v_conv_b, v_conv_ln_g=v_conv_ln_g, v_conv_ln_b=v_conv_ln_b, v_w_conv_out=v_w_conv_out, v_w_o=v_w_o, v_post_norm_g=v_post_norm_g)
    weights = {n: given[n] for n in TWIN_WEIGHTS}
    shared = {n: given[n] for n in SHARED_INPUTS}
    per_example = {n: given[n] for n in ['x']}
    grad_fn = _jax.value_and_grad(_loss, argnums=(0, 1))

    def one_microbatch(ex, loss_target):
        ex = dict(ex)
        diff = ex.pop(TWIN_DIFF_INPUT)
        return grad_fn(weights, diff, {**shared, **ex}, loss_target)

    if N_MICROBATCH == 1:
        loss, (grad_w, grad_x) = one_microbatch(per_example, given["loss_target"])
    else:
        def body(carry, xs):
            loss_sum, grad_sum = carry
            l_k, (gw_k, gx_k) = one_microbatch(xs[0], xs[1])
            with _jax.named_scope("update"):
                return (loss_sum + l_k, _jax.tree.map(_jnp.add, grad_sum, gw_k)), gx_k

        init = (_jnp.zeros((), _jnp.float32), _jax.tree.map(_jnp.zeros_like, weights))
        (loss, grad_w), grad_x = _jax.lax.scan(body, init, (per_example, given["loss_target"]))
    with _jax.named_scope("update"):
        delta_w, new_m, new_v = {}, {}, {}
        for n in TWIN_WEIGHTS:
            delta_w[n], new_m[n], new_v[n] = _adamw(weights[n], grad_w[n], given["m_" + n], given["v_" + n])
    return (loss, grad_x, *[grad_w[n] for n in TWIN_WEIGHTS], *[delta_w[n] for n in TWIN_WEIGHTS],
            *[new_m[n] for n in TWIN_WEIGHTS], *[new_v[n] for n in TWIN_WEIGHTS])
```

```python
import functools

import numpy as np
import jax
import jax.numpy as jnp
from jax import lax
from jax.experimental import pallas as pl
from jax.experimental.pallas import tpu as pltpu

F32 = jnp.float32
BF16 = jnp.bfloat16

HEADS = 8
DK = 64
DV = 128
CONV_K = 31
CHUNK = 64
ROPE_BASE = 10000.0
EPS = 1e-6
HALO = 32
CONV_RB = 32

ADAM_LR = 0.001
ADAM_B1 = 0.9
ADAM_B2 = 0.999
ADAM_EPS = 1e-08
ADAM_WD = 0.01
ADAM_STEP = 10

VMEM_LIMIT = 56 * 1024 * 1024
MESH_T = pl.DeviceIdType.MESH
ANY = pl.BlockSpec(memory_space=pl.ANY)

SMALL_ROWS = 56
ROW_CB, ROW_LG, ROW_LB = 32, 33, 34
ROW_PRE, ROW_POST = 40, 48


def _cp(sem=None, **kw):
    return pltpu.CompilerParams(dimension_semantics=sem, vmem_limit_bytes=VMEM_LIMIT, **kw)


def _dot(a, b):
    return jnp.dot(a, b, preferred_element_type=F32)


def _dot_nt(a, b):
    return lax.dot_general(a, b, (((1,), (1,)), ((), ())), preferred_element_type=F32)


def _dot_tn(a, b):
    return lax.dot_general(a, b, (((0,), (0,)), ((), ())), preferred_element_type=F32)


def _sigmoid(x):
    return jax.nn.sigmoid(x)


def _silu(x):
    return x * _sigmoid(x)


def _rms(x, g):
    return x * lax.rsqrt(jnp.mean(x * x, axis=-1, keepdims=True) + EPS) * g


def _gn_gate(o, g):
    mu = jnp.mean(o, axis=-1, keepdims=True)
    d = o - mu
    var = jnp.mean(d * d, axis=-1, keepdims=True)
    return d * lax.rsqrt(var + EPS) * _silu(g)


def _ln_gate(y, gc, lg, lb):
    mu = jnp.mean(y, axis=-1, keepdims=True)
    d = y - mu
    var = jnp.mean(d * d, axis=-1, keepdims=True)
    return _silu(d * lax.rsqrt(var + EPS) * lg + lb) * _silu(gc)


def _tables(T, L):
    lane = np.arange(128)
    d = lane % DK
    half = DK // 2
    inv = (ROPE_BASE ** (-(np.arange(half, dtype=np.float32)) / half)).astype(np.float32)
    ang = (np.arange(T, dtype=np.float32)[:, None] * inv[None, :]).astype(np.float64)
    angl = ang[:, d % half]
    cos = np.cos(angl)
    sin = np.sin(angl)
    lo = (d < half)[None, :]
    rope = np.stack([cos, np.where(lo, -sin, 0.0), np.where(lo, 0.0, sin)]).astype(np.float32)

    hh = np.arange(HEADS, dtype=np.float64)
    log_g = np.log1p(-np.exp2(-5.0 - hh))
    n = np.arange(L, dtype=np.float64)
    cn = np.arange(L) // CHUNK
    allowed = (cn[None, :] <= cn[:, None])
    dist = np.abs(n[:, None] - n[None, :])
    mask = np.exp(log_g[:, None, None] * dist[None]) * allowed[None]
    mq = ((lane[None, :] // DK) == (np.arange(HEADS)[:, None] % 2)).astype(np.float64)
    qd = np.exp(log_g[:, None] * n[None, :])
    kd = np.exp(log_g[:, None] * (L - n[None, :]))
    qdm = qd[:, :, None] * mq[:, None, :] * (DK ** -0.5)
    kdm = kd[:, :, None] * mq[:, None, :]
    mqs = np.broadcast_to((mq * (DK ** -0.5))[:, None, :], (HEADS, 8, 128))
    cd = np.broadcast_to(np.exp(log_g * L)[:, None, None], (HEADS, 8, 128))
    f = lambda a: jnp.asarray(np.ascontiguousarray(a), dtype=F32)
    return dict(rope=f(rope), mask=f(mask), qdm=f(qdm), kdm=f(kdm), mqs=f(mqs), cd=f(cd))


def _rot(b, c, sl, sh):
    return b * c + pltpu.roll(b, 96, axis=1) * sl + pltpu.roll(b, 32, axis=1) * sh


def _rot_t(d, c, sl, sh):
    return d * c + pltpu.roll(d * sl, 32, axis=1) + pltpu.roll(d * sh, 96, axis=1)


def _cast_win(chip, w_in):
    NL, D, W = w_in.shape
    tr = min(256, D)

    def body(chip_ref, w_ref, o_ref):
        o_ref[...] = w_ref[...].astype(BF16)

    return pl.pallas_call(
        body, name="cast_win",
        out_shape=jax.ShapeDtypeStruct((4, NL, D, W), BF16),
        grid_spec=pltpu.PrefetchScalarGridSpec(
            num_scalar_prefetch=1, grid=(NL, D // tr),
            in_specs=[pl.BlockSpec((None, tr, W), lambda l, r, c: (l, r, 0))],
            out_specs=pl.BlockSpec((None, None, tr, W), lambda l, r, c: (c[0], l, r, 0))),
        compiler_params=_cp(("arbitrary", "arbitrary")),
    )(chip, w_in)


def _cast_wsq(chip, w_ro, w_co, w_o):
    NL, R, D = w_ro.shape

    def body(chip_ref, a_ref, b_ref, c_ref, o_ref):
        o_ref[0:R, :] = a_ref[...].astype(BF16)
        o_ref[R:2 * R, :] = b_ref[...].astype(BF16)
        o_ref[2 * R:3 * R, :] = c_ref[...].astype(BF16)

    spec = pl.BlockSpec((None, R, D), lambda l, c: (l, 0, 0))
    return pl.pallas_call(
        body, name="cast_wsq",
        out_shape=jax.ShapeDtypeStruct((4, NL, 3 * R, D), BF16),
        grid_spec=pltpu.PrefetchScalarGridSpec(
            num_scalar_prefetch=1, grid=(NL,),
            in_specs=[spec, spec, spec],
            out_specs=pl.BlockSpec((None, None, 3 * R, D), lambda l, c: (c[0], l, 0, 0))),
        compiler_params=_cp(("arbitrary",)),
    )(chip, w_ro, w_co, w_o)


def _place_cw(chip, conv_w):
    NL, K, Cc = conv_w.shape

    def body(chip_ref, w_ref, o_ref):
        o_ref[...] = w_ref[...]

    return pl.pallas_call(
        body, name="place_cw",
        out_shape=jax.ShapeDtypeStruct((4, NL, K, Cc), F32),
        grid_spec=pltpu.PrefetchScalarGridSpec(
            num_scalar_prefetch=1, grid=(1,),
            in_specs=[pl.BlockSpec((NL, K, Cc), lambda i, c: (0, 0, 0))],
            out_specs=pl.BlockSpec((None, NL, K, Cc), lambda i, c: (c[0], 0, 0, 0))),
        compiler_params=_cp(("arbitrary",)),
    )(chip, conv_w)


def _fwd_in(x, pre_g, win, l):
    T, D = x.shape
    W = win.shape[-1]
    tm = min(512, T)

    def body(x_ref, g_ref, w_ref, p_ref, h_ref, hs):
        @pl.when(pl.program_id(1) == 0)
        def _():
            hb = _rms(x_ref[...], g_ref[...]).astype(BF16)
            hs[...] = hb
            h_ref[...] = hb

        p_ref[...] = _dot(hs[...], w_ref[...]).astype(BF16)

    return pl.pallas_call(
        body, name="fwd_in",
        out_shape=(jax.ShapeDtypeStruct((T, 4 * W), BF16), jax.ShapeDtypeStruct((T, D), BF16)),
        grid=(T // tm, 4),
        in_specs=[pl.BlockSpec((tm, D), lambda i, j: (i, 0)),
                  pl.BlockSpec((None, 1, D), lambda i, j: (l, 0, 0)),
                  pl.BlockSpec((None, None, D, W), lambda i, j: (j, l, 0, 0))],
        out_specs=(pl.BlockSpec((tm, W), lambda i, j: (i, j)),
                   pl.BlockSpec((tm, D), lambda i, j: (i, 0))),
        scratch_shapes=[pltpu.VMEM((tm, D), BF16)],
        compiler_params=_cp(("arbitrary", "arbitrary")),
    )(x, pre_g, win)


def _ret_specs(T, L):
    rope = pl.BlockSpec((3, L, 128), lambda s: (0, s, 0))
    mask = pl.BlockSpec((HEADS, L, L), lambda s: (0, 0, 0))
    qdm = pl.BlockSpec((HEADS, L, 128), lambda s: (0, 0, 0))
    small = pl.BlockSpec((HEADS, 8, 128), lambda s: (0, 0, 0))
    return [rope, mask, qdm, qdm, small, small]


def _ret_fwd(proj, tb, L):
    T = proj.shape[0]
    QK = HEADS * DK
    VW = HEADS * DV
    nS = T // L

    def body(p_ref, rope_ref, m_ref, qdm_ref, kdm_ref, mqs_ref, cd_ref, a_ref, st_ref, state):
        @pl.when(pl.program_id(0) == 0)
        def _():
            state[...] = jnp.zeros_like(state)

        c, sl, sh = rope_ref[0], rope_ref[1], rope_ref[2]
        for j in range(HEADS // 2):
            rq = _rot(p_ref[:, 128 * j:128 * (j + 1)].astype(F32), c, sl, sh)
            rk = _rot(p_ref[:, QK + 128 * j:QK + 128 * (j + 1)].astype(F32), c, sl, sh)
            rkb = rk.astype(BF16)
            for e in range(2):
                h = 2 * j + e
                v = p_ref[:, 2 * QK + DV * h:2 * QK + DV * (h + 1)]
                g = p_ref[:, 2 * QK + VW + DV * h:2 * QK + VW + DV * (h + 1)].astype(F32)
                a = (rq * mqs_ref[h, 0:1, :]).astype(BF16)
                p = (_dot_nt(a, rkb) * m_ref[h]).astype(BF16)
                st = state[h]
                st_ref[h] = st
                o = _dot(p, v) + _dot((rq * qdm_ref[h]).astype(BF16), st.astype(BF16))
                state[h] = st * cd_ref[h, 0:1, :] + _dot_tn((rk * kdm_ref[h]).astype(BF16), v)
                a_ref[:, DV * h:DV * (h + 1)] = _gn_gate(o, g).astype(BF16)

    return pl.pallas_call(
        body, name="ret_fwd",
        out_shape=(jax.ShapeDtypeStruct((T, VW), BF16), jax.ShapeDtypeStruct((nS, HEADS, 128, DV), F32)),
        grid=(nS,),
        in_specs=[pl.BlockSpec((L, 2 * QK + 2 * VW), lambda s: (s, 0))] + _ret_specs(T, L),
        out_specs=(pl.BlockSpec((L, VW), lambda s: (s, 0)),
                   pl.BlockSpec((None, HEADS, 128, DV), lambda s: (s, 0, 0, 0))),
        scratch_shapes=[pltpu.VMEM((HEADS, 128, DV), F32)],
        compiler_params=_cp(("arbitrary",)),
    )(proj, tb["rope"], tb["mask"], tb["qdm"], tb["kdm"], tb["mqs"], tb["cd"])


def _conv_taps(wbuf, src, base, rows):
    acc = wbuf[pl.ds(0, 1), :] * src[pl.ds(base, rows), :]
    for k in range(1, CONV_K):
        acc = acc + wbuf[pl.ds(k, 1), :] * src[pl.ds(base + k, rows), :]
    return acc


def _load_conv_w(cw_ref, wbuf, flip):
    for k in range(CONV_K):
        row = jnp.concatenate([cw_ref[c, pl.ds(k, 1), :] for c in range(4)], axis=-1)
        wbuf[pl.ds(CONV_K - 1 - k if flip else k, 1), :] = row
    wbuf[pl.ds(CONV_K, 1), :] = jnp.zeros_like(wbuf[pl.ds(CONV_K, 1), :])


def _conv_fwd(proj, wcw, conv_b, ln_g, ln_b, l):
    T = proj.shape[0]
    C = conv_b.shape[-1]
    Cc = wcw.shape[-1]
    tc = min(256, T)
    off = HALO - (CONV_K - 1)

    def body(p_ref, cw_ref, cb_ref, lg_ref, lb_ref, a_ref, y_ref, ubuf, wbuf):
        i = pl.program_id(0)

        @pl.when(i == 0)
        def _():
            ubuf[0:HALO, :] = jnp.zeros((HALO, C), F32)
            _load_conv_w(cw_ref, wbuf, False)

        @pl.when(i > 0)
        def _():
            ubuf[0:HALO, :] = ubuf[tc:tc + HALO, :]

        ga = p_ref[:, 0:C].astype(F32)
        gb = p_ref[:, C:2 * C].astype(F32)
        ubuf[HALO:HALO + tc, :] = ga * _sigmoid(gb)
        for r in range(tc // CONV_RB):
            y_ref[r * CONV_RB:(r + 1) * CONV_RB, :] = _conv_taps(wbuf, ubuf, r * CONV_RB + off, CONV_RB) + cb_ref[...]
        gc = p_ref[:, 2 * C:3 * C].astype(F32)
        a_ref[...] = _ln_gate(y_ref[...], gc, lg_ref[...], lb_ref[...]).astype(BF16)

    vec = pl.BlockSpec((None, 1, C), lambda i: (l, 0, 0))
    return pl.pallas_call(
        body, name="conv_fwd",
        out_shape=(jax.ShapeDtypeStruct((T, C), BF16), jax.ShapeDtypeStruct((T, C), F32)),
        grid=(T // tc,),
        in_specs=[pl.BlockSpec((tc, 3 * C), lambda i: (i, 1)),
                  pl.BlockSpec((4, None, CONV_K, Cc), lambda i: (0, l, 0, 0)),
                  vec, vec, vec],
        out_specs=(pl.BlockSpec((tc, C), lambda i: (i, 0)), pl.BlockSpec((tc, C), lambda i: (i, 0))),
        scratch_shapes=[pltpu.VMEM((HALO + tc, C), F32), pltpu.VMEM((HALO, C), F32)],
        compiler_params=_cp(("arbitrary",)),
    )(proj, wcw, conv_b, ln_g, ln_b)


def _merge_fwd(x, proj, a_ret, a_conv, wsq, post_g, l):
    T, D = x.shape
    R = wsq.shape[2] // 3
    tm = min(512, T)

    def body(x_ref, p_ref, ar_ref, ac_ref, wro_ref, wco_ref, wo_ref, g_ref, xn_ref, ro_ref, co_ref, ym_ref, z_ref):
        ro = _dot(ar_ref[...], wro_ref[...].reshape(4 * R, D))
        co = _dot(ac_ref[...], wco_ref[...].reshape(4 * R, D))
        ym = (_sigmoid(p_ref[:, 0:D].astype(F32)) * ro + _sigmoid(p_ref[:, D:2 * D].astype(F32)) * co).astype(BF16)
        z = _dot(ym, wo_ref[...].reshape(4 * R, D))
        ro_ref[...] = ro.astype(BF16)
        co_ref[...] = co.astype(BF16)
        ym_ref[...] = ym
        z_ref[...] = z.astype(BF16)
        xn_ref[...] = x_ref[...] + _rms(z, g_ref[...])

    tile = pl.BlockSpec((tm, D), lambda i: (i, 0))
    wspec = lambda m: pl.BlockSpec((4, None, R, D), lambda i: (0, l, m, 0))
    act = jax.ShapeDtypeStruct((T, D), BF16)
    return pl.pallas_call(
        body, name="merge_fwd",
        out_shape=(jax.ShapeDtypeStruct((T, D), F32), act, act, act, act),
        grid=(T // tm,),
        in_specs=[tile, pl.BlockSpec((tm, 2 * D), lambda i: (i, 3)), tile, tile,
                  wspec(0), wspec(1), wspec(2), pl.BlockSpec((None, 1, D), lambda i: (l, 0, 0))],
        out_specs=(tile, tile, tile, tile, tile),
        compiler_params=_cp(("arbitrary",)),
    )(x, proj, a_ret, a_conv, wsq, wsq, wsq, post_g)


def _loss_fwd_bwd(y, target):
    T, D = y.shape
    tm = min(512, T)

    def body(y_ref, t_ref, dy_ref, ls_ref):
        @pl.when(pl.program_id(0) == 0)
        def _():
            ls_ref[...] = jnp.zeros_like(ls_ref)

        e = y_ref[...] - t_ref[...]
        dy_ref[...] = e * (1.0 / D)
        ls_ref[...] += jnp.sum((e * e).reshape(tm // 8, 8, D), axis=0) * (0.5 / D)

    tile = pl.BlockSpec((tm, D), lambda i: (i, 0))
    return pl.pallas_call(
        body, name="loss",
        out_shape=(jax.ShapeDtypeStruct((T, D), F32), jax.ShapeDtypeStruct((8, D), F32)),
        grid=(T // tm,),
        in_specs=[tile, tile],
        out_specs=(tile, pl.BlockSpec((8, D), lambda i: (0, 0))),
        compiler_params=_cp(("arbitrary",)),
    )(y, target)


def _merge_bwd(dxn, proj, a_ret, a_conv, ro, co, ym, z, wsq, post_g, gsq, l):
    T, D = dxn.shape
    R = wsq.shape[2] // 3
    tm = min(256, T)
    n = T // tm

    def body(dx_ref, p_ref, ar_ref, ac_ref, ro_ref, co_ref, ym_ref, z_ref, wro_ref, wco_ref, wo_ref, g_ref, gsq_in,
             dp_ref, dar_ref, dac_ref, gsq_ref, dg_ref, acc, stage):
        i = pl.program_id(0)

        @pl.when(i == 0)
        def _():
            acc[...] = jnp.zeros_like(acc)
            dg_ref[...] = jnp.zeros_like(dg_ref)

        _, vjp = jax.vjp(_rms, z_ref[...].astype(F32), g_ref[...])
        dz, dg = vjp(dx_ref[...])
        dg_ref[0:1, :] += dg
        dzb = dz.astype(BF16)
        dym = _dot_nt(dzb, wo_ref[...].reshape(4 * R, D))
        acc[2] += _dot_tn(ym_ref[...], dzb)
        sr = _sigmoid(p_ref[:, 0:D].astype(F32))
        sc = _sigmoid(p_ref[:, D:2 * D].astype(F32))
        rov = ro_ref[...].astype(F32)
        cov = co_ref[...].astype(F32)
        dp_ref[:, 0:D] = (dym * rov * sr * (1.0 - sr)).astype(BF16)
        dp_ref[:, D:2 * D] = (dym * cov * sc * (1.0 - sc)).astype(BF16)
        dro = (dym * sr).astype(BF16)
        dco = (dym * sc).astype(BF16)
        dar_ref[...] = _dot_nt(dro, wro_ref[...].reshape(4 * R, D)).astype(BF16)
        dac_ref[...] = _dot_nt(dco, wco_ref[...].reshape(4 * R, D)).astype(BF16)
        acc[0] += _dot_tn(ar_ref[...], dro)
        acc[1] += _dot_tn(ac_ref[...], dco)

        @pl.when(i == n - 1)
        def _():
            for m in range(3):
                stage[...] = acc[m].astype(BF16).reshape(4, R, D)
                pltpu.sync_copy(stage, gsq_ref.at[l, :, pl.ds(m * R, R), :])

    tile = pl.BlockSpec((tm, D), lambda i: (i, 0))
    wspec = lambda m: pl.BlockSpec((4, None, R, D), lambda i: (0, l, m, 0), pipeline_mode=pl.Buffered(1))
    return pl.pallas_call(
        body, name="merge_bwd",
        out_shape=(jax.ShapeDtypeStruct(proj.shape, BF16), jax.ShapeDtypeStruct((T, D), BF16),
                   jax.ShapeDtypeStruct((T, D), BF16), jax.ShapeDtypeStruct(gsq.shape, BF16),
                   jax.ShapeDtypeStruct((8, D), F32)),
        grid=(n,),
        in_specs=[tile, pl.BlockSpec((tm, 2 * D), lambda i: (i, 3)), tile, tile, tile, tile, tile, tile,
                  wspec(0), wspec(1), wspec(2), pl.BlockSpec((None, 1, D), lambda i: (l, 0, 0)), ANY],
        out_specs=(pl.BlockSpec((tm, 2 * D), lambda i: (i, 3)), tile, tile,
                   ANY, pl.BlockSpec((8, D), lambda i: (0, 0))),
        scratch_shapes=[pltpu.VMEM((3, 4 * R, D), F32), pltpu.VMEM((4, R, D), BF16)],
        input_output_aliases={12: 3},
        compiler_params=_cp(("arbitrary",)),
    )(dxn, proj, a_ret, a_conv, ro, co, ym, z, wsq, wsq, wsq, post_g, gsq)


def _conv_bwd(dproj, da_conv, y, proj, wcw, ln_g, ln_b, l):
    T, C = y.shape
    Cc = wcw.shape[-1]
    tc = min(256, T)
    n = T // tc
    hb = tc // HALO
    off = HALO - (CONV_K - 1)
    nrb = tc // CONV_RB

    def body(dpin_ref, da_ref, y_ref, p_ref, ph_ref, cw_ref, lg_ref, lb_ref, dp_ref, sg_ref,
             dcbuf, ubuf, dubuf, wbuf, dwacc, vacc):
        t = pl.program_id(0)
        i = n - 1 - t

        @pl.when(t == 0)
        def _():
            dcbuf[tc:tc + HALO, :] = jnp.zeros((HALO, C), F32)
            dwacc[...] = jnp.zeros_like(dwacc)
            vacc[...] = jnp.zeros_like(vacc)
            _load_conv_w(cw_ref, wbuf, True)

        @pl.when(t > 0)
        def _():
            dcbuf[tc:tc + HALO, :] = dcbuf[0:HALO, :]

        gc = p_ref[:, 2 * C:3 * C].astype(F32)
        _, vjp = jax.vjp(_ln_gate, y_ref[...], gc, lg_ref[...], lb_ref[...])
        dy, dgc, dlg, dlb = vjp(da_ref[...].astype(F32))
        dcbuf[0:tc, :] = dy
        dp_ref[:, 2 * C:3 * C] = dgc.astype(BF16)
        vacc[0:1, :] += jnp.sum(dy, axis=0, keepdims=True)
        vacc[1:2, :] += dlg
        vacc[2:3, :] += dlb

        ga = p_ref[:, 0:C].astype(F32)
        sb = _sigmoid(p_ref[:, C:2 * C].astype(F32))
        ubuf[HALO:HALO + tc, :] = ga * sb
        uh = ph_ref[:, 0:C].astype(F32) * _sigmoid(ph_ref[:, C:2 * C].astype(F32))
        ubuf[0:HALO, :] = jnp.where(i > 0, uh, 0.0)

        for r in range(nrb):
            dubuf[r * CONV_RB:(r + 1) * CONV_RB, :] = _conv_taps(wbuf, dcbuf, r * CONV_RB, CONV_RB)
        du = dubuf[...]
        dp_ref[:, 0:C] = (du * sb).astype(BF16)
        dp_ref[:, C:2 * C] = (du * ga * sb * (1.0 - sb)).astype(BF16)

        for r in range(nrb):
            dyb = dcbuf[r * CONV_RB:(r + 1) * CONV_RB, :]
            for k in range(CONV_K):
                pr = dyb * ubuf[pl.ds(r * CONV_RB + off + k, CONV_RB), :]
                dwacc[8 * k:8 * k + 8, :] += jnp.sum(pr.reshape(CONV_RB // 8, 8, C), axis=0)

        @pl.when(t == n - 1)
        def _():
            for k in range(CONV_K):
                sg_ref[pl.ds(k, 1), :] = jnp.sum(dwacc[8 * k:8 * k + 8, :], axis=0, keepdims=True)
            sg_ref[pl.ds(CONV_K, 1), :] = jnp.zeros((1, C), F32)
            sg_ref[ROW_CB:ROW_CB + 8, :] = jnp.zeros((8, C), F32)
            sg_ref[ROW_CB:ROW_CB + 3, :] = vacc[0:3, :]

    vec = pl.BlockSpec((None, 1, C), lambda t: (l, 0, 0))
    tile = pl.BlockSpec((tc, C), lambda t: (n - 1 - t, 0))
    ptile = pl.BlockSpec((tc, 3 * C), lambda t: (n - 1 - t, 1))
    halo = pl.BlockSpec((HALO, 3 * C), lambda t: (jnp.maximum((n - 1 - t) * hb - 1, 0), 1))
    return pl.pallas_call(
        body, name="conv_bwd",
        out_shape=(jax.ShapeDtypeStruct(dproj.shape, BF16), jax.ShapeDtypeStruct((ROW_PRE, C), F32)),
        grid=(n,),
        in_specs=[ANY, tile, tile, ptile, halo,
                  pl.BlockSpec((4, None, CONV_K, Cc), lambda t: (0, l, 0, 0)), vec, vec],
        out_specs=(ptile, pl.BlockSpec((ROW_PRE, C), lambda t: (0, 0))),
        scratch_shapes=[pltpu.VMEM((tc + HALO, C), F32), pltpu.VMEM((HALO + tc, C), F32), pltpu.VMEM((tc, C), F32),
                        pltpu.VMEM((HALO, C), F32), pltpu.VMEM((8 * CONV_K, C), F32), pltpu.VMEM((8, C), F32)],
        input_output_aliases={0: 0},
        compiler_params=_cp(("arbitrary",)),
    )(dproj, da_conv, y, proj, proj, wcw, ln_g, ln_b)


def _ret_bwd(dproj, da_ret, proj, states, tb, L):
    T = proj.shape[0]
    QK = HEADS * DK
    VW = HEADS * DV
    nS = T // L
    PW = 2 * QK + 2 * VW

    def body(dpin_ref, da_ref, p_ref, st_ref, rope_ref, m_ref, qdm_ref, kdm_ref, mqs_ref, cd_ref, dp_ref, gst):
        @pl.when(pl.program_id(0) == 0)
        def _():
            gst[...] = jnp.zeros_like(gst)

        c, sl, sh = rope_ref[0], rope_ref[1], rope_ref[2]
        for j in range(HEADS // 2):
            rq = _rot(p_ref[:, 128 * j:128 * (j + 1)].astype(F32), c, sl, sh)
            rk = _rot(p_ref[:, QK + 128 * j:QK + 128 * (j + 1)].astype(F32), c, sl, sh)
            rkb = rk.astype(BF16)
            drq = jnp.zeros_like(rq)
            drk = jnp.zeros_like(rk)
            for e in range(2):
                h = 2 * j + e
                v = p_ref[:, 2 * QK + DV * h:2 * QK + DV * (h + 1)]
                g = p_ref[:, 2 * QK + VW + DV * h:2 * QK + VW + DV * (h + 1)].astype(F32)
                mqs = mqs_ref[h, 0:1, :]
                a = (rq * mqs).astype(BF16)
                aq = (rq * qdm_ref[h]).astype(BF16)
                kdv = (rk * kdm_ref[h]).astype(BF16)
                mk = m_ref[h]
                p = (_dot_nt(a, rkb) * mk).astype(BF16)
                stb = st_ref[h].astype(BF16)
                o = _dot(p, v) + _dot(aq, stb)
                _, vjp = jax.vjp(_gn_gate, o, g)
                do, dg = vjp(da_ref[:, DV * h:DV * (h + 1)].astype(F32))
                dob = do.astype(BF16)
                gs = gst[h]
                gsb = gs.astype(BF16)
                ds = (_dot_nt(dob, v) * mk).astype(BF16)
                drq = drq + _dot(ds, rkb) * mqs + _dot_nt(dob, stb) * qdm_ref[h]
                drk = drk + _dot_tn(ds, a) + _dot_nt(v, gsb) * kdm_ref[h]
                dv = _dot_tn(p, dob) + _dot(kdv, gsb)
                gst[h] = _dot_tn(aq, dob) + gs * cd_ref[h, 0:1, :]
                dp_ref[:, 2 * QK + DV * h:2 * QK + DV * (h + 1)] = dv.astype(BF16)
                dp_ref[:, 2 * QK + VW + DV * h:2 * QK + VW + DV * (h + 1)] = dg.astype(BF16)
            dp_ref[:, 128 * j:128 * (j + 1)] = _rot_t(drq, c, sl, sh).astype(BF16)
            dp_ref[:, QK + 128 * j:QK + 128 * (j + 1)] = _rot_t(drk, c, sl, sh).astype(BF16)

    rev = lambda s: nS - 1 - s
    specs = _ret_specs(T, L)
    specs[0] = pl.BlockSpec((3, L, 128), lambda s: (0, rev(s), 0))
    return pl.pallas_call(
        body, name="ret_bwd",
        out_shape=jax.ShapeDtypeStruct(dproj.shape, BF16),
        grid=(nS,),
        in_specs=[ANY, pl.BlockSpec((L, VW), lambda s: (rev(s), 0)), pl.BlockSpec((L, PW), lambda s: (rev(s), 0)),
                  pl.BlockSpec((None, HEADS, 128, DV), lambda s: (rev(s), 0, 0, 0))] + specs,
        out_specs=pl.BlockSpec((L, PW), lambda s: (rev(s), 0)),
        scratch_shapes=[pltpu.VMEM((HEADS, 128, DV), F32)],
        input_output_aliases={0: 0},
        compiler_params=_cp(("arbitrary",)),
    )(dproj, da_ret, proj, states, tb["rope"], tb["mask"], tb["qdm"], tb["kdm"], tb["mqs"], tb["cd"])


def _win_grad(gin, h, dproj, l):
    T, D = h.shape
    W = gin.shape[-1]
    tk = min(512, T)
    nk = T // tk

    def body(gin_in, h_ref, dp_ref, g_ref, acc):
        k = pl.program_id(1)

        @pl.when(k == 0)
        def _():
            acc[...] = jnp.zeros_like(acc)

        acc[...] += _dot_tn(h_ref[...], dp_ref[...])

        @pl.when(k == nk - 1)
        def _():
            g_ref[...] = acc[...].astype(BF16)

    return pl.pallas_call(
        body, name="win_grad",
        out_shape=jax.ShapeDtypeStruct(gin.shape, BF16),
        grid=(4, nk),
        in_specs=[ANY, pl.BlockSpec((tk, D), lambda j, k: (k, 0)), pl.BlockSpec((tk, W), lambda j, k: (k, j))],
        out_specs=pl.BlockSpec((None, None, D, W), lambda j, k: (l, j, 0, 0)),
        scratch_shapes=[pltpu.VMEM((D, W), F32)],
        input_output_aliases={0: 0},
        compiler_params=_cp(("arbitrary", "arbitrary")),
    )(gin, h, dproj)


def _in_bwd(dxn, dproj, x, pre_g, win, l):
    T, D = x.shape
    W = win.shape[-1]
    tm = min(512, T)

    def body(dxn_ref, dp_ref, x_ref, g_ref, w_ref, dx_ref, dg_ref, acc):
        i = pl.program_id(0)
        j = pl.program_id(1)

        @pl.when(j == 0)
        def _():
            acc[...] = jnp.zeros_like(acc)

        @pl.when((i == 0) & (j == 0))
        def _():
            dg_ref[...] = jnp.zeros_like(dg_ref)

        acc[...] += _dot_nt(dp_ref[...], w_ref[...])

        @pl.when(j == 3)
        def _():
            _, vjp = jax.vjp(_rms, x_ref[...], g_ref[...])
            dx, dg = vjp(acc[...])
            dx_ref[...] = dxn_ref[...] + dx
            dg_ref[0:1, :] += dg

    tile = pl.BlockSpec((tm, D), lambda i, j: (i, 0))
    return pl.pallas_call(
        body, name="in_bwd",
        out_shape=(jax.ShapeDtypeStruct((T, D), F32), jax.ShapeDtypeStruct((8, D), F32)),
        grid=(T // tm, 4),
        in_specs=[tile, pl.BlockSpec((tm, W), lambda i, j: (i, j)), tile,
                  pl.BlockSpec((None, 1, D), lambda i, j: (l, 0, 0)),
                  pl.BlockSpec((None, None, D, W), lambda i, j: (j, l, 0, 0))],
        out_specs=(tile, pl.BlockSpec((8, D), lambda i, j: (0, 0))),
        scratch_shapes=[pltpu.VMEM((tm, D), F32)],
        compiler_params=_cp(("arbitrary", "arbitrary")),
    )(dxn, dproj, x, pre_g, win)


def _local_step(x, target, win, wsq, wcw, pre_g, conv_b, ln_g, ln_b, post_g):
    T, D = x.shape
    _, NL, _, W = win.shape
    L = min(256, T)
    tb = _tables(T, L)
    pre_g, conv_b, ln_g, ln_b, post_g = (a.reshape(NL, 1, D) for a in (pre_g, conv_b, ln_g, ln_b, post_g))
    saved = []
    h_in = x
    for l in range(NL):
        proj, h = _fwd_in(h_in, pre_g, win, l)
        a_ret, states = _ret_fwd(proj, tb, L)
        a_conv, y = _conv_fwd(proj, wcw, conv_b, ln_g, ln_b, l)
        xn, ro, co, ym, z = _merge_fwd(h_in, proj, a_ret, a_conv, wsq, post_g, l)
        saved.append((h_in, proj, h, a_ret, states, a_conv, y, ro, co, ym, z))
        h_in = xn
    dx, lsum = _loss_fwd_bwd(h_in, target)
    gin = lax.empty((NL, 4, D, W), BF16)
    gsq = lax.empty((NL,) + wsq.shape[:1] + wsq.shape[2:], BF16)
    small = [None] * NL
    for l in reversed(range(NL)):
        xin, proj, h, a_ret, states, a_conv, y, ro, co, ym, z = saved[l]
        dproj, da_ret, da_conv, gsq, dpost = _merge_bwd(dx, proj, a_ret, a_conv, ro, co, ym, z, wsq, post_g, gsq, l)
        dproj, sg = _conv_bwd(dproj, da_conv, y, proj, wcw, ln_g, ln_b, l)
        dproj = _ret_bwd(dproj, da_ret, proj, states, tb, L)
        gin = _win_grad(gin, h, dproj, l)
        dx, dpre = _in_bwd(dx, dproj, xin, pre_g, win, l)
        small[l] = jnp.concatenate([sg, dpre, dpost], axis=0)
    return lsum, dx, gin, gsq, jnp.stack(small)


def _place():
    return lax.axis_index("x"), lax.axis_index("y"), lax.axis_index("c")


def _other_chips(x, y):
    return [(1 - x, y), (x, 1 - y), (1 - x, 1 - y)]


def _remote(src, dst, send_sems, recv_sems, k, to):
    return pltpu.make_async_remote_copy(src_ref=src, dst_ref=dst, send_sem=send_sems.at[k], recv_sem=recv_sems.at[k],
                                        device_id=to, device_id_type=MESH_T)


def _gather_weights(win, wsq, wcw):
    nlh = win.shape[1] // 2
    n_arr = 3

    def body(a0, a1, a2, o0, o1, o2, send_sems, recv_sems):
        x, y, c = _place()
        sibling = (x, y, 1 - c)
        chips = _other_chips(x, y)
        outs = (o0, o1, o2)

        def blk(a, cx, cy, cc):
            return outs[a].at[2 * cx + cy, pl.ds(nlh * cc, nlh)]

        def copy(k, a, block, to):
            return _remote(blk(a, *block), blk(a, *block), send_sems, recv_sems, k, to)

        first = [copy(3 * a + j, a, (x, y, c), (*chip, c)) for a in range(n_arr) for j, chip in enumerate(chips)]
        for cp in first:
            cp.start()
        passed = [copy(3 * n_arr + 3 * a + j, a, (*chip, c), sibling) for a in range(n_arr) for j, chip in enumerate(chips)]
        for a in range(n_arr):
            for j, chip in enumerate(chips):
                copy(3 * a + j, a, (*chip, c), sibling).wait_recv()
                passed[3 * a + j].start()
        for a in range(n_arr):
            for j, chip in enumerate(chips):
                copy(3 * n_arr + 3 * a + j, a, (*chip, 1 - c), sibling).wait_recv()
        for cp in first + passed:
            cp.wait_send()

    ins = (win, wsq, wcw)
    return pl.pallas_call(
        body, name="gather_weights",
        out_shape=tuple(jax.ShapeDtypeStruct(a.shape, a.dtype) for a in ins),
        in_specs=[ANY] * 3, out_specs=(ANY,) * 3,
        scratch_shapes=[pltpu.SemaphoreType.DMA((6 * n_arr,)), pltpu.SemaphoreType.DMA((6 * n_arr,))],
        input_output_aliases={0: 0, 1: 1, 2: 2},
        compiler_params=pltpu.CompilerParams(has_side_effects=True),
    )(*ins)


def _swap_halves(gin, gsq):
    nlh = gin.shape[0] // 2

    def body(g0, g1, r0, r1, send_sems, recv_sems):
        x, y, c = _place()
        sibling = (x, y, 1 - c)
        cps = [_remote(g.at[pl.ds(nlh * (1 - c), nlh)], r, send_sems, recv_sems, k, sibling)
               for k, (g, r) in enumerate(((g0, r0), (g1, r1)))]
        for cp in cps:
            cp.start()
        for cp in cps:
            cp.wait()

    return pl.pallas_call(
        body, name="swap_halves",
        out_shape=tuple(jax.ShapeDtypeStruct((nlh,) + a.shape[1:], a.dtype) for a in (gin, gsq)),
        in_specs=[ANY] * 2, out_specs=(ANY,) * 2,
        scratch_shapes=[pltpu.SemaphoreType.DMA((2,)), pltpu.SemaphoreType.DMA((2,))],
        compiler_params=pltpu.CompilerParams(has_side_effects=True),
    )(gin, gsq)


def _add_halves(cidx, g, r):
    nlh, _, A, B = r.shape
    tr = min(256, A)

    def body(c_ref, g_ref, r_ref, o_ref):
        o_ref[...] = (g_ref[...].astype(F32) + r_ref[...].astype(F32)).astype(BF16)

    blk = (None, None, tr, B)
    return pl.pallas_call(
        body, name="add_halves",
        out_shape=jax.ShapeDtypeStruct(r.shape, BF16),
        grid_spec=pltpu.PrefetchScalarGridSpec(
            num_scalar_prefetch=1, grid=(nlh, 4, A // tr),
            in_specs=[pl.BlockSpec(blk, lambda l, k, i, c: (nlh * c[0] + l, k, i, 0)),
                      pl.BlockSpec(blk, lambda l, k, i, c: (l, k, i, 0))],
            out_specs=pl.BlockSpec(blk, lambda l, k, i, c: (l, k, i, 0))),
        compiler_params=_cp(("arbitrary",) * 3),
    )(cidx, g, r)


def _exchange_shards(tin, tsq):
    nlh = tin.shape[0]

    def body(t0, t1, u0, u1, send_sems, recv_sems):
        x, y, c = _place()
        cps = []
        for a, (t, u) in enumerate(((t0, u0), (t1, u1))):
            for j, (cx, cy) in enumerate(_other_chips(x, y)):
                cps.append(_remote(t.at[pl.ds(0, nlh), 2 * cx + cy], u.at[j], send_sems, recv_sems, 3 * a + j, (cx, cy, c)))
        for cp in cps:
            cp.start()
        for cp in cps:
            cp.wait()

    return pl.pallas_call(
        body, name="exchange_shards",
        out_shape=tuple(jax.ShapeDtypeStruct((3, nlh) + a.shape[2:], a.dtype) for a in (tin, tsq)),
        in_specs=[ANY] * 2, out_specs=(ANY,) * 2,
        scratch_shapes=[pltpu.SemaphoreType.DMA((6,)), pltpu.SemaphoreType.DMA((6,))],
        compiler_params=pltpu.CompilerParams(has_side_effects=True),
    )(tin, tsq)


def _sum_shard(cidx, chip, t, u, nl):
    nlh, _, A, B = t.shape
    tr = min(256, A)

    def body(c_ref, k_ref, t_ref, u_ref, o_ref):
        o_ref[...] = ((t_ref[...].astype(F32) + u_ref[0].astype(F32)) + u_ref[1].astype(F32)) + u_ref[2].astype(F32)

    return pl.pallas_call(
        body, name="sum_shard",
        out_shape=jax.ShapeDtypeStruct((nl, A, B), F32),
        grid_spec=pltpu.PrefetchScalarGridSpec(
            num_scalar_prefetch=2, grid=(nlh, A // tr),
            in_specs=[pl.BlockSpec((None, None, tr, B), lambda l, i, c, k: (l, k[0], i, 0)),
                      pl.BlockSpec((3, None, tr, B), lambda l, i, c, k: (0, l, i, 0))],
            out_specs=pl.BlockSpec((None, tr, B), lambda l, i, c, k: (nlh * c[0] + l, i, 0))),
        compiler_params=_cp(("arbitrary",) * 2),
    )(cidx, chip, t, u)


def _share_halves(gin, gsq):
    nlh = gin.shape[0] // 2

    def body(a0, a1, o0, o1, send_sems, recv_sems):
        x, y, c = _place()
        sibling = (x, y, 1 - c)
        sends = [_remote(o.at[pl.ds(nlh * c, nlh)], o.at[pl.ds(nlh * c, nlh)], send_sems, recv_sems, k, sibling)
                 for k, o in enumerate((o0, o1))]
        for cp in sends:
            cp.start()
        for k, o in enumerate((o0, o1)):
            theirs = o.at[pl.ds(nlh * (1 - c), nlh)]
            _remote(theirs, theirs, send_sems, recv_sems, k, sibling).wait_recv()
        for cp in sends:
            cp.wait_send()

    return pl.pallas_call(
        body, name="share_halves",
        out_shape=tuple(jax.ShapeDtypeStruct(a.shape, a.dtype) for a in (gin, gsq)),
        in_specs=[ANY] * 2, out_specs=(ANY,) * 2,
        scratch_shapes=[pltpu.SemaphoreType.DMA((2,)), pltpu.SemaphoreType.DMA((2,))],
        input_output_aliases={0: 0, 1: 1},
        compiler_params=pltpu.CompilerParams(has_side_effects=True),
    )(gin, gsq)


def _gather_small(small):
    def body(s_ref, o_ref, send_sems, recv_sems, local_sem):
        x, y, c = _place()
        me = 4 * x + 2 * y + c
        mine = pltpu.make_async_copy(s_ref, o_ref.at[me], local_sem)
        mine.start()
        flip = lambda v, b: 1 - v if b else v
        sends = []
        for r in range(1, 8):
            peer = (flip(x, r & 4), flip(y, r & 2), flip(c, r & 1))
            sends.append(_remote(s_ref, o_ref.at[me], send_sems, recv_sems, r - 1, peer))
        for cp in sends:
            cp.start()
        for r in range(1, 8):
            peer = (flip(x, r & 4), flip(y, r & 2), flip(c, r & 1))
            theirs = o_ref.at[4 * peer[0] + 2 * peer[1] + peer[2]]
            _remote(theirs, theirs, send_sems, recv_sems, r - 1, peer).wait_recv()
        for cp in sends:
            cp.wait_send()
        mine.wait()

    return pl.pallas_call(
        body, name="gather_small",
        out_shape=jax.ShapeDtypeStruct((8,) + small.shape, small.dtype),
        in_specs=[ANY], out_specs=ANY,
        scratch_shapes=[pltpu.SemaphoreType.DMA((7,)), pltpu.SemaphoreType.DMA((7,)), pltpu.SemaphoreType.DMA],
        compiler_params=pltpu.CompilerParams(has_side_effects=True),
    )(small)


def _sum_devices(gs):
    _, NL, R, D = gs.shape

    def body(g_ref, o_ref):
        acc = g_ref[0]
        for k in range(1, 8):
            acc = acc + g_ref[k]
        o_ref[...] = acc

    return pl.pallas_call(
        body, name="sum_devices",
        out_shape=jax.ShapeDtypeStruct((NL, R, D), F32),
        grid=(NL,),
        in_specs=[pl.BlockSpec((8, None, R, D), lambda l: (0, l, 0, 0))],
        out_specs=pl.BlockSpec((None, R, D), lambda l: (l, 0, 0)),
        compiler_params=_cp(("arbitrary",)),
    )(gs)


ADAM_BLOCK_BYTES = 2 * 1024 * 1024


def _adamw(w, g, m, v):
    shape = w.shape
    cols = shape[-1]
    rows = int(np.prod(shape[:-1]))
    tr = rows
    while tr * cols * 4 > ADAM_BLOCK_BYTES and tr % 16 == 0:
        tr //= 2
    c1 = 1.0 / (1.0 - ADAM_B1 ** ADAM_STEP)
    c2 = 1.0 / (1.0 - ADAM_B2 ** ADAM_STEP)

    def body(w_ref, g_ref, m_ref, v_ref, d_ref, nm_ref, nv_ref):
        gv = g_ref[...]
        nm = ADAM_B1 * m_ref[...] + (1.0 - ADAM_B1) * gv
        nv = ADAM_B2 * v_ref[...] + (1.0 - ADAM_B2) * (gv * gv)
        nm_ref[...] = nm
        nv_ref[...] = nv
        d_ref[...] = -ADAM_LR * ((nm * c1) / (jnp.sqrt(nv * c2) + ADAM_EPS) + ADAM_WD * w_ref[...])

    tile = pl.BlockSpec((tr, cols), lambda i: (i, 0))
    out = jax.ShapeDtypeStruct((rows, cols), F32)
    res = pl.pallas_call(
        body, name="adamw",
        out_shape=(out, out, out),
        grid=(rows // tr,),
        in_specs=[tile] * 4, out_specs=(tile,) * 3,
        compiler_params=_cp(("arbitrary",)),
    )(*[a.reshape(rows, cols) for a in (w, g, m, v)])
    return tuple(a.reshape(shape) for a in res)


def kernel(x, pre_norm_g, w_in, w_ret_out, conv_w, conv_b, conv_ln_g, conv_ln_b, w_conv_out, w_o, post_norm_g, loss_target, m_pre_norm_g, m_w_in, m_w_ret_out, m_conv_w, m_conv_b, m_conv_ln_g, m_conv_ln_b, m_w_conv_out, m_w_o, m_post_norm_g, v_pre_norm_g, v_w_in, v_w_ret_out, v_conv_w, v_conv_b, v_conv_ln_g, v_conv_ln_b, v_w_conv_out, v_w_o, v_post_norm_g):
    NL, D, W = w_in.shape
    R = w_o.shape[1]
    Cc = conv_w.shape[-1]
    ax, ay, ac = _place()
    chip = (2 * ax + ay).astype(jnp.int32).reshape(1)
    cidx = ac.astype(jnp.int32).reshape(1)

    win, wsq, wcw = _gather_weights(_cast_win(chip, w_in), _cast_wsq(chip, w_ret_out, w_conv_out, w_o),
                                    _place_cw(chip, conv_w))
    lsum, grad_x, gin, gsq, small = _local_step(x[0], loss_target[0], win, wsq, wcw, pre_norm_g, conv_b,
                                                conv_ln_g, conv_ln_b, post_norm_g)
    loss = lax.psum(jnp.sum(lsum), ("x", "y", "c"))

    rin, rsq = _swap_halves(gin, gsq)
    tin, tsq = _add_halves(cidx, gin, rin), _add_halves(cidx, gsq, rsq)
    uin, usq = _exchange_shards(tin, tsq)
    g_in, g_sq = _share_halves(_sum_shard(cidx, chip, tin, uin, NL), _sum_shard(cidx, chip, tsq, usq, NL))
    gsm = _sum_devices(_gather_small(small))

    grads = {
        "pre_norm_g": gsm[:, ROW_PRE], "w_in": g_in, "w_ret_out": g_sq[:, 0:R], "w_conv_out": g_sq[:, R:2 * R],
        "w_o": g_sq[:, 2 * R:3 * R], "conv_w": lax.dynamic_slice_in_dim(gsm[:, 0:CONV_K], chip[0] * Cc, Cc, axis=2),
        "conv_b": gsm[:, ROW_CB], "conv_ln_g": gsm[:, ROW_LG], "conv_ln_b": gsm[:, ROW_LB], "post_norm_g": gsm[:, ROW_POST],
    }
    weights = dict(pre_norm_g=pre_norm_g, w_in=w_in, w_ret_out=w_ret_out, conv_w=conv_w, conv_b=conv_b,
                   conv_ln_g=conv_ln_g, conv_ln_b=conv_ln_b, w_conv_out=w_conv_out, w_o=w_o, post_norm_g=post_norm_g)
    m1 = dict(pre_norm_g=m_pre_norm_g, w_in=m_w_in, w_ret_out=m_w_ret_out, conv_w=m_conv_w, conv_b=m_conv_b,
              conv_ln_g=m_conv_ln_g, conv_ln_b=m_conv_ln_b, w_conv_out=m_w_conv_out, w_o=m_w_o, post_norm_g=m_post_norm_g)
    m2 = dict(pre_norm_g=v_pre_norm_g, w_in=v_w_in, w_ret_out=v_w_ret_out, conv_w=v_conv_w, conv_b=v_conv_b,
              conv_ln_g=v_conv_ln_g, conv_ln_b=v_conv_ln_b, w_conv_out=v_w_conv_out, w_o=v_w_o, post_norm_g=v_post_norm_g)
    order = ["pre_norm_g", "w_in", "w_ret_out", "conv_w", "conv_b", "conv_ln_g", "conv_ln_b", "w_conv_out", "w_o", "post_norm_g"]
    upd = {n: _adamw(weights[n], grads[n], m1[n], m2[n]) for n in order}
    return (loss, grad_x[None], *[grads[n] for n in order], *[upd[n][0] for n in order],
            *[upd[n][1] for n in order], *[upd[n][2] for n in order])
```

```python
import numpy as np
import jax
import jax.numpy as jnp
from jax import lax
from jax.experimental import pallas as pl
from jax.experimental.pallas import tpu as pltpu

F32 = jnp.float32
BF16 = jnp.bfloat16

HEADS = 8
DK = 64
DV = 128
CONV_K = 31
CHUNK = 64
ROPE_BASE = 10000.0
EPS = 1e-6
HALO = 32
CONV_RB = 32

ADAM_LR = 0.001
ADAM_B1 = 0.9
ADAM_B2 = 0.999
ADAM_EPS = 1e-08
ADAM_WD = 0.01
ADAM_STEP = 10
ADAM_BLOCK_BYTES = 2 * 1024 * 1024

VMEM_LIMIT = 56 * 1024 * 1024
MESH_T = pl.DeviceIdType.MESH
ANY = pl.BlockSpec(memory_space=pl.ANY)

ROW_CB, ROW_LG, ROW_LB = 32, 33, 34
ROW_PRE, ROW_POST = 40, 48


def _cp(sem=None, **kw):
    return pltpu.CompilerParams(dimension_semantics=sem, vmem_limit_bytes=VMEM_LIMIT, **kw)


def _dot(a, b):
    return jnp.dot(a, b, preferred_element_type=F32)


def _dot_nt(a, b):
    return lax.dot_general(a, b, (((1,), (1,)), ((), ())), preferred_element_type=F32)


def _dot_tn(a, b):
    return lax.dot_general(a, b, (((0,), (0,)), ((), ())), preferred_element_type=F32)


def _sigmoid(x):
    return jax.nn.sigmoid(x)


def _silu(x):
    return x * _sigmoid(x)


def _rms(x, g):
    return x * lax.rsqrt(jnp.mean(x * x, axis=-1, keepdims=True) + EPS) * g


def _gn_gate(o, g):
    mu = jnp.mean(o, axis=-1, keepdims=True)
    d = o - mu
    var = jnp.mean(d * d, axis=-1, keepdims=True)
    return d * lax.rsqrt(var + EPS) * _silu(g)


def _ln_gate(y, gc, lg, lb):
    mu = jnp.mean(y, axis=-1, keepdims=True)
    d = y - mu
    var = jnp.mean(d * d, axis=-1, keepdims=True)
    return _silu(d * lax.rsqrt(var + EPS) * lg + lb) * _silu(gc)


def _tables(T, L):
    lane = np.arange(128)
    d = lane % DK
    half = DK // 2
    inv = (ROPE_BASE ** (-(np.arange(half, dtype=np.float32)) / half)).astype(np.float32)
    ang = (np.arange(T, dtype=np.float32)[:, None] * inv[None, :]).astype(np.float64)
    angl = ang[:, d % half]
    cos = np.cos(angl)
    sin = np.sin(angl)
    lo = (d < half)[None, :]
    rope = np.stack([cos, np.where(lo, -sin, 0.0), np.where(lo, 0.0, sin)]).astype(np.float32)

    hh = np.arange(HEADS, dtype=np.float64)
    log_g = np.log1p(-np.exp2(-5.0 - hh))
    n = np.arange(L, dtype=np.float64)
    cn = np.arange(L) // CHUNK
    allowed = (cn[None, :] <= cn[:, None])
    dist = np.abs(n[:, None] - n[None, :])
    mask = np.exp(log_g[:, None, None] * dist[None]) * allowed[None]
    mq = ((lane[None, :] // DK) == (np.arange(HEADS)[:, None] % 2)).astype(np.float64)
    qd = np.exp(log_g[:, None] * n[None, :])
    kd = np.exp(log_g[:, None] * (L - n[None, :]))
    qdm = qd[:, :, None] * mq[:, None, :] * (DK ** -0.5)
    kdm = kd[:, :, None] * mq[:, None, :]
    mqs = np.broadcast_to((mq * (DK ** -0.5))[:, None, :], (HEADS, 8, 128))
    cd = np.broadcast_to(np.exp(log_g * L)[:, None, None], (HEADS, 8, 128))
    f = lambda a: jnp.asarray(np.ascontiguousarray(a), dtype=F32)
    return dict(rope=f(rope), mask=f(mask), qdm=f(qdm), kdm=f(kdm), mqs=f(mqs), cd=f(cd))


def _rot(b, c, sl, sh):
    return b * c + pltpu.roll(b, 96, axis=1) * sl + pltpu.roll(b, 32, axis=1) * sh


def _rot_t(d, c, sl, sh):
    return d * c + pltpu.roll(d * sl, 32, axis=1) + pltpu.roll(d * sh, 96, axis=1)


def _place():
    return lax.axis_index("x"), lax.axis_index("y"), lax.axis_index("c")


def _other_chips(x, y):
    return [(1 - x, y), (x, 1 - y), (1 - x, 1 - y)]


def _remote(src, dst, send_sems, recv_sems, k, to):
    return pltpu.make_async_remote_copy(src_ref=src, dst_ref=dst, send_sem=send_sems.at[k], recv_sem=recv_sems.at[k],
                                        device_id=to, device_id_type=MESH_T)


class _Ride:
    def __init__(self):
        self.arrays, self.kinds, self.names = [], [], []
        self.fresh = []
        self.ops = []

    def read(self, name, a):
        self.names.append(name)
        self.arrays.append(a)
        self.kinds.append("in")

    def inout(self, name, a):
        self.names.append(name)
        self.arrays.append(a)
        self.kinds.append("inout")

    def land(self, name, shape, dtype):
        self.fresh.append((name, jax.ShapeDtypeStruct(shape, dtype)))

    def op(self, n_sems, start, finish):
        self.ops.append((n_sems, start, finish))


def _pcall(body, *, name, grid, in_specs, out_specs, out_shape, args, scratch_shapes=(), sem, aliases=None, ride=None):
    if ride is None or not ride.ops:
        outs = pl.pallas_call(body, name=name, grid=grid, in_specs=list(in_specs), out_specs=tuple(out_specs),
                              out_shape=tuple(out_shape), scratch_shapes=list(scratch_shapes),
                              input_output_aliases=dict(aliases or {}), compiler_params=_cp(sem))(*args)
        return outs, {}

    ni, no, nr = len(args), len(out_shape), len(ride.arrays)
    inout = [i for i, k in enumerate(ride.kinds) if k == "inout"]
    r_out_shapes = [jax.ShapeDtypeStruct(ride.arrays[i].shape, ride.arrays[i].dtype) for i in inout] + [s for _, s in ride.fresh]
    r_out_names = [ride.names[i] for i in inout] + [n for n, _ in ride.fresh]
    nro = len(r_out_shapes)
    n_sems = sum(n for n, _, _ in ride.ops)
    n_scr = len(scratch_shapes)
    nd = len(grid)

    def wrapped(*refs):
        ins, rin = refs[:ni], refs[ni:ni + nr]
        outs, rout = refs[ni + nr:ni + nr + no], refs[ni + nr + no:ni + nr + no + nro]
        scr = refs[ni + nr + no + nro:ni + nr + no + nro + n_scr]
        send_sems, recv_sems = refs[-2], refs[-1]
        view = {nm: r for nm, r, k in zip(ride.names, rin, ride.kinds) if k == "in"}
        view.update(dict(zip(r_out_names, rout)))
        first = pl.program_id(0) == 0
        last = pl.program_id(0) == grid[0] - 1
        for d in range(1, nd):
            first = first & (pl.program_id(d) == 0)
            last = last & (pl.program_id(d) == grid[d] - 1)

        @pl.when(first)
        def _():
            base = 0
            for n, start, _ in ride.ops:
                start(view, send_sems, recv_sems, base)
                base += n

        body(*ins, *outs, *scr)

        @pl.when(last)
        def _():
            base = 0
            for n, _, finish in ride.ops:
                finish(view, send_sems, recv_sems, base)
                base += n

    res = pl.pallas_call(
        wrapped, name=name, grid=grid,
        in_specs=list(in_specs) + [ANY] * nr, out_specs=tuple(out_specs) + (ANY,) * nro,
        out_shape=tuple(out_shape) + tuple(r_out_shapes),
        scratch_shapes=list(scratch_shapes) + [pltpu.SemaphoreType.DMA((n_sems,)), pltpu.SemaphoreType.DMA((n_sems,))],
        input_output_aliases={**dict(aliases or {}), **{ni + i: no + j for j, i in enumerate(inout)}},
        compiler_params=_cp(sem),
    )(*args, *ride.arrays)
    return res[:no], dict(zip(r_out_names, res[no:]))


def _half(ref, chip_idx, cc):
    n = ref.shape[1] // 2
    return ref.at[chip_idx, pl.ds(cc * n, n)]


def _ride_gather_ici(ride, name, a):
    ride.inout(name, a)

    def start(view, ss, rs, b):
        x, y, c = _place()
        mine = _half(view[name], 2 * x + y, c)
        for j, (cx, cy) in enumerate(_other_chips(x, y)):
            _remote(mine, mine, ss, rs, b + j, (cx, cy, c)).start()

    def finish(view, ss, rs, b):
        x, y, c = _place()
        mine = _half(view[name], 2 * x + y, c)
        for j, (cx, cy) in enumerate(_other_chips(x, y)):
            theirs = _half(view[name], 2 * cx + cy, c)
            _remote(theirs, theirs, ss, rs, b + j, (cx, cy, c)).wait_recv()
        for j, (cx, cy) in enumerate(_other_chips(x, y)):
            _remote(mine, mine, ss, rs, b + j, (cx, cy, c)).wait_send()

    ride.op(3, start, finish)


def _ride_gather_pass(ride, name, a):
    ride.inout(name, a)

    def start(view, ss, rs, b):
        x, y, c = _place()
        for j, (cx, cy) in enumerate(_other_chips(x, y)):
            blk = _half(view[name], 2 * cx + cy, c)
            _remote(blk, blk, ss, rs, b + j, (x, y, 1 - c)).start()

    def finish(view, ss, rs, b):
        x, y, c = _place()
        for j, (cx, cy) in enumerate(_other_chips(x, y)):
            theirs = _half(view[name], 2 * cx + cy, 1 - c)
            _remote(theirs, theirs, ss, rs, b + j, (x, y, 1 - c)).wait_recv()
        for j, (cx, cy) in enumerate(_other_chips(x, y)):
            blk = _half(view[name], 2 * cx + cy, c)
            _remote(blk, blk, ss, rs, b + j, (x, y, 1 - c)).wait_send()

    ride.op(3, start, finish)


def _ride_exchange(ride, src_name, src, dst_name, dst, rels):
    ride.read(src_name, src)
    if dst is None:
        ride.land(dst_name, (3,) + src.shape[1:], src.dtype)
    else:
        ride.inout(dst_name, dst)

    def start(view, ss, rs, b):
        x, y, c = _place()
        chips = _other_chips(x, y)
        for i, j in enumerate(rels):
            cx, cy = chips[j]
            _remote(view[src_name].at[2 * cx + cy], view[dst_name].at[j], ss, rs, b + i, (cx, cy, c)).start()

    def finish(view, ss, rs, b):
        x, y, c = _place()
        chips = _other_chips(x, y)
        for i, j in enumerate(rels):
            cx, cy = chips[j]
            _remote(view[src_name].at[2 * cx + cy], view[dst_name].at[j], ss, rs, b + i, (cx, cy, c)).wait()

    ride.op(len(rels), start, finish)


def _ride_swap(ride, src_name, src, dst_name):
    ride.read(src_name, src)
    ride.land(dst_name, src.shape, src.dtype)

    def start(view, ss, rs, b):
        x, y, c = _place()
        _remote(view[src_name], view[dst_name], ss, rs, b, (x, y, 1 - c)).start()

    def finish(view, ss, rs, b):
        x, y, c = _place()
        _remote(view[src_name], view[dst_name], ss, rs, b, (x, y, 1 - c)).wait()

    ride.op(1, start, finish)


def _cast_win(chip, w_in, l):
    _, D, W = w_in.shape
    tr = min(256, D)

    def body(chip_ref, w_ref, o_ref):
        o_ref[...] = w_ref[...].astype(BF16)

    return pl.pallas_call(
        body, name="cast_win",
        out_shape=jax.ShapeDtypeStruct((4, D, W), BF16),
        grid_spec=pltpu.PrefetchScalarGridSpec(
            num_scalar_prefetch=1, grid=(D // tr,),
            in_specs=[pl.BlockSpec((None, tr, W), lambda r, c: (l, r, 0))],
            out_specs=pl.BlockSpec((None, tr, W), lambda r, c: (c[0], r, 0))),
        compiler_params=_cp(("arbitrary",)),
    )(chip, w_in)


def _cast_wsq(chip, w_ro, w_co, w_o, l):
    _, R, D = w_ro.shape

    def body(chip_ref, a_ref, b_ref, c_ref, o_ref):
        o_ref[0:R, :] = a_ref[...].astype(BF16)
        o_ref[R:2 * R, :] = b_ref[...].astype(BF16)
        o_ref[2 * R:3 * R, :] = c_ref[...].astype(BF16)

    spec = pl.BlockSpec((None, R, D), lambda i, c: (l, 0, 0))
    return pl.pallas_call(
        body, name="cast_wsq",
        out_shape=jax.ShapeDtypeStruct((4, 3 * R, D), BF16),
        grid_spec=pltpu.PrefetchScalarGridSpec(
            num_scalar_prefetch=1, grid=(1,),
            in_specs=[spec, spec, spec],
            out_specs=pl.BlockSpec((None, 3 * R, D), lambda i, c: (c[0], 0, 0))),
        compiler_params=_cp(("arbitrary",)),
    )(chip, w_ro, w_co, w_o)


def _place_cw(chip, conv_w):
    NL, K, Cc = conv_w.shape

    def body(chip_ref, w_ref, o_ref):
        o_ref[...] = w_ref[...]

    return pl.pallas_call(
        body, name="place_cw",
        out_shape=jax.ShapeDtypeStruct((4, NL, K, Cc), F32),
        grid_spec=pltpu.PrefetchScalarGridSpec(
            num_scalar_prefetch=1, grid=(1,),
            in_specs=[pl.BlockSpec((NL, K, Cc), lambda i, c: (0, 0, 0))],
            out_specs=pl.BlockSpec((None, NL, K, Cc), lambda i, c: (c[0], 0, 0, 0))),
        compiler_params=_cp(("arbitrary",)),
    )(chip, conv_w)


def _gather_first(win0, wsq0, wcw):
    n_arr = 3

    def body(a0, a1, a2, o0, o1, o2, send_sems, recv_sems):
        x, y, c = _place()
        sibling = (x, y, 1 - c)
        chips = _other_chips(x, y)
        outs = (o0, o1, o2)

        def copy(k, a, cx, cy, cc, to):
            blk = _half(outs[a], 2 * cx + cy, cc)
            return _remote(blk, blk, send_sems, recv_sems, k, to)

        first = [copy(3 * a + j, a, x, y, c, (*chip, c)) for a in range(n_arr) for j, chip in enumerate(chips)]
        for cp in first:
            cp.start()
        passed = [copy(3 * n_arr + 3 * a + j, a, *chip, c, sibling) for a in range(n_arr) for j, chip in enumerate(chips)]
        for a in range(n_arr):
            for j, chip in enumerate(chips):
                copy(3 * a + j, a, *chip, c, sibling).wait_recv()
                passed[3 * a + j].start()
        for a in range(n_arr):
            for j, chip in enumerate(chips):
                copy(3 * n_arr + 3 * a + j, a, *chip, 1 - c, sibling).wait_recv()
        for cp in first + passed:
            cp.wait_send()

    ins = (win0, wsq0, wcw)
    return pl.pallas_call(
        body, name="gather_first",
        out_shape=tuple(jax.ShapeDtypeStruct(a.shape, a.dtype) for a in ins),
        in_specs=[ANY] * 3, out_specs=(ANY,) * 3,
        scratch_shapes=[pltpu.SemaphoreType.DMA((6 * n_arr,)), pltpu.SemaphoreType.DMA((6 * n_arr,))],
        input_output_aliases={0: 0, 1: 1, 2: 2},
        compiler_params=pltpu.CompilerParams(has_side_effects=True),
    )(*ins)


def _fwd_in(x, pre_g, win, l, ride=None):
    T, D = x.shape
    W = win.shape[-1]
    tm = min(512, T)

    def body(x_ref, g_ref, w_ref, p_ref, h_ref, hs):
        @pl.when(pl.program_id(1) == 0)
        def _():
            hb = _rms(x_ref[...], g_ref[...]).astype(BF16)
            hs[...] = hb
            h_ref[...] = hb

        p_ref[...] = _dot(hs[...], w_ref[...]).astype(BF16)

    return _pcall(
        body, name="fwd_in",
        out_shape=(jax.ShapeDtypeStruct((T, 4 * W), BF16), jax.ShapeDtypeStruct((T, D), BF16)),
        grid=(T // tm, 4),
        in_specs=[pl.BlockSpec((tm, D), lambda i, j: (i, 0)),
                  pl.BlockSpec((None, 1, D), lambda i, j: (l, 0, 0)),
                  pl.BlockSpec((None, D, W), lambda i, j: (j, 0, 0))],
        out_specs=(pl.BlockSpec((tm, W), lambda i, j: (i, j)),
                   pl.BlockSpec((tm, D), lambda i, j: (i, 0))),
        scratch_shapes=[pltpu.VMEM((tm, D), BF16)],
        sem=("arbitrary", "arbitrary"), args=(x, pre_g, win), ride=ride)


def _ret_specs(T, L):
    rope = pl.BlockSpec((3, L, 128), lambda s: (0, s, 0))
    mask = pl.BlockSpec((HEADS, L, L), lambda s: (0, 0, 0))
    qdm = pl.BlockSpec((HEADS, L, 128), lambda s: (0, 0, 0))
    small = pl.BlockSpec((HEADS, 8, 128), lambda s: (0, 0, 0))
    return [rope, mask, qdm, qdm, small, small]


def _ret_fwd(proj, tb, L):
    T = proj.shape[0]
    QK = HEADS * DK
    VW = HEADS * DV
    nS = T // L

    def body(p_ref, rope_ref, m_ref, qdm_ref, kdm_ref, mqs_ref, cd_ref, a_ref, st_ref, state):
        @pl.when(pl.program_id(0) == 0)
        def _():
            state[...] = jnp.zeros_like(state)

        c, sl, sh = rope_ref[0], rope_ref[1], rope_ref[2]
        for j in range(HEADS // 2):
            rq = _rot(p_ref[:, 128 * j:128 * (j + 1)].astype(F32), c, sl, sh)
            rk = _rot(p_ref[:, QK + 128 * j:QK + 128 * (j + 1)].astype(F32), c, sl, sh)
            rkb = rk.astype(BF16)
            for e in range(2):
                h = 2 * j + e
                v = p_ref[:, 2 * QK + DV * h:2 * QK + DV * (h + 1)]
                g = p_ref[:, 2 * QK + VW + DV * h:2 * QK + VW + DV * (h + 1)].astype(F32)
                a = (rq * mqs_ref[h, 0:1, :]).astype(BF16)
                p = (_dot_nt(a, rkb) * m_ref[h]).astype(BF16)
                st = state[h]
                st_ref[h] = st
                o = _dot(p, v) + _dot((rq * qdm_ref[h]).astype(BF16), st.astype(BF16))
                state[h] = st * cd_ref[h, 0:1, :] + _dot_tn((rk * kdm_ref[h]).astype(BF16), v)
                a_ref[:, DV * h:DV * (h + 1)] = _gn_gate(o, g).astype(BF16)

    return pl.pallas_call(
        body, name="ret_fwd",
        out_shape=(jax.ShapeDtypeStruct((T, VW), BF16), jax.ShapeDtypeStruct((nS, HEADS, 128, DV), F32)),
        grid=(nS,),
        in_specs=[pl.BlockSpec((L, 2 * QK + 2 * VW), lambda s: (s, 0))] + _ret_specs(T, L),
        out_specs=(pl.BlockSpec((L, VW), lambda s: (s, 0)),
                   pl.BlockSpec((None, HEADS, 128, DV), lambda s: (s, 0, 0, 0))),
        scratch_shapes=[pltpu.VMEM((HEADS, 128, DV), F32)],
        compiler_params=_cp(("arbitrary",)),
    )(proj, tb["rope"], tb["mask"], tb["qdm"], tb["kdm"], tb["mqs"], tb["cd"])


def _shift_copies(src, sh):
    rows = sh.shape[1]
    for b in range(1, 8):
        sh[b - 1, :, :] = src[pl.ds(b, rows), :]


def _window(src, sh, start, rows):
    b = start % 8
    if b == 0:
        return src[pl.ds(start, rows), :]
    return sh[b - 1, pl.ds(start - b, rows), :]


def _conv_taps(wbuf, src, sh, base, rows):
    acc = wbuf[pl.ds(0, 1), :] * _window(src, sh, base, rows)
    for k in range(1, CONV_K):
        acc = acc + wbuf[pl.ds(k, 1), :] * _window(src, sh, base + k, rows)
    return acc


def _load_conv_w(cw_ref, wbuf, flip):
    for k in range(CONV_K):
        row = jnp.concatenate([cw_ref[c, pl.ds(k, 1), :] for c in range(4)], axis=-1)
        wbuf[pl.ds(CONV_K - 1 - k if flip else k, 1), :] = row
    wbuf[pl.ds(CONV_K, 1), :] = jnp.zeros_like(wbuf[pl.ds(CONV_K, 1), :])


def _conv_fwd(proj, wcw, conv_b, ln_g, ln_b, l, ride=None):
    T = proj.shape[0]
    C = conv_b.shape[-1]
    Cc = wcw.shape[-1]
    tc = min(256, T)
    off = HALO - (CONV_K - 1)

    def body(p_ref, cw_ref, cb_ref, lg_ref, lb_ref, a_ref, y_ref, ubuf, wbuf, ush):
        i = pl.program_id(0)

        @pl.when(i == 0)
        def _():
            ubuf[0:HALO, :] = jnp.zeros((HALO, C), F32)
            _load_conv_w(cw_ref, wbuf, False)

        @pl.when(i > 0)
        def _():
            ubuf[0:HALO, :] = ubuf[tc:tc + HALO, :]

        ga = p_ref[:, 0:C].astype(F32)
        gb = p_ref[:, C:2 * C].astype(F32)
        ubuf[HALO:HALO + tc, :] = ga * _sigmoid(gb)
        _shift_copies(ubuf, ush)
        for r in range(tc // CONV_RB):
            y_ref[r * CONV_RB:(r + 1) * CONV_RB, :] = _conv_taps(wbuf, ubuf, ush, r * CONV_RB + off, CONV_RB) + cb_ref[...]
        gc = p_ref[:, 2 * C:3 * C].astype(F32)
        a_ref[...] = _ln_gate(y_ref[...], gc, lg_ref[...], lb_ref[...]).astype(BF16)

    vec = pl.BlockSpec((None, 1, C), lambda i: (l, 0, 0))
    return _pcall(
        body, name="conv_fwd",
        out_shape=(jax.ShapeDtypeStruct((T, C), BF16), jax.ShapeDtypeStruct((T, C), F32)),
        grid=(T // tc,),
        in_specs=[pl.BlockSpec((tc, 3 * C), lambda i: (i, 1)),
                  pl.BlockSpec((4, None, CONV_K, Cc), lambda i: (0, l, 0, 0)),
                  vec, vec, vec],
        out_specs=(pl.BlockSpec((tc, C), lambda i: (i, 0)), pl.BlockSpec((tc, C), lambda i: (i, 0))),
        scratch_shapes=[pltpu.VMEM((HALO + tc, C), F32), pltpu.VMEM((HALO, C), F32), pltpu.VMEM((7, HALO + tc - 8, C), F32)],
        sem=("arbitrary",), args=(proj, wcw, conv_b, ln_g, ln_b), ride=ride)


def _merge_fwd(x, proj, a_ret, a_conv, wsq, post_g, l):
    T, D = x.shape
    R = wsq.shape[1] // 3
    tm = min(512, T)

    def body(x_ref, p_ref, ar_ref, ac_ref, wro_ref, wco_ref, wo_ref, g_ref, xn_ref, ro_ref, co_ref, ym_ref, z_ref):
        ro = _dot(ar_ref[...], wro_ref[...].reshape(4 * R, D))
        co = _dot(ac_ref[...], wco_ref[...].reshape(4 * R, D))
        ym = (_sigmoid(p_ref[:, 0:D].astype(F32)) * ro + _sigmoid(p_ref[:, D:2 * D].astype(F32)) * co).astype(BF16)
        z = _dot(ym, wo_ref[...].reshape(4 * R, D))
        ro_ref[...] = ro.astype(BF16)
        co_ref[...] = co.astype(BF16)
        ym_ref[...] = ym
        z_ref[...] = z.astype(BF16)
        xn_ref[...] = x_ref[...] + _rms(z, g_ref[...])

    tile = pl.BlockSpec((tm, D), lambda i: (i, 0))
    wspec = lambda m: pl.BlockSpec((4, R, D), lambda i: (0, m, 0))
    act = jax.ShapeDtypeStruct((T, D), BF16)
    return pl.pallas_call(
        body, name="merge_fwd",
        out_shape=(jax.ShapeDtypeStruct((T, D), F32), act, act, act, act),
        grid=(T // tm,),
        in_specs=[tile, pl.BlockSpec((tm, 2 * D), lambda i: (i, 3)), tile, tile,
                  wspec(0), wspec(1), wspec(2), pl.BlockSpec((None, 1, D), lambda i: (l, 0, 0))],
        out_specs=(tile, tile, tile, tile, tile),
        compiler_params=_cp(("arbitrary",)),
    )(x, proj, a_ret, a_conv, wsq, wsq, wsq, post_g)


def _loss_fwd_bwd(y, target):
    T, D = y.shape
    tm = min(512, T)

    def body(y_ref, t_ref, dy_ref, ls_ref):
        @pl.when(pl.program_id(0) == 0)
        def _():
            ls_ref[...] = jnp.zeros_like(ls_ref)

        e = y_ref[...] - t_ref[...]
        dy_ref[...] = e * (1.0 / D)
        ls_ref[...] += jnp.sum((e * e).reshape(tm // 8, 8, D), axis=0) * (0.5 / D)

    tile = pl.BlockSpec((tm, D), lambda i: (i, 0))
    return pl.pallas_call(
        body, name="loss",
        out_shape=(jax.ShapeDtypeStruct((T, D), F32), jax.ShapeDtypeStruct((8, D), F32)),
        grid=(T // tm,),
        in_specs=[tile, tile],
        out_specs=(tile, pl.BlockSpec((8, D), lambda i: (0, 0))),
        compiler_params=_cp(("arbitrary",)),
    )(y, target)


def _merge_bwd(dxn, proj, a_ret, a_conv, ro, co, ym, z, wsq, post_g, l, ride=None):
    T, D = dxn.shape
    R = wsq.shape[1] // 3
    tm = min(256, T)
    n = T // tm

    def body(dx_ref, p_ref, ar_ref, ac_ref, ro_ref, co_ref, ym_ref, z_ref, wro_ref, wco_ref, wo_ref, g_ref,
             dp_ref, dar_ref, dac_ref, gsq_ref, dg_ref, acc, stage):
        i = pl.program_id(0)

        @pl.when(i == 0)
        def _():
            acc[...] = jnp.zeros_like(acc)
            dg_ref[...] = jnp.zeros_like(dg_ref)

        _, vjp = jax.vjp(_rms, z_ref[...].astype(F32), g_ref[...])
        dz, dg = vjp(dx_ref[...])
        dg_ref[0:1, :] += dg
        dzb = dz.astype(BF16)
        dym = _dot_nt(dzb, wo_ref[...].reshape(4 * R, D))
        acc[2] += _dot_tn(ym_ref[...], dzb)
        sr = _sigmoid(p_ref[:, 0:D].astype(F32))
        sc = _sigmoid(p_ref[:, D:2 * D].astype(F32))
        rov = ro_ref[...].astype(F32)
        cov = co_ref[...].astype(F32)
        dp_ref[:, 0:D] = (dym * rov * sr * (1.0 - sr)).astype(BF16)
        dp_ref[:, D:2 * D] = (dym * cov * sc * (1.0 - sc)).astype(BF16)
        dro = (dym * sr).astype(BF16)
        dco = (dym * sc).astype(BF16)
        dar_ref[...] = _dot_nt(dro, wro_ref[...].reshape(4 * R, D)).astype(BF16)
        dac_ref[...] = _dot_nt(dco, wco_ref[...].reshape(4 * R, D)).astype(BF16)
        acc[0] += _dot_tn(ar_ref[...], dro)
        acc[1] += _dot_tn(ac_ref[...], dco)

        @pl.when(i == n - 1)
        def _():
            for m in range(3):
                stage[...] = acc[m].astype(BF16).reshape(4, R, D)
                pltpu.sync_copy(stage, gsq_ref.at[:, pl.ds(m * R, R), :])

    tile = pl.BlockSpec((tm, D), lambda i: (i, 0))
    wspec = lambda m: pl.BlockSpec((4, R, D), lambda i: (0, m, 0), pipeline_mode=pl.Buffered(1))
    return _pcall(
        body, name="merge_bwd",
        out_shape=(jax.ShapeDtypeStruct(proj.shape, BF16), jax.ShapeDtypeStruct((T, D), BF16),
                   jax.ShapeDtypeStruct((T, D), BF16), jax.ShapeDtypeStruct(wsq.shape, BF16),
                   jax.ShapeDtypeStruct((8, D), F32)),
        grid=(n,),
        in_specs=[tile, pl.BlockSpec((tm, 2 * D), lambda i: (i, 3)), tile, tile, tile, tile, tile, tile,
                  wspec(0), wspec(1), wspec(2), pl.BlockSpec((None, 1, D), lambda i: (l, 0, 0))],
        out_specs=(pl.BlockSpec((tm, 2 * D), lambda i: (i, 3)), tile, tile, ANY, pl.BlockSpec((8, D), lambda i: (0, 0))),
        scratch_shapes=[pltpu.VMEM((3, 4 * R, D), F32), pltpu.VMEM((4, R, D), BF16)],
        sem=("arbitrary",), args=(dxn, proj, a_ret, a_conv, ro, co, ym, z, wsq, wsq, wsq, post_g), ride=ride)


def _conv_bwd(dproj, da_conv, y, proj, wcw, ln_g, ln_b, l, ride=None):
    T, C = y.shape
    Cc = wcw.shape[-1]
    tc = min(256, T)
    n = T // tc
    hb = tc // HALO
    off = HALO - (CONV_K - 1)
    nrb = tc // CONV_RB

    def body(dpin_ref, da_ref, y_ref, p_ref, ph_ref, cw_ref, lg_ref, lb_ref, dp_ref, sg_ref,
             dcbuf, ubuf, dubuf, wbuf, dwacc, vacc, dsh, ush):
        t = pl.program_id(0)
        i = n - 1 - t

        @pl.when(t == 0)
        def _():
            dcbuf[tc:tc + HALO, :] = jnp.zeros((HALO, C), F32)
            dwacc[...] = jnp.zeros_like(dwacc)
            vacc[...] = jnp.zeros_like(vacc)
            _load_conv_w(cw_ref, wbuf, True)

        @pl.when(t > 0)
        def _():
            dcbuf[tc:tc + HALO, :] = dcbuf[0:HALO, :]

        gc = p_ref[:, 2 * C:3 * C].astype(F32)
        _, vjp = jax.vjp(_ln_gate, y_ref[...], gc, lg_ref[...], lb_ref[...])
        dy, dgc, dlg, dlb = vjp(da_ref[...].astype(F32))
        dcbuf[0:tc, :] = dy
        dp_ref[:, 2 * C:3 * C] = dgc.astype(BF16)
        vacc[0:1, :] += jnp.sum(dy, axis=0, keepdims=True)
        vacc[1:2, :] += dlg
        vacc[2:3, :] += dlb

        ga = p_ref[:, 0:C].astype(F32)
        sb = _sigmoid(p_ref[:, C:2 * C].astype(F32))
        ubuf[HALO:HALO + tc, :] = ga * sb
        uh = ph_ref[:, 0:C].astype(F32) * _sigmoid(ph_ref[:, C:2 * C].astype(F32))
        ubuf[0:HALO, :] = jnp.where(i > 0, uh, 0.0)

        _shift_copies(dcbuf, dsh)
        _shift_copies(ubuf, ush)
        for r in range(nrb):
            dubuf[r * CONV_RB:(r + 1) * CONV_RB, :] = _conv_taps(wbuf, dcbuf, dsh, r * CONV_RB, CONV_RB)
        du = dubuf[...]
        dp_ref[:, 0:C] = (du * sb).astype(BF16)
        dp_ref[:, C:2 * C] = (du * ga * sb * (1.0 - sb)).astype(BF16)

        for r in range(nrb):
            dyb = dcbuf[r * CONV_RB:(r + 1) * CONV_RB, :]
            for k in range(CONV_K):
                pr = dyb * _window(ubuf, ush, r * CONV_RB + off + k, CONV_RB)
                dwacc[8 * k:8 * k + 8, :] += jnp.sum(pr.reshape(CONV_RB // 8, 8, C), axis=0)

        @pl.when(t == n - 1)
        def _():
            for k in range(CONV_K):
                sg_ref[pl.ds(k, 1), :] = jnp.sum(dwacc[8 * k:8 * k + 8, :], axis=0, keepdims=True)
            sg_ref[pl.ds(CONV_K, 1), :] = jnp.zeros((1, C), F32)
            sg_ref[ROW_CB:ROW_CB + 8, :] = jnp.zeros((8, C), F32)
            sg_ref[ROW_CB:ROW_CB + 3, :] = vacc[0:3, :]

    vec = pl.BlockSpec((None, 1, C), lambda t: (l, 0, 0))
    tile = pl.BlockSpec((tc, C), lambda t: (n - 1 - t, 0))
    ptile = pl.BlockSpec((tc, 3 * C), lambda t: (n - 1 - t, 1))
    halo = pl.BlockSpec((HALO, 3 * C), lambda t: (jnp.maximum((n - 1 - t) * hb - 1, 0), 1))
    return _pcall(
        body, name="conv_bwd",
        out_shape=(jax.ShapeDtypeStruct(dproj.shape, BF16), jax.ShapeDtypeStruct((ROW_PRE, C), F32)),
        grid=(n,),
        in_specs=[ANY, tile, tile, ptile, halo,
                  pl.BlockSpec((4, None, CONV_K, Cc), lambda t: (0, l, 0, 0)), vec, vec],
        out_specs=(ptile, pl.BlockSpec((ROW_PRE, C), lambda t: (0, 0))),
        scratch_shapes=[pltpu.VMEM((tc + HALO, C), F32), pltpu.VMEM((HALO + tc, C), F32), pltpu.VMEM((tc, C), F32),
                        pltpu.VMEM((HALO, C), F32), pltpu.VMEM((8 * CONV_K, C), F32), pltpu.VMEM((8, C), F32),
                        pltpu.VMEM((7, HALO + tc - 8, C), F32), pltpu.VMEM((7, HALO + tc - 8, C), F32)],
        sem=("arbitrary",), aliases={0: 0}, args=(dproj, da_conv, y, proj, proj, wcw, ln_g, ln_b), ride=ride)


def _ret_bwd(dproj, da_ret, proj, states, tb, L):
    T = proj.shape[0]
    QK = HEADS * DK
    VW = HEADS * DV
    nS = T // L
    PW = 2 * QK + 2 * VW

    def body(dpin_ref, da_ref, p_ref, st_ref, rope_ref, m_ref, qdm_ref, kdm_ref, mqs_ref, cd_ref, dp_ref, gst):
        @pl.when(pl.program_id(0) == 0)
        def _():
            gst[...] = jnp.zeros_like(gst)

        c, sl, sh = rope_ref[0], rope_ref[1], rope_ref[2]
        for j in range(HEADS // 2):
            rq = _rot(p_ref[:, 128 * j:128 * (j + 1)].astype(F32), c, sl, sh)
            rk = _rot(p_ref[:, QK + 128 * j:QK + 128 * (j + 1)].astype(F32), c, sl, sh)
            rkb = rk.astype(BF16)
            drq = jnp.zeros_like(rq)
            drk = jnp.zeros_like(rk)
            for e in range(2):
                h = 2 * j + e
                v = p_ref[:, 2 * QK + DV * h:2 * QK + DV * (h + 1)]
                g = p_ref[:, 2 * QK + VW + DV * h:2 * QK + VW + DV * (h + 1)].astype(F32)
                mqs = mqs_ref[h, 0:1, :]
                a = (rq * mqs).astype(BF16)
                aq = (rq * qdm_ref[h]).astype(BF16)
                kdv = (rk * kdm_ref[h]).astype(BF16)
                mk = m_ref[h]
                p = (_dot_nt(a, rkb) * mk).astype(BF16)
                stb = st_ref[h].astype(BF16)
                o = _dot(p, v) + _dot(aq, stb)
                _, vjp = jax.vjp(_gn_gate, o, g)
                do, dg = vjp(da_ref[:, DV * h:DV * (h + 1)].astype(F32))
                dob = do.astype(BF16)
                gs = gst[h]
                gsb = gs.astype(BF16)
                ds = (_dot_nt(dob, v) * mk).astype(BF16)
                drq = drq + _dot(ds, rkb) * mqs + _dot_nt(dob, stb) * qdm_ref[h]
                drk = drk + _dot_tn(ds, a) + _dot_nt(v, gsb) * kdm_ref[h]
                dv = _dot_tn(p, dob) + _dot(kdv, gsb)
                gst[h] = _dot_tn(aq, dob) + gs * cd_ref[h, 0:1, :]
                dp_ref[:, 2 * QK + DV * h:2 * QK + DV * (h + 1)] = dv.astype(BF16)
                dp_ref[:, 2 * QK + VW + DV * h:2 * QK + VW + DV * (h + 1)] = dg.astype(BF16)
            dp_ref[:, 128 * j:128 * (j + 1)] = _rot_t(drq, c, sl, sh).astype(BF16)
            dp_ref[:, QK + 128 * j:QK + 128 * (j + 1)] = _rot_t(drk, c, sl, sh).astype(BF16)

    rev = lambda s: nS - 1 - s
    specs = _ret_specs(T, L)
    specs[0] = pl.BlockSpec((3, L, 128), lambda s: (0, rev(s), 0))
    return pl.pallas_call(
        body, name="ret_bwd",
        out_shape=jax.ShapeDtypeStruct(dproj.shape, BF16),
        grid=(nS,),
        in_specs=[ANY, pl.BlockSpec((L, VW), lambda s: (rev(s), 0)), pl.BlockSpec((L, PW), lambda s: (rev(s), 0)),
                  pl.BlockSpec((None, HEADS, 128, DV), lambda s: (rev(s), 0, 0, 0))] + specs,
        out_specs=pl.BlockSpec((L, PW), lambda s: (rev(s), 0)),
        scratch_shapes=[pltpu.VMEM((HEADS, 128, DV), F32)],
        input_output_aliases={0: 0},
        compiler_params=_cp(("arbitrary",)),
    )(dproj, da_ret, proj, states, tb["rope"], tb["mask"], tb["qdm"], tb["kdm"], tb["mqs"], tb["cd"])


def _win_grad(h, dproj, W):
    T, D = h.shape
    tk = min(512, T)
    nk = T // tk

    def body(h_ref, dp_ref, g_ref, acc):
        k = pl.program_id(1)

        @pl.when(k == 0)
        def _():
            acc[...] = jnp.zeros_like(acc)

        acc[...] += _dot_tn(h_ref[...], dp_ref[...])

        @pl.when(k == nk - 1)
        def _():
            g_ref[...] = acc[...].astype(BF16)

    return pl.pallas_call(
        body, name="win_grad",
        out_shape=jax.ShapeDtypeStruct((4, D, W), BF16),
        grid=(4, nk),
        in_specs=[pl.BlockSpec((tk, D), lambda j, k: (k, 0)), pl.BlockSpec((tk, W), lambda j, k: (k, j))],
        out_specs=pl.BlockSpec((None, D, W), lambda j, k: (j, 0, 0)),
        scratch_shapes=[pltpu.VMEM((D, W), F32)],
        compiler_params=_cp(("arbitrary", "arbitrary")),
    )(h, dproj)


def _in_bwd(dxn, dproj, x, pre_g, win, l, ride=None):
    T, D = x.shape
    W = win.shape[-1]
    tm = min(512, T)

    def body(dxn_ref, dp_ref, x_ref, g_ref, w_ref, dx_ref, dg_ref, acc):
        i = pl.program_id(0)
        j = pl.program_id(1)

        @pl.when(j == 0)
        def _():
            acc[...] = jnp.zeros_like(acc)

        @pl.when((i == 0) & (j == 0))
        def _():
            dg_ref[...] = jnp.zeros_like(dg_ref)

        acc[...] += _dot_nt(dp_ref[...], w_ref[...])

        @pl.when(j == 3)
        def _():
            _, vjp = jax.vjp(_rms, x_ref[...], g_ref[...])
            dx, dg = vjp(acc[...])
            dx_ref[...] = dxn_ref[...] + dx
            dg_ref[0:1, :] += dg

    tile = pl.BlockSpec((tm, D), lambda i, j: (i, 0))
    return _pcall(
        body, name="in_bwd",
        out_shape=(jax.ShapeDtypeStruct((T, D), F32), jax.ShapeDtypeStruct((8, D), F32)),
        grid=(T // tm, 4),
        in_specs=[tile, pl.BlockSpec((tm, W), lambda i, j: (i, j)), tile,
                  pl.BlockSpec((None, 1, D), lambda i, j: (l, 0, 0)),
                  pl.BlockSpec((None, D, W), lambda i, j: (j, 0, 0))],
        out_specs=(tile, pl.BlockSpec((8, D), lambda i, j: (0, 0))),
        scratch_shapes=[pltpu.VMEM((tm, D), F32)],
        sem=("arbitrary", "arbitrary"), args=(dxn, dproj, x, pre_g, win), ride=ride)


def _sum_group(chip, t, u):
    _, A, B = t.shape
    tr = min(256, A)

    def body(k_ref, t_ref, u_ref, o_ref):
        o_ref[...] = ((t_ref[...].astype(F32) + u_ref[0].astype(F32)) + u_ref[1].astype(F32)) + u_ref[2].astype(F32)

    return pl.pallas_call(
        body, name="sum_group",
        out_shape=jax.ShapeDtypeStruct((A, B), F32),
        grid_spec=pltpu.PrefetchScalarGridSpec(
            num_scalar_prefetch=1, grid=(A // tr,),
            in_specs=[pl.BlockSpec((None, tr, B), lambda i, k: (k[0], i, 0)),
                      pl.BlockSpec((3, tr, B), lambda i, k: (0, i, 0))],
            out_specs=pl.BlockSpec((tr, B), lambda i, k: (i, 0))),
        compiler_params=_cp(("arbitrary",)),
    )(chip, t, u)


def _adam_math(w, g, m, v):
    c1 = 1.0 / (1.0 - ADAM_B1 ** ADAM_STEP)
    c2 = 1.0 / (1.0 - ADAM_B2 ** ADAM_STEP)
    nm = ADAM_B1 * m + (1.0 - ADAM_B1) * g
    nv = ADAM_B2 * v + (1.0 - ADAM_B2) * (g * g)
    return -ADAM_LR * ((nm * c1) / (jnp.sqrt(nv * c2) + ADAM_EPS) + ADAM_WD * w), nm, nv


def _adamw_layer(prev, w, m, v, sa, sb, l, part, ride=None):
    NL, A, B = w.shape
    tr = A
    while tr * B * 4 > ADAM_BLOCK_BYTES and tr % 16 == 0:
        tr //= 2
    nb = A // tr

    def body(p0, p1, p2, p3, w_ref, m_ref, v_ref, sa_ref, sb_ref, g_ref, d_ref, nm_ref, nv_ref):
        g = sa_ref[...] + sb_ref[...]
        g_ref[...] = g
        d_ref[...], nm_ref[...], nv_ref[...] = _adam_math(w_ref[...], g, m_ref[...], v_ref[...])

    lay = pl.BlockSpec((None, tr, B), lambda i: (l, i, 0))
    src = pl.BlockSpec((tr, B), lambda i: (part * nb + i, 0))
    full = jax.ShapeDtypeStruct((NL, A, B), F32)
    if prev is None:
        prev = tuple(lax.empty((NL, A, B), F32) for _ in range(4))
    outs, landed = _pcall(
        body, name="adamw_layer",
        out_shape=(full,) * 4, grid=(nb,),
        in_specs=[ANY] * 4 + [lay, lay, lay, src, src], out_specs=(lay,) * 4,
        sem=("arbitrary",), aliases={0: 0, 1: 1, 2: 2, 3: 3}, args=(*prev, w, m, v, sa, sb), ride=ride)
    return tuple(outs), landed


def _adamw(w, g, m, v):
    shape = w.shape
    cols = shape[-1]
    rows = int(np.prod(shape[:-1]))

    def body(w_ref, g_ref, m_ref, v_ref, d_ref, nm_ref, nv_ref):
        d_ref[...], nm_ref[...], nv_ref[...] = _adam_math(w_ref[...], g_ref[...], m_ref[...], v_ref[...])

    tile = pl.BlockSpec((rows, cols), lambda i: (0, 0))
    out = jax.ShapeDtypeStruct((rows, cols), F32)
    res = pl.pallas_call(
        body, name="adamw",
        out_shape=(out, out, out), grid=(1,),
        in_specs=[tile] * 4, out_specs=(tile,) * 3,
        compiler_params=_cp(("arbitrary",)),
    )(*[a.reshape(rows, cols) for a in (w, g, m, v)])
    return tuple(a.reshape(shape) for a in res)


def _gather_small(small):
    def body(s_ref, o_ref, send_sems, recv_sems, local_sem):
        x, y, c = _place()
        me = 4 * x + 2 * y + c
        mine = pltpu.make_async_copy(s_ref, o_ref.at[me], local_sem)
        mine.start()
        flip = lambda v, b: 1 - v if b else v
        sends = []
        for r in range(1, 8):
            peer = (flip(x, r & 4), flip(y, r & 2), flip(c, r & 1))
            sends.append(_remote(s_ref, o_ref.at[me], send_sems, recv_sems, r - 1, peer))
        for cp in sends:
            cp.start()
        for r in range(1, 8):
            peer = (flip(x, r & 4), flip(y, r & 2), flip(c, r & 1))
            theirs = o_ref.at[4 * peer[0] + 2 * peer[1] + peer[2]]
            _remote(theirs, theirs, send_sems, recv_sems, r - 1, peer).wait_recv()
        for cp in sends:
            cp.wait_send()
        mine.wait()

    return pl.pallas_call(
        body, name="gather_small",
        out_shape=jax.ShapeDtypeStruct((8,) + small.shape, small.dtype),
        in_specs=[ANY], out_specs=ANY,
        scratch_shapes=[pltpu.SemaphoreType.DMA((7,)), pltpu.SemaphoreType.DMA((7,)), pltpu.SemaphoreType.DMA],
        compiler_params=pltpu.CompilerParams(has_side_effects=True),
    )(small)


def _sum_devices(gs):
    _, NL, R, D = gs.shape

    def body(g_ref, o_ref):
        acc = g_ref[0]
        for k in range(1, 8):
            acc = acc + g_ref[k]
        o_ref[...] = acc

    return pl.pallas_call(
        body, name="sum_devices",
        out_shape=jax.ShapeDtypeStruct((NL, R, D), F32),
        grid=(NL,),
        in_specs=[pl.BlockSpec((8, None, R, D), lambda l: (0, l, 0, 0))],
        out_specs=pl.BlockSpec((None, R, D), lambda l: (l, 0, 0)),
        compiler_params=_cp(("arbitrary",)),
    )(gs)


def kernel(x, pre_norm_g, w_in, w_ret_out, conv_w, conv_b, conv_ln_g, conv_ln_b, w_conv_out, w_o, post_norm_g, loss_target, m_pre_norm_g, m_w_in, m_w_ret_out, m_conv_w, m_conv_b, m_conv_ln_g, m_conv_ln_b, m_w_conv_out, m_w_o, m_post_norm_g, v_pre_norm_g, v_w_in, v_w_ret_out, v_conv_w, v_conv_b, v_conv_ln_g, v_conv_ln_b, v_w_conv_out, v_w_o, v_post_norm_g):
    NL, D, W = w_in.shape
    Cc = conv_w.shape[-1]
    T = x.shape[1]
    L = min(256, T)
    tb = _tables(T, L)
    ax, ay, _ = _place()
    chip = (2 * ax + ay).astype(jnp.int32).reshape(1)
    pre_g, cb, lg, lb, post_g = (a.reshape(NL, 1, D) for a in (pre_norm_g, conv_b, conv_ln_g, conv_ln_b, post_norm_g))

    win = [_cast_win(chip, w_in, l) for l in range(NL)]
    wsq = [_cast_wsq(chip, w_ret_out, w_conv_out, w_o, l) for l in range(NL)]
    win[0], wsq[0], wcw = _gather_first(win[0], wsq[0], _place_cw(chip, conv_w))

    saved = []
    xl = x[0]
    for l in range(NL):
        ride = _Ride()
        if l + 1 < NL:
            _ride_gather_ici(ride, "win", win[l + 1])
            _ride_gather_ici(ride, "wsq", wsq[l + 1])
        (proj, h), got = _fwd_in(xl, pre_g, win[l], l, ride=ride)
        if l + 1 < NL:
            win[l + 1], wsq[l + 1] = got["win"], got["wsq"]
        a_ret, states = _ret_fwd(proj, tb, L)
        ride = _Ride()
        if l + 1 < NL:
            _ride_gather_pass(ride, "win", win[l + 1])
            _ride_gather_pass(ride, "wsq", wsq[l + 1])
        (a_conv, y), got = _conv_fwd(proj, wcw, cb, lg, lb, l, ride=ride)
        if l + 1 < NL:
            win[l + 1], wsq[l + 1] = got["win"], got["wsq"]
        xn, ro, co, ym, z = _merge_fwd(xl, proj, a_ret, a_conv, wsq[l], post_g, l)
        saved.append((xl, proj, h, a_ret, states, a_conv, y, ro, co, ym, z))
        xl = xn
    dx, lsum = _loss_fwd_bwd(xl, loss_target[0])
    loss = lax.psum(jnp.sum(lsum), ("x", "y", "c"))

    gin, gsq, uin, usq = [None] * NL, [None] * NL, [None] * NL, [None] * NL
    s_in, s_sq, o_in, o_sq = [None] * NL, [None] * NL, [None] * NL, [None] * NL
    small = [None] * NL
    for l in reversed(range(NL)):
        xin, proj, h, a_ret, states, a_conv, y, ro, co, ym, z = saved[l]
        ride = _Ride()
        if l + 1 < NL:
            _ride_exchange(ride, "gin", gin[l + 1], "uin", uin[l + 1], (2,))
        (dproj, da_ret, da_conv, gsq[l], dpost), got = _merge_bwd(dx, proj, a_ret, a_conv, ro, co, ym, z, wsq[l], post_g, l, ride=ride)
        ride = _Ride()
        _ride_exchange(ride, "gsq", gsq[l], "usq", None, (0, 1, 2))
        if l + 1 < NL:
            uin[l + 1] = got["uin"]
            s_in[l + 1] = _sum_group(chip, gin[l + 1], uin[l + 1])
            s_sq[l + 1] = _sum_group(chip, gsq[l + 1], usq[l + 1])
            _ride_swap(ride, "s_in", s_in[l + 1], "o_in")
            _ride_swap(ride, "s_sq", s_sq[l + 1], "o_sq")
        (dproj, sg), got = _conv_bwd(dproj, da_conv, y, proj, wcw, lg, lb, l, ride=ride)
        usq[l] = got["usq"]
        if l + 1 < NL:
            o_in[l + 1], o_sq[l + 1] = got["o_in"], got["o_sq"]
        dproj = _ret_bwd(dproj, da_ret, proj, states, tb, L)
        gin[l] = _win_grad(h, dproj, W)
        ride = _Ride()
        _ride_exchange(ride, "gin", gin[l], "uin", None, (0, 1))
        (dx, dpre), got = _in_bwd(dx, dproj, xin, pre_g, win[l], l, ride=ride)
        uin[l] = got["uin"]
        small[l] = jnp.concatenate([sg, dpre, dpost], axis=0)
    grad_x = dx

    big = {"w_in": None, "w_ret_out": None, "w_conv_out": None, "w_o": None}
    wts = {"w_in": (w_in, m_w_in, v_w_in), "w_ret_out": (w_ret_out, m_w_ret_out, v_w_ret_out),
           "w_conv_out": (w_conv_out, m_w_conv_out, v_w_conv_out), "w_o": (w_o, m_w_o, v_w_o)}
    sq_names = ("w_ret_out", "w_conv_out", "w_o")

    def adam_in(l, ride=None):
        big["w_in"], got = _adamw_layer(big["w_in"], *wts["w_in"], s_in[l], o_in[l], l, 0, ride=ride)
        return got

    def adam_sq(l, part, ride=None):
        n = sq_names[part]
        big[n], got = _adamw_layer(big[n], *wts[n], s_sq[l], o_sq[l], l, part, ride=ride)
        return got

    top = NL - 1
    ride = _Ride()
    _ride_exchange(ride, "gin", gin[0], "uin", uin[0], (2,))
    uin[0] = adam_in(top, ride)["uin"]
    s_in[0] = _sum_group(chip, gin[0], uin[0])
    s_sq[0] = _sum_group(chip, gsq[0], usq[0])
    ride = _Ride()
    _ride_swap(ride, "s_in", s_in[0], "o_in")
    _ride_swap(ride, "s_sq", s_sq[0], "o_sq")
    got = adam_sq(top, 0, ride)
    o_in[0], o_sq[0] = got["o_in"], got["o_sq"]
    adam_sq(top, 1)
    adam_sq(top, 2)
    for l in reversed(range(top)):
        adam_in(l)
        for part in range(3):
            adam_sq(l, part)

    gsm = _sum_devices(_gather_small(jnp.stack(small)))
    grads = {
        "pre_norm_g": gsm[:, ROW_PRE], "conv_w": lax.dynamic_slice_in_dim(gsm[:, 0:CONV_K], chip[0] * Cc, Cc, axis=2),
        "conv_b": gsm[:, ROW_CB], "conv_ln_g": gsm[:, ROW_LG], "conv_ln_b": gsm[:, ROW_LB], "post_norm_g": gsm[:, ROW_POST],
    }
    weights = dict(pre_norm_g=pre_norm_g, conv_w=conv_w, conv_b=conv_b, conv_ln_g=conv_ln_g, conv_ln_b=conv_ln_b,
                   post_norm_g=post_norm_g)
    m1 = dict(pre_norm_g=m_pre_norm_g, conv_w=m_conv_w, conv_b=m_conv_b, conv_ln_g=m_conv_ln_g, conv_ln_b=m_conv_ln_b,
              post_norm_g=m_post_norm_g)
    m2 = dict(pre_norm_g=v_pre_norm_g, conv_w=v_conv_w, conv_b=v_conv_b, conv_ln_g=v_conv_ln_g, conv_ln_b=v_conv_ln_b,
              post_norm_g=v_post_norm_g)
    res = {n: (grads[n],) + _adamw(weights[n], grads[n], m1[n], m2[n]) for n in grads}
    res.update(big)
    order = ["pre_norm_g", "w_in", "w_ret_out", "conv_w", "conv_b", "conv_ln_g", "conv_ln_b", "w_conv_out", "w_o", "post_norm_g"]
    return (loss, grad_x[None], *[res[n][0] for n in order], *[res[n][1] for n in order],
            *[res[n][2] for n in order], *[res[n][3] for n in order])
```

```python
import numpy as np
import jax
import jax.numpy as jnp
from jax import lax
from jax.experimental import pallas as pl
from jax.experimental.pallas import tpu as pltpu

F32 = jnp.float32
BF16 = jnp.bfloat16

HEADS = 8
DK = 64
DV = 128
CONV_K = 31
CHUNK = 64
ROPE_BASE = 10000.0
EPS = 1e-6
HALO = 32
CONV_RB = 32

ADAM_LR = 0.001
ADAM_B1 = 0.9
ADAM_B2 = 0.999
ADAM_EPS = 1e-08
ADAM_WD = 0.01
ADAM_STEP = 10
ADAM_BLOCK_BYTES = 2 * 1024 * 1024

VMEM_LIMIT = 56 * 1024 * 1024
MESH_T = pl.DeviceIdType.MESH
ANY = pl.BlockSpec(memory_space=pl.ANY)

ROW_CB, ROW_LG, ROW_LB = 32, 33, 34
ROW_PRE, ROW_POST = 40, 48


def _cp(sem=None, **kw):
    return pltpu.CompilerParams(dimension_semantics=sem, vmem_limit_bytes=VMEM_LIMIT, **kw)


def _dot(a, b):
    return jnp.dot(a, b, preferred_element_type=F32)


def _dot_nt(a, b):
    return lax.dot_general(a, b, (((1,), (1,)), ((), ())), preferred_element_type=F32)


def _dot_tn(a, b):
    return lax.dot_general(a, b, (((0,), (0,)), ((), ())), preferred_element_type=F32)


def _sigmoid(x):
    return jax.nn.sigmoid(x)


def _silu(x):
    return x * _sigmoid(x)


def _rms(x, g):
    return x * lax.rsqrt(jnp.mean(x * x, axis=-1, keepdims=True) + EPS) * g


def _gn_gate(o, g):
    mu = jnp.mean(o, axis=-1, keepdims=True)
    d = o - mu
    var = jnp.mean(d * d, axis=-1, keepdims=True)
    return d * lax.rsqrt(var + EPS) * _silu(g)


def _ln_gate(y, gc, lg, lb):
    mu = jnp.mean(y, axis=-1, keepdims=True)
    d = y - mu
    var = jnp.mean(d * d, axis=-1, keepdims=True)
    return _silu(d * lax.rsqrt(var + EPS) * lg + lb) * _silu(gc)


def _tables(T, L):
    lane = np.arange(128)
    d = lane % DK
    half = DK // 2
    inv = (ROPE_BASE ** (-(np.arange(half, dtype=np.float32)) / half)).astype(np.float32)
    ang = (np.arange(T, dtype=np.float32)[:, None] * inv[None, :]).astype(np.float64)
    angl = ang[:, d % half]
    cos = np.cos(angl)
    sin = np.sin(angl)
    lo = (d < half)[None, :]
    rope = np.stack([cos, np.where(lo, -sin, 0.0), np.where(lo, 0.0, sin)]).astype(np.float32)

    hh = np.arange(HEADS, dtype=np.float64)
    log_g = np.log1p(-np.exp2(-5.0 - hh))
    n = np.arange(L, dtype=np.float64)
    cn = np.arange(L) // CHUNK
    allowed = (cn[None, :] <= cn[:, None])
    dist = np.abs(n[:, None] - n[None, :])
    mask = np.exp(log_g[:, None, None] * dist[None]) * allowed[None]
    mq = ((lane[None, :] // DK) == (np.arange(HEADS)[:, None] % 2)).astype(np.float64)
    qd = np.exp(log_g[:, None] * n[None, :])
    kd = np.exp(log_g[:, None] * (L - n[None, :]))
    qdm = qd[:, :, None] * mq[:, None, :] * (DK ** -0.5)
    kdm = kd[:, :, None] * mq[:, None, :]
    mqs = np.broadcast_to((mq * (DK ** -0.5))[:, None, :], (HEADS, 8, 128))
    cd = np.broadcast_to(np.exp(log_g * L)[:, None, None], (HEADS, 8, 128))
    f = lambda a: jnp.asarray(np.ascontiguousarray(a), dtype=F32)
    return dict(rope=f(rope), mask=f(mask), qdm=f(qdm), kdm=f(kdm), mqs=f(mqs), cd=f(cd))


def _rot(b, c, sl, sh):
    return b * c + pltpu.roll(b, 96, axis=1) * sl + pltpu.roll(b, 32, axis=1) * sh


def _rot_t(d, c, sl, sh):
    return d * c + pltpu.roll(d * sl, 32, axis=1) + pltpu.roll(d * sh, 96, axis=1)


def _place():
    return lax.axis_index("x"), lax.axis_index("y"), lax.axis_index("c")


def _other_chips(x, y):
    return [(1 - x, y), (x, 1 - y), (1 - x, 1 - y)]


def _remote(src, dst, send_sems, recv_sems, k, to):
    return pltpu.make_async_remote_copy(src_ref=src, dst_ref=dst, send_sem=send_sems.at[k], recv_sem=recv_sems.at[k],
                                        device_id=to, device_id_type=MESH_T)


class _Ride:
    def __init__(self):
        self.arrays, self.kinds, self.names = [], [], []
        self.fresh = []
        self.ops = []

    def read(self, name, a):
        self.names.append(name)
        self.arrays.append(a)
        self.kinds.append("in")

    def inout(self, name, a):
        self.names.append(name)
        self.arrays.append(a)
        self.kinds.append("inout")

    def land(self, name, shape, dtype):
        self.fresh.append((name, jax.ShapeDtypeStruct(shape, dtype)))

    def op(self, n_sems, start, finish):
        self.ops.append((n_sems, start, finish))


def _pcall(body, *, name, grid, in_specs, out_specs, out_shape, args, scratch_shapes=(), sem, aliases=None, ride=None):
    if ride is None or not ride.ops:
        outs = pl.pallas_call(body, name=name, grid=grid, in_specs=list(in_specs), out_specs=tuple(out_specs),
                              out_shape=tuple(out_shape), scratch_shapes=list(scratch_shapes),
                              input_output_aliases=dict(aliases or {}), compiler_params=_cp(sem))(*args)
        return outs, {}

    ni, no, nr = len(args), len(out_shape), len(ride.arrays)
    inout = [i for i, k in enumerate(ride.kinds) if k == "inout"]
    r_out_shapes = [jax.ShapeDtypeStruct(ride.arrays[i].shape, ride.arrays[i].dtype) for i in inout] + [s for _, s in ride.fresh]
    r_out_names = [ride.names[i] for i in inout] + [n for n, _ in ride.fresh]
    nro = len(r_out_shapes)
    n_sems = sum(n for n, _, _ in ride.ops)
    n_scr = len(scratch_shapes)
    nd = len(grid)

    def wrapped(*refs):
        ins, rin = refs[:ni], refs[ni:ni + nr]
        outs, rout = refs[ni + nr:ni + nr + no], refs[ni + nr + no:ni + nr + no + nro]
        scr = refs[ni + nr + no + nro:ni + nr + no + nro + n_scr]
        send_sems, recv_sems = refs[-2], refs[-1]
        view = {nm: r for nm, r, k in zip(ride.names, rin, ride.kinds) if k == "in"}
        view.update(dict(zip(r_out_names, rout)))
        first = pl.program_id(0) == 0
        last = pl.program_id(0) == grid[0] - 1
        for d in range(1, nd):
            first = first & (pl.program_id(d) == 0)
            last = last & (pl.program_id(d) == grid[d] - 1)

        @pl.when(first)
        def _():
            base = 0
            for n, start, _ in ride.ops:
                start(view, send_sems, recv_sems, base)
                base += n

        body(*ins, *outs, *scr)

        @pl.when(last)
        def _():
            base = 0
            for n, _, finish in ride.ops:
                finish(view, send_sems, recv_sems, base)
                base += n

    res = pl.pallas_call(
        wrapped, name=name, grid=grid,
        in_specs=list(in_specs) + [ANY] * nr, out_specs=tuple(out_specs) + (ANY,) * nro,
        out_shape=tuple(out_shape) + tuple(r_out_shapes),
        scratch_shapes=list(scratch_shapes) + [pltpu.SemaphoreType.DMA((n_sems,)), pltpu.SemaphoreType.DMA((n_sems,))],
        input_output_aliases={**dict(aliases or {}), **{ni + i: no + j for j, i in enumerate(inout)}},
        compiler_params=_cp(sem),
    )(*args, *ride.arrays)
    return res[:no], dict(zip(r_out_names, res[no:]))


def _half(ref, chip_idx, cc):
    n = ref.shape[1] // 2
    return ref.at[chip_idx, pl.ds(cc * n, n)]


def _ride_gather_ici(ride, name, a):
    ride.inout(name, a)

    def start(view, ss, rs, b):
        x, y, c = _place()
        mine = _half(view[name], 2 * x + y, c)
        for j, (cx, cy) in enumerate(_other_chips(x, y)):
            _remote(mine, mine, ss, rs, b + j, (cx, cy, c)).start()

    def finish(view, ss, rs, b):
        x, y, c = _place()
        mine = _half(view[name], 2 * x + y, c)
        for j, (cx, cy) in enumerate(_other_chips(x, y)):
            theirs = _half(view[name], 2 * cx + cy, c)
            _remote(theirs, theirs, ss, rs, b + j, (cx, cy, c)).wait_recv()
        for j, (cx, cy) in enumerate(_other_chips(x, y)):
            _remote(mine, mine, ss, rs, b + j, (cx, cy, c)).wait_send()

    ride.op(3, start, finish)


def _ride_gather_pass(ride, name, a):
    ride.inout(name, a)

    def start(view, ss, rs, b):
        x, y, c = _place()
        for j, (cx, cy) in enumerate(_other_chips(x, y)):
            blk = _half(view[name], 2 * cx + cy, c)
            _remote(blk, blk, ss, rs, b + j, (x, y, 1 - c)).start()

    def finish(view, ss, rs, b):
        x, y, c = _place()
        for j, (cx, cy) in enumerate(_other_chips(x, y)):
            theirs = _half(view[name], 2 * cx + cy, 1 - c)
            _remote(theirs, theirs, ss, rs, b + j, (x, y, 1 - c)).wait_recv()
        for j, (cx, cy) in enumerate(_other_chips(x, y)):
            blk = _half(view[name], 2 * cx + cy, c)
            _remote(blk, blk, ss, rs, b + j, (x, y, 1 - c)).wait_send()

    ride.op(3, start, finish)


def _ride_exchange(ride, src_name, src, dst_name, dst, rels):
    ride.read(src_name, src)
    if dst is None:
        ride.land(dst_name, (3,) + src.shape[1:], src.dtype)
    else:
        ride.inout(dst_name, dst)

    def start(view, ss, rs, b):
        x, y, c = _place()
        chips = _other_chips(x, y)
        for i, j in enumerate(rels):
            cx, cy = chips[j]
            _remote(view[src_name].at[2 * cx + cy], view[dst_name].at[j], ss, rs, b + i, (cx, cy, c)).start()

    def finish(view, ss, rs, b):
        x, y, c = _place()
        chips = _other_chips(x, y)
        for i, j in enumerate(rels):
            cx, cy = chips[j]
            _remote(view[src_name].at[2 * cx + cy], view[dst_name].at[j], ss, rs, b + i, (cx, cy, c)).wait()

    ride.op(len(rels), start, finish)


def _ride_swap(ride, src_name, src, dst_name):
    ride.read(src_name, src)
    ride.land(dst_name, src.shape, src.dtype)

    def start(view, ss, rs, b):
        x, y, c = _place()
        _remote(view[src_name], view[dst_name], ss, rs, b, (x, y, 1 - c)).start()

    def finish(view, ss, rs, b):
        x, y, c = _place()
        _remote(view[src_name], view[dst_name], ss, rs, b, (x, y, 1 - c)).wait()

    ride.op(1, start, finish)


def _peers(x, y, c):
    flip = lambda v, b: 1 - v if b else v
    return [(flip(x, r & 4), flip(y, r & 2), flip(c, r & 1)) for r in range(1, 8)]


def _ride_gather_all(ride, src_name, src, dst_name):
    ride.read(src_name, src)
    ride.land(dst_name, (8,) + src.shape, src.dtype)

    def start(view, ss, rs, b):
        x, y, c = _place()
        me = 4 * x + 2 * y + c
        pltpu.make_async_copy(view[src_name], view[dst_name].at[me], ss.at[b + 7]).start()
        for r, peer in enumerate(_peers(x, y, c)):
            _remote(view[src_name], view[dst_name].at[me], ss, rs, b + r, peer).start()

    def finish(view, ss, rs, b):
        x, y, c = _place()
        me = 4 * x + 2 * y + c
        for r, peer in enumerate(_peers(x, y, c)):
            theirs = view[dst_name].at[4 * peer[0] + 2 * peer[1] + peer[2]]
            _remote(theirs, theirs, ss, rs, b + r, peer).wait_recv()
        for r, peer in enumerate(_peers(x, y, c)):
            _remote(view[src_name], view[dst_name].at[me], ss, rs, b + r, peer).wait_send()
        pltpu.make_async_copy(view[src_name], view[dst_name].at[me], ss.at[b + 7]).wait()

    ride.op(8, start, finish)


def _cast_win(chip, w_in, l):
    _, D, W = w_in.shape
    tr = min(256, D)

    def body(chip_ref, w_ref, o_ref):
        o_ref[...] = w_ref[...].astype(BF16)

    return pl.pallas_call(
        body, name="cast_win",
        out_shape=jax.ShapeDtypeStruct((4, D, W), BF16),
        grid_spec=pltpu.PrefetchScalarGridSpec(
            num_scalar_prefetch=1, grid=(D // tr,),
            in_specs=[pl.BlockSpec((None, tr, W), lambda r, c: (l, r, 0))],
            out_specs=pl.BlockSpec((None, tr, W), lambda r, c: (c[0], r, 0))),
        compiler_params=_cp(("arbitrary",)),
    )(chip, w_in)


def _cast_wsq(chip, w_ro, w_co, w_o, l):
    _, R, D = w_ro.shape

    def body(chip_ref, a_ref, b_ref, c_ref, o_ref):
        o_ref[0:R, :] = a_ref[...].astype(BF16)
        o_ref[R:2 * R, :] = b_ref[...].astype(BF16)
        o_ref[2 * R:3 * R, :] = c_ref[...].astype(BF16)

    spec = pl.BlockSpec((None, R, D), lambda i, c: (l, 0, 0))
    return pl.pallas_call(
        body, name="cast_wsq",
        out_shape=jax.ShapeDtypeStruct((4, 3 * R, D), BF16),
        grid_spec=pltpu.PrefetchScalarGridSpec(
            num_scalar_prefetch=1, grid=(1,),
            in_specs=[spec, spec, spec],
            out_specs=pl.BlockSpec((None, 3 * R, D), lambda i, c: (c[0], 0, 0))),
        compiler_params=_cp(("arbitrary",)),
    )(chip, w_ro, w_co, w_o)


def _place_cw(chip, conv_w):
    NL, K, Cc = conv_w.shape

    def body(chip_ref, w_ref, o_ref):
        o_ref[...] = w_ref[...]

    return pl.pallas_call(
        body, name="place_cw",
        out_shape=jax.ShapeDtypeStruct((4, NL, K, Cc), F32),
        grid_spec=pltpu.PrefetchScalarGridSpec(
            num_scalar_prefetch=1, grid=(1,),
            in_specs=[pl.BlockSpec((NL, K, Cc), lambda i, c: (0, 0, 0))],
            out_specs=pl.BlockSpec((None, NL, K, Cc), lambda i, c: (c[0], 0, 0, 0))),
        compiler_params=_cp(("arbitrary",)),
    )(chip, conv_w)


def _gather_first(win0, wcw):
    n_arr = 2

    def body(a0, a1, o0, o1, send_sems, recv_sems):
        x, y, c = _place()
        sibling = (x, y, 1 - c)
        chips = _other_chips(x, y)
        outs = (o0, o1)

        def copy(k, a, cx, cy, cc, to):
            blk = _half(outs[a], 2 * cx + cy, cc)
            return _remote(blk, blk, send_sems, recv_sems, k, to)

        first = [copy(3 * a + j, a, x, y, c, (*chip, c)) for a in range(n_arr) for j, chip in enumerate(chips)]
        for cp in first:
            cp.start()
        passed = [copy(3 * n_arr + 3 * a + j, a, *chip, c, sibling) for a in range(n_arr) for j, chip in enumerate(chips)]
        for a in range(n_arr):
            for j, chip in enumerate(chips):
                copy(3 * a + j, a, *chip, c, sibling).wait_recv()
                passed[3 * a + j].start()
        for a in range(n_arr):
            for j, chip in enumerate(chips):
                copy(3 * n_arr + 3 * a + j, a, *chip, 1 - c, sibling).wait_recv()
        for cp in first + passed:
            cp.wait_send()

    ins = (win0, wcw)
    return pl.pallas_call(
        body, name="gather_first",
        out_shape=tuple(jax.ShapeDtypeStruct(a.shape, a.dtype) for a in ins),
        in_specs=[ANY] * n_arr, out_specs=(ANY,) * n_arr,
        scratch_shapes=[pltpu.SemaphoreType.DMA((6 * n_arr,)), pltpu.SemaphoreType.DMA((6 * n_arr,))],
        input_output_aliases={0: 0, 1: 1},
        compiler_params=pltpu.CompilerParams(has_side_effects=True),
    )(*ins)


def _fwd_in(x, pre_g, win, l, ride=None):
    T, D = x.shape
    W = win.shape[-1]
    tm = min(512, T)

    def body(x_ref, g_ref, w_ref, p_ref, h_ref, hs):
        @pl.when(pl.program_id(1) == 0)
        def _():
            hb = _rms(x_ref[...], g_ref[...]).astype(BF16)
            hs[...] = hb
            h_ref[...] = hb

        p_ref[...] = _dot(hs[...], w_ref[...]).astype(BF16)

    return _pcall(
        body, name="fwd_in",
        out_shape=(jax.ShapeDtypeStruct((T, 4 * W), BF16), jax.ShapeDtypeStruct((T, D), BF16)),
        grid=(T // tm, 4),
        in_specs=[pl.BlockSpec((tm, D), lambda i, j: (i, 0)),
                  pl.BlockSpec((None, 1, D), lambda i, j: (l, 0, 0)),
                  pl.BlockSpec((None, D, W), lambda i, j: (j, 0, 0))],
        out_specs=(pl.BlockSpec((tm, W), lambda i, j: (i, j)),
                   pl.BlockSpec((tm, D), lambda i, j: (i, 0))),
        scratch_shapes=[pltpu.VMEM((tm, D), BF16)],
        sem=("arbitrary", "arbitrary"), args=(x, pre_g, win), ride=ride)


def _ret_specs(T, L):
    rope = pl.BlockSpec((3, L, 128), lambda s: (0, s, 0))
    mask = pl.BlockSpec((HEADS, L, L), lambda s: (0, 0, 0))
    qdm = pl.BlockSpec((HEADS, L, 128), lambda s: (0, 0, 0))
    small = pl.BlockSpec((HEADS, 8, 128), lambda s: (0, 0, 0))
    return [rope, mask, qdm, qdm, small, small]


def _ret_fwd(proj, tb, L, ride=None):
    T = proj.shape[0]
    QK = HEADS * DK
    VW = HEADS * DV
    nS = T // L

    def body(p_ref, rope_ref, m_ref, qdm_ref, kdm_ref, mqs_ref, cd_ref, a_ref, st_ref, state):
        @pl.when(pl.program_id(0) == 0)
        def _():
            state[...] = jnp.zeros_like(state)

        c, sl, sh = rope_ref[0], rope_ref[1], rope_ref[2]
        for j in range(HEADS // 2):
            rq = _rot(p_ref[:, 128 * j:128 * (j + 1)].astype(F32), c, sl, sh)
            rk = _rot(p_ref[:, QK + 128 * j:QK + 128 * (j + 1)].astype(F32), c, sl, sh)
            rkb = rk.astype(BF16)
            for e in range(2):
                h = 2 * j + e
                v = p_ref[:, 2 * QK + DV * h:2 * QK + DV * (h + 1)]
                g = p_ref[:, 2 * QK + VW + DV * h:2 * QK + VW + DV * (h + 1)].astype(F32)
                a = (rq * mqs_ref[h, 0:1, :]).astype(BF16)
                p = (_dot_nt(a, rkb) * m_ref[h]).astype(BF16)
                st = state[h]
                st_ref[h] = st
                o = _dot(p, v) + _dot((rq * qdm_ref[h]).astype(BF16), st.astype(BF16))
                state[h] = st * cd_ref[h, 0:1, :] + _dot_tn((rk * kdm_ref[h]).astype(BF16), v)
                a_ref[:, DV * h:DV * (h + 1)] = _gn_gate(o, g).astype(BF16)

    return _pcall(
        body, name="ret_fwd",
        out_shape=(jax.ShapeDtypeStruct((T, VW), BF16), jax.ShapeDtypeStruct((nS, HEADS, 128, DV), F32)),
        grid=(nS,),
        in_specs=[pl.BlockSpec((L, 2 * QK + 2 * VW), lambda s: (s, 0))] + _ret_specs(T, L),
        out_specs=(pl.BlockSpec((L, VW), lambda s: (s, 0)),
                   pl.BlockSpec((None, HEADS, 128, DV), lambda s: (s, 0, 0, 0))),
        scratch_shapes=[pltpu.VMEM((HEADS, 128, DV), F32)],
        sem=("arbitrary",), ride=ride,
        args=(proj, tb["rope"], tb["mask"], tb["qdm"], tb["kdm"], tb["mqs"], tb["cd"]))


def _shift_copies(src, sh):
    rows = sh.shape[1]
    for b in range(1, 8):
        sh[b - 1, :, :] = src[pl.ds(b, rows), :]


def _window(src, sh, start, rows):
    b = start % 8
    if b == 0:
        return src[pl.ds(start, rows), :]
    return sh[b - 1, pl.ds(start - b, rows), :]


def _conv_taps(wbuf, src, sh, base, rows):
    acc = wbuf[pl.ds(0, 1), :] * _window(src, sh, base, rows)
    for k in range(1, CONV_K):
        acc = acc + wbuf[pl.ds(k, 1), :] * _window(src, sh, base + k, rows)
    return acc


def _load_conv_w(cw_ref, wbuf, flip):
    for k in range(CONV_K):
        row = jnp.concatenate([cw_ref[c, pl.ds(k, 1), :] for c in range(4)], axis=-1)
        wbuf[pl.ds(CONV_K - 1 - k if flip else k, 1), :] = row
    wbuf[pl.ds(CONV_K, 1), :] = jnp.zeros_like(wbuf[pl.ds(CONV_K, 1), :])


def _conv_fwd(proj, wcw, conv_b, ln_g, ln_b, l, ride=None):
    T = proj.shape[0]
    C = conv_b.shape[-1]
    Cc = wcw.shape[-1]
    tc = min(256, T)
    off = HALO - (CONV_K - 1)

    def body(p_ref, cw_ref, cb_ref, lg_ref, lb_ref, a_ref, y_ref, ubuf, wbuf, ush):
        i = pl.program_id(0)

        @pl.when(i == 0)
        def _():
            ubuf[0:HALO, :] = jnp.zeros((HALO, C), F32)
            _load_conv_w(cw_ref, wbuf, False)

        @pl.when(i > 0)
        def _():
            ubuf[0:HALO, :] = ubuf[tc:tc + HALO, :]

        ga = p_ref[:, 0:C].astype(F32)
        gb = p_ref[:, C:2 * C].astype(F32)
        ubuf[HALO:HALO + tc, :] = ga * _sigmoid(gb)
        _shift_copies(ubuf, ush)
        for r in range(tc // CONV_RB):
            y_ref[r * CONV_RB:(r + 1) * CONV_RB, :] = _conv_taps(wbuf, ubuf, ush, r * CONV_RB + off, CONV_RB) + cb_ref[...]
        gc = p_ref[:, 2 * C:3 * C].astype(F32)
        a_ref[...] = _ln_gate(y_ref[...], gc, lg_ref[...], lb_ref[...]).astype(BF16)

    vec = pl.BlockSpec((None, 1, C), lambda i: (l, 0, 0))
    return _pcall(
        body, name="conv_fwd",
        out_shape=(jax.ShapeDtypeStruct((T, C), BF16), jax.ShapeDtypeStruct((T, C), F32)),
        grid=(T // tc,),
        in_specs=[pl.BlockSpec((tc, 3 * C), lambda i: (i, 1)),
                  pl.BlockSpec((4, None, CONV_K, Cc), lambda i: (0, l, 0, 0)),
                  vec, vec, vec],
        out_specs=(pl.BlockSpec((tc, C), lambda i: (i, 0)), pl.BlockSpec((tc, C), lambda i: (i, 0))),
        scratch_shapes=[pltpu.VMEM((HALO + tc, C), F32), pltpu.VMEM((HALO, C), F32), pltpu.VMEM((7, HALO + tc - 8, C), F32)],
        sem=("arbitrary",), args=(proj, wcw, conv_b, ln_g, ln_b), ride=ride)


def _merge_fwd(x, proj, a_ret, a_conv, wsq, post_g, l):
    T, D = x.shape
    R = wsq.shape[1] // 3
    tm = min(512, T)

    def body(x_ref, p_ref, ar_ref, ac_ref, wro_ref, wco_ref, wo_ref, g_ref, xn_ref, ro_ref, co_ref, ym_ref, z_ref):
        ro = _dot(ar_ref[...], wro_ref[...].reshape(4 * R, D))
        co = _dot(ac_ref[...], wco_ref[...].reshape(4 * R, D))
        ym = (_sigmoid(p_ref[:, 0:D].astype(F32)) * ro + _sigmoid(p_ref[:, D:2 * D].astype(F32)) * co).astype(BF16)
        z = _dot(ym, wo_ref[...].reshape(4 * R, D))
        ro_ref[...] = ro.astype(BF16)
        co_ref[...] = co.astype(BF16)
        ym_ref[...] = ym
        z_ref[...] = z.astype(BF16)
        xn_ref[...] = x_ref[...] + _rms(z, g_ref[...])

    tile = pl.BlockSpec((tm, D), lambda i: (i, 0))
    wspec = lambda m: pl.BlockSpec((4, R, D), lambda i: (0, m, 0))
    act = jax.ShapeDtypeStruct((T, D), BF16)
    return pl.pallas_call(
        body, name="merge_fwd",
        out_shape=(jax.ShapeDtypeStruct((T, D), F32), act, act, act, act),
        grid=(T // tm,),
        in_specs=[tile, pl.BlockSpec((tm, 2 * D), lambda i: (i, 3)), tile, tile,
                  wspec(0), wspec(1), wspec(2), pl.BlockSpec((None, 1, D), lambda i: (l, 0, 0))],
        out_specs=(tile, tile, tile, tile, tile),
        compiler_params=_cp(("arbitrary",)),
    )(x, proj, a_ret, a_conv, wsq, wsq, wsq, post_g)


def _loss_fwd_bwd(y, target):
    T, D = y.shape
    tm = min(512, T)

    def body(y_ref, t_ref, dy_ref, ls_ref):
        @pl.when(pl.program_id(0) == 0)
        def _():
            ls_ref[...] = jnp.zeros_like(ls_ref)

        e = y_ref[...] - t_ref[...]
        dy_ref[...] = e * (1.0 / D)
        ls_ref[...] += jnp.sum((e * e).reshape(tm // 8, 8, D), axis=0) * (0.5 / D)

    tile = pl.BlockSpec((tm, D), lambda i: (i, 0))
    return pl.pallas_call(
        body, name="loss",
        out_shape=(jax.ShapeDtypeStruct((T, D), F32), jax.ShapeDtypeStruct((8, D), F32)),
        grid=(T // tm,),
        in_specs=[tile, tile],
        out_specs=(tile, pl.BlockSpec((8, D), lambda i: (0, 0))),
        compiler_params=_cp(("arbitrary",)),
    )(y, target)


def _merge_bwd(dxn, proj, a_ret, a_conv, ro, co, ym, z, wsq, post_g, l, ride=None):
    T, D = dxn.shape
    R = wsq.shape[1] // 3
    tm = min(256, T)
    n = T // tm

    def body(dx_ref, p_ref, ar_ref, ac_ref, ro_ref, co_ref, ym_ref, z_ref, wro_ref, wco_ref, wo_ref, g_ref,
             dp_ref, dar_ref, dac_ref, gsq_ref, dg_ref, acc, stage):
        i = pl.program_id(0)

        @pl.when(i == 0)
        def _():
            acc[...] = jnp.zeros_like(acc)
            dg_ref[...] = jnp.zeros_like(dg_ref)

        _, vjp = jax.vjp(_rms, z_ref[...].astype(F32), g_ref[...])
        dz, dg = vjp(dx_ref[...])
        dg_ref[0:1, :] += dg
        dzb = dz.astype(BF16)
        dym = _dot_nt(dzb, wo_ref[...].reshape(4 * R, D))
        acc[2] += _dot_tn(ym_ref[...], dzb)
        sr = _sigmoid(p_ref[:, 0:D].astype(F32))
        sc = _sigmoid(p_ref[:, D:2 * D].astype(F32))
        rov = ro_ref[...].astype(F32)
        cov = co_ref[...].astype(F32)
        dp_ref[:, 0:D] = (dym * rov * sr * (1.0 - sr)).astype(BF16)
        dp_ref[:, D:2 * D] = (dym * cov * sc * (1.0 - sc)).astype(BF16)
        dro = (dym * sr).astype(BF16)
        dco = (dym * sc).astype(BF16)
        dar_ref[...] = _dot_nt(dro, wro_ref[...].reshape(4 * R, D)).astype(BF16)
        dac_ref[...] = _dot_nt(dco, wco_ref[...].reshape(4 * R, D)).astype(BF16)
        acc[0] += _dot_tn(ar_ref[...], dro)
        acc[1] += _dot_tn(ac_ref[...], dco)

        @pl.when(i == n - 1)
        def _():
            for m in range(3):
                stage[...] = acc[m].astype(BF16).reshape(4, R, D)
                pltpu.sync_copy(stage, gsq_ref.at[:, pl.ds(m * R, R), :])

    tile = pl.BlockSpec((tm, D), lambda i: (i, 0))
    wspec = lambda m: pl.BlockSpec((4, R, D), lambda i: (0, m, 0), pipeline_mode=pl.Buffered(1))
    return _pcall(
        body, name="merge_bwd",
        out_shape=(jax.ShapeDtypeStruct(proj.shape, BF16), jax.ShapeDtypeStruct((T, D), BF16),
                   jax.ShapeDtypeStruct((T, D), BF16), jax.ShapeDtypeStruct(wsq.shape, BF16),
                   jax.ShapeDtypeStruct((8, D), F32)),
        grid=(n,),
        in_specs=[tile, pl.BlockSpec((tm, 2 * D), lambda i: (i, 3)), tile, tile, tile, tile, tile, tile,
                  wspec(0), wspec(1), wspec(2), pl.BlockSpec((None, 1, D), lambda i: (l, 0, 0))],
        out_specs=(pl.BlockSpec((tm, 2 * D), lambda i: (i, 3)), tile, tile, ANY, pl.BlockSpec((8, D), lambda i: (0, 0))),
        scratch_shapes=[pltpu.VMEM((3, 4 * R, D), F32), pltpu.VMEM((4, R, D), BF16)],
        sem=("arbitrary",), args=(dxn, proj, a_ret, a_conv, ro, co, ym, z, wsq, wsq, wsq, post_g), ride=ride)


def _conv_bwd(dproj, da_conv, y, proj, wcw, ln_g, ln_b, l, ride=None):
    T, C = y.shape
    Cc = wcw.shape[-1]
    tc = min(256, T)
    n = T // tc
    hb = tc // HALO
    off = HALO - (CONV_K - 1)
    nrb = tc // CONV_RB

    def body(dpin_ref, da_ref, y_ref, p_ref, ph_ref, cw_ref, lg_ref, lb_ref, dp_ref, sg_ref,
             dcbuf, ubuf, dubuf, wbuf, dwacc, vacc, dsh, ush):
        t = pl.program_id(0)
        i = n - 1 - t

        @pl.when(t == 0)
        def _():
            dcbuf[tc:tc + HALO, :] = jnp.zeros((HALO, C), F32)
            dwacc[...] = jnp.zeros_like(dwacc)
            vacc[...] = jnp.zeros_like(vacc)
            _load_conv_w(cw_ref, wbuf, True)

        @pl.when(t > 0)
        def _():
            dcbuf[tc:tc + HALO, :] = dcbuf[0:HALO, :]

        gc = p_ref[:, 2 * C:3 * C].astype(F32)
        _, vjp = jax.vjp(_ln_gate, y_ref[...], gc, lg_ref[...], lb_ref[...])
        dy, dgc, dlg, dlb = vjp(da_ref[...].astype(F32))
        dcbuf[0:tc, :] = dy
        dp_ref[:, 2 * C:3 * C] = dgc.astype(BF16)
        vacc[0:1, :] += jnp.sum(dy, axis=0, keepdims=True)
        vacc[1:2, :] += dlg
        vacc[2:3, :] += dlb

        ga = p_ref[:, 0:C].astype(F32)
        sb = _sigmoid(p_ref[:, C:2 * C].astype(F32))
        ubuf[HALO:HALO + tc, :] = ga * sb
        uh = ph_ref[:, 0:C].astype(F32) * _sigmoid(ph_ref[:, C:2 * C].astype(F32))
        ubuf[0:HALO, :] = jnp.where(i > 0, uh, 0.0)

        _shift_copies(dcbuf, dsh)
        _shift_copies(ubuf, ush)
        for r in range(nrb):
            dubuf[r * CONV_RB:(r + 1) * CONV_RB, :] = _conv_taps(wbuf, dcbuf, dsh, r * CONV_RB, CONV_RB)
        du = dubuf[...]
        dp_ref[:, 0:C] = (du * sb).astype(BF16)
        dp_ref[:, C:2 * C] = (du * ga * sb * (1.0 - sb)).astype(BF16)

        for r in range(nrb):
            dyb = dcbuf[r * CONV_RB:(r + 1) * CONV_RB, :]
            for k in range(CONV_K):
                pr = dyb * _window(ubuf, ush, r * CONV_RB + off + k, CONV_RB)
                dwacc[8 * k:8 * k + 8, :] += jnp.sum(pr.reshape(CONV_RB // 8, 8, C), axis=0)

        @pl.when(t == n - 1)
        def _():
            for k in range(CONV_K):
                sg_ref[pl.ds(k, 1), :] = jnp.sum(dwacc[8 * k:8 * k + 8, :], axis=0, keepdims=True)
            sg_ref[pl.ds(CONV_K, 1), :] = jnp.zeros((1, C), F32)
            sg_ref[ROW_CB:ROW_CB + 8, :] = jnp.zeros((8, C), F32)
            sg_ref[ROW_CB:ROW_CB + 3, :] = vacc[0:3, :]

    vec = pl.BlockSpec((None, 1, C), lambda t: (l, 0, 0))
    tile = pl.BlockSpec((tc, C), lambda t: (n - 1 - t, 0))
    ptile = pl.BlockSpec((tc, 3 * C), lambda t: (n - 1 - t, 1))
    halo = pl.BlockSpec((HALO, 3 * C), lambda t: (jnp.maximum((n - 1 - t) * hb - 1, 0), 1))
    return _pcall(
        body, name="conv_bwd",
        out_shape=(jax.ShapeDtypeStruct(dproj.shape, BF16), jax.ShapeDtypeStruct((ROW_PRE, C), F32)),
        grid=(n,),
        in_specs=[ANY, tile, tile, ptile, halo,
                  pl.BlockSpec((4, None, CONV_K, Cc), lambda t: (0, l, 0, 0)), vec, vec],
        out_specs=(ptile, pl.BlockSpec((ROW_PRE, C), lambda t: (0, 0))),
        scratch_shapes=[pltpu.VMEM((tc + HALO, C), F32), pltpu.VMEM((HALO + tc, C), F32), pltpu.VMEM((tc, C), F32),
                        pltpu.VMEM((HALO, C), F32), pltpu.VMEM((8 * CONV_K, C), F32), pltpu.VMEM((8, C), F32),
                        pltpu.VMEM((7, HALO + tc - 8, C), F32), pltpu.VMEM((7, HALO + tc - 8, C), F32)],
        sem=("arbitrary",), aliases={0: 0}, args=(dproj, da_conv, y, proj, proj, wcw, ln_g, ln_b), ride=ride)


def _ret_bwd(dproj, da_ret, proj, states, tb, L, ride=None):
    T = proj.shape[0]
    QK = HEADS * DK
    VW = HEADS * DV
    nS = T // L
    PW = 2 * QK + 2 * VW

    def body(dpin_ref, da_ref, p_ref, st_ref, rope_ref, m_ref, qdm_ref, kdm_ref, mqs_ref, cd_ref, dp_ref, gst):
        @pl.when(pl.program_id(0) == 0)
        def _():
            gst[...] = jnp.zeros_like(gst)

        c, sl, sh = rope_ref[0], rope_ref[1], rope_ref[2]
        for j in range(HEADS // 2):
            rq = _rot(p_ref[:, 128 * j:128 * (j + 1)].astype(F32), c, sl, sh)
            rk = _rot(p_ref[:, QK + 128 * j:QK + 128 * (j + 1)].astype(F32), c, sl, sh)
            rkb = rk.astype(BF16)
            drq = jnp.zeros_like(rq)
            drk = jnp.zeros_like(rk)
            for e in range(2):
                h = 2 * j + e
                v = p_ref[:, 2 * QK + DV * h:2 * QK + DV * (h + 1)]
                g = p_ref[:, 2 * QK + VW + DV * h:2 * QK + VW + DV * (h + 1)].astype(F32)
                mqs = mqs_ref[h, 0:1, :]
                a = (rq * mqs).astype(BF16)
                aq = (rq * qdm_ref[h]).astype(BF16)
                kdv = (rk * kdm_ref[h]).astype(BF16)
                mk = m_ref[h]
                p = (_dot_nt(a, rkb) * mk).astype(BF16)
                stb = st_ref[h].astype(BF16)
                o = _dot(p, v) + _dot(aq, stb)
                _, vjp = jax.vjp(_gn_gate, o, g)
                do, dg = vjp(da_ref[:, DV * h:DV * (h + 1)].astype(F32))
                dob = do.astype(BF16)
                gs = gst[h]
                gsb = gs.astype(BF16)
                ds = (_dot_nt(dob, v) * mk).astype(BF16)
                drq = drq + _dot(ds, rkb) * mqs + _dot_nt(dob, stb) * qdm_ref[h]
                drk = drk + _dot_tn(ds, a) + _dot_nt(v, gsb) * kdm_ref[h]
                dv = _dot_tn(p, dob) + _dot(kdv, gsb)
                gst[h] = _dot_tn(aq, dob) + gs * cd_ref[h, 0:1, :]
                dp_ref[:, 2 * QK + DV * h:2 * QK + DV * (h + 1)] = dv.astype(BF16)
                dp_ref[:, 2 * QK + VW + DV * h:2 * QK + VW + DV * (h + 1)] = dg.astype(BF16)
            dp_ref[:, 128 * j:128 * (j + 1)] = _rot_t(drq, c, sl, sh).astype(BF16)
            dp_ref[:, QK + 128 * j:QK + 128 * (j + 1)] = _rot_t(drk, c, sl, sh).astype(BF16)

    rev = lambda s: nS - 1 - s
    specs = _ret_specs(T, L)
    specs[0] = pl.BlockSpec((3, L, 128), lambda s: (0, rev(s), 0))
    return _pcall(
        body, name="ret_bwd",
        out_shape=(jax.ShapeDtypeStruct(dproj.shape, BF16),),
        grid=(nS,),
        in_specs=[ANY, pl.BlockSpec((L, VW), lambda s: (rev(s), 0)), pl.BlockSpec((L, PW), lambda s: (rev(s), 0)),
                  pl.BlockSpec((None, HEADS, 128, DV), lambda s: (rev(s), 0, 0, 0))] + specs,
        out_specs=(pl.BlockSpec((L, PW), lambda s: (rev(s), 0)),),
        scratch_shapes=[pltpu.VMEM((HEADS, 128, DV), F32)],
        sem=("arbitrary",), aliases={0: 0}, ride=ride,
        args=(dproj, da_ret, proj, states, tb["rope"], tb["mask"], tb["qdm"], tb["kdm"], tb["mqs"], tb["cd"]))


def _win_grad(h, dproj, W):
    T, D = h.shape
    tk = min(512, T)
    nk = T // tk

    def body(h_ref, dp_ref, g_ref, acc):
        k = pl.program_id(1)

        @pl.when(k == 0)
        def _():
            acc[...] = jnp.zeros_like(acc)

        acc[...] += _dot_tn(h_ref[...], dp_ref[...])

        @pl.when(k == nk - 1)
        def _():
            g_ref[...] = acc[...].astype(BF16)

    return pl.pallas_call(
        body, name="win_grad",
        out_shape=jax.ShapeDtypeStruct((4, D, W), BF16),
        grid=(4, nk),
        in_specs=[pl.BlockSpec((tk, D), lambda j, k: (k, 0)), pl.BlockSpec((tk, W), lambda j, k: (k, j))],
        out_specs=pl.BlockSpec((None, D, W), lambda j, k: (j, 0, 0)),
        scratch_shapes=[pltpu.VMEM((D, W), F32)],
        compiler_params=_cp(("arbitrary", "arbitrary")),
    )(h, dproj)


def _in_bwd(dxn, dproj, x, pre_g, win, l, ride=None):
    T, D = x.shape
    W = win.shape[-1]
    tm = min(512, T)

    def body(dxn_ref, dp_ref, x_ref, g_ref, w_ref, dx_ref, dg_ref, acc):
        i = pl.program_id(0)
        j = pl.program_id(1)

        @pl.when(j == 0)
        def _():
            acc[...] = jnp.zeros_like(acc)

        @pl.when((i == 0) & (j == 0))
        def _():
            dg_ref[...] = jnp.zeros_like(dg_ref)

        acc[...] += _dot_nt(dp_ref[...], w_ref[...])

        @pl.when(j == 3)
        def _():
            _, vjp = jax.vjp(_rms, x_ref[...], g_ref[...])
            dx, dg = vjp(acc[...])
            dx_ref[...] = dxn_ref[...] + dx
            dg_ref[0:1, :] += dg

    tile = pl.BlockSpec((tm, D), lambda i, j: (i, 0))
    return _pcall(
        body, name="in_bwd",
        out_shape=(jax.ShapeDtypeStruct((T, D), F32), jax.ShapeDtypeStruct((8, D), F32)),
        grid=(T // tm, 4),
        in_specs=[tile, pl.BlockSpec((tm, W), lambda i, j: (i, j)), tile,
                  pl.BlockSpec((None, 1, D), lambda i, j: (l, 0, 0)),
                  pl.BlockSpec((None, D, W), lambda i, j: (j, 0, 0))],
        out_specs=(tile, pl.BlockSpec((8, D), lambda i, j: (0, 0))),
        scratch_shapes=[pltpu.VMEM((tm, D), F32)],
        sem=("arbitrary", "arbitrary"), args=(dxn, dproj, x, pre_g, win), ride=ride)


def _sum_group(chip, t, u):
    _, A, B = t.shape
    tr = min(256, A)

    def body(k_ref, t_ref, u_ref, o_ref):
        o_ref[...] = ((t_ref[...].astype(F32) + u_ref[0].astype(F32)) + u_ref[1].astype(F32)) + u_ref[2].astype(F32)

    return pl.pallas_call(
        body, name="sum_group",
        out_shape=jax.ShapeDtypeStruct((A, B), F32),
        grid_spec=pltpu.PrefetchScalarGridSpec(
            num_scalar_prefetch=1, grid=(A // tr,),
            in_specs=[pl.BlockSpec((None, tr, B), lambda i, k: (k[0], i, 0)),
                      pl.BlockSpec((3, tr, B), lambda i, k: (0, i, 0))],
            out_specs=pl.BlockSpec((tr, B), lambda i, k: (i, 0))),
        compiler_params=_cp(("arbitrary",)),
    )(chip, t, u)


def _adam_math(w, g, m, v):
    c1 = 1.0 / (1.0 - ADAM_B1 ** ADAM_STEP)
    c2 = 1.0 / (1.0 - ADAM_B2 ** ADAM_STEP)
    nm = ADAM_B1 * m + (1.0 - ADAM_B1) * g
    nv = ADAM_B2 * v + (1.0 - ADAM_B2) * (g * g)
    return -ADAM_LR * ((nm * c1) / (jnp.sqrt(nv * c2) + ADAM_EPS) + ADAM_WD * w), nm, nv


def _adamw_layer(prev, w, m, v, sa, sb, l, part, ride=None):
    NL, A, B = w.shape
    tr = A
    while tr * B * 4 > ADAM_BLOCK_BYTES and tr % 16 == 0:
        tr //= 2
    nb = A // tr

    def body(p0, p1, p2, p3, w_ref, m_ref, v_ref, sa_ref, sb_ref, g_ref, d_ref, nm_ref, nv_ref):
        g = sa_ref[...] + sb_ref[...]
        g_ref[...] = g
        d_ref[...], nm_ref[...], nv_ref[...] = _adam_math(w_ref[...], g, m_ref[...], v_ref[...])

    lay = pl.BlockSpec((None, tr, B), lambda i: (l, i, 0))
    src = pl.BlockSpec((tr, B), lambda i: (part * nb + i, 0))
    full = jax.ShapeDtypeStruct((NL, A, B), F32)
    if prev is None:
        prev = tuple(lax.empty((NL, A, B), F32) for _ in range(4))
    outs, landed = _pcall(
        body, name="adamw_layer",
        out_shape=(full,) * 4, grid=(nb,),
        in_specs=[ANY] * 4 + [lay, lay, lay, src, src], out_specs=(lay,) * 4,
        sem=("arbitrary",), aliases={0: 0, 1: 1, 2: 2, 3: 3}, args=(*prev, w, m, v, sa, sb), ride=ride)
    return tuple(outs), landed


def _adamw(w, g, m, v):
    shape = w.shape
    cols = shape[-1]
    rows = int(np.prod(shape[:-1]))

    def body(w_ref, g_ref, m_ref, v_ref, d_ref, nm_ref, nv_ref):
        d_ref[...], nm_ref[...], nv_ref[...] = _adam_math(w_ref[...], g_ref[...], m_ref[...], v_ref[...])

    tile = pl.BlockSpec((rows, cols), lambda i: (0, 0))
    out = jax.ShapeDtypeStruct((rows, cols), F32)
    res = pl.pallas_call(
        body, name="adamw",
        out_shape=(out, out, out), grid=(1,),
        in_specs=[tile] * 4, out_specs=(tile,) * 3,
        compiler_params=_cp(("arbitrary",)),
    )(*[a.reshape(rows, cols) for a in (w, g, m, v)])
    return tuple(a.reshape(shape) for a in res)


def _tail_exchange(small, s_in, s_sq):
    def body(s_ref, a_ref, b_ref, o_ref, oa_ref, ob_ref, send_sems, recv_sems, local_sem):
        x, y, c = _place()
        me = 4 * x + 2 * y + c
        sibling = (x, y, 1 - c)
        mine = pltpu.make_async_copy(s_ref, o_ref.at[me], local_sem)
        mine.start()
        swaps = [_remote(a_ref, oa_ref, send_sems, recv_sems, 7, sibling), _remote(b_ref, ob_ref, send_sems, recv_sems, 8, sibling)]
        sends = [_remote(s_ref, o_ref.at[me], send_sems, recv_sems, r, peer) for r, peer in enumerate(_peers(x, y, c))]
        for cp in swaps + sends:
            cp.start()
        for r, peer in enumerate(_peers(x, y, c)):
            theirs = o_ref.at[4 * peer[0] + 2 * peer[1] + peer[2]]
            _remote(theirs, theirs, send_sems, recv_sems, r, peer).wait_recv()
        for cp in sends:
            cp.wait_send()
        for cp in swaps:
            cp.wait()
        mine.wait()

    return pl.pallas_call(
        body, name="tail_exchange",
        out_shape=(jax.ShapeDtypeStruct((8,) + small.shape, small.dtype),
                   jax.ShapeDtypeStruct(s_in.shape, s_in.dtype), jax.ShapeDtypeStruct(s_sq.shape, s_sq.dtype)),
        in_specs=[ANY] * 3, out_specs=(ANY,) * 3,
        scratch_shapes=[pltpu.SemaphoreType.DMA((9,)), pltpu.SemaphoreType.DMA((9,)), pltpu.SemaphoreType.DMA],
        compiler_params=pltpu.CompilerParams(has_side_effects=True),
    )(small, s_in, s_sq)


def _sum_devices(gs):
    NL = len(gs)
    _, R, D = gs[0].shape

    def body(*refs):
        o_ref = refs[NL]
        for l in range(NL):
            acc = refs[l][0]
            for k in range(1, 8):
                acc = acc + refs[l][k]
            o_ref[l] = acc

    return pl.pallas_call(
        body, name="sum_devices",
        out_shape=jax.ShapeDtypeStruct((NL, R, D), F32),
        grid=(1,),
        in_specs=[pl.BlockSpec((8, R, D), lambda i: (0, 0, 0))] * NL,
        out_specs=pl.BlockSpec((NL, R, D), lambda i: (0, 0, 0)),
        compiler_params=_cp(("arbitrary",)),
    )(*gs)


def kernel(x, pre_norm_g, w_in, w_ret_out, conv_w, conv_b, conv_ln_g, conv_ln_b, w_conv_out, w_o, post_norm_g, loss_target, m_pre_norm_g, m_w_in, m_w_ret_out, m_conv_w, m_conv_b, m_conv_ln_g, m_conv_ln_b, m_w_conv_out, m_w_o, m_post_norm_g, v_pre_norm_g, v_w_in, v_w_ret_out, v_conv_w, v_conv_b, v_conv_ln_g, v_conv_ln_b, v_w_conv_out, v_w_o, v_post_norm_g):
    NL, D, W = w_in.shape
    Cc = conv_w.shape[-1]
    T = x.shape[1]
    L = min(256, T)
    tb = _tables(T, L)
    ax, ay, _ = _place()
    chip = (2 * ax + ay).astype(jnp.int32).reshape(1)
    pre_g, cb, lg, lb, post_g = (a.reshape(NL, 1, D) for a in (pre_norm_g, conv_b, conv_ln_g, conv_ln_b, post_norm_g))

    win = [_cast_win(chip, w_in, l) for l in range(NL)]
    wsq = [_cast_wsq(chip, w_ret_out, w_conv_out, w_o, l) for l in range(NL)]
    win[0], wcw = _gather_first(win[0], _place_cw(chip, conv_w))

    saved = []
    xl = x[0]
    for l in range(NL):
        ride = _Ride()
        if l + 1 < NL:
            _ride_gather_ici(ride, "win", win[l + 1])
        (proj, h), got = _fwd_in(xl, pre_g, win[l], l, ride=ride)
        if l + 1 < NL:
            win[l + 1] = got["win"]
        ride = _Ride()
        _ride_gather_ici(ride, "wsq", wsq[l])
        (a_ret, states), got = _ret_fwd(proj, tb, L, ride=ride)
        wsq[l] = got["wsq"]
        ride = _Ride()
        _ride_gather_pass(ride, "wsq", wsq[l])
        if l + 1 < NL:
            _ride_gather_pass(ride, "win", win[l + 1])
        (a_conv, y), got = _conv_fwd(proj, wcw, cb, lg, lb, l, ride=ride)
        wsq[l] = got["wsq"]
        if l + 1 < NL:
            win[l + 1] = got["win"]
        xn, ro, co, ym, z = _merge_fwd(xl, proj, a_ret, a_conv, wsq[l], post_g, l)
        saved.append((xl, proj, h, a_ret, states, a_conv, y, ro, co, ym, z))
        xl = xn
    dx, lsum = _loss_fwd_bwd(xl, loss_target[0])
    loss = lax.psum(jnp.sum(lsum), ("x", "y", "c"))

    gin, gsq, uin, usq = [None] * NL, [None] * NL, [None] * NL, [None] * NL
    s_in, s_sq, o_in, o_sq = [None] * NL, [None] * NL, [None] * NL, [None] * NL
    small, gs = [None] * NL, [None] * NL
    for l in reversed(range(NL)):
        xin, proj, h, a_ret, states, a_conv, y, ro, co, ym, z = saved[l]
        (dproj, da_ret, da_conv, gsq[l], dpost), _ = _merge_bwd(dx, proj, a_ret, a_conv, ro, co, ym, z, wsq[l], post_g, l)
        ride = _Ride()
        _ride_exchange(ride, "gsq", gsq[l], "usq", None, (0, 1, 2))
        (dproj, sg), got = _conv_bwd(dproj, da_conv, y, proj, wcw, lg, lb, l, ride=ride)
        usq[l] = got["usq"]
        ride = _Ride()
        if l + 1 < NL:
            _ride_exchange(ride, "gin", gin[l + 1], "uin", uin[l + 1], (2,))
            _ride_gather_all(ride, "small", small[l + 1], "gs")
        (dproj,), got = _ret_bwd(dproj, da_ret, proj, states, tb, L, ride=ride)
        gin[l] = _win_grad(h, dproj, W)
        ride = _Ride()
        _ride_exchange(ride, "gin", gin[l], "uin", None, (0, 1) if l > 0 else (0, 1, 2))
        if l + 1 < NL:
            uin[l + 1], gs[l + 1] = got["uin"], got["gs"]
            s_in[l + 1] = _sum_group(chip, gin[l + 1], uin[l + 1])
            s_sq[l + 1] = _sum_group(chip, gsq[l + 1], usq[l + 1])
            _ride_swap(ride, "s_in", s_in[l + 1], "o_in")
            _ride_swap(ride, "s_sq", s_sq[l + 1], "o_sq")
        (dx, dpre), got = _in_bwd(dx, dproj, xin, pre_g, win[l], l, ride=ride)
        uin[l] = got["uin"]
        if l + 1 < NL:
            o_in[l + 1], o_sq[l + 1] = got["o_in"], got["o_sq"]
        small[l] = jnp.concatenate([sg, dpre, dpost], axis=0)
    grad_x = dx

    big = {"w_in": None, "w_ret_out": None, "w_conv_out": None, "w_o": None}
    wts = {"w_in": (w_in, m_w_in, v_w_in), "w_ret_out": (w_ret_out, m_w_ret_out, v_w_ret_out),
           "w_conv_out": (w_conv_out, m_w_conv_out, v_w_conv_out), "w_o": (w_o, m_w_o, v_w_o)}
    sq_names = ("w_ret_out", "w_conv_out", "w_o")

    def adam_in(l, ride=None):
        big["w_in"], got = _adamw_layer(big["w_in"], *wts["w_in"], s_in[l], o_in[l], l, 0, ride=ride)
        return got

    def adam_sq(l, part, ride=None):
        n = sq_names[part]
        big[n], got = _adamw_layer(big[n], *wts[n], s_sq[l], o_sq[l], l, part, ride=ride)
        return got

    s_in[0] = _sum_group(chip, gin[0], uin[0])
    s_sq[0] = _sum_group(chip, gsq[0], usq[0])
    gs[0], o_in[0], o_sq[0] = _tail_exchange(small[0], s_in[0], s_sq[0])
    for l in reversed(range(NL)):
        adam_in(l)
        for part in range(3):
            adam_sq(l, part)

    gsm = _sum_devices(gs)
    grads = {
        "pre_norm_g": gsm[:, ROW_PRE], "conv_w": lax.dynamic_slice_in_dim(gsm[:, 0:CONV_K], chip[0] * Cc, Cc, axis=2),
        "conv_b": gsm[:, ROW_CB], "conv_ln_g": gsm[:, ROW_LG], "conv_ln_b": gsm[:, ROW_LB], "post_norm_g": gsm[:, ROW_POST],
    }
    weights = dict(pre_norm_g=pre_norm_g, conv_w=conv_w, conv_b=conv_b, conv_ln_g=conv_ln_g, conv_ln_b=conv_ln_b,
                   post_norm_g=post_norm_g)
    m1 = dict(pre_norm_g=m_pre_norm_g, conv_w=m_conv_w, conv_b=m_conv_b, conv_ln_g=m_conv_ln_g, conv_ln_b=m_conv_ln_b,
              post_norm_g=m_post_norm_g)
    m2 = dict(pre_norm_g=v_pre_norm_g, conv_w=v_conv_w, conv_b=v_conv_b, conv_ln_g=v_conv_ln_g, conv_ln_b=v_conv_ln_b,
              post_norm_g=v_post_norm_g)
    res = {n: (grads[n],) + _adamw(weights[n], grads[n], m1[n], m2[n]) for n in grads}
    res.update(big)
    order = ["pre_norm_g", "w_in", "w_ret_out", "conv_w", "conv_b", "conv_ln_g", "conv_ln_b", "w_conv_out", "w_o", "post_norm_g"]
    return (loss, grad_x[None], *[res[n][0] for n in order], *[res[n][1] for n in order],
            *[res[n][2] for n in order], *[res[n][3] for n in order])
```

```python
import numpy as np
import jax
import jax.numpy as jnp
from jax import lax
from jax.experimental import pallas as pl
from jax.experimental.pallas import tpu as pltpu

F32 = jnp.float32
BF16 = jnp.bfloat16

HEADS = 8
DK = 64
DV = 128
CONV_K = 31
CHUNK = 64
ROPE_BASE = 10000.0
EPS = 1e-6
HALO = 32
CONV_RB = 32

ADAM_LR = 0.001
ADAM_B1 = 0.9
ADAM_B2 = 0.999
ADAM_EPS = 1e-08
ADAM_WD = 0.01
ADAM_STEP = 10
ADAM_BLOCK_BYTES = 2 * 1024 * 1024

VMEM_LIMIT = 56 * 1024 * 1024
MESH_T = pl.DeviceIdType.MESH
ANY = pl.BlockSpec(memory_space=pl.ANY)

ROW_CB, ROW_LG, ROW_LB = 32, 33, 34
ROW_PRE, ROW_POST = 40, 48


def _cp(sem=None, **kw):
    return pltpu.CompilerParams(dimension_semantics=sem, vmem_limit_bytes=VMEM_LIMIT, **kw)


def _dot(a, b):
    return jnp.dot(a, b, preferred_element_type=F32)


def _dot_nt(a, b):
    return lax.dot_general(a, b, (((1,), (1,)), ((), ())), preferred_element_type=F32)


def _dot_tn(a, b):
    return lax.dot_general(a, b, (((0,), (0,)), ((), ())), preferred_element_type=F32)


def _sigmoid(x):
    return jax.nn.sigmoid(x)


def _silu(x):
    return x * _sigmoid(x)


def _rms(x, g):
    return x * lax.rsqrt(jnp.mean(x * x, axis=-1, keepdims=True) + EPS) * g


def _gn_gate(o, g):
    mu = jnp.mean(o, axis=-1, keepdims=True)
    d = o - mu
    var = jnp.mean(d * d, axis=-1, keepdims=True)
    return d * lax.rsqrt(var + EPS) * _silu(g)


def _ln_gate(y, gc, lg, lb):
    mu = jnp.mean(y, axis=-1, keepdims=True)
    d = y - mu
    var = jnp.mean(d * d, axis=-1, keepdims=True)
    return _silu(d * lax.rsqrt(var + EPS) * lg + lb) * _silu(gc)


def _tables(T, L):
    lane = np.arange(128)
    d = lane % DK
    half = DK // 2
    inv = (ROPE_BASE ** (-(np.arange(half, dtype=np.float32)) / half)).astype(np.float32)
    ang = (np.arange(T, dtype=np.float32)[:, None] * inv[None, :]).astype(np.float64)
    angl = ang[:, d % half]
    cos = np.cos(angl)
    sin = np.sin(angl)
    lo = (d < half)[None, :]
    rope = np.stack([cos, np.where(lo, -sin, 0.0), np.where(lo, 0.0, sin)]).astype(np.float32)

    hh = np.arange(HEADS, dtype=np.float64)
    log_g = np.log1p(-np.exp2(-5.0 - hh))
    n = np.arange(L, dtype=np.float64)
    cn = np.arange(L) // CHUNK
    allowed = (cn[None, :] <= cn[:, None])
    dist = np.abs(n[:, None] - n[None, :])
    mask = np.exp(log_g[:, None, None] * dist[None]) * allowed[None]
    mq = ((lane[None, :] // DK) == (np.arange(HEADS)[:, None] % 2)).astype(np.float64)
    qd = np.exp(log_g[:, None] * n[None, :])
    kd = np.exp(log_g[:, None] * (L - n[None, :]))
    qdm = qd[:, :, None] * mq[:, None, :] * (DK ** -0.5)
    kdm = kd[:, :, None] * mq[:, None, :]
    mqs = np.broadcast_to((mq * (DK ** -0.5))[:, None, :], (HEADS, 8, 128))
    cd = np.broadcast_to(np.exp(log_g * L)[:, None, None], (HEADS, 8, 128))
    f = lambda a: jnp.asarray(np.ascontiguousarray(a), dtype=F32)
    return dict(rope=f(rope), mask=f(mask), qdm=f(qdm), kdm=f(kdm), mqs=f(mqs), cd=f(cd))


def _rot(b, c, sl, sh):
    return b * c + pltpu.roll(b, 96, axis=1) * sl + pltpu.roll(b, 32, axis=1) * sh


def _rot_t(d, c, sl, sh):
    return d * c + pltpu.roll(d * sl, 32, axis=1) + pltpu.roll(d * sh, 96, axis=1)


def _place():
    return lax.axis_index("x"), lax.axis_index("y"), lax.axis_index("c")


def _other_chips(x, y):
    return [(1 - x, y), (x, 1 - y), (1 - x, 1 - y)]


def _remote(src, dst, send_sems, recv_sems, k, to):
    return pltpu.make_async_remote_copy(src_ref=src, dst_ref=dst, send_sem=send_sems.at[k], recv_sem=recv_sems.at[k],
                                        device_id=to, device_id_type=MESH_T)


class _Ride:
    def __init__(self):
        self.arrays, self.kinds, self.names = [], [], []
        self.fresh = []
        self.ops = []

    def read(self, name, a):
        self.names.append(name)
        self.arrays.append(a)
        self.kinds.append("in")

    def inout(self, name, a):
        self.names.append(name)
        self.arrays.append(a)
        self.kinds.append("inout")

    def land(self, name, shape, dtype):
        self.fresh.append((name, jax.ShapeDtypeStruct(shape, dtype)))

    def op(self, n_sems, start, finish):
        self.ops.append((n_sems, start, finish))


def _pcall(body, *, name, grid, in_specs, out_specs, out_shape, args, scratch_shapes=(), sem, aliases=None, ride=None):
    if ride is None or not ride.ops:
        outs = pl.pallas_call(body, name=name, grid=grid, in_specs=list(in_specs), out_specs=tuple(out_specs),
                              out_shape=tuple(out_shape), scratch_shapes=list(scratch_shapes),
                              input_output_aliases=dict(aliases or {}), compiler_params=_cp(sem))(*args)
        return outs, {}

    ni, no, nr = len(args), len(out_shape), len(ride.arrays)
    inout = [i for i, k in enumerate(ride.kinds) if k == "inout"]
    r_out_shapes = [jax.ShapeDtypeStruct(ride.arrays[i].shape, ride.arrays[i].dtype) for i in inout] + [s for _, s in ride.fresh]
    r_out_names = [ride.names[i] for i in inout] + [n for n, _ in ride.fresh]
    nro = len(r_out_shapes)
    n_sems = sum(n for n, _, _ in ride.ops)
    n_scr = len(scratch_shapes)
    nd = len(grid)

    def wrapped(*refs):
        ins, rin = refs[:ni], refs[ni:ni + nr]
        outs, rout = refs[ni + nr:ni + nr + no], refs[ni + nr + no:ni + nr + no + nro]
        scr = refs[ni + nr + no + nro:ni + nr + no + nro + n_scr]
        send_sems, recv_sems = refs[-2], refs[-1]
        view = {nm: r for nm, r, k in zip(ride.names, rin, ride.kinds) if k == "in"}
        view.update(dict(zip(r_out_names, rout)))
        first = pl.program_id(0) == 0
        last = pl.program_id(0) == grid[0] - 1
        for d in range(1, nd):
            first = first & (pl.program_id(d) == 0)
            last = last & (pl.program_id(d) == grid[d] - 1)

        @pl.when(first)
        def _():
            base = 0
            for n, start, _ in ride.ops:
                start(view, send_sems, recv_sems, base)
                base += n

        body(*ins, *outs, *scr)

        @pl.when(last)
        def _():
            base = 0
            for n, _, finish in ride.ops:
                finish(view, send_sems, recv_sems, base)
                base += n

    res = pl.pallas_call(
        wrapped, name=name, grid=grid,
        in_specs=list(in_specs) + [ANY] * nr, out_specs=tuple(out_specs) + (ANY,) * nro,
        out_shape=tuple(out_shape) + tuple(r_out_shapes),
        scratch_shapes=list(scratch_shapes) + [pltpu.SemaphoreType.DMA((n_sems,)), pltpu.SemaphoreType.DMA((n_sems,))],
        input_output_aliases={**dict(aliases or {}), **{ni + i: no + j for j, i in enumerate(inout)}},
        compiler_params=_cp(sem),
    )(*args, *ride.arrays)
    return res[:no], dict(zip(r_out_names, res[no:]))


def _half(ref, chip_idx, cc):
    n = ref.shape[1] // 2
    return ref.at[chip_idx, pl.ds(cc * n, n)]


def _ride_gather_ici(ride, name, a):
    ride.inout(name, a)

    def start(view, ss, rs, b):
        x, y, c = _place()
        mine = _half(view[name], 2 * x + y, c)
        for j, (cx, cy) in enumerate(_other_chips(x, y)):
            _remote(mine, mine, ss, rs, b + j, (cx, cy, c)).start()

    def finish(view, ss, rs, b):
        x, y, c = _place()
        mine = _half(view[name], 2 * x + y, c)
        for j, (cx, cy) in enumerate(_other_chips(x, y)):
            theirs = _half(view[name], 2 * cx + cy, c)
            _remote(theirs, theirs, ss, rs, b + j, (cx, cy, c)).wait_recv()
        for j, (cx, cy) in enumerate(_other_chips(x, y)):
            _remote(mine, mine, ss, rs, b + j, (cx, cy, c)).wait_send()

    ride.op(3, start, finish)


def _ride_gather_pass(ride, name, a):
    ride.inout(name, a)

    def start(view, ss, rs, b):
        x, y, c = _place()
        for j, (cx, cy) in enumerate(_other_chips(x, y)):
            blk = _half(view[name], 2 * cx + cy, c)
            _remote(blk, blk, ss, rs, b + j, (x, y, 1 - c)).start()

    def finish(view, ss, rs, b):
        x, y, c = _place()
        for j, (cx, cy) in enumerate(_other_chips(x, y)):
            theirs = _half(view[name], 2 * cx + cy, 1 - c)
            _remote(theirs, theirs, ss, rs, b + j, (x, y, 1 - c)).wait_recv()
        for j, (cx, cy) in enumerate(_other_chips(x, y)):
            blk = _half(view[name], 2 * cx + cy, c)
            _remote(blk, blk, ss, rs, b + j, (x, y, 1 - c)).wait_send()

    ride.op(3, start, finish)


def _ride_exchange(ride, src_name, src, dst_name, dst, rels):
    ride.read(src_name, src)
    if dst is None:
        ride.land(dst_name, (3,) + src.shape[1:], src.dtype)
    else:
        ride.inout(dst_name, dst)

    def start(view, ss, rs, b):
        x, y, c = _place()
        chips = _other_chips(x, y)
        for i, j in enumerate(rels):
            cx, cy = chips[j]
            _remote(view[src_name].at[2 * cx + cy], view[dst_name].at[j], ss, rs, b + i, (cx, cy, c)).start()

    def finish(view, ss, rs, b):
        x, y, c = _place()
        chips = _other_chips(x, y)
        for i, j in enumerate(rels):
            cx, cy = chips[j]
            _remote(view[src_name].at[2 * cx + cy], view[dst_name].at[j], ss, rs, b + i, (cx, cy, c)).wait()

    ride.op(len(rels), start, finish)


def _ride_swap(ride, src_name, src, dst_name):
    ride.read(src_name, src)
    ride.land(dst_name, src.shape, src.dtype)

    def start(view, ss, rs, b):
        x, y, c = _place()
        _remote(view[src_name], view[dst_name], ss, rs, b, (x, y, 1 - c)).start()

    def finish(view, ss, rs, b):
        x, y, c = _place()
        _remote(view[src_name], view[dst_name], ss, rs, b, (x, y, 1 - c)).wait()

    ride.op(1, start, finish)


def _peers(x, y, c):
    flip = lambda v, b: 1 - v if b else v
    return [(flip(x, r & 4), flip(y, r & 2), flip(c, r & 1)) for r in range(1, 8)]


def _ride_gather_all(ride, src_name, src, dst_name):
    ride.read(src_name, src)
    ride.land(dst_name, (8,) + src.shape, src.dtype)

    def start(view, ss, rs, b):
        x, y, c = _place()
        me = 4 * x + 2 * y + c
        pltpu.make_async_copy(view[src_name], view[dst_name].at[me], ss.at[b + 7]).start()
        for r, peer in enumerate(_peers(x, y, c)):
            _remote(view[src_name], view[dst_name].at[me], ss, rs, b + r, peer).start()

    def finish(view, ss, rs, b):
        x, y, c = _place()
        me = 4 * x + 2 * y + c
        for r, peer in enumerate(_peers(x, y, c)):
            theirs = view[dst_name].at[4 * peer[0] + 2 * peer[1] + peer[2]]
            _remote(theirs, theirs, ss, rs, b + r, peer).wait_recv()
        for r, peer in enumerate(_peers(x, y, c)):
            _remote(view[src_name], view[dst_name].at[me], ss, rs, b + r, peer).wait_send()
        pltpu.make_async_copy(view[src_name], view[dst_name].at[me], ss.at[b + 7]).wait()

    ride.op(8, start, finish)


def _cast_win(chip, w_in, l):
    _, D, W = w_in.shape
    tr = min(256, D)

    def body(chip_ref, w_ref, o_ref):
        o_ref[...] = w_ref[...].astype(BF16)

    return pl.pallas_call(
        body, name="cast_win",
        out_shape=jax.ShapeDtypeStruct((4, D, W), BF16),
        grid_spec=pltpu.PrefetchScalarGridSpec(
            num_scalar_prefetch=1, grid=(D // tr,),
            in_specs=[pl.BlockSpec((None, tr, W), lambda r, c: (l, r, 0))],
            out_specs=pl.BlockSpec((None, tr, W), lambda r, c: (c[0], r, 0))),
        compiler_params=_cp(("arbitrary",)),
    )(chip, w_in)


def _cast_wsq(chip, w_ro, w_co, w_o, l):
    _, R, D = w_ro.shape

    def body(chip_ref, a_ref, b_ref, c_ref, o_ref):
        o_ref[0:R, :] = a_ref[...].astype(BF16)
        o_ref[R:2 * R, :] = b_ref[...].astype(BF16)
        o_ref[2 * R:3 * R, :] = c_ref[...].astype(BF16)

    spec = pl.BlockSpec((None, R, D), lambda i, c: (l, 0, 0))
    return pl.pallas_call(
        body, name="cast_wsq",
        out_shape=jax.ShapeDtypeStruct((4, 3 * R, D), BF16),
        grid_spec=pltpu.PrefetchScalarGridSpec(
            num_scalar_prefetch=1, grid=(1,),
            in_specs=[spec, spec, spec],
            out_specs=pl.BlockSpec((None, 3 * R, D), lambda i, c: (c[0], 0, 0))),
        compiler_params=_cp(("arbitrary",)),
    )(chip, w_ro, w_co, w_o)


def _place_cw(chip, conv_w):
    NL, K, Cc = conv_w.shape

    def body(chip_ref, w_ref, o_ref):
        o_ref[...] = w_ref[...]

    return pl.pallas_call(
        body, name="place_cw",
        out_shape=jax.ShapeDtypeStruct((4, NL, K, Cc), F32),
        grid_spec=pltpu.PrefetchScalarGridSpec(
            num_scalar_prefetch=1, grid=(1,),
            in_specs=[pl.BlockSpec((NL, K, Cc), lambda i, c: (0, 0, 0))],
            out_specs=pl.BlockSpec((None, NL, K, Cc), lambda i, c: (c[0], 0, 0, 0))),
        compiler_params=_cp(("arbitrary",)),
    )(chip, conv_w)


def _gather_first(win0, wcw):
    n_arr = 2

    def body(a0, a1, o0, o1, send_sems, recv_sems):
        x, y, c = _place()
        sibling = (x, y, 1 - c)
        chips = _other_chips(x, y)
        outs = (o0, o1)

        def copy(k, a, cx, cy, cc, to):
            blk = _half(outs[a], 2 * cx + cy, cc)
            return _remote(blk, blk, send_sems, recv_sems, k, to)

        first = [copy(3 * a + j, a, x, y, c, (*chip, c)) for a in range(n_arr) for j, chip in enumerate(chips)]
        for cp in first:
            cp.start()
        passed = [copy(3 * n_arr + 3 * a + j, a, *chip, c, sibling) for a in range(n_arr) for j, chip in enumerate(chips)]
        for a in range(n_arr):
            for j, chip in enumerate(chips):
                copy(3 * a + j, a, *chip, c, sibling).wait_recv()
                passed[3 * a + j].start()
        for a in range(n_arr):
            for j, chip in enumerate(chips):
                copy(3 * n_arr + 3 * a + j, a, *chip, 1 - c, sibling).wait_recv()
        for cp in first + passed:
            cp.wait_send()

    ins = (win0, wcw)
    return pl.pallas_call(
        body, name="gather_first",
        out_shape=tuple(jax.ShapeDtypeStruct(a.shape, a.dtype) for a in ins),
        in_specs=[ANY] * n_arr, out_specs=(ANY,) * n_arr,
        scratch_shapes=[pltpu.SemaphoreType.DMA((6 * n_arr,)), pltpu.SemaphoreType.DMA((6 * n_arr,))],
        input_output_aliases={0: 0, 1: 1},
        compiler_params=pltpu.CompilerParams(has_side_effects=True),
    )(*ins)


def _fwd_in(x, pre_g, win, l, ride=None):
    T, D = x.shape
    W = win.shape[-1]
    tm = min(1024, T)

    def body(x_ref, g_ref, w_ref, p_ref, h_ref, hs):
        @pl.when(pl.program_id(1) == 0)
        def _():
            hb = _rms(x_ref[...], g_ref[...]).astype(BF16)
            hs[...] = hb
            h_ref[...] = hb

        p_ref[...] = _dot(hs[...], w_ref[...]).astype(BF16)

    return _pcall(
        body, name="fwd_in",
        out_shape=(jax.ShapeDtypeStruct((T, 4 * W), BF16), jax.ShapeDtypeStruct((T, D), BF16)),
        grid=(T // tm, 4),
        in_specs=[pl.BlockSpec((tm, D), lambda i, j: (i, 0)),
                  pl.BlockSpec((None, 1, D), lambda i, j: (l, 0, 0)),
                  pl.BlockSpec((None, D, W), lambda i, j: (j, 0, 0))],
        out_specs=(pl.BlockSpec((tm, W), lambda i, j: (i, j)),
                   pl.BlockSpec((tm, D), lambda i, j: (i, 0))),
        scratch_shapes=[pltpu.VMEM((tm, D), BF16)],
        sem=("arbitrary", "arbitrary"), args=(x, pre_g, win), ride=ride)


def _ret_specs(T, L):
    rope = pl.BlockSpec((3, L, 128), lambda s: (0, s, 0))
    mask = pl.BlockSpec((HEADS, L, L), lambda s: (0, 0, 0), pipeline_mode=pl.Buffered(1))
    qdm = pl.BlockSpec((HEADS, L, 128), lambda s: (0, 0, 0), pipeline_mode=pl.Buffered(1))
    small = pl.BlockSpec((HEADS, 8, 128), lambda s: (0, 0, 0))
    return [rope, mask, qdm, qdm, small, small]


QK = HEADS * DK
VW = HEADS * DV
PW = 2 * QK + 2 * VW


def _zero_at_start(ref):
    @pl.when(pl.program_id(0) == 0)
    def _():
        ref[...] = jnp.zeros_like(ref)


def _ret_fwd_part(p_ref, rope_ref, m_ref, qdm_ref, kdm_ref, mqs_ref, cd_ref, a_ref, st_ref, state):
    c, sl, sh = rope_ref[0], rope_ref[1], rope_ref[2]
    for j in range(HEADS // 2):
        rq = _rot(p_ref[:, 128 * j:128 * (j + 1)].astype(F32), c, sl, sh)
        rk = _rot(p_ref[:, QK + 128 * j:QK + 128 * (j + 1)].astype(F32), c, sl, sh)
        rkb = rk.astype(BF16)
        for e in range(2):
            h = 2 * j + e
            v = p_ref[:, 2 * QK + DV * h:2 * QK + DV * (h + 1)]
            g = p_ref[:, 2 * QK + VW + DV * h:2 * QK + VW + DV * (h + 1)].astype(F32)
            a = (rq * mqs_ref[h, 0:1, :]).astype(BF16)
            p = (_dot_nt(a, rkb) * m_ref[h]).astype(BF16)
            st = state[h]
            st_ref[h] = st
            o = _dot(p, v) + _dot((rq * qdm_ref[h]).astype(BF16), st.astype(BF16))
            state[h] = st * cd_ref[h, 0:1, :] + _dot_tn((rk * kdm_ref[h]).astype(BF16), v)
            a_ref[:, DV * h:DV * (h + 1)] = _gn_gate(o, g).astype(BF16)


def _shift_copies(src, sh):
    rows = sh.shape[1]
    for b in range(1, 8):
        sh[b - 1, :, :] = src[pl.ds(b, rows), :]


def _window(src, sh, start, rows):
    b = start % 8
    if b == 0:
        return src[pl.ds(start, rows), :]
    return sh[b - 1, pl.ds(start - b, rows), :]


def _conv_taps(wbuf, src, sh, base, rows):
    acc = wbuf[pl.ds(0, 1), :] * _window(src, sh, base, rows)
    for k in range(1, CONV_K):
        acc = acc + wbuf[pl.ds(k, 1), :] * _window(src, sh, base + k, rows)
    return acc


def _load_conv_w(cw_ref, wbuf, flip):
    for k in range(CONV_K):
        row = jnp.concatenate([cw_ref[c, pl.ds(k, 1), :] for c in range(4)], axis=-1)
        wbuf[pl.ds(CONV_K - 1 - k if flip else k, 1), :] = row
    wbuf[pl.ds(CONV_K, 1), :] = jnp.zeros_like(wbuf[pl.ds(CONV_K, 1), :])


def _conv_fwd_part(p_ref, cw_ref, cb_ref, lg_ref, lb_ref, a_ref, y_ref, ubuf, wbuf, ush):
    tc, C = y_ref.shape
    off = HALO - (CONV_K - 1)
    i = pl.program_id(0)

    @pl.when(i == 0)
    def _():
        ubuf[0:HALO, :] = jnp.zeros((HALO, C), F32)
        _load_conv_w(cw_ref, wbuf, False)

    @pl.when(i > 0)
    def _():
        ubuf[0:HALO, :] = ubuf[tc:tc + HALO, :]

    ga = p_ref[:, PW:PW + C].astype(F32)
    gb = p_ref[:, PW + C:PW + 2 * C].astype(F32)
    ubuf[HALO:HALO + tc, :] = ga * _sigmoid(gb)
    _shift_copies(ubuf, ush)
    for r in range(tc // CONV_RB):
        y_ref[r * CONV_RB:(r + 1) * CONV_RB, :] = _conv_taps(wbuf, ubuf, ush, r * CONV_RB + off, CONV_RB) + cb_ref[...]
    gc = p_ref[:, PW + 2 * C:PW + 3 * C].astype(F32)
    a_ref[...] = _ln_gate(y_ref[...], gc, lg_ref[...], lb_ref[...]).astype(BF16)


def _branch_fwd(proj, tb, L, wcw, conv_b, ln_g, ln_b, l, ride=None):
    T = proj.shape[0]
    C = conv_b.shape[-1]
    Cc = wcw.shape[-1]
    nS = T // L

    def body(p_ref, rope_ref, m_ref, qdm_ref, kdm_ref, mqs_ref, cd_ref, cw_ref, cb_ref, lg_ref, lb_ref,
             ar_ref, st_ref, ac_ref, y_ref, state, ubuf, wbuf, ush):
        _zero_at_start(state)
        _conv_fwd_part(p_ref, cw_ref, cb_ref, lg_ref, lb_ref, ac_ref, y_ref, ubuf, wbuf, ush)
        _ret_fwd_part(p_ref, rope_ref, m_ref, qdm_ref, kdm_ref, mqs_ref, cd_ref, ar_ref, st_ref, state)

    vec = pl.BlockSpec((None, 1, C), lambda s: (l, 0, 0))
    tile = pl.BlockSpec((L, C), lambda s: (s, 0))
    return _pcall(
        body, name="branch_fwd",
        out_shape=(jax.ShapeDtypeStruct((T, VW), BF16), jax.ShapeDtypeStruct((nS, HEADS, 128, DV), F32),
                   jax.ShapeDtypeStruct((T, C), BF16), jax.ShapeDtypeStruct((T, C), F32)),
        grid=(nS,),
        in_specs=[pl.BlockSpec((L, PW + 3 * C), lambda s: (s, 0))] + _ret_specs(T, L)
                 + [pl.BlockSpec((4, None, CONV_K, Cc), lambda s: (0, l, 0, 0)), vec, vec, vec],
        out_specs=(pl.BlockSpec((L, VW), lambda s: (s, 0)), pl.BlockSpec((None, HEADS, 128, DV), lambda s: (s, 0, 0, 0)),
                   tile, tile),
        scratch_shapes=[pltpu.VMEM((HEADS, 128, DV), F32), pltpu.VMEM((HALO + L, C), F32), pltpu.VMEM((HALO, C), F32),
                        pltpu.VMEM((7, HALO + L - 8, C), F32)],
        sem=("arbitrary",), ride=ride,
        args=(proj, tb["rope"], tb["mask"], tb["qdm"], tb["kdm"], tb["mqs"], tb["cd"], wcw, conv_b, ln_g, ln_b))


def _merge_fwd(x, proj, a_ret, a_conv, wsq, post_g, l, ride=None):
    T, D = x.shape
    R = wsq.shape[1] // 3
    tm = min(512, T)

    def body(x_ref, p_ref, ar_ref, ac_ref, wro_ref, wco_ref, wo_ref, g_ref, xn_ref, ro_ref, co_ref, ym_ref, z_ref):
        ro = _dot(ar_ref[...], wro_ref[...].reshape(4 * R, D))
        co = _dot(ac_ref[...], wco_ref[...].reshape(4 * R, D))
        ym = (_sigmoid(p_ref[:, 0:D].astype(F32)) * ro + _sigmoid(p_ref[:, D:2 * D].astype(F32)) * co).astype(BF16)
        z = _dot(ym, wo_ref[...].reshape(4 * R, D))
        ro_ref[...] = ro.astype(BF16)
        co_ref[...] = co.astype(BF16)
        ym_ref[...] = ym
        z_ref[...] = z.astype(BF16)
        xn_ref[...] = x_ref[...] + _rms(z, g_ref[...])

    tile = pl.BlockSpec((tm, D), lambda i: (i, 0))
    wspec = lambda m: pl.BlockSpec((4, R, D), lambda i: (0, m, 0))
    act = jax.ShapeDtypeStruct((T, D), BF16)
    return _pcall(
        body, name="merge_fwd",
        out_shape=(jax.ShapeDtypeStruct((T, D), F32), act, act, act, act),
        grid=(T // tm,),
        in_specs=[tile, pl.BlockSpec((tm, 2 * D), lambda i: (i, 3)), tile, tile,
                  wspec(0), wspec(1), wspec(2), pl.BlockSpec((None, 1, D), lambda i: (l, 0, 0))],
        out_specs=(tile, tile, tile, tile, tile),
        sem=("arbitrary",), ride=ride, args=(x, proj, a_ret, a_conv, wsq, wsq, wsq, post_g))


def _loss_fwd_bwd(y, target):
    T, D = y.shape
    tm = min(512, T)

    def body(y_ref, t_ref, dy_ref, ls_ref):
        @pl.when(pl.program_id(0) == 0)
        def _():
            ls_ref[...] = jnp.zeros_like(ls_ref)

        e = y_ref[...] - t_ref[...]
        dy_ref[...] = e * (1.0 / D)
        ls_ref[...] += jnp.sum((e * e).reshape(tm // 8, 8, D), axis=0) * (0.5 / D)

    tile = pl.BlockSpec((tm, D), lambda i: (i, 0))
    return pl.pallas_call(
        body, name="loss",
        out_shape=(jax.ShapeDtypeStruct((T, D), F32), jax.ShapeDtypeStruct((8, D), F32)),
        grid=(T // tm,),
        in_specs=[tile, tile],
        out_specs=(tile, pl.BlockSpec((8, D), lambda i: (0, 0))),
        compiler_params=_cp(("arbitrary",)),
    )(y, target)


def _merge_bwd(dxn, proj, a_ret, a_conv, ro, co, ym, z, wsq, post_g, l, ride=None):
    T, D = dxn.shape
    R = wsq.shape[1] // 3
    tm = min(512, T)
    n = T // tm

    def body(dx_ref, p_ref, ar_ref, ac_ref, ro_ref, co_ref, ym_ref, z_ref, wro_ref, wco_ref, wo_ref, g_ref,
             dp_ref, dar_ref, dac_ref, gsq_ref, dg_ref, acc, stage):
        i = pl.program_id(0)

        @pl.when(i == 0)
        def _():
            acc[...] = jnp.zeros_like(acc)
            dg_ref[...] = jnp.zeros_like(dg_ref)

        _, vjp = jax.vjp(_rms, z_ref[...].astype(F32), g_ref[...])
        dz, dg = vjp(dx_ref[...])
        dg_ref[0:1, :] += dg
        dzb = dz.astype(BF16)
        dym = _dot_nt(dzb, wo_ref[...].reshape(4 * R, D))
        acc[2] += _dot_tn(ym_ref[...], dzb)
        sr = _sigmoid(p_ref[:, 0:D].astype(F32))
        sc = _sigmoid(p_ref[:, D:2 * D].astype(F32))
        rov = ro_ref[...].astype(F32)
        cov = co_ref[...].astype(F32)
        dp_ref[:, 0:D] = (dym * rov * sr * (1.0 - sr)).astype(BF16)
        dp_ref[:, D:2 * D] = (dym * cov * sc * (1.0 - sc)).astype(BF16)
        dro = (dym * sr).astype(BF16)
        dco = (dym * sc).astype(BF16)
        dar_ref[...] = _dot_nt(dro, wro_ref[...].reshape(4 * R, D)).astype(BF16)
        dac_ref[...] = _dot_nt(dco, wco_ref[...].reshape(4 * R, D)).astype(BF16)
        acc[0] += _dot_tn(ar_ref[...], dro)
        acc[1] += _dot_tn(ac_ref[...], dco)

        @pl.when(i == n - 1)
        def _():
            for m in range(3):
                stage[...] = acc[m].astype(BF16).reshape(4, R, D)
                pltpu.sync_copy(stage, gsq_ref.at[:, pl.ds(m * R, R), :])

    tile = pl.BlockSpec((tm, D), lambda i: (i, 0))
    wspec = lambda m: pl.BlockSpec((4, R, D), lambda i: (0, m, 0), pipeline_mode=pl.Buffered(1))
    return _pcall(
        body, name="merge_bwd",
        out_shape=(jax.ShapeDtypeStruct(proj.shape, BF16), jax.ShapeDtypeStruct((T, D), BF16),
                   jax.ShapeDtypeStruct((T, D), BF16), jax.ShapeDtypeStruct(wsq.shape, BF16),
                   jax.ShapeDtypeStruct((8, D), F32)),
        grid=(n,),
        in_specs=[tile, pl.BlockSpec((tm, 2 * D), lambda i: (i, 3)), tile, tile, tile, tile, tile, tile,
                  wspec(0), wspec(1), wspec(2), pl.BlockSpec((None, 1, D), lambda i: (l, 0, 0))],
        out_specs=(pl.BlockSpec((tm, 2 * D), lambda i: (i, 3)), tile, tile, ANY, pl.BlockSpec((8, D), lambda i: (0, 0))),
        scratch_shapes=[pltpu.VMEM((3, 4 * R, D), F32), pltpu.VMEM((4, R, D), BF16)],
        sem=("arbitrary",), args=(dxn, proj, a_ret, a_conv, ro, co, ym, z, wsq, wsq, wsq, post_g), ride=ride)


def _conv_bwd_part(n, da_ref, y_ref, p_ref, ph_ref, cw_ref, lg_ref, lb_ref, dp_ref,
                   dcbuf, ubuf, dubuf, wbuf, dwacc, vacc, dsh, ush):
    tc, C = y_ref.shape
    off = HALO - (CONV_K - 1)
    nrb = tc // CONV_RB
    t = pl.program_id(0)
    i = n - 1 - t

    @pl.when(t == 0)
    def _():
        dcbuf[tc:tc + HALO, :] = jnp.zeros((HALO, C), F32)
        dwacc[...] = jnp.zeros_like(dwacc)
        vacc[...] = jnp.zeros_like(vacc)
        _load_conv_w(cw_ref, wbuf, True)

    @pl.when(t > 0)
    def _():
        dcbuf[tc:tc + HALO, :] = dcbuf[0:HALO, :]

    gc = p_ref[:, PW + 2 * C:PW + 3 * C].astype(F32)
    _, vjp = jax.vjp(_ln_gate, y_ref[...], gc, lg_ref[...], lb_ref[...])
    dy, dgc, dlg, dlb = vjp(da_ref[...].astype(F32))
    dcbuf[0:tc, :] = dy
    dp_ref[:, PW + 2 * C:PW + 3 * C] = dgc.astype(BF16)
    vacc[0:1, :] += jnp.sum(dy, axis=0, keepdims=True)
    vacc[1:2, :] += dlg
    vacc[2:3, :] += dlb

    ga = p_ref[:, PW:PW + C].astype(F32)
    sb = _sigmoid(p_ref[:, PW + C:PW + 2 * C].astype(F32))
    ubuf[HALO:HALO + tc, :] = ga * sb
    uh = ph_ref[:, 0:C].astype(F32) * _sigmoid(ph_ref[:, C:2 * C].astype(F32))
    ubuf[0:HALO, :] = jnp.where(i > 0, uh, 0.0)

    _shift_copies(dcbuf, dsh)
    _shift_copies(ubuf, ush)
    for r in range(nrb):
        dubuf[r * CONV_RB:(r + 1) * CONV_RB, :] = _conv_taps(wbuf, dcbuf, dsh, r * CONV_RB, CONV_RB)
    du = dubuf[...]
    dp_ref[:, PW:PW + C] = (du * sb).astype(BF16)
    dp_ref[:, PW + C:PW + 2 * C] = (du * ga * sb * (1.0 - sb)).astype(BF16)

    for r in range(nrb):
        dyb = dcbuf[r * CONV_RB:(r + 1) * CONV_RB, :]
        for k in range(CONV_K):
            pr = dyb * _window(ubuf, ush, r * CONV_RB + off + k, CONV_RB)
            dwacc[8 * k:8 * k + 8, :] += jnp.sum(pr.reshape(CONV_RB // 8, 8, C), axis=0)


def _conv_bwd_final(n, sg_ref, dwacc, vacc):
    C = sg_ref.shape[-1]

    @pl.when(pl.program_id(0) == n - 1)
    def _():
        for k in range(CONV_K):
            sg_ref[pl.ds(k, 1), :] = jnp.sum(dwacc[8 * k:8 * k + 8, :], axis=0, keepdims=True)
        sg_ref[pl.ds(CONV_K, 1), :] = jnp.zeros((1, C), F32)
        sg_ref[ROW_CB:ROW_CB + 8, :] = jnp.zeros((8, C), F32)
        sg_ref[ROW_CB:ROW_CB + 3, :] = vacc[0:3, :]


def _ret_bwd_part(da_ref, p_ref, st_ref, rope_ref, m_ref, qdm_ref, kdm_ref, mqs_ref, cd_ref, dp_ref, gst):
    c, sl, sh = rope_ref[0], rope_ref[1], rope_ref[2]
    for j in range(HEADS // 2):
        rq = _rot(p_ref[:, 128 * j:128 * (j + 1)].astype(F32), c, sl, sh)
        rk = _rot(p_ref[:, QK + 128 * j:QK + 128 * (j + 1)].astype(F32), c, sl, sh)
        rkb = rk.astype(BF16)
        drq = jnp.zeros_like(rq)
        drk = jnp.zeros_like(rk)
        for e in range(2):
            h = 2 * j + e
            v = p_ref[:, 2 * QK + DV * h:2 * QK + DV * (h + 1)]
            g = p_ref[:, 2 * QK + VW + DV * h:2 * QK + VW + DV * (h + 1)].astype(F32)
            mqs = mqs_ref[h, 0:1, :]
            a = (rq * mqs).astype(BF16)
            aq = (rq * qdm_ref[h]).astype(BF16)
            kdv = (rk * kdm_ref[h]).astype(BF16)
            mk = m_ref[h]
            p = (_dot_nt(a, rkb) * mk).astype(BF16)
            stb = st_ref[h].astype(BF16)
            o = _dot(p, v) + _dot(aq, stb)
            _, vjp = jax.vjp(_gn_gate, o, g)
            do, dg = vjp(da_ref[:, DV * h:DV * (h + 1)].astype(F32))
            dob = do.astype(BF16)
            gs = gst[h]
            gsb = gs.astype(BF16)
            ds = (_dot_nt(dob, v) * mk).astype(BF16)
            drq = drq + _dot(ds, rkb) * mqs + _dot_nt(dob, stb) * qdm_ref[h]
            drk = drk + _dot_tn(ds, a) + _dot_nt(v, gsb) * kdm_ref[h]
            dv = _dot_tn(p, dob) + _dot(kdv, gsb)
            gst[h] = _dot_tn(aq, dob) + gs * cd_ref[h, 0:1, :]
            dp_ref[:, 2 * QK + DV * h:2 * QK + DV * (h + 1)] = dv.astype(BF16)
            dp_ref[:, 2 * QK + VW + DV * h:2 * QK + VW + DV * (h + 1)] = dg.astype(BF16)
        dp_ref[:, 128 * j:128 * (j + 1)] = _rot_t(drq, c, sl, sh).astype(BF16)
        dp_ref[:, QK + 128 * j:QK + 128 * (j + 1)] = _rot_t(drk, c, sl, sh).astype(BF16)


def _branch_bwd(dproj, da_ret, da_conv, y, proj, states, tb, L, wcw, ln_g, ln_b, l, ride=None):
    T, C = y.shape
    Cc = wcw.shape[-1]
    nS = T // L
    hb = L // HALO

    def body(dpin_ref, dar_ref, dac_ref, y_ref, p_ref, ph_ref, st_ref, rope_ref, m_ref, qdm_ref, kdm_ref, mqs_ref, cd_ref,
             cw_ref, lg_ref, lb_ref, dp_ref, sg_ref, gst, dcbuf, ubuf, dubuf, wbuf, dwacc, vacc, dsh, ush):
        _zero_at_start(gst)
        _conv_bwd_part(nS, dac_ref, y_ref, p_ref, ph_ref, cw_ref, lg_ref, lb_ref, dp_ref,
                       dcbuf, ubuf, dubuf, wbuf, dwacc, vacc, dsh, ush)
        _ret_bwd_part(dar_ref, p_ref, st_ref, rope_ref, m_ref, qdm_ref, kdm_ref, mqs_ref, cd_ref, dp_ref, gst)
        _conv_bwd_final(nS, sg_ref, dwacc, vacc)

    rev = lambda s: nS - 1 - s
    specs = _ret_specs(T, L)
    specs[0] = pl.BlockSpec((3, L, 128), lambda s: (0, rev(s), 0))
    vec = pl.BlockSpec((None, 1, C), lambda s: (l, 0, 0))
    tile = pl.BlockSpec((L, C), lambda s: (rev(s), 0))
    ptile = pl.BlockSpec((L, PW + 3 * C), lambda s: (rev(s), 0))
    halo = pl.BlockSpec((HALO, 3 * C), lambda s: (jnp.maximum(rev(s) * hb - 1, 0), 1))
    return _pcall(
        body, name="branch_bwd",
        out_shape=(jax.ShapeDtypeStruct(dproj.shape, BF16), jax.ShapeDtypeStruct((ROW_PRE, C), F32)),
        grid=(nS,),
        in_specs=[ANY, pl.BlockSpec((L, VW), lambda s: (rev(s), 0)), tile, tile, ptile, halo,
                  pl.BlockSpec((None, HEADS, 128, DV), lambda s: (rev(s), 0, 0, 0))] + specs
                 + [pl.BlockSpec((4, None, CONV_K, Cc), lambda s: (0, l, 0, 0)), vec, vec],
        out_specs=(ptile, pl.BlockSpec((ROW_PRE, C), lambda s: (0, 0))),
        scratch_shapes=[pltpu.VMEM((HEADS, 128, DV), F32),
                        pltpu.VMEM((L + HALO, C), F32), pltpu.VMEM((HALO + L, C), F32), pltpu.VMEM((L, C), F32),
                        pltpu.VMEM((HALO, C), F32), pltpu.VMEM((8 * CONV_K, C), F32), pltpu.VMEM((8, C), F32),
                        pltpu.VMEM((7, HALO + L - 8, C), F32), pltpu.VMEM((7, HALO + L - 8, C), F32)],
        sem=("arbitrary",), aliases={0: 0}, ride=ride,
        args=(dproj, da_ret, da_conv, y, proj, proj, states, tb["rope"], tb["mask"], tb["qdm"], tb["kdm"], tb["mqs"],
              tb["cd"], wcw, ln_g, ln_b))


def _win_grad(h, dproj, W):
    T, D = h.shape
    tk = min(2048, T)
    nk = T // tk

    def body(h_ref, dp_ref, g_ref, acc):
        k = pl.program_id(1)

        @pl.when(k == 0)
        def _():
            acc[...] = jnp.zeros_like(acc)

        acc[...] += _dot_tn(h_ref[...], dp_ref[...])

        @pl.when(k == nk - 1)
        def _():
            g_ref[...] = acc[...].astype(BF16)

    return pl.pallas_call(
        body, name="win_grad",
        out_shape=jax.ShapeDtypeStruct((4, D, W), BF16),
        grid=(4, nk),
        in_specs=[pl.BlockSpec((tk, D), lambda j, k: (k, 0)), pl.BlockSpec((tk, W), lambda j, k: (k, j))],
        out_specs=pl.BlockSpec((None, D, W), lambda j, k: (j, 0, 0)),
        scratch_shapes=[pltpu.VMEM((D, W), F32)],
        compiler_params=_cp(("arbitrary", "arbitrary")),
    )(h, dproj)


def _in_bwd(dxn, dproj, x, pre_g, win, l, ride=None):
    T, D = x.shape
    W = win.shape[-1]
    tm = min(1024, T)

    def body(dxn_ref, dp_ref, x_ref, g_ref, w_ref, dx_ref, dg_ref, acc):
        i = pl.program_id(0)
        j = pl.program_id(1)

        @pl.when(j == 0)
        def _():
            acc[...] = jnp.zeros_like(acc)

        @pl.when((i == 0) & (j == 0))
        def _():
            dg_ref[...] = jnp.zeros_like(dg_ref)

        acc[...] += _dot_nt(dp_ref[...], w_ref[...])

        @pl.when(j == 3)
        def _():
            _, vjp = jax.vjp(_rms, x_ref[...], g_ref[...])
            dx, dg = vjp(acc[...])
            dx_ref[...] = dxn_ref[...] + dx
            dg_ref[0:1, :] += dg

    tile = pl.BlockSpec((tm, D), lambda i, j: (i, 0))
    return _pcall(
        body, name="in_bwd",
        out_shape=(jax.ShapeDtypeStruct((T, D), F32), jax.ShapeDtypeStruct((8, D), F32)),
        grid=(T // tm, 4),
        in_specs=[tile, pl.BlockSpec((tm, W), lambda i, j: (i, j)), tile,
                  pl.BlockSpec((None, 1, D), lambda i, j: (l, 0, 0)),
                  pl.BlockSpec((None, D, W), lambda i, j: (j, 0, 0))],
        out_specs=(tile, pl.BlockSpec((8, D), lambda i, j: (0, 0))),
        scratch_shapes=[pltpu.VMEM((tm, D), F32)],
        sem=("arbitrary", "arbitrary"), args=(dxn, dproj, x, pre_g, win), ride=ride)


def _sum_group(chip, t, u):
    _, A, B = t.shape
    tr = min(256, A)

    def body(k_ref, t_ref, u_ref, o_ref):
        o_ref[...] = ((t_ref[...].astype(F32) + u_ref[0].astype(F32)) + u_ref[1].astype(F32)) + u_ref[2].astype(F32)

    return pl.pallas_call(
        body, name="sum_group",
        out_shape=jax.ShapeDtypeStruct((A, B), F32),
        grid_spec=pltpu.PrefetchScalarGridSpec(
            num_scalar_prefetch=1, grid=(A // tr,),
            in_specs=[pl.BlockSpec((None, tr, B), lambda i, k: (k[0], i, 0)),
                      pl.BlockSpec((3, tr, B), lambda i, k: (0, i, 0))],
            out_specs=pl.BlockSpec((tr, B), lambda i, k: (i, 0))),
        compiler_params=_cp(("arbitrary",)),
    )(chip, t, u)


def _adam_math(w, g, m, v):
    c1 = 1.0 / (1.0 - ADAM_B1 ** ADAM_STEP)
    c2 = 1.0 / (1.0 - ADAM_B2 ** ADAM_STEP)
    nm = ADAM_B1 * m + (1.0 - ADAM_B1) * g
    nv = ADAM_B2 * v + (1.0 - ADAM_B2) * (g * g)
    return -ADAM_LR * ((nm * c1) / (jnp.sqrt(nv * c2) + ADAM_EPS) + ADAM_WD * w), nm, nv


def _adamw_layer(prev, w, m, v, sa, sb, l, part, ride=None):
    NL, A, B = w.shape
    tr = A
    while tr * B * 4 > ADAM_BLOCK_BYTES and tr % 16 == 0:
        tr //= 2
    nb = A // tr

    def body(p0, p1, p2, p3, w_ref, m_ref, v_ref, sa_ref, sb_ref, g_ref, d_ref, nm_ref, nv_ref):
        g = sa_ref[...] + sb_ref[...]
        g_ref[...] = g
        d_ref[...], nm_ref[...], nv_ref[...] = _adam_math(w_ref[...], g, m_ref[...], v_ref[...])

    lay = pl.BlockSpec((None, tr, B), lambda i: (l, i, 0))
    src = pl.BlockSpec((tr, B), lambda i: (part * nb + i, 0))
    full = jax.ShapeDtypeStruct((NL, A, B), F32)
    if prev is None:
        prev = tuple(lax.empty((NL, A, B), F32) for _ in range(4))
    outs, landed = _pcall(
        body, name="adamw_layer",
        out_shape=(full,) * 4, grid=(nb,),
        in_specs=[ANY] * 4 + [lay, lay, lay, src, src], out_specs=(lay,) * 4,
        sem=("arbitrary",), aliases={0: 0, 1: 1, 2: 2, 3: 3}, args=(*prev, w, m, v, sa, sb), ride=ride)
    return tuple(outs), landed


def _adamw(w, g, m, v):
    shape = w.shape
    cols = shape[-1]
    rows = int(np.prod(shape[:-1]))

    def body(w_ref, g_ref, m_ref, v_ref, d_ref, nm_ref, nv_ref):
        d_ref[...], nm_ref[...], nv_ref[...] = _adam_math(w_ref[...], g_ref[...], m_ref[...], v_ref[...])

    tile = pl.BlockSpec((rows, cols), lambda i: (0, 0))
    out = jax.ShapeDtypeStruct((rows, cols), F32)
    res = pl.pallas_call(
        body, name="adamw",
        out_shape=(out, out, out), grid=(1,),
        in_specs=[tile] * 4, out_specs=(tile,) * 3,
        compiler_params=_cp(("arbitrary",)),
    )(*[a.reshape(rows, cols) for a in (w, g, m, v)])
    return tuple(a.reshape(shape) for a in res)


def _tail_exchange(small, s_in, s_sq):
    def body(s_ref, a_ref, b_ref, o_ref, oa_ref, ob_ref, send_sems, recv_sems, local_sem):
        x, y, c = _place()
        me = 4 * x + 2 * y + c
        sibling = (x, y, 1 - c)
        mine = pltpu.make_async_copy(s_ref, o_ref.at[me], local_sem)
        mine.start()
        swaps = [_remote(a_ref, oa_ref, send_sems, recv_sems, 7, sibling), _remote(b_ref, ob_ref, send_sems, recv_sems, 8, sibling)]
        sends = [_remote(s_ref, o_ref.at[me], send_sems, recv_sems, r, peer) for r, peer in enumerate(_peers(x, y, c))]
        for cp in swaps + sends:
            cp.start()
        for r, peer in enumerate(_peers(x, y, c)):
            theirs = o_ref.at[4 * peer[0] + 2 * peer[1] + peer[2]]
            _remote(theirs, theirs, send_sems, recv_sems, r, peer).wait_recv()
        for cp in sends:
            cp.wait_send()
        for cp in swaps:
            cp.wait()
        mine.wait()

    return pl.pallas_call(
        body, name="tail_exchange",
        out_shape=(jax.ShapeDtypeStruct((8,) + small.shape, small.dtype),
                   jax.ShapeDtypeStruct(s_in.shape, s_in.dtype), jax.ShapeDtypeStruct(s_sq.shape, s_sq.dtype)),
        in_specs=[ANY] * 3, out_specs=(ANY,) * 3,
        scratch_shapes=[pltpu.SemaphoreType.DMA((9,)), pltpu.SemaphoreType.DMA((9,)), pltpu.SemaphoreType.DMA],
        compiler_params=pltpu.CompilerParams(has_side_effects=True),
    )(small, s_in, s_sq)


def _sum_devices(gs):
    NL = len(gs)
    _, R, D = gs[0].shape

    def body(*refs):
        o_ref = refs[NL]
        for l in range(NL):
            acc = refs[l][0]
            for k in range(1, 8):
                acc = acc + refs[l][k]
            o_ref[l] = acc

    return pl.pallas_call(
        body, name="sum_devices",
        out_shape=jax.ShapeDtypeStruct((NL, R, D), F32),
        grid=(1,),
        in_specs=[pl.BlockSpec((8, R, D), lambda i: (0, 0, 0))] * NL,
        out_specs=pl.BlockSpec((NL, R, D), lambda i: (0, 0, 0)),
        compiler_params=_cp(("arbitrary",)),
    )(*gs)


def kernel(x, pre_norm_g, w_in, w_ret_out, conv_w, conv_b, conv_ln_g, conv_ln_b, w_conv_out, w_o, post_norm_g, loss_target, m_pre_norm_g, m_w_in, m_w_ret_out, m_conv_w, m_conv_b, m_conv_ln_g, m_conv_ln_b, m_w_conv_out, m_w_o, m_post_norm_g, v_pre_norm_g, v_w_in, v_w_ret_out, v_conv_w, v_conv_b, v_conv_ln_g, v_conv_ln_b, v_w_conv_out, v_w_o, v_post_norm_g):
    NL, D, W = w_in.shape
    Cc = conv_w.shape[-1]
    T = x.shape[1]
    L = min(256, T)
    tb = _tables(T, L)
    ax, ay, _ = _place()
    chip = (2 * ax + ay).astype(jnp.int32).reshape(1)
    pre_g, cb, lg, lb, post_g = (a.reshape(NL, 1, D) for a in (pre_norm_g, conv_b, conv_ln_g, conv_ln_b, post_norm_g))

    win = [_cast_win(chip, w_in, l) for l in range(NL)]
    wsq = [_cast_wsq(chip, w_ret_out, w_conv_out, w_o, l) for l in range(NL)]
    win[0], wcw = _gather_first(win[0], _place_cw(chip, conv_w))

    saved = []
    xl = x[0]
    for l in range(NL):
        more = l + 1 < NL
        ride = _Ride()
        if more:
            _ride_gather_ici(ride, "win", win[l + 1])
        if l == 0:
            _ride_gather_ici(ride, "wsq0", wsq[0])
        (proj, h), got = _fwd_in(xl, pre_g, win[l], l, ride=ride)
        if more:
            win[l + 1] = got["win"]
        if l == 0:
            wsq[0] = got["wsq0"]
        ride = _Ride()
        if more:
            _ride_gather_ici(ride, "wsq", wsq[l + 1])
        if l == 0:
            _ride_gather_pass(ride, "wsq0", wsq[0])
        (a_ret, states, a_conv, y), got = _branch_fwd(proj, tb, L, wcw, cb, lg, lb, l, ride=ride)
        if more:
            wsq[l + 1] = got["wsq"]
        if l == 0:
            wsq[0] = got["wsq0"]
        ride = _Ride()
        if more:
            _ride_gather_pass(ride, "win", win[l + 1])
            _ride_gather_pass(ride, "wsq", wsq[l + 1])
        (xn, ro, co, ym, z), got = _merge_fwd(xl, proj, a_ret, a_conv, wsq[l], post_g, l, ride=ride)
        if more:
            win[l + 1], wsq[l + 1] = got["win"], got["wsq"]
        saved.append((xl, proj, h, a_ret, states, a_conv, y, ro, co, ym, z))
        xl = xn
    dx, lsum = _loss_fwd_bwd(xl, loss_target[0])
    loss = lax.psum(jnp.sum(lsum), ("x", "y", "c"))

    gin, gsq, uin, usq = [None] * NL, [None] * NL, [None] * NL, [None] * NL
    s_in, s_sq, o_in, o_sq = [None] * NL, [None] * NL, [None] * NL, [None] * NL
    small, gs = [None] * NL, [None] * NL
    for l in reversed(range(NL)):
        xin, proj, h, a_ret, states, a_conv, y, ro, co, ym, z = saved[l]
        (dproj, da_ret, da_conv, gsq[l], dpost), _ = _merge_bwd(dx, proj, a_ret, a_conv, ro, co, ym, z, wsq[l], post_g, l)
        ride = _Ride()
        _ride_exchange(ride, "gsq", gsq[l], "usq", None, (0, 1, 2))
        if l + 1 < NL:
            _ride_exchange(ride, "gin", gin[l + 1], "uin", uin[l + 1], (2,))
            _ride_gather_all(ride, "small", small[l + 1], "gs")
        (dproj, sg), got = _branch_bwd(dproj, da_ret, da_conv, y, proj, states, tb, L, wcw, lg, lb, l, ride=ride)
        usq[l] = got["usq"]
        gin[l] = _win_grad(h, dproj, W)
        ride = _Ride()
        _ride_exchange(ride, "gin", gin[l], "uin", None, (0, 1) if l > 0 else (0, 1, 2))
        if l + 1 < NL:
            uin[l + 1], gs[l + 1] = got["uin"], got["gs"]
            s_in[l + 1] = _sum_group(chip, gin[l + 1], uin[l + 1])
            s_sq[l + 1] = _sum_group(chip, gsq[l + 1], usq[l + 1])
            _ride_swap(ride, "s_in", s_in[l + 1], "o_in")
            _ride_swap(ride, "s_sq", s_sq[l + 1], "o_sq")
        (dx, dpre), got = _in_bwd(dx, dproj, xin, pre_g, win[l], l, ride=ride)
        uin[l] = got["uin"]
        if l + 1 < NL:
            o_in[l + 1], o_sq[l + 1] = got["o_in"], got["o_sq"]
        small[l] = jnp.concatenate([sg, dpre, dpost], axis=0)
    grad_x = dx

    big = {"w_in": None, "w_ret_out": None, "w_conv_out": None, "w_o": None}
    wts = {"w_in": (w_in, m_w_in, v_w_in), "w_ret_out": (w_ret_out, m_w_ret_out, v_w_ret_out),
           "w_conv_out": (w_conv_out, m_w_conv_out, v_w_conv_out), "w_o": (w_o, m_w_o, v_w_o)}
    sq_names = ("w_ret_out", "w_conv_out", "w_o")

    def adam_in(l, ride=None):
        big["w_in"], got = _adamw_layer(big["w_in"], *wts["w_in"], s_in[l], o_in[l], l, 0, ride=ride)
        return got

    def adam_sq(l, part, ride=None):
        n = sq_names[part]
        big[n], got = _adamw_layer(big[n], *wts[n], s_sq[l], o_sq[l], l, part, ride=ride)
        return got

    s_in[0] = _sum_group(chip, gin[0], uin[0])
    s_sq[0] = _sum_group(chip, gsq[0], usq[0])
    gs[0], o_in[0], o_sq[0] = _tail_exchange(small[0], s_in[0], s_sq[0])
    for l in reversed(range(NL)):
        adam_in(l)
        for part in range(3):
            adam_sq(l, part)

    gsm = _sum_devices(gs)
    grads = {
        "pre_norm_g": gsm[:, ROW_PRE], "conv_w": lax.dynamic_slice_in_dim(gsm[:, 0:CONV_K], chip[0] * Cc, Cc, axis=2),
        "conv_b": gsm[:, ROW_CB], "conv_ln_g": gsm[:, ROW_LG], "conv_ln_b": gsm[:, ROW_LB], "post_norm_g": gsm[:, ROW_POST],
    }
    weights = dict(pre_norm_g=pre_norm_g, conv_w=conv_w, conv_b=conv_b, conv_ln_g=conv_ln_g, conv_ln_b=conv_ln_b,
                   post_norm_g=post_norm_g)
    m1 = dict(pre_norm_g=m_pre_norm_g, conv_w=m_conv_w, conv_b=m_conv_b, conv_ln_g=m_conv_ln_g, conv_ln_b=m_conv_ln_b,
              post_norm_g=m_post_norm_g)
    m2 = dict(pre_norm_g=v_pre_norm_g, conv_w=v_conv_w, conv_b=v_conv_b, conv_ln_g=v_conv_ln_g, conv_ln_b=v_conv_ln_b,
              post_norm_g=v_post_norm_g)
    res = {n: (grads[n],) + _adamw(weights[n], grads[n], m1[n], m2[n]) for n in grads}
    res.update(big)
    order = ["pre_norm_g", "w_in", "w_ret_out", "conv_w", "conv_b", "conv_ln_g", "conv_ln_b", "w_conv_out", "w_o", "post_norm_g"]
    return (loss, grad_x[None], *[res[n][0] for n in order], *[res[n][1] for n in order],
            *[res[n][2] for n in order], *[res[n][3] for n in order])
```

```python
import numpy as np
import jax
import jax.numpy as jnp
from jax import lax
from jax.experimental import pallas as pl
from jax.experimental.pallas import tpu as pltpu

F32 = jnp.float32
BF16 = jnp.bfloat16

HEADS = 8
DK = 64
DV = 128
CONV_K = 31
CHUNK = 64
ROPE_BASE = 10000.0
EPS = 1e-6
HALO = 32
CONV_RB = 32

ADAM_LR = 0.001
ADAM_B1 = 0.9
ADAM_B2 = 0.999
ADAM_EPS = 1e-08
ADAM_WD = 0.01
ADAM_STEP = 10
ADAM_BLOCK_BYTES = 2 * 1024 * 1024

VMEM_LIMIT = 56 * 1024 * 1024
MESH_T = pl.DeviceIdType.MESH
ANY = pl.BlockSpec(memory_space=pl.ANY)

ROW_CB, ROW_LG, ROW_LB = 32, 33, 34
ROW_PRE, ROW_POST = 40, 48


def _cp(sem=None, **kw):
    return pltpu.CompilerParams(dimension_semantics=sem, vmem_limit_bytes=VMEM_LIMIT, **kw)


def _dot(a, b):
    return jnp.dot(a, b, preferred_element_type=F32)


def _dot_nt(a, b):
    return lax.dot_general(a, b, (((1,), (1,)), ((), ())), preferred_element_type=F32)


def _dot_tn(a, b):
    return lax.dot_general(a, b, (((0,), (0,)), ((), ())), preferred_element_type=F32)


def _sigmoid(x):
    return jax.nn.sigmoid(x)


def _silu(x):
    return x * _sigmoid(x)


def _rms(x, g):
    return x * lax.rsqrt(jnp.mean(x * x, axis=-1, keepdims=True) + EPS) * g


def _gn_gate(o, g):
    mu = jnp.mean(o, axis=-1, keepdims=True)
    d = o - mu
    var = jnp.mean(d * d, axis=-1, keepdims=True)
    return d * lax.rsqrt(var + EPS) * _silu(g)


def _ln_gate(y, gc, lg, lb):
    mu = jnp.mean(y, axis=-1, keepdims=True)
    d = y - mu
    var = jnp.mean(d * d, axis=-1, keepdims=True)
    return _silu(d * lax.rsqrt(var + EPS) * lg + lb) * _silu(gc)


def _tables(T, L):
    lane = np.arange(128)
    d = lane % DK
    half = DK // 2
    inv = (ROPE_BASE ** (-(np.arange(half, dtype=np.float32)) / half)).astype(np.float32)
    ang = (np.arange(T, dtype=np.float32)[:, None] * inv[None, :]).astype(np.float64)
    angl = ang[:, d % half]
    cos = np.cos(angl)
    sin = np.sin(angl)
    lo = (d < half)[None, :]
    rope = np.stack([cos, np.where(lo, -sin, 0.0), np.where(lo, 0.0, sin)]).astype(np.float32)

    hh = np.arange(HEADS, dtype=np.float64)
    log_g = np.log1p(-np.exp2(-5.0 - hh))
    n = np.arange(L, dtype=np.float64)
    cn = np.arange(L) // CHUNK
    allowed = (cn[None, :] <= cn[:, None])
    dist = np.abs(n[:, None] - n[None, :])
    mask = np.exp(log_g[:, None, None] * dist[None]) * allowed[None]
    mq = ((lane[None, :] // DK) == (np.arange(HEADS)[:, None] % 2)).astype(np.float64)
    qd = np.exp(log_g[:, None] * n[None, :])
    kd = np.exp(log_g[:, None] * (L - n[None, :]))
    qdm = qd[:, :, None] * mq[:, None, :] * (DK ** -0.5)
    kdm = kd[:, :, None] * mq[:, None, :]
    mqs = np.broadcast_to((mq * (DK ** -0.5))[:, None, :], (HEADS, 8, 128))
    cd = np.broadcast_to(np.exp(log_g * L)[:, None, None], (HEADS, 8, 128))
    f = lambda a: jnp.asarray(np.ascontiguousarray(a), dtype=F32)
    return dict(rope=f(rope), mask=f(mask), qdm=f(qdm), kdm=f(kdm), mqs=f(mqs), cd=f(cd))


def _rot(b, c, sl, sh):
    return b * c + pltpu.roll(b, 96, axis=1) * sl + pltpu.roll(b, 32, axis=1) * sh


def _rot_t(d, c, sl, sh):
    return d * c + pltpu.roll(d * sl, 32, axis=1) + pltpu.roll(d * sh, 96, axis=1)


def _place():
    return lax.axis_index("x"), lax.axis_index("y"), lax.axis_index("c")


def _other_chips(x, y):
    return [(1 - x, y), (x, 1 - y), (1 - x, 1 - y)]


def _remote(src, dst, send_sems, recv_sems, k, to):
    return pltpu.make_async_remote_copy(src_ref=src, dst_ref=dst, send_sem=send_sems.at[k], recv_sem=recv_sems.at[k],
                                        device_id=to, device_id_type=MESH_T)


class _Ride:
    def __init__(self):
        self.arrays, self.kinds, self.names = [], [], []
        self.fresh = []
        self.ops = []

    def read(self, name, a):
        self.names.append(name)
        self.arrays.append(a)
        self.kinds.append("in")

    def inout(self, name, a):
        self.names.append(name)
        self.arrays.append(a)
        self.kinds.append("inout")

    def land(self, name, shape, dtype):
        self.fresh.append((name, jax.ShapeDtypeStruct(shape, dtype)))

    def op(self, n_sems, start, finish):
        self.ops.append((n_sems, start, finish))


def _pcall(body, *, name, grid, in_specs, out_specs, out_shape, args, scratch_shapes=(), sem, aliases=None, ride=None):
    if ride is None or not ride.ops:
        outs = pl.pallas_call(body, name=name, grid=grid, in_specs=list(in_specs), out_specs=tuple(out_specs),
                              out_shape=tuple(out_shape), scratch_shapes=list(scratch_shapes),
                              input_output_aliases=dict(aliases or {}), compiler_params=_cp(sem))(*args)
        return outs, {}

    ni, no, nr = len(args), len(out_shape), len(ride.arrays)
    inout = [i for i, k in enumerate(ride.kinds) if k == "inout"]
    r_out_shapes = [jax.ShapeDtypeStruct(ride.arrays[i].shape, ride.arrays[i].dtype) for i in inout] + [s for _, s in ride.fresh]
    r_out_names = [ride.names[i] for i in inout] + [n for n, _ in ride.fresh]
    nro = len(r_out_shapes)
    n_sems = sum(n for n, _, _ in ride.ops)
    n_scr = len(scratch_shapes)
    nd = len(grid)

    def wrapped(*refs):
        ins, rin = refs[:ni], refs[ni:ni + nr]
        outs, rout = refs[ni + nr:ni + nr + no], refs[ni + nr + no:ni + nr + no + nro]
        scr = refs[ni + nr + no + nro:ni + nr + no + nro + n_scr]
        send_sems, recv_sems = refs[-2], refs[-1]
        view = {nm: r for nm, r, k in zip(ride.names, rin, ride.kinds) if k == "in"}
        view.update(dict(zip(r_out_names, rout)))
        first = pl.program_id(0) == 0
        last = pl.program_id(0) == grid[0] - 1
        for d in range(1, nd):
            first = first & (pl.program_id(d) == 0)
            last = last & (pl.program_id(d) == grid[d] - 1)

        @pl.when(first)
        def _():
            base = 0
            for n, start, _ in ride.ops:
                start(view, send_sems, recv_sems, base)
                base += n

        body(*ins, *outs, *scr)

        @pl.when(last)
        def _():
            base = 0
            for n, _, finish in ride.ops:
                finish(view, send_sems, recv_sems, base)
                base += n

    res = pl.pallas_call(
        wrapped, name=name, grid=grid,
        in_specs=list(in_specs) + [ANY] * nr, out_specs=tuple(out_specs) + (ANY,) * nro,
        out_shape=tuple(out_shape) + tuple(r_out_shapes),
        scratch_shapes=list(scratch_shapes) + [pltpu.SemaphoreType.DMA((n_sems,)), pltpu.SemaphoreType.DMA((n_sems,))],
        input_output_aliases={**dict(aliases or {}), **{ni + i: no + j for j, i in enumerate(inout)}},
        compiler_params=_cp(sem),
    )(*args, *ride.arrays)
    return res[:no], dict(zip(r_out_names, res[no:]))


def _half(ref, chip_idx, cc, part=(0, 1, 1)):
    n = ref.shape[1] // 2
    lo, hi, k = part
    return ref.at[chip_idx, pl.ds(cc * n + lo * n // k, (hi - lo) * n // k)]


def _ride_gather_ici(ride, name, a, part=(0, 1, 1)):
    ride.inout(name, a)

    def start(view, ss, rs, b):
        x, y, c = _place()
        mine = _half(view[name], 2 * x + y, c, part)
        for j, (cx, cy) in enumerate(_other_chips(x, y)):
            _remote(mine, mine, ss, rs, b + j, (cx, cy, c)).start()

    def finish(view, ss, rs, b):
        x, y, c = _place()
        mine = _half(view[name], 2 * x + y, c, part)
        for j, (cx, cy) in enumerate(_other_chips(x, y)):
            theirs = _half(view[name], 2 * cx + cy, c, part)
            _remote(theirs, theirs, ss, rs, b + j, (cx, cy, c)).wait_recv()
        for j, (cx, cy) in enumerate(_other_chips(x, y)):
            _remote(mine, mine, ss, rs, b + j, (cx, cy, c)).wait_send()

    ride.op(3, start, finish)


def _ride_gather_direct(ride, name, a):
    ride.inout(name, a)

    def start(view, ss, rs, b):
        x, y, c = _place()
        mine = _half(view[name], 2 * x + y, c)
        for j, (cx, cy) in enumerate(_other_chips(x, y)):
            _remote(mine, mine, ss, rs, b + 2 * j, (cx, cy, c)).start()
            _remote(mine, mine, ss, rs, b + 2 * j + 1, (cx, cy, 1 - c)).start()

    def finish(view, ss, rs, b):
        x, y, c = _place()
        mine = _half(view[name], 2 * x + y, c)
        for j, (cx, cy) in enumerate(_other_chips(x, y)):
            same = _half(view[name], 2 * cx + cy, c)
            other = _half(view[name], 2 * cx + cy, 1 - c)
            _remote(same, same, ss, rs, b + 2 * j, (cx, cy, c)).wait_recv()
            _remote(other, other, ss, rs, b + 2 * j + 1, (cx, cy, 1 - c)).wait_recv()
        for j, (cx, cy) in enumerate(_other_chips(x, y)):
            _remote(mine, mine, ss, rs, b + 2 * j, (cx, cy, c)).wait_send()
            _remote(mine, mine, ss, rs, b + 2 * j + 1, (cx, cy, 1 - c)).wait_send()

    ride.op(6, start, finish)


def _ride_gather_pass(ride, name, a):
    ride.inout(name, a)

    def start(view, ss, rs, b):
        x, y, c = _place()
        for j, (cx, cy) in enumerate(_other_chips(x, y)):
            blk = _half(view[name], 2 * cx + cy, c)
            _remote(blk, blk, ss, rs, b + j, (x, y, 1 - c)).start()

    def finish(view, ss, rs, b):
        x, y, c = _place()
        for j, (cx, cy) in enumerate(_other_chips(x, y)):
            theirs = _half(view[name], 2 * cx + cy, 1 - c)
            _remote(theirs, theirs, ss, rs, b + j, (x, y, 1 - c)).wait_recv()
        for j, (cx, cy) in enumerate(_other_chips(x, y)):
            blk = _half(view[name], 2 * cx + cy, c)
            _remote(blk, blk, ss, rs, b + j, (x, y, 1 - c)).wait_send()

    ride.op(3, start, finish)


def _ride_exchange(ride, src_name, src, dst_name, dst, rels):
    ride.read(src_name, src)
    if dst is None:
        ride.land(dst_name, (3,) + src.shape[1:], src.dtype)
    else:
        ride.inout(dst_name, dst)

    def start(view, ss, rs, b):
        x, y, c = _place()
        chips = _other_chips(x, y)
        for i, j in enumerate(rels):
            cx, cy = chips[j]
            _remote(view[src_name].at[2 * cx + cy], view[dst_name].at[j], ss, rs, b + i, (cx, cy, c)).start()

    def finish(view, ss, rs, b):
        x, y, c = _place()
        chips = _other_chips(x, y)
        for i, j in enumerate(rels):
            cx, cy = chips[j]
            _remote(view[src_name].at[2 * cx + cy], view[dst_name].at[j], ss, rs, b + i, (cx, cy, c)).wait()

    ride.op(len(rels), start, finish)


def _ride_swap(ride, src_name, src, dst_name):
    ride.read(src_name, src)
    ride.land(dst_name, src.shape, src.dtype)

    def start(view, ss, rs, b):
        x, y, c = _place()
        _remote(view[src_name], view[dst_name], ss, rs, b, (x, y, 1 - c)).start()

    def finish(view, ss, rs, b):
        x, y, c = _place()
        _remote(view[src_name], view[dst_name], ss, rs, b, (x, y, 1 - c)).wait()

    ride.op(1, start, finish)


def _peers(x, y, c):
    flip = lambda v, b: 1 - v if b else v
    return [(flip(x, r & 4), flip(y, r & 2), flip(c, r & 1)) for r in range(1, 8)]


def _ride_gather_all(ride, src_name, src, dst_name):
    ride.read(src_name, src)
    ride.land(dst_name, (8,) + src.shape, src.dtype)

    def start(view, ss, rs, b):
        x, y, c = _place()
        me = 4 * x + 2 * y + c
        pltpu.make_async_copy(view[src_name], view[dst_name].at[me], ss.at[b + 7]).start()
        for r, peer in enumerate(_peers(x, y, c)):
            _remote(view[src_name], view[dst_name].at[me], ss, rs, b + r, peer).start()

    def finish(view, ss, rs, b):
        x, y, c = _place()
        me = 4 * x + 2 * y + c
        for r, peer in enumerate(_peers(x, y, c)):
            theirs = view[dst_name].at[4 * peer[0] + 2 * peer[1] + peer[2]]
            _remote(theirs, theirs, ss, rs, b + r, peer).wait_recv()
        for r, peer in enumerate(_peers(x, y, c)):
            _remote(view[src_name], view[dst_name].at[me], ss, rs, b + r, peer).wait_send()
        pltpu.make_async_copy(view[src_name], view[dst_name].at[me], ss.at[b + 7]).wait()

    ride.op(8, start, finish)


def _cast_win(chip, w_in, l):
    _, D, W = w_in.shape
    tr = min(256, D)

    def body(chip_ref, w_ref, o_ref):
        o_ref[...] = w_ref[...].astype(BF16)

    return pl.pallas_call(
        body, name="cast_win",
        out_shape=jax.ShapeDtypeStruct((4, D, W), BF16),
        grid_spec=pltpu.PrefetchScalarGridSpec(
            num_scalar_prefetch=1, grid=(D // tr,),
            in_specs=[pl.BlockSpec((None, tr, W), lambda r, c: (l, r, 0))],
            out_specs=pl.BlockSpec((None, tr, W), lambda r, c: (c[0], r, 0))),
        compiler_params=_cp(("arbitrary",)),
    )(chip, w_in)


def _cast_wsq(chip, w_ro, w_co, w_o, l):
    _, R, D = w_ro.shape

    def body(chip_ref, a_ref, b_ref, c_ref, o_ref):
        o_ref[0:R, :] = a_ref[...].astype(BF16)
        o_ref[R:2 * R, :] = b_ref[...].astype(BF16)
        o_ref[2 * R:3 * R, :] = c_ref[...].astype(BF16)

    spec = pl.BlockSpec((None, R, D), lambda i, c: (l, 0, 0))
    return pl.pallas_call(
        body, name="cast_wsq",
        out_shape=jax.ShapeDtypeStruct((4, 3 * R, D), BF16),
        grid_spec=pltpu.PrefetchScalarGridSpec(
            num_scalar_prefetch=1, grid=(1,),
            in_specs=[spec, spec, spec],
            out_specs=pl.BlockSpec((None, 3 * R, D), lambda i, c: (c[0], 0, 0))),
        compiler_params=_cp(("arbitrary",)),
    )(chip, w_ro, w_co, w_o)


def _place_cw(chip, conv_w):
    NL, K, Cc = conv_w.shape

    def body(chip_ref, w_ref, o_ref):
        o_ref[...] = w_ref[...]

    return pl.pallas_call(
        body, name="place_cw",
        out_shape=jax.ShapeDtypeStruct((4, NL, K, Cc), F32),
        grid_spec=pltpu.PrefetchScalarGridSpec(
            num_scalar_prefetch=1, grid=(1,),
            in_specs=[pl.BlockSpec((NL, K, Cc), lambda i, c: (0, 0, 0))],
            out_specs=pl.BlockSpec((None, NL, K, Cc), lambda i, c: (c[0], 0, 0, 0))),
        compiler_params=_cp(("arbitrary",)),
    )(chip, conv_w)


def _gather_first(win0, wcw):
    n_arr = 2

    def body(a0, a1, o0, o1, send_sems, recv_sems):
        x, y, c = _place()
        sibling = (x, y, 1 - c)
        chips = _other_chips(x, y)
        outs = (o0, o1)

        def copy(k, a, cx, cy, cc, to):
            blk = _half(outs[a], 2 * cx + cy, cc)
            return _remote(blk, blk, send_sems, recv_sems, k, to)

        first = [copy(3 * a + j, a, x, y, c, (*chip, c)) for a in range(n_arr) for j, chip in enumerate(chips)]
        for cp in first:
            cp.start()
        passed = [copy(3 * n_arr + 3 * a + j, a, *chip, c, sibling) for a in range(n_arr) for j, chip in enumerate(chips)]
        for a in range(n_arr):
            for j, chip in enumerate(chips):
                copy(3 * a + j, a, *chip, c, sibling).wait_recv()
                passed[3 * a + j].start()
        for a in range(n_arr):
            for j, chip in enumerate(chips):
                copy(3 * n_arr + 3 * a + j, a, *chip, 1 - c, sibling).wait_recv()
        for cp in first + passed:
            cp.wait_send()

    ins = (win0, wcw)
    return pl.pallas_call(
        body, name="gather_first",
        out_shape=tuple(jax.ShapeDtypeStruct(a.shape, a.dtype) for a in ins),
        in_specs=[ANY] * n_arr, out_specs=(ANY,) * n_arr,
        scratch_shapes=[pltpu.SemaphoreType.DMA((6 * n_arr,)), pltpu.SemaphoreType.DMA((6 * n_arr,))],
        input_output_aliases={0: 0, 1: 1},
        compiler_params=pltpu.CompilerParams(has_side_effects=True),
    )(*ins)


def _fwd_in(x, pre_g, win, l, ride=None):
    T, D = x.shape
    W = win.shape[-1]
    tm = min(1024, T)

    def body(x_ref, g_ref, w_ref, p_ref, h_ref, hs):
        @pl.when(pl.program_id(1) == 0)
        def _():
            hb = _rms(x_ref[...], g_ref[...]).astype(BF16)
            hs[...] = hb
            h_ref[...] = hb

        p_ref[...] = _dot(hs[...], w_ref[...]).astype(BF16)

    return _pcall(
        body, name="fwd_in",
        out_shape=(jax.ShapeDtypeStruct((T, 4 * W), BF16), jax.ShapeDtypeStruct((T, D), BF16)),
        grid=(T // tm, 4),
        in_specs=[pl.BlockSpec((tm, D), lambda i, j: (i, 0)),
                  pl.BlockSpec((None, 1, D), lambda i, j: (l, 0, 0)),
                  pl.BlockSpec((None, D, W), lambda i, j: (j, 0, 0))],
        out_specs=(pl.BlockSpec((tm, W), lambda i, j: (i, j)),
                   pl.BlockSpec((tm, D), lambda i, j: (i, 0))),
        scratch_shapes=[pltpu.VMEM((tm, D), BF16)],
        sem=("arbitrary", "arbitrary"), args=(x, pre_g, win), ride=ride)


def _ret_specs(T, L):
    rope = pl.BlockSpec((3, L, 128), lambda s: (0, s, 0))
    mask = pl.BlockSpec((HEADS, L, L), lambda s: (0, 0, 0), pipeline_mode=pl.Buffered(1))
    qdm = pl.BlockSpec((HEADS, L, 128), lambda s: (0, 0, 0), pipeline_mode=pl.Buffered(1))
    small = pl.BlockSpec((HEADS, 8, 128), lambda s: (0, 0, 0))
    return [rope, mask, qdm, qdm, small, small]


QK = HEADS * DK
VW = HEADS * DV
PW = 2 * QK + 2 * VW


def _zero_at_start(ref):
    @pl.when(pl.program_id(0) == 0)
    def _():
        ref[...] = jnp.zeros_like(ref)


def _ret_fwd_part(p_ref, rope_ref, m_ref, qdm_ref, kdm_ref, mqs_ref, cd_ref, a_ref, st_ref, state):
    c, sl, sh = rope_ref[0], rope_ref[1], rope_ref[2]
    for j in range(HEADS // 2):
        rq = _rot(p_ref[:, 128 * j:128 * (j + 1)].astype(F32), c, sl, sh)
        rk = _rot(p_ref[:, QK + 128 * j:QK + 128 * (j + 1)].astype(F32), c, sl, sh)
        rkb = rk.astype(BF16)
        for e in range(2):
            h = 2 * j + e
            v = p_ref[:, 2 * QK + DV * h:2 * QK + DV * (h + 1)]
            g = p_ref[:, 2 * QK + VW + DV * h:2 * QK + VW + DV * (h + 1)].astype(F32)
            a = (rq * mqs_ref[h, 0:1, :]).astype(BF16)
            p = (_dot_nt(a, rkb) * m_ref[h]).astype(BF16)
            st = state[h]
            st_ref[h] = st
            o = _dot(p, v) + _dot((rq * qdm_ref[h]).astype(BF16), st.astype(BF16))
            state[h] = st * cd_ref[h, 0:1, :] + _dot_tn((rk * kdm_ref[h]).astype(BF16), v)
            a_ref[:, DV * h:DV * (h + 1)] = _gn_gate(o, g).astype(BF16)


def _shift_copies(src, sh):
    rows = sh.shape[1]
    for b in range(1, 8):
        sh[b - 1, :, :] = src[pl.ds(b, rows), :]


def _window(src, sh, start, rows):
    b = start % 8
    if b == 0:
        return src[pl.ds(start, rows), :]
    return sh[b - 1, pl.ds(start - b, rows), :]


def _conv_taps(wbuf, src, sh, base, rows):
    acc = wbuf[pl.ds(0, 1), :] * _window(src, sh, base, rows)
    for k in range(1, CONV_K):
        acc = acc + wbuf[pl.ds(k, 1), :] * _window(src, sh, base + k, rows)
    return acc


def _load_conv_w(cw_ref, wbuf, flip):
    for k in range(CONV_K):
        row = jnp.concatenate([cw_ref[c, pl.ds(k, 1), :] for c in range(4)], axis=-1)
        wbuf[pl.ds(CONV_K - 1 - k if flip else k, 1), :] = row
    wbuf[pl.ds(CONV_K, 1), :] = jnp.zeros_like(wbuf[pl.ds(CONV_K, 1), :])


def _conv_fwd_part(p_ref, cw_ref, cb_ref, lg_ref, lb_ref, a_ref, y_ref, ubuf, wbuf, ush):
    tc, C = y_ref.shape
    off = HALO - (CONV_K - 1)
    i = pl.program_id(0)

    @pl.when(i == 0)
    def _():
        ubuf[0:HALO, :] = jnp.zeros((HALO, C), F32)
        _load_conv_w(cw_ref, wbuf, False)

    @pl.when(i > 0)
    def _():
        ubuf[0:HALO, :] = ubuf[tc:tc + HALO, :]

    ga = p_ref[:, PW:PW + C].astype(F32)
    gb = p_ref[:, PW + C:PW + 2 * C].astype(F32)
    ubuf[HALO:HALO + tc, :] = ga * _sigmoid(gb)
    _shift_copies(ubuf, ush)
    for r in range(tc // CONV_RB):
        y_ref[r * CONV_RB:(r + 1) * CONV_RB, :] = _conv_taps(wbuf, ubuf, ush, r * CONV_RB + off, CONV_RB) + cb_ref[...]
    gc = p_ref[:, PW + 2 * C:PW + 3 * C].astype(F32)
    a_ref[...] = _ln_gate(y_ref[...], gc, lg_ref[...], lb_ref[...]).astype(BF16)


def _branch_fwd(proj, tb, L, wcw, conv_b, ln_g, ln_b, l, ride=None):
    T = proj.shape[0]
    C = conv_b.shape[-1]
    Cc = wcw.shape[-1]
    nS = T // L

    def body(p_ref, rope_ref, m_ref, qdm_ref, kdm_ref, mqs_ref, cd_ref, cw_ref, cb_ref, lg_ref, lb_ref,
             ar_ref, st_ref, ac_ref, y_ref, state, ubuf, wbuf, ush):
        _zero_at_start(state)
        _conv_fwd_part(p_ref, cw_ref, cb_ref, lg_ref, lb_ref, ac_ref, y_ref, ubuf, wbuf, ush)
        _ret_fwd_part(p_ref, rope_ref, m_ref, qdm_ref, kdm_ref, mqs_ref, cd_ref, ar_ref, st_ref, state)

    vec = pl.BlockSpec((None, 1, C), lambda s: (l, 0, 0))
    tile = pl.BlockSpec((L, C), lambda s: (s, 0))
    return _pcall(
        body, name="branch_fwd",
        out_shape=(jax.ShapeDtypeStruct((T, VW), BF16), jax.ShapeDtypeStruct((nS, HEADS, 128, DV), F32),
                   jax.ShapeDtypeStruct((T, C), BF16), jax.ShapeDtypeStruct((T, C), F32)),
        grid=(nS,),
        in_specs=[pl.BlockSpec((L, PW + 3 * C), lambda s: (s, 0))] + _ret_specs(T, L)
                 + [pl.BlockSpec((4, None, CONV_K, Cc), lambda s: (0, l, 0, 0)), vec, vec, vec],
        out_specs=(pl.BlockSpec((L, VW), lambda s: (s, 0)), pl.BlockSpec((None, HEADS, 128, DV), lambda s: (s, 0, 0, 0)),
                   tile, tile),
        scratch_shapes=[pltpu.VMEM((HEADS, 128, DV), F32), pltpu.VMEM((HALO + L, C), F32), pltpu.VMEM((HALO, C), F32),
                        pltpu.VMEM((7, HALO + L - 8, C), F32)],
        sem=("arbitrary",), ride=ride,
        args=(proj, tb["rope"], tb["mask"], tb["qdm"], tb["kdm"], tb["mqs"], tb["cd"], wcw, conv_b, ln_g, ln_b))


def _merge_fwd(x, proj, a_ret, a_conv, wsq, post_g, l, ride=None):
    T, D = x.shape
    R = wsq.shape[1] // 3
    tm = min(512, T)

    def body(x_ref, p_ref, ar_ref, ac_ref, wro_ref, wco_ref, wo_ref, g_ref, xn_ref, ro_ref, co_ref, ym_ref, z_ref):
        ro = _dot(ar_ref[...], wro_ref[...].reshape(4 * R, D))
        co = _dot(ac_ref[...], wco_ref[...].reshape(4 * R, D))
        ym = (_sigmoid(p_ref[:, 0:D].astype(F32)) * ro + _sigmoid(p_ref[:, D:2 * D].astype(F32)) * co).astype(BF16)
        z = _dot(ym, wo_ref[...].reshape(4 * R, D))
        ro_ref[...] = ro.astype(BF16)
        co_ref[...] = co.astype(BF16)
        ym_ref[...] = ym
        z_ref[...] = z.astype(BF16)
        xn_ref[...] = x_ref[...] + _rms(z, g_ref[...])

    tile = pl.BlockSpec((tm, D), lambda i: (i, 0))
    wspec = lambda m: pl.BlockSpec((4, R, D), lambda i: (0, m, 0))
    act = jax.ShapeDtypeStruct((T, D), BF16)
    return _pcall(
        body, name="merge_fwd",
        out_shape=(jax.ShapeDtypeStruct((T, D), F32), act, act, act, act),
        grid=(T // tm,),
        in_specs=[tile, pl.BlockSpec((tm, 2 * D), lambda i: (i, 3)), tile, tile,
                  wspec(0), wspec(1), wspec(2), pl.BlockSpec((None, 1, D), lambda i: (l, 0, 0))],
        out_specs=(tile, tile, tile, tile, tile),
        sem=("arbitrary",), ride=ride, args=(x, proj, a_ret, a_conv, wsq, wsq, wsq, post_g))


def _loss_fwd_bwd(y, target):
    T, D = y.shape
    tm = min(512, T)

    def body(y_ref, t_ref, dy_ref, ls_ref):
        @pl.when(pl.program_id(0) == 0)
        def _():
            ls_ref[...] = jnp.zeros_like(ls_ref)

        e = y_ref[...] - t_ref[...]
        dy_ref[...] = e * (1.0 / D)
        ls_ref[...] += jnp.sum((e * e).reshape(tm // 8, 8, D), axis=0) * (0.5 / D)

    tile = pl.BlockSpec((tm, D), lambda i: (i, 0))
    return pl.pallas_call(
        body, name="loss",
        out_shape=(jax.ShapeDtypeStruct((T, D), F32), jax.ShapeDtypeStruct((8, D), F32)),
        grid=(T // tm,),
        in_specs=[tile, tile],
        out_specs=(tile, pl.BlockSpec((8, D), lambda i: (0, 0))),
        compiler_params=_cp(("arbitrary",)),
    )(y, target)


def _merge_bwd(dxn, proj, a_ret, a_conv, ro, co, ym, z, wsq, post_g, l, ride=None):
    T, D = dxn.shape
    R = wsq.shape[1] // 3
    tm = min(512, T)
    n = T // tm

    def body(dx_ref, p_ref, ar_ref, ac_ref, ro_ref, co_ref, ym_ref, z_ref, wro_ref, wco_ref, wo_ref, g_ref,
             dp_ref, dar_ref, dac_ref, gsq_ref, dg_ref, acc, stage):
        i = pl.program_id(0)

        @pl.when(i == 0)
        def _():
            acc[...] = jnp.zeros_like(acc)
            dg_ref[...] = jnp.zeros_like(dg_ref)

        _, vjp = jax.vjp(_rms, z_ref[...].astype(F32), g_ref[...])
        dz, dg = vjp(dx_ref[...])
        dg_ref[0:1, :] += dg
        dzb = dz.astype(BF16)
        dym = _dot_nt(dzb, wo_ref[...].reshape(4 * R, D))
        acc[2] += _dot_tn(ym_ref[...], dzb)
        sr = _sigmoid(p_ref[:, 0:D].astype(F32))
        sc = _sigmoid(p_ref[:, D:2 * D].astype(F32))
        rov = ro_ref[...].astype(F32)
        cov = co_ref[...].astype(F32)
        dp_ref[:, 0:D] = (dym * rov * sr * (1.0 - sr)).astype(BF16)
        dp_ref[:, D:2 * D] = (dym * cov * sc * (1.0 - sc)).astype(BF16)
        dro = (dym * sr).astype(BF16)
        dco = (dym * sc).astype(BF16)
        dar_ref[...] = _dot_nt(dro, wro_ref[...].reshape(4 * R, D)).astype(BF16)
        dac_ref[...] = _dot_nt(dco, wco_ref[...].reshape(4 * R, D)).astype(BF16)
        acc[0] += _dot_tn(ar_ref[...], dro)
        acc[1] += _dot_tn(ac_ref[...], dco)

        @pl.when(i == n - 1)
        def _():
            for m in range(3):
                stage[...] = acc[m].astype(BF16).reshape(4, R, D)
                pltpu.sync_copy(stage, gsq_ref.at[:, pl.ds(m * R, R), :])

    tile = pl.BlockSpec((tm, D), lambda i: (i, 0))
    wspec = lambda m: pl.BlockSpec((4, R, D), lambda i: (0, m, 0), pipeline_mode=pl.Buffered(1))
    return _pcall(
        body, name="merge_bwd",
        out_shape=(jax.ShapeDtypeStruct(proj.shape, BF16), jax.ShapeDtypeStruct((T, D), BF16),
                   jax.ShapeDtypeStruct((T, D), BF16), jax.ShapeDtypeStruct(wsq.shape, BF16),
                   jax.ShapeDtypeStruct((8, D), F32)),
        grid=(n,),
        in_specs=[tile, pl.BlockSpec((tm, 2 * D), lambda i: (i, 3)), tile, tile, tile, tile, tile, tile,
                  wspec(0), wspec(1), wspec(2), pl.BlockSpec((None, 1, D), lambda i: (l, 0, 0))],
        out_specs=(pl.BlockSpec((tm, 2 * D), lambda i: (i, 3)), tile, tile, ANY, pl.BlockSpec((8, D), lambda i: (0, 0))),
        scratch_shapes=[pltpu.VMEM((3, 4 * R, D), F32), pltpu.VMEM((4, R, D), BF16)],
        sem=("arbitrary",), args=(dxn, proj, a_ret, a_conv, ro, co, ym, z, wsq, wsq, wsq, post_g), ride=ride)


def _conv_bwd_part(n, da_ref, y_ref, p_ref, ph_ref, cw_ref, lg_ref, lb_ref, dp_ref,
                   dcbuf, ubuf, dubuf, wbuf, dwacc, vacc, dsh, ush):
    tc, C = y_ref.shape
    off = HALO - (CONV_K - 1)
    nrb = tc // CONV_RB
    t = pl.program_id(0)
    i = n - 1 - t

    @pl.when(t == 0)
    def _():
        dcbuf[tc:tc + HALO, :] = jnp.zeros((HALO, C), F32)
        dwacc[...] = jnp.zeros_like(dwacc)
        vacc[...] = jnp.zeros_like(vacc)
        _load_conv_w(cw_ref, wbuf, True)

    @pl.when(t > 0)
    def _():
        dcbuf[tc:tc + HALO, :] = dcbuf[0:HALO, :]

    gc = p_ref[:, PW + 2 * C:PW + 3 * C].astype(F32)
    _, vjp = jax.vjp(_ln_gate, y_ref[...], gc, lg_ref[...], lb_ref[...])
    dy, dgc, dlg, dlb = vjp(da_ref[...].astype(F32))
    dcbuf[0:tc, :] = dy
    dp_ref[:, PW + 2 * C:PW + 3 * C] = dgc.astype(BF16)
    vacc[0:1, :] += jnp.sum(dy, axis=0, keepdims=True)
    vacc[1:2, :] += dlg
    vacc[2:3, :] += dlb

    ga = p_ref[:, PW:PW + C].astype(F32)
    sb = _sigmoid(p_ref[:, PW + C:PW + 2 * C].astype(F32))
    ubuf[HALO:HALO + tc, :] = ga * sb
    uh = ph_ref[:, 0:C].astype(F32) * _sigmoid(ph_ref[:, C:2 * C].astype(F32))
    ubuf[0:HALO, :] = jnp.where(i > 0, uh, 0.0)

    _shift_copies(dcbuf, dsh)
    _shift_copies(ubuf, ush)
    for r in range(nrb):
        dubuf[r * CONV_RB:(r + 1) * CONV_RB, :] = _conv_taps(wbuf, dcbuf, dsh, r * CONV_RB, CONV_RB)
    du = dubuf[...]
    dp_ref[:, PW:PW + C] = (du * sb).astype(BF16)
    dp_ref[:, PW + C:PW + 2 * C] = (du * ga * sb * (1.0 - sb)).astype(BF16)

    for r in range(nrb):
        dyb = dcbuf[r * CONV_RB:(r + 1) * CONV_RB, :]
        for k in range(CONV_K):
            pr = dyb * _window(ubuf, ush, r * CONV_RB + off + k, CONV_RB)
            dwacc[8 * k:8 * k + 8, :] += jnp.sum(pr.reshape(CONV_RB // 8, 8, C), axis=0)


def _conv_bwd_final(n, sg_ref, dwacc, vacc):
    C = sg_ref.shape[-1]

    @pl.when(pl.program_id(0) == n - 1)
    def _():
        for k in range(CONV_K):
            sg_ref[pl.ds(k, 1), :] = jnp.sum(dwacc[8 * k:8 * k + 8, :], axis=0, keepdims=True)
        sg_ref[pl.ds(CONV_K, 1), :] = jnp.zeros((1, C), F32)
        sg_ref[ROW_CB:ROW_CB + 8, :] = jnp.zeros((8, C), F32)
        sg_ref[ROW_CB:ROW_CB + 3, :] = vacc[0:3, :]


def _ret_bwd_part(da_ref, p_ref, st_ref, rope_ref, m_ref, qdm_ref, kdm_ref, mqs_ref, cd_ref, dp_ref, gst):
    c, sl, sh = rope_ref[0], rope_ref[1], rope_ref[2]
    for j in range(HEADS // 2):
        rq = _rot(p_ref[:, 128 * j:128 * (j + 1)].astype(F32), c, sl, sh)
        rk = _rot(p_ref[:, QK + 128 * j:QK + 128 * (j + 1)].astype(F32), c, sl, sh)
        rkb = rk.astype(BF16)
        drq = jnp.zeros_like(rq)
        drk = jnp.zeros_like(rk)
        for e in range(2):
            h = 2 * j + e
            v = p_ref[:, 2 * QK + DV * h:2 * QK + DV * (h + 1)]
            g = p_ref[:, 2 * QK + VW + DV * h:2 * QK + VW + DV * (h + 1)].astype(F32)
            mqs = mqs_ref[h, 0:1, :]
            a = (rq * mqs).astype(BF16)
            aq = (rq * qdm_ref[h]).astype(BF16)
            kdv = (rk * kdm_ref[h]).astype(BF16)
            mk = m_ref[h]
            p = (_dot_nt(a, rkb) * mk).astype(BF16)
            stb = st_ref[h].astype(BF16)
            o = _dot(p, v) + _dot(aq, stb)
            _, vjp = jax.vjp(_gn_gate, o, g)
            do, dg = vjp(da_ref[:, DV * h:DV * (h + 1)].astype(F32))
            dob = do.astype(BF16)
            gs = gst[h]
            gsb = gs.astype(BF16)
            ds = (_dot_nt(dob, v) * mk).astype(BF16)
            drq = drq + _dot(ds, rkb) * mqs + _dot_nt(dob, stb) * qdm_ref[h]
            drk = drk + _dot_tn(ds, a) + _dot_nt(v, gsb) * kdm_ref[h]
            dv = _dot_tn(p, dob) + _dot(kdv, gsb)
            gst[h] = _dot_tn(aq, dob) + gs * cd_ref[h, 0:1, :]
            dp_ref[:, 2 * QK + DV * h:2 * QK + DV * (h + 1)] = dv.astype(BF16)
            dp_ref[:, 2 * QK + VW + DV * h:2 * QK + VW + DV * (h + 1)] = dg.astype(BF16)
        dp_ref[:, 128 * j:128 * (j + 1)] = _rot_t(drq, c, sl, sh).astype(BF16)
        dp_ref[:, QK + 128 * j:QK + 128 * (j + 1)] = _rot_t(drk, c, sl, sh).astype(BF16)


def _branch_bwd(dproj, da_ret, da_conv, y, proj, states, tb, L, wcw, ln_g, ln_b, l, ride=None):
    T, C = y.shape
    Cc = wcw.shape[-1]
    nS = T // L
    hb = L // HALO

    def body(dpin_ref, dar_ref, dac_ref, y_ref, p_ref, ph_ref, st_ref, rope_ref, m_ref, qdm_ref, kdm_ref, mqs_ref, cd_ref,
             cw_ref, lg_ref, lb_ref, dp_ref, sg_ref, gst, dcbuf, ubuf, dubuf, wbuf, dwacc, vacc, dsh, ush):
        _zero_at_start(gst)
        _conv_bwd_part(nS, dac_ref, y_ref, p_ref, ph_ref, cw_ref, lg_ref, lb_ref, dp_ref,
                       dcbuf, ubuf, dubuf, wbuf, dwacc, vacc, dsh, ush)
        _ret_bwd_part(dar_ref, p_ref, st_ref, rope_ref, m_ref, qdm_ref, kdm_ref, mqs_ref, cd_ref, dp_ref, gst)
        _conv_bwd_final(nS, sg_ref, dwacc, vacc)

    rev = lambda s: nS - 1 - s
    specs = _ret_specs(T, L)
    specs[0] = pl.BlockSpec((3, L, 128), lambda s: (0, rev(s), 0))
    vec = pl.BlockSpec((None, 1, C), lambda s: (l, 0, 0))
    tile = pl.BlockSpec((L, C), lambda s: (rev(s), 0))
    ptile = pl.BlockSpec((L, PW + 3 * C), lambda s: (rev(s), 0))
    halo = pl.BlockSpec((HALO, 3 * C), lambda s: (jnp.maximum(rev(s) * hb - 1, 0), 1))
    return _pcall(
        body, name="branch_bwd",
        out_shape=(jax.ShapeDtypeStruct(dproj.shape, BF16), jax.ShapeDtypeStruct((ROW_PRE, C), F32)),
        grid=(nS,),
        in_specs=[ANY, pl.BlockSpec((L, VW), lambda s: (rev(s), 0)), tile, tile, ptile, halo,
                  pl.BlockSpec((None, HEADS, 128, DV), lambda s: (rev(s), 0, 0, 0))] + specs
                 + [pl.BlockSpec((4, None, CONV_K, Cc), lambda s: (0, l, 0, 0)), vec, vec],
        out_specs=(ptile, pl.BlockSpec((ROW_PRE, C), lambda s: (0, 0))),
        scratch_shapes=[pltpu.VMEM((HEADS, 128, DV), F32),
                        pltpu.VMEM((L + HALO, C), F32), pltpu.VMEM((HALO + L, C), F32), pltpu.VMEM((L, C), F32),
                        pltpu.VMEM((HALO, C), F32), pltpu.VMEM((8 * CONV_K, C), F32), pltpu.VMEM((8, C), F32),
                        pltpu.VMEM((7, HALO + L - 8, C), F32), pltpu.VMEM((7, HALO + L - 8, C), F32)],
        sem=("arbitrary",), aliases={0: 0}, ride=ride,
        args=(dproj, da_ret, da_conv, y, proj, proj, states, tb["rope"], tb["mask"], tb["qdm"], tb["kdm"], tb["mqs"],
              tb["cd"], wcw, ln_g, ln_b))


def _win_grad(h, dproj, W):
    T, D = h.shape
    tk = min(2048, T)
    nk = T // tk

    def body(h_ref, dp_ref, g_ref, acc):
        k = pl.program_id(1)

        @pl.when(k == 0)
        def _():
            acc[...] = jnp.zeros_like(acc)

        acc[...] += _dot_tn(h_ref[...], dp_ref[...])

        @pl.when(k == nk - 1)
        def _():
            g_ref[...] = acc[...].astype(BF16)

    return pl.pallas_call(
        body, name="win_grad",
        out_shape=jax.ShapeDtypeStruct((4, D, W), BF16),
        grid=(4, nk),
        in_specs=[pl.BlockSpec((tk, D), lambda j, k: (k, 0)), pl.BlockSpec((tk, W), lambda j, k: (k, j))],
        out_specs=pl.BlockSpec((None, D, W), lambda j, k: (j, 0, 0)),
        scratch_shapes=[pltpu.VMEM((D, W), F32)],
        compiler_params=_cp(("arbitrary", "arbitrary")),
    )(h, dproj)


def _in_bwd(dxn, dproj, x, pre_g, win, l, ride=None):
    T, D = x.shape
    W = win.shape[-1]
    tm = min(1024, T)

    def body(dxn_ref, dp_ref, x_ref, g_ref, w_ref, dx_ref, dg_ref, acc):
        i = pl.program_id(0)
        j = pl.program_id(1)

        @pl.when(j == 0)
        def _():
            acc[...] = jnp.zeros_like(acc)

        @pl.when((i == 0) & (j == 0))
        def _():
            dg_ref[...] = jnp.zeros_like(dg_ref)

        acc[...] += _dot_nt(dp_ref[...], w_ref[...])

        @pl.when(j == 3)
        def _():
            _, vjp = jax.vjp(_rms, x_ref[...], g_ref[...])
            dx, dg = vjp(acc[...])
            dx_ref[...] = dxn_ref[...] + dx
            dg_ref[0:1, :] += dg

    tile = pl.BlockSpec((tm, D), lambda i, j: (i, 0))
    return _pcall(
        body, name="in_bwd",
        out_shape=(jax.ShapeDtypeStruct((T, D), F32), jax.ShapeDtypeStruct((8, D), F32)),
        grid=(T // tm, 4),
        in_specs=[tile, pl.BlockSpec((tm, W), lambda i, j: (i, j)), tile,
                  pl.BlockSpec((None, 1, D), lambda i, j: (l, 0, 0)),
                  pl.BlockSpec((None, D, W), lambda i, j: (j, 0, 0))],
        out_specs=(tile, pl.BlockSpec((8, D), lambda i, j: (0, 0))),
        scratch_shapes=[pltpu.VMEM((tm, D), F32)],
        sem=("arbitrary", "arbitrary"), args=(dxn, dproj, x, pre_g, win), ride=ride)


def _sum_group(chip, t, u):
    _, A, B = t.shape
    tr = min(256, A)

    def body(k_ref, t_ref, u_ref, o_ref):
        o_ref[...] = ((t_ref[...].astype(F32) + u_ref[0].astype(F32)) + u_ref[1].astype(F32)) + u_ref[2].astype(F32)

    return pl.pallas_call(
        body, name="sum_group",
        out_shape=jax.ShapeDtypeStruct((A, B), F32),
        grid_spec=pltpu.PrefetchScalarGridSpec(
            num_scalar_prefetch=1, grid=(A // tr,),
            in_specs=[pl.BlockSpec((None, tr, B), lambda i, k: (k[0], i, 0)),
                      pl.BlockSpec((3, tr, B), lambda i, k: (0, i, 0))],
            out_specs=pl.BlockSpec((tr, B), lambda i, k: (i, 0))),
        compiler_params=_cp(("arbitrary",)),
    )(chip, t, u)


def _swap_rows(g):
    _, A, B = g.shape
    nh = A // 2

    def body(g_ref, r_ref, send_sems, recv_sems):
        x, y, c = _place()
        cp = _remote(g_ref.at[:, pl.ds((1 - c) * nh, nh)], r_ref, send_sems, recv_sems, 0, (x, y, 1 - c))
        cp.start()
        cp.wait()

    return pl.pallas_call(
        body, name="swap_rows",
        out_shape=jax.ShapeDtypeStruct((4, nh, B), g.dtype),
        in_specs=[ANY], out_specs=ANY,
        scratch_shapes=[pltpu.SemaphoreType.DMA((1,)), pltpu.SemaphoreType.DMA((1,))],
        compiler_params=pltpu.CompilerParams(has_side_effects=True),
    )(g)


def _add_rows(cidx, g, r):
    _, nh, B = r.shape
    tr = min(256, nh)
    nb = nh // tr

    def body(c_ref, g_ref, r_ref, o_ref):
        o_ref[...] = (g_ref[...].astype(F32) + r_ref[...].astype(F32)).astype(BF16)

    blk = (None, tr, B)
    return pl.pallas_call(
        body, name="add_rows",
        out_shape=jax.ShapeDtypeStruct(r.shape, BF16),
        grid_spec=pltpu.PrefetchScalarGridSpec(
            num_scalar_prefetch=1, grid=(4, nb),
            in_specs=[pl.BlockSpec(blk, lambda k, i, c: (k, c[0] * nb + i, 0)),
                      pl.BlockSpec(blk, lambda k, i, c: (k, i, 0))],
            out_specs=pl.BlockSpec(blk, lambda k, i, c: (k, i, 0))),
        compiler_params=_cp(("arbitrary", "arbitrary")),
    )(cidx, g, r)


def _sum_group_half(chip, cidx, t, u):
    _, nh, B = t.shape
    tr = min(256, nh)
    nb = nh // tr

    def body(k_ref, c_ref, t_ref, u_ref, o_ref):
        mine = (pl.program_id(0) // nb) == c_ref[0]

        @pl.when(mine)
        def _():
            o_ref[...] = ((t_ref[...].astype(F32) + u_ref[0].astype(F32)) + u_ref[1].astype(F32)) + u_ref[2].astype(F32)

        @pl.when(jnp.logical_not(mine))
        def _():
            o_ref[...] = jnp.zeros_like(o_ref)

    own = lambda i, c: jnp.clip(i - c[0] * nb, 0, nb - 1)
    return pl.pallas_call(
        body, name="sum_group_half",
        out_shape=jax.ShapeDtypeStruct((2 * nh, B), F32),
        grid_spec=pltpu.PrefetchScalarGridSpec(
            num_scalar_prefetch=2, grid=(2 * nb,),
            in_specs=[pl.BlockSpec((None, tr, B), lambda i, k, c: (k[0], own(i, c), 0)),
                      pl.BlockSpec((3, tr, B), lambda i, k, c: (0, own(i, c), 0))],
            out_specs=pl.BlockSpec((tr, B), lambda i, k, c: (i, 0))),
        compiler_params=_cp(("arbitrary",)),
    )(chip, cidx, t, u)


def _adam_math(w, g, m, v):
    c1 = 1.0 / (1.0 - ADAM_B1 ** ADAM_STEP)
    c2 = 1.0 / (1.0 - ADAM_B2 ** ADAM_STEP)
    nm = ADAM_B1 * m + (1.0 - ADAM_B1) * g
    nv = ADAM_B2 * v + (1.0 - ADAM_B2) * (g * g)
    return -ADAM_LR * ((nm * c1) / (jnp.sqrt(nv * c2) + ADAM_EPS) + ADAM_WD * w), nm, nv


def _adamw_layer(prev, w, m, v, sa, sb, l, part, ride=None):
    NL, A, B = w.shape
    tr = A
    while tr * B * 4 > ADAM_BLOCK_BYTES and tr % 16 == 0:
        tr //= 2
    nb = A // tr

    def body(p0, p1, p2, p3, w_ref, m_ref, v_ref, sa_ref, sb_ref, g_ref, d_ref, nm_ref, nv_ref):
        g = sa_ref[...] + sb_ref[...]
        g_ref[...] = g
        d_ref[...], nm_ref[...], nv_ref[...] = _adam_math(w_ref[...], g, m_ref[...], v_ref[...])

    lay = pl.BlockSpec((None, tr, B), lambda i: (l, i, 0))
    src = pl.BlockSpec((tr, B), lambda i: (part * nb + i, 0))
    full = jax.ShapeDtypeStruct((NL, A, B), F32)
    if prev is None:
        prev = tuple(lax.empty((NL, A, B), F32) for _ in range(4))
    outs, landed = _pcall(
        body, name="adamw_layer",
        out_shape=(full,) * 4, grid=(nb,),
        in_specs=[ANY] * 4 + [lay, lay, lay, src, src], out_specs=(lay,) * 4,
        sem=("arbitrary",), aliases={0: 0, 1: 1, 2: 2, 3: 3}, args=(*prev, w, m, v, sa, sb), ride=ride)
    return tuple(outs), landed


def _adamw(w, g, m, v):
    shape = w.shape
    cols = shape[-1]
    rows = int(np.prod(shape[:-1]))

    def body(w_ref, g_ref, m_ref, v_ref, d_ref, nm_ref, nv_ref):
        d_ref[...], nm_ref[...], nv_ref[...] = _adam_math(w_ref[...], g_ref[...], m_ref[...], v_ref[...])

    tile = pl.BlockSpec((rows, cols), lambda i: (0, 0))
    out = jax.ShapeDtypeStruct((rows, cols), F32)
    res = pl.pallas_call(
        body, name="adamw",
        out_shape=(out, out, out), grid=(1,),
        in_specs=[tile] * 4, out_specs=(tile,) * 3,
        compiler_params=_cp(("arbitrary",)),
    )(*[a.reshape(rows, cols) for a in (w, g, m, v)])
    return tuple(a.reshape(shape) for a in res)


def _tail_exchange(small, s_in, s_sq):
    def body(s_ref, a_ref, b_ref, o_ref, oa_ref, ob_ref, send_sems, recv_sems, local_sem):
        x, y, c = _place()
        me = 4 * x + 2 * y + c
        sibling = (x, y, 1 - c)
        mine = pltpu.make_async_copy(s_ref, o_ref.at[me], local_sem)
        mine.start()
        swaps = [_remote(a_ref, oa_ref, send_sems, recv_sems, 7, sibling), _remote(b_ref, ob_ref, send_sems, recv_sems, 8, sibling)]
        sends = [_remote(s_ref, o_ref.at[me], send_sems, recv_sems, r, peer) for r, peer in enumerate(_peers(x, y, c))]
        for cp in swaps + sends:
            cp.start()
        for r, peer in enumerate(_peers(x, y, c)):
            theirs = o_ref.at[4 * peer[0] + 2 * peer[1] + peer[2]]
            _remote(theirs, theirs, send_sems, recv_sems, r, peer).wait_recv()
        for cp in sends:
            cp.wait_send()
        for cp in swaps:
            cp.wait()
        mine.wait()

    return pl.pallas_call(
        body, name="tail_exchange",
        out_shape=(jax.ShapeDtypeStruct((8,) + small.shape, small.dtype),
                   jax.ShapeDtypeStruct(s_in.shape, s_in.dtype), jax.ShapeDtypeStruct(s_sq.shape, s_sq.dtype)),
        in_specs=[ANY] * 3, out_specs=(ANY,) * 3,
        scratch_shapes=[pltpu.SemaphoreType.DMA((9,)), pltpu.SemaphoreType.DMA((9,)), pltpu.SemaphoreType.DMA],
        compiler_params=pltpu.CompilerParams(has_side_effects=True),
    )(small, s_in, s_sq)


def _sum_devices(gs):
    NL = len(gs)
    _, R, D = gs[0].shape

    def body(*refs):
        o_ref = refs[NL]
        for l in range(NL):
            acc = refs[l][0]
            for k in range(1, 8):
                acc = acc + refs[l][k]
            o_ref[l] = acc

    return pl.pallas_call(
        body, name="sum_devices",
        out_shape=jax.ShapeDtypeStruct((NL, R, D), F32),
        grid=(1,),
        in_specs=[pl.BlockSpec((8, R, D), lambda i: (0, 0, 0))] * NL,
        out_specs=pl.BlockSpec((NL, R, D), lambda i: (0, 0, 0)),
        compiler_params=_cp(("arbitrary",)),
    )(*gs)


def kernel(x, pre_norm_g, w_in, w_ret_out, conv_w, conv_b, conv_ln_g, conv_ln_b, w_conv_out, w_o, post_norm_g, loss_target, m_pre_norm_g, m_w_in, m_w_ret_out, m_conv_w, m_conv_b, m_conv_ln_g, m_conv_ln_b, m_w_conv_out, m_w_o, m_post_norm_g, v_pre_norm_g, v_w_in, v_w_ret_out, v_conv_w, v_conv_b, v_conv_ln_g, v_conv_ln_b, v_w_conv_out, v_w_o, v_post_norm_g):
    NL, D, W = w_in.shape
    Cc = conv_w.shape[-1]
    T = x.shape[1]
    L = min(256, T)
    tb = _tables(T, L)
    ax, ay, ac = _place()
    chip = (2 * ax + ay).astype(jnp.int32).reshape(1)
    cidx = ac.astype(jnp.int32).reshape(1)
    pre_g, cb, lg, lb, post_g = (a.reshape(NL, 1, D) for a in (pre_norm_g, conv_b, conv_ln_g, conv_ln_b, post_norm_g))

    win = [_cast_win(chip, w_in, l) for l in range(NL)]
    wsq = [_cast_wsq(chip, w_ret_out, w_conv_out, w_o, l) for l in range(NL)]
    win[0], wcw = _gather_first(win[0], _place_cw(chip, conv_w))

    saved = []
    xl = x[0]
    for l in range(NL):
        more = l + 1 < NL
        ride = _Ride()
        if more:
            _ride_gather_ici(ride, "win", win[l + 1], (0, 3, 4))
        (proj, h), got = _fwd_in(xl, pre_g, win[l], l, ride=ride)
        if more:
            win[l + 1] = got["win"]
        ride = _Ride()
        if more:
            _ride_gather_ici(ride, "win", win[l + 1], (3, 4, 4))
            _ride_gather_ici(ride, "wsq", wsq[l + 1])
        if l == 0:
            _ride_gather_direct(ride, "wsq0", wsq[0])
        (a_ret, states, a_conv, y), got = _branch_fwd(proj, tb, L, wcw, cb, lg, lb, l, ride=ride)
        if more:
            win[l + 1], wsq[l + 1] = got["win"], got["wsq"]
        if l == 0:
            wsq[0] = got["wsq0"]
        ride = _Ride()
        if more:
            _ride_gather_pass(ride, "win", win[l + 1])
            _ride_gather_pass(ride, "wsq", wsq[l + 1])
        (xn, ro, co, ym, z), got = _merge_fwd(xl, proj, a_ret, a_conv, wsq[l], post_g, l, ride=ride)
        if more:
            win[l + 1], wsq[l + 1] = got["win"], got["wsq"]
        saved.append((xl, proj, h, a_ret, states, a_conv, y, ro, co, ym, z))
        xl = xn
    dx, lsum = _loss_fwd_bwd(xl, loss_target[0])
    loss = lax.psum(jnp.sum(lsum), ("x", "y", "c"))

    gin, gsq, uin, usq = [None] * NL, [None] * NL, [None] * NL, [None] * NL
    s_in, s_sq, o_in, o_sq = [None] * NL, [None] * NL, [None] * NL, [None] * NL
    small, gs = [None] * NL, [None] * NL
    for l in reversed(range(NL)):
        xin, proj, h, a_ret, states, a_conv, y, ro, co, ym, z = saved[l]
        (dproj, da_ret, da_conv, gsq[l], dpost), _ = _merge_bwd(dx, proj, a_ret, a_conv, ro, co, ym, z, wsq[l], post_g, l)
        ride = _Ride()
        _ride_exchange(ride, "gsq", gsq[l], "usq", None, (0, 1, 2))
        if l + 1 < NL:
            _ride_exchange(ride, "gin", gin[l + 1], "uin", uin[l + 1], (2,))
            _ride_gather_all(ride, "small", small[l + 1], "gs")
        (dproj, sg), got = _branch_bwd(dproj, da_ret, da_conv, y, proj, states, tb, L, wcw, lg, lb, l, ride=ride)
        usq[l] = got["usq"]
        gin[l] = _win_grad(h, dproj, W)
        ride = _Ride()
        if l > 0:
            _ride_exchange(ride, "gin", gin[l], "uin", None, (0, 1))
        else:
            gin[0] = _add_rows(cidx, gin[0], _swap_rows(gin[0]))
            _ride_exchange(ride, "gin", gin[0], "uin", None, (0, 1, 2))
        if l + 1 < NL:
            uin[l + 1], gs[l + 1] = got["uin"], got["gs"]
            s_in[l + 1] = _sum_group(chip, gin[l + 1], uin[l + 1])
            s_sq[l + 1] = _sum_group(chip, gsq[l + 1], usq[l + 1])
            _ride_swap(ride, "s_in", s_in[l + 1], "o_in")
            _ride_swap(ride, "s_sq", s_sq[l + 1], "o_sq")
        (dx, dpre), got = _in_bwd(dx, dproj, xin, pre_g, win[l], l, ride=ride)
        uin[l] = got["uin"]
        if l + 1 < NL:
            o_in[l + 1], o_sq[l + 1] = got["o_in"], got["o_sq"]
        small[l] = jnp.concatenate([sg, dpre, dpost], axis=0)
    grad_x = dx

    big = {"w_in": None, "w_ret_out": None, "w_conv_out": None, "w_o": None}
    wts = {"w_in": (w_in, m_w_in, v_w_in), "w_ret_out": (w_ret_out, m_w_ret_out, v_w_ret_out),
           "w_conv_out": (w_conv_out, m_w_conv_out, v_w_conv_out), "w_o": (w_o, m_w_o, v_w_o)}
    sq_names = ("w_ret_out", "w_conv_out", "w_o")

    def adam_in(l, ride=None):
        big["w_in"], got = _adamw_layer(big["w_in"], *wts["w_in"], s_in[l], o_in[l], l, 0, ride=ride)
        return got

    def adam_sq(l, part, ride=None):
        n = sq_names[part]
        big[n], got = _adamw_layer(big[n], *wts[n], s_sq[l], o_sq[l], l, part, ride=ride)
        return got

    s_in[0] = _sum_group_half(chip, cidx, gin[0], uin[0])
    s_sq[0] = _sum_group(chip, gsq[0], usq[0])
    gs[0], o_in[0], o_sq[0] = _tail_exchange(small[0], s_in[0], s_sq[0])
    for l in reversed(range(NL)):
        adam_in(l)
        for part in range(3):
            adam_sq(l, part)

    gsm = _sum_devices(gs)
    grads = {
        "pre_norm_g": gsm[:, ROW_PRE], "conv_w": lax.dynamic_slice_in_dim(gsm[:, 0:CONV_K], chip[0] * Cc, Cc, axis=2),
        "conv_b": gsm[:, ROW_CB], "conv_ln_g": gsm[:, ROW_LG], "conv_ln_b": gsm[:, ROW_LB], "post_norm_g": gsm[:, ROW_POST],
    }
    weights = dict(pre_norm_g=pre_norm_g, conv_w=conv_w, conv_b=conv_b, conv_ln_g=conv_ln_g, conv_ln_b=conv_ln_b,
                   post_norm_g=post_norm_g)
    m1 = dict(pre_norm_g=m_pre_norm_g, conv_w=m_conv_w, conv_b=m_conv_b, conv_ln_g=m_conv_ln_g, conv_ln_b=m_conv_ln_b,
              post_norm_g=m_post_norm_g)
    m2 = dict(pre_norm_g=v_pre_norm_g, conv_w=v_conv_w, conv_b=v_conv_b, conv_ln_g=v_conv_ln_g, conv_ln_b=v_conv_ln_b,
              post_norm_g=v_post_norm_g)
    res = {n: (grads[n],) + _adamw(weights[n], grads[n], m1[n], m2[n]) for n in grads}
    res.update(big)
    order = ["pre_norm_g", "w_in", "w_ret_out", "conv_w", "conv_b", "conv_ln_g", "conv_ln_b", "w_conv_out", "w_o", "post_norm_g"]
    return (loss, grad_x[None], *[res[n][0] for n in order], *[res[n][1] for n in order],
            *[res[n][2] for n in order], *[res[n][3] for n in order])
```

```python
import numpy as np
import jax
import jax.numpy as jnp
from jax import lax
from jax.experimental import pallas as pl
from jax.experimental.pallas import tpu as pltpu

F32 = jnp.float32
BF16 = jnp.bfloat16

HEADS = 8
DK = 64
DV = 128
CONV_K = 31
CHUNK = 64
ROPE_BASE = 10000.0
EPS = 1e-6
HALO = 32
CONV_RB = 32
CONV_TILE = 256
RET_BLOCK = 512

ADAM_LR = 0.001
ADAM_B1 = 0.9
ADAM_B2 = 0.999
ADAM_EPS = 1e-08
ADAM_WD = 0.01
ADAM_STEP = 10
ADAM_BLOCK_BYTES = 2 * 1024 * 1024

VMEM_LIMIT = 56 * 1024 * 1024
MESH_T = pl.DeviceIdType.MESH
ANY = pl.BlockSpec(memory_space=pl.ANY)

ROW_CB, ROW_LG, ROW_LB = 32, 33, 34
ROW_PRE, ROW_POST = 40, 48


def _cp(sem=None, **kw):
    return pltpu.CompilerParams(dimension_semantics=sem, vmem_limit_bytes=VMEM_LIMIT, **kw)


def _dot(a, b):
    return jnp.dot(a, b, preferred_element_type=F32)


def _dot_nt(a, b):
    return lax.dot_general(a, b, (((1,), (1,)), ((), ())), preferred_element_type=F32)


def _dot_tn(a, b):
    return lax.dot_general(a, b, (((0,), (0,)), ((), ())), preferred_element_type=F32)


def _sigmoid(x):
    return jax.nn.sigmoid(x)


def _silu(x):
    return x * _sigmoid(x)


def _rms(x, g):
    return x * lax.rsqrt(jnp.mean(x * x, axis=-1, keepdims=True) + EPS) * g


def _gn_gate(o, g):
    mu = jnp.mean(o, axis=-1, keepdims=True)
    d = o - mu
    var = jnp.mean(d * d, axis=-1, keepdims=True)
    return d * lax.rsqrt(var + EPS) * _silu(g)


def _ln_gate(y, gc, lg, lb):
    mu = jnp.mean(y, axis=-1, keepdims=True)
    d = y - mu
    var = jnp.mean(d * d, axis=-1, keepdims=True)
    return _silu(d * lax.rsqrt(var + EPS) * lg + lb) * _silu(gc)


def _tables(T, L):
    lane = np.arange(128)
    d = lane % DK
    half = DK // 2
    inv = (ROPE_BASE ** (-(np.arange(half, dtype=np.float32)) / half)).astype(np.float32)
    ang = (np.arange(T, dtype=np.float32)[:, None] * inv[None, :]).astype(np.float64)
    angl = ang[:, d % half]
    cos = np.cos(angl)
    sin = np.sin(angl)
    lo = (d < half)[None, :]
    rope = np.stack([cos, np.where(lo, -sin, 0.0), np.where(lo, 0.0, sin)]).astype(np.float32)

    hh = np.arange(HEADS, dtype=np.float64)
    log_g = np.log1p(-np.exp2(-5.0 - hh))
    n = np.arange(L, dtype=np.float64)
    cn = np.arange(L) // CHUNK
    allowed = (cn[None, :] <= cn[:, None])
    dist = np.abs(n[:, None] - n[None, :])
    mask = np.exp(log_g[:, None, None] * dist[None]) * allowed[None]
    mq = ((lane[None, :] // DK) == (np.arange(HEADS)[:, None] % 2)).astype(np.float64)
    qd = np.exp(log_g[:, None] * n[None, :])
    kd = np.exp(log_g[:, None] * (L - n[None, :]))
    qdm = qd[:, :, None] * mq[:, None, :] * (DK ** -0.5)
    kdm = kd[:, :, None] * mq[:, None, :]
    mqs = np.broadcast_to((mq * (DK ** -0.5))[:, None, :], (HEADS, 8, 128))
    cd = np.broadcast_to(np.exp(log_g * L)[:, None, None], (HEADS, 8, 128))
    f = lambda a: jnp.asarray(np.ascontiguousarray(a), dtype=F32)
    return dict(rope=f(rope), mask=f(mask), qdm=f(qdm), kdm=f(kdm), mqs=f(mqs), cd=f(cd))


def _rot(b, c, sl, sh):
    return b * c + pltpu.roll(b, 96, axis=1) * sl + pltpu.roll(b, 32, axis=1) * sh


def _rot_t(d, c, sl, sh):
    return d * c + pltpu.roll(d * sl, 32, axis=1) + pltpu.roll(d * sh, 96, axis=1)


def _place():
    return lax.axis_index("x"), lax.axis_index("y"), lax.axis_index("c")


def _other_chips(x, y):
    return [(1 - x, y), (x, 1 - y), (1 - x, 1 - y)]


def _remote(src, dst, send_sems, recv_sems, k, to):
    return pltpu.make_async_remote_copy(src_ref=src, dst_ref=dst, send_sem=send_sems.at[k], recv_sem=recv_sems.at[k],
                                        device_id=to, device_id_type=MESH_T)


class _Ride:
    def __init__(self):
        self.arrays, self.kinds, self.names = [], [], []
        self.fresh = []
        self.ops = []

    def read(self, name, a):
        self.names.append(name)
        self.arrays.append(a)
        self.kinds.append("in")

    def inout(self, name, a):
        self.names.append(name)
        self.arrays.append(a)
        self.kinds.append("inout")

    def land(self, name, shape, dtype):
        self.fresh.append((name, jax.ShapeDtypeStruct(shape, dtype)))

    def op(self, n_sems, start, finish):
        self.ops.append((n_sems, start, finish))


def _pcall(body, *, name, grid, in_specs, out_specs, out_shape, args, scratch_shapes=(), sem, aliases=None, ride=None):
    if ride is None or not ride.ops:
        outs = pl.pallas_call(body, name=name, grid=grid, in_specs=list(in_specs), out_specs=tuple(out_specs),
                              out_shape=tuple(out_shape), scratch_shapes=list(scratch_shapes),
                              input_output_aliases=dict(aliases or {}), compiler_params=_cp(sem))(*args)
        return outs, {}

    ni, no, nr = len(args), len(out_shape), len(ride.arrays)
    inout = [i for i, k in enumerate(ride.kinds) if k == "inout"]
    r_out_shapes = [jax.ShapeDtypeStruct(ride.arrays[i].shape, ride.arrays[i].dtype) for i in inout] + [s for _, s in ride.fresh]
    r_out_names = [ride.names[i] for i in inout] + [n for n, _ in ride.fresh]
    nro = len(r_out_shapes)
    n_sems = sum(n for n, _, _ in ride.ops)
    n_scr = len(scratch_shapes)
    nd = len(grid)

    def wrapped(*refs):
        ins, rin = refs[:ni], refs[ni:ni + nr]
        outs, rout = refs[ni + nr:ni + nr + no], refs[ni + nr + no:ni + nr + no + nro]
        scr = refs[ni + nr + no + nro:ni + nr + no + nro + n_scr]
        send_sems, recv_sems = refs[-2], refs[-1]
        view = {nm: r for nm, r, k in zip(ride.names, rin, ride.kinds) if k == "in"}
        view.update(dict(zip(r_out_names, rout)))
        first = pl.program_id(0) == 0
        last = pl.program_id(0) == grid[0] - 1
        for d in range(1, nd):
            first = first & (pl.program_id(d) == 0)
            last = last & (pl.program_id(d) == grid[d] - 1)

        @pl.when(first)
        def _():
            base = 0
            for n, start, _ in ride.ops:
                start(view, send_sems, recv_sems, base)
                base += n

        body(*ins, *outs, *scr)

        @pl.when(last)
        def _():
            base = 0
            for n, _, finish in ride.ops:
                finish(view, send_sems, recv_sems, base)
                base += n

    res = pl.pallas_call(
        wrapped, name=name, grid=grid,
        in_specs=list(in_specs) + [ANY] * nr, out_specs=tuple(out_specs) + (ANY,) * nro,
        out_shape=tuple(out_shape) + tuple(r_out_shapes),
        scratch_shapes=list(scratch_shapes) + [pltpu.SemaphoreType.DMA((n_sems,)), pltpu.SemaphoreType.DMA((n_sems,))],
        input_output_aliases={**dict(aliases or {}), **{ni + i: no + j for j, i in enumerate(inout)}},
        compiler_params=_cp(sem),
    )(*args, *ride.arrays)
    return res[:no], dict(zip(r_out_names, res[no:]))


def _half(ref, chip_idx, cc, part=(0, 1, 1)):
    n = ref.shape[1] // 2
    lo, hi, k = part
    return ref.at[chip_idx, pl.ds(cc * n + lo * n // k, (hi - lo) * n // k)]


def _ride_gather_ici(ride, name, a, part=(0, 1, 1)):
    ride.inout(name, a)

    def start(view, ss, rs, b):
        x, y, c = _place()
        mine = _half(view[name], 2 * x + y, c, part)
        for j, (cx, cy) in enumerate(_other_chips(x, y)):
            _remote(mine, mine, ss, rs, b + j, (cx, cy, c)).start()

    def finish(view, ss, rs, b):
        x, y, c = _place()
        mine = _half(view[name], 2 * x + y, c, part)
        for j, (cx, cy) in enumerate(_other_chips(x, y)):
            theirs = _half(view[name], 2 * cx + cy, c, part)
            _remote(theirs, theirs, ss, rs, b + j, (cx, cy, c)).wait_recv()
        for j, (cx, cy) in enumerate(_other_chips(x, y)):
            _remote(mine, mine, ss, rs, b + j, (cx, cy, c)).wait_send()

    ride.op(3, start, finish)


def _ride_gather_direct(ride, name, a):
    ride.inout(name, a)

    def start(view, ss, rs, b):
        x, y, c = _place()
        mine = _half(view[name], 2 * x + y, c)
        for j, (cx, cy) in enumerate(_other_chips(x, y)):
            _remote(mine, mine, ss, rs, b + 2 * j, (cx, cy, c)).start()
            _remote(mine, mine, ss, rs, b + 2 * j + 1, (cx, cy, 1 - c)).start()

    def finish(view, ss, rs, b):
        x, y, c = _place()
        mine = _half(view[name], 2 * x + y, c)
        for j, (cx, cy) in enumerate(_other_chips(x, y)):
            same = _half(view[name], 2 * cx + cy, c)
            other = _half(view[name], 2 * cx + cy, 1 - c)
            _remote(same, same, ss, rs, b + 2 * j, (cx, cy, c)).wait_recv()
            _remote(other, other, ss, rs, b + 2 * j + 1, (cx, cy, 1 - c)).wait_recv()
        for j, (cx, cy) in enumerate(_other_chips(x, y)):
            _remote(mine, mine, ss, rs, b + 2 * j, (cx, cy, c)).wait_send()
            _remote(mine, mine, ss, rs, b + 2 * j + 1, (cx, cy, 1 - c)).wait_send()

    ride.op(6, start, finish)


def _ride_gather_pass(ride, name, a):
    ride.inout(name, a)

    def start(view, ss, rs, b):
        x, y, c = _place()
        for j, (cx, cy) in enumerate(_other_chips(x, y)):
            blk = _half(view[name], 2 * cx + cy, c)
            _remote(blk, blk, ss, rs, b + j, (x, y, 1 - c)).start()

    def finish(view, ss, rs, b):
        x, y, c = _place()
        for j, (cx, cy) in enumerate(_other_chips(x, y)):
            theirs = _half(view[name], 2 * cx + cy, 1 - c)
            _remote(theirs, theirs, ss, rs, b + j, (x, y, 1 - c)).wait_recv()
        for j, (cx, cy) in enumerate(_other_chips(x, y)):
            blk = _half(view[name], 2 * cx + cy, c)
            _remote(blk, blk, ss, rs, b + j, (x, y, 1 - c)).wait_send()

    ride.op(3, start, finish)


def _ride_exchange(ride, src_name, src, dst_name, dst, rels):
    ride.read(src_name, src)
    if dst is None:
        ride.land(dst_name, (3,) + src.shape[1:], src.dtype)
    else:
        ride.inout(dst_name, dst)

    def start(view, ss, rs, b):
        x, y, c = _place()
        chips = _other_chips(x, y)
        for i, j in enumerate(rels):
            cx, cy = chips[j]
            _remote(view[src_name].at[2 * cx + cy], view[dst_name].at[j], ss, rs, b + i, (cx, cy, c)).start()

    def finish(view, ss, rs, b):
        x, y, c = _place()
        chips = _other_chips(x, y)
        for i, j in enumerate(rels):
            cx, cy = chips[j]
            _remote(view[src_name].at[2 * cx + cy], view[dst_name].at[j], ss, rs, b + i, (cx, cy, c)).wait()

    ride.op(len(rels), start, finish)


def _ride_swap(ride, src_name, src, dst_name):
    ride.read(src_name, src)
    ride.land(dst_name, src.shape, src.dtype)

    def start(view, ss, rs, b):
        x, y, c = _place()
        _remote(view[src_name], view[dst_name], ss, rs, b, (x, y, 1 - c)).start()

    def finish(view, ss, rs, b):
        x, y, c = _place()
        _remote(view[src_name], view[dst_name], ss, rs, b, (x, y, 1 - c)).wait()

    ride.op(1, start, finish)


def _peers(x, y, c):
    flip = lambda v, b: 1 - v if b else v
    return [(flip(x, r & 4), flip(y, r & 2), flip(c, r & 1)) for r in range(1, 8)]


def _ride_gather_all(ride, src_name, src, dst_name):
    ride.read(src_name, src)
    ride.land(dst_name, (8,) + src.shape, src.dtype)

    def start(view, ss, rs, b):
        x, y, c = _place()
        me = 4 * x + 2 * y + c
        pltpu.make_async_copy(view[src_name], view[dst_name].at[me], ss.at[b + 7]).start()
        for r, peer in enumerate(_peers(x, y, c)):
            _remote(view[src_name], view[dst_name].at[me], ss, rs, b + r, peer).start()

    def finish(view, ss, rs, b):
        x, y, c = _place()
        me = 4 * x + 2 * y + c
        for r, peer in enumerate(_peers(x, y, c)):
            theirs = view[dst_name].at[4 * peer[0] + 2 * peer[1] + peer[2]]
            _remote(theirs, theirs, ss, rs, b + r, peer).wait_recv()
        for r, peer in enumerate(_peers(x, y, c)):
            _remote(view[src_name], view[dst_name].at[me], ss, rs, b + r, peer).wait_send()
        pltpu.make_async_copy(view[src_name], view[dst_name].at[me], ss.at[b + 7]).wait()

    ride.op(8, start, finish)


def _cast_win(chip, w_in, l):
    _, D, W = w_in.shape
    tr = min(256, D)

    def body(chip_ref, w_ref, o_ref):
        o_ref[...] = w_ref[...].astype(BF16)

    return pl.pallas_call(
        body, name="cast_win",
        out_shape=jax.ShapeDtypeStruct((4, D, W), BF16),
        grid_spec=pltpu.PrefetchScalarGridSpec(
            num_scalar_prefetch=1, grid=(D // tr,),
            in_specs=[pl.BlockSpec((None, tr, W), lambda r, c: (l, r, 0))],
            out_specs=pl.BlockSpec((None, tr, W), lambda r, c: (c[0], r, 0))),
        compiler_params=_cp(("arbitrary",)),
    )(chip, w_in)


def _cast_wsq(chip, w_ro, w_co, w_o, l):
    _, R, D = w_ro.shape

    def body(chip_ref, a_ref, b_ref, c_ref, o_ref):
        o_ref[0:R, :] = a_ref[...].astype(BF16)
        o_ref[R:2 * R, :] = b_ref[...].astype(BF16)
        o_ref[2 * R:3 * R, :] = c_ref[...].astype(BF16)

    spec = pl.BlockSpec((None, R, D), lambda i, c: (l, 0, 0))
    return pl.pallas_call(
        body, name="cast_wsq",
        out_shape=jax.ShapeDtypeStruct((4, 3 * R, D), BF16),
        grid_spec=pltpu.PrefetchScalarGridSpec(
            num_scalar_prefetch=1, grid=(1,),
            in_specs=[spec, spec, spec],
            out_specs=pl.BlockSpec((None, 3 * R, D), lambda i, c: (c[0], 0, 0))),
        compiler_params=_cp(("arbitrary",)),
    )(chip, w_ro, w_co, w_o)


def _place_cw(chip, conv_w):
    NL, K, Cc = conv_w.shape

    def body(chip_ref, w_ref, o_ref):
        o_ref[...] = w_ref[...]

    return pl.pallas_call(
        body, name="place_cw",
        out_shape=jax.ShapeDtypeStruct((4, NL, K, Cc), F32),
        grid_spec=pltpu.PrefetchScalarGridSpec(
            num_scalar_prefetch=1, grid=(1,),
            in_specs=[pl.BlockSpec((NL, K, Cc), lambda i, c: (0, 0, 0))],
            out_specs=pl.BlockSpec((None, NL, K, Cc), lambda i, c: (c[0], 0, 0, 0))),
        compiler_params=_cp(("arbitrary",)),
    )(chip, conv_w)


def _gather_first(win0, wcw):
    n_arr = 2

    def body(a0, a1, o0, o1, send_sems, recv_sems):
        x, y, c = _place()
        sibling = (x, y, 1 - c)
        chips = _other_chips(x, y)
        outs = (o0, o1)

        def copy(k, a, cx, cy, cc, to):
            blk = _half(outs[a], 2 * cx + cy, cc)
            return _remote(blk, blk, send_sems, recv_sems, k, to)

        first = [copy(3 * a + j, a, x, y, c, (*chip, c)) for a in range(n_arr) for j, chip in enumerate(chips)]
        for cp in first:
            cp.start()
        passed = [copy(3 * n_arr + 3 * a + j, a, *chip, c, sibling) for a in range(n_arr) for j, chip in enumerate(chips)]
        for a in range(n_arr):
            for j, chip in enumerate(chips):
                copy(3 * a + j, a, *chip, c, sibling).wait_recv()
                passed[3 * a + j].start()
        for a in range(n_arr):
            for j, chip in enumerate(chips):
                copy(3 * n_arr + 3 * a + j, a, *chip, 1 - c, sibling).wait_recv()
        for cp in first + passed:
            cp.wait_send()

    ins = (win0, wcw)
    return pl.pallas_call(
        body, name="gather_first",
        out_shape=tuple(jax.ShapeDtypeStruct(a.shape, a.dtype) for a in ins),
        in_specs=[ANY] * n_arr, out_specs=(ANY,) * n_arr,
        scratch_shapes=[pltpu.SemaphoreType.DMA((6 * n_arr,)), pltpu.SemaphoreType.DMA((6 * n_arr,))],
        input_output_aliases={0: 0, 1: 1},
        compiler_params=pltpu.CompilerParams(has_side_effects=True),
    )(*ins)


def _fwd_in(x, pre_g, win, l, ride=None):
    T, D = x.shape
    W = win.shape[-1]
    tm = min(1024, T)

    def body(x_ref, g_ref, w_ref, p_ref, h_ref, hs):
        @pl.when(pl.program_id(1) == 0)
        def _():
            hb = _rms(x_ref[...], g_ref[...]).astype(BF16)
            hs[...] = hb
            h_ref[...] = hb

        p_ref[...] = _dot(hs[...], w_ref[...]).astype(BF16)

    return _pcall(
        body, name="fwd_in",
        out_shape=(jax.ShapeDtypeStruct((T, 4 * W), BF16), jax.ShapeDtypeStruct((T, D), BF16)),
        grid=(T // tm, 4),
        in_specs=[pl.BlockSpec((tm, D), lambda i, j: (i, 0)),
                  pl.BlockSpec((None, 1, D), lambda i, j: (l, 0, 0)),
                  pl.BlockSpec((None, D, W), lambda i, j: (j, 0, 0))],
        out_specs=(pl.BlockSpec((tm, W), lambda i, j: (i, j)),
                   pl.BlockSpec((tm, D), lambda i, j: (i, 0))),
        scratch_shapes=[pltpu.VMEM((tm, D), BF16)],
        sem=("arbitrary", "arbitrary"), args=(x, pre_g, win), ride=ride)


def _ret_specs(T, L):
    rope = pl.BlockSpec((3, L, 128), lambda s: (0, s, 0))
    mask = pl.BlockSpec((HEADS, L, L), lambda s: (0, 0, 0), pipeline_mode=pl.Buffered(1))
    qdm = pl.BlockSpec((HEADS, L, 128), lambda s: (0, 0, 0), pipeline_mode=pl.Buffered(1))
    small = pl.BlockSpec((HEADS, 8, 128), lambda s: (0, 0, 0))
    return [rope, mask, qdm, qdm, small, small]


QK = HEADS * DK
VW = HEADS * DV
PW = 2 * QK + 2 * VW


def _zero_at_start(ref):
    @pl.when(pl.program_id(0) == 0)
    def _():
        ref[...] = jnp.zeros_like(ref)


def _ret_fwd_part(p_ref, rope_ref, m_ref, qdm_ref, kdm_ref, mqs_ref, cd_ref, a_ref, st_ref, state):
    c, sl, sh = rope_ref[0], rope_ref[1], rope_ref[2]
    for j in range(HEADS // 2):
        rq = _rot(p_ref[:, 128 * j:128 * (j + 1)].astype(F32), c, sl, sh)
        rk = _rot(p_ref[:, QK + 128 * j:QK + 128 * (j + 1)].astype(F32), c, sl, sh)
        rkb = rk.astype(BF16)
        for e in range(2):
            h = 2 * j + e
            v = p_ref[:, 2 * QK + DV * h:2 * QK + DV * (h + 1)]
            g = p_ref[:, 2 * QK + VW + DV * h:2 * QK + VW + DV * (h + 1)].astype(F32)
            a = (rq * mqs_ref[h, 0:1, :]).astype(BF16)
            p = (_dot_nt(a, rkb) * m_ref[h]).astype(BF16)
            st = state[h]
            st_ref[h] = st
            o = _dot(p, v) + _dot((rq * qdm_ref[h]).astype(BF16), st.astype(BF16))
            state[h] = st * cd_ref[h, 0:1, :] + _dot_tn((rk * kdm_ref[h]).astype(BF16), v)
            a_ref[:, DV * h:DV * (h + 1)] = _gn_gate(o, g).astype(BF16)


def _shift_copies(src, sh):
    rows = sh.shape[1]
    for b in range(1, 8):
        sh[b - 1, :, :] = src[pl.ds(b, rows), :]


def _window(src, sh, start, rows):
    b = start % 8
    if b == 0:
        return src[pl.ds(start, rows), :]
    return sh[b - 1, pl.ds(start - b, rows), :]


def _conv_taps(wbuf, src, sh, base, rows):
    acc = wbuf[pl.ds(0, 1), :] * _window(src, sh, base, rows)
    for k in range(1, CONV_K):
        acc = acc + wbuf[pl.ds(k, 1), :] * _window(src, sh, base + k, rows)
    return acc


def _load_conv_w(cw_ref, wbuf, flip):
    for k in range(CONV_K):
        row = jnp.concatenate([cw_ref[c, pl.ds(k, 1), :] for c in range(4)], axis=-1)
        wbuf[pl.ds(CONV_K - 1 - k if flip else k, 1), :] = row
    wbuf[pl.ds(CONV_K, 1), :] = jnp.zeros_like(wbuf[pl.ds(CONV_K, 1), :])


def _conv_fwd_part(p_ref, cw_ref, cb_ref, lg_ref, lb_ref, a_ref, y_ref, ubuf, wbuf, ush):
    tc, C = y_ref.shape
    off = HALO - (CONV_K - 1)
    i = pl.program_id(0)

    @pl.when(i == 0)
    def _():
        ubuf[0:HALO, :] = jnp.zeros((HALO, C), F32)
        _load_conv_w(cw_ref, wbuf, False)

    @pl.when(i > 0)
    def _():
        ubuf[0:HALO, :] = ubuf[tc:tc + HALO, :]

    ga = p_ref[:, 0:C].astype(F32)
    gb = p_ref[:, C:2 * C].astype(F32)
    ubuf[HALO:HALO + tc, :] = ga * _sigmoid(gb)
    _shift_copies(ubuf, ush)
    for r in range(tc // CONV_RB):
        y_ref[r * CONV_RB:(r + 1) * CONV_RB, :] = _conv_taps(wbuf, ubuf, ush, r * CONV_RB + off, CONV_RB) + cb_ref[...]
    gc = p_ref[:, 2 * C:3 * C].astype(F32)
    a_ref[...] = _ln_gate(y_ref[...], gc, lg_ref[...], lb_ref[...]).astype(BF16)


def _ret_fwd(proj, tb, L, ride=None):
    T = proj.shape[0]
    nS = T // L

    def body(p_ref, rope_ref, m_ref, qdm_ref, kdm_ref, mqs_ref, cd_ref, ar_ref, st_ref, state):
        _zero_at_start(state)
        _ret_fwd_part(p_ref, rope_ref, m_ref, qdm_ref, kdm_ref, mqs_ref, cd_ref, ar_ref, st_ref, state)

    return _pcall(
        body, name="ret_fwd",
        out_shape=(jax.ShapeDtypeStruct((T, VW), BF16), jax.ShapeDtypeStruct((nS, HEADS, 128, DV), F32)),
        grid=(nS,),
        in_specs=[pl.BlockSpec((L, PW), lambda s: (s, 0))] + _ret_specs(T, L),
        out_specs=(pl.BlockSpec((L, VW), lambda s: (s, 0)), pl.BlockSpec((None, HEADS, 128, DV), lambda s: (s, 0, 0, 0))),
        scratch_shapes=[pltpu.VMEM((HEADS, 128, DV), F32)],
        sem=("arbitrary",), ride=ride,
        args=(proj, tb["rope"], tb["mask"], tb["qdm"], tb["kdm"], tb["mqs"], tb["cd"]))


def _conv_fwd(proj, wcw, conv_b, ln_g, ln_b, l, ride=None):
    T = proj.shape[0]
    C = conv_b.shape[-1]
    Cc = wcw.shape[-1]
    tc = min(CONV_TILE, T)
    assert PW == 3 * C

    def body(p_ref, cw_ref, cb_ref, lg_ref, lb_ref, ac_ref, y_ref, ubuf, wbuf, ush):
        _conv_fwd_part(p_ref, cw_ref, cb_ref, lg_ref, lb_ref, ac_ref, y_ref, ubuf, wbuf, ush)

    vec = pl.BlockSpec((None, 1, C), lambda i: (l, 0, 0))
    tile = pl.BlockSpec((tc, C), lambda i: (i, 0))
    return _pcall(
        body, name="conv_fwd",
        out_shape=(jax.ShapeDtypeStruct((T, C), BF16), jax.ShapeDtypeStruct((T, C), F32)),
        grid=(T // tc,),
        in_specs=[pl.BlockSpec((tc, 3 * C), lambda i: (i, 1)),
                  pl.BlockSpec((4, None, CONV_K, Cc), lambda i: (0, l, 0, 0)), vec, vec, vec],
        out_specs=(tile, tile),
        scratch_shapes=[pltpu.VMEM((HALO + tc, C), F32), pltpu.VMEM((HALO, C), F32), pltpu.VMEM((7, HALO + tc - 8, C), F32)],
        sem=("arbitrary",), ride=ride, args=(proj, wcw, conv_b, ln_g, ln_b))


def _merge_fwd(x, proj, a_ret, a_conv, wsq, post_g, l, ride=None):
    T, D = x.shape
    R = wsq.shape[1] // 3
    tm = min(512, T)

    def body(x_ref, p_ref, ar_ref, ac_ref, wro_ref, wco_ref, wo_ref, g_ref, xn_ref, ro_ref, co_ref, ym_ref, z_ref):
        ro = _dot(ar_ref[...], wro_ref[...].reshape(4 * R, D))
        co = _dot(ac_ref[...], wco_ref[...].reshape(4 * R, D))
        ym = (_sigmoid(p_ref[:, 0:D].astype(F32)) * ro + _sigmoid(p_ref[:, D:2 * D].astype(F32)) * co).astype(BF16)
        z = _dot(ym, wo_ref[...].reshape(4 * R, D))
        ro_ref[...] = ro.astype(BF16)
        co_ref[...] = co.astype(BF16)
        ym_ref[...] = ym
        z_ref[...] = z.astype(BF16)
        xn_ref[...] = x_ref[...] + _rms(z, g_ref[...])

    tile = pl.BlockSpec((tm, D), lambda i: (i, 0))
    wspec = lambda m: pl.BlockSpec((4, R, D), lambda i: (0, m, 0))
    act = jax.ShapeDtypeStruct((T, D), BF16)
    return _pcall(
        body, name="merge_fwd",
        out_shape=(jax.ShapeDtypeStruct((T, D), F32), act, act, act, act),
        grid=(T // tm,),
        in_specs=[tile, pl.BlockSpec((tm, 2 * D), lambda i: (i, 3)), tile, tile,
                  wspec(0), wspec(1), wspec(2), pl.BlockSpec((None, 1, D), lambda i: (l, 0, 0))],
        out_specs=(tile, tile, tile, tile, tile),
        sem=("arbitrary",), ride=ride, args=(x, proj, a_ret, a_conv, wsq, wsq, wsq, post_g))


def _loss_fwd_bwd(y, target):
    T, D = y.shape
    tm = min(512, T)

    def body(y_ref, t_ref, dy_ref, ls_ref):
        @pl.when(pl.program_id(0) == 0)
        def _():
            ls_ref[...] = jnp.zeros_like(ls_ref)

        e = y_ref[...] - t_ref[...]
        dy_ref[...] = e * (1.0 / D)
        ls_ref[...] += jnp.sum((e * e).reshape(tm // 8, 8, D), axis=0) * (0.5 / D)

    tile = pl.BlockSpec((tm, D), lambda i: (i, 0))
    return pl.pallas_call(
        body, name="loss",
        out_shape=(jax.ShapeDtypeStruct((T, D), F32), jax.ShapeDtypeStruct((8, D), F32)),
        grid=(T // tm,),
        in_specs=[tile, tile],
        out_specs=(tile, pl.BlockSpec((8, D), lambda i: (0, 0))),
        compiler_params=_cp(("arbitrary",)),
    )(y, target)


def _merge_bwd(dxn, proj, a_ret, a_conv, ro, co, ym, z, wsq, post_g, l, ride=None):
    T, D = dxn.shape
    R = wsq.shape[1] // 3
    tm = min(512, T)
    n = T // tm

    def body(dx_ref, p_ref, ar_ref, ac_ref, ro_ref, co_ref, ym_ref, z_ref, wro_ref, wco_ref, wo_ref, g_ref,
             dp_ref, dar_ref, dac_ref, gsq_ref, dg_ref, acc, stage):
        i = pl.program_id(0)

        @pl.when(i == 0)
        def _():
            acc[...] = jnp.zeros_like(acc)
            dg_ref[...] = jnp.zeros_like(dg_ref)

        _, vjp = jax.vjp(_rms, z_ref[...].astype(F32), g_ref[...])
        dz, dg = vjp(dx_ref[...])
        dg_ref[0:1, :] += dg
        dzb = dz.astype(BF16)
        dym = _dot_nt(dzb, wo_ref[...].reshape(4 * R, D))
        acc[2] += _dot_tn(ym_ref[...], dzb)
        sr = _sigmoid(p_ref[:, 0:D].astype(F32))
        sc = _sigmoid(p_ref[:, D:2 * D].astype(F32))
        rov = ro_ref[...].astype(F32)
        cov = co_ref[...].astype(F32)
        dp_ref[:, 0:D] = (dym * rov * sr * (1.0 - sr)).astype(BF16)
        dp_ref[:, D:2 * D] = (dym * cov * sc * (1.0 - sc)).astype(BF16)
        dro = (dym * sr).astype(BF16)
        dco = (dym * sc).astype(BF16)
        dar_ref[...] = _dot_nt(dro, wro_ref[...].reshape(4 * R, D)).astype(BF16)
        dac_ref[...] = _dot_nt(dco, wco_ref[...].reshape(4 * R, D)).astype(BF16)
        acc[0] += _dot_tn(ar_ref[...], dro)
        acc[1] += _dot_tn(ac_ref[...], dco)

        @pl.when(i == n - 1)
        def _():
            for m in range(3):
                stage[...] = acc[m].astype(BF16).reshape(4, R, D)
                pltpu.sync_copy(stage, gsq_ref.at[:, pl.ds(m * R, R), :])

    tile = pl.BlockSpec((tm, D), lambda i: (i, 0))
    wspec = lambda m: pl.BlockSpec((4, R, D), lambda i: (0, m, 0), pipeline_mode=pl.Buffered(1))
    return _pcall(
        body, name="merge_bwd",
        out_shape=(jax.ShapeDtypeStruct(proj.shape, BF16), jax.ShapeDtypeStruct((T, D), BF16),
                   jax.ShapeDtypeStruct((T, D), BF16), jax.ShapeDtypeStruct(wsq.shape, BF16),
                   jax.ShapeDtypeStruct((8, D), F32)),
        grid=(n,),
        in_specs=[tile, pl.BlockSpec((tm, 2 * D), lambda i: (i, 3)), tile, tile, tile, tile, tile, tile,
                  wspec(0), wspec(1), wspec(2), pl.BlockSpec((None, 1, D), lambda i: (l, 0, 0))],
        out_specs=(pl.BlockSpec((tm, 2 * D), lambda i: (i, 3)), tile, tile, ANY, pl.BlockSpec((8, D), lambda i: (0, 0))),
        scratch_shapes=[pltpu.VMEM((3, 4 * R, D), F32), pltpu.VMEM((4, R, D), BF16)],
        sem=("arbitrary",), args=(dxn, proj, a_ret, a_conv, ro, co, ym, z, wsq, wsq, wsq, post_g), ride=ride)


def _conv_bwd_part(n, da_ref, y_ref, p_ref, ph_ref, cw_ref, lg_ref, lb_ref, dp_ref,
                   dcbuf, ubuf, dubuf, wbuf, dwacc, vacc, dsh, ush):
    tc, C = y_ref.shape
    off = HALO - (CONV_K - 1)
    nrb = tc // CONV_RB
    t = pl.program_id(0)
    i = n - 1 - t

    @pl.when(t == 0)
    def _():
        dcbuf[tc:tc + HALO, :] = jnp.zeros((HALO, C), F32)
        dwacc[...] = jnp.zeros_like(dwacc)
        vacc[...] = jnp.zeros_like(vacc)
        _load_conv_w(cw_ref, wbuf, True)

    @pl.when(t > 0)
    def _():
        dcbuf[tc:tc + HALO, :] = dcbuf[0:HALO, :]

    gc = p_ref[:, 2 * C:3 * C].astype(F32)
    _, vjp = jax.vjp(_ln_gate, y_ref[...], gc, lg_ref[...], lb_ref[...])
    dy, dgc, dlg, dlb = vjp(da_ref[...].astype(F32))
    dcbuf[0:tc, :] = dy
    dp_ref[:, 2 * C:3 * C] = dgc.astype(BF16)
    vacc[0:1, :] += jnp.sum(dy, axis=0, keepdims=True)
    vacc[1:2, :] += dlg
    vacc[2:3, :] += dlb

    ga = p_ref[:, 0:C].astype(F32)
    sb = _sigmoid(p_ref[:, C:2 * C].astype(F32))
    ubuf[HALO:HALO + tc, :] = ga * sb
    uh = ph_ref[:, 0:C].astype(F32) * _sigmoid(ph_ref[:, C:2 * C].astype(F32))
    ubuf[0:HALO, :] = jnp.where(i > 0, uh, 0.0)

    _shift_copies(dcbuf, dsh)
    _shift_copies(ubuf, ush)
    for r in range(nrb):
        dubuf[r * CONV_RB:(r + 1) * CONV_RB, :] = _conv_taps(wbuf, dcbuf, dsh, r * CONV_RB, CONV_RB)
    du = dubuf[...]
    dp_ref[:, 0:C] = (du * sb).astype(BF16)
    dp_ref[:, C:2 * C] = (du * ga * sb * (1.0 - sb)).astype(BF16)

    for r in range(nrb):
        dyb = dcbuf[r * CONV_RB:(r + 1) * CONV_RB, :]
        for k in range(CONV_K):
            pr = dyb * _window(ubuf, ush, r * CONV_RB + off + k, CONV_RB)
            dwacc[8 * k:8 * k + 8, :] += jnp.sum(pr.reshape(CONV_RB // 8, 8, C), axis=0)


def _conv_bwd_final(n, sg_ref, dwacc, vacc):
    C = sg_ref.shape[-1]

    @pl.when(pl.program_id(0) == n - 1)
    def _():
        for k in range(CONV_K):
            sg_ref[pl.ds(k, 1), :] = jnp.sum(dwacc[8 * k:8 * k + 8, :], axis=0, keepdims=True)
        sg_ref[pl.ds(CONV_K, 1), :] = jnp.zeros((1, C), F32)
        sg_ref[ROW_CB:ROW_CB + 8, :] = jnp.zeros((8, C), F32)
        sg_ref[ROW_CB:ROW_CB + 3, :] = vacc[0:3, :]


def _ret_bwd_part(da_ref, p_ref, st_ref, rope_ref, m_ref, qdm_ref, kdm_ref, mqs_ref, cd_ref, dp_ref, gst):
    c, sl, sh = rope_ref[0], rope_ref[1], rope_ref[2]
    for j in range(HEADS // 2):
        rq = _rot(p_ref[:, 128 * j:128 * (j + 1)].astype(F32), c, sl, sh)
        rk = _rot(p_ref[:, QK + 128 * j:QK + 128 * (j + 1)].astype(F32), c, sl, sh)
        rkb = rk.astype(BF16)
        drq = jnp.zeros_like(rq)
        drk = jnp.zeros_like(rk)
        for e in range(2):
            h = 2 * j + e
            v = p_ref[:, 2 * QK + DV * h:2 * QK + DV * (h + 1)]
            g = p_ref[:, 2 * QK + VW + DV * h:2 * QK + VW + DV * (h + 1)].astype(F32)
            mqs = mqs_ref[h, 0:1, :]
            a = (rq * mqs).astype(BF16)
            aq = (rq * qdm_ref[h]).astype(BF16)
            kdv = (rk * kdm_ref[h]).astype(BF16)
            mk = m_ref[h]
            p = (_dot_nt(a, rkb) * mk).astype(BF16)
            stb = st_ref[h].astype(BF16)
            o = _dot(p, v) + _dot(aq, stb)
            _, vjp = jax.vjp(_gn_gate, o, g)
            do, dg = vjp(da_ref[:, DV * h:DV * (h + 1)].astype(F32))
            dob = do.astype(BF16)
            gs = gst[h]
            gsb = gs.astype(BF16)
            ds = (_dot_nt(dob, v) * mk).astype(BF16)
            drq = drq + _dot(ds, rkb) * mqs + _dot_nt(dob, stb) * qdm_ref[h]
            drk = drk + _dot_tn(ds, a) + _dot_nt(v, gsb) * kdm_ref[h]
            dv = _dot_tn(p, dob) + _dot(kdv, gsb)
            gst[h] = _dot_tn(aq, dob) + gs * cd_ref[h, 0:1, :]
            dp_ref[:, 2 * QK + DV * h:2 * QK + DV * (h + 1)] = dv.astype(BF16)
            dp_ref[:, 2 * QK + VW + DV * h:2 * QK + VW + DV * (h + 1)] = dg.astype(BF16)
        dp_ref[:, 128 * j:128 * (j + 1)] = _rot_t(drq, c, sl, sh).astype(BF16)
        dp_ref[:, QK + 128 * j:QK + 128 * (j + 1)] = _rot_t(drk, c, sl, sh).astype(BF16)


def _ret_bwd(dproj, da_ret, proj, states, tb, L, ride=None):
    T = proj.shape[0]
    nS = T // L

    def body(dpin_ref, dar_ref, p_ref, st_ref, rope_ref, m_ref, qdm_ref, kdm_ref, mqs_ref, cd_ref, dp_ref, gst):
        _zero_at_start(gst)
        _ret_bwd_part(dar_ref, p_ref, st_ref, rope_ref, m_ref, qdm_ref, kdm_ref, mqs_ref, cd_ref, dp_ref, gst)

    rev = lambda s: nS - 1 - s
    specs = _ret_specs(T, L)
    specs[0] = pl.BlockSpec((3, L, 128), lambda s: (0, rev(s), 0))
    ptile = pl.BlockSpec((L, PW), lambda s: (rev(s), 0))
    return _pcall(
        body, name="ret_bwd",
        out_shape=(jax.ShapeDtypeStruct(dproj.shape, BF16),),
        grid=(nS,),
        in_specs=[ANY, pl.BlockSpec((L, VW), lambda s: (rev(s), 0)), ptile,
                  pl.BlockSpec((None, HEADS, 128, DV), lambda s: (rev(s), 0, 0, 0))] + specs,
        out_specs=(ptile,),
        scratch_shapes=[pltpu.VMEM((HEADS, 128, DV), F32)],
        sem=("arbitrary",), aliases={0: 0}, ride=ride,
        args=(dproj, da_ret, proj, states, tb["rope"], tb["mask"], tb["qdm"], tb["kdm"], tb["mqs"], tb["cd"]))


def _conv_bwd(dproj, da_conv, y, proj, wcw, ln_g, ln_b, l, ride=None):
    T, C = y.shape
    Cc = wcw.shape[-1]
    tc = min(CONV_TILE, T)
    n = T // tc
    hb = tc // HALO

    def body(dpin_ref, dac_ref, y_ref, p_ref, ph_ref, cw_ref, lg_ref, lb_ref, dp_ref, sg_ref,
             dcbuf, ubuf, dubuf, wbuf, dwacc, vacc, dsh, ush):
        _conv_bwd_part(n, dac_ref, y_ref, p_ref, ph_ref, cw_ref, lg_ref, lb_ref, dp_ref,
                       dcbuf, ubuf, dubuf, wbuf, dwacc, vacc, dsh, ush)
        _conv_bwd_final(n, sg_ref, dwacc, vacc)

    rev = lambda t: n - 1 - t
    vec = pl.BlockSpec((None, 1, C), lambda t: (l, 0, 0))
    tile = pl.BlockSpec((tc, C), lambda t: (rev(t), 0))
    ptile = pl.BlockSpec((tc, 3 * C), lambda t: (rev(t), 1))
    halo = pl.BlockSpec((HALO, 3 * C), lambda t: (jnp.maximum(rev(t) * hb - 1, 0), 1))
    return _pcall(
        body, name="conv_bwd",
        out_shape=(jax.ShapeDtypeStruct(dproj.shape, BF16), jax.ShapeDtypeStruct((ROW_PRE, C), F32)),
        grid=(n,),
        in_specs=[ANY, tile, tile, ptile, halo, pl.BlockSpec((4, None, CONV_K, Cc), lambda t: (0, l, 0, 0)), vec, vec],
        out_specs=(ptile, pl.BlockSpec((ROW_PRE, C), lambda t: (0, 0))),
        scratch_shapes=[pltpu.VMEM((tc + HALO, C), F32), pltpu.VMEM((HALO + tc, C), F32), pltpu.VMEM((tc, C), F32),
                        pltpu.VMEM((HALO, C), F32), pltpu.VMEM((8 * CONV_K, C), F32), pltpu.VMEM((8, C), F32),
                        pltpu.VMEM((7, HALO + tc - 8, C), F32), pltpu.VMEM((7, HALO + tc - 8, C), F32)],
        sem=("arbitrary",), aliases={0: 0}, ride=ride, args=(dproj, da_conv, y, proj, proj, wcw, ln_g, ln_b))


def _win_grad(h, dproj, W):
    T, D = h.shape
    tk = min(2048, T)
    nk = T // tk

    def body(h_ref, dp_ref, g_ref, acc):
        k = pl.program_id(1)

        @pl.when(k == 0)
        def _():
            acc[...] = jnp.zeros_like(acc)

        acc[...] += _dot_tn(h_ref[...], dp_ref[...])

        @pl.when(k == nk - 1)
        def _():
            g_ref[...] = acc[...].astype(BF16)

    return pl.pallas_call(
        body, name="win_grad",
        out_shape=jax.ShapeDtypeStruct((4, D, W), BF16),
        grid=(4, nk),
        in_specs=[pl.BlockSpec((tk, D), lambda j, k: (k, 0)), pl.BlockSpec((tk, W), lambda j, k: (k, j))],
        out_specs=pl.BlockSpec((None, D, W), lambda j, k: (j, 0, 0)),
        scratch_shapes=[pltpu.VMEM((D, W), F32)],
        compiler_params=_cp(("arbitrary", "arbitrary")),
    )(h, dproj)


def _in_bwd(dxn, dproj, x, pre_g, win, l, ride=None):
    T, D = x.shape
    W = win.shape[-1]
    tm = min(1024, T)

    def body(dxn_ref, dp_ref, x_ref, g_ref, w_ref, dx_ref, dg_ref, acc):
        i = pl.program_id(0)
        j = pl.program_id(1)

        @pl.when(j == 0)
        def _():
            acc[...] = jnp.zeros_like(acc)

        @pl.when((i == 0) & (j == 0))
        def _():
            dg_ref[...] = jnp.zeros_like(dg_ref)

        acc[...] += _dot_nt(dp_ref[...], w_ref[...])

        @pl.when(j == 3)
        def _():
            _, vjp = jax.vjp(_rms, x_ref[...], g_ref[...])
            dx, dg = vjp(acc[...])
            dx_ref[...] = dxn_ref[...] + dx
            dg_ref[0:1, :] += dg

    tile = pl.BlockSpec((tm, D), lambda i, j: (i, 0))
    return _pcall(
        body, name="in_bwd",
        out_shape=(jax.ShapeDtypeStruct((T, D), F32), jax.ShapeDtypeStruct((8, D), F32)),
        grid=(T // tm, 4),
        in_specs=[tile, pl.BlockSpec((tm, W), lambda i, j: (i, j)), tile,
                  pl.BlockSpec((None, 1, D), lambda i, j: (l, 0, 0)),
                  pl.BlockSpec((None, D, W), lambda i, j: (j, 0, 0))],
        out_specs=(tile, pl.BlockSpec((8, D), lambda i, j: (0, 0))),
        scratch_shapes=[pltpu.VMEM((tm, D), F32)],
        sem=("arbitrary", "arbitrary"), args=(dxn, dproj, x, pre_g, win), ride=ride)


def _sum_group(chip, t, u):
    _, A, B = t.shape
    tr = min(256, A)

    def body(k_ref, t_ref, u_ref, o_ref):
        o_ref[...] = ((t_ref[...].astype(F32) + u_ref[0].astype(F32)) + u_ref[1].astype(F32)) + u_ref[2].astype(F32)

    return pl.pallas_call(
        body, name="sum_group",
        out_shape=jax.ShapeDtypeStruct((A, B), F32),
        grid_spec=pltpu.PrefetchScalarGridSpec(
            num_scalar_prefetch=1, grid=(A // tr,),
            in_specs=[pl.BlockSpec((None, tr, B), lambda i, k: (k[0], i, 0)),
                      pl.BlockSpec((3, tr, B), lambda i, k: (0, i, 0))],
            out_specs=pl.BlockSpec((tr, B), lambda i, k: (i, 0))),
        compiler_params=_cp(("arbitrary",)),
    )(chip, t, u)


def _swap_rows(g):
    _, A, B = g.shape
    nh = A // 2

    def body(g_ref, r_ref, send_sems, recv_sems):
        x, y, c = _place()
        cp = _remote(g_ref.at[:, pl.ds((1 - c) * nh, nh)], r_ref, send_sems, recv_sems, 0, (x, y, 1 - c))
        cp.start()
        cp.wait()

    return pl.pallas_call(
        body, name="swap_rows",
        out_shape=jax.ShapeDtypeStruct((4, nh, B), g.dtype),
        in_specs=[ANY], out_specs=ANY,
        scratch_shapes=[pltpu.SemaphoreType.DMA((1,)), pltpu.SemaphoreType.DMA((1,))],
        compiler_params=pltpu.CompilerParams(has_side_effects=True),
    )(g)


def _add_rows(cidx, g, r):
    _, nh, B = r.shape
    tr = min(256, nh)
    nb = nh // tr

    def body(c_ref, g_ref, r_ref, o_ref):
        o_ref[...] = (g_ref[...].astype(F32) + r_ref[...].astype(F32)).astype(BF16)

    blk = (None, tr, B)
    return pl.pallas_call(
        body, name="add_rows",
        out_shape=jax.ShapeDtypeStruct(r.shape, BF16),
        grid_spec=pltpu.PrefetchScalarGridSpec(
            num_scalar_prefetch=1, grid=(4, nb),
            in_specs=[pl.BlockSpec(blk, lambda k, i, c: (k, c[0] * nb + i, 0)),
                      pl.BlockSpec(blk, lambda k, i, c: (k, i, 0))],
            out_specs=pl.BlockSpec(blk, lambda k, i, c: (k, i, 0))),
        compiler_params=_cp(("arbitrary", "arbitrary")),
    )(cidx, g, r)


def _sum_group_half(chip, cidx, t, u):
    _, nh, B = t.shape
    tr = min(256, nh)
    nb = nh // tr

    def body(k_ref, c_ref, t_ref, u_ref, o_ref):
        mine = (pl.program_id(0) // nb) == c_ref[0]

        @pl.when(mine)
        def _():
            o_ref[...] = ((t_ref[...].astype(F32) + u_ref[0].astype(F32)) + u_ref[1].astype(F32)) + u_ref[2].astype(F32)

        @pl.when(jnp.logical_not(mine))
        def _():
            o_ref[...] = jnp.zeros_like(o_ref)

    own = lambda i, c: jnp.clip(i - c[0] * nb, 0, nb - 1)
    return pl.pallas_call(
        body, name="sum_group_half",
        out_shape=jax.ShapeDtypeStruct((2 * nh, B), F32),
        grid_spec=pltpu.PrefetchScalarGridSpec(
            num_scalar_prefetch=2, grid=(2 * nb,),
            in_specs=[pl.BlockSpec((None, tr, B), lambda i, k, c: (k[0], own(i, c), 0)),
                      pl.BlockSpec((3, tr, B), lambda i, k, c: (0, own(i, c), 0))],
            out_specs=pl.BlockSpec((tr, B), lambda i, k, c: (i, 0))),
        compiler_params=_cp(("arbitrary",)),
    )(chip, cidx, t, u)


def _adam_math(w, g, m, v):
    c1 = 1.0 / (1.0 - ADAM_B1 ** ADAM_STEP)
    c2 = 1.0 / (1.0 - ADAM_B2 ** ADAM_STEP)
    nm = ADAM_B1 * m + (1.0 - ADAM_B1) * g
    nv = ADAM_B2 * v + (1.0 - ADAM_B2) * (g * g)
    return -ADAM_LR * ((nm * c1) / (jnp.sqrt(nv * c2) + ADAM_EPS) + ADAM_WD * w), nm, nv


def _adamw_layer(prev, w, m, v, sa, sb, l, part, ride=None):
    NL, A, B = w.shape
    tr = A
    while tr * B * 4 > ADAM_BLOCK_BYTES and tr % 16 == 0:
        tr //= 2
    nb = A // tr

    def body(p0, p1, p2, p3, w_ref, m_ref, v_ref, sa_ref, sb_ref, g_ref, d_ref, nm_ref, nv_ref):
        g = sa_ref[...] + sb_ref[...]
        g_ref[...] = g
        d_ref[...], nm_ref[...], nv_ref[...] = _adam_math(w_ref[...], g, m_ref[...], v_ref[...])

    lay = pl.BlockSpec((None, tr, B), lambda i: (l, i, 0))
    src = pl.BlockSpec((tr, B), lambda i: (part * nb + i, 0))
    full = jax.ShapeDtypeStruct((NL, A, B), F32)
    if prev is None:
        prev = tuple(lax.empty((NL, A, B), F32) for _ in range(4))
    outs, landed = _pcall(
        body, name="adamw_layer",
        out_shape=(full,) * 4, grid=(nb,),
        in_specs=[ANY] * 4 + [lay, lay, lay, src, src], out_specs=(lay,) * 4,
        sem=("arbitrary",), aliases={0: 0, 1: 1, 2: 2, 3: 3}, args=(*prev, w, m, v, sa, sb), ride=ride)
    return tuple(outs), landed


def _adamw(w, g, m, v):
    shape = w.shape
    cols = shape[-1]
    rows = int(np.prod(shape[:-1]))

    def body(w_ref, g_ref, m_ref, v_ref, d_ref, nm_ref, nv_ref):
        d_ref[...], nm_ref[...], nv_ref[...] = _adam_math(w_ref[...], g_ref[...], m_ref[...], v_ref[...])

    tile = pl.BlockSpec((rows, cols), lambda i: (0, 0))
    out = jax.ShapeDtypeStruct((rows, cols), F32)
    res = pl.pallas_call(
        body, name="adamw",
        out_shape=(out, out, out), grid=(1,),
        in_specs=[tile] * 4, out_specs=(tile,) * 3,
        compiler_params=_cp(("arbitrary",)),
    )(*[a.reshape(rows, cols) for a in (w, g, m, v)])
    return tuple(a.reshape(shape) for a in res)


def _tail_exchange(small, s_in, s_sq):
    def body(s_ref, a_ref, b_ref, o_ref, oa_ref, ob_ref, send_sems, recv_sems, local_sem):
        x, y, c = _place()
        me = 4 * x + 2 * y + c
        sibling = (x, y, 1 - c)
        mine = pltpu.make_async_copy(s_ref, o_ref.at[me], local_sem)
        mine.start()
        swaps = [_remote(a_ref, oa_ref, send_sems, recv_sems, 7, sibling), _remote(b_ref, ob_ref, send_sems, recv_sems, 8, sibling)]
        sends = [_remote(s_ref, o_ref.at[me], send_sems, recv_sems, r, peer) for r, peer in enumerate(_peers(x, y, c))]
        for cp in swaps + sends:
            cp.start()
        for r, peer in enumerate(_peers(x, y, c)):
            theirs = o_ref.at[4 * peer[0] + 2 * peer[1] + peer[2]]
            _remote(theirs, theirs, send_sems, recv_sems, r, peer).wait_recv()
        for cp in sends:
            cp.wait_send()
        for cp in swaps:
            cp.wait()
        mine.wait()

    return pl.pallas_call(
        body, name="tail_exchange",
        out_shape=(jax.ShapeDtypeStruct((8,) + small.shape, small.dtype),
                   jax.ShapeDtypeStruct(s_in.shape, s_in.dtype), jax.ShapeDtypeStruct(s_sq.shape, s_sq.dtype)),
        in_specs=[ANY] * 3, out_specs=(ANY,) * 3,
        scratch_shapes=[pltpu.SemaphoreType.DMA((9,)), pltpu.SemaphoreType.DMA((9,)), pltpu.SemaphoreType.DMA],
        compiler_params=pltpu.CompilerParams(has_side_effects=True),
    )(small, s_in, s_sq)


def _sum_devices(gs):
    NL = len(gs)
    _, R, D = gs[0].shape

    def body(*refs):
        o_ref = refs[NL]
        for l in range(NL):
            acc = refs[l][0]
            for k in range(1, 8):
                acc = acc + refs[l][k]
            o_ref[l] = acc

    return pl.pallas_call(
        body, name="sum_devices",
        out_shape=jax.ShapeDtypeStruct((NL, R, D), F32),
        grid=(1,),
        in_specs=[pl.BlockSpec((8, R, D), lambda i: (0, 0, 0))] * NL,
        out_specs=pl.BlockSpec((NL, R, D), lambda i: (0, 0, 0)),
        compiler_params=_cp(("arbitrary",)),
    )(*gs)


def kernel(x, pre_norm_g, w_in, w_ret_out, conv_w, conv_b, conv_ln_g, conv_ln_b, w_conv_out, w_o, post_norm_g, loss_target, m_pre_norm_g, m_w_in, m_w_ret_out, m_conv_w, m_conv_b, m_conv_ln_g, m_conv_ln_b, m_w_conv_out, m_w_o, m_post_norm_g, v_pre_norm_g, v_w_in, v_w_ret_out, v_conv_w, v_conv_b, v_conv_ln_g, v_conv_ln_b, v_w_conv_out, v_w_o, v_post_norm_g):
    NL, D, W = w_in.shape
    Cc = conv_w.shape[-1]
    T = x.shape[1]
    L = min(RET_BLOCK, T)
    tb = _tables(T, L)
    ax, ay, ac = _place()
    chip = (2 * ax + ay).astype(jnp.int32).reshape(1)
    cidx = ac.astype(jnp.int32).reshape(1)
    pre_g, cb, lg, lb, post_g = (a.reshape(NL, 1, D) for a in (pre_norm_g, conv_b, conv_ln_g, conv_ln_b, post_norm_g))

    win = [_cast_win(chip, w_in, l) for l in range(NL)]
    wsq = [_cast_wsq(chip, w_ret_out, w_conv_out, w_o, l) for l in range(NL)]
    win[0], wcw = _gather_first(win[0], _place_cw(chip, conv_w))

    saved = []
    xl = x[0]
    for l in range(NL):
        more = l + 1 < NL
        ride = _Ride()
        if more:
            _ride_gather_ici(ride, "win", win[l + 1], (0, 3, 4))
        (proj, h), got = _fwd_in(xl, pre_g, win[l], l, ride=ride)
        if more:
            win[l + 1] = got["win"]
        ride = _Ride()
        if l == 0:
            _ride_gather_direct(ride, "wsq0", wsq[0])
        (a_ret, states), got = _ret_fwd(proj, tb, L, ride=ride)
        if l == 0:
            wsq[0] = got["wsq0"]
        ride = _Ride()
        if more:
            _ride_gather_ici(ride, "win", win[l + 1], (3, 4, 4))
            _ride_gather_ici(ride, "wsq", wsq[l + 1])
        (a_conv, y), got = _conv_fwd(proj, wcw, cb, lg, lb, l, ride=ride)
        if more:
            win[l + 1], wsq[l + 1] = got["win"], got["wsq"]
        ride = _Ride()
        if more:
            _ride_gather_pass(ride, "win", win[l + 1])
            _ride_gather_pass(ride, "wsq", wsq[l + 1])
        (xn, ro, co, ym, z), got = _merge_fwd(xl, proj, a_ret, a_conv, wsq[l], post_g, l, ride=ride)
        if more:
            win[l + 1], wsq[l + 1] = got["win"], got["wsq"]
        saved.append((xl, proj, h, a_ret, states, a_conv, y, ro, co, ym, z))
        xl = xn
    dx, lsum = _loss_fwd_bwd(xl, loss_target[0])
    loss = lax.psum(jnp.sum(lsum), ("x", "y", "c"))

    gin, gsq, uin, usq = [None] * NL, [None] * NL, [None] * NL, [None] * NL
    s_in, s_sq, o_in, o_sq = [None] * NL, [None] * NL, [None] * NL, [None] * NL
    small, gs = [None] * NL, [None] * NL
    for l in reversed(range(NL)):
        xin, proj, h, a_ret, states, a_conv, y, ro, co, ym, z = saved[l]
        (dproj, da_ret, da_conv, gsq[l], dpost), _ = _merge_bwd(dx, proj, a_ret, a_conv, ro, co, ym, z, wsq[l], post_g, l)
        ride = _Ride()
        _ride_exchange(ride, "gsq", gsq[l], "usq", None, (0, 1, 2))
        if l + 1 < NL:
            _ride_exchange(ride, "gin", gin[l + 1], "uin", uin[l + 1], (2,))
        (dproj, sg), got = _conv_bwd(dproj, da_conv, y, proj, wcw, lg, lb, l, ride=ride)
        usq[l] = got["usq"]
        if l + 1 < NL:
            uin[l + 1] = got["uin"]
        ride = _Ride()
        if l + 1 < NL:
            _ride_gather_all(ride, "small", small[l + 1], "gs")
        (dproj,), got = _ret_bwd(dproj, da_ret, proj, states, tb, L, ride=ride)
        if l + 1 < NL:
            gs[l + 1] = got["gs"]
        gin[l] = _win_grad(h, dproj, W)
        ride = _Ride()
        if l > 0:
            _ride_exchange(ride, "gin", gin[l], "uin", None, (0, 1))
        else:
            gin[0] = _add_rows(cidx, gin[0], _swap_rows(gin[0]))
            _ride_exchange(ride, "gin", gin[0], "uin", None, (0, 1, 2))
        if l + 1 < NL:
            s_in[l + 1] = _sum_group(chip, gin[l + 1], uin[l + 1])
            s_sq[l + 1] = _sum_group(chip, gsq[l + 1], usq[l + 1])
            _ride_swap(ride, "s_in", s_in[l + 1], "o_in")
            _ride_swap(ride, "s_sq", s_sq[l + 1], "o_sq")
        (dx, dpre), got = _in_bwd(dx, dproj, xin, pre_g, win[l], l, ride=ride)
        uin[l] = got["uin"]
        if l + 1 < NL:
            o_in[l + 1], o_sq[l + 1] = got["o_in"], got["o_sq"]
        small[l] = jnp.concatenate([sg, dpre, dpost], axis=0)
    grad_x = dx

    big = {"w_in": None, "w_ret_out": None, "w_conv_out": None, "w_o": None}
    wts = {"w_in": (w_in, m_w_in, v_w_in), "w_ret_out": (w_ret_out, m_w_ret_out, v_w_ret_out),
           "w_conv_out": (w_conv_out, m_w_conv_out, v_w_conv_out), "w_o": (w_o, m_w_o, v_w_o)}
    sq_names = ("w_ret_out", "w_conv_out", "w_o")

    def adam_in(l, ride=None):
        big["w_in"], got = _adamw_layer(big["w_in"], *wts["w_in"], s_in[l], o_in[l], l, 0, ride=ride)
        return got

    def adam_sq(l, part, ride=None):
        n = sq_names[part]
        big[n], got = _adamw_layer(big[n], *wts[n], s_sq[l], o_sq[l], l, part, ride=ride)
        return got

    s_in[0] = _sum_group_half(chip, cidx, gin[0], uin[0])
    s_sq[0] = _sum_group(chip, gsq[0], usq[0])
    gs[0], o_in[0], o_sq[0] = _tail_exchange(small[0], s_in[0], s_sq[0])
    for l in reversed(range(NL)):
        adam_in(l)
        for part in range(3):
            adam_sq(l, part)

    gsm = _sum_devices(gs)
    grads = {
        "pre_norm_g": gsm[:, ROW_PRE], "conv_w": lax.dynamic_slice_in_dim(gsm[:, 0:CONV_K], chip[0] * Cc, Cc, axis=2),
        "conv_b": gsm[:, ROW_CB], "conv_ln_g": gsm[:, ROW_LG], "conv_ln_b": gsm[:, ROW_LB], "post_norm_g": gsm[:, ROW_POST],
    }
    weights = dict(pre_norm_g=pre_norm_g, conv_w=conv_w, conv_b=conv_b, conv_ln_g=conv_ln_g, conv_ln_b=conv_ln_b,
                   post_norm_g=post_norm_g)
    m1 = dict(pre_norm_g=m_pre_norm_g, conv_w=m_conv_w, conv_b=m_conv_b, conv_ln_g=m_conv_ln_g, conv_ln_b=m_conv_ln_b,
              post_norm_g=m_post_norm_g)
    m2 = dict(pre_norm_g=v_pre_norm_g, conv_w=v_conv_w, conv_b=v_conv_b, conv_ln_g=v_conv_ln_g, conv_ln_b=v_conv_ln_b,
              post_norm_g=v_post_norm_g)
    res = {n: (grads[n],) + _adamw(weights[n], grads[n], m1[n], m2[n]) for n in grads}
    res.update(big)
    order = ["pre_norm_g", "w_in", "w_ret_out", "conv_w", "conv_b", "conv_ln_g", "conv_ln_b", "w_conv_out", "w_o", "post_norm_g"]
    return (loss, grad_x[None], *[res[n][0] for n in order], *[res[n][1] for n in order],
            *[res[n][2] for n in order], *[res[n][3] for n in order])
```

```python
import numpy as np
import jax
import jax.numpy as jnp
from jax import lax
from jax.experimental import pallas as pl
from jax.experimental.pallas import tpu as pltpu

F32 = jnp.float32
BF16 = jnp.bfloat16

HEADS = 8
DK = 64
DV = 128
CONV_K = 31
CHUNK = 64
ROPE_BASE = 10000.0
EPS = 1e-6
HALO = 32
CONV_RB = 32
CONV_LANES = 512
CONV_TILE = 256
RET_BLOCK = 512

ADAM_LR = 0.001
ADAM_B1 = 0.9
ADAM_B2 = 0.999
ADAM_EPS = 1e-08
ADAM_WD = 0.01
ADAM_STEP = 10
ADAM_BLOCK_BYTES = 2 * 1024 * 1024

VMEM_LIMIT = 56 * 1024 * 1024
MESH_T = pl.DeviceIdType.MESH
ANY = pl.BlockSpec(memory_space=pl.ANY)

ROW_CB, ROW_LG, ROW_LB = 32, 33, 34
ROW_PRE, ROW_POST = 40, 48


def _cp(sem=None, **kw):
    return pltpu.CompilerParams(dimension_semantics=sem, vmem_limit_bytes=VMEM_LIMIT, **kw)


def _dot(a, b):
    return jnp.dot(a, b, preferred_element_type=F32)


def _dot_nt(a, b):
    return lax.dot_general(a, b, (((1,), (1,)), ((), ())), preferred_element_type=F32)


def _dot_tn(a, b):
    return lax.dot_general(a, b, (((0,), (0,)), ((), ())), preferred_element_type=F32)


def _sigmoid(x):
    return jax.nn.sigmoid(x)


def _silu(x):
    return x * _sigmoid(x)


def _rms(x, g):
    return x * lax.rsqrt(jnp.mean(x * x, axis=-1, keepdims=True) + EPS) * g


def _gn_gate(o, g):
    mu = jnp.mean(o, axis=-1, keepdims=True)
    d = o - mu
    var = jnp.mean(d * d, axis=-1, keepdims=True)
    return d * lax.rsqrt(var + EPS) * _silu(g)


def _ln_gate(y, gc, lg, lb):
    mu = jnp.mean(y, axis=-1, keepdims=True)
    d = y - mu
    var = jnp.mean(d * d, axis=-1, keepdims=True)
    return _silu(d * lax.rsqrt(var + EPS) * lg + lb) * _silu(gc)


def _tables(T, L):
    lane = np.arange(128)
    d = lane % DK
    half = DK // 2
    inv = (ROPE_BASE ** (-(np.arange(half, dtype=np.float32)) / half)).astype(np.float32)
    ang = (np.arange(T, dtype=np.float32)[:, None] * inv[None, :]).astype(np.float64)
    angl = ang[:, d % half]
    cos = np.cos(angl)
    sin = np.sin(angl)
    lo = (d < half)[None, :]
    rope = np.stack([cos, np.where(lo, -sin, 0.0), np.where(lo, 0.0, sin)]).astype(np.float32)

    hh = np.arange(HEADS, dtype=np.float64)
    log_g = np.log1p(-np.exp2(-5.0 - hh))
    n = np.arange(L, dtype=np.float64)
    cn = np.arange(L) // CHUNK
    allowed = (cn[None, :] <= cn[:, None])
    dist = np.abs(n[:, None] - n[None, :])
    mask = np.exp(log_g[:, None, None] * dist[None]) * allowed[None]
    mq = ((lane[None, :] // DK) == (np.arange(HEADS)[:, None] % 2)).astype(np.float64)
    qd = np.exp(log_g[:, None] * n[None, :])
    kd = np.exp(log_g[:, None] * (L - n[None, :]))
    qdm = qd[:, :, None] * mq[:, None, :] * (DK ** -0.5)
    kdm = kd[:, :, None] * mq[:, None, :]
    mqs = np.broadcast_to((mq * (DK ** -0.5))[:, None, :], (HEADS, 8, 128))
    cd = np.broadcast_to(np.exp(log_g * L)[:, None, None], (HEADS, 8, 128))
    f = lambda a: jnp.asarray(np.ascontiguousarray(a), dtype=F32)
    return dict(rope=f(rope), mask=f(mask), qdm=f(qdm), kdm=f(kdm), mqs=f(mqs), cd=f(cd))


def _rot(b, c, sl, sh):
    return b * c + pltpu.roll(b, 96, axis=1) * sl + pltpu.roll(b, 32, axis=1) * sh


def _rot_t(d, c, sl, sh):
    return d * c + pltpu.roll(d * sl, 32, axis=1) + pltpu.roll(d * sh, 96, axis=1)


def _place():
    return lax.axis_index("x"), lax.axis_index("y"), lax.axis_index("c")


def _other_chips(x, y):
    return [(1 - x, y), (x, 1 - y), (1 - x, 1 - y)]


def _remote(src, dst, send_sems, recv_sems, k, to):
    return pltpu.make_async_remote_copy(src_ref=src, dst_ref=dst, send_sem=send_sems.at[k], recv_sem=recv_sems.at[k],
                                        device_id=to, device_id_type=MESH_T)


class _Ride:
    def __init__(self):
        self.arrays, self.kinds, self.names = [], [], []
        self.fresh = []
        self.ops = []

    def read(self, name, a):
        self.names.append(name)
        self.arrays.append(a)
        self.kinds.append("in")

    def inout(self, name, a):
        self.names.append(name)
        self.arrays.append(a)
        self.kinds.append("inout")

    def land(self, name, shape, dtype):
        self.fresh.append((name, jax.ShapeDtypeStruct(shape, dtype)))

    def op(self, n_sems, start, finish):
        self.ops.append((n_sems, start, finish))


def _pcall(body, *, name, grid, in_specs, out_specs, out_shape, args, scratch_shapes=(), sem, aliases=None, ride=None):
    if ride is None or not ride.ops:
        outs = pl.pallas_call(body, name=name, grid=grid, in_specs=list(in_specs), out_specs=tuple(out_specs),
                              out_shape=tuple(out_shape), scratch_shapes=list(scratch_shapes),
                              input_output_aliases=dict(aliases or {}), compiler_params=_cp(sem))(*args)
        return outs, {}

    ni, no, nr = len(args), len(out_shape), len(ride.arrays)
    inout = [i for i, k in enumerate(ride.kinds) if k == "inout"]
    r_out_shapes = [jax.ShapeDtypeStruct(ride.arrays[i].shape, ride.arrays[i].dtype) for i in inout] + [s for _, s in ride.fresh]
    r_out_names = [ride.names[i] for i in inout] + [n for n, _ in ride.fresh]
    nro = len(r_out_shapes)
    n_sems = sum(n for n, _, _ in ride.ops)
    n_scr = len(scratch_shapes)
    nd = len(grid)

    def wrapped(*refs):
        ins, rin = refs[:ni], refs[ni:ni + nr]
        outs, rout = refs[ni + nr:ni + nr + no], refs[ni + nr + no:ni + nr + no + nro]
        scr = refs[ni + nr + no + nro:ni + nr + no + nro + n_scr]
        send_sems, recv_sems = refs[-2], refs[-1]
        view = {nm: r for nm, r, k in zip(ride.names, rin, ride.kinds) if k == "in"}
        view.update(dict(zip(r_out_names, rout)))
        first = pl.program_id(0) == 0
        last = pl.program_id(0) == grid[0] - 1
        for d in range(1, nd):
            first = first & (pl.program_id(d) == 0)
            last = last & (pl.program_id(d) == grid[d] - 1)

        @pl.when(first)
        def _():
            base = 0
            for n, start, _ in ride.ops:
                start(view, send_sems, recv_sems, base)
                base += n

        body(*ins, *outs, *scr)

        @pl.when(last)
        def _():
            base = 0
            for n, _, finish in ride.ops:
                finish(view, send_sems, recv_sems, base)
                base += n

    res = pl.pallas_call(
        wrapped, name=name, grid=grid,
        in_specs=list(in_specs) + [ANY] * nr, out_specs=tuple(out_specs) + (ANY,) * nro,
        out_shape=tuple(out_shape) + tuple(r_out_shapes),
        scratch_shapes=list(scratch_shapes) + [pltpu.SemaphoreType.DMA((n_sems,)), pltpu.SemaphoreType.DMA((n_sems,))],
        input_output_aliases={**dict(aliases or {}), **{ni + i: no + j for j, i in enumerate(inout)}},
        compiler_params=_cp(sem),
    )(*args, *ride.arrays)
    return res[:no], dict(zip(r_out_names, res[no:]))


def _half(ref, chip_idx, cc, part=(0, 1, 1)):
    n = ref.shape[1] // 2
    lo, hi, k = part
    return ref.at[chip_idx, pl.ds(cc * n + lo * n // k, (hi - lo) * n // k)]


def _ride_gather_ici(ride, name, a, part=(0, 1, 1)):
    ride.inout(name, a)

    def start(view, ss, rs, b):
        x, y, c = _place()
        mine = _half(view[name], 2 * x + y, c, part)
        for j, (cx, cy) in enumerate(_other_chips(x, y)):
            _remote(mine, mine, ss, rs, b + j, (cx, cy, c)).start()

    def finish(view, ss, rs, b):
        x, y, c = _place()
        mine = _half(view[name], 2 * x + y, c, part)
        for j, (cx, cy) in enumerate(_other_chips(x, y)):
            theirs = _half(view[name], 2 * cx + cy, c, part)
            _remote(theirs, theirs, ss, rs, b + j, (cx, cy, c)).wait_recv()
        for j, (cx, cy) in enumerate(_other_chips(x, y)):
            _remote(mine, mine, ss, rs, b + j, (cx, cy, c)).wait_send()

    ride.op(3, start, finish)


def _ride_gather_direct(ride, name, a):
    ride.inout(name, a)

    def start(view, ss, rs, b):
        x, y, c = _place()
        mine = _half(view[name], 2 * x + y, c)
        for j, (cx, cy) in enumerate(_other_chips(x, y)):
            _remote(mine, mine, ss, rs, b + 2 * j, (cx, cy, c)).start()
            _remote(mine, mine, ss, rs, b + 2 * j + 1, (cx, cy, 1 - c)).start()

    def finish(view, ss, rs, b):
        x, y, c = _place()
        mine = _half(view[name], 2 * x + y, c)
        for j, (cx, cy) in enumerate(_other_chips(x, y)):
            same = _half(view[name], 2 * cx + cy, c)
            other = _half(view[name], 2 * cx + cy, 1 - c)
            _remote(same, same, ss, rs, b + 2 * j, (cx, cy, c)).wait_recv()
            _remote(other, other, ss, rs, b + 2 * j + 1, (cx, cy, 1 - c)).wait_recv()
        for j, (cx, cy) in enumerate(_other_chips(x, y)):
            _remote(mine, mine, ss, rs, b + 2 * j, (cx, cy, c)).wait_send()
            _remote(mine, mine, ss, rs, b + 2 * j + 1, (cx, cy, 1 - c)).wait_send()

    ride.op(6, start, finish)


def _ride_gather_pass(ride, name, a):
    ride.inout(name, a)

    def start(view, ss, rs, b):
        x, y, c = _place()
        for j, (cx, cy) in enumerate(_other_chips(x, y)):
            blk = _half(view[name], 2 * cx + cy, c)
            _remote(blk, blk, ss, rs, b + j, (x, y, 1 - c)).start()

    def finish(view, ss, rs, b):
        x, y, c = _place()
        for j, (cx, cy) in enumerate(_other_chips(x, y)):
            theirs = _half(view[name], 2 * cx + cy, 1 - c)
            _remote(theirs, theirs, ss, rs, b + j, (x, y, 1 - c)).wait_recv()
        for j, (cx, cy) in enumerate(_other_chips(x, y)):
            blk = _half(view[name], 2 * cx + cy, c)
            _remote(blk, blk, ss, rs, b + j, (x, y, 1 - c)).wait_send()

    ride.op(3, start, finish)


def _ride_exchange(ride, src_name, src, dst_name, dst, rels):
    ride.read(src_name, src)
    if dst is None:
        ride.land(dst_name, (3,) + src.shape[1:], src.dtype)
    else:
        ride.inout(dst_name, dst)

    def start(view, ss, rs, b):
        x, y, c = _place()
        chips = _other_chips(x, y)
        for i, j in enumerate(rels):
            cx, cy = chips[j]
            _remote(view[src_name].at[2 * cx + cy], view[dst_name].at[j], ss, rs, b + i, (cx, cy, c)).start()

    def finish(view, ss, rs, b):
        x, y, c = _place()
        chips = _other_chips(x, y)
        for i, j in enumerate(rels):
            cx, cy = chips[j]
            _remote(view[src_name].at[2 * cx + cy], view[dst_name].at[j], ss, rs, b + i, (cx, cy, c)).wait()

    ride.op(len(rels), start, finish)


def _ride_swap(ride, src_name, src, dst_name):
    ride.read(src_name, src)
    ride.land(dst_name, src.shape, src.dtype)

    def start(view, ss, rs, b):
        x, y, c = _place()
        _remote(view[src_name], view[dst_name], ss, rs, b, (x, y, 1 - c)).start()

    def finish(view, ss, rs, b):
        x, y, c = _place()
        _remote(view[src_name], view[dst_name], ss, rs, b, (x, y, 1 - c)).wait()

    ride.op(1, start, finish)


def _peers(x, y, c):
    flip = lambda v, b: 1 - v if b else v
    return [(flip(x, r & 4), flip(y, r & 2), flip(c, r & 1)) for r in range(1, 8)]


def _ride_gather_all(ride, src_name, src, dst_name):
    ride.read(src_name, src)
    ride.land(dst_name, (8,) + src.shape, src.dtype)

    def start(view, ss, rs, b):
        x, y, c = _place()
        me = 4 * x + 2 * y + c
        pltpu.make_async_copy(view[src_name], view[dst_name].at[me], ss.at[b + 7]).start()
        for r, peer in enumerate(_peers(x, y, c)):
            _remote(view[src_name], view[dst_name].at[me], ss, rs, b + r, peer).start()

    def finish(view, ss, rs, b):
        x, y, c = _place()
        me = 4 * x + 2 * y + c
        for r, peer in enumerate(_peers(x, y, c)):
            theirs = view[dst_name].at[4 * peer[0] + 2 * peer[1] + peer[2]]
            _remote(theirs, theirs, ss, rs, b + r, peer).wait_recv()
        for r, peer in enumerate(_peers(x, y, c)):
            _remote(view[src_name], view[dst_name].at[me], ss, rs, b + r, peer).wait_send()
        pltpu.make_async_copy(view[src_name], view[dst_name].at[me], ss.at[b + 7]).wait()

    ride.op(8, start, finish)


def _cast_win(chip, w_in, l):
    _, D, W = w_in.shape
    tr = min(256, D)

    def body(chip_ref, w_ref, o_ref):
        o_ref[...] = w_ref[...].astype(BF16)

    return pl.pallas_call(
        body, name="cast_win",
        out_shape=jax.ShapeDtypeStruct((4, D, W), BF16),
        grid_spec=pltpu.PrefetchScalarGridSpec(
            num_scalar_prefetch=1, grid=(D // tr,),
            in_specs=[pl.BlockSpec((None, tr, W), lambda r, c: (l, r, 0))],
            out_specs=pl.BlockSpec((None, tr, W), lambda r, c: (c[0], r, 0))),
        compiler_params=_cp(("arbitrary",)),
    )(chip, w_in)


def _cast_wsq(chip, w_ro, w_co, w_o, l):
    _, R, D = w_ro.shape

    def body(chip_ref, a_ref, b_ref, c_ref, o_ref):
        o_ref[0:R, :] = a_ref[...].astype(BF16)
        o_ref[R:2 * R, :] = b_ref[...].astype(BF16)
        o_ref[2 * R:3 * R, :] = c_ref[...].astype(BF16)

    spec = pl.BlockSpec((None, R, D), lambda i, c: (l, 0, 0))
    return pl.pallas_call(
        body, name="cast_wsq",
        out_shape=jax.ShapeDtypeStruct((4, 3 * R, D), BF16),
        grid_spec=pltpu.PrefetchScalarGridSpec(
            num_scalar_prefetch=1, grid=(1,),
            in_specs=[spec, spec, spec],
            out_specs=pl.BlockSpec((None, 3 * R, D), lambda i, c: (c[0], 0, 0))),
        compiler_params=_cp(("arbitrary",)),
    )(chip, w_ro, w_co, w_o)


def _place_cw(chip, conv_w):
    NL, K, Cc = conv_w.shape

    def body(chip_ref, w_ref, o_ref):
        o_ref[...] = w_ref[...]

    return pl.pallas_call(
        body, name="place_cw",
        out_shape=jax.ShapeDtypeStruct((4, NL, K, Cc), F32),
        grid_spec=pltpu.PrefetchScalarGridSpec(
            num_scalar_prefetch=1, grid=(1,),
            in_specs=[pl.BlockSpec((NL, K, Cc), lambda i, c: (0, 0, 0))],
            out_specs=pl.BlockSpec((None, NL, K, Cc), lambda i, c: (c[0], 0, 0, 0))),
        compiler_params=_cp(("arbitrary",)),
    )(chip, conv_w)


def _gather_first(win0, wcw):
    n_arr = 2

    def body(a0, a1, o0, o1, send_sems, recv_sems):
        x, y, c = _place()
        sibling = (x, y, 1 - c)
        chips = _other_chips(x, y)
        outs = (o0, o1)

        def copy(k, a, cx, cy, cc, to):
            blk = _half(outs[a], 2 * cx + cy, cc)
            return _remote(blk, blk, send_sems, recv_sems, k, to)

        first = [copy(3 * a + j, a, x, y, c, (*chip, c)) for a in range(n_arr) for j, chip in enumerate(chips)]
        for cp in first:
            cp.start()
        passed = [copy(3 * n_arr + 3 * a + j, a, *chip, c, sibling) for a in range(n_arr) for j, chip in enumerate(chips)]
        for a in range(n_arr):
            for j, chip in enumerate(chips):
                copy(3 * a + j, a, *chip, c, sibling).wait_recv()
                passed[3 * a + j].start()
        for a in range(n_arr):
            for j, chip in enumerate(chips):
                copy(3 * n_arr + 3 * a + j, a, *chip, 1 - c, sibling).wait_recv()
        for cp in first + passed:
            cp.wait_send()

    ins = (win0, wcw)
    return pl.pallas_call(
        body, name="gather_first",
        out_shape=tuple(jax.ShapeDtypeStruct(a.shape, a.dtype) for a in ins),
        in_specs=[ANY] * n_arr, out_specs=(ANY,) * n_arr,
        scratch_shapes=[pltpu.SemaphoreType.DMA((6 * n_arr,)), pltpu.SemaphoreType.DMA((6 * n_arr,))],
        input_output_aliases={0: 0, 1: 1},
        compiler_params=pltpu.CompilerParams(has_side_effects=True),
    )(*ins)


def _fwd_in(x, pre_g, win, l, ride=None):
    T, D = x.shape
    W = win.shape[-1]
    tm = min(1024, T)

    def body(x_ref, g_ref, w_ref, p_ref, h_ref, hs):
        @pl.when(pl.program_id(1) == 0)
        def _():
            hb = _rms(x_ref[...], g_ref[...]).astype(BF16)
            hs[...] = hb
            h_ref[...] = hb

        p_ref[...] = _dot(hs[...], w_ref[...]).astype(BF16)

    return _pcall(
        body, name="fwd_in",
        out_shape=(jax.ShapeDtypeStruct((T, 4 * W), BF16), jax.ShapeDtypeStruct((T, D), BF16)),
        grid=(T // tm, 4),
        in_specs=[pl.BlockSpec((tm, D), lambda i, j: (i, 0)),
                  pl.BlockSpec((None, 1, D), lambda i, j: (l, 0, 0)),
                  pl.BlockSpec((None, D, W), lambda i, j: (j, 0, 0))],
        out_specs=(pl.BlockSpec((tm, W), lambda i, j: (i, j)),
                   pl.BlockSpec((tm, D), lambda i, j: (i, 0))),
        scratch_shapes=[pltpu.VMEM((tm, D), BF16)],
        sem=("arbitrary", "arbitrary"), args=(x, pre_g, win), ride=ride)


def _ret_specs(T, L):
    rope = pl.BlockSpec((3, L, 128), lambda s: (0, s, 0))
    mask = pl.BlockSpec((HEADS, L, L), lambda s: (0, 0, 0), pipeline_mode=pl.Buffered(1))
    qdm = pl.BlockSpec((HEADS, L, 128), lambda s: (0, 0, 0), pipeline_mode=pl.Buffered(1))
    small = pl.BlockSpec((HEADS, 8, 128), lambda s: (0, 0, 0))
    return [rope, mask, qdm, qdm, small, small]


QK = HEADS * DK
VW = HEADS * DV
PW = 2 * QK + 2 * VW


def _zero_at_start(ref):
    @pl.when(pl.program_id(0) == 0)
    def _():
        ref[...] = jnp.zeros_like(ref)


def _ret_fwd_part(p_ref, rope_ref, m_ref, qdm_ref, kdm_ref, mqs_ref, cd_ref, a_ref, st_ref, state):
    c, sl, sh = rope_ref[0], rope_ref[1], rope_ref[2]
    for j in range(HEADS // 2):
        rq = _rot(p_ref[:, 128 * j:128 * (j + 1)].astype(F32), c, sl, sh)
        rk = _rot(p_ref[:, QK + 128 * j:QK + 128 * (j + 1)].astype(F32), c, sl, sh)
        rkb = rk.astype(BF16)
        for e in range(2):
            h = 2 * j + e
            v = p_ref[:, 2 * QK + DV * h:2 * QK + DV * (h + 1)]
            g = p_ref[:, 2 * QK + VW + DV * h:2 * QK + VW + DV * (h + 1)].astype(F32)
            a = (rq * mqs_ref[h, 0:1, :]).astype(BF16)
            p = (_dot_nt(a, rkb) * m_ref[h]).astype(BF16)
            st = state[h]
            st_ref[h] = st
            o = _dot(p, v) + _dot((rq * qdm_ref[h]).astype(BF16), st.astype(BF16))
            state[h] = st * cd_ref[h, 0:1, :] + _dot_tn((rk * kdm_ref[h]).astype(BF16), v)
            a_ref[:, DV * h:DV * (h + 1)] = _gn_gate(o, g).astype(BF16)


def _shift_copies(src, sh):
    rows = sh.shape[1]
    for b in range(1, 8):
        sh[b - 1, :, :] = src[pl.ds(b, rows), :]


def _window(src, sh, r0, const, rows, lanes):
    b = const % 8
    at = pl.ds(pl.multiple_of(r0 + (const - b), 8), rows)
    w = src[at, lanes] if b == 0 else sh[b - 1, at, lanes]
    return w.reshape(rows // 8, 8, w.shape[-1])


def _conv_taps(wbuf, src, sh, r0, const, rows, lanes):
    groups = rows // 8
    accs = [None] * groups
    for k in range(CONV_K):
        w8 = wbuf[pl.ds(8 * k, 8), lanes]
        win = _window(src, sh, r0, const + k, rows, lanes)
        for g in range(groups):
            term = w8 * win[g]
            accs[g] = term if k == 0 else accs[g] + term
    return jnp.concatenate(accs, axis=0)


def _lane_parts(C):
    return [pl.ds(j * CONV_LANES, CONV_LANES) for j in range(C // CONV_LANES)]


def _load_conv_w(cw_ref, wbuf, flip):
    for k in range(CONV_K):
        row = jnp.concatenate([cw_ref[c, pl.ds(k, 1), :] for c in range(4)], axis=-1)
        kk = CONV_K - 1 - k if flip else k
        wbuf[pl.ds(8 * kk, 8), :] = jnp.broadcast_to(row, (8, row.shape[-1]))


def _conv_fwd_part(p_ref, cw_ref, cb_ref, lg_ref, lb_ref, a_ref, y_ref, ubuf, wbuf, ush):
    tc, C = y_ref.shape
    off = HALO - (CONV_K - 1)
    i = pl.program_id(0)

    @pl.when(i == 0)
    def _():
        ubuf[0:HALO, :] = jnp.zeros((HALO, C), F32)
        _load_conv_w(cw_ref, wbuf, False)

    @pl.when(i > 0)
    def _():
        ubuf[0:HALO, :] = ubuf[tc:tc + HALO, :]

    ga = p_ref[:, 0:C].astype(F32)
    gb = p_ref[:, C:2 * C].astype(F32)
    ubuf[HALO:HALO + tc, :] = ga * _sigmoid(gb)
    _shift_copies(ubuf, ush)

    def rows_block(r, carry):
        r0 = pl.multiple_of(r * CONV_RB, CONV_RB)
        for lanes in _lane_parts(C):
            y_ref[pl.ds(r0, CONV_RB), lanes] = _conv_taps(wbuf, ubuf, ush, r0, off, CONV_RB, lanes) + cb_ref[:, lanes]
        return carry

    lax.fori_loop(0, tc // CONV_RB, rows_block, 0)
    gc = p_ref[:, 2 * C:3 * C].astype(F32)
    a_ref[...] = _ln_gate(y_ref[...], gc, lg_ref[...], lb_ref[...]).astype(BF16)


def _ret_fwd(proj, tb, L, ride=None):
    T = proj.shape[0]
    nS = T // L

    def body(p_ref, rope_ref, m_ref, qdm_ref, kdm_ref, mqs_ref, cd_ref, ar_ref, st_ref, state):
        _zero_at_start(state)
        _ret_fwd_part(p_ref, rope_ref, m_ref, qdm_ref, kdm_ref, mqs_ref, cd_ref, ar_ref, st_ref, state)

    return _pcall(
        body, name="ret_fwd",
        out_shape=(jax.ShapeDtypeStruct((T, VW), BF16), jax.ShapeDtypeStruct((nS, HEADS, 128, DV), F32)),
        grid=(nS,),
        in_specs=[pl.BlockSpec((L, PW), lambda s: (s, 0))] + _ret_specs(T, L),
        out_specs=(pl.BlockSpec((L, VW), lambda s: (s, 0)), pl.BlockSpec((None, HEADS, 128, DV), lambda s: (s, 0, 0, 0))),
        scratch_shapes=[pltpu.VMEM((HEADS, 128, DV), F32)],
        sem=("arbitrary",), ride=ride,
        args=(proj, tb["rope"], tb["mask"], tb["qdm"], tb["kdm"], tb["mqs"], tb["cd"]))


def _conv_fwd(proj, wcw, conv_b, ln_g, ln_b, l, ride=None):
    T = proj.shape[0]
    C = conv_b.shape[-1]
    Cc = wcw.shape[-1]
    tc = min(CONV_TILE, T)
    assert PW == 3 * C

    def body(p_ref, cw_ref, cb_ref, lg_ref, lb_ref, ac_ref, y_ref, ubuf, wbuf, ush):
        _conv_fwd_part(p_ref, cw_ref, cb_ref, lg_ref, lb_ref, ac_ref, y_ref, ubuf, wbuf, ush)

    vec = pl.BlockSpec((None, 1, C), lambda i: (l, 0, 0))
    tile = pl.BlockSpec((tc, C), lambda i: (i, 0))
    return _pcall(
        body, name="conv_fwd",
        out_shape=(jax.ShapeDtypeStruct((T, C), BF16), jax.ShapeDtypeStruct((T, C), F32)),
        grid=(T // tc,),
        in_specs=[pl.BlockSpec((tc, 3 * C), lambda i: (i, 1)),
                  pl.BlockSpec((4, None, CONV_K, Cc), lambda i: (0, l, 0, 0)), vec, vec, vec],
        out_specs=(tile, tile),
        scratch_shapes=[pltpu.VMEM((HALO + tc, C), F32), pltpu.VMEM((8 * CONV_K, C), F32),
                        pltpu.VMEM((7, HALO + tc - 8, C), F32)],
        sem=("arbitrary",), ride=ride, args=(proj, wcw, conv_b, ln_g, ln_b))


def _merge_fwd(x, proj, a_ret, a_conv, wsq, post_g, l, ride=None):
    T, D = x.shape
    R = wsq.shape[1] // 3
    tm = min(512, T)

    def body(x_ref, p_ref, ar_ref, ac_ref, wro_ref, wco_ref, wo_ref, g_ref, xn_ref, ro_ref, co_ref, ym_ref, z_ref):
        ro = _dot(ar_ref[...], wro_ref[...].reshape(4 * R, D))
        co = _dot(ac_ref[...], wco_ref[...].reshape(4 * R, D))
        ym = (_sigmoid(p_ref[:, 0:D].astype(F32)) * ro + _sigmoid(p_ref[:, D:2 * D].astype(F32)) * co).astype(BF16)
        z = _dot(ym, wo_ref[...].reshape(4 * R, D))
        ro_ref[...] = ro.astype(BF16)
        co_ref[...] = co.astype(BF16)
        ym_ref[...] = ym
        z_ref[...] = z.astype(BF16)
        xn_ref[...] = x_ref[...] + _rms(z, g_ref[...])

    tile = pl.BlockSpec((tm, D), lambda i: (i, 0))
    wspec = lambda m: pl.BlockSpec((4, R, D), lambda i: (0, m, 0))
    act = jax.ShapeDtypeStruct((T, D), BF16)
    return _pcall(
        body, name="merge_fwd",
        out_shape=(jax.ShapeDtypeStruct((T, D), F32), act, act, act, act),
        grid=(T // tm,),
        in_specs=[tile, pl.BlockSpec((tm, 2 * D), lambda i: (i, 3)), tile, tile,
                  wspec(0), wspec(1), wspec(2), pl.BlockSpec((None, 1, D), lambda i: (l, 0, 0))],
        out_specs=(tile, tile, tile, tile, tile),
        sem=("arbitrary",), ride=ride, args=(x, proj, a_ret, a_conv, wsq, wsq, wsq, post_g))


def _loss_fwd_bwd(y, target):
    T, D = y.shape
    tm = min(512, T)

    def body(y_ref, t_ref, dy_ref, ls_ref):
        @pl.when(pl.program_id(0) == 0)
        def _():
            ls_ref[...] = jnp.zeros_like(ls_ref)

        e = y_ref[...] - t_ref[...]
        dy_ref[...] = e * (1.0 / D)
        ls_ref[...] += jnp.sum((e * e).reshape(tm // 8, 8, D), axis=0) * (0.5 / D)

    tile = pl.BlockSpec((tm, D), lambda i: (i, 0))
    return pl.pallas_call(
        body, name="loss",
        out_shape=(jax.ShapeDtypeStruct((T, D), F32), jax.ShapeDtypeStruct((8, D), F32)),
        grid=(T // tm,),
        in_specs=[tile, tile],
        out_specs=(tile, pl.BlockSpec((8, D), lambda i: (0, 0))),
        compiler_params=_cp(("arbitrary",)),
    )(y, target)


def _merge_bwd(dxn, proj, a_ret, a_conv, ro, co, ym, z, wsq, post_g, l, ride=None):
    T, D = dxn.shape
    R = wsq.shape[1] // 3
    tm = min(512, T)
    n = T // tm

    def body(dx_ref, p_ref, ar_ref, ac_ref, ro_ref, co_ref, ym_ref, z_ref, wro_ref, wco_ref, wo_ref, g_ref,
             dp_ref, dar_ref, dac_ref, gsq_ref, dg_ref, acc, stage):
        i = pl.program_id(0)

        @pl.when(i == 0)
        def _():
            acc[...] = jnp.zeros_like(acc)
            dg_ref[...] = jnp.zeros_like(dg_ref)

        _, vjp = jax.vjp(_rms, z_ref[...].astype(F32), g_ref[...])
        dz, dg = vjp(dx_ref[...])
        dg_ref[0:1, :] += dg
        dzb = dz.astype(BF16)
        dym = _dot_nt(dzb, wo_ref[...].reshape(4 * R, D))
        acc[2] += _dot_tn(ym_ref[...], dzb)
        sr = _sigmoid(p_ref[:, 0:D].astype(F32))
        sc = _sigmoid(p_ref[:, D:2 * D].astype(F32))
        rov = ro_ref[...].astype(F32)
        cov = co_ref[...].astype(F32)
        dp_ref[:, 0:D] = (dym * rov * sr * (1.0 - sr)).astype(BF16)
        dp_ref[:, D:2 * D] = (dym * cov * sc * (1.0 - sc)).astype(BF16)
        dro = (dym * sr).astype(BF16)
        dco = (dym * sc).astype(BF16)
        dar_ref[...] = _dot_nt(dro, wro_ref[...].reshape(4 * R, D)).astype(BF16)
        dac_ref[...] = _dot_nt(dco, wco_ref[...].reshape(4 * R, D)).astype(BF16)
        acc[0] += _dot_tn(ar_ref[...], dro)
        acc[1] += _dot_tn(ac_ref[...], dco)

        @pl.when(i == n - 1)
        def _():
            for m in range(3):
                stage[...] = acc[m].astype(BF16).reshape(4, R, D)
                pltpu.sync_copy(stage, gsq_ref.at[:, pl.ds(m * R, R), :])

    tile = pl.BlockSpec((tm, D), lambda i: (i, 0))
    wspec = lambda m: pl.BlockSpec((4, R, D), lambda i: (0, m, 0), pipeline_mode=pl.Buffered(1))
    return _pcall(
        body, name="merge_bwd",
        out_shape=(jax.ShapeDtypeStruct(proj.shape, BF16), jax.ShapeDtypeStruct((T, D), BF16),
                   jax.ShapeDtypeStruct((T, D), BF16), jax.ShapeDtypeStruct(wsq.shape, BF16),
                   jax.ShapeDtypeStruct((8, D), F32)),
        grid=(n,),
        in_specs=[tile, pl.BlockSpec((tm, 2 * D), lambda i: (i, 3)), tile, tile, tile, tile, tile, tile,
                  wspec(0), wspec(1), wspec(2), pl.BlockSpec((None, 1, D), lambda i: (l, 0, 0))],
        out_specs=(pl.BlockSpec((tm, 2 * D), lambda i: (i, 3)), tile, tile, ANY, pl.BlockSpec((8, D), lambda i: (0, 0))),
        scratch_shapes=[pltpu.VMEM((3, 4 * R, D), F32), pltpu.VMEM((4, R, D), BF16)],
        sem=("arbitrary",), args=(dxn, proj, a_ret, a_conv, ro, co, ym, z, wsq, wsq, wsq, post_g), ride=ride)


def _conv_bwd_part(n, da_ref, y_ref, p_ref, ph_ref, cw_ref, lg_ref, lb_ref, dp_ref,
                   dcbuf, ubuf, dubuf, wbuf, dwacc, vacc, dsh, ush):
    tc, C = y_ref.shape
    off = HALO - (CONV_K - 1)
    nrb = tc // CONV_RB
    t = pl.program_id(0)
    i = n - 1 - t

    @pl.when(t == 0)
    def _():
        dcbuf[tc:tc + HALO, :] = jnp.zeros((HALO, C), F32)
        dwacc[...] = jnp.zeros_like(dwacc)
        vacc[...] = jnp.zeros_like(vacc)
        _load_conv_w(cw_ref, wbuf, True)

    @pl.when(t > 0)
    def _():
        dcbuf[tc:tc + HALO, :] = dcbuf[0:HALO, :]

    gc = p_ref[:, 2 * C:3 * C].astype(F32)
    _, vjp = jax.vjp(_ln_gate, y_ref[...], gc, lg_ref[...], lb_ref[...])
    dy, dgc, dlg, dlb = vjp(da_ref[...].astype(F32))
    dcbuf[0:tc, :] = dy
    dp_ref[:, 2 * C:3 * C] = dgc.astype(BF16)
    vacc[0:1, :] += jnp.sum(dy, axis=0, keepdims=True)
    vacc[1:2, :] += dlg
    vacc[2:3, :] += dlb

    ga = p_ref[:, 0:C].astype(F32)
    sb = _sigmoid(p_ref[:, C:2 * C].astype(F32))
    ubuf[HALO:HALO + tc, :] = ga * sb
    uh = ph_ref[:, 0:C].astype(F32) * _sigmoid(ph_ref[:, C:2 * C].astype(F32))
    ubuf[0:HALO, :] = jnp.where(i > 0, uh, 0.0)

    _shift_copies(dcbuf, dsh)
    _shift_copies(ubuf, ush)
    def du_block(r, carry):
        r0 = pl.multiple_of(r * CONV_RB, CONV_RB)
        for lanes in _lane_parts(C):
            dubuf[pl.ds(r0, CONV_RB), lanes] = _conv_taps(wbuf, dcbuf, dsh, r0, 0, CONV_RB, lanes)
        return carry

    lax.fori_loop(0, nrb, du_block, 0)
    du = dubuf[...]
    dp_ref[:, 0:C] = (du * sb).astype(BF16)
    dp_ref[:, C:2 * C] = (du * ga * sb * (1.0 - sb)).astype(BF16)

    def dw_block(r, carry):
        r0 = pl.multiple_of(r * CONV_RB, CONV_RB)
        for lanes in _lane_parts(C):
            dyb = dcbuf[pl.ds(r0, CONV_RB), lanes].reshape(CONV_RB // 8, 8, CONV_LANES)
            for k in range(CONV_K):
                dwacc[8 * k:8 * k + 8, lanes] += jnp.sum(dyb * _window(ubuf, ush, r0, off + k, CONV_RB, lanes), axis=0)
        return carry

    lax.fori_loop(0, nrb, dw_block, 0)


def _conv_bwd_final(n, sg_ref, dwacc, vacc):
    C = sg_ref.shape[-1]

    @pl.when(pl.program_id(0) == n - 1)
    def _():
        for k in range(CONV_K):
            sg_ref[pl.ds(k, 1), :] = jnp.sum(dwacc[8 * k:8 * k + 8, :], axis=0, keepdims=True)
        sg_ref[pl.ds(CONV_K, 1), :] = jnp.zeros((1, C), F32)
        sg_ref[ROW_CB:ROW_CB + 8, :] = jnp.zeros((8, C), F32)
        sg_ref[ROW_CB:ROW_CB + 3, :] = vacc[0:3, :]


def _ret_bwd_part(da_ref, p_ref, st_ref, rope_ref, m_ref, qdm_ref, kdm_ref, mqs_ref, cd_ref, dp_ref, gst):
    c, sl, sh = rope_ref[0], rope_ref[1], rope_ref[2]
    for j in range(HEADS // 2):
        rq = _rot(p_ref[:, 128 * j:128 * (j + 1)].astype(F32), c, sl, sh)
        rk = _rot(p_ref[:, QK + 128 * j:QK + 128 * (j + 1)].astype(F32), c, sl, sh)
        rkb = rk.astype(BF16)
        drq = jnp.zeros_like(rq)
        drk = jnp.zeros_like(rk)
        for e in range(2):
            h = 2 * j + e
            v = p_ref[:, 2 * QK + DV * h:2 * QK + DV * (h + 1)]
            g = p_ref[:, 2 * QK + VW + DV * h:2 * QK + VW + DV * (h + 1)].astype(F32)
            mqs = mqs_ref[h, 0:1, :]
            a = (rq * mqs).astype(BF16)
            aq = (rq * qdm_ref[h]).astype(BF16)
            kdv = (rk * kdm_ref[h]).astype(BF16)
            mk = m_ref[h]
            p = (_dot_nt(a, rkb) * mk).astype(BF16)
            stb = st_ref[h].astype(BF16)
            o = _dot(p, v) + _dot(aq, stb)
            _, vjp = jax.vjp(_gn_gate, o, g)
            do, dg = vjp(da_ref[:, DV * h:DV * (h + 1)].astype(F32))
            dob = do.astype(BF16)
            gs = gst[h]
            gsb = gs.astype(BF16)
            ds = (_dot_nt(dob, v) * mk).astype(BF16)
            drq = drq + _dot(ds, rkb) * mqs + _dot_nt(dob, stb) * qdm_ref[h]
            drk = drk + _dot_tn(ds, a) + _dot_nt(v, gsb) * kdm_ref[h]
            dv = _dot_tn(p, dob) + _dot(kdv, gsb)
            gst[h] = _dot_tn(aq, dob) + gs * cd_ref[h, 0:1, :]
            dp_ref[:, 2 * QK + DV * h:2 * QK + DV * (h + 1)] = dv.astype(BF16)
            dp_ref[:, 2 * QK + VW + DV * h:2 * QK + VW + DV * (h + 1)] = dg.astype(BF16)
        dp_ref[:, 128 * j:128 * (j + 1)] = _rot_t(drq, c, sl, sh).astype(BF16)
        dp_ref[:, QK + 128 * j:QK + 128 * (j + 1)] = _rot_t(drk, c, sl, sh).astype(BF16)


def _ret_bwd(dproj, da_ret, proj, states, tb, L, ride=None):
    T = proj.shape[0]
    nS = T // L

    def body(dpin_ref, dar_ref, p_ref, st_ref, rope_ref, m_ref, qdm_ref, kdm_ref, mqs_ref, cd_ref, dp_ref, gst):
        _zero_at_start(gst)
        _ret_bwd_part(dar_ref, p_ref, st_ref, rope_ref, m_ref, qdm_ref, kdm_ref, mqs_ref, cd_ref, dp_ref, gst)

    rev = lambda s: nS - 1 - s
    specs = _ret_specs(T, L)
    specs[0] = pl.BlockSpec((3, L, 128), lambda s: (0, rev(s), 0))
    ptile = pl.BlockSpec((L, PW), lambda s: (rev(s), 0))
    return _pcall(
        body, name="ret_bwd",
        out_shape=(jax.ShapeDtypeStruct(dproj.shape, BF16),),
        grid=(nS,),
        in_specs=[ANY, pl.BlockSpec((L, VW), lambda s: (rev(s), 0)), ptile,
                  pl.BlockSpec((None, HEADS, 128, DV), lambda s: (rev(s), 0, 0, 0))] + specs,
        out_specs=(ptile,),
        scratch_shapes=[pltpu.VMEM((HEADS, 128, DV), F32)],
        sem=("arbitrary",), aliases={0: 0}, ride=ride,
        args=(dproj, da_ret, proj, states, tb["rope"], tb["mask"], tb["qdm"], tb["kdm"], tb["mqs"], tb["cd"]))


def _conv_bwd(dproj, da_conv, y, proj, wcw, ln_g, ln_b, l, ride=None):
    T, C = y.shape
    Cc = wcw.shape[-1]
    tc = min(CONV_TILE, T)
    n = T // tc
    hb = tc // HALO

    def body(dpin_ref, dac_ref, y_ref, p_ref, ph_ref, cw_ref, lg_ref, lb_ref, dp_ref, sg_ref,
             dcbuf, ubuf, dubuf, wbuf, dwacc, vacc, dsh, ush):
        _conv_bwd_part(n, dac_ref, y_ref, p_ref, ph_ref, cw_ref, lg_ref, lb_ref, dp_ref,
                       dcbuf, ubuf, dubuf, wbuf, dwacc, vacc, dsh, ush)
        _conv_bwd_final(n, sg_ref, dwacc, vacc)

    rev = lambda t: n - 1 - t
    vec = pl.BlockSpec((None, 1, C), lambda t: (l, 0, 0))
    tile = pl.BlockSpec((tc, C), lambda t: (rev(t), 0))
    ptile = pl.BlockSpec((tc, 3 * C), lambda t: (rev(t), 1))
    halo = pl.BlockSpec((HALO, 3 * C), lambda t: (jnp.maximum(rev(t) * hb - 1, 0), 1))
    return _pcall(
        body, name="conv_bwd",
        out_shape=(jax.ShapeDtypeStruct(dproj.shape, BF16), jax.ShapeDtypeStruct((ROW_PRE, C), F32)),
        grid=(n,),
        in_specs=[ANY, tile, tile, ptile, halo, pl.BlockSpec((4, None, CONV_K, Cc), lambda t: (0, l, 0, 0)), vec, vec],
        out_specs=(ptile, pl.BlockSpec((ROW_PRE, C), lambda t: (0, 0))),
        scratch_shapes=[pltpu.VMEM((tc + HALO, C), F32), pltpu.VMEM((HALO + tc, C), F32), pltpu.VMEM((tc, C), F32),
                        pltpu.VMEM((8 * CONV_K, C), F32), pltpu.VMEM((8 * CONV_K, C), F32), pltpu.VMEM((8, C), F32),
                        pltpu.VMEM((7, HALO + tc - 8, C), F32), pltpu.VMEM((7, HALO + tc - 8, C), F32)],
        sem=("arbitrary",), aliases={0: 0}, ride=ride, args=(dproj, da_conv, y, proj, proj, wcw, ln_g, ln_b))


def _win_grad(h, dproj, W):
    T, D = h.shape
    tk = min(2048, T)
    nk = T // tk

    def body(h_ref, dp_ref, g_ref, acc):
        k = pl.program_id(1)

        @pl.when(k == 0)
        def _():
            acc[...] = jnp.zeros_like(acc)

        acc[...] += _dot_tn(h_ref[...], dp_ref[...])

        @pl.when(k == nk - 1)
        def _():
            g_ref[...] = acc[...].astype(BF16)

    return pl.pallas_call(
        body, name="win_grad",
        out_shape=jax.ShapeDtypeStruct((4, D, W), BF16),
        grid=(4, nk),
        in_specs=[pl.BlockSpec((tk, D), lambda j, k: (k, 0)), pl.BlockSpec((tk, W), lambda j, k: (k, j))],
        out_specs=pl.BlockSpec((None, D, W), lambda j, k: (j, 0, 0)),
        scratch_shapes=[pltpu.VMEM((D, W), F32)],
        compiler_params=_cp(("arbitrary", "arbitrary")),
    )(h, dproj)


def _in_bwd(dxn, dproj, x, pre_g, win, l, ride=None):
    T, D = x.shape
    W = win.shape[-1]
    tm = min(1024, T)

    def body(dxn_ref, dp_ref, x_ref, g_ref, w_ref, dx_ref, dg_ref, acc):
        i = pl.program_id(0)
        j = pl.program_id(1)

        @pl.when(j == 0)
        def _():
            acc[...] = jnp.zeros_like(acc)

        @pl.when((i == 0) & (j == 0))
        def _():
            dg_ref[...] = jnp.zeros_like(dg_ref)

        acc[...] += _dot_nt(dp_ref[...], w_ref[...])

        @pl.when(j == 3)
        def _():
            _, vjp = jax.vjp(_rms, x_ref[...], g_ref[...])
            dx, dg = vjp(acc[...])
            dx_ref[...] = dxn_ref[...] + dx
            dg_ref[0:1, :] += dg

    tile = pl.BlockSpec((tm, D), lambda i, j: (i, 0))
    return _pcall(
        body, name="in_bwd",
        out_shape=(jax.ShapeDtypeStruct((T, D), F32), jax.ShapeDtypeStruct((8, D), F32)),
        grid=(T // tm, 4),
        in_specs=[tile, pl.BlockSpec((tm, W), lambda i, j: (i, j)), tile,
                  pl.BlockSpec((None, 1, D), lambda i, j: (l, 0, 0)),
                  pl.BlockSpec((None, D, W), lambda i, j: (j, 0, 0))],
        out_specs=(tile, pl.BlockSpec((8, D), lambda i, j: (0, 0))),
        scratch_shapes=[pltpu.VMEM((tm, D), F32)],
        sem=("arbitrary", "arbitrary"), args=(dxn, dproj, x, pre_g, win), ride=ride)


def _sum_group(chip, t, u):
    _, A, B = t.shape
    tr = min(256, A)

    def body(k_ref, t_ref, u_ref, o_ref):
        o_ref[...] = ((t_ref[...].astype(F32) + u_ref[0].astype(F32)) + u_ref[1].astype(F32)) + u_ref[2].astype(F32)

    return pl.pallas_call(
        body, name="sum_group",
        out_shape=jax.ShapeDtypeStruct((A, B), F32),
        grid_spec=pltpu.PrefetchScalarGridSpec(
            num_scalar_prefetch=1, grid=(A // tr,),
            in_specs=[pl.BlockSpec((None, tr, B), lambda i, k: (k[0], i, 0)),
                      pl.BlockSpec((3, tr, B), lambda i, k: (0, i, 0))],
            out_specs=pl.BlockSpec((tr, B), lambda i, k: (i, 0))),
        compiler_params=_cp(("arbitrary",)),
    )(chip, t, u)


def _swap_rows(g):
    _, A, B = g.shape
    nh = A // 2

    def body(g_ref, r_ref, send_sems, recv_sems):
        x, y, c = _place()
        cp = _remote(g_ref.at[:, pl.ds((1 - c) * nh, nh)], r_ref, send_sems, recv_sems, 0, (x, y, 1 - c))
        cp.start()
        cp.wait()

    return pl.pallas_call(
        body, name="swap_rows",
        out_shape=jax.ShapeDtypeStruct((4, nh, B), g.dtype),
        in_specs=[ANY], out_specs=ANY,
        scratch_shapes=[pltpu.SemaphoreType.DMA((1,)), pltpu.SemaphoreType.DMA((1,))],
        compiler_params=pltpu.CompilerParams(has_side_effects=True),
    )(g)


def _add_rows(cidx, g, r):
    _, nh, B = r.shape
    tr = min(256, nh)
    nb = nh // tr

    def body(c_ref, g_ref, r_ref, o_ref):
        o_ref[...] = (g_ref[...].astype(F32) + r_ref[...].astype(F32)).astype(BF16)

    blk = (None, tr, B)
    return pl.pallas_call(
        body, name="add_rows",
        out_shape=jax.ShapeDtypeStruct(r.shape, BF16),
        grid_spec=pltpu.PrefetchScalarGridSpec(
            num_scalar_prefetch=1, grid=(4, nb),
            in_specs=[pl.BlockSpec(blk, lambda k, i, c: (k, c[0] * nb + i, 0)),
                      pl.BlockSpec(blk, lambda k, i, c: (k, i, 0))],
            out_specs=pl.BlockSpec(blk, lambda k, i, c: (k, i, 0))),
        compiler_params=_cp(("arbitrary", "arbitrary")),
    )(cidx, g, r)


def _sum_group_half(chip, cidx, t, u):
    _, nh, B = t.shape
    tr = min(256, nh)
    nb = nh // tr

    def body(k_ref, c_ref, t_ref, u_ref, o_ref):
        mine = (pl.program_id(0) // nb) == c_ref[0]

        @pl.when(mine)
        def _():
            o_ref[...] = ((t_ref[...].astype(F32) + u_ref[0].astype(F32)) + u_ref[1].astype(F32)) + u_ref[2].astype(F32)

        @pl.when(jnp.logical_not(mine))
        def _():
            o_ref[...] = jnp.zeros_like(o_ref)

    own = lambda i, c: jnp.clip(i - c[0] * nb, 0, nb - 1)
    return pl.pallas_call(
        body, name="sum_group_half",
        out_shape=jax.ShapeDtypeStruct((2 * nh, B), F32),
        grid_spec=pltpu.PrefetchScalarGridSpec(
            num_scalar_prefetch=2, grid=(2 * nb,),
            in_specs=[pl.BlockSpec((None, tr, B), lambda i, k, c: (k[0], own(i, c), 0)),
                      pl.BlockSpec((3, tr, B), lambda i, k, c: (0, own(i, c), 0))],
            out_specs=pl.BlockSpec((tr, B), lambda i, k, c: (i, 0))),
        compiler_params=_cp(("arbitrary",)),
    )(chip, cidx, t, u)


def _adam_math(w, g, m, v):
    c1 = 1.0 / (1.0 - ADAM_B1 ** ADAM_STEP)
    c2 = 1.0 / (1.0 - ADAM_B2 ** ADAM_STEP)
    nm = ADAM_B1 * m + (1.0 - ADAM_B1) * g
    nv = ADAM_B2 * v + (1.0 - ADAM_B2) * (g * g)
    return -ADAM_LR * ((nm * c1) / (jnp.sqrt(nv * c2) + ADAM_EPS) + ADAM_WD * w), nm, nv


def _adamw_layer(prev, w, m, v, sa, sb, l, part, ride=None):
    NL, A, B = w.shape
    tr = A
    while tr * B * 4 > ADAM_BLOCK_BYTES and tr % 16 == 0:
        tr //= 2
    nb = A // tr

    def body(p0, p1, p2, p3, w_ref, m_ref, v_ref, sa_ref, sb_ref, g_ref, d_ref, nm_ref, nv_ref):
        g = sa_ref[...] + sb_ref[...]
        g_ref[...] = g
        d_ref[...], nm_ref[...], nv_ref[...] = _adam_math(w_ref[...], g, m_ref[...], v_ref[...])

    lay = pl.BlockSpec((None, tr, B), lambda i: (l, i, 0))
    src = pl.BlockSpec((tr, B), lambda i: (part * nb + i, 0))
    full = jax.ShapeDtypeStruct((NL, A, B), F32)
    if prev is None:
        prev = tuple(lax.empty((NL, A, B), F32) for _ in range(4))
    outs, landed = _pcall(
        body, name="adamw_layer",
        out_shape=(full,) * 4, grid=(nb,),
        in_specs=[ANY] * 4 + [lay, lay, lay, src, src], out_specs=(lay,) * 4,
        sem=("arbitrary",), aliases={0: 0, 1: 1, 2: 2, 3: 3}, args=(*prev, w, m, v, sa, sb), ride=ride)
    return tuple(outs), landed


def _adamw(w, g, m, v):
    shape = w.shape
    cols = shape[-1]
    rows = int(np.prod(shape[:-1]))

    def body(w_ref, g_ref, m_ref, v_ref, d_ref, nm_ref, nv_ref):
        d_ref[...], nm_ref[...], nv_ref[...] = _adam_math(w_ref[...], g_ref[...], m_ref[...], v_ref[...])

    tile = pl.BlockSpec((rows, cols), lambda i: (0, 0))
    out = jax.ShapeDtypeStruct((rows, cols), F32)
    res = pl.pallas_call(
        body, name="adamw",
        out_shape=(out, out, out), grid=(1,),
        in_specs=[tile] * 4, out_specs=(tile,) * 3,
        compiler_params=_cp(("arbitrary",)),
    )(*[a.reshape(rows, cols) for a in (w, g, m, v)])
    return tuple(a.reshape(shape) for a in res)


def _tail_exchange(small, s_in, s_sq):
    def body(s_ref, a_ref, b_ref, o_ref, oa_ref, ob_ref, send_sems, recv_sems, local_sem):
        x, y, c = _place()
        me = 4 * x + 2 * y + c
        sibling = (x, y, 1 - c)
        mine = pltpu.make_async_copy(s_ref, o_ref.at[me], local_sem)
        mine.start()
        swaps = [_remote(a_ref, oa_ref, send_sems, recv_sems, 7, sibling), _remote(b_ref, ob_ref, send_sems, recv_sems, 8, sibling)]
        sends = [_remote(s_ref, o_ref.at[me], send_sems, recv_sems, r, peer) for r, peer in enumerate(_peers(x, y, c))]
        for cp in swaps + sends:
            cp.start()
        for r, peer in enumerate(_peers(x, y, c)):
            theirs = o_ref.at[4 * peer[0] + 2 * peer[1] + peer[2]]
            _remote(theirs, theirs, send_sems, recv_sems, r, peer).wait_recv()
        for cp in sends:
            cp.wait_send()
        for cp in swaps:
            cp.wait()
        mine.wait()

    return pl.pallas_call(
        body, name="tail_exchange",
        out_shape=(jax.ShapeDtypeStruct((8,) + small.shape, small.dtype),
                   jax.ShapeDtypeStruct(s_in.shape, s_in.dtype), jax.ShapeDtypeStruct(s_sq.shape, s_sq.dtype)),
        in_specs=[ANY] * 3, out_specs=(ANY,) * 3,
        scratch_shapes=[pltpu.SemaphoreType.DMA((9,)), pltpu.SemaphoreType.DMA((9,)), pltpu.SemaphoreType.DMA],
        compiler_params=pltpu.CompilerParams(has_side_effects=True),
    )(small, s_in, s_sq)


def _sum_devices(gs):
    NL = len(gs)
    _, R, D = gs[0].shape

    def body(*refs):
        o_ref = refs[NL]
        for l in range(NL):
            acc = refs[l][0]
            for k in range(1, 8):
                acc = acc + refs[l][k]
            o_ref[l] = acc

    return pl.pallas_call(
        body, name="sum_devices",
        out_shape=jax.ShapeDtypeStruct((NL, R, D), F32),
        grid=(1,),
        in_specs=[pl.BlockSpec((8, R, D), lambda i: (0, 0, 0))] * NL,
        out_specs=pl.BlockSpec((NL, R, D), lambda i: (0, 0, 0)),
        compiler_params=_cp(("arbitrary",)),
    )(*gs)


def kernel(x, pre_norm_g, w_in, w_ret_out, conv_w, conv_b, conv_ln_g, conv_ln_b, w_conv_out, w_o, post_norm_g, loss_target, m_pre_norm_g, m_w_in, m_w_ret_out, m_conv_w, m_conv_b, m_conv_ln_g, m_conv_ln_b, m_w_conv_out, m_w_o, m_post_norm_g, v_pre_norm_g, v_w_in, v_w_ret_out, v_conv_w, v_conv_b, v_conv_ln_g, v_conv_ln_b, v_w_conv_out, v_w_o, v_post_norm_g):
    NL, D, W = w_in.shape
    Cc = conv_w.shape[-1]
    T = x.shape[1]
    L = min(RET_BLOCK, T)
    tb = _tables(T, L)
    ax, ay, ac = _place()
    chip = (2 * ax + ay).astype(jnp.int32).reshape(1)
    cidx = ac.astype(jnp.int32).reshape(1)
    pre_g, cb, lg, lb, post_g = (a.reshape(NL, 1, D) for a in (pre_norm_g, conv_b, conv_ln_g, conv_ln_b, post_norm_g))

    win = [_cast_win(chip, w_in, l) for l in range(NL)]
    wsq = [_cast_wsq(chip, w_ret_out, w_conv_out, w_o, l) for l in range(NL)]
    win[0], wcw = _gather_first(win[0], _place_cw(chip, conv_w))

    saved = []
    xl = x[0]
    for l in range(NL):
        more = l + 1 < NL
        ride = _Ride()
        if more:
            _ride_gather_ici(ride, "win", win[l + 1], (0, 3, 4))
        (proj, h), got = _fwd_in(xl, pre_g, win[l], l, ride=ride)
        if more:
            win[l + 1] = got["win"]
        ride = _Ride()
        if l == 0:
            _ride_gather_ici(ride, "wsq0", wsq[0])
        (a_ret, states), got = _ret_fwd(proj, tb, L, ride=ride)
        if l == 0:
            wsq[0] = got["wsq0"]
        ride = _Ride()
        if more:
            _ride_gather_ici(ride, "win", win[l + 1], (3, 4, 4))
            _ride_gather_ici(ride, "wsq", wsq[l + 1])
        if l == 0:
            _ride_gather_pass(ride, "wsq0", wsq[0])
        (a_conv, y), got = _conv_fwd(proj, wcw, cb, lg, lb, l, ride=ride)
        if more:
            win[l + 1], wsq[l + 1] = got["win"], got["wsq"]
        if l == 0:
            wsq[0] = got["wsq0"]
        ride = _Ride()
        if more:
            _ride_gather_pass(ride, "win", win[l + 1])
            _ride_gather_pass(ride, "wsq", wsq[l + 1])
        (xn, ro, co, ym, z), got = _merge_fwd(xl, proj, a_ret, a_conv, wsq[l], post_g, l, ride=ride)
        if more:
            win[l + 1], wsq[l + 1] = got["win"], got["wsq"]
        saved.append((xl, proj, h, a_ret, states, a_conv, y, ro, co, ym, z))
        xl = xn
    dx, lsum = _loss_fwd_bwd(xl, loss_target[0])
    loss = lax.psum(jnp.sum(lsum), ("x", "y", "c"))

    gin, gsq, uin, usq = [None] * NL, [None] * NL, [None] * NL, [None] * NL
    s_in, s_sq, o_in, o_sq = [None] * NL, [None] * NL, [None] * NL, [None] * NL
    small, gs = [None] * NL, [None] * NL
    for l in reversed(range(NL)):
        xin, proj, h, a_ret, states, a_conv, y, ro, co, ym, z = saved[l]
        (dproj, da_ret, da_conv, gsq[l], dpost), _ = _merge_bwd(dx, proj, a_ret, a_conv, ro, co, ym, z, wsq[l], post_g, l)
        ride = _Ride()
        _ride_exchange(ride, "gsq", gsq[l], "usq", None, (0, 1, 2))
        if l + 1 < NL:
            _ride_exchange(ride, "gin", gin[l + 1], "uin", uin[l + 1], (2,))
        (dproj, sg), got = _conv_bwd(dproj, da_conv, y, proj, wcw, lg, lb, l, ride=ride)
        usq[l] = got["usq"]
        if l + 1 < NL:
            uin[l + 1] = got["uin"]
        ride = _Ride()
        if l + 1 < NL:
            _ride_gather_all(ride, "small", small[l + 1], "gs")
        (dproj,), got = _ret_bwd(dproj, da_ret, proj, states, tb, L, ride=ride)
        if l + 1 < NL:
            gs[l + 1] = got["gs"]
        gin[l] = _win_grad(h, dproj, W)
        ride = _Ride()
        if l > 0:
            _ride_exchange(ride, "gin", gin[l], "uin", None, (0, 1))
        else:
            gin[0] = _add_rows(cidx, gin[0], _swap_rows(gin[0]))
            _ride_exchange(ride, "gin", gin[0], "uin", None, (0, 1, 2))
        if l + 1 < NL:
            s_in[l + 1] = _sum_group(chip, gin[l + 1], uin[l + 1])
            s_sq[l + 1] = _sum_group(chip, gsq[l + 1], usq[l + 1])
            _ride_swap(ride, "s_in", s_in[l + 1], "o_in")
            _ride_swap(ride, "s_sq", s_sq[l + 1], "o_sq")
        (dx, dpre), got = _in_bwd(dx, dproj, xin, pre_g, win[l], l, ride=ride)
        uin[l] = got["uin"]
        if l + 1 < NL:
            o_in[l + 1], o_sq[l + 1] = got["o_in"], got["o_sq"]
        small[l] = jnp.concatenate([sg, dpre, dpost], axis=0)
    grad_x = dx

    big = {"w_in": None, "w_ret_out": None, "w_conv_out": None, "w_o": None}
    wts = {"w_in": (w_in, m_w_in, v_w_in), "w_ret_out": (w_ret_out, m_w_ret_out, v_w_ret_out),
           "w_conv_out": (w_conv_out, m_w_conv_out, v_w_conv_out), "w_o": (w_o, m_w_o, v_w_o)}
    sq_names = ("w_ret_out", "w_conv_out", "w_o")

    def adam_in(l, ride=None):
        big["w_in"], got = _adamw_layer(big["w_in"], *wts["w_in"], s_in[l], o_in[l], l, 0, ride=ride)
        return got

    def adam_sq(l, part, ride=None):
        n = sq_names[part]
        big[n], got = _adamw_layer(big[n], *wts[n], s_sq[l], o_sq[l], l, part, ride=ride)
        return got

    s_in[0] = _sum_group_half(chip, cidx, gin[0], uin[0])
    s_sq[0] = _sum_group(chip, gsq[0], usq[0])
    gs[0], o_in[0], o_sq[0] = _tail_exchange(small[0], s_in[0], s_sq[0])
    for l in reversed(range(NL)):
        adam_in(l)
        for part in range(3):
            adam_sq(l, part)

    gsm = _sum_devices(gs)
    grads = {
        "pre_norm_g": gsm[:, ROW_PRE], "conv_w": lax.dynamic_slice_in_dim(gsm[:, 0:CONV_K], chip[0] * Cc, Cc, axis=2),
        "conv_b": gsm[:, ROW_CB], "conv_ln_g": gsm[:, ROW_LG], "conv_ln_b": gsm[:, ROW_LB], "post_norm_g": gsm[:, ROW_POST],
    }
    weights = dict(pre_norm_g=pre_norm_g, conv_w=conv_w, conv_b=conv_b, conv_ln_g=conv_ln_g, conv_ln_b=conv_ln_b,
                   post_norm_g=post_norm_g)
    m1 = dict(pre_norm_g=m_pre_norm_g, conv_w=m_conv_w, conv_b=m_conv_b, conv_ln_g=m_conv_ln_g, conv_ln_b=m_conv_ln_b,
              post_norm_g=m_post_norm_g)
    m2 = dict(pre_norm_g=v_pre_norm_g, conv_w=v_conv_w, conv_b=v_conv_b, conv_ln_g=v_conv_ln_g, conv_ln_b=v_conv_ln_b,
              post_norm_g=v_post_norm_g)
    res = {n: (grads[n],) + _adamw(weights[n], grads[n], m1[n], m2[n]) for n in grads}
    res.update(big)
    order = ["pre_norm_g", "w_in", "w_ret_out", "conv_w", "conv_b", "conv_ln_g", "conv_ln_b", "w_conv_out", "w_o", "post_norm_g"]
    return (loss, grad_x[None], *[res[n][0] for n in order], *[res[n][1] for n in order],
            *[res[n][2] for n in order], *[res[n][3] for n in order])
```

```python
import numpy as np
import jax
import jax.numpy as jnp
from jax import lax
from jax.experimental import pallas as pl
from jax.experimental.pallas import tpu as pltpu

F32 = jnp.float32
BF16 = jnp.bfloat16

HEADS = 8
DK = 64
DV = 128
CONV_K = 31
CHUNK = 64
ROPE_BASE = 10000.0
EPS = 1e-6
HALO = 32
CONV_RB = 32
CONV_LANES = 512
CONV_TILE = 256
RET_BLOCK = 512

ADAM_LR = 0.001
ADAM_B1 = 0.9
ADAM_B2 = 0.999
ADAM_EPS = 1e-08
ADAM_WD = 0.01
ADAM_STEP = 10
ADAM_BLOCK_BYTES = 2 * 1024 * 1024

VMEM_LIMIT = 56 * 1024 * 1024
MESH_T = pl.DeviceIdType.MESH
ANY = pl.BlockSpec(memory_space=pl.ANY)

ROW_CB, ROW_LG, ROW_LB = 32, 33, 34
ROW_PRE, ROW_POST = 40, 48


def _cp(sem=None, **kw):
    return pltpu.CompilerParams(dimension_semantics=sem, vmem_limit_bytes=VMEM_LIMIT, **kw)


def _dot(a, b):
    return jnp.dot(a, b, preferred_element_type=F32)


def _dot_nt(a, b):
    return lax.dot_general(a, b, (((1,), (1,)), ((), ())), preferred_element_type=F32)


def _dot_tn(a, b):
    return lax.dot_general(a, b, (((0,), (0,)), ((), ())), preferred_element_type=F32)


def _sigmoid(x):
    return jax.nn.sigmoid(x)


def _silu(x):
    return x * _sigmoid(x)


def _rms(x, g):
    return x * lax.rsqrt(jnp.mean(x * x, axis=-1, keepdims=True) + EPS) * g


def _gn_gate(o, g):
    mu = jnp.mean(o, axis=-1, keepdims=True)
    d = o - mu
    var = jnp.mean(d * d, axis=-1, keepdims=True)
    return d * lax.rsqrt(var + EPS) * _silu(g)


def _ln_gate(y, gc, lg, lb):
    mu = jnp.mean(y, axis=-1, keepdims=True)
    d = y - mu
    var = jnp.mean(d * d, axis=-1, keepdims=True)
    return _silu(d * lax.rsqrt(var + EPS) * lg + lb) * _silu(gc)


def _tables(T, L):
    lane = np.arange(128)
    d = lane % DK
    half = DK // 2
    inv = (ROPE_BASE ** (-(np.arange(half, dtype=np.float32)) / half)).astype(np.float32)
    ang = (np.arange(T, dtype=np.float32)[:, None] * inv[None, :]).astype(np.float64)
    angl = ang[:, d % half]
    cos = np.cos(angl)
    sin = np.sin(angl)
    lo = (d < half)[None, :]
    rope = np.stack([cos, np.where(lo, -sin, 0.0), np.where(lo, 0.0, sin)]).astype(np.float32)

    hh = np.arange(HEADS, dtype=np.float64)
    log_g = np.log1p(-np.exp2(-5.0 - hh))
    n = np.arange(L, dtype=np.float64)
    cn = np.arange(L) // CHUNK
    allowed = (cn[None, :] <= cn[:, None])
    dist = np.abs(n[:, None] - n[None, :])
    mask = np.exp(log_g[:, None, None] * dist[None]) * allowed[None]
    mq = ((lane[None, :] // DK) == (np.arange(HEADS)[:, None] % 2)).astype(np.float64)
    qd = np.exp(log_g[:, None] * n[None, :])
    kd = np.exp(log_g[:, None] * (L - n[None, :]))
    qdm = qd[:, :, None] * mq[:, None, :] * (DK ** -0.5)
    kdm = kd[:, :, None] * mq[:, None, :]
    mqs = np.broadcast_to((mq * (DK ** -0.5))[:, None, :], (HEADS, 8, 128))
    cd = np.broadcast_to(np.exp(log_g * L)[:, None, None], (HEADS, 8, 128))
    f = lambda a: jnp.asarray(np.ascontiguousarray(a), dtype=F32)
    return dict(rope=f(rope), mask=f(mask), qdm=f(qdm), kdm=f(kdm), mqs=f(mqs), cd=f(cd))


def _rot(b, c, sl, sh):
    return b * c + pltpu.roll(b, 96, axis=1) * sl + pltpu.roll(b, 32, axis=1) * sh


def _rot_t(d, c, sl, sh):
    return d * c + pltpu.roll(d * sl, 32, axis=1) + pltpu.roll(d * sh, 96, axis=1)


def _place():
    return lax.axis_index("x"), lax.axis_index("y"), lax.axis_index("c")


def _other_chips(x, y):
    return [(1 - x, y), (x, 1 - y), (1 - x, 1 - y)]


def _remote(src, dst, send_sems, recv_sems, k, to):
    return pltpu.make_async_remote_copy(src_ref=src, dst_ref=dst, send_sem=send_sems.at[k], recv_sem=recv_sems.at[k],
                                        device_id=to, device_id_type=MESH_T)


class _Ride:
    def __init__(self):
        self.arrays, self.kinds, self.names = [], [], []
        self.fresh = []
        self.ops = []

    def read(self, name, a):
        self.names.append(name)
        self.arrays.append(a)
        self.kinds.append("in")

    def inout(self, name, a):
        self.names.append(name)
        self.arrays.append(a)
        self.kinds.append("inout")

    def land(self, name, shape, dtype):
        self.fresh.append((name, jax.ShapeDtypeStruct(shape, dtype)))

    def op(self, n_sems, start, finish):
        self.ops.append((n_sems, start, finish))


def _pcall(body, *, name, grid, in_specs, out_specs, out_shape, args, scratch_shapes=(), sem, aliases=None, ride=None):
    if ride is None or not ride.ops:
        outs = pl.pallas_call(body, name=name, grid=grid, in_specs=list(in_specs), out_specs=tuple(out_specs),
                              out_shape=tuple(out_shape), scratch_shapes=list(scratch_shapes),
                              input_output_aliases=dict(aliases or {}), compiler_params=_cp(sem))(*args)
        return outs, {}

    ni, no, nr = len(args), len(out_shape), len(ride.arrays)
    inout = [i for i, k in enumerate(ride.kinds) if k == "inout"]
    r_out_shapes = [jax.ShapeDtypeStruct(ride.arrays[i].shape, ride.arrays[i].dtype) for i in inout] + [s for _, s in ride.fresh]
    r_out_names = [ride.names[i] for i in inout] + [n for n, _ in ride.fresh]
    nro = len(r_out_shapes)
    n_sems = sum(n for n, _, _ in ride.ops)
    n_scr = len(scratch_shapes)
    nd = len(grid)

    def wrapped(*refs):
        ins, rin = refs[:ni], refs[ni:ni + nr]
        outs, rout = refs[ni + nr:ni + nr + no], refs[ni + nr + no:ni + nr + no + nro]
        scr = refs[ni + nr + no + nro:ni + nr + no + nro + n_scr]
        send_sems, recv_sems = refs[-2], refs[-1]
        view = {nm: r for nm, r, k in zip(ride.names, rin, ride.kinds) if k == "in"}
        view.update(dict(zip(r_out_names, rout)))
        first = pl.program_id(0) == 0
        last = pl.program_id(0) == grid[0] - 1
        for d in range(1, nd):
            first = first & (pl.program_id(d) == 0)
            last = last & (pl.program_id(d) == grid[d] - 1)

        @pl.when(first)
        def _():
            base = 0
            for n, start, _ in ride.ops:
                start(view, send_sems, recv_sems, base)
                base += n

        body(*ins, *outs, *scr)

        @pl.when(last)
        def _():
            base = 0
            for n, _, finish in ride.ops:
                finish(view, send_sems, recv_sems, base)
                base += n

    res = pl.pallas_call(
        wrapped, name=name, grid=grid,
        in_specs=list(in_specs) + [ANY] * nr, out_specs=tuple(out_specs) + (ANY,) * nro,
        out_shape=tuple(out_shape) + tuple(r_out_shapes),
        scratch_shapes=list(scratch_shapes) + [pltpu.SemaphoreType.DMA((n_sems,)), pltpu.SemaphoreType.DMA((n_sems,))],
        input_output_aliases={**dict(aliases or {}), **{ni + i: no + j for j, i in enumerate(inout)}},
        compiler_params=_cp(sem),
    )(*args, *ride.arrays)
    return res[:no], dict(zip(r_out_names, res[no:]))


def _half(ref, chip_idx, cc, part=(0, 1, 1)):
    n = ref.shape[1] // 2
    lo, hi, k = part
    return ref.at[chip_idx, pl.ds(cc * n + lo * n // k, (hi - lo) * n // k)]


def _ride_gather_ici(ride, name, a, part=(0, 1, 1)):
    ride.inout(name, a)

    def start(view, ss, rs, b):
        x, y, c = _place()
        mine = _half(view[name], 2 * x + y, c, part)
        for j, (cx, cy) in enumerate(_other_chips(x, y)):
            _remote(mine, mine, ss, rs, b + j, (cx, cy, c)).start()

    def finish(view, ss, rs, b):
        x, y, c = _place()
        mine = _half(view[name], 2 * x + y, c, part)
        for j, (cx, cy) in enumerate(_other_chips(x, y)):
            theirs = _half(view[name], 2 * cx + cy, c, part)
            _remote(theirs, theirs, ss, rs, b + j, (cx, cy, c)).wait_recv()
        for j, (cx, cy) in enumerate(_other_chips(x, y)):
            _remote(mine, mine, ss, rs, b + j, (cx, cy, c)).wait_send()

    ride.op(3, start, finish)


def _ride_gather_direct(ride, name, a):
    ride.inout(name, a)

    def start(view, ss, rs, b):
        x, y, c = _place()
        mine = _half(view[name], 2 * x + y, c)
        for j, (cx, cy) in enumerate(_other_chips(x, y)):
            _remote(mine, mine, ss, rs, b + 2 * j, (cx, cy, c)).start()
            _remote(mine, mine, ss, rs, b + 2 * j + 1, (cx, cy, 1 - c)).start()

    def finish(view, ss, rs, b):
        x, y, c = _place()
        mine = _half(view[name], 2 * x + y, c)
        for j, (cx, cy) in enumerate(_other_chips(x, y)):
            same = _half(view[name], 2 * cx + cy, c)
            other = _half(view[name], 2 * cx + cy, 1 - c)
            _remote(same, same, ss, rs, b + 2 * j, (cx, cy, c)).wait_recv()
            _remote(other, other, ss, rs, b + 2 * j + 1, (cx, cy, 1 - c)).wait_recv()
        for j, (cx, cy) in enumerate(_other_chips(x, y)):
            _remote(mine, mine, ss, rs, b + 2 * j, (cx, cy, c)).wait_send()
            _remote(mine, mine, ss, rs, b + 2 * j + 1, (cx, cy, 1 - c)).wait_send()

    ride.op(6, start, finish)


def _ride_gather_pass(ride, name, a):
    ride.inout(name, a)

    def start(view, ss, rs, b):
        x, y, c = _place()
        for j, (cx, cy) in enumerate(_other_chips(x, y)):
            blk = _half(view[name], 2 * cx + cy, c)
            _remote(blk, blk, ss, rs, b + j, (x, y, 1 - c)).start()

    def finish(view, ss, rs, b):
        x, y, c = _place()
        for j, (cx, cy) in enumerate(_other_chips(x, y)):
            theirs = _half(view[name], 2 * cx + cy, 1 - c)
            _remote(theirs, theirs, ss, rs, b + j, (x, y, 1 - c)).wait_recv()
        for j, (cx, cy) in enumerate(_other_chips(x, y)):
            blk = _half(view[name], 2 * cx + cy, c)
            _remote(blk, blk, ss, rs, b + j, (x, y, 1 - c)).wait_send()

    ride.op(3, start, finish)


def _ride_exchange(ride, src_name, src, dst_name, dst, rels, part=(0, 1, 1)):
    ride.read(src_name, src)
    if dst is None:
        ride.land(dst_name, (3,) + src.shape[1:], src.dtype)
    else:
        ride.inout(dst_name, dst)
    lo, hi, k = part
    rows = pl.ds(lo * src.shape[1] // k, (hi - lo) * src.shape[1] // k)

    def copy(view, ss, rs, sem, j, x, y, c):
        cx, cy = _other_chips(x, y)[j]
        return _remote(view[src_name].at[2 * cx + cy, rows], view[dst_name].at[j, rows], ss, rs, sem, (cx, cy, c))

    def start(view, ss, rs, b):
        x, y, c = _place()
        for i, j in enumerate(rels):
            copy(view, ss, rs, b + i, j, x, y, c).start()

    def finish(view, ss, rs, b):
        x, y, c = _place()
        for i, j in enumerate(rels):
            copy(view, ss, rs, b + i, j, x, y, c).wait()

    ride.op(len(rels), start, finish)


def _ride_swap(ride, src_name, src, dst_name):
    ride.read(src_name, src)
    ride.land(dst_name, src.shape, src.dtype)

    def start(view, ss, rs, b):
        x, y, c = _place()
        _remote(view[src_name], view[dst_name], ss, rs, b, (x, y, 1 - c)).start()

    def finish(view, ss, rs, b):
        x, y, c = _place()
        _remote(view[src_name], view[dst_name], ss, rs, b, (x, y, 1 - c)).wait()

    ride.op(1, start, finish)


def _peers(x, y, c):
    flip = lambda v, b: 1 - v if b else v
    return [(flip(x, r & 4), flip(y, r & 2), flip(c, r & 1)) for r in range(1, 8)]


def _ride_gather_all(ride, src_name, src, dst_name):
    ride.read(src_name, src)
    ride.land(dst_name, (8,) + src.shape, src.dtype)

    def start(view, ss, rs, b):
        x, y, c = _place()
        me = 4 * x + 2 * y + c
        pltpu.make_async_copy(view[src_name], view[dst_name].at[me], ss.at[b + 7]).start()
        for r, peer in enumerate(_peers(x, y, c)):
            _remote(view[src_name], view[dst_name].at[me], ss, rs, b + r, peer).start()

    def finish(view, ss, rs, b):
        x, y, c = _place()
        me = 4 * x + 2 * y + c
        for r, peer in enumerate(_peers(x, y, c)):
            theirs = view[dst_name].at[4 * peer[0] + 2 * peer[1] + peer[2]]
            _remote(theirs, theirs, ss, rs, b + r, peer).wait_recv()
        for r, peer in enumerate(_peers(x, y, c)):
            _remote(view[src_name], view[dst_name].at[me], ss, rs, b + r, peer).wait_send()
        pltpu.make_async_copy(view[src_name], view[dst_name].at[me], ss.at[b + 7]).wait()

    ride.op(8, start, finish)


def _cast_win(chip, w_in, l):
    _, D, W = w_in.shape
    tr = min(256, D)

    def body(chip_ref, w_ref, o_ref):
        o_ref[...] = w_ref[...].astype(BF16)

    return pl.pallas_call(
        body, name="cast_win",
        out_shape=jax.ShapeDtypeStruct((4, D, W), BF16),
        grid_spec=pltpu.PrefetchScalarGridSpec(
            num_scalar_prefetch=1, grid=(D // tr,),
            in_specs=[pl.BlockSpec((None, tr, W), lambda r, c: (l, r, 0))],
            out_specs=pl.BlockSpec((None, tr, W), lambda r, c: (c[0], r, 0))),
        compiler_params=_cp(("arbitrary",)),
    )(chip, w_in)


def _cast_wsq(chip, w_ro, w_co, w_o, l):
    _, R, D = w_ro.shape

    def body(chip_ref, a_ref, b_ref, c_ref, o_ref):
        o_ref[0:R, :] = a_ref[...].astype(BF16)
        o_ref[R:2 * R, :] = b_ref[...].astype(BF16)
        o_ref[2 * R:3 * R, :] = c_ref[...].astype(BF16)

    spec = pl.BlockSpec((None, R, D), lambda i, c: (l, 0, 0))
    return pl.pallas_call(
        body, name="cast_wsq",
        out_shape=jax.ShapeDtypeStruct((4, 3 * R, D), BF16),
        grid_spec=pltpu.PrefetchScalarGridSpec(
            num_scalar_prefetch=1, grid=(1,),
            in_specs=[spec, spec, spec],
            out_specs=pl.BlockSpec((None, 3 * R, D), lambda i, c: (c[0], 0, 0))),
        compiler_params=_cp(("arbitrary",)),
    )(chip, w_ro, w_co, w_o)


def _place_cw(chip, conv_w):
    NL, K, Cc = conv_w.shape

    def body(chip_ref, w_ref, o_ref):
        o_ref[...] = w_ref[...]

    return pl.pallas_call(
        body, name="place_cw",
        out_shape=jax.ShapeDtypeStruct((4, NL, K, Cc), F32),
        grid_spec=pltpu.PrefetchScalarGridSpec(
            num_scalar_prefetch=1, grid=(1,),
            in_specs=[pl.BlockSpec((NL, K, Cc), lambda i, c: (0, 0, 0))],
            out_specs=pl.BlockSpec((None, NL, K, Cc), lambda i, c: (c[0], 0, 0, 0))),
        compiler_params=_cp(("arbitrary",)),
    )(chip, conv_w)


def _gather_first(win0, wcw):
    n_arr = 2

    def body(a0, a1, o0, o1, send_sems, recv_sems):
        x, y, c = _place()
        sibling = (x, y, 1 - c)
        chips = _other_chips(x, y)
        outs = (o0, o1)

        def copy(k, a, cx, cy, cc, to):
            blk = _half(outs[a], 2 * cx + cy, cc)
            return _remote(blk, blk, send_sems, recv_sems, k, to)

        first = [copy(3 * a + j, a, x, y, c, (*chip, c)) for a in range(n_arr) for j, chip in enumerate(chips)]
        for cp in first:
            cp.start()
        passed = [copy(3 * n_arr + 3 * a + j, a, *chip, c, sibling) for a in range(n_arr) for j, chip in enumerate(chips)]
        for a in range(n_arr):
            for j, chip in enumerate(chips):
                copy(3 * a + j, a, *chip, c, sibling).wait_recv()
                passed[3 * a + j].start()
        for a in range(n_arr):
            for j, chip in enumerate(chips):
                copy(3 * n_arr + 3 * a + j, a, *chip, 1 - c, sibling).wait_recv()
        for cp in first + passed:
            cp.wait_send()

    ins = (win0, wcw)
    return pl.pallas_call(
        body, name="gather_first",
        out_shape=tuple(jax.ShapeDtypeStruct(a.shape, a.dtype) for a in ins),
        in_specs=[ANY] * n_arr, out_specs=(ANY,) * n_arr,
        scratch_shapes=[pltpu.SemaphoreType.DMA((6 * n_arr,)), pltpu.SemaphoreType.DMA((6 * n_arr,))],
        input_output_aliases={0: 0, 1: 1},
        compiler_params=pltpu.CompilerParams(has_side_effects=True),
    )(*ins)


def _fwd_in(x, pre_g, win, l, ride=None):
    T, D = x.shape
    W = win.shape[-1]
    tm = min(512, T)

    def body(x_ref, g_ref, w_ref, p_ref, h_ref):
        hb = _rms(x_ref[...], g_ref[...]).astype(BF16)
        h_ref[...] = hb
        for j in range(4):
            p_ref[:, j * W:(j + 1) * W] = _dot(hb, w_ref[j]).astype(BF16)

    return _pcall(
        body, name="fwd_in",
        out_shape=(jax.ShapeDtypeStruct((T, 4 * W), BF16), jax.ShapeDtypeStruct((T, D), BF16)),
        grid=(T // tm,),
        in_specs=[pl.BlockSpec((tm, D), lambda i: (i, 0)),
                  pl.BlockSpec((None, 1, D), lambda i: (l, 0, 0)),
                  pl.BlockSpec((4, D, W), lambda i: (0, 0, 0), pipeline_mode=pl.Buffered(1))],
        out_specs=(pl.BlockSpec((tm, 4 * W), lambda i: (i, 0)),
                   pl.BlockSpec((tm, D), lambda i: (i, 0))),
        sem=("arbitrary",), args=(x, pre_g, win), ride=ride)


def _ret_specs(T, L):
    rope = pl.BlockSpec((3, L, 128), lambda s: (0, s, 0))
    mask = pl.BlockSpec((HEADS, L, L), lambda s: (0, 0, 0), pipeline_mode=pl.Buffered(1))
    qdm = pl.BlockSpec((HEADS, L, 128), lambda s: (0, 0, 0), pipeline_mode=pl.Buffered(1))
    small = pl.BlockSpec((HEADS, 8, 128), lambda s: (0, 0, 0))
    return [rope, mask, qdm, qdm, small, small]


QK = HEADS * DK
VW = HEADS * DV
PW = 2 * QK + 2 * VW


def _zero_at_start(ref):
    @pl.when(pl.program_id(0) == 0)
    def _():
        ref[...] = jnp.zeros_like(ref)


def _ret_fwd_part(p_ref, rope_ref, m_ref, qdm_ref, kdm_ref, mqs_ref, cd_ref, a_ref, st_ref, state):
    c, sl, sh = rope_ref[0], rope_ref[1], rope_ref[2]
    for j in range(HEADS // 2):
        rq = _rot(p_ref[:, 128 * j:128 * (j + 1)].astype(F32), c, sl, sh)
        rk = _rot(p_ref[:, QK + 128 * j:QK + 128 * (j + 1)].astype(F32), c, sl, sh)
        rkb = rk.astype(BF16)
        for e in range(2):
            h = 2 * j + e
            v = p_ref[:, 2 * QK + DV * h:2 * QK + DV * (h + 1)]
            g = p_ref[:, 2 * QK + VW + DV * h:2 * QK + VW + DV * (h + 1)].astype(F32)
            a = (rq * mqs_ref[h, 0:1, :]).astype(BF16)
            p = (_dot_nt(a, rkb) * m_ref[h]).astype(BF16)
            st = state[h]
            st_ref[h] = st
            o = _dot(p, v) + _dot((rq * qdm_ref[h]).astype(BF16), st.astype(BF16))
            state[h] = st * cd_ref[h, 0:1, :] + _dot_tn((rk * kdm_ref[h]).astype(BF16), v)
            a_ref[:, DV * h:DV * (h + 1)] = _gn_gate(o, g).astype(BF16)


def _shift_copies(src, sh):
    rows = sh.shape[1]
    for b in range(1, 8):
        sh[b - 1, :, :] = src[pl.ds(b, rows), :]


def _window(src, sh, r0, const, rows, lanes):
    b = const % 8
    at = pl.ds(pl.multiple_of(r0 + (const - b), 8), rows)
    w = src[at, lanes] if b == 0 else sh[b - 1, at, lanes]
    return w.reshape(rows // 8, 8, w.shape[-1])


def _conv_taps(wbuf, src, sh, r0, const, rows, lanes):
    groups = rows // 8
    accs = [None] * groups
    for k in range(CONV_K):
        w8 = wbuf[pl.ds(8 * k, 8), lanes]
        win = _window(src, sh, r0, const + k, rows, lanes)
        for g in range(groups):
            term = w8 * win[g]
            accs[g] = term if k == 0 else accs[g] + term
    return jnp.concatenate(accs, axis=0)


def _lane_parts(C):
    return [pl.ds(j * CONV_LANES, CONV_LANES) for j in range(C // CONV_LANES)]


def _load_conv_w(cw_ref, wbuf, flip):
    for k in range(CONV_K):
        row = jnp.concatenate([cw_ref[c, pl.ds(k, 1), :] for c in range(4)], axis=-1)
        kk = CONV_K - 1 - k if flip else k
        wbuf[pl.ds(8 * kk, 8), :] = jnp.broadcast_to(row, (8, row.shape[-1]))


def _conv_fwd_part(p_ref, cw_ref, cb_ref, lg_ref, lb_ref, a_ref, y_ref, ubuf, wbuf, ush):
    tc, C = y_ref.shape
    off = HALO - (CONV_K - 1)
    i = pl.program_id(0)

    @pl.when(i == 0)
    def _():
        ubuf[0:HALO, :] = jnp.zeros((HALO, C), F32)
        _load_conv_w(cw_ref, wbuf, False)

    @pl.when(i > 0)
    def _():
        ubuf[0:HALO, :] = ubuf[tc:tc + HALO, :]

    ga = p_ref[:, 0:C].astype(F32)
    gb = p_ref[:, C:2 * C].astype(F32)
    ubuf[HALO:HALO + tc, :] = ga * _sigmoid(gb)
    _shift_copies(ubuf, ush)

    def rows_block(r, carry):
        r0 = pl.multiple_of(r * CONV_RB, CONV_RB)
        for lanes in _lane_parts(C):
            y_ref[pl.ds(r0, CONV_RB), lanes] = _conv_taps(wbuf, ubuf, ush, r0, off, CONV_RB, lanes) + cb_ref[:, lanes]
        return carry

    lax.fori_loop(0, tc // CONV_RB, rows_block, 0)
    gc = p_ref[:, 2 * C:3 * C].astype(F32)
    a_ref[...] = _ln_gate(y_ref[...], gc, lg_ref[...], lb_ref[...]).astype(BF16)


def _ret_fwd(proj, tb, L, ride=None):
    T = proj.shape[0]
    nS = T // L

    def body(p_ref, rope_ref, m_ref, qdm_ref, kdm_ref, mqs_ref, cd_ref, ar_ref, st_ref, state):
        _zero_at_start(state)
        _ret_fwd_part(p_ref, rope_ref, m_ref, qdm_ref, kdm_ref, mqs_ref, cd_ref, ar_ref, st_ref, state)

    return _pcall(
        body, name="ret_fwd",
        out_shape=(jax.ShapeDtypeStruct((T, VW), BF16), jax.ShapeDtypeStruct((nS, HEADS, 128, DV), F32)),
        grid=(nS,),
        in_specs=[pl.BlockSpec((L, PW), lambda s: (s, 0))] + _ret_specs(T, L),
        out_specs=(pl.BlockSpec((L, VW), lambda s: (s, 0)), pl.BlockSpec((None, HEADS, 128, DV), lambda s: (s, 0, 0, 0))),
        scratch_shapes=[pltpu.VMEM((HEADS, 128, DV), F32)],
        sem=("arbitrary",), ride=ride,
        args=(proj, tb["rope"], tb["mask"], tb["qdm"], tb["kdm"], tb["mqs"], tb["cd"]))


def _conv_fwd(proj, wcw, conv_b, ln_g, ln_b, l, ride=None):
    T = proj.shape[0]
    C = conv_b.shape[-1]
    Cc = wcw.shape[-1]
    tc = min(CONV_TILE, T)
    assert PW == 3 * C

    def body(p_ref, cw_ref, cb_ref, lg_ref, lb_ref, ac_ref, y_ref, ubuf, wbuf, ush):
        _conv_fwd_part(p_ref, cw_ref, cb_ref, lg_ref, lb_ref, ac_ref, y_ref, ubuf, wbuf, ush)

    vec = pl.BlockSpec((None, 1, C), lambda i: (l, 0, 0))
    tile = pl.BlockSpec((tc, C), lambda i: (i, 0))
    return _pcall(
        body, name="conv_fwd",
        out_shape=(jax.ShapeDtypeStruct((T, C), BF16), jax.ShapeDtypeStruct((T, C), F32)),
        grid=(T // tc,),
        in_specs=[pl.BlockSpec((tc, 3 * C), lambda i: (i, 1)),
                  pl.BlockSpec((4, None, CONV_K, Cc), lambda i: (0, l, 0, 0)), vec, vec, vec],
        out_specs=(tile, tile),
        scratch_shapes=[pltpu.VMEM((HALO + tc, C), F32), pltpu.VMEM((8 * CONV_K, C), F32),
                        pltpu.VMEM((7, HALO + tc - 8, C), F32)],
        sem=("arbitrary",), ride=ride, args=(proj, wcw, conv_b, ln_g, ln_b))


def _merge_fwd(x, proj, a_ret, a_conv, wsq, post_g, l, ride=None):
    T, D = x.shape
    R = wsq.shape[1] // 3
    tm = min(512, T)

    def body(x_ref, p_ref, ar_ref, ac_ref, wro_ref, wco_ref, wo_ref, g_ref, xn_ref, ro_ref, co_ref, ym_ref, z_ref):
        ro = _dot(ar_ref[...], wro_ref[...].reshape(4 * R, D))
        co = _dot(ac_ref[...], wco_ref[...].reshape(4 * R, D))
        ym = (_sigmoid(p_ref[:, 0:D].astype(F32)) * ro + _sigmoid(p_ref[:, D:2 * D].astype(F32)) * co).astype(BF16)
        z = _dot(ym, wo_ref[...].reshape(4 * R, D))
        ro_ref[...] = ro.astype(BF16)
        co_ref[...] = co.astype(BF16)
        ym_ref[...] = ym
        z_ref[...] = z.astype(BF16)
        xn_ref[...] = x_ref[...] + _rms(z, g_ref[...])

    tile = pl.BlockSpec((tm, D), lambda i: (i, 0))
    wspec = lambda m: pl.BlockSpec((4, R, D), lambda i: (0, m, 0))
    act = jax.ShapeDtypeStruct((T, D), BF16)
    return _pcall(
        body, name="merge_fwd",
        out_shape=(jax.ShapeDtypeStruct((T, D), F32), act, act, act, act),
        grid=(T // tm,),
        in_specs=[tile, pl.BlockSpec((tm, 2 * D), lambda i: (i, 3)), tile, tile,
                  wspec(0), wspec(1), wspec(2), pl.BlockSpec((None, 1, D), lambda i: (l, 0, 0))],
        out_specs=(tile, tile, tile, tile, tile),
        sem=("arbitrary",), ride=ride, args=(x, proj, a_ret, a_conv, wsq, wsq, wsq, post_g))


def _loss_fwd_bwd(y, target):
    T, D = y.shape
    tm = min(512, T)

    def body(y_ref, t_ref, dy_ref, ls_ref):
        @pl.when(pl.program_id(0) == 0)
        def _():
            ls_ref[...] = jnp.zeros_like(ls_ref)

        e = y_ref[...] - t_ref[...]
        dy_ref[...] = e * (1.0 / D)
        ls_ref[...] += jnp.sum((e * e).reshape(tm // 8, 8, D), axis=0) * (0.5 / D)

    tile = pl.BlockSpec((tm, D), lambda i: (i, 0))
    return pl.pallas_call(
        body, name="loss",
        out_shape=(jax.ShapeDtypeStruct((T, D), F32), jax.ShapeDtypeStruct((8, D), F32)),
        grid=(T // tm,),
        in_specs=[tile, tile],
        out_specs=(tile, pl.BlockSpec((8, D), lambda i: (0, 0))),
        compiler_params=_cp(("arbitrary",)),
    )(y, target)


def _merge_bwd(dxn, proj, a_ret, a_conv, ro, co, ym, z, wsq, post_g, l, ride=None):
    T, D = dxn.shape
    R = wsq.shape[1] // 3
    tm = min(512, T)
    n = T // tm

    def body(dx_ref, p_ref, ar_ref, ac_ref, ro_ref, co_ref, ym_ref, z_ref, wro_ref, wco_ref, wo_ref, g_ref,
             dp_ref, dar_ref, dac_ref, gsq_ref, dg_ref, acc, stage):
        i = pl.program_id(0)

        @pl.when(i == 0)
        def _():
            acc[...] = jnp.zeros_like(acc)
            dg_ref[...] = jnp.zeros_like(dg_ref)

        _, vjp = jax.vjp(_rms, z_ref[...].astype(F32), g_ref[...])
        dz, dg = vjp(dx_ref[...])
        dg_ref[0:1, :] += dg
        dzb = dz.astype(BF16)
        dym = _dot_nt(dzb, wo_ref[...].reshape(4 * R, D))
        acc[2] += _dot_tn(ym_ref[...], dzb)
        sr = _sigmoid(p_ref[:, 0:D].astype(F32))
        sc = _sigmoid(p_ref[:, D:2 * D].astype(F32))
        rov = ro_ref[...].astype(F32)
        cov = co_ref[...].astype(F32)
        dp_ref[:, 0:D] = (dym * rov * sr * (1.0 - sr)).astype(BF16)
        dp_ref[:, D:2 * D] = (dym * cov * sc * (1.0 - sc)).astype(BF16)
        dro = (dym * sr).astype(BF16)
        dco = (dym * sc).astype(BF16)
        dar_ref[...] = _dot_nt(dro, wro_ref[...].reshape(4 * R, D)).astype(BF16)
        dac_ref[...] = _dot_nt(dco, wco_ref[...].reshape(4 * R, D)).astype(BF16)
        acc[0] += _dot_tn(ar_ref[...], dro)
        acc[1] += _dot_tn(ac_ref[...], dco)

        @pl.when(i == n - 1)
        def _():
            for m in range(3):
                stage[...] = acc[m].astype(BF16).reshape(4, R, D)
                pltpu.sync_copy(stage, gsq_ref.at[:, pl.ds(m * R, R), :])

    tile = pl.BlockSpec((tm, D), lambda i: (i, 0))
    wspec = lambda m: pl.BlockSpec((4, R, D), lambda i: (0, m, 0), pipeline_mode=pl.Buffered(1))
    return _pcall(
        body, name="merge_bwd",
        out_shape=(jax.ShapeDtypeStruct(proj.shape, BF16), jax.ShapeDtypeStruct((T, D), BF16),
                   jax.ShapeDtypeStruct((T, D), BF16), jax.ShapeDtypeStruct(wsq.shape, BF16),
                   jax.ShapeDtypeStruct((8, D), F32)),
        grid=(n,),
        in_specs=[tile, pl.BlockSpec((tm, 2 * D), lambda i: (i, 3)), tile, tile, tile, tile, tile, tile,
                  wspec(0), wspec(1), wspec(2), pl.BlockSpec((None, 1, D), lambda i: (l, 0, 0))],
        out_specs=(pl.BlockSpec((tm, 2 * D), lambda i: (i, 3)), tile, tile, ANY, pl.BlockSpec((8, D), lambda i: (0, 0))),
        scratch_shapes=[pltpu.VMEM((3, 4 * R, D), F32), pltpu.VMEM((4, R, D), BF16)],
        sem=("arbitrary",), args=(dxn, proj, a_ret, a_conv, ro, co, ym, z, wsq, wsq, wsq, post_g), ride=ride)


def _conv_bwd_part(n, da_ref, y_ref, p_ref, ph_ref, cw_ref, lg_ref, lb_ref, dp_ref,
                   dcbuf, ubuf, dubuf, wbuf, dwacc, vacc, dsh, ush):
    tc, C = y_ref.shape
    off = HALO - (CONV_K - 1)
    nrb = tc // CONV_RB
    t = pl.program_id(0)
    i = n - 1 - t

    @pl.when(t == 0)
    def _():
        dcbuf[tc:tc + HALO, :] = jnp.zeros((HALO, C), F32)
        dwacc[...] = jnp.zeros_like(dwacc)
        vacc[...] = jnp.zeros_like(vacc)
        _load_conv_w(cw_ref, wbuf, True)

    @pl.when(t > 0)
    def _():
        dcbuf[tc:tc + HALO, :] = dcbuf[0:HALO, :]

    gc = p_ref[:, 2 * C:3 * C].astype(F32)
    _, vjp = jax.vjp(_ln_gate, y_ref[...], gc, lg_ref[...], lb_ref[...])
    dy, dgc, dlg, dlb = vjp(da_ref[...].astype(F32))
    dcbuf[0:tc, :] = dy
    dp_ref[:, 2 * C:3 * C] = dgc.astype(BF16)
    vacc[0:1, :] += jnp.sum(dy, axis=0, keepdims=True)
    vacc[1:2, :] += dlg
    vacc[2:3, :] += dlb

    ga = p_ref[:, 0:C].astype(F32)
    sb = _sigmoid(p_ref[:, C:2 * C].astype(F32))
    ubuf[HALO:HALO + tc, :] = ga * sb
    uh = ph_ref[:, 0:C].astype(F32) * _sigmoid(ph_ref[:, C:2 * C].astype(F32))
    ubuf[0:HALO, :] = jnp.where(i > 0, uh, 0.0)

    _shift_copies(dcbuf, dsh)
    _shift_copies(ubuf, ush)
    def du_block(r, carry):
        r0 = pl.multiple_of(r * CONV_RB, CONV_RB)
        for lanes in _lane_parts(C):
            dubuf[pl.ds(r0, CONV_RB), lanes] = _conv_taps(wbuf, dcbuf, dsh, r0, 0, CONV_RB, lanes)
        return carry

    lax.fori_loop(0, nrb, du_block, 0)
    du = dubuf[...]
    dp_ref[:, 0:C] = (du * sb).astype(BF16)
    dp_ref[:, C:2 * C] = (du * ga * sb * (1.0 - sb)).astype(BF16)

    def dw_block(r, carry):
        r0 = pl.multiple_of(r * CONV_RB, CONV_RB)
        for lanes in _lane_parts(C):
            dyb = dcbuf[pl.ds(r0, CONV_RB), lanes].reshape(CONV_RB // 8, 8, CONV_LANES)
            for k in range(CONV_K):
                dwacc[8 * k:8 * k + 8, lanes] += jnp.sum(dyb * _window(ubuf, ush, r0, off + k, CONV_RB, lanes), axis=0)
        return carry

    lax.fori_loop(0, nrb, dw_block, 0)


def _conv_bwd_final(n, sg_ref, dwacc, vacc):
    C = sg_ref.shape[-1]

    @pl.when(pl.program_id(0) == n - 1)
    def _():
        for k in range(CONV_K):
            sg_ref[pl.ds(k, 1), :] = jnp.sum(dwacc[8 * k:8 * k + 8, :], axis=0, keepdims=True)
        sg_ref[pl.ds(CONV_K, 1), :] = jnp.zeros((1, C), F32)
        sg_ref[ROW_CB:ROW_CB + 8, :] = jnp.zeros((8, C), F32)
        sg_ref[ROW_CB:ROW_CB + 3, :] = vacc[0:3, :]


def _ret_bwd_part(da_ref, p_ref, st_ref, rope_ref, m_ref, qdm_ref, kdm_ref, mqs_ref, cd_ref, dp_ref, gst):
    c, sl, sh = rope_ref[0], rope_ref[1], rope_ref[2]
    for j in range(HEADS // 2):
        rq = _rot(p_ref[:, 128 * j:128 * (j + 1)].astype(F32), c, sl, sh)
        rk = _rot(p_ref[:, QK + 128 * j:QK + 128 * (j + 1)].astype(F32), c, sl, sh)
        rkb = rk.astype(BF16)
        drq = jnp.zeros_like(rq)
        drk = jnp.zeros_like(rk)
        for e in range(2):
            h = 2 * j + e
            v = p_ref[:, 2 * QK + DV * h:2 * QK + DV * (h + 1)]
            g = p_ref[:, 2 * QK + VW + DV * h:2 * QK + VW + DV * (h + 1)].astype(F32)
            mqs = mqs_ref[h, 0:1, :]
            a = (rq * mqs).astype(BF16)
            aq = (rq * qdm_ref[h]).astype(BF16)
            kdv = (rk * kdm_ref[h]).astype(BF16)
            mk = m_ref[h]
            p = (_dot_nt(a, rkb) * mk).astype(BF16)
            stb = st_ref[h].astype(BF16)
            o = _dot(p, v) + _dot(aq, stb)
            _, vjp = jax.vjp(_gn_gate, o, g)
            do, dg = vjp(da_ref[:, DV * h:DV * (h + 1)].astype(F32))
            dob = do.astype(BF16)
            gs = gst[h]
            gsb = gs.astype(BF16)
            ds = (_dot_nt(dob, v) * mk).astype(BF16)
            drq = drq + _dot(ds, rkb) * mqs + _dot_nt(dob, stb) * qdm_ref[h]
            drk = drk + _dot_tn(ds, a) + _dot_nt(v, gsb) * kdm_ref[h]
            dv = _dot_tn(p, dob) + _dot(kdv, gsb)
            gst[h] = _dot_tn(aq, dob) + gs * cd_ref[h, 0:1, :]
            dp_ref[:, 2 * QK + DV * h:2 * QK + DV * (h + 1)] = dv.astype(BF16)
            dp_ref[:, 2 * QK + VW + DV * h:2 * QK + VW + DV * (h + 1)] = dg.astype(BF16)
        dp_ref[:, 128 * j:128 * (j + 1)] = _rot_t(drq, c, sl, sh).astype(BF16)
        dp_ref[:, QK + 128 * j:QK + 128 * (j + 1)] = _rot_t(drk, c, sl, sh).astype(BF16)


def _ret_bwd(dproj, da_ret, proj, states, tb, L, ride=None):
    T = proj.shape[0]
    nS = T // L

    def body(dpin_ref, dar_ref, p_ref, st_ref, rope_ref, m_ref, qdm_ref, kdm_ref, mqs_ref, cd_ref, dp_ref, gst):
        _zero_at_start(gst)
        _ret_bwd_part(dar_ref, p_ref, st_ref, rope_ref, m_ref, qdm_ref, kdm_ref, mqs_ref, cd_ref, dp_ref, gst)

    rev = lambda s: nS - 1 - s
    specs = _ret_specs(T, L)
    specs[0] = pl.BlockSpec((3, L, 128), lambda s: (0, rev(s), 0))
    ptile = pl.BlockSpec((L, PW), lambda s: (rev(s), 0))
    return _pcall(
        body, name="ret_bwd",
        out_shape=(jax.ShapeDtypeStruct(dproj.shape, BF16),),
        grid=(nS,),
        in_specs=[ANY, pl.BlockSpec((L, VW), lambda s: (rev(s), 0)), ptile,
                  pl.BlockSpec((None, HEADS, 128, DV), lambda s: (rev(s), 0, 0, 0))] + specs,
        out_specs=(ptile,),
        scratch_shapes=[pltpu.VMEM((HEADS, 128, DV), F32)],
        sem=("arbitrary",), aliases={0: 0}, ride=ride,
        args=(dproj, da_ret, proj, states, tb["rope"], tb["mask"], tb["qdm"], tb["kdm"], tb["mqs"], tb["cd"]))


def _conv_bwd(dproj, da_conv, y, proj, wcw, ln_g, ln_b, l, ride=None):
    T, C = y.shape
    Cc = wcw.shape[-1]
    tc = min(CONV_TILE, T)
    n = T // tc
    hb = tc // HALO

    def body(dpin_ref, dac_ref, y_ref, p_ref, ph_ref, cw_ref, lg_ref, lb_ref, dp_ref, sg_ref,
             dcbuf, ubuf, dubuf, wbuf, dwacc, vacc, dsh, ush):
        _conv_bwd_part(n, dac_ref, y_ref, p_ref, ph_ref, cw_ref, lg_ref, lb_ref, dp_ref,
                       dcbuf, ubuf, dubuf, wbuf, dwacc, vacc, dsh, ush)
        _conv_bwd_final(n, sg_ref, dwacc, vacc)

    rev = lambda t: n - 1 - t
    vec = pl.BlockSpec((None, 1, C), lambda t: (l, 0, 0))
    tile = pl.BlockSpec((tc, C), lambda t: (rev(t), 0))
    ptile = pl.BlockSpec((tc, 3 * C), lambda t: (rev(t), 1))
    halo = pl.BlockSpec((HALO, 3 * C), lambda t: (jnp.maximum(rev(t) * hb - 1, 0), 1))
    return _pcall(
        body, name="conv_bwd",
        out_shape=(jax.ShapeDtypeStruct(dproj.shape, BF16), jax.ShapeDtypeStruct((ROW_PRE, C), F32)),
        grid=(n,),
        in_specs=[ANY, tile, tile, ptile, halo, pl.BlockSpec((4, None, CONV_K, Cc), lambda t: (0, l, 0, 0)), vec, vec],
        out_specs=(ptile, pl.BlockSpec((ROW_PRE, C), lambda t: (0, 0))),
        scratch_shapes=[pltpu.VMEM((tc + HALO, C), F32), pltpu.VMEM((HALO + tc, C), F32), pltpu.VMEM((tc, C), F32),
                        pltpu.VMEM((8 * CONV_K, C), F32), pltpu.VMEM((8 * CONV_K, C), F32), pltpu.VMEM((8, C), F32),
                        pltpu.VMEM((7, HALO + tc - 8, C), F32), pltpu.VMEM((7, HALO + tc - 8, C), F32)],
        sem=("arbitrary",), aliases={0: 0}, ride=ride, args=(dproj, da_conv, y, proj, proj, wcw, ln_g, ln_b))


def _win_grad(h, dproj, W):
    T, D = h.shape
    tk = min(2048, T)
    nk = T // tk

    def body(h_ref, dp_ref, g_ref, acc):
        k = pl.program_id(1)

        @pl.when(k == 0)
        def _():
            acc[...] = jnp.zeros_like(acc)

        acc[...] += _dot_tn(h_ref[...], dp_ref[...])

        @pl.when(k == nk - 1)
        def _():
            g_ref[...] = acc[...].astype(BF16)

    return pl.pallas_call(
        body, name="win_grad",
        out_shape=jax.ShapeDtypeStruct((4, D, W), BF16),
        grid=(4, nk),
        in_specs=[pl.BlockSpec((tk, D), lambda j, k: (k, 0)), pl.BlockSpec((tk, W), lambda j, k: (k, j))],
        out_specs=pl.BlockSpec((None, D, W), lambda j, k: (j, 0, 0)),
        scratch_shapes=[pltpu.VMEM((D, W), F32)],
        compiler_params=_cp(("arbitrary", "arbitrary")),
    )(h, dproj)


def _in_bwd(dxn, dproj, x, pre_g, win, l, ride=None):
    T, D = x.shape
    W = win.shape[-1]
    tm = min(512, T)

    def body(dxn_ref, dp_ref, x_ref, g_ref, w_ref, dx_ref, dg_ref):
        @pl.when(pl.program_id(0) == 0)
        def _():
            dg_ref[...] = jnp.zeros_like(dg_ref)

        dh = _dot_nt(dp_ref[:, 0:W], w_ref[0])
        for j in range(1, 4):
            dh = dh + _dot_nt(dp_ref[:, j * W:(j + 1) * W], w_ref[j])
        _, vjp = jax.vjp(_rms, x_ref[...], g_ref[...])
        dx, dg = vjp(dh)
        dx_ref[...] = dxn_ref[...] + dx
        dg_ref[0:1, :] += dg

    tile = pl.BlockSpec((tm, D), lambda i: (i, 0))
    return _pcall(
        body, name="in_bwd",
        out_shape=(jax.ShapeDtypeStruct((T, D), F32), jax.ShapeDtypeStruct((8, D), F32)),
        grid=(T // tm,),
        in_specs=[tile, pl.BlockSpec((tm, 4 * W), lambda i: (i, 0)), tile,
                  pl.BlockSpec((None, 1, D), lambda i: (l, 0, 0)),
                  pl.BlockSpec((4, D, W), lambda i: (0, 0, 0), pipeline_mode=pl.Buffered(1))],
        out_specs=(tile, pl.BlockSpec((8, D), lambda i: (0, 0))),
        sem=("arbitrary",), args=(dxn, dproj, x, pre_g, win), ride=ride)


def _sum_group(chip, t, u):
    _, A, B = t.shape
    tr = min(256, A)

    def body(k_ref, t_ref, u_ref, o_ref):
        o_ref[...] = ((t_ref[...].astype(F32) + u_ref[0].astype(F32)) + u_ref[1].astype(F32)) + u_ref[2].astype(F32)

    return pl.pallas_call(
        body, name="sum_group",
        out_shape=jax.ShapeDtypeStruct((A, B), F32),
        grid_spec=pltpu.PrefetchScalarGridSpec(
            num_scalar_prefetch=1, grid=(A // tr,),
            in_specs=[pl.BlockSpec((None, tr, B), lambda i, k: (k[0], i, 0)),
                      pl.BlockSpec((3, tr, B), lambda i, k: (0, i, 0))],
            out_specs=pl.BlockSpec((tr, B), lambda i, k: (i, 0))),
        compiler_params=_cp(("arbitrary",)),
    )(chip, t, u)


def _swap_rows(g):
    _, A, B = g.shape
    nh = A // 2

    def body(g_ref, r_ref, send_sems, recv_sems):
        x, y, c = _place()
        cp = _remote(g_ref.at[:, pl.ds((1 - c) * nh, nh)], r_ref, send_sems, recv_sems, 0, (x, y, 1 - c))
        cp.start()
        cp.wait()

    return pl.pallas_call(
        body, name="swap_rows",
        out_shape=jax.ShapeDtypeStruct((4, nh, B), g.dtype),
        in_specs=[ANY], out_specs=ANY,
        scratch_shapes=[pltpu.SemaphoreType.DMA((1,)), pltpu.SemaphoreType.DMA((1,))],
        compiler_params=pltpu.CompilerParams(has_side_effects=True),
    )(g)


def _add_rows(cidx, g, r):
    _, nh, B = r.shape
    tr = min(256, nh)
    nb = nh // tr

    def body(c_ref, g_ref, r_ref, o_ref):
        o_ref[...] = (g_ref[...].astype(F32) + r_ref[...].astype(F32)).astype(BF16)

    blk = (None, tr, B)
    return pl.pallas_call(
        body, name="add_rows",
        out_shape=jax.ShapeDtypeStruct(r.shape, BF16),
        grid_spec=pltpu.PrefetchScalarGridSpec(
            num_scalar_prefetch=1, grid=(4, nb),
            in_specs=[pl.BlockSpec(blk, lambda k, i, c: (k, c[0] * nb + i, 0)),
                      pl.BlockSpec(blk, lambda k, i, c: (k, i, 0))],
            out_specs=pl.BlockSpec(blk, lambda k, i, c: (k, i, 0))),
        compiler_params=_cp(("arbitrary", "arbitrary")),
    )(cidx, g, r)


def _sum_group_half(chip, cidx, t, u):
    _, nh, B = t.shape
    tr = min(256, nh)
    nb = nh // tr

    def body(k_ref, c_ref, t_ref, u_ref, o_ref):
        mine = (pl.program_id(0) // nb) == c_ref[0]

        @pl.when(mine)
        def _():
            o_ref[...] = ((t_ref[...].astype(F32) + u_ref[0].astype(F32)) + u_ref[1].astype(F32)) + u_ref[2].astype(F32)

        @pl.when(jnp.logical_not(mine))
        def _():
            o_ref[...] = jnp.zeros_like(o_ref)

    own = lambda i, c: jnp.clip(i - c[0] * nb, 0, nb - 1)
    return pl.pallas_call(
        body, name="sum_group_half",
        out_shape=jax.ShapeDtypeStruct((2 * nh, B), F32),
        grid_spec=pltpu.PrefetchScalarGridSpec(
            num_scalar_prefetch=2, grid=(2 * nb,),
            in_specs=[pl.BlockSpec((None, tr, B), lambda i, k, c: (k[0], own(i, c), 0)),
                      pl.BlockSpec((3, tr, B), lambda i, k, c: (0, own(i, c), 0))],
            out_specs=pl.BlockSpec((tr, B), lambda i, k, c: (i, 0))),
        compiler_params=_cp(("arbitrary",)),
    )(chip, cidx, t, u)


def _adam_math(w, g, m, v):
    c1 = 1.0 / (1.0 - ADAM_B1 ** ADAM_STEP)
    c2 = 1.0 / (1.0 - ADAM_B2 ** ADAM_STEP)
    nm = ADAM_B1 * m + (1.0 - ADAM_B1) * g
    nv = ADAM_B2 * v + (1.0 - ADAM_B2) * (g * g)
    return -ADAM_LR * ((nm * c1) / (jnp.sqrt(nv * c2) + ADAM_EPS) + ADAM_WD * w), nm, nv


def _adamw_layer(prev, w, m, v, sa, sb, l, part, ride=None):
    NL, A, B = w.shape
    tr = A
    while tr * B * 4 > ADAM_BLOCK_BYTES and tr % 16 == 0:
        tr //= 2
    nb = A // tr

    def body(p0, p1, p2, p3, w_ref, m_ref, v_ref, sa_ref, sb_ref, g_ref, d_ref, nm_ref, nv_ref):
        g = sa_ref[...] + sb_ref[...]
        g_ref[...] = g
        d_ref[...], nm_ref[...], nv_ref[...] = _adam_math(w_ref[...], g, m_ref[...], v_ref[...])

    lay = pl.BlockSpec((None, tr, B), lambda i: (l, i, 0))
    src = pl.BlockSpec((tr, B), lambda i: (part * nb + i, 0))
    full = jax.ShapeDtypeStruct((NL, A, B), F32)
    if prev is None:
        prev = tuple(lax.empty((NL, A, B), F32) for _ in range(4))
    outs, landed = _pcall(
        body, name="adamw_layer",
        out_shape=(full,) * 4, grid=(nb,),
        in_specs=[ANY] * 4 + [lay, lay, lay, src, src], out_specs=(lay,) * 4,
        sem=("arbitrary",), aliases={0: 0, 1: 1, 2: 2, 3: 3}, args=(*prev, w, m, v, sa, sb), ride=ride)
    return tuple(outs), landed


def _adamw(w, g, m, v):
    shape = w.shape
    cols = shape[-1]
    rows = int(np.prod(shape[:-1]))

    def body(w_ref, g_ref, m_ref, v_ref, d_ref, nm_ref, nv_ref):
        d_ref[...], nm_ref[...], nv_ref[...] = _adam_math(w_ref[...], g_ref[...], m_ref[...], v_ref[...])

    tile = pl.BlockSpec((rows, cols), lambda i: (0, 0))
    out = jax.ShapeDtypeStruct((rows, cols), F32)
    res = pl.pallas_call(
        body, name="adamw",
        out_shape=(out, out, out), grid=(1,),
        in_specs=[tile] * 4, out_specs=(tile,) * 3,
        compiler_params=_cp(("arbitrary",)),
    )(*[a.reshape(rows, cols) for a in (w, g, m, v)])
    return tuple(a.reshape(shape) for a in res)


def _tail_exchange(small, s_in, s_sq):
    def body(s_ref, a_ref, b_ref, o_ref, oa_ref, ob_ref, send_sems, recv_sems, local_sem):
        x, y, c = _place()
        me = 4 * x + 2 * y + c
        sibling = (x, y, 1 - c)
        mine = pltpu.make_async_copy(s_ref, o_ref.at[me], local_sem)
        mine.start()
        swaps = [_remote(a_ref, oa_ref, send_sems, recv_sems, 7, sibling), _remote(b_ref, ob_ref, send_sems, recv_sems, 8, sibling)]
        sends = [_remote(s_ref, o_ref.at[me], send_sems, recv_sems, r, peer) for r, peer in enumerate(_peers(x, y, c))]
        for cp in swaps + sends:
            cp.start()
        for r, peer in enumerate(_peers(x, y, c)):
            theirs = o_ref.at[4 * peer[0] + 2 * peer[1] + peer[2]]
            _remote(theirs, theirs, send_sems, recv_sems, r, peer).wait_recv()
        for cp in sends:
            cp.wait_send()
        for cp in swaps:
            cp.wait()
        mine.wait()

    return pl.pallas_call(
        body, name="tail_exchange",
        out_shape=(jax.ShapeDtypeStruct((8,) + small.shape, small.dtype),
                   jax.ShapeDtypeStruct(s_in.shape, s_in.dtype), jax.ShapeDtypeStruct(s_sq.shape, s_sq.dtype)),
        in_specs=[ANY] * 3, out_specs=(ANY,) * 3,
        scratch_shapes=[pltpu.SemaphoreType.DMA((9,)), pltpu.SemaphoreType.DMA((9,)), pltpu.SemaphoreType.DMA],
        compiler_params=pltpu.CompilerParams(has_side_effects=True),
    )(small, s_in, s_sq)


def _sum_devices(gs):
    NL = len(gs)
    _, R, D = gs[0].shape

    def body(*refs):
        o_ref = refs[NL]
        for l in range(NL):
            acc = refs[l][0]
            for k in range(1, 8):
                acc = acc + refs[l][k]
            o_ref[l] = acc

    return pl.pallas_call(
        body, name="sum_devices",
        out_shape=jax.ShapeDtypeStruct((NL, R, D), F32),
        grid=(1,),
        in_specs=[pl.BlockSpec((8, R, D), lambda i: (0, 0, 0))] * NL,
        out_specs=pl.BlockSpec((NL, R, D), lambda i: (0, 0, 0)),
        compiler_params=_cp(("arbitrary",)),
    )(*gs)


def kernel(x, pre_norm_g, w_in, w_ret_out, conv_w, conv_b, conv_ln_g, conv_ln_b, w_conv_out, w_o, post_norm_g, loss_target, m_pre_norm_g, m_w_in, m_w_ret_out, m_conv_w, m_conv_b, m_conv_ln_g, m_conv_ln_b, m_w_conv_out, m_w_o, m_post_norm_g, v_pre_norm_g, v_w_in, v_w_ret_out, v_conv_w, v_conv_b, v_conv_ln_g, v_conv_ln_b, v_w_conv_out, v_w_o, v_post_norm_g):
    NL, D, W = w_in.shape
    Cc = conv_w.shape[-1]
    T = x.shape[1]
    L = min(RET_BLOCK, T)
    tb = _tables(T, L)
    ax, ay, ac = _place()
    chip = (2 * ax + ay).astype(jnp.int32).reshape(1)
    cidx = ac.astype(jnp.int32).reshape(1)
    pre_g, cb, lg, lb, post_g = (a.reshape(NL, 1, D) for a in (pre_norm_g, conv_b, conv_ln_g, conv_ln_b, post_norm_g))

    win = [_cast_win(chip, w_in, l) for l in range(NL)]
    wsq = [_cast_wsq(chip, w_ret_out, w_conv_out, w_o, l) for l in range(NL)]
    win[0], wcw = _gather_first(win[0], _place_cw(chip, conv_w))

    saved = []
    xl = x[0]
    for l in range(NL):
        more = l + 1 < NL
        ride = _Ride()
        if more:
            _ride_gather_ici(ride, "win", win[l + 1], (0, 5, 8))
        (proj, h), got = _fwd_in(xl, pre_g, win[l], l, ride=ride)
        if more:
            win[l + 1] = got["win"]
        ride = _Ride()
        if l == 0:
            _ride_gather_ici(ride, "wsq0", wsq[0])
        (a_ret, states), got = _ret_fwd(proj, tb, L, ride=ride)
        if l == 0:
            wsq[0] = got["wsq0"]
        ride = _Ride()
        if more:
            _ride_gather_ici(ride, "win", win[l + 1], (5, 8, 8))
            _ride_gather_ici(ride, "wsq", wsq[l + 1])
        if l == 0:
            _ride_gather_pass(ride, "wsq0", wsq[0])
        (a_conv, y), got = _conv_fwd(proj, wcw, cb, lg, lb, l, ride=ride)
        if more:
            win[l + 1], wsq[l + 1] = got["win"], got["wsq"]
        if l == 0:
            wsq[0] = got["wsq0"]
        ride = _Ride()
        if more:
            _ride_gather_pass(ride, "win", win[l + 1])
            _ride_gather_pass(ride, "wsq", wsq[l + 1])
        (xn, ro, co, ym, z), got = _merge_fwd(xl, proj, a_ret, a_conv, wsq[l], post_g, l, ride=ride)
        if more:
            win[l + 1], wsq[l + 1] = got["win"], got["wsq"]
        saved.append((xl, proj, h, a_ret, states, a_conv, y, ro, co, ym, z))
        xl = xn
    dx, lsum = _loss_fwd_bwd(xl, loss_target[0])
    loss = lax.psum(jnp.sum(lsum), ("x", "y", "c"))

    gin, gsq, uin, usq = [None] * NL, [None] * NL, [None] * NL, [None] * NL
    s_in, s_sq, o_in, o_sq = [None] * NL, [None] * NL, [None] * NL, [None] * NL
    small, gs = [None] * NL, [None] * NL
    for l in reversed(range(NL)):
        xin, proj, h, a_ret, states, a_conv, y, ro, co, ym, z = saved[l]
        (dproj, da_ret, da_conv, gsq[l], dpost), _ = _merge_bwd(dx, proj, a_ret, a_conv, ro, co, ym, z, wsq[l], post_g, l)
        ride = _Ride()
        _ride_exchange(ride, "gsq", gsq[l], "usq", None, (0, 1, 2))
        if l + 1 < NL:
            _ride_exchange(ride, "gin", gin[l + 1], "uin", uin[l + 1], (0, 1), (1, 2, 2))
        (dproj, sg), got = _conv_bwd(dproj, da_conv, y, proj, wcw, lg, lb, l, ride=ride)
        usq[l] = got["usq"]
        if l + 1 < NL:
            uin[l + 1] = got["uin"]
        ride = _Ride()
        if l + 1 < NL:
            _ride_exchange(ride, "gin", gin[l + 1], "uin", uin[l + 1], (2,))
            _ride_gather_all(ride, "small", small[l + 1], "gs")
        (dproj,), got = _ret_bwd(dproj, da_ret, proj, states, tb, L, ride=ride)
        if l + 1 < NL:
            uin[l + 1], gs[l + 1] = got["uin"], got["gs"]
        gin[l] = _win_grad(h, dproj, W)
        ride = _Ride()
        if l > 0:
            _ride_exchange(ride, "gin", gin[l], "uin", None, (0, 1), (0, 1, 2))
        else:
            gin[0] = _add_rows(cidx, gin[0], _swap_rows(gin[0]))
            _ride_exchange(ride, "gin", gin[0], "uin", None, (0, 1, 2))
        if l + 1 < NL:
            s_in[l + 1] = _sum_group(chip, gin[l + 1], uin[l + 1])
            s_sq[l + 1] = _sum_group(chip, gsq[l + 1], usq[l + 1])
            _ride_swap(ride, "s_in", s_in[l + 1], "o_in")
            _ride_swap(ride, "s_sq", s_sq[l + 1], "o_sq")
        (dx, dpre), got = _in_bwd(dx, dproj, xin, pre_g, win[l], l, ride=ride)
        uin[l] = got["uin"]
        if l + 1 < NL:
            o_in[l + 1], o_sq[l + 1] = got["o_in"], got["o_sq"]
        small[l] = jnp.concatenate([sg, dpre, dpost], axis=0)
    grad_x = dx

    big = {"w_in": None, "w_ret_out": None, "w_conv_out": None, "w_o": None}
    wts = {"w_in": (w_in, m_w_in, v_w_in), "w_ret_out": (w_ret_out, m_w_ret_out, v_w_ret_out),
           "w_conv_out": (w_conv_out, m_w_conv_out, v_w_conv_out), "w_o": (w_o, m_w_o, v_w_o)}
    sq_names = ("w_ret_out", "w_conv_out", "w_o")

    def adam_in(l, ride=None):
        big["w_in"], got = _adamw_layer(big["w_in"], *wts["w_in"], s_in[l], o_in[l], l, 0, ride=ride)
        return got

    def adam_sq(l, part, ride=None):
        n = sq_names[part]
        big[n], got = _adamw_layer(big[n], *wts[n], s_sq[l], o_sq[l], l, part, ride=ride)
        return got

    s_in[0] = _sum_group_half(chip, cidx, gin[0], uin[0])
    s_sq[0] = _sum_group(chip, gsq[0], usq[0])
    gs[0], o_in[0], o_sq[0] = _tail_exchange(small[0], s_in[0], s_sq[0])
    for l in reversed(range(NL)):
        adam_in(l)
        for part in range(3):
            adam_sq(l, part)

    gsm = _sum_devices(gs)
    grads = {
        "pre_norm_g": gsm[:, ROW_PRE], "conv_w": lax.dynamic_slice_in_dim(gsm[:, 0:CONV_K], chip[0] * Cc, Cc, axis=2),
        "conv_b": gsm[:, ROW_CB], "conv_ln_g": gsm[:, ROW_LG], "conv_ln_b": gsm[:, ROW_LB], "post_norm_g": gsm[:, ROW_POST],
    }
    weights = dict(pre_norm_g=pre_norm_g, conv_w=conv_w, conv_b=conv_b, conv_ln_g=conv_ln_g, conv_ln_b=conv_ln_b,
                   post_norm_g=post_norm_g)
    m1 = dict(pre_norm_g=m_pre_norm_g, conv_w=m_conv_w, conv_b=m_conv_b, conv_ln_g=m_conv_ln_g, conv_ln_b=m_conv_ln_b,
              post_norm_g=m_post_norm_g)
    m2 = dict(pre_norm_g=v_pre_norm_g, conv_w=v_conv_w, conv_b=v_conv_b, conv_ln_g=v_conv_ln_g, conv_ln_b=v_conv_ln_b,
              post_norm_g=v_post_norm_g)
    res = {n: (grads[n],) + _adamw(weights[n], grads[n], m1[n], m2[n]) for n in grads}
    res.update(big)
    order = ["pre_norm_g", "w_in", "w_ret_out", "conv_w", "conv_b", "conv_ln_g", "conv_ln_b", "w_conv_out", "w_o", "post_norm_g"]
    return (loss, grad_x[None], *[res[n][0] for n in order], *[res[n][1] for n in order],
            *[res[n][2] for n in order], *[res[n][3] for n in order])
```

```python
import numpy as np
import jax
import jax.numpy as jnp
from jax import lax
from jax.experimental import pallas as pl
from jax.experimental.pallas import tpu as pltpu

F32 = jnp.float32
BF16 = jnp.bfloat16

HEADS = 8
DK = 64
DV = 128
CONV_K = 31
CHUNK = 64
ROPE_BASE = 10000.0
EPS = 1e-6
HALO = 32
CONV_RB = 32
CONV_LANES = 512
CONV_TILE = 256
RET_BLOCK = 512

ADAM_LR = 0.001
ADAM_B1 = 0.9
ADAM_B2 = 0.999
ADAM_EPS = 1e-08
ADAM_WD = 0.01
ADAM_STEP = 10
ADAM_BLOCK_BYTES = 2 * 1024 * 1024

VMEM_LIMIT = 56 * 1024 * 1024
MESH_T = pl.DeviceIdType.MESH
ANY = pl.BlockSpec(memory_space=pl.ANY)

ROW_CB, ROW_LG, ROW_LB = 32, 33, 34
ROW_PRE, ROW_POST = 40, 48


def _cp(sem=None, **kw):
    return pltpu.CompilerParams(dimension_semantics=sem, vmem_limit_bytes=VMEM_LIMIT, **kw)


def _dot(a, b):
    return jnp.dot(a, b, preferred_element_type=F32)


def _dot_nt(a, b):
    return lax.dot_general(a, b, (((1,), (1,)), ((), ())), preferred_element_type=F32)


def _dot_tn(a, b):
    return lax.dot_general(a, b, (((0,), (0,)), ((), ())), preferred_element_type=F32)


def _sigmoid(x):
    return jax.nn.sigmoid(x)


def _silu(x):
    return x * _sigmoid(x)


def _rms(x, g):
    return x * lax.rsqrt(jnp.mean(x * x, axis=-1, keepdims=True) + EPS) * g


def _gn_gate(o, g):
    mu = jnp.mean(o, axis=-1, keepdims=True)
    d = o - mu
    var = jnp.mean(d * d, axis=-1, keepdims=True)
    return d * lax.rsqrt(var + EPS) * _silu(g)


def _ln_gate(y, gc, lg, lb):
    mu = jnp.mean(y, axis=-1, keepdims=True)
    d = y - mu
    var = jnp.mean(d * d, axis=-1, keepdims=True)
    return _silu(d * lax.rsqrt(var + EPS) * lg + lb) * _silu(gc)


def _tables(T, L):
    lane = np.arange(128)
    d = lane % DK
    half = DK // 2
    inv = (ROPE_BASE ** (-(np.arange(half, dtype=np.float32)) / half)).astype(np.float32)
    ang = (np.arange(T, dtype=np.float32)[:, None] * inv[None, :]).astype(np.float64)
    angl = ang[:, d % half]
    cos = np.cos(angl)
    sin = np.sin(angl)
    lo = (d < half)[None, :]
    rope = np.stack([cos, np.where(lo, -sin, 0.0), np.where(lo, 0.0, sin)]).astype(np.float32)

    hh = np.arange(HEADS, dtype=np.float64)
    log_g = np.log1p(-np.exp2(-5.0 - hh))
    n = np.arange(L, dtype=np.float64)
    cn = np.arange(L) // CHUNK
    allowed = (cn[None, :] <= cn[:, None])
    dist = np.abs(n[:, None] - n[None, :])
    mask = np.exp(log_g[:, None, None] * dist[None]) * allowed[None]
    mq = ((lane[None, :] // DK) == (np.arange(HEADS)[:, None] % 2)).astype(np.float64)
    qd = np.exp(log_g[:, None] * n[None, :])
    kd = np.exp(log_g[:, None] * (L - n[None, :]))
    qdm = qd[:, :, None] * mq[:, None, :] * (DK ** -0.5)
    kdm = kd[:, :, None] * mq[:, None, :]
    mqs = np.broadcast_to((mq * (DK ** -0.5))[:, None, :], (HEADS, 8, 128))
    cd = np.broadcast_to(np.exp(log_g * L)[:, None, None], (HEADS, 8, 128))
    f = lambda a: jnp.asarray(np.ascontiguousarray(a), dtype=F32)
    return dict(rope=f(rope), mask=f(mask), qdm=f(qdm), kdm=f(kdm), mqs=f(mqs), cd=f(cd))


def _rot(b, c, sl, sh):
    return b * c + pltpu.roll(b, 96, axis=1) * sl + pltpu.roll(b, 32, axis=1) * sh


def _rot_t(d, c, sl, sh):
    return d * c + pltpu.roll(d * sl, 32, axis=1) + pltpu.roll(d * sh, 96, axis=1)


def _place():
    return lax.axis_index("x"), lax.axis_index("y"), lax.axis_index("c")


def _other_chips(x, y):
    return [(1 - x, y), (x, 1 - y), (1 - x, 1 - y)]


def _remote(src, dst, send_sems, recv_sems, k, to):
    return pltpu.make_async_remote_copy(src_ref=src, dst_ref=dst, send_sem=send_sems.at[k], recv_sem=recv_sems.at[k],
                                        device_id=to, device_id_type=MESH_T)


class _Ride:
    def __init__(self):
        self.arrays, self.kinds, self.names = [], [], []
        self.fresh = []
        self.ops = []

    def read(self, name, a):
        self.names.append(name)
        self.arrays.append(a)
        self.kinds.append("in")

    def inout(self, name, a):
        self.names.append(name)
        self.arrays.append(a)
        self.kinds.append("inout")

    def land(self, name, shape, dtype):
        self.fresh.append((name, jax.ShapeDtypeStruct(shape, dtype)))

    def op(self, n_sems, start, finish):
        self.ops.append((n_sems, start, finish))


def _pcall(body, *, name, grid, in_specs, out_specs, out_shape, args, scratch_shapes=(), sem, aliases=None, ride=None):
    if ride is None or not ride.ops:
        outs = pl.pallas_call(body, name=name, grid=grid, in_specs=list(in_specs), out_specs=tuple(out_specs),
                              out_shape=tuple(out_shape), scratch_shapes=list(scratch_shapes),
                              input_output_aliases=dict(aliases or {}), compiler_params=_cp(sem))(*args)
        return outs, {}

    ni, no, nr = len(args), len(out_shape), len(ride.arrays)
    inout = [i for i, k in enumerate(ride.kinds) if k == "inout"]
    r_out_shapes = [jax.ShapeDtypeStruct(ride.arrays[i].shape, ride.arrays[i].dtype) for i in inout] + [s for _, s in ride.fresh]
    r_out_names = [ride.names[i] for i in inout] + [n for n, _ in ride.fresh]
    nro = len(r_out_shapes)
    n_sems = sum(n for n, _, _ in ride.ops)
    n_scr = len(scratch_shapes)
    nd = len(grid)

    def wrapped(*refs):
        ins, rin = refs[:ni], refs[ni:ni + nr]
        outs, rout = refs[ni + nr:ni + nr + no], refs[ni + nr + no:ni + nr + no + nro]
        scr = refs[ni + nr + no + nro:ni + nr + no + nro + n_scr]
        send_sems, recv_sems = refs[-2], refs[-1]
        view = {nm: r for nm, r, k in zip(ride.names, rin, ride.kinds) if k == "in"}
        view.update(dict(zip(r_out_names, rout)))
        first = pl.program_id(0) == 0
        last = pl.program_id(0) == grid[0] - 1
        for d in range(1, nd):
            first = first & (pl.program_id(d) == 0)
            last = last & (pl.program_id(d) == grid[d] - 1)

        @pl.when(first)
        def _():
            base = 0
            for n, start, _ in ride.ops:
                start(view, send_sems, recv_sems, base)
                base += n

        body(*ins, *outs, *scr)

        @pl.when(last)
        def _():
            base = 0
            for n, _, finish in ride.ops:
                finish(view, send_sems, recv_sems, base)
                base += n

    res = pl.pallas_call(
        wrapped, name=name, grid=grid,
        in_specs=list(in_specs) + [ANY] * nr, out_specs=tuple(out_specs) + (ANY,) * nro,
        out_shape=tuple(out_shape) + tuple(r_out_shapes),
        scratch_shapes=list(scratch_shapes) + [pltpu.SemaphoreType.DMA((n_sems,)), pltpu.SemaphoreType.DMA((n_sems,))],
        input_output_aliases={**dict(aliases or {}), **{ni + i: no + j for j, i in enumerate(inout)}},
        compiler_params=_cp(sem),
    )(*args, *ride.arrays)
    return res[:no], dict(zip(r_out_names, res[no:]))


def _half(ref, chip_idx, cc, part=(0, 1, 1)):
    n = ref.shape[1] // 2
    lo, hi, k = part
    return ref.at[chip_idx, pl.ds(cc * n + lo * n // k, (hi - lo) * n // k)]


def _ride_gather_ici(ride, name, a, part=(0, 1, 1)):
    ride.inout(name, a)

    def start(view, ss, rs, b):
        x, y, c = _place()
        mine = _half(view[name], 2 * x + y, c, part)
        for j, (cx, cy) in enumerate(_other_chips(x, y)):
            _remote(mine, mine, ss, rs, b + j, (cx, cy, c)).start()

    def finish(view, ss, rs, b):
        x, y, c = _place()
        mine = _half(view[name], 2 * x + y, c, part)
        for j, (cx, cy) in enumerate(_other_chips(x, y)):
            theirs = _half(view[name], 2 * cx + cy, c, part)
            _remote(theirs, theirs, ss, rs, b + j, (cx, cy, c)).wait_recv()
        for j, (cx, cy) in enumerate(_other_chips(x, y)):
            _remote(mine, mine, ss, rs, b + j, (cx, cy, c)).wait_send()

    ride.op(3, start, finish)


def _ride_gather_direct(ride, name, a):
    ride.inout(name, a)

    def start(view, ss, rs, b):
        x, y, c = _place()
        mine = _half(view[name], 2 * x + y, c)
        for j, (cx, cy) in enumerate(_other_chips(x, y)):
            _remote(mine, mine, ss, rs, b + 2 * j, (cx, cy, c)).start()
            _remote(mine, mine, ss, rs, b + 2 * j + 1, (cx, cy, 1 - c)).start()

    def finish(view, ss, rs, b):
        x, y, c = _place()
        mine = _half(view[name], 2 * x + y, c)
        for j, (cx, cy) in enumerate(_other_chips(x, y)):
            same = _half(view[name], 2 * cx + cy, c)
            other = _half(view[name], 2 * cx + cy, 1 - c)
            _remote(same, same, ss, rs, b + 2 * j, (cx, cy, c)).wait_recv()
            _remote(other, other, ss, rs, b + 2 * j + 1, (cx, cy, 1 - c)).wait_recv()
        for j, (cx, cy) in enumerate(_other_chips(x, y)):
            _remote(mine, mine, ss, rs, b + 2 * j, (cx, cy, c)).wait_send()
            _remote(mine, mine, ss, rs, b + 2 * j + 1, (cx, cy, 1 - c)).wait_send()

    ride.op(6, start, finish)


def _ride_gather_pass(ride, name, a):
    ride.inout(name, a)

    def start(view, ss, rs, b):
        x, y, c = _place()
        for j, (cx, cy) in enumerate(_other_chips(x, y)):
            blk = _half(view[name], 2 * cx + cy, c)
            _remote(blk, blk, ss, rs, b + j, (x, y, 1 - c)).start()

    def finish(view, ss, rs, b):
        x, y, c = _place()
        for j, (cx, cy) in enumerate(_other_chips(x, y)):
            theirs = _half(view[name], 2 * cx + cy, 1 - c)
            _remote(theirs, theirs, ss, rs, b + j, (x, y, 1 - c)).wait_recv()
        for j, (cx, cy) in enumerate(_other_chips(x, y)):
            blk = _half(view[name], 2 * cx + cy, c)
            _remote(blk, blk, ss, rs, b + j, (x, y, 1 - c)).wait_send()

    ride.op(3, start, finish)


def _ride_exchange(ride, src_name, src, dst_name, dst, rels, part=(0, 1, 1)):
    ride.read(src_name, src)
    if dst is None:
        ride.land(dst_name, (3,) + src.shape[1:], src.dtype)
    else:
        ride.inout(dst_name, dst)
    lo, hi, k = part
    rows = pl.ds(lo * src.shape[1] // k, (hi - lo) * src.shape[1] // k)

    def copy(view, ss, rs, sem, j, x, y, c):
        cx, cy = _other_chips(x, y)[j]
        return _remote(view[src_name].at[2 * cx + cy, rows], view[dst_name].at[j, rows], ss, rs, sem, (cx, cy, c))

    def start(view, ss, rs, b):
        x, y, c = _place()
        for i, j in enumerate(rels):
            copy(view, ss, rs, b + i, j, x, y, c).start()

    def finish(view, ss, rs, b):
        x, y, c = _place()
        for i, j in enumerate(rels):
            copy(view, ss, rs, b + i, j, x, y, c).wait()

    ride.op(len(rels), start, finish)


def _ride_swap(ride, src_name, src, dst_name):
    ride.read(src_name, src)
    ride.land(dst_name, src.shape, src.dtype)

    def start(view, ss, rs, b):
        x, y, c = _place()
        _remote(view[src_name], view[dst_name], ss, rs, b, (x, y, 1 - c)).start()

    def finish(view, ss, rs, b):
        x, y, c = _place()
        _remote(view[src_name], view[dst_name], ss, rs, b, (x, y, 1 - c)).wait()

    ride.op(1, start, finish)


def _peers(x, y, c):
    flip = lambda v, b: 1 - v if b else v
    return [(flip(x, r & 4), flip(y, r & 2), flip(c, r & 1)) for r in range(1, 8)]


def _ride_gather_all(ride, src_name, src, dst_name):
    ride.read(src_name, src)
    ride.land(dst_name, (8,) + src.shape, src.dtype)

    def start(view, ss, rs, b):
        x, y, c = _place()
        me = 4 * x + 2 * y + c
        pltpu.make_async_copy(view[src_name], view[dst_name].at[me], ss.at[b + 7]).start()
        for r, peer in enumerate(_peers(x, y, c)):
            _remote(view[src_name], view[dst_name].at[me], ss, rs, b + r, peer).start()

    def finish(view, ss, rs, b):
        x, y, c = _place()
        me = 4 * x + 2 * y + c
        for r, peer in enumerate(_peers(x, y, c)):
            theirs = view[dst_name].at[4 * peer[0] + 2 * peer[1] + peer[2]]
            _remote(theirs, theirs, ss, rs, b + r, peer).wait_recv()
        for r, peer in enumerate(_peers(x, y, c)):
            _remote(view[src_name], view[dst_name].at[me], ss, rs, b + r, peer).wait_send()
        pltpu.make_async_copy(view[src_name], view[dst_name].at[me], ss.at[b + 7]).wait()

    ride.op(8, start, finish)


def _cast_win(chip, w_in, l):
    _, D, W = w_in.shape
    tr = min(256, D)

    def body(chip_ref, w_ref, o_ref):
        o_ref[...] = w_ref[...].astype(BF16)

    return pl.pallas_call(
        body, name="cast_win",
        out_shape=jax.ShapeDtypeStruct((4, D, W), BF16),
        grid_spec=pltpu.PrefetchScalarGridSpec(
            num_scalar_prefetch=1, grid=(D // tr,),
            in_specs=[pl.BlockSpec((None, tr, W), lambda r, c: (l, r, 0))],
            out_specs=pl.BlockSpec((None, tr, W), lambda r, c: (c[0], r, 0))),
        compiler_params=_cp(("arbitrary",)),
    )(chip, w_in)


def _cast_wsq(chip, w_ro, w_co, w_o, l):
    _, R, D = w_ro.shape

    def body(chip_ref, a_ref, b_ref, c_ref, o_ref):
        o_ref[0:R, :] = a_ref[...].astype(BF16)
        o_ref[R:2 * R, :] = b_ref[...].astype(BF16)
        o_ref[2 * R:3 * R, :] = c_ref[...].astype(BF16)

    spec = pl.BlockSpec((None, R, D), lambda i, c: (l, 0, 0))
    return pl.pallas_call(
        body, name="cast_wsq",
        out_shape=jax.ShapeDtypeStruct((4, 3 * R, D), BF16),
        grid_spec=pltpu.PrefetchScalarGridSpec(
            num_scalar_prefetch=1, grid=(1,),
            in_specs=[spec, spec, spec],
            out_specs=pl.BlockSpec((None, 3 * R, D), lambda i, c: (c[0], 0, 0))),
        compiler_params=_cp(("arbitrary",)),
    )(chip, w_ro, w_co, w_o)


def _place_cw(chip, conv_w):
    NL, K, Cc = conv_w.shape

    def body(chip_ref, w_ref, o_ref):
        o_ref[...] = w_ref[...]

    return pl.pallas_call(
        body, name="place_cw",
        out_shape=jax.ShapeDtypeStruct((4, NL, K, Cc), F32),
        grid_spec=pltpu.PrefetchScalarGridSpec(
            num_scalar_prefetch=1, grid=(1,),
            in_specs=[pl.BlockSpec((NL, K, Cc), lambda i, c: (0, 0, 0))],
            out_specs=pl.BlockSpec((None, NL, K, Cc), lambda i, c: (c[0], 0, 0, 0))),
        compiler_params=_cp(("arbitrary",)),
    )(chip, conv_w)


def _gather_first(win0, wcw):
    n_arr = 2

    def body(a0, a1, o0, o1, send_sems, recv_sems):
        x, y, c = _place()
        sibling = (x, y, 1 - c)
        chips = _other_chips(x, y)
        outs = (o0, o1)

        def copy(k, a, cx, cy, cc, to):
            blk = _half(outs[a], 2 * cx + cy, cc)
            return _remote(blk, blk, send_sems, recv_sems, k, to)

        first = [copy(3 * a + j, a, x, y, c, (*chip, c)) for a in range(n_arr) for j, chip in enumerate(chips)]
        for cp in first:
            cp.start()
        passed = [copy(3 * n_arr + 3 * a + j, a, *chip, c, sibling) for a in range(n_arr) for j, chip in enumerate(chips)]
        for a in range(n_arr):
            for j, chip in enumerate(chips):
                copy(3 * a + j, a, *chip, c, sibling).wait_recv()
                passed[3 * a + j].start()
        for a in range(n_arr):
            for j, chip in enumerate(chips):
                copy(3 * n_arr + 3 * a + j, a, *chip, 1 - c, sibling).wait_recv()
        for cp in first + passed:
            cp.wait_send()

    ins = (win0, wcw)
    return pl.pallas_call(
        body, name="gather_first",
        out_shape=tuple(jax.ShapeDtypeStruct(a.shape, a.dtype) for a in ins),
        in_specs=[ANY] * n_arr, out_specs=(ANY,) * n_arr,
        scratch_shapes=[pltpu.SemaphoreType.DMA((6 * n_arr,)), pltpu.SemaphoreType.DMA((6 * n_arr,))],
        input_output_aliases={0: 0, 1: 1},
        compiler_params=pltpu.CompilerParams(has_side_effects=True),
    )(*ins)


def _fwd_in(x, pre_g, win, l, ride=None):
    T, D = x.shape
    W = win.shape[-1]
    tm = min(512, T)

    def body(x_ref, g_ref, w_ref, p_ref, h_ref):
        hb = _rms(x_ref[...], g_ref[...]).astype(BF16)
        h_ref[...] = hb
        for j in range(4):
            p_ref[:, j * W:(j + 1) * W] = _dot(hb, w_ref[j]).astype(BF16)

    return _pcall(
        body, name="fwd_in",
        out_shape=(jax.ShapeDtypeStruct((T, 4 * W), BF16), jax.ShapeDtypeStruct((T, D), BF16)),
        grid=(T // tm,),
        in_specs=[pl.BlockSpec((tm, D), lambda i: (i, 0)),
                  pl.BlockSpec((None, 1, D), lambda i: (l, 0, 0)),
                  pl.BlockSpec((4, D, W), lambda i: (0, 0, 0), pipeline_mode=pl.Buffered(1))],
        out_specs=(pl.BlockSpec((tm, 4 * W), lambda i: (i, 0)),
                   pl.BlockSpec((tm, D), lambda i: (i, 0))),
        sem=("arbitrary",), args=(x, pre_g, win), ride=ride)


def _ret_specs(T, L):
    rope = pl.BlockSpec((3, L, 128), lambda s: (0, s, 0))
    mask = pl.BlockSpec((HEADS, L, L), lambda s: (0, 0, 0), pipeline_mode=pl.Buffered(1))
    qdm = pl.BlockSpec((HEADS, L, 128), lambda s: (0, 0, 0), pipeline_mode=pl.Buffered(1))
    small = pl.BlockSpec((HEADS, 8, 128), lambda s: (0, 0, 0))
    return [rope, mask, qdm, qdm, small, small]


QK = HEADS * DK
VW = HEADS * DV
PW = 2 * QK + 2 * VW


def _zero_at_start(ref):
    @pl.when(pl.program_id(0) == 0)
    def _():
        ref[...] = jnp.zeros_like(ref)


def _ret_fwd_part(p_ref, rope_ref, m_ref, qdm_ref, kdm_ref, mqs_ref, cd_ref, a_ref, st_ref, state):
    c, sl, sh = rope_ref[0], rope_ref[1], rope_ref[2]
    for j in range(HEADS // 2):
        rq = _rot(p_ref[:, 128 * j:128 * (j + 1)].astype(F32), c, sl, sh)
        rk = _rot(p_ref[:, QK + 128 * j:QK + 128 * (j + 1)].astype(F32), c, sl, sh)
        rkb = rk.astype(BF16)
        for e in range(2):
            h = 2 * j + e
            v = p_ref[:, 2 * QK + DV * h:2 * QK + DV * (h + 1)]
            g = p_ref[:, 2 * QK + VW + DV * h:2 * QK + VW + DV * (h + 1)].astype(F32)
            a = (rq * mqs_ref[h, 0:1, :]).astype(BF16)
            p = (_dot_nt(a, rkb) * m_ref[h]).astype(BF16)
            st = state[h]
            st_ref[h] = st
            o = _dot(p, v) + _dot((rq * qdm_ref[h]).astype(BF16), st.astype(BF16))
            state[h] = st * cd_ref[h, 0:1, :] + _dot_tn((rk * kdm_ref[h]).astype(BF16), v)
            a_ref[:, DV * h:DV * (h + 1)] = _gn_gate(o, g).astype(BF16)


def _shift_copies(src, sh):
    rows = sh.shape[1]
    for b in range(1, 8):
        sh[b - 1, :, :] = src[pl.ds(b, rows), :]


def _window(src, sh, r0, const, rows, lanes):
    b = const % 8
    at = pl.ds(pl.multiple_of(r0 + (const - b), 8), rows)
    w = src[at, lanes] if b == 0 else sh[b - 1, at, lanes]
    return w.reshape(rows // 8, 8, w.shape[-1])


def _conv_taps(wbuf, src, sh, r0, const, rows, lanes):
    groups = rows // 8
    accs = [None] * groups
    for k in range(CONV_K):
        w8 = wbuf[pl.ds(8 * k, 8), lanes]
        win = _window(src, sh, r0, const + k, rows, lanes)
        for g in range(groups):
            term = w8 * win[g]
            accs[g] = term if k == 0 else accs[g] + term
    return jnp.concatenate(accs, axis=0)


def _lane_parts(C):
    return [pl.ds(j * CONV_LANES, CONV_LANES) for j in range(C // CONV_LANES)]


def _load_conv_w(cw_ref, wbuf, flip):
    for k in range(CONV_K):
        row = jnp.concatenate([cw_ref[c, pl.ds(k, 1), :] for c in range(4)], axis=-1)
        kk = CONV_K - 1 - k if flip else k
        wbuf[pl.ds(8 * kk, 8), :] = jnp.broadcast_to(row, (8, row.shape[-1]))


def _conv_fwd_part(p_ref, cw_ref, cb_ref, lg_ref, lb_ref, a_ref, y_ref, ubuf, wbuf, ush):
    tc, C = y_ref.shape
    off = HALO - (CONV_K - 1)
    i = pl.program_id(0)

    @pl.when(i == 0)
    def _():
        ubuf[0:HALO, :] = jnp.zeros((HALO, C), F32)
        _load_conv_w(cw_ref, wbuf, False)

    @pl.when(i > 0)
    def _():
        ubuf[0:HALO, :] = ubuf[tc:tc + HALO, :]

    ga = p_ref[:, 0:C].astype(F32)
    gb = p_ref[:, C:2 * C].astype(F32)
    ubuf[HALO:HALO + tc, :] = ga * _sigmoid(gb)
    _shift_copies(ubuf, ush)

    def rows_block(r, carry):
        r0 = pl.multiple_of(r * CONV_RB, CONV_RB)
        for lanes in _lane_parts(C):
            y_ref[pl.ds(r0, CONV_RB), lanes] = _conv_taps(wbuf, ubuf, ush, r0, off, CONV_RB, lanes) + cb_ref[:, lanes]
        return carry

    lax.fori_loop(0, tc // CONV_RB, rows_block, 0)
    gc = p_ref[:, 2 * C:3 * C].astype(F32)
    a_ref[...] = _ln_gate(y_ref[...], gc, lg_ref[...], lb_ref[...]).astype(BF16)


def _ret_fwd(proj, tb, L, ride=None):
    T = proj.shape[0]
    nS = T // L

    def body(p_ref, rope_ref, m_ref, qdm_ref, kdm_ref, mqs_ref, cd_ref, ar_ref, st_ref, state):
        _zero_at_start(state)
        _ret_fwd_part(p_ref, rope_ref, m_ref, qdm_ref, kdm_ref, mqs_ref, cd_ref, ar_ref, st_ref, state)

    return _pcall(
        body, name="ret_fwd",
        out_shape=(jax.ShapeDtypeStruct((T, VW), BF16), jax.ShapeDtypeStruct((nS, HEADS, 128, DV), F32)),
        grid=(nS,),
        in_specs=[pl.BlockSpec((L, PW), lambda s: (s, 0))] + _ret_specs(T, L),
        out_specs=(pl.BlockSpec((L, VW), lambda s: (s, 0)), pl.BlockSpec((None, HEADS, 128, DV), lambda s: (s, 0, 0, 0))),
        scratch_shapes=[pltpu.VMEM((HEADS, 128, DV), F32)],
        sem=("arbitrary",), ride=ride,
        args=(proj, tb["rope"], tb["mask"], tb["qdm"], tb["kdm"], tb["mqs"], tb["cd"]))


def _conv_fwd(proj, wcw, conv_b, ln_g, ln_b, l, ride=None):
    T = proj.shape[0]
    C = conv_b.shape[-1]
    Cc = wcw.shape[-1]
    tc = min(CONV_TILE, T)
    assert PW == 3 * C

    def body(p_ref, cw_ref, cb_ref, lg_ref, lb_ref, ac_ref, y_ref, ubuf, wbuf, ush):
        _conv_fwd_part(p_ref, cw_ref, cb_ref, lg_ref, lb_ref, ac_ref, y_ref, ubuf, wbuf, ush)

    vec = pl.BlockSpec((None, 1, C), lambda i: (l, 0, 0))
    tile = pl.BlockSpec((tc, C), lambda i: (i, 0))
    return _pcall(
        body, name="conv_fwd",
        out_shape=(jax.ShapeDtypeStruct((T, C), BF16), jax.ShapeDtypeStruct((T, C), F32)),
        grid=(T // tc,),
        in_specs=[pl.BlockSpec((tc, 3 * C), lambda i: (i, 1)),
                  pl.BlockSpec((4, None, CONV_K, Cc), lambda i: (0, l, 0, 0)), vec, vec, vec],
        out_specs=(tile, tile),
        scratch_shapes=[pltpu.VMEM((HALO + tc, C), F32), pltpu.VMEM((8 * CONV_K, C), F32),
                        pltpu.VMEM((7, HALO + tc - 8, C), F32)],
        sem=("arbitrary",), ride=ride, args=(proj, wcw, conv_b, ln_g, ln_b))


def _merge_fwd(x, proj, a_ret, a_conv, wsq, post_g, l, ride=None):
    T, D = x.shape
    R = wsq.shape[1] // 3
    tm = min(512, T)

    def body(x_ref, p_ref, ar_ref, ac_ref, wro_ref, wco_ref, wo_ref, g_ref, xn_ref, ro_ref, co_ref, ym_ref, z_ref):
        ro = _dot(ar_ref[...], wro_ref[...].reshape(4 * R, D))
        co = _dot(ac_ref[...], wco_ref[...].reshape(4 * R, D))
        ym = (_sigmoid(p_ref[:, 0:D].astype(F32)) * ro + _sigmoid(p_ref[:, D:2 * D].astype(F32)) * co).astype(BF16)
        z = _dot(ym, wo_ref[...].reshape(4 * R, D))
        ro_ref[...] = ro.astype(BF16)
        co_ref[...] = co.astype(BF16)
        ym_ref[...] = ym
        z_ref[...] = z.astype(BF16)
        xn_ref[...] = x_ref[...] + _rms(z, g_ref[...])

    tile = pl.BlockSpec((tm, D), lambda i: (i, 0))
    wspec = lambda m: pl.BlockSpec((4, R, D), lambda i: (0, m, 0))
    act = jax.ShapeDtypeStruct((T, D), BF16)
    return _pcall(
        body, name="merge_fwd",
        out_shape=(jax.ShapeDtypeStruct((T, D), F32), act, act, act, act),
        grid=(T // tm,),
        in_specs=[tile, pl.BlockSpec((tm, 2 * D), lambda i: (i, 3)), tile, tile,
                  wspec(0), wspec(1), wspec(2), pl.BlockSpec((None, 1, D), lambda i: (l, 0, 0))],
        out_specs=(tile, tile, tile, tile, tile),
        sem=("arbitrary",), ride=ride, args=(x, proj, a_ret, a_conv, wsq, wsq, wsq, post_g))


def _loss_fwd_bwd(y, target):
    T, D = y.shape
    tm = min(512, T)

    def body(y_ref, t_ref, dy_ref, ls_ref):
        @pl.when(pl.program_id(0) == 0)
        def _():
            ls_ref[...] = jnp.zeros_like(ls_ref)

        e = y_ref[...] - t_ref[...]
        dy_ref[...] = e * (1.0 / D)
        ls_ref[...] += jnp.sum((e * e).reshape(tm // 8, 8, D), axis=0) * (0.5 / D)

    tile = pl.BlockSpec((tm, D), lambda i: (i, 0))
    return pl.pallas_call(
        body, name="loss",
        out_shape=(jax.ShapeDtypeStruct((T, D), F32), jax.ShapeDtypeStruct((8, D), F32)),
        grid=(T // tm,),
        in_specs=[tile, tile],
        out_specs=(tile, pl.BlockSpec((8, D), lambda i: (0, 0))),
        compiler_params=_cp(("arbitrary",)),
    )(y, target)


def _merge_bwd(dxn, proj, a_ret, a_conv, ro, co, ym, z, wsq, post_g, l, ride=None):
    T, D = dxn.shape
    R = wsq.shape[1] // 3
    tm = min(512, T)
    n = T // tm

    def body(dx_ref, p_ref, ar_ref, ac_ref, ro_ref, co_ref, ym_ref, z_ref, wro_ref, wco_ref, wo_ref, g_ref,
             dp_ref, dar_ref, dac_ref, gsq_ref, dg_ref, acc, stage):
        i = pl.program_id(0)

        @pl.when(i == 0)
        def _():
            acc[...] = jnp.zeros_like(acc)
            dg_ref[...] = jnp.zeros_like(dg_ref)

        _, vjp = jax.vjp(_rms, z_ref[...].astype(F32), g_ref[...])
        dz, dg = vjp(dx_ref[...])
        dg_ref[0:1, :] += dg
        dzb = dz.astype(BF16)
        dym = _dot_nt(dzb, wo_ref[...].reshape(4 * R, D))
        acc[2] += _dot_tn(ym_ref[...], dzb)
        sr = _sigmoid(p_ref[:, 0:D].astype(F32))
        sc = _sigmoid(p_ref[:, D:2 * D].astype(F32))
        rov = ro_ref[...].astype(F32)
        cov = co_ref[...].astype(F32)
        dp_ref[:, 0:D] = (dym * rov * sr * (1.0 - sr)).astype(BF16)
        dp_ref[:, D:2 * D] = (dym * cov * sc * (1.0 - sc)).astype(BF16)
        dro = (dym * sr).astype(BF16)
        dco = (dym * sc).astype(BF16)
        dar_ref[...] = _dot_nt(dro, wro_ref[...].reshape(4 * R, D)).astype(BF16)
        dac_ref[...] = _dot_nt(dco, wco_ref[...].reshape(4 * R, D)).astype(BF16)
        acc[0] += _dot_tn(ar_ref[...], dro)
        acc[1] += _dot_tn(ac_ref[...], dco)

        @pl.when(i == n - 1)
        def _():
            for m in range(3):
                stage[...] = acc[m].astype(BF16).reshape(4, R, D)
                pltpu.sync_copy(stage, gsq_ref.at[:, pl.ds(m * R, R), :])

    tile = pl.BlockSpec((tm, D), lambda i: (i, 0))
    wspec = lambda m: pl.BlockSpec((4, R, D), lambda i: (0, m, 0), pipeline_mode=pl.Buffered(1))
    return _pcall(
        body, name="merge_bwd",
        out_shape=(jax.ShapeDtypeStruct(proj.shape, BF16), jax.ShapeDtypeStruct((T, D), BF16),
                   jax.ShapeDtypeStruct((T, D), BF16), jax.ShapeDtypeStruct(wsq.shape, BF16),
                   jax.ShapeDtypeStruct((8, D), F32)),
        grid=(n,),
        in_specs=[tile, pl.BlockSpec((tm, 2 * D), lambda i: (i, 3)), tile, tile, tile, tile, tile, tile,
                  wspec(0), wspec(1), wspec(2), pl.BlockSpec((None, 1, D), lambda i: (l, 0, 0))],
        out_specs=(pl.BlockSpec((tm, 2 * D), lambda i: (i, 3)), tile, tile, ANY, pl.BlockSpec((8, D), lambda i: (0, 0))),
        scratch_shapes=[pltpu.VMEM((3, 4 * R, D), F32), pltpu.VMEM((4, R, D), BF16)],
        sem=("arbitrary",), args=(dxn, proj, a_ret, a_conv, ro, co, ym, z, wsq, wsq, wsq, post_g), ride=ride)


def _conv_bwd_part(n, da_ref, y_ref, p_ref, ph_ref, cw_ref, lg_ref, lb_ref, dp_ref,
                   dcbuf, ubuf, dubuf, wbuf, dwacc, vacc, dsh, ush):
    tc, C = y_ref.shape
    off = HALO - (CONV_K - 1)
    nrb = tc // CONV_RB
    t = pl.program_id(0)
    i = n - 1 - t

    @pl.when(t == 0)
    def _():
        dcbuf[tc:tc + HALO, :] = jnp.zeros((HALO, C), F32)
        dwacc[...] = jnp.zeros_like(dwacc)
        vacc[...] = jnp.zeros_like(vacc)
        _load_conv_w(cw_ref, wbuf, True)

    @pl.when(t > 0)
    def _():
        dcbuf[tc:tc + HALO, :] = dcbuf[0:HALO, :]

    gc = p_ref[:, 2 * C:3 * C].astype(F32)
    _, vjp = jax.vjp(_ln_gate, y_ref[...], gc, lg_ref[...], lb_ref[...])
    dy, dgc, dlg, dlb = vjp(da_ref[...].astype(F32))
    dcbuf[0:tc, :] = dy
    dp_ref[:, 2 * C:3 * C] = dgc.astype(BF16)
    vacc[0:1, :] += jnp.sum(dy, axis=0, keepdims=True)
    vacc[1:2, :] += dlg
    vacc[2:3, :] += dlb

    ga = p_ref[:, 0:C].astype(F32)
    sb = _sigmoid(p_ref[:, C:2 * C].astype(F32))
    ubuf[HALO:HALO + tc, :] = ga * sb
    uh = ph_ref[:, 0:C].astype(F32) * _sigmoid(ph_ref[:, C:2 * C].astype(F32))
    ubuf[0:HALO, :] = jnp.where(i > 0, uh, 0.0)

    _shift_copies(dcbuf, dsh)
    _shift_copies(ubuf, ush)
    def du_block(r, carry):
        r0 = pl.multiple_of(r * CONV_RB, CONV_RB)
        for lanes in _lane_parts(C):
            dubuf[pl.ds(r0, CONV_RB), lanes] = _conv_taps(wbuf, dcbuf, dsh, r0, 0, CONV_RB, lanes)
        return carry

    lax.fori_loop(0, nrb, du_block, 0)
    du = dubuf[...]
    dp_ref[:, 0:C] = (du * sb).astype(BF16)
    dp_ref[:, C:2 * C] = (du * ga * sb * (1.0 - sb)).astype(BF16)

    def dw_block(r, carry):
        r0 = pl.multiple_of(r * CONV_RB, CONV_RB)
        for lanes in _lane_parts(C):
            dyb = dcbuf[pl.ds(r0, CONV_RB), lanes].reshape(CONV_RB // 8, 8, CONV_LANES)
            for k in range(CONV_K):
                dwacc[8 * k:8 * k + 8, lanes] += jnp.sum(dyb * _window(ubuf, ush, r0, off + k, CONV_RB, lanes), axis=0)
        return carry

    lax.fori_loop(0, nrb, dw_block, 0)


def _conv_bwd_final(n, sg_ref, dwacc, vacc):
    C = sg_ref.shape[-1]

    @pl.when(pl.program_id(0) == n - 1)
    def _():
        for k in range(CONV_K):
            sg_ref[pl.ds(k, 1), :] = jnp.sum(dwacc[8 * k:8 * k + 8, :], axis=0, keepdims=True)
        sg_ref[pl.ds(CONV_K, 1), :] = jnp.zeros((1, C), F32)
        sg_ref[ROW_CB:ROW_CB + 8, :] = jnp.zeros((8, C), F32)
        sg_ref[ROW_CB:ROW_CB + 3, :] = vacc[0:3, :]


def _ret_bwd_part(da_ref, p_ref, st_ref, rope_ref, m_ref, qdm_ref, kdm_ref, mqs_ref, cd_ref, dp_ref, gst):
    c, sl, sh = rope_ref[0], rope_ref[1], rope_ref[2]
    for j in range(HEADS // 2):
        rq = _rot(p_ref[:, 128 * j:128 * (j + 1)].astype(F32), c, sl, sh)
        rk = _rot(p_ref[:, QK + 128 * j:QK + 128 * (j + 1)].astype(F32), c, sl, sh)
        rkb = rk.astype(BF16)
        drq = jnp.zeros_like(rq)
        drk = jnp.zeros_like(rk)
        for e in range(2):
            h = 2 * j + e
            v = p_ref[:, 2 * QK + DV * h:2 * QK + DV * (h + 1)]
            g = p_ref[:, 2 * QK + VW + DV * h:2 * QK + VW + DV * (h + 1)].astype(F32)
            mqs = mqs_ref[h, 0:1, :]
            a = (rq * mqs).astype(BF16)
            aq = (rq * qdm_ref[h]).astype(BF16)
            kdv = (rk * kdm_ref[h]).astype(BF16)
            mk = m_ref[h]
            p = (_dot_nt(a, rkb) * mk).astype(BF16)
            stb = st_ref[h].astype(BF16)
            o = _dot(p, v) + _dot(aq, stb)
            _, vjp = jax.vjp(_gn_gate, o, g)
            do, dg = vjp(da_ref[:, DV * h:DV * (h + 1)].astype(F32))
            dob = do.astype(BF16)
            gs = gst[h]
            gsb = gs.astype(BF16)
            ds = (_dot_nt(dob, v) * mk).astype(BF16)
            drq = drq + _dot(ds, rkb) * mqs + _dot_nt(dob, stb) * qdm_ref[h]
            drk = drk + _dot_tn(ds, a) + _dot_nt(v, gsb) * kdm_ref[h]
            dv = _dot_tn(p, dob) + _dot(kdv, gsb)
            gst[h] = _dot_tn(aq, dob) + gs * cd_ref[h, 0:1, :]
            dp_ref[:, 2 * QK + DV * h:2 * QK + DV * (h + 1)] = dv.astype(BF16)
            dp_ref[:, 2 * QK + VW + DV * h:2 * QK + VW + DV * (h + 1)] = dg.astype(BF16)
        dp_ref[:, 128 * j:128 * (j + 1)] = _rot_t(drq, c, sl, sh).astype(BF16)
        dp_ref[:, QK + 128 * j:QK + 128 * (j + 1)] = _rot_t(drk, c, sl, sh).astype(BF16)


def _ret_bwd(dproj, da_ret, proj, states, tb, L, ride=None):
    T = proj.shape[0]
    nS = T // L

    def body(dpin_ref, dar_ref, p_ref, st_ref, rope_ref, m_ref, qdm_ref, kdm_ref, mqs_ref, cd_ref, dp_ref, gst):
        _zero_at_start(gst)
        _ret_bwd_part(dar_ref, p_ref, st_ref, rope_ref, m_ref, qdm_ref, kdm_ref, mqs_ref, cd_ref, dp_ref, gst)

    rev = lambda s: nS - 1 - s
    specs = _ret_specs(T, L)
    specs[0] = pl.BlockSpec((3, L, 128), lambda s: (0, rev(s), 0))
    ptile = pl.BlockSpec((L, PW), lambda s: (rev(s), 0))
    return _pcall(
        body, name="ret_bwd",
        out_shape=(jax.ShapeDtypeStruct(dproj.shape, BF16),),
        grid=(nS,),
        in_specs=[ANY, pl.BlockSpec((L, VW), lambda s: (rev(s), 0)), ptile,
                  pl.BlockSpec((None, HEADS, 128, DV), lambda s: (rev(s), 0, 0, 0))] + specs,
        out_specs=(ptile,),
        scratch_shapes=[pltpu.VMEM((HEADS, 128, DV), F32)],
        sem=("arbitrary",), aliases={0: 0}, ride=ride,
        args=(dproj, da_ret, proj, states, tb["rope"], tb["mask"], tb["qdm"], tb["kdm"], tb["mqs"], tb["cd"]))


def _conv_bwd(dproj, da_conv, y, proj, wcw, ln_g, ln_b, l, ride=None):
    T, C = y.shape
    Cc = wcw.shape[-1]
    tc = min(CONV_TILE, T)
    n = T // tc
    hb = tc // HALO

    def body(dpin_ref, dac_ref, y_ref, p_ref, ph_ref, cw_ref, lg_ref, lb_ref, dp_ref, sg_ref,
             dcbuf, ubuf, dubuf, wbuf, dwacc, vacc, dsh, ush):
        _conv_bwd_part(n, dac_ref, y_ref, p_ref, ph_ref, cw_ref, lg_ref, lb_ref, dp_ref,
                       dcbuf, ubuf, dubuf, wbuf, dwacc, vacc, dsh, ush)
        _conv_bwd_final(n, sg_ref, dwacc, vacc)

    rev = lambda t: n - 1 - t
    vec = pl.BlockSpec((None, 1, C), lambda t: (l, 0, 0))
    tile = pl.BlockSpec((tc, C), lambda t: (rev(t), 0))
    ptile = pl.BlockSpec((tc, 3 * C), lambda t: (rev(t), 1))
    halo = pl.BlockSpec((HALO, 3 * C), lambda t: (jnp.maximum(rev(t) * hb - 1, 0), 1))
    return _pcall(
        body, name="conv_bwd",
        out_shape=(jax.ShapeDtypeStruct(dproj.shape, BF16), jax.ShapeDtypeStruct((ROW_PRE, C), F32)),
        grid=(n,),
        in_specs=[ANY, tile, tile, ptile, halo, pl.BlockSpec((4, None, CONV_K, Cc), lambda t: (0, l, 0, 0)), vec, vec],
        out_specs=(ptile, pl.BlockSpec((ROW_PRE, C), lambda t: (0, 0))),
        scratch_shapes=[pltpu.VMEM((tc + HALO, C), F32), pltpu.VMEM((HALO + tc, C), F32), pltpu.VMEM((tc, C), F32),
                        pltpu.VMEM((8 * CONV_K, C), F32), pltpu.VMEM((8 * CONV_K, C), F32), pltpu.VMEM((8, C), F32),
                        pltpu.VMEM((7, HALO + tc - 8, C), F32), pltpu.VMEM((7, HALO + tc - 8, C), F32)],
        sem=("arbitrary",), aliases={0: 0}, ride=ride, args=(dproj, da_conv, y, proj, proj, wcw, ln_g, ln_b))


def _win_grad(h, dproj, W):
    T, D = h.shape
    tk = min(2048, T)
    nk = T // tk

    def body(h_ref, dp_ref, g_ref, acc):
        k = pl.program_id(1)

        @pl.when(k == 0)
        def _():
            acc[...] = jnp.zeros_like(acc)

        acc[...] += _dot_tn(h_ref[...], dp_ref[...])

        @pl.when(k == nk - 1)
        def _():
            g_ref[...] = acc[...].astype(BF16)

    return pl.pallas_call(
        body, name="win_grad",
        out_shape=jax.ShapeDtypeStruct((4, D, W), BF16),
        grid=(4, nk),
        in_specs=[pl.BlockSpec((tk, D), lambda j, k: (k, 0)), pl.BlockSpec((tk, W), lambda j, k: (k, j))],
        out_specs=pl.BlockSpec((None, D, W), lambda j, k: (j, 0, 0)),
        scratch_shapes=[pltpu.VMEM((D, W), F32)],
        compiler_params=_cp(("arbitrary", "arbitrary")),
    )(h, dproj)


def _in_bwd(dxn, dproj, x, pre_g, win, l, ride=None):
    T, D = x.shape
    W = win.shape[-1]
    tm = min(512, T)

    def body(dxn_ref, dp_ref, x_ref, g_ref, w_ref, dx_ref, dg_ref):
        @pl.when(pl.program_id(0) == 0)
        def _():
            dg_ref[...] = jnp.zeros_like(dg_ref)

        dh = _dot_nt(dp_ref[:, 0:W], w_ref[0])
        for j in range(1, 4):
            dh = dh + _dot_nt(dp_ref[:, j * W:(j + 1) * W], w_ref[j])
        _, vjp = jax.vjp(_rms, x_ref[...], g_ref[...])
        dx, dg = vjp(dh)
        dx_ref[...] = dxn_ref[...] + dx
        dg_ref[0:1, :] += dg

    tile = pl.BlockSpec((tm, D), lambda i: (i, 0))
    return _pcall(
        body, name="in_bwd",
        out_shape=(jax.ShapeDtypeStruct((T, D), F32), jax.ShapeDtypeStruct((8, D), F32)),
        grid=(T // tm,),
        in_specs=[tile, pl.BlockSpec((tm, 4 * W), lambda i: (i, 0)), tile,
                  pl.BlockSpec((None, 1, D), lambda i: (l, 0, 0)),
                  pl.BlockSpec((4, D, W), lambda i: (0, 0, 0), pipeline_mode=pl.Buffered(1))],
        out_specs=(tile, pl.BlockSpec((8, D), lambda i: (0, 0))),
        sem=("arbitrary",), args=(dxn, dproj, x, pre_g, win), ride=ride)


def _sum_group(chip, t, u):
    _, A, B = t.shape
    tr = min(256, A)

    def body(k_ref, t_ref, u_ref, o_ref):
        o_ref[...] = ((t_ref[...].astype(F32) + u_ref[0].astype(F32)) + u_ref[1].astype(F32)) + u_ref[2].astype(F32)

    return pl.pallas_call(
        body, name="sum_group",
        out_shape=jax.ShapeDtypeStruct((A, B), F32),
        grid_spec=pltpu.PrefetchScalarGridSpec(
            num_scalar_prefetch=1, grid=(A // tr,),
            in_specs=[pl.BlockSpec((None, tr, B), lambda i, k: (k[0], i, 0)),
                      pl.BlockSpec((3, tr, B), lambda i, k: (0, i, 0))],
            out_specs=pl.BlockSpec((tr, B), lambda i, k: (i, 0))),
        compiler_params=_cp(("arbitrary",)),
    )(chip, t, u)


def _swap_rows(g):
    _, A, B = g.shape
    nh = A // 2

    def body(g_ref, r_ref, send_sems, recv_sems):
        x, y, c = _place()
        cp = _remote(g_ref.at[:, pl.ds((1 - c) * nh, nh)], r_ref, send_sems, recv_sems, 0, (x, y, 1 - c))
        cp.start()
        cp.wait()

    return pl.pallas_call(
        body, name="swap_rows",
        out_shape=jax.ShapeDtypeStruct((4, nh, B), g.dtype),
        in_specs=[ANY], out_specs=ANY,
        scratch_shapes=[pltpu.SemaphoreType.DMA((1,)), pltpu.SemaphoreType.DMA((1,))],
        compiler_params=pltpu.CompilerParams(has_side_effects=True),
    )(g)


def _add_rows(cidx, g, r):
    _, nh, B = r.shape
    tr = min(256, nh)
    nb = nh // tr

    def body(c_ref, g_ref, r_ref, o_ref):
        o_ref[...] = (g_ref[...].astype(F32) + r_ref[...].astype(F32)).astype(BF16)

    blk = (None, tr, B)
    return pl.pallas_call(
        body, name="add_rows",
        out_shape=jax.ShapeDtypeStruct(r.shape, BF16),
        grid_spec=pltpu.PrefetchScalarGridSpec(
            num_scalar_prefetch=1, grid=(4, nb),
            in_specs=[pl.BlockSpec(blk, lambda k, i, c: (k, c[0] * nb + i, 0)),
                      pl.BlockSpec(blk, lambda k, i, c: (k, i, 0))],
            out_specs=pl.BlockSpec(blk, lambda k, i, c: (k, i, 0))),
        compiler_params=_cp(("arbitrary", "arbitrary")),
    )(cidx, g, r)


def _sum_group_half(chip, cidx, t, u):
    _, nh, B = t.shape
    tr = min(256, nh)
    nb = nh // tr

    def body(k_ref, c_ref, t_ref, u_ref, o_ref):
        mine = (pl.program_id(0) // nb) == c_ref[0]

        @pl.when(mine)
        def _():
            o_ref[...] = ((t_ref[...].astype(F32) + u_ref[0].astype(F32)) + u_ref[1].astype(F32)) + u_ref[2].astype(F32)

        @pl.when(jnp.logical_not(mine))
        def _():
            o_ref[...] = jnp.zeros_like(o_ref)

    own = lambda i, c: jnp.clip(i - c[0] * nb, 0, nb - 1)
    return pl.pallas_call(
        body, name="sum_group_half",
        out_shape=jax.ShapeDtypeStruct((2 * nh, B), F32),
        grid_spec=pltpu.PrefetchScalarGridSpec(
            num_scalar_prefetch=2, grid=(2 * nb,),
            in_specs=[pl.BlockSpec((None, tr, B), lambda i, k, c: (k[0], own(i, c), 0)),
                      pl.BlockSpec((3, tr, B), lambda i, k, c: (0, own(i, c), 0))],
            out_specs=pl.BlockSpec((tr, B), lambda i, k, c: (i, 0))),
        compiler_params=_cp(("arbitrary",)),
    )(chip, cidx, t, u)


def _adam_math(w, g, m, v):
    c1 = 1.0 / (1.0 - ADAM_B1 ** ADAM_STEP)
    c2 = 1.0 / (1.0 - ADAM_B2 ** ADAM_STEP)
    nm = ADAM_B1 * m + (1.0 - ADAM_B1) * g
    nv = ADAM_B2 * v + (1.0 - ADAM_B2) * (g * g)
    return -ADAM_LR * ((nm * c1) / (jnp.sqrt(nv * c2) + ADAM_EPS) + ADAM_WD * w), nm, nv


def _adamw_layer(prev, w, m, v, sa, sb, l, part, ride=None):
    NL, A, B = w.shape
    tr = A
    while tr * B * 4 > ADAM_BLOCK_BYTES and tr % 16 == 0:
        tr //= 2
    nb = A // tr

    def body(p0, p1, p2, p3, w_ref, m_ref, v_ref, sa_ref, sb_ref, g_ref, d_ref, nm_ref, nv_ref):
        g = sa_ref[...] + sb_ref[...]
        g_ref[...] = g
        d_ref[...], nm_ref[...], nv_ref[...] = _adam_math(w_ref[...], g, m_ref[...], v_ref[...])

    lay = pl.BlockSpec((None, tr, B), lambda i: (l, i, 0))
    src = pl.BlockSpec((tr, B), lambda i: (part * nb + i, 0))
    full = jax.ShapeDtypeStruct((NL, A, B), F32)
    if prev is None:
        prev = tuple(lax.empty((NL, A, B), F32) for _ in range(4))
    outs, landed = _pcall(
        body, name="adamw_layer",
        out_shape=(full,) * 4, grid=(nb,),
        in_specs=[ANY] * 4 + [lay, lay, lay, src, src], out_specs=(lay,) * 4,
        sem=("arbitrary",), aliases={0: 0, 1: 1, 2: 2, 3: 3}, args=(*prev, w, m, v, sa, sb), ride=ride)
    return tuple(outs), landed


def _adamw(w, g, m, v):
    shape = w.shape
    cols = shape[-1]
    rows = int(np.prod(shape[:-1]))

    def body(w_ref, g_ref, m_ref, v_ref, d_ref, nm_ref, nv_ref):
        d_ref[...], nm_ref[...], nv_ref[...] = _adam_math(w_ref[...], g_ref[...], m_ref[...], v_ref[...])

    tile = pl.BlockSpec((rows, cols), lambda i: (0, 0))
    out = jax.ShapeDtypeStruct((rows, cols), F32)
    res = pl.pallas_call(
        body, name="adamw",
        out_shape=(out, out, out), grid=(1,),
        in_specs=[tile] * 4, out_specs=(tile,) * 3,
        compiler_params=_cp(("arbitrary",)),
    )(*[a.reshape(rows, cols) for a in (w, g, m, v)])
    return tuple(a.reshape(shape) for a in res)


def _tail_exchange(small, s_in, s_sq):
    def body(s_ref, a_ref, b_ref, o_ref, oa_ref, ob_ref, send_sems, recv_sems, local_sem):
        x, y, c = _place()
        me = 4 * x + 2 * y + c
        sibling = (x, y, 1 - c)
        mine = pltpu.make_async_copy(s_ref, o_ref.at[me], local_sem)
        mine.start()
        swaps = [_remote(a_ref, oa_ref, send_sems, recv_sems, 7, sibling), _remote(b_ref, ob_ref, send_sems, recv_sems, 8, sibling)]
        sends = [_remote(s_ref, o_ref.at[me], send_sems, recv_sems, r, peer) for r, peer in enumerate(_peers(x, y, c))]
        for cp in swaps + sends:
            cp.start()
        for r, peer in enumerate(_peers(x, y, c)):
            theirs = o_ref.at[4 * peer[0] + 2 * peer[1] + peer[2]]
            _remote(theirs, theirs, send_sems, recv_sems, r, peer).wait_recv()
        for cp in sends:
            cp.wait_send()
        for cp in swaps:
            cp.wait()
        mine.wait()

    return pl.pallas_call(
        body, name="tail_exchange",
        out_shape=(jax.ShapeDtypeStruct((8,) + small.shape, small.dtype),
                   jax.ShapeDtypeStruct(s_in.shape, s_in.dtype), jax.ShapeDtypeStruct(s_sq.shape, s_sq.dtype)),
        in_specs=[ANY] * 3, out_specs=(ANY,) * 3,
        scratch_shapes=[pltpu.SemaphoreType.DMA((9,)), pltpu.SemaphoreType.DMA((9,)), pltpu.SemaphoreType.DMA],
        compiler_params=pltpu.CompilerParams(has_side_effects=True),
    )(small, s_in, s_sq)


def _sum_devices(gs):
    NL = len(gs)
    _, R, D = gs[0].shape

    def body(*refs):
        o_ref = refs[NL]
        for l in range(NL):
            acc = refs[l][0]
            for k in range(1, 8):
                acc = acc + refs[l][k]
            o_ref[l] = acc

    return pl.pallas_call(
        body, name="sum_devices",
        out_shape=jax.ShapeDtypeStruct((NL, R, D), F32),
        grid=(1,),
        in_specs=[pl.BlockSpec((8, R, D), lambda i: (0, 0, 0))] * NL,
        out_specs=pl.BlockSpec((NL, R, D), lambda i: (0, 0, 0)),
        compiler_params=_cp(("arbitrary",)),
    )(*gs)


def kernel(x, pre_norm_g, w_in, w_ret_out, conv_w, conv_b, conv_ln_g, conv_ln_b, w_conv_out, w_o, post_norm_g, loss_target, m_pre_norm_g, m_w_in, m_w_ret_out, m_conv_w, m_conv_b, m_conv_ln_g, m_conv_ln_b, m_w_conv_out, m_w_o, m_post_norm_g, v_pre_norm_g, v_w_in, v_w_ret_out, v_conv_w, v_conv_b, v_conv_ln_g, v_conv_ln_b, v_w_conv_out, v_w_o, v_post_norm_g):
    NL, D, W = w_in.shape
    Cc = conv_w.shape[-1]
    T = x.shape[1]
    L = min(RET_BLOCK, T)
    tb = _tables(T, L)
    ax, ay, ac = _place()
    chip = (2 * ax + ay).astype(jnp.int32).reshape(1)
    cidx = ac.astype(jnp.int32).reshape(1)
    pre_g, cb, lg, lb, post_g = (a.reshape(NL, 1, D) for a in (pre_norm_g, conv_b, conv_ln_g, conv_ln_b, post_norm_g))

    win = [_cast_win(chip, w_in, l) for l in range(NL)]
    wsq = [_cast_wsq(chip, w_ret_out, w_conv_out, w_o, l) for l in range(NL)]
    win[0], wcw = _gather_first(win[0], _place_cw(chip, conv_w))

    saved = []
    xl = x[0]
    for l in range(NL):
        more = l + 1 < NL
        ride = _Ride()
        if more:
            _ride_gather_ici(ride, "win", win[l + 1], (0, 5, 8))
        (proj, h), got = _fwd_in(xl, pre_g, win[l], l, ride=ride)
        if more:
            win[l + 1] = got["win"]
        ride = _Ride()
        if l == 0:
            _ride_gather_ici(ride, "wsq0", wsq[0])
        (a_ret, states), got = _ret_fwd(proj, tb, L, ride=ride)
        if l == 0:
            wsq[0] = got["wsq0"]
        ride = _Ride()
        if more:
            _ride_gather_ici(ride, "win", win[l + 1], (5, 8, 8))
            _ride_gather_ici(ride, "wsq", wsq[l + 1])
        if l == 0:
            _ride_gather_pass(ride, "wsq0", wsq[0])
        (a_conv, y), got = _conv_fwd(proj, wcw, cb, lg, lb, l, ride=ride)
        if more:
            win[l + 1], wsq[l + 1] = got["win"], got["wsq"]
        if l == 0:
            wsq[0] = got["wsq0"]
        ride = _Ride()
        if more:
            _ride_gather_pass(ride, "win", win[l + 1])
            _ride_gather_pass(ride, "wsq", wsq[l + 1])
        (xn, ro, co, ym, z), got = _merge_fwd(xl, proj, a_ret, a_conv, wsq[l], post_g, l, ride=ride)
        if more:
            win[l + 1], wsq[l + 1] = got["win"], got["wsq"]
        saved.append((xl, proj, h, a_ret, states, a_conv, y, ro, co, ym, z))
        xl = xn
    dx, lsum = _loss_fwd_bwd(xl, loss_target[0])
    loss = lax.psum(jnp.sum(lsum), ("x", "y", "c"))

    gin, gsq, uin, usq = [None] * NL, [None] * NL, [None] * NL, [None] * NL
    s_in, s_sq, o_in, o_sq = [None] * NL, [None] * NL, [None] * NL, [None] * NL
    small, gs = [None] * NL, [None] * NL
    for l in reversed(range(NL)):
        xin, proj, h, a_ret, states, a_conv, y, ro, co, ym, z = saved[l]
        (dproj, da_ret, da_conv, gsq[l], dpost), _ = _merge_bwd(dx, proj, a_ret, a_conv, ro, co, ym, z, wsq[l], post_g, l)
        ride = _Ride()
        _ride_exchange(ride, "gsq", gsq[l], "usq", None, (0, 1, 2))
        if l + 1 < NL:
            _ride_exchange(ride, "gin", gin[l + 1], "uin", uin[l + 1], (2,), (0, 1, 2))
        (dproj, sg), got = _conv_bwd(dproj, da_conv, y, proj, wcw, lg, lb, l, ride=ride)
        usq[l] = got["usq"]
        if l + 1 < NL:
            uin[l + 1] = got["uin"]
        ride = _Ride()
        if l + 1 < NL:
            _ride_exchange(ride, "gin", gin[l + 1], "uin", uin[l + 1], (0, 1, 2), (1, 2, 2))
            _ride_gather_all(ride, "small", small[l + 1], "gs")
        (dproj,), got = _ret_bwd(dproj, da_ret, proj, states, tb, L, ride=ride)
        if l + 1 < NL:
            uin[l + 1], gs[l + 1] = got["uin"], got["gs"]
        gin[l] = _win_grad(h, dproj, W)
        ride = _Ride()
        if l > 0:
            _ride_exchange(ride, "gin", gin[l], "uin", None, (0, 1), (0, 1, 2))
        else:
            gin[0] = _add_rows(cidx, gin[0], _swap_rows(gin[0]))
            _ride_exchange(ride, "gin", gin[0], "uin", None, (0, 1, 2))
        if l + 1 < NL:
            s_in[l + 1] = _sum_group(chip, gin[l + 1], uin[l + 1])
            s_sq[l + 1] = _sum_group(chip, gsq[l + 1], usq[l + 1])
            _ride_swap(ride, "s_in", s_in[l + 1], "o_in")
            _ride_swap(ride, "s_sq", s_sq[l + 1], "o_sq")
        (dx, dpre), got = _in_bwd(dx, dproj, xin, pre_g, win[l], l, ride=ride)
        uin[l] = got["uin"]
        if l + 1 < NL:
            o_in[l + 1], o_sq[l + 1] = got["o_in"], got["o_sq"]
        small[l] = jnp.concatenate([sg, dpre, dpost], axis=0)
    grad_x = dx

    big = {"w_in": None, "w_ret_out": None, "w_conv_out": None, "w_o": None}
    wts = {"w_in": (w_in, m_w_in, v_w_in), "w_ret_out": (w_ret_out, m_w_ret_out, v_w_ret_out),
           "w_conv_out": (w_conv_out, m_w_conv_out, v_w_conv_out), "w_o": (w_o, m_w_o, v_w_o)}
    sq_names = ("w_ret_out", "w_conv_out", "w_o")

    def adam_in(l, ride=None):
        big["w_in"], got = _adamw_layer(big["w_in"], *wts["w_in"], s_in[l], o_in[l], l, 0, ride=ride)
        return got

    def adam_sq(l, part, ride=None):
        n = sq_names[part]
        big[n], got = _adamw_layer(big[n], *wts[n], s_sq[l], o_sq[l], l, part, ride=ride)
        return got

    s_in[0] = _sum_group_half(chip, cidx, gin[0], uin[0])
    s_sq[0] = _sum_group(chip, gsq[0], usq[0])
    gs[0], o_in[0], o_sq[0] = _tail_exchange(small[0], s_in[0], s_sq[0])
    for l in reversed(range(NL)):
        adam_in(l)
        for part in range(3):
            adam_sq(l, part)

    gsm = _sum_devices(gs)
    grads = {
        "pre_norm_g": gsm[:, ROW_PRE], "conv_w": lax.dynamic_slice_in_dim(gsm[:, 0:CONV_K], chip[0] * Cc, Cc, axis=2),
        "conv_b": gsm[:, ROW_CB], "conv_ln_g": gsm[:, ROW_LG], "conv_ln_b": gsm[:, ROW_LB], "post_norm_g": gsm[:, ROW_POST],
    }
    weights = dict(pre_norm_g=pre_norm_g, conv_w=conv_w, conv_b=conv_b, conv_ln_g=conv_ln_g, conv_ln_b=conv_ln_b,
                   post_norm_g=post_norm_g)
    m1 = dict(pre_norm_g=m_pre_norm_g, conv_w=m_conv_w, conv_b=m_conv_b, conv_ln_g=m_conv_ln_g, conv_ln_b=m_conv_ln_b,
              post_norm_g=m_post_norm_g)
    m2 = dict(pre_norm_g=v_pre_norm_g, conv_w=v_conv_w, conv_b=v_conv_b, conv_ln_g=v_conv_ln_g, conv_ln_b=v_conv_ln_b,
              post_norm_g=v_post_norm_g)
    res = {n: (grads[n],) + _adamw(weights[n], grads[n], m1[n], m2[n]) for n in grads}
    res.update(big)
    order = ["pre_norm_g", "w_in", "w_ret_out", "conv_w", "conv_b", "conv_ln_g", "conv_ln_b", "w_conv_out", "w_o", "post_norm_g"]
    return (loss, grad_x[None], *[res[n][0] for n in order], *[res[n][1] for n in order],
            *[res[n][2] for n in order], *[res[n][3] for n in order])
```

```python
import numpy as np
import jax
import jax.numpy as jnp
from jax import lax
from jax.experimental import pallas as pl
from jax.experimental.pallas import tpu as pltpu

F32 = jnp.float32
BF16 = jnp.bfloat16

HEADS = 8
DK = 64
DV = 128
CONV_K = 31
CHUNK = 64
ROPE_BASE = 10000.0
EPS = 1e-6
HALO = 32
CONV_RB = 32
CONV_LANES = 512
CONV_TILE = 256
RET_BLOCK = 512

ADAM_LR = 0.001
ADAM_B1 = 0.9
ADAM_B2 = 0.999
ADAM_EPS = 1e-08
ADAM_WD = 0.01
ADAM_STEP = 10
ADAM_BLOCK_BYTES = 2 * 1024 * 1024

VMEM_LIMIT = 56 * 1024 * 1024
MESH_T = pl.DeviceIdType.MESH
ANY = pl.BlockSpec(memory_space=pl.ANY)

ROW_CB, ROW_LG, ROW_LB = 32, 33, 34
ROW_PRE, ROW_POST = 40, 48


def _cp(sem=None, **kw):
    return pltpu.CompilerParams(dimension_semantics=sem, vmem_limit_bytes=VMEM_LIMIT, **kw)


def _dot(a, b):
    return jnp.dot(a, b, preferred_element_type=F32)


def _dot_nt(a, b):
    return lax.dot_general(a, b, (((1,), (1,)), ((), ())), preferred_element_type=F32)


def _dot_tn(a, b):
    return lax.dot_general(a, b, (((0,), (0,)), ((), ())), preferred_element_type=F32)


def _sigmoid(x):
    return jax.nn.sigmoid(x)


def _silu(x):
    return x * _sigmoid(x)


def _rms(x, g):
    return x * lax.rsqrt(jnp.mean(x * x, axis=-1, keepdims=True) + EPS) * g


def _gn_gate(o, g):
    mu = jnp.mean(o, axis=-1, keepdims=True)
    d = o - mu
    var = jnp.mean(d * d, axis=-1, keepdims=True)
    return d * lax.rsqrt(var + EPS) * _silu(g)


def _ln_gate(y, gc, lg, lb):
    mu = jnp.mean(y, axis=-1, keepdims=True)
    d = y - mu
    var = jnp.mean(d * d, axis=-1, keepdims=True)
    return _silu(d * lax.rsqrt(var + EPS) * lg + lb) * _silu(gc)


def _tables(T, L):
    lane = np.arange(128)
    d = lane % DK
    half = DK // 2
    inv = (ROPE_BASE ** (-(np.arange(half, dtype=np.float32)) / half)).astype(np.float32)
    ang = (np.arange(T, dtype=np.float32)[:, None] * inv[None, :]).astype(np.float64)
    angl = ang[:, d % half]
    cos = np.cos(angl)
    sin = np.sin(angl)
    lo = (d < half)[None, :]
    rope = np.stack([cos, np.where(lo, -sin, 0.0), np.where(lo, 0.0, sin)]).astype(np.float32)

    hh = np.arange(HEADS, dtype=np.float64)
    log_g = np.log1p(-np.exp2(-5.0 - hh))
    n = np.arange(L, dtype=np.float64)
    cn = np.arange(L) // CHUNK
    allowed = (cn[None, :] <= cn[:, None])
    dist = np.abs(n[:, None] - n[None, :])
    mask = np.exp(log_g[:, None, None] * dist[None]) * allowed[None]
    mq = ((lane[None, :] // DK) == (np.arange(HEADS)[:, None] % 2)).astype(np.float64)
    qd = np.exp(log_g[:, None] * n[None, :])
    kd = np.exp(log_g[:, None] * (L - n[None, :]))
    qdm = qd[:, :, None] * mq[:, None, :] * (DK ** -0.5)
    kdm = kd[:, :, None] * mq[:, None, :]
    mqs = np.broadcast_to((mq * (DK ** -0.5))[:, None, :], (HEADS, 8, 128))
    cd = np.broadcast_to(np.exp(log_g * L)[:, None, None], (HEADS, 8, 128))
    f = lambda a: jnp.asarray(np.ascontiguousarray(a), dtype=F32)
    return dict(rope=f(rope), mask=f(mask), qdm=f(qdm), kdm=f(kdm), mqs=f(mqs), cd=f(cd))


def _rot(b, c, sl, sh):
    return b * c + pltpu.roll(b, 96, axis=1) * sl + pltpu.roll(b, 32, axis=1) * sh


def _rot_t(d, c, sl, sh):
    return d * c + pltpu.roll(d * sl, 32, axis=1) + pltpu.roll(d * sh, 96, axis=1)


def _place():
    return lax.axis_index("x"), lax.axis_index("y"), lax.axis_index("c")


def _other_chips(x, y):
    return [(1 - x, y), (x, 1 - y), (1 - x, 1 - y)]


def _remote(src, dst, send_sems, recv_sems, k, to):
    return pltpu.make_async_remote_copy(src_ref=src, dst_ref=dst, send_sem=send_sems.at[k], recv_sem=recv_sems.at[k],
                                        device_id=to, device_id_type=MESH_T)


class _Ride:
    def __init__(self):
        self.arrays, self.kinds, self.names = [], [], []
        self.fresh = []
        self.ops = []

    def read(self, name, a):
        self.names.append(name)
        self.arrays.append(a)
        self.kinds.append("in")

    def inout(self, name, a):
        self.names.append(name)
        self.arrays.append(a)
        self.kinds.append("inout")

    def land(self, name, shape, dtype):
        self.fresh.append((name, jax.ShapeDtypeStruct(shape, dtype)))

    def op(self, n_sems, start, finish):
        self.ops.append((n_sems, start, finish))


def _pcall(body, *, name, grid, in_specs, out_specs, out_shape, args, scratch_shapes=(), sem, aliases=None, ride=None):
    if ride is None or not ride.ops:
        outs = pl.pallas_call(body, name=name, grid=grid, in_specs=list(in_specs), out_specs=tuple(out_specs),
                              out_shape=tuple(out_shape), scratch_shapes=list(scratch_shapes),
                              input_output_aliases=dict(aliases or {}), compiler_params=_cp(sem))(*args)
        return outs, {}

    ni, no, nr = len(args), len(out_shape), len(ride.arrays)
    inout = [i for i, k in enumerate(ride.kinds) if k == "inout"]
    r_out_shapes = [jax.ShapeDtypeStruct(ride.arrays[i].shape, ride.arrays[i].dtype) for i in inout] + [s for _, s in ride.fresh]
    r_out_names = [ride.names[i] for i in inout] + [n for n, _ in ride.fresh]
    nro = len(r_out_shapes)
    n_sems = sum(n for n, _, _ in ride.ops)
    n_scr = len(scratch_shapes)
    nd = len(grid)

    def wrapped(*refs):
        ins, rin = refs[:ni], refs[ni:ni + nr]
        outs, rout = refs[ni + nr:ni + nr + no], refs[ni + nr + no:ni + nr + no + nro]
        scr = refs[ni + nr + no + nro:ni + nr + no + nro + n_scr]
        send_sems, recv_sems = refs[-2], refs[-1]
        view = {nm: r for nm, r, k in zip(ride.names, rin, ride.kinds) if k == "in"}
        view.update(dict(zip(r_out_names, rout)))
        first = pl.program_id(0) == 0
        last = pl.program_id(0) == grid[0] - 1
        for d in range(1, nd):
            first = first & (pl.program_id(d) == 0)
            last = last & (pl.program_id(d) == grid[d] - 1)

        @pl.when(first)
        def _():
            base = 0
            for n, start, _ in ride.ops:
                start(view, send_sems, recv_sems, base)
                base += n

        body(*ins, *outs, *scr)

        @pl.when(last)
        def _():
            base = 0
            for n, _, finish in ride.ops:
                finish(view, send_sems, recv_sems, base)
                base += n

    res = pl.pallas_call(
        wrapped, name=name, grid=grid,
        in_specs=list(in_specs) + [ANY] * nr, out_specs=tuple(out_specs) + (ANY,) * nro,
        out_shape=tuple(out_shape) + tuple(r_out_shapes),
        scratch_shapes=list(scratch_shapes) + [pltpu.SemaphoreType.DMA((n_sems,)), pltpu.SemaphoreType.DMA((n_sems,))],
        input_output_aliases={**dict(aliases or {}), **{ni + i: no + j for j, i in enumerate(inout)}},
        compiler_params=_cp(sem),
    )(*args, *ride.arrays)
    return res[:no], dict(zip(r_out_names, res[no:]))


def _half(ref, chip_idx, cc, part=(0, 1, 1)):
    n = ref.shape[1] // 2
    lo, hi, k = part
    return ref.at[chip_idx, pl.ds(cc * n + lo * n // k, (hi - lo) * n // k)]


def _ride_gather_ici(ride, name, a, part=(0, 1, 1)):
    ride.inout(name, a)

    def start(view, ss, rs, b):
        x, y, c = _place()
        mine = _half(view[name], 2 * x + y, c, part)
        for j, (cx, cy) in enumerate(_other_chips(x, y)):
            _remote(mine, mine, ss, rs, b + j, (cx, cy, c)).start()

    def finish(view, ss, rs, b):
        x, y, c = _place()
        mine = _half(view[name], 2 * x + y, c, part)
        for j, (cx, cy) in enumerate(_other_chips(x, y)):
            theirs = _half(view[name], 2 * cx + cy, c, part)
            _remote(theirs, theirs, ss, rs, b + j, (cx, cy, c)).wait_recv()
        for j, (cx, cy) in enumerate(_other_chips(x, y)):
            _remote(mine, mine, ss, rs, b + j, (cx, cy, c)).wait_send()

    ride.op(3, start, finish)


def _ride_gather_direct(ride, name, a):
    ride.inout(name, a)

    def start(view, ss, rs, b):
        x, y, c = _place()
        mine = _half(view[name], 2 * x + y, c)
        for j, (cx, cy) in enumerate(_other_chips(x, y)):
            _remote(mine, mine, ss, rs, b + 2 * j, (cx, cy, c)).start()
            _remote(mine, mine, ss, rs, b + 2 * j + 1, (cx, cy, 1 - c)).start()

    def finish(view, ss, rs, b):
        x, y, c = _place()
        mine = _half(view[name], 2 * x + y, c)
        for j, (cx, cy) in enumerate(_other_chips(x, y)):
            same = _half(view[name], 2 * cx + cy, c)
            other = _half(view[name], 2 * cx + cy, 1 - c)
            _remote(same, same, ss, rs, b + 2 * j, (cx, cy, c)).wait_recv()
            _remote(other, other, ss, rs, b + 2 * j + 1, (cx, cy, 1 - c)).wait_recv()
        for j, (cx, cy) in enumerate(_other_chips(x, y)):
            _remote(mine, mine, ss, rs, b + 2 * j, (cx, cy, c)).wait_send()
            _remote(mine, mine, ss, rs, b + 2 * j + 1, (cx, cy, 1 - c)).wait_send()

    ride.op(6, start, finish)


def _ride_gather_pass(ride, name, a):
    ride.inout(name, a)

    def start(view, ss, rs, b):
        x, y, c = _place()
        for j, (cx, cy) in enumerate(_other_chips(x, y)):
            blk = _half(view[name], 2 * cx + cy, c)
            _remote(blk, blk, ss, rs, b + j, (x, y, 1 - c)).start()

    def finish(view, ss, rs, b):
        x, y, c = _place()
        for j, (cx, cy) in enumerate(_other_chips(x, y)):
            theirs = _half(view[name], 2 * cx + cy, 1 - c)
            _remote(theirs, theirs, ss, rs, b + j, (x, y, 1 - c)).wait_recv()
        for j, (cx, cy) in enumerate(_other_chips(x, y)):
            blk = _half(view[name], 2 * cx + cy, c)
            _remote(blk, blk, ss, rs, b + j, (x, y, 1 - c)).wait_send()

    ride.op(3, start, finish)


def _ride_exchange(ride, src_name, src, dst_name, dst, plan):
    ride.read(src_name, src)
    if dst is None:
        ride.land(dst_name, (3,) + src.shape[1:], src.dtype)
    else:
        ride.inout(dst_name, dst)
    A = src.shape[1]

    def copy(view, ss, rs, sem, j, part, x, y, c):
        lo, hi, k = part
        rows = pl.ds(lo * A // k, (hi - lo) * A // k)
        cx, cy = _other_chips(x, y)[j]
        return _remote(view[src_name].at[2 * cx + cy, rows], view[dst_name].at[j, rows], ss, rs, sem, (cx, cy, c))

    def start(view, ss, rs, b):
        x, y, c = _place()
        for i, (j, part) in enumerate(plan):
            copy(view, ss, rs, b + i, j, part, x, y, c).start()

    def finish(view, ss, rs, b):
        x, y, c = _place()
        for i, (j, part) in enumerate(plan):
            copy(view, ss, rs, b + i, j, part, x, y, c).wait()

    ride.op(len(plan), start, finish)


WHOLE, FIRST_HALF, SECOND_HALF = (0, 1, 1), (0, 1, 2), (1, 2, 2)


def _ride_swap(ride, src_name, src, dst_name):
    ride.read(src_name, src)
    ride.land(dst_name, src.shape, src.dtype)

    def start(view, ss, rs, b):
        x, y, c = _place()
        _remote(view[src_name], view[dst_name], ss, rs, b, (x, y, 1 - c)).start()

    def finish(view, ss, rs, b):
        x, y, c = _place()
        _remote(view[src_name], view[dst_name], ss, rs, b, (x, y, 1 - c)).wait()

    ride.op(1, start, finish)


def _peers(x, y, c):
    flip = lambda v, b: 1 - v if b else v
    return [(flip(x, r & 4), flip(y, r & 2), flip(c, r & 1)) for r in range(1, 8)]


def _ride_gather_all(ride, src_name, src, dst_name):
    ride.read(src_name, src)
    ride.land(dst_name, (8,) + src.shape, src.dtype)

    def start(view, ss, rs, b):
        x, y, c = _place()
        me = 4 * x + 2 * y + c
        pltpu.make_async_copy(view[src_name], view[dst_name].at[me], ss.at[b + 7]).start()
        for r, peer in enumerate(_peers(x, y, c)):
            _remote(view[src_name], view[dst_name].at[me], ss, rs, b + r, peer).start()

    def finish(view, ss, rs, b):
        x, y, c = _place()
        me = 4 * x + 2 * y + c
        for r, peer in enumerate(_peers(x, y, c)):
            theirs = view[dst_name].at[4 * peer[0] + 2 * peer[1] + peer[2]]
            _remote(theirs, theirs, ss, rs, b + r, peer).wait_recv()
        for r, peer in enumerate(_peers(x, y, c)):
            _remote(view[src_name], view[dst_name].at[me], ss, rs, b + r, peer).wait_send()
        pltpu.make_async_copy(view[src_name], view[dst_name].at[me], ss.at[b + 7]).wait()

    ride.op(8, start, finish)


def _cast_win(chip, w_in, l):
    _, D, W = w_in.shape
    tr = min(256, D)

    def body(chip_ref, w_ref, o_ref):
        o_ref[...] = w_ref[...].astype(BF16)

    return pl.pallas_call(
        body, name="cast_win",
        out_shape=jax.ShapeDtypeStruct((4, D, W), BF16),
        grid_spec=pltpu.PrefetchScalarGridSpec(
            num_scalar_prefetch=1, grid=(D // tr,),
            in_specs=[pl.BlockSpec((None, tr, W), lambda r, c: (l, r, 0))],
            out_specs=pl.BlockSpec((None, tr, W), lambda r, c: (c[0], r, 0))),
        compiler_params=_cp(("arbitrary",)),
    )(chip, w_in)


def _cast_wsq(chip, w_ro, w_co, w_o, l):
    _, R, D = w_ro.shape

    def body(chip_ref, a_ref, b_ref, c_ref, o_ref):
        o_ref[0:R, :] = a_ref[...].astype(BF16)
        o_ref[R:2 * R, :] = b_ref[...].astype(BF16)
        o_ref[2 * R:3 * R, :] = c_ref[...].astype(BF16)

    spec = pl.BlockSpec((None, R, D), lambda i, c: (l, 0, 0))
    return pl.pallas_call(
        body, name="cast_wsq",
        out_shape=jax.ShapeDtypeStruct((4, 3 * R, D), BF16),
        grid_spec=pltpu.PrefetchScalarGridSpec(
            num_scalar_prefetch=1, grid=(1,),
            in_specs=[spec, spec, spec],
            out_specs=pl.BlockSpec((None, 3 * R, D), lambda i, c: (c[0], 0, 0))),
        compiler_params=_cp(("arbitrary",)),
    )(chip, w_ro, w_co, w_o)


def _place_cw(chip, conv_w):
    NL, K, Cc = conv_w.shape

    def body(chip_ref, w_ref, o_ref):
        o_ref[...] = w_ref[...]

    return pl.pallas_call(
        body, name="place_cw",
        out_shape=jax.ShapeDtypeStruct((4, NL, K, Cc), F32),
        grid_spec=pltpu.PrefetchScalarGridSpec(
            num_scalar_prefetch=1, grid=(1,),
            in_specs=[pl.BlockSpec((NL, K, Cc), lambda i, c: (0, 0, 0))],
            out_specs=pl.BlockSpec((None, NL, K, Cc), lambda i, c: (c[0], 0, 0, 0))),
        compiler_params=_cp(("arbitrary",)),
    )(chip, conv_w)


def _gather_first(win0, wcw):
    n_arr = 2

    def body(a0, a1, o0, o1, send_sems, recv_sems):
        x, y, c = _place()
        sibling = (x, y, 1 - c)
        chips = _other_chips(x, y)
        outs = (o0, o1)

        def copy(k, a, cx, cy, cc, to):
            blk = _half(outs[a], 2 * cx + cy, cc)
            return _remote(blk, blk, send_sems, recv_sems, k, to)

        first = [copy(3 * a + j, a, x, y, c, (*chip, c)) for a in range(n_arr) for j, chip in enumerate(chips)]
        for cp in first:
            cp.start()
        passed = [copy(3 * n_arr + 3 * a + j, a, *chip, c, sibling) for a in range(n_arr) for j, chip in enumerate(chips)]
        for a in range(n_arr):
            for j, chip in enumerate(chips):
                copy(3 * a + j, a, *chip, c, sibling).wait_recv()
                passed[3 * a + j].start()
        for a in range(n_arr):
            for j, chip in enumerate(chips):
                copy(3 * n_arr + 3 * a + j, a, *chip, 1 - c, sibling).wait_recv()
        for cp in first + passed:
            cp.wait_send()

    ins = (win0, wcw)
    return pl.pallas_call(
        body, name="gather_first",
        out_shape=tuple(jax.ShapeDtypeStruct(a.shape, a.dtype) for a in ins),
        in_specs=[ANY] * n_arr, out_specs=(ANY,) * n_arr,
        scratch_shapes=[pltpu.SemaphoreType.DMA((6 * n_arr,)), pltpu.SemaphoreType.DMA((6 * n_arr,))],
        input_output_aliases={0: 0, 1: 1},
        compiler_params=pltpu.CompilerParams(has_side_effects=True),
    )(*ins)


def _fwd_in(x, pre_g, win, l, ride=None):
    T, D = x.shape
    W = win.shape[-1]
    tm = min(512, T)

    def body(x_ref, g_ref, w_ref, p_ref, h_ref):
        hb = _rms(x_ref[...], g_ref[...]).astype(BF16)
        h_ref[...] = hb
        for j in range(4):
            p_ref[:, j * W:(j + 1) * W] = _dot(hb, w_ref[j]).astype(BF16)

    return _pcall(
        body, name="fwd_in",
        out_shape=(jax.ShapeDtypeStruct((T, 4 * W), BF16), jax.ShapeDtypeStruct((T, D), BF16)),
        grid=(T // tm,),
        in_specs=[pl.BlockSpec((tm, D), lambda i: (i, 0)),
                  pl.BlockSpec((None, 1, D), lambda i: (l, 0, 0)),
                  pl.BlockSpec((4, D, W), lambda i: (0, 0, 0), pipeline_mode=pl.Buffered(1))],
        out_specs=(pl.BlockSpec((tm, 4 * W), lambda i: (i, 0)),
                   pl.BlockSpec((tm, D), lambda i: (i, 0))),
        sem=("arbitrary",), args=(x, pre_g, win), ride=ride)


def _ret_specs(T, L):
    rope = pl.BlockSpec((3, L, 128), lambda s: (0, s, 0))
    mask = pl.BlockSpec((HEADS, L, L), lambda s: (0, 0, 0), pipeline_mode=pl.Buffered(1))
    qdm = pl.BlockSpec((HEADS, L, 128), lambda s: (0, 0, 0), pipeline_mode=pl.Buffered(1))
    small = pl.BlockSpec((HEADS, 8, 128), lambda s: (0, 0, 0))
    return [rope, mask, qdm, qdm, small, small]


QK = HEADS * DK
VW = HEADS * DV
PW = 2 * QK + 2 * VW


def _zero_at_start(ref):
    @pl.when(pl.program_id(0) == 0)
    def _():
        ref[...] = jnp.zeros_like(ref)


def _ret_fwd_part(p_ref, rope_ref, m_ref, qdm_ref, kdm_ref, mqs_ref, cd_ref, a_ref, st_ref, state):
    c, sl, sh = rope_ref[0], rope_ref[1], rope_ref[2]
    for j in range(HEADS // 2):
        rq = _rot(p_ref[:, 128 * j:128 * (j + 1)].astype(F32), c, sl, sh)
        rk = _rot(p_ref[:, QK + 128 * j:QK + 128 * (j + 1)].astype(F32), c, sl, sh)
        rkb = rk.astype(BF16)
        for e in range(2):
            h = 2 * j + e
            v = p_ref[:, 2 * QK + DV * h:2 * QK + DV * (h + 1)]
            g = p_ref[:, 2 * QK + VW + DV * h:2 * QK + VW + DV * (h + 1)].astype(F32)
            a = (rq * mqs_ref[h, 0:1, :]).astype(BF16)
            p = (_dot_nt(a, rkb) * m_ref[h]).astype(BF16)
            st = state[h]
            st_ref[h] = st
            o = _dot(p, v) + _dot((rq * qdm_ref[h]).astype(BF16), st.astype(BF16))
            state[h] = st * cd_ref[h, 0:1, :] + _dot_tn((rk * kdm_ref[h]).astype(BF16), v)
            a_ref[:, DV * h:DV * (h + 1)] = _gn_gate(o, g).astype(BF16)


def _shift_copies(src, sh):
    rows = sh.shape[1]
    for b in range(1, 8):
        sh[b - 1, :, :] = src[pl.ds(b, rows), :]


def _window(src, sh, r0, const, rows, lanes):
    b = const % 8
    at = pl.ds(pl.multiple_of(r0 + (const - b), 8), rows)
    w = src[at, lanes] if b == 0 else sh[b - 1, at, lanes]
    return w.reshape(rows // 8, 8, w.shape[-1])


def _conv_taps(wbuf, src, sh, r0, const, rows, lanes):
    groups = rows // 8
    accs = [None] * groups
    for k in range(CONV_K):
        w8 = wbuf[pl.ds(8 * k, 8), lanes]
        win = _window(src, sh, r0, const + k, rows, lanes)
        for g in range(groups):
            term = w8 * win[g]
            accs[g] = term if k == 0 else accs[g] + term
    return jnp.concatenate(accs, axis=0)


def _lane_parts(C):
    return [pl.ds(j * CONV_LANES, CONV_LANES) for j in range(C // CONV_LANES)]


def _load_conv_w(cw_ref, wbuf, flip):
    for k in range(CONV_K):
        row = jnp.concatenate([cw_ref[c, pl.ds(k, 1), :] for c in range(4)], axis=-1)
        kk = CONV_K - 1 - k if flip else k
        wbuf[pl.ds(8 * kk, 8), :] = jnp.broadcast_to(row, (8, row.shape[-1]))


def _conv_fwd_part(p_ref, cw_ref, cb_ref, lg_ref, lb_ref, a_ref, y_ref, ubuf, wbuf, ush):
    tc, C = y_ref.shape
    off = HALO - (CONV_K - 1)
    i = pl.program_id(0)

    @pl.when(i == 0)
    def _():
        ubuf[0:HALO, :] = jnp.zeros((HALO, C), F32)
        _load_conv_w(cw_ref, wbuf, False)

    @pl.when(i > 0)
    def _():
        ubuf[0:HALO, :] = ubuf[tc:tc + HALO, :]

    ga = p_ref[:, 0:C].astype(F32)
    gb = p_ref[:, C:2 * C].astype(F32)
    ubuf[HALO:HALO + tc, :] = ga * _sigmoid(gb)
    _shift_copies(ubuf, ush)

    def rows_block(r, carry):
        r0 = pl.multiple_of(r * CONV_RB, CONV_RB)
        for lanes in _lane_parts(C):
            y_ref[pl.ds(r0, CONV_RB), lanes] = _conv_taps(wbuf, ubuf, ush, r0, off, CONV_RB, lanes) + cb_ref[:, lanes]
        return carry

    lax.fori_loop(0, tc // CONV_RB, rows_block, 0)
    gc = p_ref[:, 2 * C:3 * C].astype(F32)
    a_ref[...] = _ln_gate(y_ref[...], gc, lg_ref[...], lb_ref[...]).astype(BF16)


def _ret_fwd(proj, tb, L, ride=None):
    T = proj.shape[0]
    nS = T // L

    def body(p_ref, rope_ref, m_ref, qdm_ref, kdm_ref, mqs_ref, cd_ref, ar_ref, st_ref, state):
        _zero_at_start(state)
        _ret_fwd_part(p_ref, rope_ref, m_ref, qdm_ref, kdm_ref, mqs_ref, cd_ref, ar_ref, st_ref, state)

    return _pcall(
        body, name="ret_fwd",
        out_shape=(jax.ShapeDtypeStruct((T, VW), BF16), jax.ShapeDtypeStruct((nS, HEADS, 128, DV), F32)),
        grid=(nS,),
        in_specs=[pl.BlockSpec((L, PW), lambda s: (s, 0))] + _ret_specs(T, L),
        out_specs=(pl.BlockSpec((L, VW), lambda s: (s, 0)), pl.BlockSpec((None, HEADS, 128, DV), lambda s: (s, 0, 0, 0))),
        scratch_shapes=[pltpu.VMEM((HEADS, 128, DV), F32)],
        sem=("arbitrary",), ride=ride,
        args=(proj, tb["rope"], tb["mask"], tb["qdm"], tb["kdm"], tb["mqs"], tb["cd"]))


def _conv_fwd(proj, wcw, conv_b, ln_g, ln_b, l, ride=None):
    T = proj.shape[0]
    C = conv_b.shape[-1]
    Cc = wcw.shape[-1]
    tc = min(CONV_TILE, T)
    assert PW == 3 * C

    def body(p_ref, cw_ref, cb_ref, lg_ref, lb_ref, ac_ref, y_ref, ubuf, wbuf, ush):
        _conv_fwd_part(p_ref, cw_ref, cb_ref, lg_ref, lb_ref, ac_ref, y_ref, ubuf, wbuf, ush)

    vec = pl.BlockSpec((None, 1, C), lambda i: (l, 0, 0))
    tile = pl.BlockSpec((tc, C), lambda i: (i, 0))
    return _pcall(
        body, name="conv_fwd",
        out_shape=(jax.ShapeDtypeStruct((T, C), BF16), jax.ShapeDtypeStruct((T, C), F32)),
        grid=(T // tc,),
        in_specs=[pl.BlockSpec((tc, 3 * C), lambda i: (i, 1)),
                  pl.BlockSpec((4, None, CONV_K, Cc), lambda i: (0, l, 0, 0)), vec, vec, vec],
        out_specs=(tile, tile),
        scratch_shapes=[pltpu.VMEM((HALO + tc, C), F32), pltpu.VMEM((8 * CONV_K, C), F32),
                        pltpu.VMEM((7, HALO + tc - 8, C), F32)],
        sem=("arbitrary",), ride=ride, args=(proj, wcw, conv_b, ln_g, ln_b))


def _merge_fwd(x, proj, a_ret, a_conv, wsq, post_g, l, ride=None):
    T, D = x.shape
    R = wsq.shape[1] // 3
    tm = min(512, T)

    def body(x_ref, p_ref, ar_ref, ac_ref, wro_ref, wco_ref, wo_ref, g_ref, xn_ref, ro_ref, co_ref, ym_ref, z_ref):
        ro = _dot(ar_ref[...], wro_ref[...].reshape(4 * R, D))
        co = _dot(ac_ref[...], wco_ref[...].reshape(4 * R, D))
        ym = (_sigmoid(p_ref[:, 0:D].astype(F32)) * ro + _sigmoid(p_ref[:, D:2 * D].astype(F32)) * co).astype(BF16)
        z = _dot(ym, wo_ref[...].reshape(4 * R, D))
        ro_ref[...] = ro.astype(BF16)
        co_ref[...] = co.astype(BF16)
        ym_ref[...] = ym
        z_ref[...] = z.astype(BF16)
        xn_ref[...] = x_ref[...] + _rms(z, g_ref[...])

    tile = pl.BlockSpec((tm, D), lambda i: (i, 0))
    wspec = lambda m: pl.BlockSpec((4, R, D), lambda i: (0, m, 0))
    act = jax.ShapeDtypeStruct((T, D), BF16)
    return _pcall(
        body, name="merge_fwd",
        out_shape=(jax.ShapeDtypeStruct((T, D), F32), act, act, act, act),
        grid=(T // tm,),
        in_specs=[tile, pl.BlockSpec((tm, 2 * D), lambda i: (i, 3)), tile, tile,
                  wspec(0), wspec(1), wspec(2), pl.BlockSpec((None, 1, D), lambda i: (l, 0, 0))],
        out_specs=(tile, tile, tile, tile, tile),
        sem=("arbitrary",), ride=ride, args=(x, proj, a_ret, a_conv, wsq, wsq, wsq, post_g))


def _loss_fwd_bwd(y, target):
    T, D = y.shape
    tm = min(512, T)

    def body(y_ref, t_ref, dy_ref, ls_ref):
        @pl.when(pl.program_id(0) == 0)
        def _():
            ls_ref[...] = jnp.zeros_like(ls_ref)

        e = y_ref[...] - t_ref[...]
        dy_ref[...] = e * (1.0 / D)
        ls_ref[...] += jnp.sum((e * e).reshape(tm // 8, 8, D), axis=0) * (0.5 / D)

    tile = pl.BlockSpec((tm, D), lambda i: (i, 0))
    return pl.pallas_call(
        body, name="loss",
        out_shape=(jax.ShapeDtypeStruct((T, D), F32), jax.ShapeDtypeStruct((8, D), F32)),
        grid=(T // tm,),
        in_specs=[tile, tile],
        out_specs=(tile, pl.BlockSpec((8, D), lambda i: (0, 0))),
        compiler_params=_cp(("arbitrary",)),
    )(y, target)


def _merge_bwd(dxn, proj, a_ret, a_conv, ro, co, ym, z, wsq, post_g, l, ride=None):
    T, D = dxn.shape
    R = wsq.shape[1] // 3
    tm = min(512, T)
    n = T // tm

    def body(dx_ref, p_ref, ar_ref, ac_ref, ro_ref, co_ref, ym_ref, z_ref, wro_ref, wco_ref, wo_ref, g_ref,
             dp_ref, dar_ref, dac_ref, gsq_ref, dg_ref, acc, stage):
        i = pl.program_id(0)

        @pl.when(i == 0)
        def _():
            acc[...] = jnp.zeros_like(acc)
            dg_ref[...] = jnp.zeros_like(dg_ref)

        _, vjp = jax.vjp(_rms, z_ref[...].astype(F32), g_ref[...])
        dz, dg = vjp(dx_ref[...])
        dg_ref[0:1, :] += dg
        dzb = dz.astype(BF16)
        dym = _dot_nt(dzb, wo_ref[...].reshape(4 * R, D))
        acc[2] += _dot_tn(ym_ref[...], dzb)
        sr = _sigmoid(p_ref[:, 0:D].astype(F32))
        sc = _sigmoid(p_ref[:, D:2 * D].astype(F32))
        rov = ro_ref[...].astype(F32)
        cov = co_ref[...].astype(F32)
        dp_ref[:, 0:D] = (dym * rov * sr * (1.0 - sr)).astype(BF16)
        dp_ref[:, D:2 * D] = (dym * cov * sc * (1.0 - sc)).astype(BF16)
        dro = (dym * sr).astype(BF16)
        dco = (dym * sc).astype(BF16)
        dar_ref[...] = _dot_nt(dro, wro_ref[...].reshape(4 * R, D)).astype(BF16)
        dac_ref[...] = _dot_nt(dco, wco_ref[...].reshape(4 * R, D)).astype(BF16)
        acc[0] += _dot_tn(ar_ref[...], dro)
        acc[1] += _dot_tn(ac_ref[...], dco)

        @pl.when(i == n - 1)
        def _():
            for m in range(3):
                stage[...] = acc[m].astype(BF16).reshape(4, R, D)
                pltpu.sync_copy(stage, gsq_ref.at[:, pl.ds(m * R, R), :])

    tile = pl.BlockSpec((tm, D), lambda i: (i, 0))
    wspec = lambda m: pl.BlockSpec((4, R, D), lambda i: (0, m, 0), pipeline_mode=pl.Buffered(1))
    return _pcall(
        body, name="merge_bwd",
        out_shape=(jax.ShapeDtypeStruct(proj.shape, BF16), jax.ShapeDtypeStruct((T, D), BF16),
                   jax.ShapeDtypeStruct((T, D), BF16), jax.ShapeDtypeStruct(wsq.shape, BF16),
                   jax.ShapeDtypeStruct((8, D), F32)),
        grid=(n,),
        in_specs=[tile, pl.BlockSpec((tm, 2 * D), lambda i: (i, 3)), tile, tile, tile, tile, tile, tile,
                  wspec(0), wspec(1), wspec(2), pl.BlockSpec((None, 1, D), lambda i: (l, 0, 0))],
        out_specs=(pl.BlockSpec((tm, 2 * D), lambda i: (i, 3)), tile, tile, ANY, pl.BlockSpec((8, D), lambda i: (0, 0))),
        scratch_shapes=[pltpu.VMEM((3, 4 * R, D), F32), pltpu.VMEM((4, R, D), BF16)],
        sem=("arbitrary",), args=(dxn, proj, a_ret, a_conv, ro, co, ym, z, wsq, wsq, wsq, post_g), ride=ride)


def _conv_bwd_part(n, da_ref, y_ref, p_ref, ph_ref, cw_ref, lg_ref, lb_ref, dp_ref,
                   dcbuf, ubuf, dubuf, wbuf, dwacc, vacc, dsh, ush):
    tc, C = y_ref.shape
    off = HALO - (CONV_K - 1)
    nrb = tc // CONV_RB
    t = pl.program_id(0)
    i = n - 1 - t

    @pl.when(t == 0)
    def _():
        dcbuf[tc:tc + HALO, :] = jnp.zeros((HALO, C), F32)
        dwacc[...] = jnp.zeros_like(dwacc)
        vacc[...] = jnp.zeros_like(vacc)
        _load_conv_w(cw_ref, wbuf, True)

    @pl.when(t > 0)
    def _():
        dcbuf[tc:tc + HALO, :] = dcbuf[0:HALO, :]

    gc = p_ref[:, 2 * C:3 * C].astype(F32)
    _, vjp = jax.vjp(_ln_gate, y_ref[...], gc, lg_ref[...], lb_ref[...])
    dy, dgc, dlg, dlb = vjp(da_ref[...].astype(F32))
    dcbuf[0:tc, :] = dy
    dp_ref[:, 2 * C:3 * C] = dgc.astype(BF16)
    vacc[0:1, :] += jnp.sum(dy, axis=0, keepdims=True)
    vacc[1:2, :] += dlg
    vacc[2:3, :] += dlb

    ga = p_ref[:, 0:C].astype(F32)
    sb = _sigmoid(p_ref[:, C:2 * C].astype(F32))
    ubuf[HALO:HALO + tc, :] = ga * sb
    uh = ph_ref[:, 0:C].astype(F32) * _sigmoid(ph_ref[:, C:2 * C].astype(F32))
    ubuf[0:HALO, :] = jnp.where(i > 0, uh, 0.0)

    _shift_copies(dcbuf, dsh)
    _shift_copies(ubuf, ush)
    def du_block(r, carry):
        r0 = pl.multiple_of(r * CONV_RB, CONV_RB)
        for lanes in _lane_parts(C):
            dubuf[pl.ds(r0, CONV_RB), lanes] = _conv_taps(wbuf, dcbuf, dsh, r0, 0, CONV_RB, lanes)
        return carry

    lax.fori_loop(0, nrb, du_block, 0)
    du = dubuf[...]
    dp_ref[:, 0:C] = (du * sb).astype(BF16)
    dp_ref[:, C:2 * C] = (du * ga * sb * (1.0 - sb)).astype(BF16)

    def dw_block(r, carry):
        r0 = pl.multiple_of(r * CONV_RB, CONV_RB)
        for lanes in _lane_parts(C):
            dyb = dcbuf[pl.ds(r0, CONV_RB), lanes].reshape(CONV_RB // 8, 8, CONV_LANES)
            for k in range(CONV_K):
                dwacc[8 * k:8 * k + 8, lanes] += jnp.sum(dyb * _window(ubuf, ush, r0, off + k, CONV_RB, lanes), axis=0)
        return carry

    lax.fori_loop(0, nrb, dw_block, 0)


def _conv_bwd_final(n, sg_ref, dwacc, vacc):
    C = sg_ref.shape[-1]

    @pl.when(pl.program_id(0) == n - 1)
    def _():
        for k in range(CONV_K):
            sg_ref[pl.ds(k, 1), :] = jnp.sum(dwacc[8 * k:8 * k + 8, :], axis=0, keepdims=True)
        sg_ref[pl.ds(CONV_K, 1), :] = jnp.zeros((1, C), F32)
        sg_ref[ROW_CB:ROW_CB + 8, :] = jnp.zeros((8, C), F32)
        sg_ref[ROW_CB:ROW_CB + 3, :] = vacc[0:3, :]


def _ret_bwd_part(da_ref, p_ref, st_ref, rope_ref, m_ref, qdm_ref, kdm_ref, mqs_ref, cd_ref, dp_ref, gst):
    c, sl, sh = rope_ref[0], rope_ref[1], rope_ref[2]
    for j in range(HEADS // 2):
        rq = _rot(p_ref[:, 128 * j:128 * (j + 1)].astype(F32), c, sl, sh)
        rk = _rot(p_ref[:, QK + 128 * j:QK + 128 * (j + 1)].astype(F32), c, sl, sh)
        rkb = rk.astype(BF16)
        drq = jnp.zeros_like(rq)
        drk = jnp.zeros_like(rk)
        for e in range(2):
            h = 2 * j + e
            v = p_ref[:, 2 * QK + DV * h:2 * QK + DV * (h + 1)]
            g = p_ref[:, 2 * QK + VW + DV * h:2 * QK + VW + DV * (h + 1)].astype(F32)
            mqs = mqs_ref[h, 0:1, :]
            a = (rq * mqs).astype(BF16)
            aq = (rq * qdm_ref[h]).astype(BF16)
            kdv = (rk * kdm_ref[h]).astype(BF16)
            mk = m_ref[h]
            p = (_dot_nt(a, rkb) * mk).astype(BF16)
            stb = st_ref[h].astype(BF16)
            o = _dot(p, v) + _dot(aq, stb)
            _, vjp = jax.vjp(_gn_gate, o, g)
            do, dg = vjp(da_ref[:, DV * h:DV * (h + 1)].astype(F32))
            dob = do.astype(BF16)
            gs = gst[h]
            gsb = gs.astype(BF16)
            ds = (_dot_nt(dob, v) * mk).astype(BF16)
            drq = drq + _dot(ds, rkb) * mqs + _dot_nt(dob, stb) * qdm_ref[h]
            drk = drk + _dot_tn(ds, a) + _dot_nt(v, gsb) * kdm_ref[h]
            dv = _dot_tn(p, dob) + _dot(kdv, gsb)
            gst[h] = _dot_tn(aq, dob) + gs * cd_ref[h, 0:1, :]
            dp_ref[:, 2 * QK + DV * h:2 * QK + DV * (h + 1)] = dv.astype(BF16)
            dp_ref[:, 2 * QK + VW + DV * h:2 * QK + VW + DV * (h + 1)] = dg.astype(BF16)
        dp_ref[:, 128 * j:128 * (j + 1)] = _rot_t(drq, c, sl, sh).astype(BF16)
        dp_ref[:, QK + 128 * j:QK + 128 * (j + 1)] = _rot_t(drk, c, sl, sh).astype(BF16)


def _ret_bwd(dproj, da_ret, proj, states, tb, L, ride=None):
    T = proj.shape[0]
    nS = T // L

    def body(dpin_ref, dar_ref, p_ref, st_ref, rope_ref, m_ref, qdm_ref, kdm_ref, mqs_ref, cd_ref, dp_ref, gst):
        _zero_at_start(gst)
        _ret_bwd_part(dar_ref, p_ref, st_ref, rope_ref, m_ref, qdm_ref, kdm_ref, mqs_ref, cd_ref, dp_ref, gst)

    rev = lambda s: nS - 1 - s
    specs = _ret_specs(T, L)
    specs[0] = pl.BlockSpec((3, L, 128), lambda s: (0, rev(s), 0))
    ptile = pl.BlockSpec((L, PW), lambda s: (rev(s), 0))
    return _pcall(
        body, name="ret_bwd",
        out_shape=(jax.ShapeDtypeStruct(dproj.shape, BF16),),
        grid=(nS,),
        in_specs=[ANY, pl.BlockSpec((L, VW), lambda s: (rev(s), 0)), ptile,
                  pl.BlockSpec((None, HEADS, 128, DV), lambda s: (rev(s), 0, 0, 0))] + specs,
        out_specs=(ptile,),
        scratch_shapes=[pltpu.VMEM((HEADS, 128, DV), F32)],
        sem=("arbitrary",), aliases={0: 0}, ride=ride,
        args=(dproj, da_ret, proj, states, tb["rope"], tb["mask"], tb["qdm"], tb["kdm"], tb["mqs"], tb["cd"]))


def _conv_bwd(dproj, da_conv, y, proj, wcw, ln_g, ln_b, l, ride=None):
    T, C = y.shape
    Cc = wcw.shape[-1]
    tc = min(CONV_TILE, T)
    n = T // tc
    hb = tc // HALO

    def body(dpin_ref, dac_ref, y_ref, p_ref, ph_ref, cw_ref, lg_ref, lb_ref, dp_ref, sg_ref,
             dcbuf, ubuf, dubuf, wbuf, dwacc, vacc, dsh, ush):
        _conv_bwd_part(n, dac_ref, y_ref, p_ref, ph_ref, cw_ref, lg_ref, lb_ref, dp_ref,
                       dcbuf, ubuf, dubuf, wbuf, dwacc, vacc, dsh, ush)
        _conv_bwd_final(n, sg_ref, dwacc, vacc)

    rev = lambda t: n - 1 - t
    vec = pl.BlockSpec((None, 1, C), lambda t: (l, 0, 0))
    tile = pl.BlockSpec((tc, C), lambda t: (rev(t), 0))
    ptile = pl.BlockSpec((tc, 3 * C), lambda t: (rev(t), 1))
    halo = pl.BlockSpec((HALO, 3 * C), lambda t: (jnp.maximum(rev(t) * hb - 1, 0), 1))
    return _pcall(
        body, name="conv_bwd",
        out_shape=(jax.ShapeDtypeStruct(dproj.shape, BF16), jax.ShapeDtypeStruct((ROW_PRE, C), F32)),
        grid=(n,),
        in_specs=[ANY, tile, tile, ptile, halo, pl.BlockSpec((4, None, CONV_K, Cc), lambda t: (0, l, 0, 0)), vec, vec],
        out_specs=(ptile, pl.BlockSpec((ROW_PRE, C), lambda t: (0, 0))),
        scratch_shapes=[pltpu.VMEM((tc + HALO, C), F32), pltpu.VMEM((HALO + tc, C), F32), pltpu.VMEM((tc, C), F32),
                        pltpu.VMEM((8 * CONV_K, C), F32), pltpu.VMEM((8 * CONV_K, C), F32), pltpu.VMEM((8, C), F32),
                        pltpu.VMEM((7, HALO + tc - 8, C), F32), pltpu.VMEM((7, HALO + tc - 8, C), F32)],
        sem=("arbitrary",), aliases={0: 0}, ride=ride, args=(dproj, da_conv, y, proj, proj, wcw, ln_g, ln_b))


def _win_grad(h, dproj, W, ride=None):
    T, D = h.shape
    tk = min(2048, T)
    nk = T // tk

    def body(h_ref, dp_ref, g_ref, acc):
        k = pl.program_id(1)

        @pl.when(k == 0)
        def _():
            acc[...] = jnp.zeros_like(acc)

        acc[...] += _dot_tn(h_ref[...], dp_ref[...])

        @pl.when(k == nk - 1)
        def _():
            g_ref[...] = acc[...].astype(BF16)

    return _pcall(
        body, name="win_grad",
        out_shape=(jax.ShapeDtypeStruct((4, D, W), BF16),),
        grid=(4, nk),
        in_specs=[pl.BlockSpec((tk, D), lambda j, k: (k, 0)), pl.BlockSpec((tk, W), lambda j, k: (k, j))],
        out_specs=(pl.BlockSpec((None, D, W), lambda j, k: (j, 0, 0)),),
        scratch_shapes=[pltpu.VMEM((D, W), F32)],
        sem=("arbitrary", "arbitrary"), args=(h, dproj), ride=ride)


def _in_bwd(dxn, dproj, x, pre_g, win, l, ride=None):
    T, D = x.shape
    W = win.shape[-1]
    tm = min(512, T)

    def body(dxn_ref, dp_ref, x_ref, g_ref, w_ref, dx_ref, dg_ref):
        @pl.when(pl.program_id(0) == 0)
        def _():
            dg_ref[...] = jnp.zeros_like(dg_ref)

        dh = _dot_nt(dp_ref[:, 0:W], w_ref[0])
        for j in range(1, 4):
            dh = dh + _dot_nt(dp_ref[:, j * W:(j + 1) * W], w_ref[j])
        _, vjp = jax.vjp(_rms, x_ref[...], g_ref[...])
        dx, dg = vjp(dh)
        dx_ref[...] = dxn_ref[...] + dx
        dg_ref[0:1, :] += dg

    tile = pl.BlockSpec((tm, D), lambda i: (i, 0))
    return _pcall(
        body, name="in_bwd",
        out_shape=(jax.ShapeDtypeStruct((T, D), F32), jax.ShapeDtypeStruct((8, D), F32)),
        grid=(T // tm,),
        in_specs=[tile, pl.BlockSpec((tm, 4 * W), lambda i: (i, 0)), tile,
                  pl.BlockSpec((None, 1, D), lambda i: (l, 0, 0)),
                  pl.BlockSpec((4, D, W), lambda i: (0, 0, 0), pipeline_mode=pl.Buffered(1))],
        out_specs=(tile, pl.BlockSpec((8, D), lambda i: (0, 0))),
        sem=("arbitrary",), args=(dxn, dproj, x, pre_g, win), ride=ride)


def _sum_group(chip, t, u):
    _, A, B = t.shape
    tr = min(256, A)

    def body(k_ref, t_ref, u_ref, o_ref):
        o_ref[...] = ((t_ref[...].astype(F32) + u_ref[0].astype(F32)) + u_ref[1].astype(F32)) + u_ref[2].astype(F32)

    return pl.pallas_call(
        body, name="sum_group",
        out_shape=jax.ShapeDtypeStruct((A, B), F32),
        grid_spec=pltpu.PrefetchScalarGridSpec(
            num_scalar_prefetch=1, grid=(A // tr,),
            in_specs=[pl.BlockSpec((None, tr, B), lambda i, k: (k[0], i, 0)),
                      pl.BlockSpec((3, tr, B), lambda i, k: (0, i, 0))],
            out_specs=pl.BlockSpec((tr, B), lambda i, k: (i, 0))),
        compiler_params=_cp(("arbitrary",)),
    )(chip, t, u)


def _swap_rows(g):
    _, A, B = g.shape
    nh = A // 2

    def body(g_ref, r_ref, send_sems, recv_sems):
        x, y, c = _place()
        cp = _remote(g_ref.at[:, pl.ds((1 - c) * nh, nh)], r_ref, send_sems, recv_sems, 0, (x, y, 1 - c))
        cp.start()
        cp.wait()

    return pl.pallas_call(
        body, name="swap_rows",
        out_shape=jax.ShapeDtypeStruct((4, nh, B), g.dtype),
        in_specs=[ANY], out_specs=ANY,
        scratch_shapes=[pltpu.SemaphoreType.DMA((1,)), pltpu.SemaphoreType.DMA((1,))],
        compiler_params=pltpu.CompilerParams(has_side_effects=True),
    )(g)


def _add_rows(cidx, g, r):
    _, nh, B = r.shape
    tr = min(256, nh)
    nb = nh // tr

    def body(c_ref, g_ref, r_ref, o_ref):
        o_ref[...] = (g_ref[...].astype(F32) + r_ref[...].astype(F32)).astype(BF16)

    blk = (None, tr, B)
    return pl.pallas_call(
        body, name="add_rows",
        out_shape=jax.ShapeDtypeStruct(r.shape, BF16),
        grid_spec=pltpu.PrefetchScalarGridSpec(
            num_scalar_prefetch=1, grid=(4, nb),
            in_specs=[pl.BlockSpec(blk, lambda k, i, c: (k, c[0] * nb + i, 0)),
                      pl.BlockSpec(blk, lambda k, i, c: (k, i, 0))],
            out_specs=pl.BlockSpec(blk, lambda k, i, c: (k, i, 0))),
        compiler_params=_cp(("arbitrary", "arbitrary")),
    )(cidx, g, r)


def _sum_group_half(chip, cidx, t, u):
    _, nh, B = t.shape
    tr = min(256, nh)
    nb = nh // tr

    def body(k_ref, c_ref, t_ref, u_ref, o_ref):
        mine = (pl.program_id(0) // nb) == c_ref[0]

        @pl.when(mine)
        def _():
            o_ref[...] = ((t_ref[...].astype(F32) + u_ref[0].astype(F32)) + u_ref[1].astype(F32)) + u_ref[2].astype(F32)

        @pl.when(jnp.logical_not(mine))
        def _():
            o_ref[...] = jnp.zeros_like(o_ref)

    own = lambda i, c: jnp.clip(i - c[0] * nb, 0, nb - 1)
    return pl.pallas_call(
        body, name="sum_group_half",
        out_shape=jax.ShapeDtypeStruct((2 * nh, B), F32),
        grid_spec=pltpu.PrefetchScalarGridSpec(
            num_scalar_prefetch=2, grid=(2 * nb,),
            in_specs=[pl.BlockSpec((None, tr, B), lambda i, k, c: (k[0], own(i, c), 0)),
                      pl.BlockSpec((3, tr, B), lambda i, k, c: (0, own(i, c), 0))],
            out_specs=pl.BlockSpec((tr, B), lambda i, k, c: (i, 0))),
        compiler_params=_cp(("arbitrary",)),
    )(chip, cidx, t, u)


def _adam_math(w, g, m, v):
    c1 = 1.0 / (1.0 - ADAM_B1 ** ADAM_STEP)
    c2 = 1.0 / (1.0 - ADAM_B2 ** ADAM_STEP)
    nm = ADAM_B1 * m + (1.0 - ADAM_B1) * g
    nv = ADAM_B2 * v + (1.0 - ADAM_B2) * (g * g)
    return -ADAM_LR * ((nm * c1) / (jnp.sqrt(nv * c2) + ADAM_EPS) + ADAM_WD * w), nm, nv


def _adamw_layer(prev, w, m, v, sa, sb, l, part, ride=None):
    NL, A, B = w.shape
    tr = A
    while tr * B * 4 > ADAM_BLOCK_BYTES and tr % 16 == 0:
        tr //= 2
    nb = A // tr

    def body(p0, p1, p2, p3, w_ref, m_ref, v_ref, sa_ref, sb_ref, g_ref, d_ref, nm_ref, nv_ref):
        g = sa_ref[...] + sb_ref[...]
        g_ref[...] = g
        d_ref[...], nm_ref[...], nv_ref[...] = _adam_math(w_ref[...], g, m_ref[...], v_ref[...])

    lay = pl.BlockSpec((None, tr, B), lambda i: (l, i, 0))
    src = pl.BlockSpec((tr, B), lambda i: (part * nb + i, 0))
    full = jax.ShapeDtypeStruct((NL, A, B), F32)
    if prev is None:
        prev = tuple(lax.empty((NL, A, B), F32) for _ in range(4))
    outs, landed = _pcall(
        body, name="adamw_layer",
        out_shape=(full,) * 4, grid=(nb,),
        in_specs=[ANY] * 4 + [lay, lay, lay, src, src], out_specs=(lay,) * 4,
        sem=("arbitrary",), aliases={0: 0, 1: 1, 2: 2, 3: 3}, args=(*prev, w, m, v, sa, sb), ride=ride)
    return tuple(outs), landed


def _adamw(w, g, m, v):
    shape = w.shape
    cols = shape[-1]
    rows = int(np.prod(shape[:-1]))

    def body(w_ref, g_ref, m_ref, v_ref, d_ref, nm_ref, nv_ref):
        d_ref[...], nm_ref[...], nv_ref[...] = _adam_math(w_ref[...], g_ref[...], m_ref[...], v_ref[...])

    tile = pl.BlockSpec((rows, cols), lambda i: (0, 0))
    out = jax.ShapeDtypeStruct((rows, cols), F32)
    res = pl.pallas_call(
        body, name="adamw",
        out_shape=(out, out, out), grid=(1,),
        in_specs=[tile] * 4, out_specs=(tile,) * 3,
        compiler_params=_cp(("arbitrary",)),
    )(*[a.reshape(rows, cols) for a in (w, g, m, v)])
    return tuple(a.reshape(shape) for a in res)


def _tail_exchange(small, s_in, s_sq):
    def body(s_ref, a_ref, b_ref, o_ref, oa_ref, ob_ref, send_sems, recv_sems, local_sem):
        x, y, c = _place()
        me = 4 * x + 2 * y + c
        sibling = (x, y, 1 - c)
        mine = pltpu.make_async_copy(s_ref, o_ref.at[me], local_sem)
        mine.start()
        swaps = [_remote(a_ref, oa_ref, send_sems, recv_sems, 7, sibling), _remote(b_ref, ob_ref, send_sems, recv_sems, 8, sibling)]
        sends = [_remote(s_ref, o_ref.at[me], send_sems, recv_sems, r, peer) for r, peer in enumerate(_peers(x, y, c))]
        for cp in swaps + sends:
            cp.start()
        for r, peer in enumerate(_peers(x, y, c)):
            theirs = o_ref.at[4 * peer[0] + 2 * peer[1] + peer[2]]
            _remote(theirs, theirs, send_sems, recv_sems, r, peer).wait_recv()
        for cp in sends:
            cp.wait_send()
        for cp in swaps:
            cp.wait()
        mine.wait()

    return pl.pallas_call(
        body, name="tail_exchange",
        out_shape=(jax.ShapeDtypeStruct((8,) + small.shape, small.dtype),
                   jax.ShapeDtypeStruct(s_in.shape, s_in.dtype), jax.ShapeDtypeStruct(s_sq.shape, s_sq.dtype)),
        in_specs=[ANY] * 3, out_specs=(ANY,) * 3,
        scratch_shapes=[pltpu.SemaphoreType.DMA((9,)), pltpu.SemaphoreType.DMA((9,)), pltpu.SemaphoreType.DMA],
        compiler_params=pltpu.CompilerParams(has_side_effects=True),
    )(small, s_in, s_sq)


def _sum_devices(gs):
    NL = len(gs)
    _, R, D = gs[0].shape

    def body(*refs):
        o_ref = refs[NL]
        for l in range(NL):
            acc = refs[l][0]
            for k in range(1, 8):
                acc = acc + refs[l][k]
            o_ref[l] = acc

    return pl.pallas_call(
        body, name="sum_devices",
        out_shape=jax.ShapeDtypeStruct((NL, R, D), F32),
        grid=(1,),
        in_specs=[pl.BlockSpec((8, R, D), lambda i: (0, 0, 0))] * NL,
        out_specs=pl.BlockSpec((NL, R, D), lambda i: (0, 0, 0)),
        compiler_params=_cp(("arbitrary",)),
    )(*gs)


def kernel(x, pre_norm_g, w_in, w_ret_out, conv_w, conv_b, conv_ln_g, conv_ln_b, w_conv_out, w_o, post_norm_g, loss_target, m_pre_norm_g, m_w_in, m_w_ret_out, m_conv_w, m_conv_b, m_conv_ln_g, m_conv_ln_b, m_w_conv_out, m_w_o, m_post_norm_g, v_pre_norm_g, v_w_in, v_w_ret_out, v_conv_w, v_conv_b, v_conv_ln_g, v_conv_ln_b, v_w_conv_out, v_w_o, v_post_norm_g):
    NL, D, W = w_in.shape
    Cc = conv_w.shape[-1]
    T = x.shape[1]
    L = min(RET_BLOCK, T)
    tb = _tables(T, L)
    ax, ay, ac = _place()
    chip = (2 * ax + ay).astype(jnp.int32).reshape(1)
    cidx = ac.astype(jnp.int32).reshape(1)
    pre_g, cb, lg, lb, post_g = (a.reshape(NL, 1, D) for a in (pre_norm_g, conv_b, conv_ln_g, conv_ln_b, post_norm_g))

    win = [_cast_win(chip, w_in, l) for l in range(NL)]
    wsq = [_cast_wsq(chip, w_ret_out, w_conv_out, w_o, l) for l in range(NL)]
    win[0], wcw = _gather_first(win[0], _place_cw(chip, conv_w))

    saved = []
    xl = x[0]
    for l in range(NL):
        more = l + 1 < NL
        ride = _Ride()
        if more:
            _ride_gather_ici(ride, "win", win[l + 1], (0, 5, 8))
        (proj, h), got = _fwd_in(xl, pre_g, win[l], l, ride=ride)
        if more:
            win[l + 1] = got["win"]
        ride = _Ride()
        if l == 0:
            _ride_gather_ici(ride, "wsq0", wsq[0])
        (a_ret, states), got = _ret_fwd(proj, tb, L, ride=ride)
        if l == 0:
            wsq[0] = got["wsq0"]
        ride = _Ride()
        if more:
            _ride_gather_ici(ride, "win", win[l + 1], (5, 8, 8))
            _ride_gather_ici(ride, "wsq", wsq[l + 1])
        if l == 0:
            _ride_gather_pass(ride, "wsq0", wsq[0])
        (a_conv, y), got = _conv_fwd(proj, wcw, cb, lg, lb, l, ride=ride)
        if more:
            win[l + 1], wsq[l + 1] = got["win"], got["wsq"]
        if l == 0:
            wsq[0] = got["wsq0"]
        ride = _Ride()
        if more:
            _ride_gather_pass(ride, "win", win[l + 1])
            _ride_gather_pass(ride, "wsq", wsq[l + 1])
        (xn, ro, co, ym, z), got = _merge_fwd(xl, proj, a_ret, a_conv, wsq[l], post_g, l, ride=ride)
        if more:
            win[l + 1], wsq[l + 1] = got["win"], got["wsq"]
        saved.append((xl, proj, h, a_ret, states, a_conv, y, ro, co, ym, z))
        xl = xn
    dx, lsum = _loss_fwd_bwd(xl, loss_target[0])
    loss = lax.psum(jnp.sum(lsum), ("x", "y", "c"))

    gin, gsq, uin, usq = [None] * NL, [None] * NL, [None] * NL, [None] * NL
    s_in, s_sq, o_in, o_sq = [None] * NL, [None] * NL, [None] * NL, [None] * NL
    small, gs = [None] * NL, [None] * NL
    for l in reversed(range(NL)):
        xin, proj, h, a_ret, states, a_conv, y, ro, co, ym, z = saved[l]
        (dproj, da_ret, da_conv, gsq[l], dpost), _ = _merge_bwd(dx, proj, a_ret, a_conv, ro, co, ym, z, wsq[l], post_g, l)
        ride = _Ride()
        if l + 1 < NL:
            _ride_exchange(ride, "gin", gin[l + 1], "uin", uin[l + 1],
                           [(2, FIRST_HALF), (0, SECOND_HALF), (1, SECOND_HALF)])
        (dproj, sg), got = _conv_bwd(dproj, da_conv, y, proj, wcw, lg, lb, l, ride=ride)
        if l + 1 < NL:
            uin[l + 1] = got["uin"]
        ride = _Ride()
        if l + 1 < NL:
            _ride_exchange(ride, "gin", gin[l + 1], "uin", uin[l + 1], [(2, SECOND_HALF)])
            _ride_gather_all(ride, "small", small[l + 1], "gs")
        (dproj,), got = _ret_bwd(dproj, da_ret, proj, states, tb, L, ride=ride)
        if l + 1 < NL:
            uin[l + 1], gs[l + 1] = got["uin"], got["gs"]
        ride = _Ride()
        _ride_exchange(ride, "gsq", gsq[l], "usq", None, [(0, WHOLE), (1, WHOLE), (2, WHOLE)])
        (gin[l],), got = _win_grad(h, dproj, W, ride=ride)
        usq[l] = got["usq"]
        ride = _Ride()
        if l > 0:
            _ride_exchange(ride, "gin", gin[l], "uin", None, [(0, FIRST_HALF), (1, FIRST_HALF)])
        else:
            gin[0] = _add_rows(cidx, gin[0], _swap_rows(gin[0]))
            _ride_exchange(ride, "gin", gin[0], "uin", None, [(0, WHOLE), (1, WHOLE), (2, WHOLE)])
        if l + 1 < NL:
            s_in[l + 1] = _sum_group(chip, gin[l + 1], uin[l + 1])
            s_sq[l + 1] = _sum_group(chip, gsq[l + 1], usq[l + 1])
            _ride_swap(ride, "s_in", s_in[l + 1], "o_in")
            _ride_swap(ride, "s_sq", s_sq[l + 1], "o_sq")
        (dx, dpre), got = _in_bwd(dx, dproj, xin, pre_g, win[l], l, ride=ride)
        uin[l] = got["uin"]
        if l + 1 < NL:
            o_in[l + 1], o_sq[l + 1] = got["o_in"], got["o_sq"]
        small[l] = jnp.concatenate([sg, dpre, dpost], axis=0)
    grad_x = dx

    big = {"w_in": None, "w_ret_out": None, "w_conv_out": None, "w_o": None}
    wts = {"w_in": (w_in, m_w_in, v_w_in), "w_ret_out": (w_ret_out, m_w_ret_out, v_w_ret_out),
           "w_conv_out": (w_conv_out, m_w_conv_out, v_w_conv_out), "w_o": (w_o, m_w_o, v_w_o)}
    sq_names = ("w_ret_out", "w_conv_out", "w_o")

    def adam_in(l, ride=None):
        big["w_in"], got = _adamw_layer(big["w_in"], *wts["w_in"], s_in[l], o_in[l], l, 0, ride=ride)
        return got

    def adam_sq(l, part, ride=None):
        n = sq_names[part]
        big[n], got = _adamw_layer(big[n], *wts[n], s_sq[l], o_sq[l], l, part, ride=ride)
        return got

    s_in[0] = _sum_group_half(chip, cidx, gin[0], uin[0])
    s_sq[0] = _sum_group(chip, gsq[0], usq[0])
    gs[0], o_in[0], o_sq[0] = _tail_exchange(small[0], s_in[0], s_sq[0])
    for l in reversed(range(NL)):
        adam_in(l)
        for part in range(3):
            adam_sq(l, part)

    gsm = _sum_devices(gs)
    grads = {
        "pre_norm_g": gsm[:, ROW_PRE], "conv_w": lax.dynamic_slice_in_dim(gsm[:, 0:CONV_K], chip[0] * Cc, Cc, axis=2),
        "conv_b": gsm[:, ROW_CB], "conv_ln_g": gsm[:, ROW_LG], "conv_ln_b": gsm[:, ROW_LB], "post_norm_g": gsm[:, ROW_POST],
    }
    weights = dict(pre_norm_g=pre_norm_g, conv_w=conv_w, conv_b=conv_b, conv_ln_g=conv_ln_g, conv_ln_b=conv_ln_b,
                   post_norm_g=post_norm_g)
    m1 = dict(pre_norm_g=m_pre_norm_g, conv_w=m_conv_w, conv_b=m_conv_b, conv_ln_g=m_conv_ln_g, conv_ln_b=m_conv_ln_b,
              post_norm_g=m_post_norm_g)
    m2 = dict(pre_norm_g=v_pre_norm_g, conv_w=v_conv_w, conv_b=v_conv_b, conv_ln_g=v_conv_ln_g, conv_ln_b=v_conv_ln_b,
              post_norm_g=v_post_norm_g)
    res = {n: (grads[n],) + _adamw(weights[n], grads[n], m1[n], m2[n]) for n in grads}
    res.update(big)
    order = ["pre_norm_g", "w_in", "w_ret_out", "conv_w", "conv_b", "conv_ln_g", "conv_ln_b", "w_conv_out", "w_o", "post_norm_g"]
    return (loss, grad_x[None], *[res[n][0] for n in order], *[res[n][1] for n in order],
            *[res[n][2] for n in order], *[res[n][3] for n in order])
```

```python
import numpy as np
import jax
import jax.numpy as jnp
from jax import lax
from jax.experimental import pallas as pl
from jax.experimental.pallas import tpu as pltpu

F32 = jnp.float32
BF16 = jnp.bfloat16

HEADS = 8
DK = 64
DV = 128
CONV_K = 31
CHUNK = 64
ROPE_BASE = 10000.0
EPS = 1e-6
HALO = 32
CONV_RB = 32
CONV_LANES = 512
CONV_TILE = 256
RET_BLOCK = 512

ADAM_LR = 0.001
ADAM_B1 = 0.9
ADAM_B2 = 0.999
ADAM_EPS = 1e-08
ADAM_WD = 0.01
ADAM_STEP = 10
ADAM_BLOCK_BYTES = 2 * 1024 * 1024

VMEM_LIMIT = 56 * 1024 * 1024
MESH_T = pl.DeviceIdType.MESH
ANY = pl.BlockSpec(memory_space=pl.ANY)

ROW_CB, ROW_LG, ROW_LB = 32, 33, 34
ROW_PRE, ROW_POST, ROW_LOSS = 40, 48, 56


def _cp(sem=None, **kw):
    return pltpu.CompilerParams(dimension_semantics=sem, vmem_limit_bytes=VMEM_LIMIT, **kw)


def _dot(a, b):
    return jnp.dot(a, b, preferred_element_type=F32)


def _dot_nt(a, b):
    return lax.dot_general(a, b, (((1,), (1,)), ((), ())), preferred_element_type=F32)


def _dot_tn(a, b):
    return lax.dot_general(a, b, (((0,), (0,)), ((), ())), preferred_element_type=F32)


def _sigmoid(x):
    return jax.nn.sigmoid(x)


def _silu(x):
    return x * _sigmoid(x)


def _rms(x, g):
    return x * lax.rsqrt(jnp.mean(x * x, axis=-1, keepdims=True) + EPS) * g


def _gn_gate(o, g):
    mu = jnp.mean(o, axis=-1, keepdims=True)
    d = o - mu
    var = jnp.mean(d * d, axis=-1, keepdims=True)
    return d * lax.rsqrt(var + EPS) * _silu(g)


def _ln_gate(y, gc, lg, lb):
    mu = jnp.mean(y, axis=-1, keepdims=True)
    d = y - mu
    var = jnp.mean(d * d, axis=-1, keepdims=True)
    return _silu(d * lax.rsqrt(var + EPS) * lg + lb) * _silu(gc)


def _tables(T, L):
    lane = np.arange(128)
    d = lane % DK
    half = DK // 2
    inv = (ROPE_BASE ** (-(np.arange(half, dtype=np.float32)) / half)).astype(np.float32)
    ang = (np.arange(T, dtype=np.float32)[:, None] * inv[None, :]).astype(np.float64)
    angl = ang[:, d % half]
    cos = np.cos(angl)
    sin = np.sin(angl)
    lo = (d < half)[None, :]
    rope = np.stack([cos, np.where(lo, -sin, 0.0), np.where(lo, 0.0, sin)]).astype(np.float32)

    hh = np.arange(HEADS, dtype=np.float64)
    log_g = np.log1p(-np.exp2(-5.0 - hh))
    n = np.arange(L, dtype=np.float64)
    cn = np.arange(L) // CHUNK
    allowed = (cn[None, :] <= cn[:, None])
    dist = np.abs(n[:, None] - n[None, :])
    mask = np.exp(log_g[:, None, None] * dist[None]) * allowed[None]
    mq = ((lane[None, :] // DK) == (np.arange(HEADS)[:, None] % 2)).astype(np.float64)
    qd = np.exp(log_g[:, None] * n[None, :])
    kd = np.exp(log_g[:, None] * (L - n[None, :]))
    qdm = qd[:, :, None] * mq[:, None, :] * (DK ** -0.5)
    kdm = kd[:, :, None] * mq[:, None, :]
    mqs = np.broadcast_to((mq * (DK ** -0.5))[:, None, :], (HEADS, 8, 128))
    cd = np.broadcast_to(np.exp(log_g * L)[:, None, None], (HEADS, 8, 128))
    f = lambda a: jnp.asarray(np.ascontiguousarray(a), dtype=F32)
    return dict(rope=f(rope), mask=f(mask), qdm=f(qdm), kdm=f(kdm), mqs=f(mqs), cd=f(cd))


def _rot(b, c, sl, sh):
    return b * c + pltpu.roll(b, 96, axis=1) * sl + pltpu.roll(b, 32, axis=1) * sh


def _rot_t(d, c, sl, sh):
    return d * c + pltpu.roll(d * sl, 32, axis=1) + pltpu.roll(d * sh, 96, axis=1)


def _place():
    return lax.axis_index("x"), lax.axis_index("y"), lax.axis_index("c")


def _other_chips(x, y):
    return [(1 - x, y), (x, 1 - y), (1 - x, 1 - y)]


def _remote(src, dst, send_sems, recv_sems, k, to):
    return pltpu.make_async_remote_copy(src_ref=src, dst_ref=dst, send_sem=send_sems.at[k], recv_sem=recv_sems.at[k],
                                        device_id=to, device_id_type=MESH_T)


class _Ride:
    def __init__(self):
        self.arrays, self.kinds, self.names = [], [], []
        self.fresh = []
        self.ops = []

    def read(self, name, a):
        self.names.append(name)
        self.arrays.append(a)
        self.kinds.append("in")

    def inout(self, name, a):
        self.names.append(name)
        self.arrays.append(a)
        self.kinds.append("inout")

    def land(self, name, shape, dtype):
        self.fresh.append((name, jax.ShapeDtypeStruct(shape, dtype)))

    def op(self, n_sems, start, finish):
        self.ops.append((n_sems, start, finish))


def _pcall(body, *, name, grid, in_specs, out_specs, out_shape, args, scratch_shapes=(), sem, aliases=None, ride=None):
    if ride is None or not ride.ops:
        outs = pl.pallas_call(body, name=name, grid=grid, in_specs=list(in_specs), out_specs=tuple(out_specs),
                              out_shape=tuple(out_shape), scratch_shapes=list(scratch_shapes),
                              input_output_aliases=dict(aliases or {}), compiler_params=_cp(sem))(*args)
        return outs, {}

    ni, no, nr = len(args), len(out_shape), len(ride.arrays)
    inout = [i for i, k in enumerate(ride.kinds) if k == "inout"]
    r_out_shapes = [jax.ShapeDtypeStruct(ride.arrays[i].shape, ride.arrays[i].dtype) for i in inout] + [s for _, s in ride.fresh]
    r_out_names = [ride.names[i] for i in inout] + [n for n, _ in ride.fresh]
    nro = len(r_out_shapes)
    n_sems = sum(n for n, _, _ in ride.ops)
    n_scr = len(scratch_shapes)
    nd = len(grid)

    def wrapped(*refs):
        ins, rin = refs[:ni], refs[ni:ni + nr]
        outs, rout = refs[ni + nr:ni + nr + no], refs[ni + nr + no:ni + nr + no + nro]
        scr = refs[ni + nr + no + nro:ni + nr + no + nro + n_scr]
        send_sems, recv_sems = refs[-2], refs[-1]
        view = {nm: r for nm, r, k in zip(ride.names, rin, ride.kinds) if k == "in"}
        view.update(dict(zip(r_out_names, rout)))
        first = pl.program_id(0) == 0
        last = pl.program_id(0) == grid[0] - 1
        for d in range(1, nd):
            first = first & (pl.program_id(d) == 0)
            last = last & (pl.program_id(d) == grid[d] - 1)

        @pl.when(first)
        def _():
            base = 0
            for n, start, _ in ride.ops:
                start(view, send_sems, recv_sems, base)
                base += n

        body(*ins, *outs, *scr)

        @pl.when(last)
        def _():
            base = 0
            for n, _, finish in ride.ops:
                finish(view, send_sems, recv_sems, base)
                base += n

    res = pl.pallas_call(
        wrapped, name=name, grid=grid,
        in_specs=list(in_specs) + [ANY] * nr, out_specs=tuple(out_specs) + (ANY,) * nro,
        out_shape=tuple(out_shape) + tuple(r_out_shapes),
        scratch_shapes=list(scratch_shapes) + [pltpu.SemaphoreType.DMA((n_sems,)), pltpu.SemaphoreType.DMA((n_sems,))],
        input_output_aliases={**dict(aliases or {}), **{ni + i: no + j for j, i in enumerate(inout)}},
        compiler_params=_cp(sem),
    )(*args, *ride.arrays)
    return res[:no], dict(zip(r_out_names, res[no:]))


def _half(ref, chip_idx, cc, part=(0, 1, 1)):
    n = ref.shape[1] // 2
    lo, hi, k = part
    return ref.at[chip_idx, pl.ds(cc * n + lo * n // k, (hi - lo) * n // k)]


def _ride_gather_ici(ride, name, a, part=(0, 1, 1)):
    ride.inout(name, a)

    def start(view, ss, rs, b):
        x, y, c = _place()
        mine = _half(view[name], 2 * x + y, c, part)
        for j, (cx, cy) in enumerate(_other_chips(x, y)):
            _remote(mine, mine, ss, rs, b + j, (cx, cy, c)).start()

    def finish(view, ss, rs, b):
        x, y, c = _place()
        mine = _half(view[name], 2 * x + y, c, part)
        for j, (cx, cy) in enumerate(_other_chips(x, y)):
            theirs = _half(view[name], 2 * cx + cy, c, part)
            _remote(theirs, theirs, ss, rs, b + j, (cx, cy, c)).wait_recv()
        for j, (cx, cy) in enumerate(_other_chips(x, y)):
            _remote(mine, mine, ss, rs, b + j, (cx, cy, c)).wait_send()

    ride.op(3, start, finish)


def _ride_gather_pass(ride, name, a):
    ride.inout(name, a)

    def start(view, ss, rs, b):
        x, y, c = _place()
        for j, (cx, cy) in enumerate(_other_chips(x, y)):
            blk = _half(view[name], 2 * cx + cy, c)
            _remote(blk, blk, ss, rs, b + j, (x, y, 1 - c)).start()

    def finish(view, ss, rs, b):
        x, y, c = _place()
        for j, (cx, cy) in enumerate(_other_chips(x, y)):
            theirs = _half(view[name], 2 * cx + cy, 1 - c)
            _remote(theirs, theirs, ss, rs, b + j, (x, y, 1 - c)).wait_recv()
        for j, (cx, cy) in enumerate(_other_chips(x, y)):
            blk = _half(view[name], 2 * cx + cy, c)
            _remote(blk, blk, ss, rs, b + j, (x, y, 1 - c)).wait_send()

    ride.op(3, start, finish)


def _ride_exchange(ride, src_name, src, dst_name, dst, plan):
    ride.read(src_name, src)
    if dst is None:
        ride.land(dst_name, (3,) + src.shape[1:], src.dtype)
    else:
        ride.inout(dst_name, dst)
    A = src.shape[1]

    def copy(view, ss, rs, sem, j, part, x, y, c):
        lo, hi, k = part
        rows = pl.ds(lo * A // k, (hi - lo) * A // k)
        cx, cy = _other_chips(x, y)[j]
        return _remote(view[src_name].at[2 * cx + cy, rows], view[dst_name].at[j, rows], ss, rs, sem, (cx, cy, c))

    def start(view, ss, rs, b):
        x, y, c = _place()
        for i, (j, part) in enumerate(plan):
            copy(view, ss, rs, b + i, j, part, x, y, c).start()

    def finish(view, ss, rs, b):
        x, y, c = _place()
        for i, (j, part) in enumerate(plan):
            copy(view, ss, rs, b + i, j, part, x, y, c).wait()

    ride.op(len(plan), start, finish)


WHOLE, FIRST_HALF, SECOND_HALF = (0, 1, 1), (0, 1, 2), (1, 2, 2)


def _ride_swap(ride, src_name, src, dst_name):
    ride.read(src_name, src)
    ride.land(dst_name, src.shape, src.dtype)

    def start(view, ss, rs, b):
        x, y, c = _place()
        _remote(view[src_name], view[dst_name], ss, rs, b, (x, y, 1 - c)).start()

    def finish(view, ss, rs, b):
        x, y, c = _place()
        _remote(view[src_name], view[dst_name], ss, rs, b, (x, y, 1 - c)).wait()

    ride.op(1, start, finish)


def _peers(x, y, c):
    flip = lambda v, b: 1 - v if b else v
    return [(flip(x, r & 4), flip(y, r & 2), flip(c, r & 1)) for r in range(1, 8)]


def _ride_gather_all(ride, src_name, src, dst_name):
    ride.read(src_name, src)
    ride.land(dst_name, (8,) + src.shape, src.dtype)

    def start(view, ss, rs, b):
        x, y, c = _place()
        me = 4 * x + 2 * y + c
        pltpu.make_async_copy(view[src_name], view[dst_name].at[me], ss.at[b + 7]).start()
        for r, peer in enumerate(_peers(x, y, c)):
            _remote(view[src_name], view[dst_name].at[me], ss, rs, b + r, peer).start()

    def finish(view, ss, rs, b):
        x, y, c = _place()
        me = 4 * x + 2 * y + c
        for r, peer in enumerate(_peers(x, y, c)):
            theirs = view[dst_name].at[4 * peer[0] + 2 * peer[1] + peer[2]]
            _remote(theirs, theirs, ss, rs, b + r, peer).wait_recv()
        for r, peer in enumerate(_peers(x, y, c)):
            _remote(view[src_name], view[dst_name].at[me], ss, rs, b + r, peer).wait_send()
        pltpu.make_async_copy(view[src_name], view[dst_name].at[me], ss.at[b + 7]).wait()

    ride.op(8, start, finish)


def _cast_win(chip, w_in, l):
    _, D, W = w_in.shape
    tr = min(256, D)

    def body(chip_ref, w_ref, o_ref):
        o_ref[...] = w_ref[...].astype(BF16)

    return pl.pallas_call(
        body, name="cast_win",
        out_shape=jax.ShapeDtypeStruct((4, D, W), BF16),
        grid_spec=pltpu.PrefetchScalarGridSpec(
            num_scalar_prefetch=1, grid=(D // tr,),
            in_specs=[pl.BlockSpec((None, tr, W), lambda r, c: (l, r, 0))],
            out_specs=pl.BlockSpec((None, tr, W), lambda r, c: (c[0], r, 0))),
        compiler_params=_cp(("arbitrary",)),
    )(chip, w_in)


def _cast_wsq(chip, w_ro, w_co, w_o, l):
    _, R, D = w_ro.shape

    def body(chip_ref, a_ref, b_ref, c_ref, o_ref):
        o_ref[0:R, :] = a_ref[...].astype(BF16)
        o_ref[R:2 * R, :] = b_ref[...].astype(BF16)
        o_ref[2 * R:3 * R, :] = c_ref[...].astype(BF16)

    spec = pl.BlockSpec((None, R, D), lambda i, c: (l, 0, 0))
    return pl.pallas_call(
        body, name="cast_wsq",
        out_shape=jax.ShapeDtypeStruct((4, 3 * R, D), BF16),
        grid_spec=pltpu.PrefetchScalarGridSpec(
            num_scalar_prefetch=1, grid=(1,),
            in_specs=[spec, spec, spec],
            out_specs=pl.BlockSpec((None, 3 * R, D), lambda i, c: (c[0], 0, 0))),
        compiler_params=_cp(("arbitrary",)),
    )(chip, w_ro, w_co, w_o)


def _place_cw(chip, conv_w):
    NL, K, Cc = conv_w.shape

    def body(chip_ref, w_ref, o_ref):
        o_ref[...] = w_ref[...]

    return pl.pallas_call(
        body, name="place_cw",
        out_shape=jax.ShapeDtypeStruct((4, NL, K, Cc), F32),
        grid_spec=pltpu.PrefetchScalarGridSpec(
            num_scalar_prefetch=1, grid=(1,),
            in_specs=[pl.BlockSpec((NL, K, Cc), lambda i, c: (0, 0, 0))],
            out_specs=pl.BlockSpec((None, NL, K, Cc), lambda i, c: (c[0], 0, 0, 0))),
        compiler_params=_cp(("arbitrary",)),
    )(chip, conv_w)


def _gather_first(win0, wcw):
    n_arr = 2

    def body(a0, a1, o0, o1, send_sems, recv_sems):
        x, y, c = _place()
        sibling = (x, y, 1 - c)
        chips = _other_chips(x, y)
        outs = (o0, o1)

        def copy(k, a, cx, cy, cc, to):
            blk = _half(outs[a], 2 * cx + cy, cc)
            return _remote(blk, blk, send_sems, recv_sems, k, to)

        first = [copy(3 * a + j, a, x, y, c, (*chip, c)) for a in range(n_arr) for j, chip in enumerate(chips)]
        for cp in first:
            cp.start()
        passed = [copy(3 * n_arr + 3 * a + j, a, *chip, c, sibling) for a in range(n_arr) for j, chip in enumerate(chips)]
        for a in range(n_arr):
            for j, chip in enumerate(chips):
                copy(3 * a + j, a, *chip, c, sibling).wait_recv()
                passed[3 * a + j].start()
        for a in range(n_arr):
            for j, chip in enumerate(chips):
                copy(3 * n_arr + 3 * a + j, a, *chip, 1 - c, sibling).wait_recv()
        for cp in first + passed:
            cp.wait_send()

    ins = (win0, wcw)
    return pl.pallas_call(
        body, name="gather_first",
        out_shape=tuple(jax.ShapeDtypeStruct(a.shape, a.dtype) for a in ins),
        in_specs=[ANY] * n_arr, out_specs=(ANY,) * n_arr,
        scratch_shapes=[pltpu.SemaphoreType.DMA((6 * n_arr,)), pltpu.SemaphoreType.DMA((6 * n_arr,))],
        input_output_aliases={0: 0, 1: 1},
        compiler_params=pltpu.CompilerParams(has_side_effects=True),
    )(*ins)


def _fwd_in(x, pre_g, win, l, ride=None):
    T, D = x.shape
    W = win.shape[-1]
    tm = min(512, T)

    def body(x_ref, g_ref, w_ref, p_ref, h_ref):
        hb = _rms(x_ref[...], g_ref[...]).astype(BF16)
        h_ref[...] = hb
        for j in range(4):
            p_ref[:, j * W:(j + 1) * W] = _dot(hb, w_ref[j]).astype(BF16)

    return _pcall(
        body, name="fwd_in",
        out_shape=(jax.ShapeDtypeStruct((T, 4 * W), BF16), jax.ShapeDtypeStruct((T, D), BF16)),
        grid=(T // tm,),
        in_specs=[pl.BlockSpec((tm, D), lambda i: (i, 0)),
                  pl.BlockSpec((None, 1, D), lambda i: (l, 0, 0)),
                  pl.BlockSpec((4, D, W), lambda i: (0, 0, 0), pipeline_mode=pl.Buffered(1))],
        out_specs=(pl.BlockSpec((tm, 4 * W), lambda i: (i, 0)),
                   pl.BlockSpec((tm, D), lambda i: (i, 0))),
        sem=("arbitrary",), args=(x, pre_g, win), ride=ride)


def _ret_specs(T, L):
    rope = pl.BlockSpec((3, L, 128), lambda s: (0, s, 0))
    mask = pl.BlockSpec((HEADS, L, L), lambda s: (0, 0, 0), pipeline_mode=pl.Buffered(1))
    qdm = pl.BlockSpec((HEADS, L, 128), lambda s: (0, 0, 0), pipeline_mode=pl.Buffered(1))
    small = pl.BlockSpec((HEADS, 8, 128), lambda s: (0, 0, 0))
    return [rope, mask, qdm, qdm, small, small]


QK = HEADS * DK
VW = HEADS * DV
PW = 2 * QK + 2 * VW


def _zero_at_start(ref):
    @pl.when(pl.program_id(0) == 0)
    def _():
        ref[...] = jnp.zeros_like(ref)


def _ret_fwd_part(p_ref, rope_ref, m_ref, qdm_ref, kdm_ref, mqs_ref, cd_ref, a_ref, st_ref, state):
    c, sl, sh = rope_ref[0], rope_ref[1], rope_ref[2]
    for j in range(HEADS // 2):
        rq = _rot(p_ref[:, 128 * j:128 * (j + 1)].astype(F32), c, sl, sh)
        rk = _rot(p_ref[:, QK + 128 * j:QK + 128 * (j + 1)].astype(F32), c, sl, sh)
        rkb = rk.astype(BF16)
        for e in range(2):
            h = 2 * j + e
            v = p_ref[:, 2 * QK + DV * h:2 * QK + DV * (h + 1)]
            g = p_ref[:, 2 * QK + VW + DV * h:2 * QK + VW + DV * (h + 1)].astype(F32)
            a = (rq * mqs_ref[h, 0:1, :]).astype(BF16)
            p = (_dot_nt(a, rkb) * m_ref[h]).astype(BF16)
            st = state[h]
            st_ref[h] = st
            o = _dot(p, v) + _dot((rq * qdm_ref[h]).astype(BF16), st.astype(BF16))
            state[h] = st * cd_ref[h, 0:1, :] + _dot_tn((rk * kdm_ref[h]).astype(BF16), v)
            a_ref[:, DV * h:DV * (h + 1)] = _gn_gate(o, g).astype(BF16)


def _shift_copies(src, sh):
    rows = sh.shape[1]
    for b in range(1, 8):
        sh[b - 1, :, :] = src[pl.ds(b, rows), :]


def _window(src, sh, r0, const, rows, lanes):
    b = const % 8
    at = pl.ds(pl.multiple_of(r0 + (const - b), 8), rows)
    w = src[at, lanes] if b == 0 else sh[b - 1, at, lanes]
    return w.reshape(rows // 8, 8, w.shape[-1])


def _conv_taps(wbuf, src, sh, r0, const, rows, lanes):
    groups = rows // 8
    accs = [None] * groups
    for k in range(CONV_K):
        w8 = wbuf[pl.ds(8 * k, 8), lanes]
        win = _window(src, sh, r0, const + k, rows, lanes)
        for g in range(groups):
            term = w8 * win[g]
            accs[g] = term if k == 0 else accs[g] + term
    return jnp.concatenate(accs, axis=0)


def _lane_parts(C):
    return [pl.ds(j * CONV_LANES, CONV_LANES) for j in range(C // CONV_LANES)]


def _load_conv_w(cw_ref, wbuf, flip):
    for k in range(CONV_K):
        row = jnp.concatenate([cw_ref[c, pl.ds(k, 1), :] for c in range(4)], axis=-1)
        kk = CONV_K - 1 - k if flip else k
        wbuf[pl.ds(8 * kk, 8), :] = jnp.broadcast_to(row, (8, row.shape[-1]))


def _conv_fwd_part(p_ref, cw_ref, cb_ref, lg_ref, lb_ref, a_ref, y_ref, ubuf, wbuf, ush):
    tc, C = y_ref.shape
    off = HALO - (CONV_K - 1)
    i = pl.program_id(0)

    @pl.when(i == 0)
    def _():
        ubuf[0:HALO, :] = jnp.zeros((HALO, C), F32)
        _load_conv_w(cw_ref, wbuf, False)

    @pl.when(i > 0)
    def _():
        ubuf[0:HALO, :] = ubuf[tc:tc + HALO, :]

    ga = p_ref[:, 0:C].astype(F32)
    gb = p_ref[:, C:2 * C].astype(F32)
    ubuf[HALO:HALO + tc, :] = ga * _sigmoid(gb)
    _shift_copies(ubuf, ush)

    def rows_block(r, carry):
        r0 = pl.multiple_of(r * CONV_RB, CONV_RB)
        for lanes in _lane_parts(C):
            y_ref[pl.ds(r0, CONV_RB), lanes] = _conv_taps(wbuf, ubuf, ush, r0, off, CONV_RB, lanes) + cb_ref[:, lanes]
        return carry

    lax.fori_loop(0, tc // CONV_RB, rows_block, 0)
    gc = p_ref[:, 2 * C:3 * C].astype(F32)
    a_ref[...] = _ln_gate(y_ref[...], gc, lg_ref[...], lb_ref[...]).astype(BF16)


def _ret_fwd(proj, tb, L, ride=None):
    T = proj.shape[0]
    nS = T // L

    def body(p_ref, rope_ref, m_ref, qdm_ref, kdm_ref, mqs_ref, cd_ref, ar_ref, st_ref, state):
        _zero_at_start(state)
        _ret_fwd_part(p_ref, rope_ref, m_ref, qdm_ref, kdm_ref, mqs_ref, cd_ref, ar_ref, st_ref, state)

    return _pcall(
        body, name="ret_fwd",
        out_shape=(jax.ShapeDtypeStruct((T, VW), BF16), jax.ShapeDtypeStruct((nS, HEADS, 128, DV), F32)),
        grid=(nS,),
        in_specs=[pl.BlockSpec((L, PW), lambda s: (s, 0))] + _ret_specs(T, L),
        out_specs=(pl.BlockSpec((L, VW), lambda s: (s, 0)), pl.BlockSpec((None, HEADS, 128, DV), lambda s: (s, 0, 0, 0))),
        scratch_shapes=[pltpu.VMEM((HEADS, 128, DV), F32)],
        sem=("arbitrary",), ride=ride,
        args=(proj, tb["rope"], tb["mask"], tb["qdm"], tb["kdm"], tb["mqs"], tb["cd"]))


def _conv_fwd(proj, wcw, conv_b, ln_g, ln_b, l, ride=None):
    T = proj.shape[0]
    C = conv_b.shape[-1]
    Cc = wcw.shape[-1]
    tc = min(CONV_TILE, T)
    assert PW == 3 * C

    def body(p_ref, cw_ref, cb_ref, lg_ref, lb_ref, ac_ref, y_ref, ubuf, wbuf, ush):
        _conv_fwd_part(p_ref, cw_ref, cb_ref, lg_ref, lb_ref, ac_ref, y_ref, ubuf, wbuf, ush)

    vec = pl.BlockSpec((None, 1, C), lambda i: (l, 0, 0))
    tile = pl.BlockSpec((tc, C), lambda i: (i, 0))
    return _pcall(
        body, name="conv_fwd",
        out_shape=(jax.ShapeDtypeStruct((T, C), BF16), jax.ShapeDtypeStruct((T, C), F32)),
        grid=(T // tc,),
        in_specs=[pl.BlockSpec((tc, 3 * C), lambda i: (i, 1)),
                  pl.BlockSpec((4, None, CONV_K, Cc), lambda i: (0, l, 0, 0)), vec, vec, vec],
        out_specs=(tile, tile),
        scratch_shapes=[pltpu.VMEM((HALO + tc, C), F32), pltpu.VMEM((8 * CONV_K, C), F32),
                        pltpu.VMEM((7, HALO + tc - 8, C), F32)],
        sem=("arbitrary",), ride=ride, args=(proj, wcw, conv_b, ln_g, ln_b))


def _merge_fwd(x, proj, a_ret, a_conv, wsq, post_g, l, ride=None):
    T, D = x.shape
    R = wsq.shape[1] // 3
    tm = min(512, T)

    def body(x_ref, p_ref, ar_ref, ac_ref, wro_ref, wco_ref, wo_ref, g_ref, xn_ref, ro_ref, co_ref, ym_ref, z_ref):
        ro = _dot(ar_ref[...], wro_ref[...].reshape(4 * R, D))
        co = _dot(ac_ref[...], wco_ref[...].reshape(4 * R, D))
        ym = (_sigmoid(p_ref[:, 0:D].astype(F32)) * ro + _sigmoid(p_ref[:, D:2 * D].astype(F32)) * co).astype(BF16)
        z = _dot(ym, wo_ref[...].reshape(4 * R, D))
        ro_ref[...] = ro.astype(BF16)
        co_ref[...] = co.astype(BF16)
        ym_ref[...] = ym
        z_ref[...] = z.astype(BF16)
        xn_ref[...] = x_ref[...] + _rms(z, g_ref[...])

    tile = pl.BlockSpec((tm, D), lambda i: (i, 0))
    wspec = lambda m: pl.BlockSpec((4, R, D), lambda i: (0, m, 0))
    act = jax.ShapeDtypeStruct((T, D), BF16)
    return _pcall(
        body, name="merge_fwd",
        out_shape=(jax.ShapeDtypeStruct((T, D), F32), act, act, act, act),
        grid=(T // tm,),
        in_specs=[tile, pl.BlockSpec((tm, 2 * D), lambda i: (i, 3)), tile, tile,
                  wspec(0), wspec(1), wspec(2), pl.BlockSpec((None, 1, D), lambda i: (l, 0, 0))],
        out_specs=(tile, tile, tile, tile, tile),
        sem=("arbitrary",), ride=ride, args=(x, proj, a_ret, a_conv, wsq, wsq, wsq, post_g))


def _loss_fwd_bwd(y, target):
    T, D = y.shape
    tm = min(512, T)

    def body(y_ref, t_ref, dy_ref, ls_ref):
        @pl.when(pl.program_id(0) == 0)
        def _():
            ls_ref[...] = jnp.zeros_like(ls_ref)

        e = y_ref[...] - t_ref[...]
        dy_ref[...] = e * (1.0 / D)
        ls_ref[...] += jnp.sum((e * e).reshape(tm // 8, 8, D), axis=0) * (0.5 / D)

    tile = pl.BlockSpec((tm, D), lambda i: (i, 0))
    return pl.pallas_call(
        body, name="loss",
        out_shape=(jax.ShapeDtypeStruct((T, D), F32), jax.ShapeDtypeStruct((8, D), F32)),
        grid=(T // tm,),
        in_specs=[tile, tile],
        out_specs=(tile, pl.BlockSpec((8, D), lambda i: (0, 0))),
        compiler_params=_cp(("arbitrary",)),
    )(y, target)


def _merge_bwd(dxn, proj, a_ret, a_conv, ro, co, ym, z, wsq, post_g, l, ride=None):
    T, D = dxn.shape
    R = wsq.shape[1] // 3
    tm = min(512, T)
    n = T // tm

    def body(dx_ref, p_ref, ar_ref, ac_ref, ro_ref, co_ref, ym_ref, z_ref, wro_ref, wco_ref, wo_ref, g_ref,
             dp_ref, dar_ref, dac_ref, gsq_ref, dg_ref, acc, stage):
        i = pl.program_id(0)

        @pl.when(i == 0)
        def _():
            acc[...] = jnp.zeros_like(acc)
            dg_ref[...] = jnp.zeros_like(dg_ref)

        _, vjp = jax.vjp(_rms, z_ref[...].astype(F32), g_ref[...])
        dz, dg = vjp(dx_ref[...])
        dg_ref[0:1, :] += dg
        dzb = dz.astype(BF16)
        dym = _dot_nt(dzb, wo_ref[...].reshape(4 * R, D))
        acc[2] += _dot_tn(ym_ref[...], dzb)
        sr = _sigmoid(p_ref[:, 0:D].astype(F32))
        sc = _sigmoid(p_ref[:, D:2 * D].astype(F32))
        rov = ro_ref[...].astype(F32)
        cov = co_ref[...].astype(F32)
        dp_ref[:, 0:D] = (dym * rov * sr * (1.0 - sr)).astype(BF16)
        dp_ref[:, D:2 * D] = (dym * cov * sc * (1.0 - sc)).astype(BF16)
        dro = (dym * sr).astype(BF16)
        dco = (dym * sc).astype(BF16)
        dar_ref[...] = _dot_nt(dro, wro_ref[...].reshape(4 * R, D)).astype(BF16)
        dac_ref[...] = _dot_nt(dco, wco_ref[...].reshape(4 * R, D)).astype(BF16)
        acc[0] += _dot_tn(ar_ref[...], dro)
        acc[1] += _dot_tn(ac_ref[...], dco)

        @pl.when(i == n - 1)
        def _():
            for m in range(3):
                stage[...] = acc[m].astype(BF16).reshape(4, R, D)
                pltpu.sync_copy(stage, gsq_ref.at[:, pl.ds(m * R, R), :])

    tile = pl.BlockSpec((tm, D), lambda i: (i, 0))
    wspec = lambda m: pl.BlockSpec((4, R, D), lambda i: (0, m, 0), pipeline_mode=pl.Buffered(1))
    return _pcall(
        body, name="merge_bwd",
        out_shape=(jax.ShapeDtypeStruct(proj.shape, BF16), jax.ShapeDtypeStruct((T, D), BF16),
                   jax.ShapeDtypeStruct((T, D), BF16), jax.ShapeDtypeStruct(wsq.shape, BF16),
                   jax.ShapeDtypeStruct((8, D), F32)),
        grid=(n,),
        in_specs=[tile, pl.BlockSpec((tm, 2 * D), lambda i: (i, 3)), tile, tile, tile, tile, tile, tile,
                  wspec(0), wspec(1), wspec(2), pl.BlockSpec((None, 1, D), lambda i: (l, 0, 0))],
        out_specs=(pl.BlockSpec((tm, 2 * D), lambda i: (i, 3)), tile, tile, ANY, pl.BlockSpec((8, D), lambda i: (0, 0))),
        scratch_shapes=[pltpu.VMEM((3, 4 * R, D), F32), pltpu.VMEM((4, R, D), BF16)],
        sem=("arbitrary",), args=(dxn, proj, a_ret, a_conv, ro, co, ym, z, wsq, wsq, wsq, post_g), ride=ride)


def _conv_bwd_part(n, da_ref, y_ref, p_ref, ph_ref, cw_ref, lg_ref, lb_ref, dp_ref,
                   dcbuf, ubuf, dubuf, wbuf, dwacc, vacc, dsh, ush):
    tc, C = y_ref.shape
    off = HALO - (CONV_K - 1)
    nrb = tc // CONV_RB
    t = pl.program_id(0)
    i = n - 1 - t

    @pl.when(t == 0)
    def _():
        dcbuf[tc:tc + HALO, :] = jnp.zeros((HALO, C), F32)
        dwacc[...] = jnp.zeros_like(dwacc)
        vacc[...] = jnp.zeros_like(vacc)
        _load_conv_w(cw_ref, wbuf, True)

    @pl.when(t > 0)
    def _():
        dcbuf[tc:tc + HALO, :] = dcbuf[0:HALO, :]

    gc = p_ref[:, 2 * C:3 * C].astype(F32)
    _, vjp = jax.vjp(_ln_gate, y_ref[...], gc, lg_ref[...], lb_ref[...])
    dy, dgc, dlg, dlb = vjp(da_ref[...].astype(F32))
    dcbuf[0:tc, :] = dy
    dp_ref[:, 2 * C:3 * C] = dgc.astype(BF16)
    vacc[0:1, :] += jnp.sum(dy, axis=0, keepdims=True)
    vacc[1:2, :] += dlg
    vacc[2:3, :] += dlb

    ga = p_ref[:, 0:C].astype(F32)
    sb = _sigmoid(p_ref[:, C:2 * C].astype(F32))
    ubuf[HALO:HALO + tc, :] = ga * sb
    uh = ph_ref[:, 0:C].astype(F32) * _sigmoid(ph_ref[:, C:2 * C].astype(F32))
    ubuf[0:HALO, :] = jnp.where(i > 0, uh, 0.0)

    _shift_copies(dcbuf, dsh)
    _shift_copies(ubuf, ush)
    def du_block(r, carry):
        r0 = pl.multiple_of(r * CONV_RB, CONV_RB)
        for lanes in _lane_parts(C):
            dubuf[pl.ds(r0, CONV_RB), lanes] = _conv_taps(wbuf, dcbuf, dsh, r0, 0, CONV_RB, lanes)
        return carry

    lax.fori_loop(0, nrb, du_block, 0)
    du = dubuf[...]
    dp_ref[:, 0:C] = (du * sb).astype(BF16)
    dp_ref[:, C:2 * C] = (du * ga * sb * (1.0 - sb)).astype(BF16)

    def dw_block(r, carry):
        r0 = pl.multiple_of(r * CONV_RB, CONV_RB)
        for lanes in _lane_parts(C):
            dyb = dcbuf[pl.ds(r0, CONV_RB), lanes].reshape(CONV_RB // 8, 8, CONV_LANES)
            for k in range(CONV_K):
                dwacc[8 * k:8 * k + 8, lanes] += jnp.sum(dyb * _window(ubuf, ush, r0, off + k, CONV_RB, lanes), axis=0)
        return carry

    lax.fori_loop(0, nrb, dw_block, 0)


def _conv_bwd_final(n, sg_ref, dwacc, vacc):
    C = sg_ref.shape[-1]

    @pl.when(pl.program_id(0) == n - 1)
    def _():
        for k in range(CONV_K):
            sg_ref[pl.ds(k, 1), :] = jnp.sum(dwacc[8 * k:8 * k + 8, :], axis=0, keepdims=True)
        sg_ref[pl.ds(CONV_K, 1), :] = jnp.zeros((1, C), F32)
        sg_ref[ROW_CB:ROW_CB + 8, :] = jnp.zeros((8, C), F32)
        sg_ref[ROW_CB:ROW_CB + 3, :] = vacc[0:3, :]


def _ret_bwd_part(da_ref, p_ref, st_ref, rope_ref, m_ref, qdm_ref, kdm_ref, mqs_ref, cd_ref, dp_ref, gst):
    c, sl, sh = rope_ref[0], rope_ref[1], rope_ref[2]
    for j in range(HEADS // 2):
        rq = _rot(p_ref[:, 128 * j:128 * (j + 1)].astype(F32), c, sl, sh)
        rk = _rot(p_ref[:, QK + 128 * j:QK + 128 * (j + 1)].astype(F32), c, sl, sh)
        rkb = rk.astype(BF16)
        drq = jnp.zeros_like(rq)
        drk = jnp.zeros_like(rk)
        for e in range(2):
            h = 2 * j + e
            v = p_ref[:, 2 * QK + DV * h:2 * QK + DV * (h + 1)]
            g = p_ref[:, 2 * QK + VW + DV * h:2 * QK + VW + DV * (h + 1)].astype(F32)
            mqs = mqs_ref[h, 0:1, :]
            a = (rq * mqs).astype(BF16)
            aq = (rq * qdm_ref[h]).astype(BF16)
            kdv = (rk * kdm_ref[h]).astype(BF16)
            mk = m_ref[h]
            p = (_dot_nt(a, rkb) * mk).astype(BF16)
            stb = st_ref[h].astype(BF16)
            o = _dot(p, v) + _dot(aq, stb)
            _, vjp = jax.vjp(_gn_gate, o, g)
            do, dg = vjp(da_ref[:, DV * h:DV * (h + 1)].astype(F32))
            dob = do.astype(BF16)
            gs = gst[h]
            gsb = gs.astype(BF16)
            ds = (_dot_nt(dob, v) * mk).astype(BF16)
            drq = drq + _dot(ds, rkb) * mqs + _dot_nt(dob, stb) * qdm_ref[h]
            drk = drk + _dot_tn(ds, a) + _dot_nt(v, gsb) * kdm_ref[h]
            dv = _dot_tn(p, dob) + _dot(kdv, gsb)
            gst[h] = _dot_tn(aq, dob) + gs * cd_ref[h, 0:1, :]
            dp_ref[:, 2 * QK + DV * h:2 * QK + DV * (h + 1)] = dv.astype(BF16)
            dp_ref[:, 2 * QK + VW + DV * h:2 * QK + VW + DV * (h + 1)] = dg.astype(BF16)
        dp_ref[:, 128 * j:128 * (j + 1)] = _rot_t(drq, c, sl, sh).astype(BF16)
        dp_ref[:, QK + 128 * j:QK + 128 * (j + 1)] = _rot_t(drk, c, sl, sh).astype(BF16)


def _ret_bwd(dproj, da_ret, proj, states, tb, L, ride=None):
    T = proj.shape[0]
    nS = T // L

    def body(dpin_ref, dar_ref, p_ref, st_ref, rope_ref, m_ref, qdm_ref, kdm_ref, mqs_ref, cd_ref, dp_ref, gst):
        _zero_at_start(gst)
        _ret_bwd_part(dar_ref, p_ref, st_ref, rope_ref, m_ref, qdm_ref, kdm_ref, mqs_ref, cd_ref, dp_ref, gst)

    rev = lambda s: nS - 1 - s
    specs = _ret_specs(T, L)
    specs[0] = pl.BlockSpec((3, L, 128), lambda s: (0, rev(s), 0))
    ptile = pl.BlockSpec((L, PW), lambda s: (rev(s), 0))
    return _pcall(
        body, name="ret_bwd",
        out_shape=(jax.ShapeDtypeStruct(dproj.shape, BF16),),
        grid=(nS,),
        in_specs=[ANY, pl.BlockSpec((L, VW), lambda s: (rev(s), 0)), ptile,
                  pl.BlockSpec((None, HEADS, 128, DV), lambda s: (rev(s), 0, 0, 0))] + specs,
        out_specs=(ptile,),
        scratch_shapes=[pltpu.VMEM((HEADS, 128, DV), F32)],
        sem=("arbitrary",), aliases={0: 0}, ride=ride,
        args=(dproj, da_ret, proj, states, tb["rope"], tb["mask"], tb["qdm"], tb["kdm"], tb["mqs"], tb["cd"]))


def _conv_bwd(dproj, da_conv, y, proj, wcw, ln_g, ln_b, l, ride=None):
    T, C = y.shape
    Cc = wcw.shape[-1]
    tc = min(CONV_TILE, T)
    n = T // tc
    hb = tc // HALO

    def body(dpin_ref, dac_ref, y_ref, p_ref, ph_ref, cw_ref, lg_ref, lb_ref, dp_ref, sg_ref,
             dcbuf, ubuf, dubuf, wbuf, dwacc, vacc, dsh, ush):
        _conv_bwd_part(n, dac_ref, y_ref, p_ref, ph_ref, cw_ref, lg_ref, lb_ref, dp_ref,
                       dcbuf, ubuf, dubuf, wbuf, dwacc, vacc, dsh, ush)
        _conv_bwd_final(n, sg_ref, dwacc, vacc)

    rev = lambda t: n - 1 - t
    vec = pl.BlockSpec((None, 1, C), lambda t: (l, 0, 0))
    tile = pl.BlockSpec((tc, C), lambda t: (rev(t), 0))
    ptile = pl.BlockSpec((tc, 3 * C), lambda t: (rev(t), 1))
    halo = pl.BlockSpec((HALO, 3 * C), lambda t: (jnp.maximum(rev(t) * hb - 1, 0), 1))
    return _pcall(
        body, name="conv_bwd",
        out_shape=(jax.ShapeDtypeStruct(dproj.shape, BF16), jax.ShapeDtypeStruct((ROW_PRE, C), F32)),
        grid=(n,),
        in_specs=[ANY, tile, tile, ptile, halo, pl.BlockSpec((4, None, CONV_K, Cc), lambda t: (0, l, 0, 0)), vec, vec],
        out_specs=(ptile, pl.BlockSpec((ROW_PRE, C), lambda t: (0, 0))),
        scratch_shapes=[pltpu.VMEM((tc + HALO, C), F32), pltpu.VMEM((HALO + tc, C), F32), pltpu.VMEM((tc, C), F32),
                        pltpu.VMEM((8 * CONV_K, C), F32), pltpu.VMEM((8 * CONV_K, C), F32), pltpu.VMEM((8, C), F32),
                        pltpu.VMEM((7, HALO + tc - 8, C), F32), pltpu.VMEM((7, HALO + tc - 8, C), F32)],
        sem=("arbitrary",), aliases={0: 0}, ride=ride, args=(dproj, da_conv, y, proj, proj, wcw, ln_g, ln_b))


def _win_grad(h, dproj, W, ride=None):
    T, D = h.shape
    tk = min(2048, T)
    nk = T // tk

    def body(h_ref, dp_ref, g_ref, acc):
        k = pl.program_id(1)

        @pl.when(k == 0)
        def _():
            acc[...] = jnp.zeros_like(acc)

        acc[...] += _dot_tn(h_ref[...], dp_ref[...])

        @pl.when(k == nk - 1)
        def _():
            g_ref[...] = acc[...].astype(BF16)

    return _pcall(
        body, name="win_grad",
        out_shape=(jax.ShapeDtypeStruct((4, D, W), BF16),),
        grid=(4, nk),
        in_specs=[pl.BlockSpec((tk, D), lambda j, k: (k, 0)), pl.BlockSpec((tk, W), lambda j, k: (k, j))],
        out_specs=(pl.BlockSpec((None, D, W), lambda j, k: (j, 0, 0)),),
        scratch_shapes=[pltpu.VMEM((D, W), F32)],
        sem=("arbitrary", "arbitrary"), args=(h, dproj), ride=ride)


def _in_bwd(dxn, dproj, x, pre_g, win, l, ride=None):
    T, D = x.shape
    W = win.shape[-1]
    tm = min(512, T)

    def body(dxn_ref, dp_ref, x_ref, g_ref, w_ref, dx_ref, dg_ref):
        @pl.when(pl.program_id(0) == 0)
        def _():
            dg_ref[...] = jnp.zeros_like(dg_ref)

        dh = _dot_nt(dp_ref[:, 0:W], w_ref[0])
        for j in range(1, 4):
            dh = dh + _dot_nt(dp_ref[:, j * W:(j + 1) * W], w_ref[j])
        _, vjp = jax.vjp(_rms, x_ref[...], g_ref[...])
        dx, dg = vjp(dh)
        dx_ref[...] = dxn_ref[...] + dx
        dg_ref[0:1, :] += dg

    tile = pl.BlockSpec((tm, D), lambda i: (i, 0))
    return _pcall(
        body, name="in_bwd",
        out_shape=(jax.ShapeDtypeStruct((T, D), F32), jax.ShapeDtypeStruct((8, D), F32)),
        grid=(T // tm,),
        in_specs=[tile, pl.BlockSpec((tm, 4 * W), lambda i: (i, 0)), tile,
                  pl.BlockSpec((None, 1, D), lambda i: (l, 0, 0)),
                  pl.BlockSpec((4, D, W), lambda i: (0, 0, 0), pipeline_mode=pl.Buffered(1))],
        out_specs=(tile, pl.BlockSpec((8, D), lambda i: (0, 0))),
        sem=("arbitrary",), args=(dxn, dproj, x, pre_g, win), ride=ride)


def _sum_group(chip, t, u):
    _, A, B = t.shape
    tr = min(256, A)

    def body(k_ref, t_ref, u_ref, o_ref):
        o_ref[...] = ((t_ref[...].astype(F32) + u_ref[0].astype(F32)) + u_ref[1].astype(F32)) + u_ref[2].astype(F32)

    return pl.pallas_call(
        body, name="sum_group",
        out_shape=jax.ShapeDtypeStruct((A, B), F32),
        grid_spec=pltpu.PrefetchScalarGridSpec(
            num_scalar_prefetch=1, grid=(A // tr,),
            in_specs=[pl.BlockSpec((None, tr, B), lambda i, k: (k[0], i, 0)),
                      pl.BlockSpec((3, tr, B), lambda i, k: (0, i, 0))],
            out_specs=pl.BlockSpec((tr, B), lambda i, k: (i, 0))),
        compiler_params=_cp(("arbitrary",)),
    )(chip, t, u)


def _swap_rows(g):
    _, A, B = g.shape
    nh = A // 2

    def body(g_ref, r_ref, send_sems, recv_sems):
        x, y, c = _place()
        cp = _remote(g_ref.at[:, pl.ds((1 - c) * nh, nh)], r_ref, send_sems, recv_sems, 0, (x, y, 1 - c))
        cp.start()
        cp.wait()

    return pl.pallas_call(
        body, name="swap_rows",
        out_shape=jax.ShapeDtypeStruct((4, nh, B), g.dtype),
        in_specs=[ANY], out_specs=ANY,
        scratch_shapes=[pltpu.SemaphoreType.DMA((1,)), pltpu.SemaphoreType.DMA((1,))],
        compiler_params=pltpu.CompilerParams(has_side_effects=True),
    )(g)


def _add_rows(cidx, g, r):
    _, nh, B = r.shape
    tr = min(256, nh)
    nb = nh // tr

    def body(c_ref, g_ref, r_ref, o_ref):
        o_ref[...] = (g_ref[...].astype(F32) + r_ref[...].astype(F32)).astype(BF16)

    blk = (None, tr, B)
    return pl.pallas_call(
        body, name="add_rows",
        out_shape=jax.ShapeDtypeStruct(r.shape, BF16),
        grid_spec=pltpu.PrefetchScalarGridSpec(
            num_scalar_prefetch=1, grid=(4, nb),
            in_specs=[pl.BlockSpec(blk, lambda k, i, c: (k, c[0] * nb + i, 0)),
                      pl.BlockSpec(blk, lambda k, i, c: (k, i, 0))],
            out_specs=pl.BlockSpec(blk, lambda k, i, c: (k, i, 0))),
        compiler_params=_cp(("arbitrary", "arbitrary")),
    )(cidx, g, r)


def _sum_group_half(chip, cidx, t, u):
    _, nh, B = t.shape
    tr = min(256, nh)
    nb = nh // tr

    def body(k_ref, c_ref, t_ref, u_ref, o_ref):
        mine = (pl.program_id(0) // nb) == c_ref[0]

        @pl.when(mine)
        def _():
            o_ref[...] = ((t_ref[...].astype(F32) + u_ref[0].astype(F32)) + u_ref[1].astype(F32)) + u_ref[2].astype(F32)

        @pl.when(jnp.logical_not(mine))
        def _():
            o_ref[...] = jnp.zeros_like(o_ref)

    own = lambda i, c: jnp.clip(i - c[0] * nb, 0, nb - 1)
    return pl.pallas_call(
        body, name="sum_group_half",
        out_shape=jax.ShapeDtypeStruct((2 * nh, B), F32),
        grid_spec=pltpu.PrefetchScalarGridSpec(
            num_scalar_prefetch=2, grid=(2 * nb,),
            in_specs=[pl.BlockSpec((None, tr, B), lambda i, k, c: (k[0], own(i, c), 0)),
                      pl.BlockSpec((3, tr, B), lambda i, k, c: (0, own(i, c), 0))],
            out_specs=pl.BlockSpec((tr, B), lambda i, k, c: (i, 0))),
        compiler_params=_cp(("arbitrary",)),
    )(chip, cidx, t, u)


def _adam_math(w, g, m, v):
    c1 = 1.0 / (1.0 - ADAM_B1 ** ADAM_STEP)
    c2 = 1.0 / (1.0 - ADAM_B2 ** ADAM_STEP)
    nm = ADAM_B1 * m + (1.0 - ADAM_B1) * g
    nv = ADAM_B2 * v + (1.0 - ADAM_B2) * (g * g)
    return -ADAM_LR * ((nm * c1) / (jnp.sqrt(nv * c2) + ADAM_EPS) + ADAM_WD * w), nm, nv


def _adamw_layer(prev, w, m, v, sa, sb, l, part, ride=None):
    NL, A, B = w.shape
    tr = A
    while tr * B * 4 > ADAM_BLOCK_BYTES and tr % 16 == 0:
        tr //= 2
    nb = A // tr

    def body(p0, p1, p2, p3, w_ref, m_ref, v_ref, sa_ref, sb_ref, g_ref, d_ref, nm_ref, nv_ref):
        g = sa_ref[...] + sb_ref[...]
        g_ref[...] = g
        d_ref[...], nm_ref[...], nv_ref[...] = _adam_math(w_ref[...], g, m_ref[...], v_ref[...])

    lay = pl.BlockSpec((None, tr, B), lambda i: (l, i, 0))
    src = pl.BlockSpec((tr, B), lambda i: (part * nb + i, 0))
    full = jax.ShapeDtypeStruct((NL, A, B), F32)
    if prev is None:
        prev = tuple(lax.empty((NL, A, B), F32) for _ in range(4))
    outs, landed = _pcall(
        body, name="adamw_layer",
        out_shape=(full,) * 4, grid=(nb,),
        in_specs=[ANY] * 4 + [lay, lay, lay, src, src], out_specs=(lay,) * 4,
        sem=("arbitrary",), aliases={0: 0, 1: 1, 2: 2, 3: 3}, args=(*prev, w, m, v, sa, sb), ride=ride)
    return tuple(outs), landed


def _adamw(w, g, m, v):
    shape = w.shape
    cols = shape[-1]
    rows = int(np.prod(shape[:-1]))

    def body(w_ref, g_ref, m_ref, v_ref, d_ref, nm_ref, nv_ref):
        d_ref[...], nm_ref[...], nv_ref[...] = _adam_math(w_ref[...], g_ref[...], m_ref[...], v_ref[...])

    tile = pl.BlockSpec((rows, cols), lambda i: (0, 0))
    out = jax.ShapeDtypeStruct((rows, cols), F32)
    res = pl.pallas_call(
        body, name="adamw",
        out_shape=(out, out, out), grid=(1,),
        in_specs=[tile] * 4, out_specs=(tile,) * 3,
        compiler_params=_cp(("arbitrary",)),
    )(*[a.reshape(rows, cols) for a in (w, g, m, v)])
    return tuple(a.reshape(shape) for a in res)


def _tail_exchange(small, s_in, s_sq):
    def body(s_ref, a_ref, b_ref, o_ref, oa_ref, ob_ref, send_sems, recv_sems, local_sem):
        x, y, c = _place()
        me = 4 * x + 2 * y + c
        sibling = (x, y, 1 - c)
        mine = pltpu.make_async_copy(s_ref, o_ref.at[me], local_sem)
        mine.start()
        swaps = [_remote(a_ref, oa_ref, send_sems, recv_sems, 7, sibling), _remote(b_ref, ob_ref, send_sems, recv_sems, 8, sibling)]
        sends = [_remote(s_ref, o_ref.at[me], send_sems, recv_sems, r, peer) for r, peer in enumerate(_peers(x, y, c))]
        for cp in swaps + sends:
            cp.start()
        for r, peer in enumerate(_peers(x, y, c)):
            theirs = o_ref.at[4 * peer[0] + 2 * peer[1] + peer[2]]
            _remote(theirs, theirs, send_sems, recv_sems, r, peer).wait_recv()
        for cp in sends:
            cp.wait_send()
        for cp in swaps:
            cp.wait()
        mine.wait()

    return pl.pallas_call(
        body, name="tail_exchange",
        out_shape=(jax.ShapeDtypeStruct((8,) + small.shape, small.dtype),
                   jax.ShapeDtypeStruct(s_in.shape, s_in.dtype), jax.ShapeDtypeStruct(s_sq.shape, s_sq.dtype)),
        in_specs=[ANY] * 3, out_specs=(ANY,) * 3,
        scratch_shapes=[pltpu.SemaphoreType.DMA((9,)), pltpu.SemaphoreType.DMA((9,)), pltpu.SemaphoreType.DMA],
        compiler_params=pltpu.CompilerParams(has_side_effects=True),
    )(small, s_in, s_sq)


def _sum_devices(gs):
    NL = len(gs)
    _, R, D = gs[0].shape

    def body(*refs):
        o_ref = refs[NL]
        for l in range(NL):
            acc = refs[l][0]
            for k in range(1, 8):
                acc = acc + refs[l][k]
            o_ref[l] = acc

    return pl.pallas_call(
        body, name="sum_devices",
        out_shape=jax.ShapeDtypeStruct((NL, R, D), F32),
        grid=(1,),
        in_specs=[pl.BlockSpec((8, R, D), lambda i: (0, 0, 0))] * NL,
        out_specs=pl.BlockSpec((NL, R, D), lambda i: (0, 0, 0)),
        compiler_params=_cp(("arbitrary",)),
    )(*gs)


def kernel(x, pre_norm_g, w_in, w_ret_out, conv_w, conv_b, conv_ln_g, conv_ln_b, w_conv_out, w_o, post_norm_g, loss_target, m_pre_norm_g, m_w_in, m_w_ret_out, m_conv_w, m_conv_b, m_conv_ln_g, m_conv_ln_b, m_w_conv_out, m_w_o, m_post_norm_g, v_pre_norm_g, v_w_in, v_w_ret_out, v_conv_w, v_conv_b, v_conv_ln_g, v_conv_ln_b, v_w_conv_out, v_w_o, v_post_norm_g):
    NL, D, W = w_in.shape
    Cc = conv_w.shape[-1]
    T = x.shape[1]
    L = min(RET_BLOCK, T)
    tb = _tables(T, L)
    ax, ay, ac = _place()
    chip = (2 * ax + ay).astype(jnp.int32).reshape(1)
    cidx = ac.astype(jnp.int32).reshape(1)
    pre_g, cb, lg, lb, post_g = (a.reshape(NL, 1, D) for a in (pre_norm_g, conv_b, conv_ln_g, conv_ln_b, post_norm_g))

    win = [_cast_win(chip, w_in, l) for l in range(NL)]
    wsq = [_cast_wsq(chip, w_ret_out, w_conv_out, w_o, l) for l in range(NL)]
    win[0], wcw = _gather_first(win[0], _place_cw(chip, conv_w))

    saved = []
    xl = x[0]
    for l in range(NL):
        more = l + 1 < NL
        ride = _Ride()
        if more:
            _ride_gather_ici(ride, "win", win[l + 1], (0, 5, 8))
        (proj, h), got = _fwd_in(xl, pre_g, win[l], l, ride=ride)
        if more:
            win[l + 1] = got["win"]
        ride = _Ride()
        if l == 0:
            _ride_gather_ici(ride, "wsq0", wsq[0])
        (a_ret, states), got = _ret_fwd(proj, tb, L, ride=ride)
        if l == 0:
            wsq[0] = got["wsq0"]
        ride = _Ride()
        if more:
            _ride_gather_ici(ride, "win", win[l + 1], (5, 8, 8))
            _ride_gather_ici(ride, "wsq", wsq[l + 1])
        if l == 0:
            _ride_gather_pass(ride, "wsq0", wsq[0])
        (a_conv, y), got = _conv_fwd(proj, wcw, cb, lg, lb, l, ride=ride)
        if more:
            win[l + 1], wsq[l + 1] = got["win"], got["wsq"]
        if l == 0:
            wsq[0] = got["wsq0"]
        ride = _Ride()
        if more:
            _ride_gather_pass(ride, "win", win[l + 1])
            _ride_gather_pass(ride, "wsq", wsq[l + 1])
        (xn, ro, co, ym, z), got = _merge_fwd(xl, proj, a_ret, a_conv, wsq[l], post_g, l, ride=ride)
        if more:
            win[l + 1], wsq[l + 1] = got["win"], got["wsq"]
        saved.append((xl, proj, h, a_ret, states, a_conv, y, ro, co, ym, z))
        xl = xn
    dx, lsum = _loss_fwd_bwd(xl, loss_target[0])

    gin, gsq, uin, usq = [None] * NL, [None] * NL, [None] * NL, [None] * NL
    s_in, s_sq, o_in, o_sq = [None] * NL, [None] * NL, [None] * NL, [None] * NL
    small, gs = [None] * NL, [None] * NL
    for l in reversed(range(NL)):
        xin, proj, h, a_ret, states, a_conv, y, ro, co, ym, z = saved[l]
        (dproj, da_ret, da_conv, gsq[l], dpost), _ = _merge_bwd(dx, proj, a_ret, a_conv, ro, co, ym, z, wsq[l], post_g, l)
        ride = _Ride()
        if l + 1 < NL:
            _ride_exchange(ride, "gin", gin[l + 1], "uin", uin[l + 1],
                           [(2, FIRST_HALF), (0, SECOND_HALF), (1, SECOND_HALF)])
        (dproj, sg), got = _conv_bwd(dproj, da_conv, y, proj, wcw, lg, lb, l, ride=ride)
        if l + 1 < NL:
            uin[l + 1] = got["uin"]
        ride = _Ride()
        if l + 1 < NL:
            _ride_exchange(ride, "gin", gin[l + 1], "uin", uin[l + 1], [(2, SECOND_HALF)])
            _ride_gather_all(ride, "small", small[l + 1], "gs")
        (dproj,), got = _ret_bwd(dproj, da_ret, proj, states, tb, L, ride=ride)
        if l + 1 < NL:
            uin[l + 1], gs[l + 1] = got["uin"], got["gs"]
        ride = _Ride()
        _ride_exchange(ride, "gsq", gsq[l], "usq", None, [(0, WHOLE), (1, WHOLE), (2, WHOLE)])
        (gin[l],), got = _win_grad(h, dproj, W, ride=ride)
        usq[l] = got["usq"]
        ride = _Ride()
        if l > 0:
            _ride_exchange(ride, "gin", gin[l], "uin", None, [(0, FIRST_HALF), (1, FIRST_HALF)])
        else:
            gin[0] = _add_rows(cidx, gin[0], _swap_rows(gin[0]))
            _ride_exchange(ride, "gin", gin[0], "uin", None, [(0, WHOLE), (1, WHOLE), (2, WHOLE)])
        if l + 1 < NL:
            s_in[l + 1] = _sum_group(chip, gin[l + 1], uin[l + 1])
            s_sq[l + 1] = _sum_group(chip, gsq[l + 1], usq[l + 1])
            _ride_swap(ride, "s_in", s_in[l + 1], "o_in")
            _ride_swap(ride, "s_sq", s_sq[l + 1], "o_sq")
        (dx, dpre), got = _in_bwd(dx, dproj, xin, pre_g, win[l], l, ride=ride)
        uin[l] = got["uin"]
        if l + 1 < NL:
            o_in[l + 1], o_sq[l + 1] = got["o_in"], got["o_sq"]
        small[l] = jnp.concatenate([sg, dpre, dpost, lsum if l == NL - 1 else jnp.zeros_like(lsum)], axis=0)
    grad_x = dx

    big = {"w_in": None, "w_ret_out": None, "w_conv_out": None, "w_o": None}
    wts = {"w_in": (w_in, m_w_in, v_w_in), "w_ret_out": (w_ret_out, m_w_ret_out, v_w_ret_out),
           "w_conv_out": (w_conv_out, m_w_conv_out, v_w_conv_out), "w_o": (w_o, m_w_o, v_w_o)}
    sq_names = ("w_ret_out", "w_conv_out", "w_o")

    def adam_in(l, ride=None):
        big["w_in"], got = _adamw_layer(big["w_in"], *wts["w_in"], s_in[l], o_in[l], l, 0, ride=ride)
        return got

    def adam_sq(l, part, ride=None):
        n = sq_names[part]
        big[n], got = _adamw_layer(big[n], *wts[n], s_sq[l], o_sq[l], l, part, ride=ride)
        return got

    s_in[0] = _sum_group_half(chip, cidx, gin[0], uin[0])
    s_sq[0] = _sum_group(chip, gsq[0], usq[0])
    gs[0], o_in[0], o_sq[0] = _tail_exchange(small[0], s_in[0], s_sq[0])
    for l in reversed(range(NL)):
        adam_in(l)
        for part in range(3):
            adam_sq(l, part)

    gsm = _sum_devices(gs)
    loss = jnp.sum(gsm[NL - 1, ROW_LOSS:ROW_LOSS + 8])
    grads = {
        "pre_norm_g": gsm[:, ROW_PRE], "conv_w": lax.dynamic_slice_in_dim(gsm[:, 0:CONV_K], chip[0] * Cc, Cc, axis=2),
        "conv_b": gsm[:, ROW_CB], "conv_ln_g": gsm[:, ROW_LG], "conv_ln_b": gsm[:, ROW_LB], "post_norm_g": gsm[:, ROW_POST],
    }
    weights = dict(pre_norm_g=pre_norm_g, conv_w=conv_w, conv_b=conv_b, conv_ln_g=conv_ln_g, conv_ln_b=conv_ln_b,
                   post_norm_g=post_norm_g)
    m1 = dict(pre_norm_g=m_pre_norm_g, conv_w=m_conv_w, conv_b=m_conv_b, conv_ln_g=m_conv_ln_g, conv_ln_b=m_conv_ln_b,
              post_norm_g=m_post_norm_g)
    m2 = dict(pre_norm_g=v_pre_norm_g, conv_w=v_conv_w, conv_b=v_conv_b, conv_ln_g=v_conv_ln_g, conv_ln_b=v_conv_ln_b,
              post_norm_g=v_post_norm_g)
    res = {n: (grads[n],) + _adamw(weights[n], grads[n], m1[n], m2[n]) for n in grads}
    res.update(big)
    order = ["pre_norm_g", "w_in", "w_ret_out", "conv_w", "conv_b", "conv_ln_g", "conv_ln_b", "w_conv_out", "w_o", "post_norm_g"]
    return (loss, grad_x[None], *[res[n][0] for n in order], *[res[n][1] for n in order],
            *[res[n][2] for n in order], *[res[n][3] for n in order])
```

```python
import numpy as np
import jax
import jax.numpy as jnp
from jax import lax
from jax.experimental import pallas as pl
from jax.experimental.pallas import tpu as pltpu

F32 = jnp.float32
BF16 = jnp.bfloat16

HEADS = 8
DK = 64
DV = 128
CONV_K = 31
CHUNK = 64
ROPE_BASE = 10000.0
EPS = 1e-6
HALO = 32
CONV_RB = 32
CONV_LANES = 512
CONV_TILE = 256
RET_BLOCK = 512

ADAM_LR = 0.001
ADAM_B1 = 0.9
ADAM_B2 = 0.999
ADAM_EPS = 1e-08
ADAM_WD = 0.01
ADAM_STEP = 10
ADAM_BLOCK_BYTES = 2 * 1024 * 1024

VMEM_LIMIT = 56 * 1024 * 1024
MESH_T = pl.DeviceIdType.MESH
ANY = pl.BlockSpec(memory_space=pl.ANY)

ROW_CB, ROW_LG, ROW_LB = 32, 33, 34
ROW_PRE, ROW_POST, ROW_LOSS = 40, 48, 56


def _cp(sem=None, **kw):
    return pltpu.CompilerParams(dimension_semantics=sem, vmem_limit_bytes=VMEM_LIMIT, **kw)


def _dot(a, b):
    return jnp.dot(a, b, preferred_element_type=F32)


def _dot_nt(a, b):
    return lax.dot_general(a, b, (((1,), (1,)), ((), ())), preferred_element_type=F32)


def _dot_tn(a, b):
    return lax.dot_general(a, b, (((0,), (0,)), ((), ())), preferred_element_type=F32)


def _sigmoid(x):
    return jax.nn.sigmoid(x)


def _silu(x):
    return x * _sigmoid(x)


def _rms(x, g):
    return x * lax.rsqrt(jnp.mean(x * x, axis=-1, keepdims=True) + EPS) * g


def _gn_gate(o, g):
    mu = jnp.mean(o, axis=-1, keepdims=True)
    d = o - mu
    var = jnp.mean(d * d, axis=-1, keepdims=True)
    return d * lax.rsqrt(var + EPS) * _silu(g)


def _ln_gate(y, gc, lg, lb):
    mu = jnp.mean(y, axis=-1, keepdims=True)
    d = y - mu
    var = jnp.mean(d * d, axis=-1, keepdims=True)
    return _silu(d * lax.rsqrt(var + EPS) * lg + lb) * _silu(gc)


def _tables(T, L):
    lane = np.arange(128)
    d = lane % DK
    half = DK // 2
    inv = (ROPE_BASE ** (-(np.arange(half, dtype=np.float32)) / half)).astype(np.float32)
    ang = (np.arange(T, dtype=np.float32)[:, None] * inv[None, :]).astype(np.float64)
    angl = ang[:, d % half]
    cos = np.cos(angl)
    sin = np.sin(angl)
    lo = (d < half)[None, :]
    rope = np.stack([cos, np.where(lo, -sin, 0.0), np.where(lo, 0.0, sin)]).astype(np.float32)

    hh = np.arange(HEADS, dtype=np.float64)
    log_g = np.log1p(-np.exp2(-5.0 - hh))
    n = np.arange(L, dtype=np.float64)
    cn = np.arange(L) // CHUNK
    allowed = (cn[None, :] <= cn[:, None])
    dist = np.abs(n[:, None] - n[None, :])
    mask = np.exp(log_g[:, None, None] * dist[None]) * allowed[None]
    mq = ((lane[None, :] // DK) == (np.arange(HEADS)[:, None] % 2)).astype(np.float64)
    qd = np.exp(log_g[:, None] * n[None, :])
    kd = np.exp(log_g[:, None] * (L - n[None, :]))
    qdm = qd[:, :, None] * mq[:, None, :] * (DK ** -0.5)
    kdm = kd[:, :, None] * mq[:, None, :]
    mqs = np.broadcast_to((mq * (DK ** -0.5))[:, None, :], (HEADS, 8, 128))
    cd = np.broadcast_to(np.exp(log_g * L)[:, None, None], (HEADS, 8, 128))
    f = lambda a: jnp.asarray(np.ascontiguousarray(a), dtype=F32)
    return dict(rope=f(rope), mask=f(mask), qdm=f(qdm), kdm=f(kdm), mqs=f(mqs), cd=f(cd))


def _rot(b, c, sl, sh):
    return b * c + pltpu.roll(b, 96, axis=1) * sl + pltpu.roll(b, 32, axis=1) * sh


def _rot_t(d, c, sl, sh):
    return d * c + pltpu.roll(d * sl, 32, axis=1) + pltpu.roll(d * sh, 96, axis=1)


def _place():
    return lax.axis_index("x"), lax.axis_index("y"), lax.axis_index("c")


def _other_chips(x, y):
    return [(1 - x, y), (x, 1 - y), (1 - x, 1 - y)]


def _remote(src, dst, send_sems, recv_sems, k, to):
    return pltpu.make_async_remote_copy(src_ref=src, dst_ref=dst, send_sem=send_sems.at[k], recv_sem=recv_sems.at[k],
                                        device_id=to, device_id_type=MESH_T)


class _Ride:
    def __init__(self):
        self.arrays, self.kinds, self.names = [], [], []
        self.fresh = []
        self.ops = []

    def read(self, name, a):
        self.names.append(name)
        self.arrays.append(a)
        self.kinds.append("in")

    def inout(self, name, a):
        self.names.append(name)
        self.arrays.append(a)
        self.kinds.append("inout")

    def land(self, name, shape, dtype):
        self.fresh.append((name, jax.ShapeDtypeStruct(shape, dtype)))

    def op(self, n_sems, start, finish):
        self.ops.append((n_sems, start, finish))


def _pcall(body, *, name, grid, in_specs, out_specs, out_shape, args, scratch_shapes=(), sem, aliases=None, ride=None):
    if ride is None or not ride.ops:
        outs = pl.pallas_call(body, name=name, grid=grid, in_specs=list(in_specs), out_specs=tuple(out_specs),
                              out_shape=tuple(out_shape), scratch_shapes=list(scratch_shapes),
                              input_output_aliases=dict(aliases or {}), compiler_params=_cp(sem))(*args)
        return outs, {}

    ni, no, nr = len(args), len(out_shape), len(ride.arrays)
    inout = [i for i, k in enumerate(ride.kinds) if k == "inout"]
    r_out_shapes = [jax.ShapeDtypeStruct(ride.arrays[i].shape, ride.arrays[i].dtype) for i in inout] + [s for _, s in ride.fresh]
    r_out_names = [ride.names[i] for i in inout] + [n for n, _ in ride.fresh]
    nro = len(r_out_shapes)
    n_sems = sum(n for n, _, _ in ride.ops)
    n_scr = len(scratch_shapes)
    nd = len(grid)

    def wrapped(*refs):
        ins, rin = refs[:ni], refs[ni:ni + nr]
        outs, rout = refs[ni + nr:ni + nr + no], refs[ni + nr + no:ni + nr + no + nro]
        scr = refs[ni + nr + no + nro:ni + nr + no + nro + n_scr]
        send_sems, recv_sems = refs[-2], refs[-1]
        view = {nm: r for nm, r, k in zip(ride.names, rin, ride.kinds) if k == "in"}
        view.update(dict(zip(r_out_names, rout)))
        first = pl.program_id(0) == 0
        last = pl.program_id(0) == grid[0] - 1
        for d in range(1, nd):
            first = first & (pl.program_id(d) == 0)
            last = last & (pl.program_id(d) == grid[d] - 1)

        @pl.when(first)
        def _():
            base = 0
            for n, start, _ in ride.ops:
                start(view, send_sems, recv_sems, base)
                base += n

        body(*ins, *outs, *scr)

        @pl.when(last)
        def _():
            base = 0
            for n, _, finish in ride.ops:
                finish(view, send_sems, recv_sems, base)
                base += n

    res = pl.pallas_call(
        wrapped, name=name, grid=grid,
        in_specs=list(in_specs) + [ANY] * nr, out_specs=tuple(out_specs) + (ANY,) * nro,
        out_shape=tuple(out_shape) + tuple(r_out_shapes),
        scratch_shapes=list(scratch_shapes) + [pltpu.SemaphoreType.DMA((n_sems,)), pltpu.SemaphoreType.DMA((n_sems,))],
        input_output_aliases={**dict(aliases or {}), **{ni + i: no + j for j, i in enumerate(inout)}},
        compiler_params=_cp(sem),
    )(*args, *ride.arrays)
    return res[:no], dict(zip(r_out_names, res[no:]))


def _half(ref, chip_idx, cc, part=(0, 1, 1)):
    n = ref.shape[1] // 2
    lo, hi, k = part
    return ref.at[chip_idx, pl.ds(cc * n + lo * n // k, (hi - lo) * n // k)]


def _ride_gather_ici(ride, name, a, part=(0, 1, 1)):
    ride.inout(name, a)

    def start(view, ss, rs, b):
        x, y, c = _place()
        mine = _half(view[name], 2 * x + y, c, part)
        for j, (cx, cy) in enumerate(_other_chips(x, y)):
            _remote(mine, mine, ss, rs, b + j, (cx, cy, c)).start()

    def finish(view, ss, rs, b):
        x, y, c = _place()
        mine = _half(view[name], 2 * x + y, c, part)
        for j, (cx, cy) in enumerate(_other_chips(x, y)):
            theirs = _half(view[name], 2 * cx + cy, c, part)
            _remote(theirs, theirs, ss, rs, b + j, (cx, cy, c)).wait_recv()
        for j, (cx, cy) in enumerate(_other_chips(x, y)):
            _remote(mine, mine, ss, rs, b + j, (cx, cy, c)).wait_send()

    ride.op(3, start, finish)


def _ride_gather_pass(ride, name, a):
    ride.inout(name, a)

    def start(view, ss, rs, b):
        x, y, c = _place()
        for j, (cx, cy) in enumerate(_other_chips(x, y)):
            blk = _half(view[name], 2 * cx + cy, c)
            _remote(blk, blk, ss, rs, b + j, (x, y, 1 - c)).start()

    def finish(view, ss, rs, b):
        x, y, c = _place()
        for j, (cx, cy) in enumerate(_other_chips(x, y)):
            theirs = _half(view[name], 2 * cx + cy, 1 - c)
            _remote(theirs, theirs, ss, rs, b + j, (x, y, 1 - c)).wait_recv()
        for j, (cx, cy) in enumerate(_other_chips(x, y)):
            blk = _half(view[name], 2 * cx + cy, c)
            _remote(blk, blk, ss, rs, b + j, (x, y, 1 - c)).wait_send()

    ride.op(3, start, finish)


def _ride_exchange(ride, src_name, src, dst_name, dst, plan):
    ride.read(src_name, src)
    if dst is None:
        ride.land(dst_name, (3,) + src.shape[1:], src.dtype)
    else:
        ride.inout(dst_name, dst)
    A = src.shape[1]

    def copy(view, ss, rs, sem, j, part, x, y, c):
        lo, hi, k = part
        rows = pl.ds(lo * A // k, (hi - lo) * A // k)
        cx, cy = _other_chips(x, y)[j]
        return _remote(view[src_name].at[2 * cx + cy, rows], view[dst_name].at[j, rows], ss, rs, sem, (cx, cy, c))

    def start(view, ss, rs, b):
        x, y, c = _place()
        for i, (j, part) in enumerate(plan):
            copy(view, ss, rs, b + i, j, part, x, y, c).start()

    def finish(view, ss, rs, b):
        x, y, c = _place()
        for i, (j, part) in enumerate(plan):
            copy(view, ss, rs, b + i, j, part, x, y, c).wait()

    ride.op(len(plan), start, finish)


WHOLE, FIRST_HALF, SECOND_HALF = (0, 1, 1), (0, 1, 2), (1, 2, 2)


def _ride_swap(ride, src_name, src, dst_name):
    ride.read(src_name, src)
    ride.land(dst_name, src.shape, src.dtype)

    def start(view, ss, rs, b):
        x, y, c = _place()
        _remote(view[src_name], view[dst_name], ss, rs, b, (x, y, 1 - c)).start()

    def finish(view, ss, rs, b):
        x, y, c = _place()
        _remote(view[src_name], view[dst_name], ss, rs, b, (x, y, 1 - c)).wait()

    ride.op(1, start, finish)


def _peers(x, y, c):
    flip = lambda v, b: 1 - v if b else v
    return [(flip(x, r & 4), flip(y, r & 2), flip(c, r & 1)) for r in range(1, 8)]


def _ride_gather_all(ride, src_name, src, dst_name):
    ride.read(src_name, src)
    ride.land(dst_name, (8,) + src.shape, src.dtype)

    def start(view, ss, rs, b):
        x, y, c = _place()
        me = 4 * x + 2 * y + c
        pltpu.make_async_copy(view[src_name], view[dst_name].at[me], ss.at[b + 7]).start()
        for r, peer in enumerate(_peers(x, y, c)):
            _remote(view[src_name], view[dst_name].at[me], ss, rs, b + r, peer).start()

    def finish(view, ss, rs, b):
        x, y, c = _place()
        me = 4 * x + 2 * y + c
        for r, peer in enumerate(_peers(x, y, c)):
            theirs = view[dst_name].at[4 * peer[0] + 2 * peer[1] + peer[2]]
            _remote(theirs, theirs, ss, rs, b + r, peer).wait_recv()
        for r, peer in enumerate(_peers(x, y, c)):
            _remote(view[src_name], view[dst_name].at[me], ss, rs, b + r, peer).wait_send()
        pltpu.make_async_copy(view[src_name], view[dst_name].at[me], ss.at[b + 7]).wait()

    ride.op(8, start, finish)


def _cast_win(chip, w_in, l):
    _, D, W = w_in.shape
    tr = min(256, D)

    def body(chip_ref, w_ref, o_ref):
        o_ref[...] = w_ref[...].astype(BF16)

    return pl.pallas_call(
        body, name="cast_win",
        out_shape=jax.ShapeDtypeStruct((4, D, W), BF16),
        grid_spec=pltpu.PrefetchScalarGridSpec(
            num_scalar_prefetch=1, grid=(D // tr,),
            in_specs=[pl.BlockSpec((None, tr, W), lambda r, c: (l, r, 0))],
            out_specs=pl.BlockSpec((None, tr, W), lambda r, c: (c[0], r, 0))),
        compiler_params=_cp(("arbitrary",)),
    )(chip, w_in)


def _cast_wsq(chip, w_ro, w_co, w_o, l):
    _, R, D = w_ro.shape

    def body(chip_ref, a_ref, b_ref, c_ref, o_ref):
        o_ref[0:R, :] = a_ref[...].astype(BF16)
        o_ref[R:2 * R, :] = b_ref[...].astype(BF16)
        o_ref[2 * R:3 * R, :] = c_ref[...].astype(BF16)

    spec = pl.BlockSpec((None, R, D), lambda i, c: (l, 0, 0))
    return pl.pallas_call(
        body, name="cast_wsq",
        out_shape=jax.ShapeDtypeStruct((4, 3 * R, D), BF16),
        grid_spec=pltpu.PrefetchScalarGridSpec(
            num_scalar_prefetch=1, grid=(1,),
            in_specs=[spec, spec, spec],
            out_specs=pl.BlockSpec((None, 3 * R, D), lambda i, c: (c[0], 0, 0))),
        compiler_params=_cp(("arbitrary",)),
    )(chip, w_ro, w_co, w_o)


def _place_cw(chip, conv_w):
    NL, K, Cc = conv_w.shape

    def body(chip_ref, w_ref, o_ref):
        o_ref[...] = w_ref[...]

    return pl.pallas_call(
        body, name="place_cw",
        out_shape=jax.ShapeDtypeStruct((4, NL, K, Cc), F32),
        grid_spec=pltpu.PrefetchScalarGridSpec(
            num_scalar_prefetch=1, grid=(1,),
            in_specs=[pl.BlockSpec((NL, K, Cc), lambda i, c: (0, 0, 0))],
            out_specs=pl.BlockSpec((None, NL, K, Cc), lambda i, c: (c[0], 0, 0, 0))),
        compiler_params=_cp(("arbitrary",)),
    )(chip, conv_w)


def _gather_first(win0, wcw):
    n_arr = 2

    def body(a0, a1, o0, o1, send_sems, recv_sems):
        x, y, c = _place()
        sibling = (x, y, 1 - c)
        chips = _other_chips(x, y)
        outs = (o0, o1)

        def copy(k, a, cx, cy, cc, to):
            blk = _half(outs[a], 2 * cx + cy, cc)
            return _remote(blk, blk, send_sems, recv_sems, k, to)

        first = [copy(3 * a + j, a, x, y, c, (*chip, c)) for a in range(n_arr) for j, chip in enumerate(chips)]
        for cp in first:
            cp.start()
        passed = [copy(3 * n_arr + 3 * a + j, a, *chip, c, sibling) for a in range(n_arr) for j, chip in enumerate(chips)]
        for a in range(n_arr):
            for j, chip in enumerate(chips):
                copy(3 * a + j, a, *chip, c, sibling).wait_recv()
                passed[3 * a + j].start()
        for a in range(n_arr):
            for j, chip in enumerate(chips):
                copy(3 * n_arr + 3 * a + j, a, *chip, 1 - c, sibling).wait_recv()
        for cp in first + passed:
            cp.wait_send()

    ins = (win0, wcw)
    return pl.pallas_call(
        body, name="gather_first",
        out_shape=tuple(jax.ShapeDtypeStruct(a.shape, a.dtype) for a in ins),
        in_specs=[ANY] * n_arr, out_specs=(ANY,) * n_arr,
        scratch_shapes=[pltpu.SemaphoreType.DMA((6 * n_arr,)), pltpu.SemaphoreType.DMA((6 * n_arr,))],
        input_output_aliases={0: 0, 1: 1},
        compiler_params=pltpu.CompilerParams(has_side_effects=True),
    )(*ins)


def _fwd_in(x, pre_g, win, l, ride=None):
    T, D = x.shape
    W = win.shape[-1]
    tm = min(512, T)

    def body(x_ref, g_ref, w_ref, p_ref, h_ref):
        hb = _rms(x_ref[...], g_ref[...]).astype(BF16)
        h_ref[...] = hb
        for j in range(4):
            p_ref[:, j * W:(j + 1) * W] = _dot(hb, w_ref[j]).astype(BF16)

    return _pcall(
        body, name="fwd_in",
        out_shape=(jax.ShapeDtypeStruct((T, 4 * W), BF16), jax.ShapeDtypeStruct((T, D), BF16)),
        grid=(T // tm,),
        in_specs=[pl.BlockSpec((tm, D), lambda i: (i, 0)),
                  pl.BlockSpec((None, 1, D), lambda i: (l, 0, 0)),
                  pl.BlockSpec((4, D, W), lambda i: (0, 0, 0), pipeline_mode=pl.Buffered(1))],
        out_specs=(pl.BlockSpec((tm, 4 * W), lambda i: (i, 0)),
                   pl.BlockSpec((tm, D), lambda i: (i, 0))),
        sem=("arbitrary",), args=(x, pre_g, win), ride=ride)


def _ret_specs(T, L):
    rope = pl.BlockSpec((3, L, 128), lambda s: (0, s, 0))
    mask = pl.BlockSpec((HEADS, L, L), lambda s: (0, 0, 0), pipeline_mode=pl.Buffered(1))
    qdm = pl.BlockSpec((HEADS, L, 128), lambda s: (0, 0, 0), pipeline_mode=pl.Buffered(1))
    small = pl.BlockSpec((HEADS, 8, 128), lambda s: (0, 0, 0))
    return [rope, mask, qdm, qdm, small, small]


QK = HEADS * DK
VW = HEADS * DV
PW = 2 * QK + 2 * VW


def _zero_at_start(ref):
    @pl.when(pl.program_id(0) == 0)
    def _():
        ref[...] = jnp.zeros_like(ref)


def _ret_fwd_part(p_ref, rope_ref, m_ref, qdm_ref, kdm_ref, mqs_ref, cd_ref, a_ref, st_ref, state):
    c, sl, sh = rope_ref[0], rope_ref[1], rope_ref[2]
    for j in range(HEADS // 2):
        rq = _rot(p_ref[:, 128 * j:128 * (j + 1)].astype(F32), c, sl, sh)
        rk = _rot(p_ref[:, QK + 128 * j:QK + 128 * (j + 1)].astype(F32), c, sl, sh)
        rkb = rk.astype(BF16)
        for e in range(2):
            h = 2 * j + e
            v = p_ref[:, 2 * QK + DV * h:2 * QK + DV * (h + 1)]
            g = p_ref[:, 2 * QK + VW + DV * h:2 * QK + VW + DV * (h + 1)].astype(F32)
            a = (rq * mqs_ref[h, 0:1, :]).astype(BF16)
            p = (_dot_nt(a, rkb) * m_ref[h]).astype(BF16)
            st = state[h]
            st_ref[h] = st
            o = _dot(p, v) + _dot((rq * qdm_ref[h]).astype(BF16), st.astype(BF16))
            state[h] = st * cd_ref[h, 0:1, :] + _dot_tn((rk * kdm_ref[h]).astype(BF16), v)
            a_ref[:, DV * h:DV * (h + 1)] = _gn_gate(o, g).astype(BF16)


def _shift_copies(src, sh):
    rows = sh.shape[1]
    for b in range(1, 8):
        sh[b - 1, :, :] = src[pl.ds(b, rows), :]


def _window(src, sh, r0, const, rows, lanes):
    b = const % 8
    at = pl.ds(pl.multiple_of(r0 + (const - b), 8), rows)
    w = src[at, lanes] if b == 0 else sh[b - 1, at, lanes]
    return w.reshape(rows // 8, 8, w.shape[-1])


def _conv_taps(wbuf, src, sh, r0, const, rows, lanes):
    groups = rows // 8
    accs = [None] * groups
    for k in range(CONV_K):
        w8 = wbuf[pl.ds(8 * k, 8), lanes]
        win = _window(src, sh, r0, const + k, rows, lanes)
        for g in range(groups):
            term = w8 * win[g]
            accs[g] = term if k == 0 else accs[g] + term
    return jnp.concatenate(accs, axis=0)


def _lane_parts(C):
    return [pl.ds(j * CONV_LANES, CONV_LANES) for j in range(C // CONV_LANES)]


def _load_conv_w(cw_ref, wbuf, flip):
    for k in range(CONV_K):
        row = jnp.concatenate([cw_ref[c, pl.ds(k, 1), :] for c in range(4)], axis=-1)
        kk = CONV_K - 1 - k if flip else k
        wbuf[pl.ds(8 * kk, 8), :] = jnp.broadcast_to(row, (8, row.shape[-1]))


def _conv_fwd_part(p_ref, cw_ref, cb_ref, lg_ref, lb_ref, a_ref, y_ref, ubuf, wbuf, ush):
    tc, C = y_ref.shape
    off = HALO - (CONV_K - 1)
    i = pl.program_id(0)

    @pl.when(i == 0)
    def _():
        ubuf[0:HALO, :] = jnp.zeros((HALO, C), F32)
        _load_conv_w(cw_ref, wbuf, False)

    @pl.when(i > 0)
    def _():
        ubuf[0:HALO, :] = ubuf[tc:tc + HALO, :]

    ga = p_ref[:, 0:C].astype(F32)
    gb = p_ref[:, C:2 * C].astype(F32)
    ubuf[HALO:HALO + tc, :] = ga * _sigmoid(gb)
    _shift_copies(ubuf, ush)

    def rows_block(r, carry):
        r0 = pl.multiple_of(r * CONV_RB, CONV_RB)
        for lanes in _lane_parts(C):
            y_ref[pl.ds(r0, CONV_RB), lanes] = _conv_taps(wbuf, ubuf, ush, r0, off, CONV_RB, lanes) + cb_ref[:, lanes]
        return carry

    lax.fori_loop(0, tc // CONV_RB, rows_block, 0)
    gc = p_ref[:, 2 * C:3 * C].astype(F32)
    a_ref[...] = _ln_gate(y_ref[...], gc, lg_ref[...], lb_ref[...]).astype(BF16)


def _ret_fwd(proj, tb, L, ride=None):
    T = proj.shape[0]
    nS = T // L

    def body(p_ref, rope_ref, m_ref, qdm_ref, kdm_ref, mqs_ref, cd_ref, ar_ref, st_ref, state):
        _zero_at_start(state)
        _ret_fwd_part(p_ref, rope_ref, m_ref, qdm_ref, kdm_ref, mqs_ref, cd_ref, ar_ref, st_ref, state)

    return _pcall(
        body, name="ret_fwd",
        out_shape=(jax.ShapeDtypeStruct((T, VW), BF16), jax.ShapeDtypeStruct((nS, HEADS, 128, DV), F32)),
        grid=(nS,),
        in_specs=[pl.BlockSpec((L, PW), lambda s: (s, 0))] + _ret_specs(T, L),
        out_specs=(pl.BlockSpec((L, VW), lambda s: (s, 0)), pl.BlockSpec((None, HEADS, 128, DV), lambda s: (s, 0, 0, 0))),
        scratch_shapes=[pltpu.VMEM((HEADS, 128, DV), F32)],
        sem=("arbitrary",), ride=ride,
        args=(proj, tb["rope"], tb["mask"], tb["qdm"], tb["kdm"], tb["mqs"], tb["cd"]))


def _conv_fwd(proj, wcw, conv_b, ln_g, ln_b, l, ride=None):
    T = proj.shape[0]
    C = conv_b.shape[-1]
    Cc = wcw.shape[-1]
    tc = min(CONV_TILE, T)
    assert PW == 3 * C

    def body(p_ref, cw_ref, cb_ref, lg_ref, lb_ref, ac_ref, y_ref, ubuf, wbuf, ush):
        _conv_fwd_part(p_ref, cw_ref, cb_ref, lg_ref, lb_ref, ac_ref, y_ref, ubuf, wbuf, ush)

    vec = pl.BlockSpec((None, 1, C), lambda i: (l, 0, 0))
    tile = pl.BlockSpec((tc, C), lambda i: (i, 0))
    return _pcall(
        body, name="conv_fwd",
        out_shape=(jax.ShapeDtypeStruct((T, C), BF16), jax.ShapeDtypeStruct((T, C), F32)),
        grid=(T // tc,),
        in_specs=[pl.BlockSpec((tc, 3 * C), lambda i: (i, 1)),
                  pl.BlockSpec((4, None, CONV_K, Cc), lambda i: (0, l, 0, 0)), vec, vec, vec],
        out_specs=(tile, tile),
        scratch_shapes=[pltpu.VMEM((HALO + tc, C), F32), pltpu.VMEM((8 * CONV_K, C), F32),
                        pltpu.VMEM((7, HALO + tc - 8, C), F32)],
        sem=("arbitrary",), ride=ride, args=(proj, wcw, conv_b, ln_g, ln_b))


def _merge_fwd(x, proj, a_ret, a_conv, wsq, post_g, l, ride=None):
    T, D = x.shape
    R = wsq.shape[1] // 3
    tm = min(512, T)

    def body(x_ref, p_ref, ar_ref, ac_ref, wro_ref, wco_ref, wo_ref, g_ref, xn_ref, ro_ref, co_ref, ym_ref, z_ref):
        ro = _dot(ar_ref[...], wro_ref[...].reshape(4 * R, D))
        co = _dot(ac_ref[...], wco_ref[...].reshape(4 * R, D))
        ym = (_sigmoid(p_ref[:, 0:D].astype(F32)) * ro + _sigmoid(p_ref[:, D:2 * D].astype(F32)) * co).astype(BF16)
        z = _dot(ym, wo_ref[...].reshape(4 * R, D))
        ro_ref[...] = ro.astype(BF16)
        co_ref[...] = co.astype(BF16)
        ym_ref[...] = ym
        z_ref[...] = z.astype(BF16)
        xn_ref[...] = x_ref[...] + _rms(z, g_ref[...])

    tile = pl.BlockSpec((tm, D), lambda i: (i, 0))
    wspec = lambda m: pl.BlockSpec((4, R, D), lambda i: (0, m, 0))
    act = jax.ShapeDtypeStruct((T, D), BF16)
    return _pcall(
        body, name="merge_fwd",
        out_shape=(jax.ShapeDtypeStruct((T, D), F32), act, act, act, act),
        grid=(T // tm,),
        in_specs=[tile, pl.BlockSpec((tm, 2 * D), lambda i: (i, 3)), tile, tile,
                  wspec(0), wspec(1), wspec(2), pl.BlockSpec((None, 1, D), lambda i: (l, 0, 0))],
        out_specs=(tile, tile, tile, tile, tile),
        sem=("arbitrary",), ride=ride, args=(x, proj, a_ret, a_conv, wsq, wsq, wsq, post_g))


def _loss_fwd_bwd(y, target):
    T, D = y.shape
    tm = min(512, T)

    def body(y_ref, t_ref, dy_ref, ls_ref):
        @pl.when(pl.program_id(0) == 0)
        def _():
            ls_ref[...] = jnp.zeros_like(ls_ref)

        e = y_ref[...] - t_ref[...]
        dy_ref[...] = e * (1.0 / D)
        ls_ref[...] += jnp.sum((e * e).reshape(tm // 8, 8, D), axis=0) * (0.5 / D)

    tile = pl.BlockSpec((tm, D), lambda i: (i, 0))
    return pl.pallas_call(
        body, name="loss",
        out_shape=(jax.ShapeDtypeStruct((T, D), F32), jax.ShapeDtypeStruct((8, D), F32)),
        grid=(T // tm,),
        in_specs=[tile, tile],
        out_specs=(tile, pl.BlockSpec((8, D), lambda i: (0, 0))),
        compiler_params=_cp(("arbitrary",)),
    )(y, target)


def _merge_bwd(dxn, proj, a_ret, a_conv, ro, co, ym, z, wsq, post_g, l, ride=None):
    T, D = dxn.shape
    R = wsq.shape[1] // 3
    tm = min(512, T)
    n = T // tm

    def body(dx_ref, p_ref, ar_ref, ac_ref, ro_ref, co_ref, ym_ref, z_ref, wro_ref, wco_ref, wo_ref, g_ref,
             dp_ref, dar_ref, dac_ref, gsq_ref, dg_ref, acc, stage):
        i = pl.program_id(0)

        @pl.when(i == 0)
        def _():
            acc[...] = jnp.zeros_like(acc)
            dg_ref[...] = jnp.zeros_like(dg_ref)

        _, vjp = jax.vjp(_rms, z_ref[...].astype(F32), g_ref[...])
        dz, dg = vjp(dx_ref[...])
        dg_ref[0:1, :] += dg
        dzb = dz.astype(BF16)
        dym = _dot_nt(dzb, wo_ref[...].reshape(4 * R, D))
        acc[2] += _dot_tn(ym_ref[...], dzb)
        sr = _sigmoid(p_ref[:, 0:D].astype(F32))
        sc = _sigmoid(p_ref[:, D:2 * D].astype(F32))
        rov = ro_ref[...].astype(F32)
        cov = co_ref[...].astype(F32)
        dp_ref[:, 0:D] = (dym * rov * sr * (1.0 - sr)).astype(BF16)
        dp_ref[:, D:2 * D] = (dym * cov * sc * (1.0 - sc)).astype(BF16)
        dro = (dym * sr).astype(BF16)
        dco = (dym * sc).astype(BF16)
        dar_ref[...] = _dot_nt(dro, wro_ref[...].reshape(4 * R, D)).astype(BF16)
        dac_ref[...] = _dot_nt(dco, wco_ref[...].reshape(4 * R, D)).astype(BF16)
        acc[0] += _dot_tn(ar_ref[...], dro)
        acc[1] += _dot_tn(ac_ref[...], dco)

        @pl.when(i == n - 1)
        def _():
            for m in range(3):
                stage[...] = acc[m].astype(BF16).reshape(4, R, D)
                pltpu.sync_copy(stage, gsq_ref.at[:, pl.ds(m * R, R), :])

    tile = pl.BlockSpec((tm, D), lambda i: (i, 0))
    wspec = lambda m: pl.BlockSpec((4, R, D), lambda i: (0, m, 0), pipeline_mode=pl.Buffered(1))
    return _pcall(
        body, name="merge_bwd",
        out_shape=(jax.ShapeDtypeStruct(proj.shape, BF16), jax.ShapeDtypeStruct((T, D), BF16),
                   jax.ShapeDtypeStruct((T, D), BF16), jax.ShapeDtypeStruct(wsq.shape, BF16),
                   jax.ShapeDtypeStruct((8, D), F32)),
        grid=(n,),
        in_specs=[tile, pl.BlockSpec((tm, 2 * D), lambda i: (i, 3)), tile, tile, tile, tile, tile, tile,
                  wspec(0), wspec(1), wspec(2), pl.BlockSpec((None, 1, D), lambda i: (l, 0, 0))],
        out_specs=(pl.BlockSpec((tm, 2 * D), lambda i: (i, 3)), tile, tile, ANY, pl.BlockSpec((8, D), lambda i: (0, 0))),
        scratch_shapes=[pltpu.VMEM((3, 4 * R, D), F32), pltpu.VMEM((4, R, D), BF16)],
        sem=("arbitrary",), args=(dxn, proj, a_ret, a_conv, ro, co, ym, z, wsq, wsq, wsq, post_g), ride=ride)


def _conv_bwd_part(n, da_ref, y_ref, p_ref, ph_ref, cw_ref, lg_ref, lb_ref, dp_ref,
                   dcbuf, ubuf, dubuf, wbuf, dwacc, vacc, dsh, ush):
    tc, C = y_ref.shape
    off = HALO - (CONV_K - 1)
    nrb = tc // CONV_RB
    t = pl.program_id(0)
    i = n - 1 - t

    @pl.when(t == 0)
    def _():
        dcbuf[tc:tc + HALO, :] = jnp.zeros((HALO, C), F32)
        dwacc[...] = jnp.zeros_like(dwacc)
        vacc[...] = jnp.zeros_like(vacc)
        _load_conv_w(cw_ref, wbuf, True)

    @pl.when(t > 0)
    def _():
        dcbuf[tc:tc + HALO, :] = dcbuf[0:HALO, :]

    gc = p_ref[:, 2 * C:3 * C].astype(F32)
    _, vjp = jax.vjp(_ln_gate, y_ref[...], gc, lg_ref[...], lb_ref[...])
    dy, dgc, dlg, dlb = vjp(da_ref[...].astype(F32))
    dcbuf[0:tc, :] = dy
    dp_ref[:, 2 * C:3 * C] = dgc.astype(BF16)
    vacc[0:1, :] += jnp.sum(dy, axis=0, keepdims=True)
    vacc[1:2, :] += dlg
    vacc[2:3, :] += dlb

    ga = p_ref[:, 0:C].astype(F32)
    sb = _sigmoid(p_ref[:, C:2 * C].astype(F32))
    ubuf[HALO:HALO + tc, :] = ga * sb
    uh = ph_ref[:, 0:C].astype(F32) * _sigmoid(ph_ref[:, C:2 * C].astype(F32))
    ubuf[0:HALO, :] = jnp.where(i > 0, uh, 0.0)

    _shift_copies(dcbuf, dsh)
    _shift_copies(ubuf, ush)
    def du_block(r, carry):
        r0 = pl.multiple_of(r * CONV_RB, CONV_RB)
        for lanes in _lane_parts(C):
            dubuf[pl.ds(r0, CONV_RB), lanes] = _conv_taps(wbuf, dcbuf, dsh, r0, 0, CONV_RB, lanes)
        return carry

    lax.fori_loop(0, nrb, du_block, 0)
    du = dubuf[...]
    dp_ref[:, 0:C] = (du * sb).astype(BF16)
    dp_ref[:, C:2 * C] = (du * ga * sb * (1.0 - sb)).astype(BF16)

    def dw_block(r, carry):
        r0 = pl.multiple_of(r * CONV_RB, CONV_RB)
        for lanes in _lane_parts(C):
            dyb = dcbuf[pl.ds(r0, CONV_RB), lanes].reshape(CONV_RB // 8, 8, CONV_LANES)
            for k in range(CONV_K):
                dwacc[8 * k:8 * k + 8, lanes] += jnp.sum(dyb * _window(ubuf, ush, r0, off + k, CONV_RB, lanes), axis=0)
        return carry

    lax.fori_loop(0, nrb, dw_block, 0)


def _conv_bwd_final(n, sg_ref, dwacc, vacc):
    C = sg_ref.shape[-1]

    @pl.when(pl.program_id(0) == n - 1)
    def _():
        for k in range(CONV_K):
            sg_ref[pl.ds(k, 1), :] = jnp.sum(dwacc[8 * k:8 * k + 8, :], axis=0, keepdims=True)
        sg_ref[pl.ds(CONV_K, 1), :] = jnp.zeros((1, C), F32)
        sg_ref[ROW_CB:ROW_CB + 8, :] = jnp.zeros((8, C), F32)
        sg_ref[ROW_CB:ROW_CB + 3, :] = vacc[0:3, :]


def _ret_bwd_part(da_ref, p_ref, st_ref, rope_ref, m_ref, qdm_ref, kdm_ref, mqs_ref, cd_ref, dp_ref, gst):
    c, sl, sh = rope_ref[0], rope_ref[1], rope_ref[2]
    for j in range(HEADS // 2):
        rq = _rot(p_ref[:, 128 * j:128 * (j + 1)].astype(F32), c, sl, sh)
        rk = _rot(p_ref[:, QK + 128 * j:QK + 128 * (j + 1)].astype(F32), c, sl, sh)
        rkb = rk.astype(BF16)
        drq = jnp.zeros_like(rq)
        drk = jnp.zeros_like(rk)
        for e in range(2):
            h = 2 * j + e
            v = p_ref[:, 2 * QK + DV * h:2 * QK + DV * (h + 1)]
            g = p_ref[:, 2 * QK + VW + DV * h:2 * QK + VW + DV * (h + 1)].astype(F32)
            mqs = mqs_ref[h, 0:1, :]
            a = (rq * mqs).astype(BF16)
            aq = (rq * qdm_ref[h]).astype(BF16)
            kdv = (rk * kdm_ref[h]).astype(BF16)
            mk = m_ref[h]
            p = (_dot_nt(a, rkb) * mk).astype(BF16)
            stb = st_ref[h].astype(BF16)
            o = _dot(p, v) + _dot(aq, stb)
            _, vjp = jax.vjp(_gn_gate, o, g)
            do, dg = vjp(da_ref[:, DV * h:DV * (h + 1)].astype(F32))
            dob = do.astype(BF16)
            gs = gst[h]
            gsb = gs.astype(BF16)
            ds = (_dot_nt(dob, v) * mk).astype(BF16)
            drq = drq + _dot(ds, rkb) * mqs + _dot_nt(dob, stb) * qdm_ref[h]
            drk = drk + _dot_tn(ds, a) + _dot_nt(v, gsb) * kdm_ref[h]
            dv = _dot_tn(p, dob) + _dot(kdv, gsb)
            gst[h] = _dot_tn(aq, dob) + gs * cd_ref[h, 0:1, :]
            dp_ref[:, 2 * QK + DV * h:2 * QK + DV * (h + 1)] = dv.astype(BF16)
            dp_ref[:, 2 * QK + VW + DV * h:2 * QK + VW + DV * (h + 1)] = dg.astype(BF16)
        dp_ref[:, 128 * j:128 * (j + 1)] = _rot_t(drq, c, sl, sh).astype(BF16)
        dp_ref[:, QK + 128 * j:QK + 128 * (j + 1)] = _rot_t(drk, c, sl, sh).astype(BF16)


def _ret_bwd(dproj, da_ret, proj, states, tb, L, ride=None):
    T = proj.shape[0]
    nS = T // L

    def body(dpin_ref, dar_ref, p_ref, st_ref, rope_ref, m_ref, qdm_ref, kdm_ref, mqs_ref, cd_ref, dp_ref, gst):
        _zero_at_start(gst)
        _ret_bwd_part(dar_ref, p_ref, st_ref, rope_ref, m_ref, qdm_ref, kdm_ref, mqs_ref, cd_ref, dp_ref, gst)

    rev = lambda s: nS - 1 - s
    specs = _ret_specs(T, L)
    specs[0] = pl.BlockSpec((3, L, 128), lambda s: (0, rev(s), 0))
    ptile = pl.BlockSpec((L, PW), lambda s: (rev(s), 0))
    return _pcall(
        body, name="ret_bwd",
        out_shape=(jax.ShapeDtypeStruct(dproj.shape, BF16),),
        grid=(nS,),
        in_specs=[ANY, pl.BlockSpec((L, VW), lambda s: (rev(s), 0)), ptile,
                  pl.BlockSpec((None, HEADS, 128, DV), lambda s: (rev(s), 0, 0, 0))] + specs,
        out_specs=(ptile,),
        scratch_shapes=[pltpu.VMEM((HEADS, 128, DV), F32)],
        sem=("arbitrary",), aliases={0: 0}, ride=ride,
        args=(dproj, da_ret, proj, states, tb["rope"], tb["mask"], tb["qdm"], tb["kdm"], tb["mqs"], tb["cd"]))


def _conv_bwd(dproj, da_conv, y, proj, wcw, ln_g, ln_b, l, ride=None):
    T, C = y.shape
    Cc = wcw.shape[-1]
    tc = min(CONV_TILE, T)
    n = T // tc
    hb = tc // HALO

    def body(dpin_ref, dac_ref, y_ref, p_ref, ph_ref, cw_ref, lg_ref, lb_ref, dp_ref, sg_ref,
             dcbuf, ubuf, dubuf, wbuf, dwacc, vacc, dsh, ush):
        _conv_bwd_part(n, dac_ref, y_ref, p_ref, ph_ref, cw_ref, lg_ref, lb_ref, dp_ref,
                       dcbuf, ubuf, dubuf, wbuf, dwacc, vacc, dsh, ush)
        _conv_bwd_final(n, sg_ref, dwacc, vacc)

    rev = lambda t: n - 1 - t
    vec = pl.BlockSpec((None, 1, C), lambda t: (l, 0, 0))
    tile = pl.BlockSpec((tc, C), lambda t: (rev(t), 0))
    ptile = pl.BlockSpec((tc, 3 * C), lambda t: (rev(t), 1))
    halo = pl.BlockSpec((HALO, 3 * C), lambda t: (jnp.maximum(rev(t) * hb - 1, 0), 1))
    return _pcall(
        body, name="conv_bwd",
        out_shape=(jax.ShapeDtypeStruct(dproj.shape, BF16), jax.ShapeDtypeStruct((ROW_PRE, C), F32)),
        grid=(n,),
        in_specs=[ANY, tile, tile, ptile, halo, pl.BlockSpec((4, None, CONV_K, Cc), lambda t: (0, l, 0, 0)), vec, vec],
        out_specs=(ptile, pl.BlockSpec((ROW_PRE, C), lambda t: (0, 0))),
        scratch_shapes=[pltpu.VMEM((tc + HALO, C), F32), pltpu.VMEM((HALO + tc, C), F32), pltpu.VMEM((tc, C), F32),
                        pltpu.VMEM((8 * CONV_K, C), F32), pltpu.VMEM((8 * CONV_K, C), F32), pltpu.VMEM((8, C), F32),
                        pltpu.VMEM((7, HALO + tc - 8, C), F32), pltpu.VMEM((7, HALO + tc - 8, C), F32)],
        sem=("arbitrary",), aliases={0: 0}, ride=ride, args=(dproj, da_conv, y, proj, proj, wcw, ln_g, ln_b))


def _win_grad(h, dproj, W, ride=None):
    T, D = h.shape
    tk = min(2048, T)
    nk = T // tk

    def body(h_ref, dp_ref, g_ref, acc):
        k = pl.program_id(1)

        @pl.when(k == 0)
        def _():
            acc[...] = jnp.zeros_like(acc)

        acc[...] += _dot_tn(h_ref[...], dp_ref[...])

        @pl.when(k == nk - 1)
        def _():
            g_ref[...] = acc[...].astype(BF16)

    return _pcall(
        body, name="win_grad",
        out_shape=(jax.ShapeDtypeStruct((4, D, W), BF16),),
        grid=(4, nk),
        in_specs=[pl.BlockSpec((tk, D), lambda j, k: (k, 0)), pl.BlockSpec((tk, W), lambda j, k: (k, j))],
        out_specs=(pl.BlockSpec((None, D, W), lambda j, k: (j, 0, 0)),),
        scratch_shapes=[pltpu.VMEM((D, W), F32)],
        sem=("arbitrary", "arbitrary"), args=(h, dproj), ride=ride)


def _in_bwd(dxn, dproj, x, pre_g, win, l, ride=None):
    T, D = x.shape
    W = win.shape[-1]
    tm = min(512, T)

    def body(dxn_ref, dp_ref, x_ref, g_ref, w_ref, dx_ref, dg_ref):
        @pl.when(pl.program_id(0) == 0)
        def _():
            dg_ref[...] = jnp.zeros_like(dg_ref)

        dh = _dot_nt(dp_ref[:, 0:W], w_ref[0])
        for j in range(1, 4):
            dh = dh + _dot_nt(dp_ref[:, j * W:(j + 1) * W], w_ref[j])
        _, vjp = jax.vjp(_rms, x_ref[...], g_ref[...])
        dx, dg = vjp(dh)
        dx_ref[...] = dxn_ref[...] + dx
        dg_ref[0:1, :] += dg

    tile = pl.BlockSpec((tm, D), lambda i: (i, 0))
    return _pcall(
        body, name="in_bwd",
        out_shape=(jax.ShapeDtypeStruct((T, D), F32), jax.ShapeDtypeStruct((8, D), F32)),
        grid=(T // tm,),
        in_specs=[tile, pl.BlockSpec((tm, 4 * W), lambda i: (i, 0)), tile,
                  pl.BlockSpec((None, 1, D), lambda i: (l, 0, 0)),
                  pl.BlockSpec((4, D, W), lambda i: (0, 0, 0), pipeline_mode=pl.Buffered(1))],
        out_specs=(tile, pl.BlockSpec((8, D), lambda i: (0, 0))),
        sem=("arbitrary",), args=(dxn, dproj, x, pre_g, win), ride=ride)


def _sum_group(chip, t, u):
    _, A, B = t.shape
    tr = min(256, A)

    def body(k_ref, t_ref, u_ref, o_ref):
        o_ref[...] = ((t_ref[...].astype(F32) + u_ref[0].astype(F32)) + u_ref[1].astype(F32)) + u_ref[2].astype(F32)

    return pl.pallas_call(
        body, name="sum_group",
        out_shape=jax.ShapeDtypeStruct((A, B), F32),
        grid_spec=pltpu.PrefetchScalarGridSpec(
            num_scalar_prefetch=1, grid=(A // tr,),
            in_specs=[pl.BlockSpec((None, tr, B), lambda i, k: (k[0], i, 0)),
                      pl.BlockSpec((3, tr, B), lambda i, k: (0, i, 0))],
            out_specs=pl.BlockSpec((tr, B), lambda i, k: (i, 0))),
        compiler_params=_cp(("arbitrary",)),
    )(chip, t, u)


def _swap_rows(g):
    _, A, B = g.shape
    nh = A // 2

    def body(g_ref, r_ref, send_sems, recv_sems):
        x, y, c = _place()
        cp = _remote(g_ref.at[:, pl.ds((1 - c) * nh, nh)], r_ref, send_sems, recv_sems, 0, (x, y, 1 - c))
        cp.start()
        cp.wait()

    return pl.pallas_call(
        body, name="swap_rows",
        out_shape=jax.ShapeDtypeStruct((4, nh, B), g.dtype),
        in_specs=[ANY], out_specs=ANY,
        scratch_shapes=[pltpu.SemaphoreType.DMA((1,)), pltpu.SemaphoreType.DMA((1,))],
        compiler_params=pltpu.CompilerParams(has_side_effects=True),
    )(g)


def _add_rows(cidx, g, r):
    _, nh, B = r.shape
    tr = min(256, nh)
    nb = nh // tr

    def body(c_ref, g_ref, r_ref, o_ref):
        o_ref[...] = (g_ref[...].astype(F32) + r_ref[...].astype(F32)).astype(BF16)

    blk = (None, tr, B)
    return pl.pallas_call(
        body, name="add_rows",
        out_shape=jax.ShapeDtypeStruct(r.shape, BF16),
        grid_spec=pltpu.PrefetchScalarGridSpec(
            num_scalar_prefetch=1, grid=(4, nb),
            in_specs=[pl.BlockSpec(blk, lambda k, i, c: (k, c[0] * nb + i, 0)),
                      pl.BlockSpec(blk, lambda k, i, c: (k, i, 0))],
            out_specs=pl.BlockSpec(blk, lambda k, i, c: (k, i, 0))),
        compiler_params=_cp(("arbitrary", "arbitrary")),
    )(cidx, g, r)


def _sum_group_half(chip, cidx, t, u):
    _, nh, B = t.shape
    tr = min(256, nh)
    nb = nh // tr

    def body(k_ref, c_ref, t_ref, u_ref, o_ref):
        mine = (pl.program_id(0) // nb) == c_ref[0]

        @pl.when(mine)
        def _():
            o_ref[...] = ((t_ref[...].astype(F32) + u_ref[0].astype(F32)) + u_ref[1].astype(F32)) + u_ref[2].astype(F32)

        @pl.when(jnp.logical_not(mine))
        def _():
            o_ref[...] = jnp.zeros_like(o_ref)

    own = lambda i, c: jnp.clip(i - c[0] * nb, 0, nb - 1)
    return pl.pallas_call(
        body, name="sum_group_half",
        out_shape=jax.ShapeDtypeStruct((2 * nh, B), F32),
        grid_spec=pltpu.PrefetchScalarGridSpec(
            num_scalar_prefetch=2, grid=(2 * nb,),
            in_specs=[pl.BlockSpec((None, tr, B), lambda i, k, c: (k[0], own(i, c), 0)),
                      pl.BlockSpec((3, tr, B), lambda i, k, c: (0, own(i, c), 0))],
            out_specs=pl.BlockSpec((tr, B), lambda i, k, c: (i, 0))),
        compiler_params=_cp(("arbitrary",)),
    )(chip, cidx, t, u)


def _adam_math(w, g, m, v):
    c1 = 1.0 / (1.0 - ADAM_B1 ** ADAM_STEP)
    c2 = 1.0 / (1.0 - ADAM_B2 ** ADAM_STEP)
    nm = ADAM_B1 * m + (1.0 - ADAM_B1) * g
    nv = ADAM_B2 * v + (1.0 - ADAM_B2) * (g * g)
    return -ADAM_LR * ((nm * c1) / (jnp.sqrt(nv * c2) + ADAM_EPS) + ADAM_WD * w), nm, nv


def _adamw_layer(prev, w, m, v, sa, sb, l, part, ride=None):
    NL, A, B = w.shape
    tr = A
    while tr * B * 4 > ADAM_BLOCK_BYTES and tr % 16 == 0:
        tr //= 2
    nb = A // tr

    def body(p0, p1, p2, p3, w_ref, m_ref, v_ref, sa_ref, sb_ref, g_ref, d_ref, nm_ref, nv_ref):
        g = sa_ref[...] + sb_ref[...]
        g_ref[...] = g
        d_ref[...], nm_ref[...], nv_ref[...] = _adam_math(w_ref[...], g, m_ref[...], v_ref[...])

    lay = pl.BlockSpec((None, tr, B), lambda i: (l, i, 0))
    src = pl.BlockSpec((tr, B), lambda i: (part * nb + i, 0))
    full = jax.ShapeDtypeStruct((NL, A, B), F32)
    if prev is None:
        prev = tuple(lax.empty((NL, A, B), F32) for _ in range(4))
    outs, landed = _pcall(
        body, name="adamw_layer",
        out_shape=(full,) * 4, grid=(nb,),
        in_specs=[ANY] * 4 + [lay, lay, lay, src, src], out_specs=(lay,) * 4,
        sem=("arbitrary",), aliases={0: 0, 1: 1, 2: 2, 3: 3}, args=(*prev, w, m, v, sa, sb), ride=ride)
    return tuple(outs), landed


def _adamw(w, g, m, v):
    shape = w.shape
    cols = shape[-1]
    rows = int(np.prod(shape[:-1]))

    def body(w_ref, g_ref, m_ref, v_ref, d_ref, nm_ref, nv_ref):
        d_ref[...], nm_ref[...], nv_ref[...] = _adam_math(w_ref[...], g_ref[...], m_ref[...], v_ref[...])

    tile = pl.BlockSpec((rows, cols), lambda i: (0, 0))
    out = jax.ShapeDtypeStruct((rows, cols), F32)
    res = pl.pallas_call(
        body, name="adamw",
        out_shape=(out, out, out), grid=(1,),
        in_specs=[tile] * 4, out_specs=(tile,) * 3,
        compiler_params=_cp(("arbitrary",)),
    )(*[a.reshape(rows, cols) for a in (w, g, m, v)])
    return tuple(a.reshape(shape) for a in res)


def _tail_exchange(small, s_in, s_sq):
    def body(s_ref, a_ref, b_ref, o_ref, oa_ref, ob_ref, send_sems, recv_sems, local_sem):
        x, y, c = _place()
        me = 4 * x + 2 * y + c
        sibling = (x, y, 1 - c)
        mine = pltpu.make_async_copy(s_ref, o_ref.at[me], local_sem)
        mine.start()
        swaps = [_remote(a_ref, oa_ref, send_sems, recv_sems, 7, sibling), _remote(b_ref, ob_ref, send_sems, recv_sems, 8, sibling)]
        sends = [_remote(s_ref, o_ref.at[me], send_sems, recv_sems, r, peer) for r, peer in enumerate(_peers(x, y, c))]
        for cp in swaps + sends:
            cp.start()
        for r, peer in enumerate(_peers(x, y, c)):
            theirs = o_ref.at[4 * peer[0] + 2 * peer[1] + peer[2]]
            _remote(theirs, theirs, send_sems, recv_sems, r, peer).wait_recv()
        for cp in sends:
            cp.wait_send()
        for cp in swaps:
            cp.wait()
        mine.wait()

    return pl.pallas_call(
        body, name="tail_exchange",
        out_shape=(jax.ShapeDtypeStruct((8,) + small.shape, small.dtype),
                   jax.ShapeDtypeStruct(s_in.shape, s_in.dtype), jax.ShapeDtypeStruct(s_sq.shape, s_sq.dtype)),
        in_specs=[ANY] * 3, out_specs=(ANY,) * 3,
        scratch_shapes=[pltpu.SemaphoreType.DMA((9,)), pltpu.SemaphoreType.DMA((9,)), pltpu.SemaphoreType.DMA],
        compiler_params=pltpu.CompilerParams(has_side_effects=True),
    )(small, s_in, s_sq)


def _sum_devices(gs):
    NL = len(gs)
    _, R, D = gs[0].shape

    def body(*refs):
        o_ref = refs[NL]
        for l in range(NL):
            acc = refs[l][0]
            for k in range(1, 8):
                acc = acc + refs[l][k]
            o_ref[l] = acc

    return pl.pallas_call(
        body, name="sum_devices",
        out_shape=jax.ShapeDtypeStruct((NL, R, D), F32),
        grid=(1,),
        in_specs=[pl.BlockSpec((8, R, D), lambda i: (0, 0, 0))] * NL,
        out_specs=pl.BlockSpec((NL, R, D), lambda i: (0, 0, 0)),
        compiler_params=_cp(("arbitrary",)),
    )(*gs)


def kernel(x, pre_norm_g, w_in, w_ret_out, conv_w, conv_b, conv_ln_g, conv_ln_b, w_conv_out, w_o, post_norm_g, loss_target, m_pre_norm_g, m_w_in, m_w_ret_out, m_conv_w, m_conv_b, m_conv_ln_g, m_conv_ln_b, m_w_conv_out, m_w_o, m_post_norm_g, v_pre_norm_g, v_w_in, v_w_ret_out, v_conv_w, v_conv_b, v_conv_ln_g, v_conv_ln_b, v_w_conv_out, v_w_o, v_post_norm_g):
    NL, D, W = w_in.shape
    Cc = conv_w.shape[-1]
    T = x.shape[1]
    L = min(RET_BLOCK, T)
    tb = _tables(T, L)
    ax, ay, ac = _place()
    chip = (2 * ax + ay).astype(jnp.int32).reshape(1)
    cidx = ac.astype(jnp.int32).reshape(1)
    pre_g, cb, lg, lb, post_g = (a.reshape(NL, 1, D) for a in (pre_norm_g, conv_b, conv_ln_g, conv_ln_b, post_norm_g))

    win = [_cast_win(chip, w_in, l) for l in range(NL)]
    wsq = [_cast_wsq(chip, w_ret_out, w_conv_out, w_o, l) for l in range(NL)]
    win[0], wcw = _gather_first(win[0], _place_cw(chip, conv_w))

    saved = []
    xl = x[0]
    for l in range(NL):
        more = l + 1 < NL
        ride = _Ride()
        if more:
            _ride_gather_ici(ride, "win", win[l + 1], (0, 5, 8))
        (proj, h), got = _fwd_in(xl, pre_g, win[l], l, ride=ride)
        if more:
            win[l + 1] = got["win"]
        ride = _Ride()
        if l == 0:
            _ride_gather_ici(ride, "wsq0", wsq[0])
        (a_ret, states), got = _ret_fwd(proj, tb, L, ride=ride)
        if l == 0:
            wsq[0] = got["wsq0"]
        ride = _Ride()
        if more:
            _ride_gather_ici(ride, "win", win[l + 1], (5, 8, 8))
            _ride_gather_ici(ride, "wsq", wsq[l + 1])
        if l == 0:
            _ride_gather_pass(ride, "wsq0", wsq[0])
        (a_conv, y), got = _conv_fwd(proj, wcw, cb, lg, lb, l, ride=ride)
        if more:
            win[l + 1], wsq[l + 1] = got["win"], got["wsq"]
        if l == 0:
            wsq[0] = got["wsq0"]
        ride = _Ride()
        if more:
            _ride_gather_pass(ride, "win", win[l + 1])
            _ride_gather_pass(ride, "wsq", wsq[l + 1])
        (xn, ro, co, ym, z), got = _merge_fwd(xl, proj, a_ret, a_conv, wsq[l], post_g, l, ride=ride)
        if more:
            win[l + 1], wsq[l + 1] = got["win"], got["wsq"]
        saved.append((xl, proj, h, a_ret, states, a_conv, y, ro, co, ym, z))
        xl = xn
    dx, lsum = _loss_fwd_bwd(xl, loss_target[0])

    gin, gsq, uin, usq = [None] * NL, [None] * NL, [None] * NL, [None] * NL
    s_in, s_sq, o_in, o_sq = [None] * NL, [None] * NL, [None] * NL, [None] * NL
    small, gs = [None] * NL, [None] * NL
    for l in reversed(range(NL)):
        xin, proj, h, a_ret, states, a_conv, y, ro, co, ym, z = saved[l]
        (dproj, da_ret, da_conv, gsq[l], dpost), _ = _merge_bwd(dx, proj, a_ret, a_conv, ro, co, ym, z, wsq[l], post_g, l)
        ride = _Ride()
        if l + 1 < NL:
            _ride_exchange(ride, "gin", gin[l + 1], "uin", uin[l + 1],
                           [(2, WHOLE), (0, SECOND_HALF), (1, SECOND_HALF)])
        (dproj, sg), got = _conv_bwd(dproj, da_conv, y, proj, wcw, lg, lb, l, ride=ride)
        if l + 1 < NL:
            uin[l + 1] = got["uin"]
        ride = _Ride()
        _ride_exchange(ride, "gsq", gsq[l], "usq", None, [(0, WHOLE), (1, WHOLE), (2, WHOLE)])
        if l + 1 < NL:
            _ride_gather_all(ride, "small", small[l + 1], "gs")
        (dproj,), got = _ret_bwd(dproj, da_ret, proj, states, tb, L, ride=ride)
        usq[l] = got["usq"]
        if l + 1 < NL:
            gs[l + 1] = got["gs"]
        (gin[l],), _ = _win_grad(h, dproj, W)
        ride = _Ride()
        if l > 0:
            _ride_exchange(ride, "gin", gin[l], "uin", None, [(0, FIRST_HALF), (1, FIRST_HALF)])
        else:
            gin[0] = _add_rows(cidx, gin[0], _swap_rows(gin[0]))
            _ride_exchange(ride, "gin", gin[0], "uin", None, [(0, WHOLE), (1, WHOLE), (2, WHOLE)])
        if l + 1 < NL:
            s_in[l + 1] = _sum_group(chip, gin[l + 1], uin[l + 1])
            s_sq[l + 1] = _sum_group(chip, gsq[l + 1], usq[l + 1])
            _ride_swap(ride, "s_in", s_in[l + 1], "o_in")
            _ride_swap(ride, "s_sq", s_sq[l + 1], "o_sq")
        (dx, dpre), got = _in_bwd(dx, dproj, xin, pre_g, win[l], l, ride=ride)
        uin[l] = got["uin"]
        if l + 1 < NL:
            o_in[l + 1], o_sq[l + 1] = got["o_in"], got["o_sq"]
        small[l] = jnp.concatenate([sg, dpre, dpost, lsum if l == NL - 1 else jnp.zeros_like(lsum)], axis=0)
    grad_x = dx

    big = {"w_in": None, "w_ret_out": None, "w_conv_out": None, "w_o": None}
    wts = {"w_in": (w_in, m_w_in, v_w_in), "w_ret_out": (w_ret_out, m_w_ret_out, v_w_ret_out),
           "w_conv_out": (w_conv_out, m_w_conv_out, v_w_conv_out), "w_o": (w_o, m_w_o, v_w_o)}
    sq_names = ("w_ret_out", "w_conv_out", "w_o")

    def adam_in(l, ride=None):
        big["w_in"], got = _adamw_layer(big["w_in"], *wts["w_in"], s_in[l], o_in[l], l, 0, ride=ride)
        return got

    def adam_sq(l, part, ride=None):
        n = sq_names[part]
        big[n], got = _adamw_layer(big[n], *wts[n], s_sq[l], o_sq[l], l, part, ride=ride)
        return got

    s_in[0] = _sum_group_half(chip, cidx, gin[0], uin[0])
    s_sq[0] = _sum_group(chip, gsq[0], usq[0])
    gs[0], o_in[0], o_sq[0] = _tail_exchange(small[0], s_in[0], s_sq[0])
    for l in reversed(range(NL)):
        adam_in(l)
        for part in range(3):
            adam_sq(l, part)

    gsm = _sum_devices(gs)
    loss = jnp.sum(gsm[NL - 1, ROW_LOSS:ROW_LOSS + 8])
    grads = {
        "pre_norm_g": gsm[:, ROW_PRE], "conv_w": lax.dynamic_slice_in_dim(gsm[:, 0:CONV_K], chip[0] * Cc, Cc, axis=2),
        "conv_b": gsm[:, ROW_CB], "conv_ln_g": gsm[:, ROW_LG], "conv_ln_b": gsm[:, ROW_LB], "post_norm_g": gsm[:, ROW_POST],
    }
    weights = dict(pre_norm_g=pre_norm_g, conv_w=conv_w, conv_b=conv_b, conv_ln_g=conv_ln_g, conv_ln_b=conv_ln_b,
                   post_norm_g=post_norm_g)
    m1 = dict(pre_norm_g=m_pre_norm_g, conv_w=m_conv_w, conv_b=m_conv_b, conv_ln_g=m_conv_ln_g, conv_ln_b=m_conv_ln_b,
              post_norm_g=m_post_norm_g)
    m2 = dict(pre_norm_g=v_pre_norm_g, conv_w=v_conv_w, conv_b=v_conv_b, conv_ln_g=v_conv_ln_g, conv_ln_b=v_conv_ln_b,
              post_norm_g=v_post_norm_g)
    res = {n: (grads[n],) + _adamw(weights[n], grads[n], m1[n], m2[n]) for n in grads}
    res.update(big)
    order = ["pre_norm_g", "w_in", "w_ret_out", "conv_w", "conv_b", "conv_ln_g", "conv_ln_b", "w_conv_out", "w_o", "post_norm_g"]
    return (loss, grad_x[None], *[res[n][0] for n in order], *[res[n][1] for n in order],
            *[res[n][2] for n in order], *[res[n][3] for n in order])
```

```python
import numpy as np
import jax
import jax.numpy as jnp
from jax import lax
from jax.experimental import pallas as pl
from jax.experimental.pallas import tpu as pltpu

F32 = jnp.float32
BF16 = jnp.bfloat16

HEADS = 8
DK = 64
DV = 128
CONV_K = 31
CHUNK = 64
ROPE_BASE = 10000.0
EPS = 1e-6
HALO = 32
CONV_RB = 32
CONV_LANES = 512
CONV_TILE = 256
RET_BLOCK = 512

ADAM_LR = 0.001
ADAM_B1 = 0.9
ADAM_B2 = 0.999
ADAM_EPS = 1e-08
ADAM_WD = 0.01
ADAM_STEP = 10
ADAM_BLOCK_BYTES = 2 * 1024 * 1024

VMEM_LIMIT = 56 * 1024 * 1024
MESH_T = pl.DeviceIdType.MESH
ANY = pl.BlockSpec(memory_space=pl.ANY)

ROW_CB, ROW_LG, ROW_LB = 32, 33, 34
ROW_PRE, ROW_POST, ROW_LOSS = 40, 48, 56


def _cp(sem=None, **kw):
    return pltpu.CompilerParams(dimension_semantics=sem, vmem_limit_bytes=VMEM_LIMIT, **kw)


def _dot(a, b):
    return jnp.dot(a, b, preferred_element_type=F32)


def _dot_nt(a, b):
    return lax.dot_general(a, b, (((1,), (1,)), ((), ())), preferred_element_type=F32)


def _dot_tn(a, b):
    return lax.dot_general(a, b, (((0,), (0,)), ((), ())), preferred_element_type=F32)


def _sigmoid(x):
    return jax.nn.sigmoid(x)


def _silu(x):
    return x * _sigmoid(x)


def _rms(x, g):
    return x * lax.rsqrt(jnp.mean(x * x, axis=-1, keepdims=True) + EPS) * g


def _gn_gate(o, g):
    mu = jnp.mean(o, axis=-1, keepdims=True)
    d = o - mu
    var = jnp.mean(d * d, axis=-1, keepdims=True)
    return d * lax.rsqrt(var + EPS) * _silu(g)


def _ln_gate(y, gc, lg, lb):
    mu = jnp.mean(y, axis=-1, keepdims=True)
    d = y - mu
    var = jnp.mean(d * d, axis=-1, keepdims=True)
    return _silu(d * lax.rsqrt(var + EPS) * lg + lb) * _silu(gc)


def _tables(T, L):
    lane = np.arange(128)
    d = lane % DK
    half = DK // 2
    inv = (ROPE_BASE ** (-(np.arange(half, dtype=np.float32)) / half)).astype(np.float32)
    ang = (np.arange(T, dtype=np.float32)[:, None] * inv[None, :]).astype(np.float64)
    angl = ang[:, d % half]
    cos = np.cos(angl)
    sin = np.sin(angl)
    lo = (d < half)[None, :]
    rope = np.stack([cos, np.where(lo, -sin, 0.0), np.where(lo, 0.0, sin)]).astype(np.float32)

    hh = np.arange(HEADS, dtype=np.float64)
    log_g = np.log1p(-np.exp2(-5.0 - hh))
    n = np.arange(L, dtype=np.float64)
    cn = np.arange(L) // CHUNK
    allowed = (cn[None, :] <= cn[:, None])
    dist = np.abs(n[:, None] - n[None, :])
    mask = np.exp(log_g[:, None, None] * dist[None]) * allowed[None]
    mq = ((lane[None, :] // DK) == (np.arange(HEADS)[:, None] % 2)).astype(np.float64)
    qd = np.exp(log_g[:, None] * n[None, :])
    kd = np.exp(log_g[:, None] * (L - n[None, :]))
    qdm = qd[:, :, None] * mq[:, None, :] * (DK ** -0.5)
    kdm = kd[:, :, None] * mq[:, None, :]
    mqs = np.broadcast_to((mq * (DK ** -0.5))[:, None, :], (HEADS, 8, 128))
    cd = np.broadcast_to(np.exp(log_g * L)[:, None, None], (HEADS, 8, 128))
    f = lambda a: jnp.asarray(np.ascontiguousarray(a), dtype=F32)
    return dict(rope=f(rope), mask=f(mask), qdm=f(qdm), kdm=f(kdm), mqs=f(mqs), cd=f(cd))


def _rot(b, c, sl, sh):
    return b * c + pltpu.roll(b, 96, axis=1) * sl + pltpu.roll(b, 32, axis=1) * sh


def _rot_t(d, c, sl, sh):
    return d * c + pltpu.roll(d * sl, 32, axis=1) + pltpu.roll(d * sh, 96, axis=1)


def _place():
    return lax.axis_index("x"), lax.axis_index("y"), lax.axis_index("c")


def _other_chips(x, y):
    return [(1 - x, y), (x, 1 - y), (1 - x, 1 - y)]


def _remote(src, dst, send_sems, recv_sems, k, to):
    return pltpu.make_async_remote_copy(src_ref=src, dst_ref=dst, send_sem=send_sems.at[k], recv_sem=recv_sems.at[k],
                                        device_id=to, device_id_type=MESH_T)


class _Ride:
    def __init__(self):
        self.arrays, self.kinds, self.names = [], [], []
        self.fresh = []
        self.ops = []

    def read(self, name, a):
        self.names.append(name)
        self.arrays.append(a)
        self.kinds.append("in")

    def inout(self, name, a):
        self.names.append(name)
        self.arrays.append(a)
        self.kinds.append("inout")

    def land(self, name, shape, dtype):
        self.fresh.append((name, jax.ShapeDtypeStruct(shape, dtype)))

    def op(self, n_sems, start, finish):
        self.ops.append((n_sems, start, finish))


def _pcall(body, *, name, grid, in_specs, out_specs, out_shape, args, scratch_shapes=(), sem, aliases=None, ride=None):
    if ride is None or not ride.ops:
        outs = pl.pallas_call(body, name=name, grid=grid, in_specs=list(in_specs), out_specs=tuple(out_specs),
                              out_shape=tuple(out_shape), scratch_shapes=list(scratch_shapes),
                              input_output_aliases=dict(aliases or {}), compiler_params=_cp(sem))(*args)
        return outs, {}

    ni, no, nr = len(args), len(out_shape), len(ride.arrays)
    inout = [i for i, k in enumerate(ride.kinds) if k == "inout"]
    r_out_shapes = [jax.ShapeDtypeStruct(ride.arrays[i].shape, ride.arrays[i].dtype) for i in inout] + [s for _, s in ride.fresh]
    r_out_names = [ride.names[i] for i in inout] + [n for n, _ in ride.fresh]
    nro = len(r_out_shapes)
    n_sems = sum(n for n, _, _ in ride.ops)
    n_scr = len(scratch_shapes)
    nd = len(grid)

    def wrapped(*refs):
        ins, rin = refs[:ni], refs[ni:ni + nr]
        outs, rout = refs[ni + nr:ni + nr + no], refs[ni + nr + no:ni + nr + no + nro]
        scr = refs[ni + nr + no + nro:ni + nr + no + nro + n_scr]
        send_sems, recv_sems = refs[-2], refs[-1]
        view = {nm: r for nm, r, k in zip(ride.names, rin, ride.kinds) if k == "in"}
        view.update(dict(zip(r_out_names, rout)))
        first = pl.program_id(0) == 0
        last = pl.program_id(0) == grid[0] - 1
        for d in range(1, nd):
            first = first & (pl.program_id(d) == 0)
            last = last & (pl.program_id(d) == grid[d] - 1)

        @pl.when(first)
        def _():
            base = 0
            for n, start, _ in ride.ops:
                start(view, send_sems, recv_sems, base)
                base += n

        body(*ins, *outs, *scr)

        @pl.when(last)
        def _():
            base = 0
            for n, _, finish in ride.ops:
                finish(view, send_sems, recv_sems, base)
                base += n

    res = pl.pallas_call(
        wrapped, name=name, grid=grid,
        in_specs=list(in_specs) + [ANY] * nr, out_specs=tuple(out_specs) + (ANY,) * nro,
        out_shape=tuple(out_shape) + tuple(r_out_shapes),
        scratch_shapes=list(scratch_shapes) + [pltpu.SemaphoreType.DMA((n_sems,)), pltpu.SemaphoreType.DMA((n_sems,))],
        input_output_aliases={**dict(aliases or {}), **{ni + i: no + j for j, i in enumerate(inout)}},
        compiler_params=_cp(sem),
    )(*args, *ride.arrays)
    return res[:no], dict(zip(r_out_names, res[no:]))


def _half(ref, chip_idx, cc, part=(0, 1, 1)):
    n = ref.shape[1] // 2
    lo, hi, k = part
    return ref.at[chip_idx, pl.ds(cc * n + lo * n // k, (hi - lo) * n // k)]


def _ride_gather_ici(ride, name, a, part=(0, 1, 1)):
    ride.inout(name, a)

    def start(view, ss, rs, b):
        x, y, c = _place()
        mine = _half(view[name], 2 * x + y, c, part)
        for j, (cx, cy) in enumerate(_other_chips(x, y)):
            _remote(mine, mine, ss, rs, b + j, (cx, cy, c)).start()

    def finish(view, ss, rs, b):
        x, y, c = _place()
        mine = _half(view[name], 2 * x + y, c, part)
        for j, (cx, cy) in enumerate(_other_chips(x, y)):
            theirs = _half(view[name], 2 * cx + cy, c, part)
            _remote(theirs, theirs, ss, rs, b + j, (cx, cy, c)).wait_recv()
        for j, (cx, cy) in enumerate(_other_chips(x, y)):
            _remote(mine, mine, ss, rs, b + j, (cx, cy, c)).wait_send()

    ride.op(3, start, finish)


def _ride_gather_pass(ride, name, a):
    ride.inout(name, a)

    def start(view, ss, rs, b):
        x, y, c = _place()
        for j, (cx, cy) in enumerate(_other_chips(x, y)):
            blk = _half(view[name], 2 * cx + cy, c)
            _remote(blk, blk, ss, rs, b + j, (x, y, 1 - c)).start()

    def finish(view, ss, rs, b):
        x, y, c = _place()
        for j, (cx, cy) in enumerate(_other_chips(x, y)):
            theirs = _half(view[name], 2 * cx + cy, 1 - c)
            _remote(theirs, theirs, ss, rs, b + j, (x, y, 1 - c)).wait_recv()
        for j, (cx, cy) in enumerate(_other_chips(x, y)):
            blk = _half(view[name], 2 * cx + cy, c)
            _remote(blk, blk, ss, rs, b + j, (x, y, 1 - c)).wait_send()

    ride.op(3, start, finish)


def _ride_exchange(ride, src_name, src, dst_name, dst, plan):
    ride.read(src_name, src)
    if dst is None:
        ride.land(dst_name, (3,) + src.shape[1:], src.dtype)
    else:
        ride.inout(dst_name, dst)
    A = src.shape[1]

    def copy(view, ss, rs, sem, j, part, x, y, c):
        lo, hi, k = part
        rows = pl.ds(lo * A // k, (hi - lo) * A // k)
        cx, cy = _other_chips(x, y)[j]
        return _remote(view[src_name].at[2 * cx + cy, rows], view[dst_name].at[j, rows], ss, rs, sem, (cx, cy, c))

    def start(view, ss, rs, b):
        x, y, c = _place()
        for i, (j, part) in enumerate(plan):
            copy(view, ss, rs, b + i, j, part, x, y, c).start()

    def finish(view, ss, rs, b):
        x, y, c = _place()
        for i, (j, part) in enumerate(plan):
            copy(view, ss, rs, b + i, j, part, x, y, c).wait()

    ride.op(len(plan), start, finish)


WHOLE, FIRST_HALF, SECOND_HALF = (0, 1, 1), (0, 1, 2), (1, 2, 2)


def _ride_swap(ride, src_name, src, dst_name):
    ride.read(src_name, src)
    ride.land(dst_name, src.shape, src.dtype)

    def start(view, ss, rs, b):
        x, y, c = _place()
        _remote(view[src_name], view[dst_name], ss, rs, b, (x, y, 1 - c)).start()

    def finish(view, ss, rs, b):
        x, y, c = _place()
        _remote(view[src_name], view[dst_name], ss, rs, b, (x, y, 1 - c)).wait()

    ride.op(1, start, finish)


def _peers(x, y, c):
    flip = lambda v, b: 1 - v if b else v
    return [(flip(x, r & 4), flip(y, r & 2), flip(c, r & 1)) for r in range(1, 8)]


def _ride_gather_all(ride, src_name, src, dst_name):
    ride.read(src_name, src)
    ride.land(dst_name, (8,) + src.shape, src.dtype)

    def start(view, ss, rs, b):
        x, y, c = _place()
        me = 4 * x + 2 * y + c
        pltpu.make_async_copy(view[src_name], view[dst_name].at[me], ss.at[b + 7]).start()
        for r, peer in enumerate(_peers(x, y, c)):
            _remote(view[src_name], view[dst_name].at[me], ss, rs, b + r, peer).start()

    def finish(view, ss, rs, b):
        x, y, c = _place()
        me = 4 * x + 2 * y + c
        for r, peer in enumerate(_peers(x, y, c)):
            theirs = view[dst_name].at[4 * peer[0] + 2 * peer[1] + peer[2]]
            _remote(theirs, theirs, ss, rs, b + r, peer).wait_recv()
        for r, peer in enumerate(_peers(x, y, c)):
            _remote(view[src_name], view[dst_name].at[me], ss, rs, b + r, peer).wait_send()
        pltpu.make_async_copy(view[src_name], view[dst_name].at[me], ss.at[b + 7]).wait()

    ride.op(8, start, finish)


def _cast_win(chip, w_in, l):
    _, D, W = w_in.shape
    tr = min(256, D)

    def body(chip_ref, w_ref, o_ref):
        o_ref[...] = w_ref[...].astype(BF16)

    return pl.pallas_call(
        body, name="cast_win",
        out_shape=jax.ShapeDtypeStruct((4, D, W), BF16),
        grid_spec=pltpu.PrefetchScalarGridSpec(
            num_scalar_prefetch=1, grid=(D // tr,),
            in_specs=[pl.BlockSpec((None, tr, W), lambda r, c: (l, r, 0))],
            out_specs=pl.BlockSpec((None, tr, W), lambda r, c: (c[0], r, 0))),
        compiler_params=_cp(("arbitrary",)),
    )(chip, w_in)


def _cast_wsq(chip, w_ro, w_co, w_o, l):
    _, R, D = w_ro.shape

    def body(chip_ref, a_ref, b_ref, c_ref, o_ref):
        o_ref[0:R, :] = a_ref[...].astype(BF16)
        o_ref[R:2 * R, :] = b_ref[...].astype(BF16)
        o_ref[2 * R:3 * R, :] = c_ref[...].astype(BF16)

    spec = pl.BlockSpec((None, R, D), lambda i, c: (l, 0, 0))
    return pl.pallas_call(
        body, name="cast_wsq",
        out_shape=jax.ShapeDtypeStruct((4, 3 * R, D), BF16),
        grid_spec=pltpu.PrefetchScalarGridSpec(
            num_scalar_prefetch=1, grid=(1,),
            in_specs=[spec, spec, spec],
            out_specs=pl.BlockSpec((None, 3 * R, D), lambda i, c: (c[0], 0, 0))),
        compiler_params=_cp(("arbitrary",)),
    )(chip, w_ro, w_co, w_o)


def _place_cw(chip, conv_w):
    NL, K, Cc = conv_w.shape

    def body(chip_ref, w_ref, o_ref):
        o_ref[...] = w_ref[...]

    return pl.pallas_call(
        body, name="place_cw",
        out_shape=jax.ShapeDtypeStruct((4, NL, K, Cc), F32),
        grid_spec=pltpu.PrefetchScalarGridSpec(
            num_scalar_prefetch=1, grid=(1,),
            in_specs=[pl.BlockSpec((NL, K, Cc), lambda i, c: (0, 0, 0))],
            out_specs=pl.BlockSpec((None, NL, K, Cc), lambda i, c: (c[0], 0, 0, 0))),
        compiler_params=_cp(("arbitrary",)),
    )(chip, conv_w)


def _gather_first(win0, wcw):
    n_arr = 2

    def body(a0, a1, o0, o1, send_sems, recv_sems):
        x, y, c = _place()
        sibling = (x, y, 1 - c)
        chips = _other_chips(x, y)
        outs = (o0, o1)

        def copy(k, a, cx, cy, cc, to):
            blk = _half(outs[a], 2 * cx + cy, cc)
            return _remote(blk, blk, send_sems, recv_sems, k, to)

        first = [copy(3 * a + j, a, x, y, c, (*chip, c)) for a in range(n_arr) for j, chip in enumerate(chips)]
        for cp in first:
            cp.start()
        passed = [copy(3 * n_arr + 3 * a + j, a, *chip, c, sibling) for a in range(n_arr) for j, chip in enumerate(chips)]
        for a in range(n_arr):
            for j, chip in enumerate(chips):
                copy(3 * a + j, a, *chip, c, sibling).wait_recv()
                passed[3 * a + j].start()
        for a in range(n_arr):
            for j, chip in enumerate(chips):
                copy(3 * n_arr + 3 * a + j, a, *chip, 1 - c, sibling).wait_recv()
        for cp in first + passed:
            cp.wait_send()

    ins = (win0, wcw)
    return pl.pallas_call(
        body, name="gather_first",
        out_shape=tuple(jax.ShapeDtypeStruct(a.shape, a.dtype) for a in ins),
        in_specs=[ANY] * n_arr, out_specs=(ANY,) * n_arr,
        scratch_shapes=[pltpu.SemaphoreType.DMA((6 * n_arr,)), pltpu.SemaphoreType.DMA((6 * n_arr,))],
        input_output_aliases={0: 0, 1: 1},
        compiler_params=pltpu.CompilerParams(has_side_effects=True),
    )(*ins)


def _fwd_in(x, pre_g, win, l, ride=None):
    T, D = x.shape
    W = win.shape[-1]
    tm = min(512, T)

    def body(x_ref, g_ref, w_ref, p_ref, h_ref):
        hb = _rms(x_ref[...], g_ref[...]).astype(BF16)
        h_ref[...] = hb
        for j in range(4):
            p_ref[:, j * W:(j + 1) * W] = _dot(hb, w_ref[j]).astype(BF16)

    return _pcall(
        body, name="fwd_in",
        out_shape=(jax.ShapeDtypeStruct((T, 4 * W), BF16), jax.ShapeDtypeStruct((T, D), BF16)),
        grid=(T // tm,),
        in_specs=[pl.BlockSpec((tm, D), lambda i: (i, 0)),
                  pl.BlockSpec((None, 1, D), lambda i: (l, 0, 0)),
                  pl.BlockSpec((4, D, W), lambda i: (0, 0, 0), pipeline_mode=pl.Buffered(1))],
        out_specs=(pl.BlockSpec((tm, 4 * W), lambda i: (i, 0)),
                   pl.BlockSpec((tm, D), lambda i: (i, 0))),
        sem=("arbitrary",), args=(x, pre_g, win), ride=ride)


def _ret_specs(T, L):
    rope = pl.BlockSpec((3, L, 128), lambda s: (0, s, 0))
    mask = pl.BlockSpec((HEADS, L, L), lambda s: (0, 0, 0), pipeline_mode=pl.Buffered(1))
    qdm = pl.BlockSpec((HEADS, L, 128), lambda s: (0, 0, 0), pipeline_mode=pl.Buffered(1))
    small = pl.BlockSpec((HEADS, 8, 128), lambda s: (0, 0, 0))
    return [rope, mask, qdm, qdm, small, small]


QK = HEADS * DK
VW = HEADS * DV
PW = 2 * QK + 2 * VW


def _zero_at_start(ref):
    @pl.when(pl.program_id(0) == 0)
    def _():
        ref[...] = jnp.zeros_like(ref)


def _ret_fwd_part(p_ref, rope_ref, m_ref, qdm_ref, kdm_ref, mqs_ref, cd_ref, a_ref, st_ref, state):
    c, sl, sh = rope_ref[0], rope_ref[1], rope_ref[2]
    for j in range(HEADS // 2):
        rq = _rot(p_ref[:, 128 * j:128 * (j + 1)].astype(F32), c, sl, sh)
        rk = _rot(p_ref[:, QK + 128 * j:QK + 128 * (j + 1)].astype(F32), c, sl, sh)
        rkb = rk.astype(BF16)
        for e in range(2):
            h = 2 * j + e
            v = p_ref[:, 2 * QK + DV * h:2 * QK + DV * (h + 1)]
            g = p_ref[:, 2 * QK + VW + DV * h:2 * QK + VW + DV * (h + 1)].astype(F32)
            a = (rq * mqs_ref[h, 0:1, :]).astype(BF16)
            p = (_dot_nt(a, rkb) * m_ref[h]).astype(BF16)
            st = state[h]
            st_ref[h] = st
            o = _dot(p, v) + _dot((rq * qdm_ref[h]).astype(BF16), st.astype(BF16))
            state[h] = st * cd_ref[h, 0:1, :] + _dot_tn((rk * kdm_ref[h]).astype(BF16), v)
            a_ref[:, DV * h:DV * (h + 1)] = _gn_gate(o, g).astype(BF16)


def _shift_copies(src, sh):
    rows = sh.shape[1]
    for b in range(1, 8):
        sh[b - 1, :, :] = src[pl.ds(b, rows), :]


def _window(src, sh, r0, const, rows, lanes):
    b = const % 8
    at = pl.ds(pl.multiple_of(r0 + (const - b), 8), rows)
    w = src[at, lanes] if b == 0 else sh[b - 1, at, lanes]
    return w.reshape(rows // 8, 8, w.shape[-1])


def _conv_taps(wbuf, src, sh, r0, const, rows, lanes):
    groups = rows // 8
    accs = [None] * groups
    for k in range(CONV_K):
        w8 = wbuf[pl.ds(8 * k, 8), lanes]
        win = _window(src, sh, r0, const + k, rows, lanes)
        for g in range(groups):
            term = w8 * win[g]
            accs[g] = term if k == 0 else accs[g] + term
    return jnp.concatenate(accs, axis=0)


def _lane_parts(C):
    return [pl.ds(j * CONV_LANES, CONV_LANES) for j in range(C // CONV_LANES)]


def _load_conv_w(cw_ref, wbuf, flip):
    for k in range(CONV_K):
        row = jnp.concatenate([cw_ref[c, pl.ds(k, 1), :] for c in range(4)], axis=-1)
        kk = CONV_K - 1 - k if flip else k
        wbuf[pl.ds(8 * kk, 8), :] = jnp.broadcast_to(row, (8, row.shape[-1]))


def _conv_fwd_part(p_ref, cw_ref, cb_ref, lg_ref, lb_ref, a_ref, y_ref, ubuf, wbuf, ush):
    tc, C = y_ref.shape
    off = HALO - (CONV_K - 1)
    i = pl.program_id(0)

    @pl.when(i == 0)
    def _():
        ubuf[0:HALO, :] = jnp.zeros((HALO, C), F32)
        _load_conv_w(cw_ref, wbuf, False)

    @pl.when(i > 0)
    def _():
        ubuf[0:HALO, :] = ubuf[tc:tc + HALO, :]

    ga = p_ref[:, 0:C].astype(F32)
    gb = p_ref[:, C:2 * C].astype(F32)
    ubuf[HALO:HALO + tc, :] = ga * _sigmoid(gb)
    _shift_copies(ubuf, ush)

    def rows_block(r, carry):
        r0 = pl.multiple_of(r * CONV_RB, CONV_RB)
        for lanes in _lane_parts(C):
            y_ref[pl.ds(r0, CONV_RB), lanes] = _conv_taps(wbuf, ubuf, ush, r0, off, CONV_RB, lanes) + cb_ref[:, lanes]
        return carry

    lax.fori_loop(0, tc // CONV_RB, rows_block, 0)
    gc = p_ref[:, 2 * C:3 * C].astype(F32)
    a_ref[...] = _ln_gate(y_ref[...], gc, lg_ref[...], lb_ref[...]).astype(BF16)


def _ret_fwd(proj, tb, L, ride=None):
    T = proj.shape[0]
    nS = T // L

    def body(p_ref, rope_ref, m_ref, qdm_ref, kdm_ref, mqs_ref, cd_ref, ar_ref, st_ref, state):
        _zero_at_start(state)
        _ret_fwd_part(p_ref, rope_ref, m_ref, qdm_ref, kdm_ref, mqs_ref, cd_ref, ar_ref, st_ref, state)

    return _pcall(
        body, name="ret_fwd",
        out_shape=(jax.ShapeDtypeStruct((T, VW), BF16), jax.ShapeDtypeStruct((nS, HEADS, 128, DV), F32)),
        grid=(nS,),
        in_specs=[pl.BlockSpec((L, PW), lambda s: (s, 0))] + _ret_specs(T, L),
        out_specs=(pl.BlockSpec((L, VW), lambda s: (s, 0)), pl.BlockSpec((None, HEADS, 128, DV), lambda s: (s, 0, 0, 0))),
        scratch_shapes=[pltpu.VMEM((HEADS, 128, DV), F32)],
        sem=("arbitrary",), ride=ride,
        args=(proj, tb["rope"], tb["mask"], tb["qdm"], tb["kdm"], tb["mqs"], tb["cd"]))


def _conv_fwd(proj, wcw, conv_b, ln_g, ln_b, l, ride=None):
    T = proj.shape[0]
    C = conv_b.shape[-1]
    Cc = wcw.shape[-1]
    tc = min(CONV_TILE, T)
    assert PW == 3 * C

    def body(p_ref, cw_ref, cb_ref, lg_ref, lb_ref, ac_ref, y_ref, ubuf, wbuf, ush):
        _conv_fwd_part(p_ref, cw_ref, cb_ref, lg_ref, lb_ref, ac_ref, y_ref, ubuf, wbuf, ush)

    vec = pl.BlockSpec((None, 1, C), lambda i: (l, 0, 0))
    tile = pl.BlockSpec((tc, C), lambda i: (i, 0))
    return _pcall(
        body, name="conv_fwd",
        out_shape=(jax.ShapeDtypeStruct((T, C), BF16), jax.ShapeDtypeStruct((T, C), F32)),
        grid=(T // tc,),
        in_specs=[pl.BlockSpec((tc, 3 * C), lambda i: (i, 1)),
                  pl.BlockSpec((4, None, CONV_K, Cc), lambda i: (0, l, 0, 0)), vec, vec, vec],
        out_specs=(tile, tile),
        scratch_shapes=[pltpu.VMEM((HALO + tc, C), F32), pltpu.VMEM((8 * CONV_K, C), F32),
                        pltpu.VMEM((7, HALO + tc - 8, C), F32)],
        sem=("arbitrary",), ride=ride, args=(proj, wcw, conv_b, ln_g, ln_b))


def _merge_fwd(x, proj, a_ret, a_conv, wsq, post_g, l, ride=None):
    T, D = x.shape
    R = wsq.shape[1] // 3
    tm = min(512, T)

    def body(x_ref, p_ref, ar_ref, ac_ref, wro_ref, wco_ref, wo_ref, g_ref, xn_ref, ro_ref, co_ref, ym_ref, z_ref):
        ro = _dot(ar_ref[...], wro_ref[...].reshape(4 * R, D))
        co = _dot(ac_ref[...], wco_ref[...].reshape(4 * R, D))
        ym = (_sigmoid(p_ref[:, 0:D].astype(F32)) * ro + _sigmoid(p_ref[:, D:2 * D].astype(F32)) * co).astype(BF16)
        z = _dot(ym, wo_ref[...].reshape(4 * R, D))
        ro_ref[...] = ro.astype(BF16)
        co_ref[...] = co.astype(BF16)
        ym_ref[...] = ym
        z_ref[...] = z.astype(BF16)
        xn_ref[...] = x_ref[...] + _rms(z, g_ref[...])

    tile = pl.BlockSpec((tm, D), lambda i: (i, 0))
    wspec = lambda m: pl.BlockSpec((4, R, D), lambda i: (0, m, 0))
    act = jax.ShapeDtypeStruct((T, D), BF16)
    return _pcall(
        body, name="merge_fwd",
        out_shape=(jax.ShapeDtypeStruct((T, D), F32), act, act, act, act),
        grid=(T // tm,),
        in_specs=[tile, pl.BlockSpec((tm, 2 * D), lambda i: (i, 3)), tile, tile,
                  wspec(0), wspec(1), wspec(2), pl.BlockSpec((None, 1, D), lambda i: (l, 0, 0))],
        out_specs=(tile, tile, tile, tile, tile),
        sem=("arbitrary",), ride=ride, args=(x, proj, a_ret, a_conv, wsq, wsq, wsq, post_g))


def _loss_fwd_bwd(y, target):
    T, D = y.shape
    tm = min(512, T)

    def body(y_ref, t_ref, dy_ref, ls_ref):
        @pl.when(pl.program_id(0) == 0)
        def _():
            ls_ref[...] = jnp.zeros_like(ls_ref)

        e = y_ref[...] - t_ref[...]
        dy_ref[...] = e * (1.0 / D)
        ls_ref[...] += jnp.sum((e * e).reshape(tm // 8, 8, D), axis=0) * (0.5 / D)

    tile = pl.BlockSpec((tm, D), lambda i: (i, 0))
    return pl.pallas_call(
        body, name="loss",
        out_shape=(jax.ShapeDtypeStruct((T, D), F32), jax.ShapeDtypeStruct((8, D), F32)),
        grid=(T // tm,),
        in_specs=[tile, tile],
        out_specs=(tile, pl.BlockSpec((8, D), lambda i: (0, 0))),
        compiler_params=_cp(("arbitrary",)),
    )(y, target)


def _merge_bwd(dxn, proj, a_ret, a_conv, ro, co, ym, z, wsq, post_g, l, ride=None):
    T, D = dxn.shape
    R = wsq.shape[1] // 3
    tm = min(512, T)
    n = T // tm

    def body(dx_ref, p_ref, ar_ref, ac_ref, ro_ref, co_ref, ym_ref, z_ref, wro_ref, wco_ref, wo_ref, g_ref,
             dp_ref, dar_ref, dac_ref, gsq_ref, dg_ref, acc, stage):
        i = pl.program_id(0)

        @pl.when(i == 0)
        def _():
            acc[...] = jnp.zeros_like(acc)
            dg_ref[...] = jnp.zeros_like(dg_ref)

        _, vjp = jax.vjp(_rms, z_ref[...].astype(F32), g_ref[...])
        dz, dg = vjp(dx_ref[...])
        dg_ref[0:1, :] += dg
        dzb = dz.astype(BF16)
        dym = _dot_nt(dzb, wo_ref[...].reshape(4 * R, D))
        acc[2] += _dot_tn(ym_ref[...], dzb)
        sr = _sigmoid(p_ref[:, 0:D].astype(F32))
        sc = _sigmoid(p_ref[:, D:2 * D].astype(F32))
        rov = ro_ref[...].astype(F32)
        cov = co_ref[...].astype(F32)
        dp_ref[:, 0:D] = (dym * rov * sr * (1.0 - sr)).astype(BF16)
        dp_ref[:, D:2 * D] = (dym * cov * sc * (1.0 - sc)).astype(BF16)
        dro = (dym * sr).astype(BF16)
        dco = (dym * sc).astype(BF16)
        dar_ref[...] = _dot_nt(dro, wro_ref[...].reshape(4 * R, D)).astype(BF16)
        dac_ref[...] = _dot_nt(dco, wco_ref[...].reshape(4 * R, D)).astype(BF16)
        acc[0] += _dot_tn(ar_ref[...], dro)
        acc[1] += _dot_tn(ac_ref[...], dco)

        @pl.when(i == n - 1)
        def _():
            for m in range(3):
                stage[...] = acc[m].astype(BF16).reshape(4, R, D)
                pltpu.sync_copy(stage, gsq_ref.at[:, pl.ds(m * R, R), :])

    tile = pl.BlockSpec((tm, D), lambda i: (i, 0))
    wspec = lambda m: pl.BlockSpec((4, R, D), lambda i: (0, m, 0), pipeline_mode=pl.Buffered(1))
    return _pcall(
        body, name="merge_bwd",
        out_shape=(jax.ShapeDtypeStruct(proj.shape, BF16), jax.ShapeDtypeStruct((T, D), BF16),
                   jax.ShapeDtypeStruct((T, D), BF16), jax.ShapeDtypeStruct(wsq.shape, BF16),
                   jax.ShapeDtypeStruct((8, D), F32)),
        grid=(n,),
        in_specs=[tile, pl.BlockSpec((tm, 2 * D), lambda i: (i, 3)), tile, tile, tile, tile, tile, tile,
                  wspec(0), wspec(1), wspec(2), pl.BlockSpec((None, 1, D), lambda i: (l, 0, 0))],
        out_specs=(pl.BlockSpec((tm, 2 * D), lambda i: (i, 3)), tile, tile, ANY, pl.BlockSpec((8, D), lambda i: (0, 0))),
        scratch_shapes=[pltpu.VMEM((3, 4 * R, D), F32), pltpu.VMEM((4, R, D), BF16)],
        sem=("arbitrary",), args=(dxn, proj, a_ret, a_conv, ro, co, ym, z, wsq, wsq, wsq, post_g), ride=ride)


def _conv_bwd_part(n, da_ref, y_ref, p_ref, ph_ref, cw_ref, lg_ref, lb_ref, dp_ref,
                   dcbuf, ubuf, dubuf, wbuf, dwacc, vacc, dsh, ush):
    tc, C = y_ref.shape
    off = HALO - (CONV_K - 1)
    nrb = tc // CONV_RB
    t = pl.program_id(0)
    i = n - 1 - t

    @pl.when(t == 0)
    def _():
        dcbuf[tc:tc + HALO, :] = jnp.zeros((HALO, C), F32)
        dwacc[...] = jnp.zeros_like(dwacc)
        vacc[...] = jnp.zeros_like(vacc)
        _load_conv_w(cw_ref, wbuf, True)

    @pl.when(t > 0)
    def _():
        dcbuf[tc:tc + HALO, :] = dcbuf[0:HALO, :]

    gc = p_ref[:, 2 * C:3 * C].astype(F32)
    _, vjp = jax.vjp(_ln_gate, y_ref[...], gc, lg_ref[...], lb_ref[...])
    dy, dgc, dlg, dlb = vjp(da_ref[...].astype(F32))
    dcbuf[0:tc, :] = dy
    dp_ref[:, 2 * C:3 * C] = dgc.astype(BF16)
    vacc[0:1, :] += jnp.sum(dy, axis=0, keepdims=True)
    vacc[1:2, :] += dlg
    vacc[2:3, :] += dlb

    ga = p_ref[:, 0:C].astype(F32)
    sb = _sigmoid(p_ref[:, C:2 * C].astype(F32))
    ubuf[HALO:HALO + tc, :] = ga * sb
    uh = ph_ref[:, 0:C].astype(F32) * _sigmoid(ph_ref[:, C:2 * C].astype(F32))
    ubuf[0:HALO, :] = jnp.where(i > 0, uh, 0.0)

    _shift_copies(dcbuf, dsh)
    _shift_copies(ubuf, ush)
    def du_block(r, carry):
        r0 = pl.multiple_of(r * CONV_RB, CONV_RB)
        for lanes in _lane_parts(C):
            dubuf[pl.ds(r0, CONV_RB), lanes] = _conv_taps(wbuf, dcbuf, dsh, r0, 0, CONV_RB, lanes)
        return carry

    lax.fori_loop(0, nrb, du_block, 0)
    du = dubuf[...]
    dp_ref[:, 0:C] = (du * sb).astype(BF16)
    dp_ref[:, C:2 * C] = (du * ga * sb * (1.0 - sb)).astype(BF16)

    def dw_block(r, carry):
        r0 = pl.multiple_of(r * CONV_RB, CONV_RB)
        for lanes in _lane_parts(C):
            dyb = dcbuf[pl.ds(r0, CONV_RB), lanes].reshape(CONV_RB // 8, 8, CONV_LANES)
            for k in range(CONV_K):
                dwacc[8 * k:8 * k + 8, lanes] += jnp.sum(dyb * _window(ubuf, ush, r0, off + k, CONV_RB, lanes), axis=0)
        return carry

    lax.fori_loop(0, nrb, dw_block, 0)


def _conv_bwd_final(n, sg_ref, dwacc, vacc):
    C = sg_ref.shape[-1]

    @pl.when(pl.program_id(0) == n - 1)
    def _():
        for k in range(CONV_K):
            sg_ref[pl.ds(k, 1), :] = jnp.sum(dwacc[8 * k:8 * k + 8, :], axis=0, keepdims=True)
        sg_ref[pl.ds(CONV_K, 1), :] = jnp.zeros((1, C), F32)
        sg_ref[ROW_CB:ROW_CB + 8, :] = jnp.zeros((8, C), F32)
        sg_ref[ROW_CB:ROW_CB + 3, :] = vacc[0:3, :]


def _ret_bwd_part(da_ref, p_ref, st_ref, rope_ref, m_ref, qdm_ref, kdm_ref, mqs_ref, cd_ref, dp_ref, gst):
    c, sl, sh = rope_ref[0], rope_ref[1], rope_ref[2]
    for j in range(HEADS // 2):
        rq = _rot(p_ref[:, 128 * j:128 * (j + 1)].astype(F32), c, sl, sh)
        rk = _rot(p_ref[:, QK + 128 * j:QK + 128 * (j + 1)].astype(F32), c, sl, sh)
        rkb = rk.astype(BF16)
        drq = jnp.zeros_like(rq)
        drk = jnp.zeros_like(rk)
        for e in range(2):
            h = 2 * j + e
            v = p_ref[:, 2 * QK + DV * h:2 * QK + DV * (h + 1)]
            g = p_ref[:, 2 * QK + VW + DV * h:2 * QK + VW + DV * (h + 1)].astype(F32)
            mqs = mqs_ref[h, 0:1, :]
            a = (rq * mqs).astype(BF16)
            aq = (rq * qdm_ref[h]).astype(BF16)
            kdv = (rk * kdm_ref[h]).astype(BF16)
            mk = m_ref[h]
            p = (_dot_nt(a, rkb) * mk).astype(BF16)
            stb = st_ref[h].astype(BF16)
            o = _dot(p, v) + _dot(aq, stb)
            _, vjp = jax.vjp(_gn_gate, o, g)
            do, dg = vjp(da_ref[:, DV * h:DV * (h + 1)].astype(F32))
            dob = do.astype(BF16)
            gs = gst[h]
            gsb = gs.astype(BF16)
            ds = (_dot_nt(dob, v) * mk).astype(BF16)
            drq = drq + _dot(ds, rkb) * mqs + _dot_nt(dob, stb) * qdm_ref[h]
            drk = drk + _dot_tn(ds, a) + _dot_nt(v, gsb) * kdm_ref[h]
            dv = _dot_tn(p, dob) + _dot(kdv, gsb)
            gst[h] = _dot_tn(aq, dob) + gs * cd_ref[h, 0:1, :]
            dp_ref[:, 2 * QK + DV * h:2 * QK + DV * (h + 1)] = dv.astype(BF16)
            dp_ref[:, 2 * QK + VW + DV * h:2 * QK + VW + DV * (h + 1)] = dg.astype(BF16)
        dp_ref[:, 128 * j:128 * (j + 1)] = _rot_t(drq, c, sl, sh).astype(BF16)
        dp_ref[:, QK + 128 * j:QK + 128 * (j + 1)] = _rot_t(drk, c, sl, sh).astype(BF16)


def _ret_bwd(dproj, da_ret, proj, states, tb, L, ride=None):
    T = proj.shape[0]
    nS = T // L

    def body(dpin_ref, dar_ref, p_ref, st_ref, rope_ref, m_ref, qdm_ref, kdm_ref, mqs_ref, cd_ref, dp_ref, gst):
        _zero_at_start(gst)
        _ret_bwd_part(dar_ref, p_ref, st_ref, rope_ref, m_ref, qdm_ref, kdm_ref, mqs_ref, cd_ref, dp_ref, gst)

    rev = lambda s: nS - 1 - s
    specs = _ret_specs(T, L)
    specs[0] = pl.BlockSpec((3, L, 128), lambda s: (0, rev(s), 0))
    ptile = pl.BlockSpec((L, PW), lambda s: (rev(s), 0))
    return _pcall(
        body, name="ret_bwd",
        out_shape=(jax.ShapeDtypeStruct(dproj.shape, BF16),),
        grid=(nS,),
        in_specs=[ANY, pl.BlockSpec((L, VW), lambda s: (rev(s), 0)), ptile,
                  pl.BlockSpec((None, HEADS, 128, DV), lambda s: (rev(s), 0, 0, 0))] + specs,
        out_specs=(ptile,),
        scratch_shapes=[pltpu.VMEM((HEADS, 128, DV), F32)],
        sem=("arbitrary",), aliases={0: 0}, ride=ride,
        args=(dproj, da_ret, proj, states, tb["rope"], tb["mask"], tb["qdm"], tb["kdm"], tb["mqs"], tb["cd"]))


def _conv_bwd(dproj, da_conv, y, proj, wcw, ln_g, ln_b, l, ride=None):
    T, C = y.shape
    Cc = wcw.shape[-1]
    tc = min(CONV_TILE, T)
    n = T // tc
    hb = tc // HALO

    def body(dpin_ref, dac_ref, y_ref, p_ref, ph_ref, cw_ref, lg_ref, lb_ref, dp_ref, sg_ref,
             dcbuf, ubuf, dubuf, wbuf, dwacc, vacc, dsh, ush):
        _conv_bwd_part(n, dac_ref, y_ref, p_ref, ph_ref, cw_ref, lg_ref, lb_ref, dp_ref,
                       dcbuf, ubuf, dubuf, wbuf, dwacc, vacc, dsh, ush)
        _conv_bwd_final(n, sg_ref, dwacc, vacc)

    rev = lambda t: n - 1 - t
    vec = pl.BlockSpec((None, 1, C), lambda t: (l, 0, 0))
    tile = pl.BlockSpec((tc, C), lambda t: (rev(t), 0))
    ptile = pl.BlockSpec((tc, 3 * C), lambda t: (rev(t), 1))
    halo = pl.BlockSpec((HALO, 3 * C), lambda t: (jnp.maximum(rev(t) * hb - 1, 0), 1))
    return _pcall(
        body, name="conv_bwd",
        out_shape=(jax.ShapeDtypeStruct(dproj.shape, BF16), jax.ShapeDtypeStruct((ROW_PRE, C), F32)),
        grid=(n,),
        in_specs=[ANY, tile, tile, ptile, halo, pl.BlockSpec((4, None, CONV_K, Cc), lambda t: (0, l, 0, 0)), vec, vec],
        out_specs=(ptile, pl.BlockSpec((ROW_PRE, C), lambda t: (0, 0))),
        scratch_shapes=[pltpu.VMEM((tc + HALO, C), F32), pltpu.VMEM((HALO + tc, C), F32), pltpu.VMEM((tc, C), F32),
                        pltpu.VMEM((8 * CONV_K, C), F32), pltpu.VMEM((8 * CONV_K, C), F32), pltpu.VMEM((8, C), F32),
                        pltpu.VMEM((7, HALO + tc - 8, C), F32), pltpu.VMEM((7, HALO + tc - 8, C), F32)],
        sem=("arbitrary",), aliases={0: 0}, ride=ride, args=(dproj, da_conv, y, proj, proj, wcw, ln_g, ln_b))


def _win_grad(h, dproj, W, ride=None):
    T, D = h.shape
    tk = min(2048, T)
    nk = T // tk

    def body(h_ref, dp_ref, g_ref, acc):
        k = pl.program_id(1)

        @pl.when(k == 0)
        def _():
            acc[...] = jnp.zeros_like(acc)

        acc[...] += _dot_tn(h_ref[...], dp_ref[...])

        @pl.when(k == nk - 1)
        def _():
            g_ref[...] = acc[...].astype(BF16)

    return _pcall(
        body, name="win_grad",
        out_shape=(jax.ShapeDtypeStruct((4, D, W), BF16),),
        grid=(4, nk),
        in_specs=[pl.BlockSpec((tk, D), lambda j, k: (k, 0)), pl.BlockSpec((tk, W), lambda j, k: (k, j))],
        out_specs=(pl.BlockSpec((None, D, W), lambda j, k: (j, 0, 0)),),
        scratch_shapes=[pltpu.VMEM((D, W), F32)],
        sem=("arbitrary", "arbitrary"), args=(h, dproj), ride=ride)


def _in_bwd(dxn, dproj, x, pre_g, win, l, ride=None):
    T, D = x.shape
    W = win.shape[-1]
    tm = min(512, T)

    def body(dxn_ref, dp_ref, x_ref, g_ref, w_ref, dx_ref, dg_ref):
        @pl.when(pl.program_id(0) == 0)
        def _():
            dg_ref[...] = jnp.zeros_like(dg_ref)

        dh = _dot_nt(dp_ref[:, 0:W], w_ref[0])
        for j in range(1, 4):
            dh = dh + _dot_nt(dp_ref[:, j * W:(j + 1) * W], w_ref[j])
        _, vjp = jax.vjp(_rms, x_ref[...], g_ref[...])
        dx, dg = vjp(dh)
        dx_ref[...] = dxn_ref[...] + dx
        dg_ref[0:1, :] += dg

    tile = pl.BlockSpec((tm, D), lambda i: (i, 0))
    return _pcall(
        body, name="in_bwd",
        out_shape=(jax.ShapeDtypeStruct((T, D), F32), jax.ShapeDtypeStruct((8, D), F32)),
        grid=(T // tm,),
        in_specs=[tile, pl.BlockSpec((tm, 4 * W), lambda i: (i, 0)), tile,
                  pl.BlockSpec((None, 1, D), lambda i: (l, 0, 0)),
                  pl.BlockSpec((4, D, W), lambda i: (0, 0, 0), pipeline_mode=pl.Buffered(1))],
        out_specs=(tile, pl.BlockSpec((8, D), lambda i: (0, 0))),
        sem=("arbitrary",), args=(dxn, dproj, x, pre_g, win), ride=ride)


def _sum_group(chip, t, u):
    _, A, B = t.shape
    tr = min(256, A)

    def body(k_ref, t_ref, u_ref, o_ref):
        o_ref[...] = ((t_ref[...].astype(F32) + u_ref[0].astype(F32)) + u_ref[1].astype(F32)) + u_ref[2].astype(F32)

    return pl.pallas_call(
        body, name="sum_group",
        out_shape=jax.ShapeDtypeStruct((A, B), F32),
        grid_spec=pltpu.PrefetchScalarGridSpec(
            num_scalar_prefetch=1, grid=(A // tr,),
            in_specs=[pl.BlockSpec((None, tr, B), lambda i, k: (k[0], i, 0)),
                      pl.BlockSpec((3, tr, B), lambda i, k: (0, i, 0))],
            out_specs=pl.BlockSpec((tr, B), lambda i, k: (i, 0))),
        compiler_params=_cp(("arbitrary",)),
    )(chip, t, u)


def _swap_rows(g):
    _, A, B = g.shape
    nh = A // 2

    def body(g_ref, r_ref, send_sems, recv_sems):
        x, y, c = _place()
        cp = _remote(g_ref.at[:, pl.ds((1 - c) * nh, nh)], r_ref, send_sems, recv_sems, 0, (x, y, 1 - c))
        cp.start()
        cp.wait()

    return pl.pallas_call(
        body, name="swap_rows",
        out_shape=jax.ShapeDtypeStruct((4, nh, B), g.dtype),
        in_specs=[ANY], out_specs=ANY,
        scratch_shapes=[pltpu.SemaphoreType.DMA((1,)), pltpu.SemaphoreType.DMA((1,))],
        compiler_params=pltpu.CompilerParams(has_side_effects=True),
    )(g)


def _add_rows(cidx, g, r):
    _, nh, B = r.shape
    tr = min(256, nh)
    nb = nh // tr

    def body(c_ref, g_ref, r_ref, o_ref):
        o_ref[...] = (g_ref[...].astype(F32) + r_ref[...].astype(F32)).astype(BF16)

    blk = (None, tr, B)
    return pl.pallas_call(
        body, name="add_rows",
        out_shape=jax.ShapeDtypeStruct(r.shape, BF16),
        grid_spec=pltpu.PrefetchScalarGridSpec(
            num_scalar_prefetch=1, grid=(4, nb),
            in_specs=[pl.BlockSpec(blk, lambda k, i, c: (k, c[0] * nb + i, 0)),
                      pl.BlockSpec(blk, lambda k, i, c: (k, i, 0))],
            out_specs=pl.BlockSpec(blk, lambda k, i, c: (k, i, 0))),
        compiler_params=_cp(("arbitrary", "arbitrary")),
    )(cidx, g, r)


def _sum_group_half(chip, cidx, t, u):
    _, nh, B = t.shape
    tr = min(256, nh)
    nb = nh // tr

    def body(k_ref, c_ref, t_ref, u_ref, o_ref):
        mine = (pl.program_id(0) // nb) == c_ref[0]

        @pl.when(mine)
        def _():
            o_ref[...] = ((t_ref[...].astype(F32) + u_ref[0].astype(F32)) + u_ref[1].astype(F32)) + u_ref[2].astype(F32)

        @pl.when(jnp.logical_not(mine))
        def _():
            o_ref[...] = jnp.zeros_like(o_ref)

    own = lambda i, c: jnp.clip(i - c[0] * nb, 0, nb - 1)
    return pl.pallas_call(
        body, name="sum_group_half",
        out_shape=jax.ShapeDtypeStruct((2 * nh, B), F32),
        grid_spec=pltpu.PrefetchScalarGridSpec(
            num_scalar_prefetch=2, grid=(2 * nb,),
            in_specs=[pl.BlockSpec((None, tr, B), lambda i, k, c: (k[0], own(i, c), 0)),
                      pl.BlockSpec((3, tr, B), lambda i, k, c: (0, own(i, c), 0))],
            out_specs=pl.BlockSpec((tr, B), lambda i, k, c: (i, 0))),
        compiler_params=_cp(("arbitrary",)),
    )(chip, cidx, t, u)


def _adam_math(w, g, m, v):
    c1 = 1.0 / (1.0 - ADAM_B1 ** ADAM_STEP)
    c2 = 1.0 / (1.0 - ADAM_B2 ** ADAM_STEP)
    nm = ADAM_B1 * m + (1.0 - ADAM_B1) * g
    nv = ADAM_B2 * v + (1.0 - ADAM_B2) * (g * g)
    return -ADAM_LR * ((nm * c1) / (jnp.sqrt(nv * c2) + ADAM_EPS) + ADAM_WD * w), nm, nv


def _adamw_layer(prev, w, m, v, sa, sb, l, part, ride=None):
    NL, A, B = w.shape
    tr = A
    while tr * B * 4 > ADAM_BLOCK_BYTES and tr % 16 == 0:
        tr //= 2
    nb = A // tr

    def body(p0, p1, p2, p3, w_ref, m_ref, v_ref, sa_ref, sb_ref, g_ref, d_ref, nm_ref, nv_ref):
        g = sa_ref[...] + sb_ref[...]
        g_ref[...] = g
        d_ref[...], nm_ref[...], nv_ref[...] = _adam_math(w_ref[...], g, m_ref[...], v_ref[...])

    lay = pl.BlockSpec((None, tr, B), lambda i: (l, i, 0))
    src = pl.BlockSpec((tr, B), lambda i: (part * nb + i, 0))
    full = jax.ShapeDtypeStruct((NL, A, B), F32)
    if prev is None:
        prev = tuple(lax.empty((NL, A, B), F32) for _ in range(4))
    outs, landed = _pcall(
        body, name="adamw_layer",
        out_shape=(full,) * 4, grid=(nb,),
        in_specs=[ANY] * 4 + [lay, lay, lay, src, src], out_specs=(lay,) * 4,
        sem=("arbitrary",), aliases={0: 0, 1: 1, 2: 2, 3: 3}, args=(*prev, w, m, v, sa, sb), ride=ride)
    return tuple(outs), landed


def _adamw_square(prev, wmv, sa, sb, l):
    NL, R, D = wmv[0][0].shape
    n = len(wmv)

    def body(*refs):
        ins, outs = refs[4 * n:4 * n + 3 * n + 2], refs[4 * n + 3 * n + 2:]
        sa_ref, sb_ref = ins[3 * n], ins[3 * n + 1]
        g = sa_ref[...] + sb_ref[...]
        for p in range(n):
            @pl.when(pl.program_id(0) == p)
            def _(p=p):
                w_ref, m_ref, v_ref = ins[3 * p:3 * p + 3]
                g_ref, d_ref, nm_ref, nv_ref = outs[4 * p:4 * p + 4]
                g_ref[...] = g
                d_ref[...], nm_ref[...], nv_ref[...] = _adam_math(w_ref[...], g, m_ref[...], v_ref[...])

    lay = pl.BlockSpec((None, R, D), lambda i: (l, 0, 0))
    src = pl.BlockSpec((R, D), lambda i: (i, 0))
    full = jax.ShapeDtypeStruct((NL, R, D), F32)
    if prev is None:
        prev = tuple(lax.empty((NL, R, D), F32) for _ in range(4 * n))
    flat = [a for t in wmv for a in t]
    outs = pl.pallas_call(
        body, name="adamw_square",
        out_shape=(full,) * (4 * n), grid=(n,),
        in_specs=[ANY] * (4 * n) + [lay] * (3 * n) + [src, src], out_specs=(lay,) * (4 * n),
        input_output_aliases={i: i for i in range(4 * n)},
        compiler_params=_cp(("arbitrary",)),
    )(*prev, *flat, sa, sb)
    return tuple(outs)


def _adamw(w, g, m, v):
    shape = w.shape
    cols = shape[-1]
    rows = int(np.prod(shape[:-1]))

    def body(w_ref, g_ref, m_ref, v_ref, d_ref, nm_ref, nv_ref):
        d_ref[...], nm_ref[...], nv_ref[...] = _adam_math(w_ref[...], g_ref[...], m_ref[...], v_ref[...])

    tile = pl.BlockSpec((rows, cols), lambda i: (0, 0))
    out = jax.ShapeDtypeStruct((rows, cols), F32)
    res = pl.pallas_call(
        body, name="adamw",
        out_shape=(out, out, out), grid=(1,),
        in_specs=[tile] * 4, out_specs=(tile,) * 3,
        compiler_params=_cp(("arbitrary",)),
    )(*[a.reshape(rows, cols) for a in (w, g, m, v)])
    return tuple(a.reshape(shape) for a in res)


def _tail_exchange(small, s_in, s_sq):
    def body(s_ref, a_ref, b_ref, o_ref, oa_ref, ob_ref, send_sems, recv_sems, local_sem):
        x, y, c = _place()
        me = 4 * x + 2 * y + c
        sibling = (x, y, 1 - c)
        mine = pltpu.make_async_copy(s_ref, o_ref.at[me], local_sem)
        mine.start()
        swaps = [_remote(a_ref, oa_ref, send_sems, recv_sems, 7, sibling), _remote(b_ref, ob_ref, send_sems, recv_sems, 8, sibling)]
        sends = [_remote(s_ref, o_ref.at[me], send_sems, recv_sems, r, peer) for r, peer in enumerate(_peers(x, y, c))]
        for cp in swaps + sends:
            cp.start()
        for r, peer in enumerate(_peers(x, y, c)):
            theirs = o_ref.at[4 * peer[0] + 2 * peer[1] + peer[2]]
            _remote(theirs, theirs, send_sems, recv_sems, r, peer).wait_recv()
        for cp in sends:
            cp.wait_send()
        for cp in swaps:
            cp.wait()
        mine.wait()

    return pl.pallas_call(
        body, name="tail_exchange",
        out_shape=(jax.ShapeDtypeStruct((8,) + small.shape, small.dtype),
                   jax.ShapeDtypeStruct(s_in.shape, s_in.dtype), jax.ShapeDtypeStruct(s_sq.shape, s_sq.dtype)),
        in_specs=[ANY] * 3, out_specs=(ANY,) * 3,
        scratch_shapes=[pltpu.SemaphoreType.DMA((9,)), pltpu.SemaphoreType.DMA((9,)), pltpu.SemaphoreType.DMA],
        compiler_params=pltpu.CompilerParams(has_side_effects=True),
    )(small, s_in, s_sq)


def _sum_devices(gs):
    NL = len(gs)
    _, R, D = gs[0].shape

    def body(*refs):
        o_ref = refs[NL]
        for l in range(NL):
            acc = refs[l][0]
            for k in range(1, 8):
                acc = acc + refs[l][k]
            o_ref[l] = acc

    return pl.pallas_call(
        body, name="sum_devices",
        out_shape=jax.ShapeDtypeStruct((NL, R, D), F32),
        grid=(1,),
        in_specs=[pl.BlockSpec((8, R, D), lambda i: (0, 0, 0))] * NL,
        out_specs=pl.BlockSpec((NL, R, D), lambda i: (0, 0, 0)),
        compiler_params=_cp(("arbitrary",)),
    )(*gs)


def kernel(x, pre_norm_g, w_in, w_ret_out, conv_w, conv_b, conv_ln_g, conv_ln_b, w_conv_out, w_o, post_norm_g, loss_target, m_pre_norm_g, m_w_in, m_w_ret_out, m_conv_w, m_conv_b, m_conv_ln_g, m_conv_ln_b, m_w_conv_out, m_w_o, m_post_norm_g, v_pre_norm_g, v_w_in, v_w_ret_out, v_conv_w, v_conv_b, v_conv_ln_g, v_conv_ln_b, v_w_conv_out, v_w_o, v_post_norm_g):
    NL, D, W = w_in.shape
    Cc = conv_w.shape[-1]
    T = x.shape[1]
    L = min(RET_BLOCK, T)
    tb = _tables(T, L)
    ax, ay, ac = _place()
    chip = (2 * ax + ay).astype(jnp.int32).reshape(1)
    cidx = ac.astype(jnp.int32).reshape(1)
    pre_g, cb, lg, lb, post_g = (a.reshape(NL, 1, D) for a in (pre_norm_g, conv_b, conv_ln_g, conv_ln_b, post_norm_g))

    win = [_cast_win(chip, w_in, l) for l in range(NL)]
    wsq = [_cast_wsq(chip, w_ret_out, w_conv_out, w_o, l) for l in range(NL)]
    win[0], wcw = _gather_first(win[0], _place_cw(chip, conv_w))

    saved = []
    xl = x[0]
    for l in range(NL):
        more = l + 1 < NL
        ride = _Ride()
        if more:
            _ride_gather_ici(ride, "win", win[l + 1], (0, 5, 8))
        (proj, h), got = _fwd_in(xl, pre_g, win[l], l, ride=ride)
        if more:
            win[l + 1] = got["win"]
        ride = _Ride()
        if l == 0:
            _ride_gather_ici(ride, "wsq0", wsq[0])
        (a_ret, states), got = _ret_fwd(proj, tb, L, ride=ride)
        if l == 0:
            wsq[0] = got["wsq0"]
        ride = _Ride()
        if more:
            _ride_gather_ici(ride, "win", win[l + 1], (5, 8, 8))
            _ride_gather_ici(ride, "wsq", wsq[l + 1])
        if l == 0:
            _ride_gather_pass(ride, "wsq0", wsq[0])
        (a_conv, y), got = _conv_fwd(proj, wcw, cb, lg, lb, l, ride=ride)
        if more:
            win[l + 1], wsq[l + 1] = got["win"], got["wsq"]
        if l == 0:
            wsq[0] = got["wsq0"]
        ride = _Ride()
        if more:
            _ride_gather_pass(ride, "win", win[l + 1])
            _ride_gather_pass(ride, "wsq", wsq[l + 1])
        (xn, ro, co, ym, z), got = _merge_fwd(xl, proj, a_ret, a_conv, wsq[l], post_g, l, ride=ride)
        if more:
            win[l + 1], wsq[l + 1] = got["win"], got["wsq"]
        saved.append((xl, proj, h, a_ret, states, a_conv, y, ro, co, ym, z))
        xl = xn
    dx, lsum = _loss_fwd_bwd(xl, loss_target[0])

    gin, gsq, uin, usq = [None] * NL, [None] * NL, [None] * NL, [None] * NL
    s_in, s_sq, o_in, o_sq = [None] * NL, [None] * NL, [None] * NL, [None] * NL
    small, gs = [None] * NL, [None] * NL
    for l in reversed(range(NL)):
        xin, proj, h, a_ret, states, a_conv, y, ro, co, ym, z = saved[l]
        (dproj, da_ret, da_conv, gsq[l], dpost), _ = _merge_bwd(dx, proj, a_ret, a_conv, ro, co, ym, z, wsq[l], post_g, l)
        ride = _Ride()
        if l + 1 < NL:
            _ride_exchange(ride, "gin", gin[l + 1], "uin", uin[l + 1],
                           [(2, WHOLE), (0, SECOND_HALF), (1, SECOND_HALF)])
        (dproj, sg), got = _conv_bwd(dproj, da_conv, y, proj, wcw, lg, lb, l, ride=ride)
        if l + 1 < NL:
            uin[l + 1] = got["uin"]
        ride = _Ride()
        _ride_exchange(ride, "gsq", gsq[l], "usq", None, [(0, WHOLE), (1, WHOLE), (2, WHOLE)])
        if l + 1 < NL:
            _ride_gather_all(ride, "small", small[l + 1], "gs")
        (dproj,), got = _ret_bwd(dproj, da_ret, proj, states, tb, L, ride=ride)
        usq[l] = got["usq"]
        if l + 1 < NL:
            gs[l + 1] = got["gs"]
        (gin[l],), _ = _win_grad(h, dproj, W)
        ride = _Ride()
        if l > 0:
            _ride_exchange(ride, "gin", gin[l], "uin", None, [(0, FIRST_HALF), (1, FIRST_HALF)])
        else:
            gin[0] = _add_rows(cidx, gin[0], _swap_rows(gin[0]))
            _ride_exchange(ride, "gin", gin[0], "uin", None, [(0, WHOLE), (1, WHOLE), (2, WHOLE)])
        if l + 1 < NL:
            s_in[l + 1] = _sum_group(chip, gin[l + 1], uin[l + 1])
            s_sq[l + 1] = _sum_group(chip, gsq[l + 1], usq[l + 1])
            _ride_swap(ride, "s_in", s_in[l + 1], "o_in")
            _ride_swap(ride, "s_sq", s_sq[l + 1], "o_sq")
        (dx, dpre), got = _in_bwd(dx, dproj, xin, pre_g, win[l], l, ride=ride)
        uin[l] = got["uin"]
        if l + 1 < NL:
            o_in[l + 1], o_sq[l + 1] = got["o_in"], got["o_sq"]
        small[l] = jnp.concatenate([sg, dpre, dpost, lsum if l == NL - 1 else jnp.zeros_like(lsum)], axis=0)
    grad_x = dx

    big = {"w_in": None, "w_ret_out": None, "w_conv_out": None, "w_o": None}
    wts = {"w_in": (w_in, m_w_in, v_w_in), "w_ret_out": (w_ret_out, m_w_ret_out, v_w_ret_out),
           "w_conv_out": (w_conv_out, m_w_conv_out, v_w_conv_out), "w_o": (w_o, m_w_o, v_w_o)}
    sq_names = ("w_ret_out", "w_conv_out", "w_o")

    def adam_in(l, ride=None):
        big["w_in"], got = _adamw_layer(big["w_in"], *wts["w_in"], s_in[l], o_in[l], l, 0, ride=ride)
        return got

    s_in[0] = _sum_group_half(chip, cidx, gin[0], uin[0])
    s_sq[0] = _sum_group(chip, gsq[0], usq[0])
    gs[0], o_in[0], o_sq[0] = _tail_exchange(small[0], s_in[0], s_sq[0])
    square = None
    for l in reversed(range(NL)):
        adam_in(l)
        square = _adamw_square(square, [wts[n] for n in sq_names], s_sq[l], o_sq[l], l)
    for p, n in enumerate(sq_names):
        big[n] = square[4 * p:4 * p + 4]

    gsm = _sum_devices(gs)
    loss = jnp.sum(gsm[NL - 1, ROW_LOSS:ROW_LOSS + 8])
    grads = {
        "pre_norm_g": gsm[:, ROW_PRE], "conv_w": lax.dynamic_slice_in_dim(gsm[:, 0:CONV_K], chip[0] * Cc, Cc, axis=2),
        "conv_b": gsm[:, ROW_CB], "conv_ln_g": gsm[:, ROW_LG], "conv_ln_b": gsm[:, ROW_LB], "post_norm_g": gsm[:, ROW_POST],
    }
    weights = dict(pre_norm_g=pre_norm_g, conv_w=conv_w, conv_b=conv_b, conv_ln_g=conv_ln_g, conv_ln_b=conv_ln_b,
                   post_norm_g=post_norm_g)
    m1 = dict(pre_norm_g=m_pre_norm_g, conv_w=m_conv_w, conv_b=m_conv_b, conv_ln_g=m_conv_ln_g, conv_ln_b=m_conv_ln_b,
              post_norm_g=m_post_norm_g)
    m2 = dict(pre_norm_g=v_pre_norm_g, conv_w=v_conv_w, conv_b=v_conv_b, conv_ln_g=v_conv_ln_g, conv_ln_b=v_conv_ln_b,
              post_norm_g=v_post_norm_g)
    res = {n: (grads[n],) + _adamw(weights[n], grads[n], m1[n], m2[n]) for n in grads}
    res.update(big)
    order = ["pre_norm_g", "w_in", "w_ret_out", "conv_w", "conv_b", "conv_ln_g", "conv_ln_b", "w_conv_out", "w_o", "post_norm_g"]
    return (loss, grad_x[None], *[res[n][0] for n in order], *[res[n][1] for n in order],
            *[res[n][2] for n in order], *[res[n][3] for n in order])
```

```python
import numpy as np
import jax
import jax.numpy as jnp
from jax import lax
from jax.experimental import pallas as pl
from jax.experimental.pallas import tpu as pltpu

F32 = jnp.float32
BF16 = jnp.bfloat16

HEADS = 8
DK = 64
DV = 128
CONV_K = 31
CHUNK = 64
ROPE_BASE = 10000.0
EPS = 1e-6
HALO = 32
CONV_RB = 32
CONV_LANES = 512
CONV_TILE = 256
RET_BLOCK = 512

ADAM_LR = 0.001
ADAM_B1 = 0.9
ADAM_B2 = 0.999
ADAM_EPS = 1e-08
ADAM_WD = 0.01
ADAM_STEP = 10
ADAM_BLOCK_BYTES = 2 * 1024 * 1024

VMEM_LIMIT = 56 * 1024 * 1024
MESH_T = pl.DeviceIdType.MESH
ANY = pl.BlockSpec(memory_space=pl.ANY)

ROW_CB, ROW_LG, ROW_LB = 32, 33, 34
ROW_PRE, ROW_POST, ROW_LOSS = 40, 48, 56


def _cp(sem=None, **kw):
    return pltpu.CompilerParams(dimension_semantics=sem, vmem_limit_bytes=VMEM_LIMIT, **kw)


def _dot(a, b):
    return jnp.dot(a, b, preferred_element_type=F32)


def _dot_nt(a, b):
    return lax.dot_general(a, b, (((1,), (1,)), ((), ())), preferred_element_type=F32)


def _dot_tn(a, b):
    return lax.dot_general(a, b, (((0,), (0,)), ((), ())), preferred_element_type=F32)


def _sigmoid(x):
    return jax.nn.sigmoid(x)


def _silu(x):
    return x * _sigmoid(x)


def _rms(x, g):
    return x * lax.rsqrt(jnp.mean(x * x, axis=-1, keepdims=True) + EPS) * g


def _gn_gate(o, g):
    mu = jnp.mean(o, axis=-1, keepdims=True)
    d = o - mu
    var = jnp.mean(d * d, axis=-1, keepdims=True)
    return d * lax.rsqrt(var + EPS) * _silu(g)


def _ln_gate(y, gc, lg, lb):
    mu = jnp.mean(y, axis=-1, keepdims=True)
    d = y - mu
    var = jnp.mean(d * d, axis=-1, keepdims=True)
    return _silu(d * lax.rsqrt(var + EPS) * lg + lb) * _silu(gc)


def _tables(T, L):
    lane = np.arange(128)
    d = lane % DK
    half = DK // 2
    inv = (ROPE_BASE ** (-(np.arange(half, dtype=np.float32)) / half)).astype(np.float32)
    ang = (np.arange(T, dtype=np.float32)[:, None] * inv[None, :]).astype(np.float64)
    angl = ang[:, d % half]
    cos = np.cos(angl)
    sin = np.sin(angl)
    lo = (d < half)[None, :]
    rope = np.stack([cos, np.where(lo, -sin, 0.0), np.where(lo, 0.0, sin)]).astype(np.float32)

    hh = np.arange(HEADS, dtype=np.float64)
    log_g = np.log1p(-np.exp2(-5.0 - hh))
    n = np.arange(L, dtype=np.float64)
    cn = np.arange(L) // CHUNK
    allowed = (cn[None, :] <= cn[:, None])
    dist = np.abs(n[:, None] - n[None, :])
    mask = np.exp(log_g[:, None, None] * dist[None]) * allowed[None]
    mq = ((lane[None, :] // DK) == (np.arange(HEADS)[:, None] % 2)).astype(np.float64)
    qd = np.exp(log_g[:, None] * n[None, :])
    kd = np.exp(log_g[:, None] * (L - n[None, :]))
    qdm = qd[:, :, None] * mq[:, None, :] * (DK ** -0.5)
    kdm = kd[:, :, None] * mq[:, None, :]
    mqs = np.broadcast_to((mq * (DK ** -0.5))[:, None, :], (HEADS, 8, 128))
    cd = np.broadcast_to(np.exp(log_g * L)[:, None, None], (HEADS, 8, 128))
    f = lambda a: jnp.asarray(np.ascontiguousarray(a), dtype=F32)
    return dict(rope=f(rope), mask=f(mask), qdm=f(qdm), kdm=f(kdm), mqs=f(mqs), cd=f(cd))


def _rot(b, c, sl, sh):
    return b * c + pltpu.roll(b, 96, axis=1) * sl + pltpu.roll(b, 32, axis=1) * sh


def _rot_t(d, c, sl, sh):
    return d * c + pltpu.roll(d * sl, 32, axis=1) + pltpu.roll(d * sh, 96, axis=1)


def _place():
    return lax.axis_index("x"), lax.axis_index("y"), lax.axis_index("c")


def _other_chips(x, y):
    return [(1 - x, y), (x, 1 - y), (1 - x, 1 - y)]


def _remote(src, dst, send_sems, recv_sems, k, to):
    return pltpu.make_async_remote_copy(src_ref=src, dst_ref=dst, send_sem=send_sems.at[k], recv_sem=recv_sems.at[k],
                                        device_id=to, device_id_type=MESH_T)


class _Ride:
    def __init__(self):
        self.arrays, self.kinds, self.names = [], [], []
        self.fresh = []
        self.ops = []

    def read(self, name, a):
        self.names.append(name)
        self.arrays.append(a)
        self.kinds.append("in")

    def inout(self, name, a):
        self.names.append(name)
        self.arrays.append(a)
        self.kinds.append("inout")

    def land(self, name, shape, dtype):
        self.fresh.append((name, jax.ShapeDtypeStruct(shape, dtype)))

    def op(self, n_sems, start, finish):
        self.ops.append((n_sems, start, finish))


def _pcall(body, *, name, grid, in_specs, out_specs, out_shape, args, scratch_shapes=(), sem, aliases=None, ride=None):
    if ride is None or not ride.ops:
        outs = pl.pallas_call(body, name=name, grid=grid, in_specs=list(in_specs), out_specs=tuple(out_specs),
                              out_shape=tuple(out_shape), scratch_shapes=list(scratch_shapes),
                              input_output_aliases=dict(aliases or {}), compiler_params=_cp(sem))(*args)
        return outs, {}

    ni, no, nr = len(args), len(out_shape), len(ride.arrays)
    inout = [i for i, k in enumerate(ride.kinds) if k == "inout"]
    r_out_shapes = [jax.ShapeDtypeStruct(ride.arrays[i].shape, ride.arrays[i].dtype) for i in inout] + [s for _, s in ride.fresh]
    r_out_names = [ride.names[i] for i in inout] + [n for n, _ in ride.fresh]
    nro = len(r_out_shapes)
    n_sems = sum(n for n, _, _ in ride.ops)
    n_scr = len(scratch_shapes)
    nd = len(grid)

    def wrapped(*refs):
        ins, rin = refs[:ni], refs[ni:ni + nr]
        outs, rout = refs[ni + nr:ni + nr + no], refs[ni + nr + no:ni + nr + no + nro]
        scr = refs[ni + nr + no + nro:ni + nr + no + nro + n_scr]
        send_sems, recv_sems = refs[-2], refs[-1]
        view = {nm: r for nm, r, k in zip(ride.names, rin, ride.kinds) if k == "in"}
        view.update(dict(zip(r_out_names, rout)))
        first = pl.program_id(0) == 0
        last = pl.program_id(0) == grid[0] - 1
        for d in range(1, nd):
            first = first & (pl.program_id(d) == 0)
            last = last & (pl.program_id(d) == grid[d] - 1)

        @pl.when(first)
        def _():
            base = 0
            for n, start, _ in ride.ops:
                start(view, send_sems, recv_sems, base)
                base += n

        body(*ins, *outs, *scr)

        @pl.when(last)
        def _():
            base = 0
            for n, _, finish in ride.ops:
                finish(view, send_sems, recv_sems, base)
                base += n

    res = pl.pallas_call(
        wrapped, name=name, grid=grid,
        in_specs=list(in_specs) + [ANY] * nr, out_specs=tuple(out_specs) + (ANY,) * nro,
        out_shape=tuple(out_shape) + tuple(r_out_shapes),
        scratch_shapes=list(scratch_shapes) + [pltpu.SemaphoreType.DMA((n_sems,)), pltpu.SemaphoreType.DMA((n_sems,))],
        input_output_aliases={**dict(aliases or {}), **{ni + i: no + j for j, i in enumerate(inout)}},
        compiler_params=_cp(sem),
    )(*args, *ride.arrays)
    return res[:no], dict(zip(r_out_names, res[no:]))


def _half(ref, chip_idx, cc, part=(0, 1, 1)):
    n = ref.shape[1] // 2
    lo, hi, k = part
    return ref.at[chip_idx, pl.ds(cc * n + lo * n // k, (hi - lo) * n // k)]


def _ride_gather_ici(ride, name, a, part=(0, 1, 1)):
    ride.inout(name, a)

    def start(view, ss, rs, b):
        x, y, c = _place()
        mine = _half(view[name], 2 * x + y, c, part)
        for j, (cx, cy) in enumerate(_other_chips(x, y)):
            _remote(mine, mine, ss, rs, b + j, (cx, cy, c)).start()

    def finish(view, ss, rs, b):
        x, y, c = _place()
        mine = _half(view[name], 2 * x + y, c, part)
        for j, (cx, cy) in enumerate(_other_chips(x, y)):
            theirs = _half(view[name], 2 * cx + cy, c, part)
            _remote(theirs, theirs, ss, rs, b + j, (cx, cy, c)).wait_recv()
        for j, (cx, cy) in enumerate(_other_chips(x, y)):
            _remote(mine, mine, ss, rs, b + j, (cx, cy, c)).wait_send()

    ride.op(3, start, finish)


def _ride_gather_pass(ride, name, a):
    ride.inout(name, a)

    def start(view, ss, rs, b):
        x, y, c = _place()
        for j, (cx, cy) in enumerate(_other_chips(x, y)):
            blk = _half(view[name], 2 * cx + cy, c)
            _remote(blk, blk, ss, rs, b + j, (x, y, 1 - c)).start()

    def finish(view, ss, rs, b):
        x, y, c = _place()
        for j, (cx, cy) in enumerate(_other_chips(x, y)):
            theirs = _half(view[name], 2 * cx + cy, 1 - c)
            _remote(theirs, theirs, ss, rs, b + j, (x, y, 1 - c)).wait_recv()
        for j, (cx, cy) in enumerate(_other_chips(x, y)):
            blk = _half(view[name], 2 * cx + cy, c)
            _remote(blk, blk, ss, rs, b + j, (x, y, 1 - c)).wait_send()

    ride.op(3, start, finish)


def _ride_exchange(ride, src_name, src, dst_name, dst, plan):
    ride.read(src_name, src)
    if dst is None:
        ride.land(dst_name, (3,) + src.shape[1:], src.dtype)
    else:
        ride.inout(dst_name, dst)
    A = src.shape[1]

    def copy(view, ss, rs, sem, j, part, x, y, c):
        lo, hi, k = part
        rows = pl.ds(lo * A // k, (hi - lo) * A // k)
        cx, cy = _other_chips(x, y)[j]
        return _remote(view[src_name].at[2 * cx + cy, rows], view[dst_name].at[j, rows], ss, rs, sem, (cx, cy, c))

    def start(view, ss, rs, b):
        x, y, c = _place()
        for i, (j, part) in enumerate(plan):
            copy(view, ss, rs, b + i, j, part, x, y, c).start()

    def finish(view, ss, rs, b):
        x, y, c = _place()
        for i, (j, part) in enumerate(plan):
            copy(view, ss, rs, b + i, j, part, x, y, c).wait()

    ride.op(len(plan), start, finish)


WHOLE, FIRST_HALF, SECOND_HALF = (0, 1, 1), (0, 1, 2), (1, 2, 2)


def _ride_swap(ride, src_name, src, dst_name):
    ride.read(src_name, src)
    ride.land(dst_name, src.shape, src.dtype)

    def start(view, ss, rs, b):
        x, y, c = _place()
        _remote(view[src_name], view[dst_name], ss, rs, b, (x, y, 1 - c)).start()

    def finish(view, ss, rs, b):
        x, y, c = _place()
        _remote(view[src_name], view[dst_name], ss, rs, b, (x, y, 1 - c)).wait()

    ride.op(1, start, finish)


def _peers(x, y, c):
    flip = lambda v, b: 1 - v if b else v
    return [(flip(x, r & 4), flip(y, r & 2), flip(c, r & 1)) for r in range(1, 8)]


def _ride_gather_all(ride, src_name, src, dst_name):
    ride.read(src_name, src)
    ride.land(dst_name, (8,) + src.shape, src.dtype)

    def start(view, ss, rs, b):
        x, y, c = _place()
        me = 4 * x + 2 * y + c
        pltpu.make_async_copy(view[src_name], view[dst_name].at[me], ss.at[b + 7]).start()
        for r, peer in enumerate(_peers(x, y, c)):
            _remote(view[src_name], view[dst_name].at[me], ss, rs, b + r, peer).start()

    def finish(view, ss, rs, b):
        x, y, c = _place()
        me = 4 * x + 2 * y + c
        for r, peer in enumerate(_peers(x, y, c)):
            theirs = view[dst_name].at[4 * peer[0] + 2 * peer[1] + peer[2]]
            _remote(theirs, theirs, ss, rs, b + r, peer).wait_recv()
        for r, peer in enumerate(_peers(x, y, c)):
            _remote(view[src_name], view[dst_name].at[me], ss, rs, b + r, peer).wait_send()
        pltpu.make_async_copy(view[src_name], view[dst_name].at[me], ss.at[b + 7]).wait()

    ride.op(8, start, finish)


def _cast_win(chip, w_in, l):
    _, D, W = w_in.shape
    tr = min(256, D)

    def body(chip_ref, w_ref, o_ref):
        o_ref[...] = w_ref[...].astype(BF16)

    return pl.pallas_call(
        body, name="cast_win",
        out_shape=jax.ShapeDtypeStruct((4, D, W), BF16),
        grid_spec=pltpu.PrefetchScalarGridSpec(
            num_scalar_prefetch=1, grid=(D // tr,),
            in_specs=[pl.BlockSpec((None, tr, W), lambda r, c: (l, r, 0))],
            out_specs=pl.BlockSpec((None, tr, W), lambda r, c: (c[0], r, 0))),
        compiler_params=_cp(("arbitrary",)),
    )(chip, w_in)


def _cast_wsq(chip, w_ro, w_co, w_o, l):
    _, R, D = w_ro.shape

    def body(chip_ref, a_ref, b_ref, c_ref, o_ref):
        o_ref[0:R, :] = a_ref[...].astype(BF16)
        o_ref[R:2 * R, :] = b_ref[...].astype(BF16)
        o_ref[2 * R:3 * R, :] = c_ref[...].astype(BF16)

    spec = pl.BlockSpec((None, R, D), lambda i, c: (l, 0, 0))
    return pl.pallas_call(
        body, name="cast_wsq",
        out_shape=jax.ShapeDtypeStruct((4, 3 * R, D), BF16),
        grid_spec=pltpu.PrefetchScalarGridSpec(
            num_scalar_prefetch=1, grid=(1,),
            in_specs=[spec, spec, spec],
            out_specs=pl.BlockSpec((None, 3 * R, D), lambda i, c: (c[0], 0, 0))),
        compiler_params=_cp(("arbitrary",)),
    )(chip, w_ro, w_co, w_o)


def _place_cw(chip, conv_w):
    NL, K, Cc = conv_w.shape

    def body(chip_ref, w_ref, o_ref):
        o_ref[...] = w_ref[...]

    return pl.pallas_call(
        body, name="place_cw",
        out_shape=jax.ShapeDtypeStruct((4, NL, K, Cc), F32),
        grid_spec=pltpu.PrefetchScalarGridSpec(
            num_scalar_prefetch=1, grid=(1,),
            in_specs=[pl.BlockSpec((NL, K, Cc), lambda i, c: (0, 0, 0))],
            out_specs=pl.BlockSpec((None, NL, K, Cc), lambda i, c: (c[0], 0, 0, 0))),
        compiler_params=_cp(("arbitrary",)),
    )(chip, conv_w)


def _gather_first(win0, wcw):
    n_arr = 2

    def body(a0, a1, o0, o1, send_sems, recv_sems):
        x, y, c = _place()
        sibling = (x, y, 1 - c)
        chips = _other_chips(x, y)
        outs = (o0, o1)

        def copy(k, a, cx, cy, cc, to):
            blk = _half(outs[a], 2 * cx + cy, cc)
            return _remote(blk, blk, send_sems, recv_sems, k, to)

        first = [copy(3 * a + j, a, x, y, c, (*chip, c)) for a in range(n_arr) for j, chip in enumerate(chips)]
        for cp in first:
            cp.start()
        passed = [copy(3 * n_arr + 3 * a + j, a, *chip, c, sibling) for a in range(n_arr) for j, chip in enumerate(chips)]
        for a in range(n_arr):
            for j, chip in enumerate(chips):
                copy(3 * a + j, a, *chip, c, sibling).wait_recv()
                passed[3 * a + j].start()
        for a in range(n_arr):
            for j, chip in enumerate(chips):
                copy(3 * n_arr + 3 * a + j, a, *chip, 1 - c, sibling).wait_recv()
        for cp in first + passed:
            cp.wait_send()

    ins = (win0, wcw)
    return pl.pallas_call(
        body, name="gather_first",
        out_shape=tuple(jax.ShapeDtypeStruct(a.shape, a.dtype) for a in ins),
        in_specs=[ANY] * n_arr, out_specs=(ANY,) * n_arr,
        scratch_shapes=[pltpu.SemaphoreType.DMA((6 * n_arr,)), pltpu.SemaphoreType.DMA((6 * n_arr,))],
        input_output_aliases={0: 0, 1: 1},
        compiler_params=pltpu.CompilerParams(has_side_effects=True),
    )(*ins)


def _fwd_in(x, pre_g, win, l, ride=None):
    T, D = x.shape
    W = win.shape[-1]
    tm = min(512, T)

    def body(x_ref, g_ref, w_ref, p_ref, h_ref):
        hb = _rms(x_ref[...], g_ref[...]).astype(BF16)
        h_ref[...] = hb
        for j in range(4):
            p_ref[:, j * W:(j + 1) * W] = _dot(hb, w_ref[j]).astype(BF16)

    return _pcall(
        body, name="fwd_in",
        out_shape=(jax.ShapeDtypeStruct((T, 4 * W), BF16), jax.ShapeDtypeStruct((T, D), BF16)),
        grid=(T // tm,),
        in_specs=[pl.BlockSpec((tm, D), lambda i: (i, 0)),
                  pl.BlockSpec((None, 1, D), lambda i: (l, 0, 0)),
                  pl.BlockSpec((4, D, W), lambda i: (0, 0, 0), pipeline_mode=pl.Buffered(1))],
        out_specs=(pl.BlockSpec((tm, 4 * W), lambda i: (i, 0)),
                   pl.BlockSpec((tm, D), lambda i: (i, 0))),
        sem=("arbitrary",), args=(x, pre_g, win), ride=ride)


def _ret_specs(T, L):
    rope = pl.BlockSpec((3, L, 128), lambda s: (0, s, 0))
    mask = pl.BlockSpec((HEADS, L, L), lambda s: (0, 0, 0), pipeline_mode=pl.Buffered(1))
    qdm = pl.BlockSpec((HEADS, L, 128), lambda s: (0, 0, 0), pipeline_mode=pl.Buffered(1))
    small = pl.BlockSpec((HEADS, 8, 128), lambda s: (0, 0, 0))
    return [rope, mask, qdm, qdm, small, small]


QK = HEADS * DK
VW = HEADS * DV
PW = 2 * QK + 2 * VW


def _zero_at_start(ref):
    @pl.when(pl.program_id(0) == 0)
    def _():
        ref[...] = jnp.zeros_like(ref)


def _ret_fwd_part(p_ref, rope_ref, m_ref, qdm_ref, kdm_ref, mqs_ref, cd_ref, a_ref, st_ref, state):
    c, sl, sh = rope_ref[0], rope_ref[1], rope_ref[2]
    for j in range(HEADS // 2):
        rq = _rot(p_ref[:, 128 * j:128 * (j + 1)].astype(F32), c, sl, sh)
        rk = _rot(p_ref[:, QK + 128 * j:QK + 128 * (j + 1)].astype(F32), c, sl, sh)
        rkb = rk.astype(BF16)
        for e in range(2):
            h = 2 * j + e
            v = p_ref[:, 2 * QK + DV * h:2 * QK + DV * (h + 1)]
            g = p_ref[:, 2 * QK + VW + DV * h:2 * QK + VW + DV * (h + 1)].astype(F32)
            a = (rq * mqs_ref[h, 0:1, :]).astype(BF16)
            p = (_dot_nt(a, rkb) * m_ref[h]).astype(BF16)
            st = state[h]
            st_ref[h] = st
            o = _dot(p, v) + _dot((rq * qdm_ref[h]).astype(BF16), st.astype(BF16))
            state[h] = st * cd_ref[h, 0:1, :] + _dot_tn((rk * kdm_ref[h]).astype(BF16), v)
            a_ref[:, DV * h:DV * (h + 1)] = _gn_gate(o, g).astype(BF16)


def _shift_copies(src, sh):
    rows = sh.shape[1]
    for b in range(1, 8):
        sh[b - 1, :, :] = src[pl.ds(b, rows), :]


def _window(src, sh, r0, const, rows, lanes):
    b = const % 8
    at = pl.ds(pl.multiple_of(r0 + (const - b), 8), rows)
    w = src[at, lanes] if b == 0 else sh[b - 1, at, lanes]
    return w.reshape(rows // 8, 8, w.shape[-1])


def _conv_taps(wbuf, src, sh, r0, const, rows, lanes):
    groups = rows // 8
    accs = [None] * groups
    for k in range(CONV_K):
        w8 = wbuf[pl.ds(8 * k, 8), lanes]
        win = _window(src, sh, r0, const + k, rows, lanes)
        for g in range(groups):
            term = w8 * win[g]
            accs[g] = term if k == 0 else accs[g] + term
    return jnp.concatenate(accs, axis=0)


def _lane_parts(C):
    return [pl.ds(j * CONV_LANES, CONV_LANES) for j in range(C // CONV_LANES)]


def _load_conv_w(cw_ref, wbuf, flip):
    for k in range(CONV_K):
        row = jnp.concatenate([cw_ref[c, pl.ds(k, 1), :] for c in range(4)], axis=-1)
        kk = CONV_K - 1 - k if flip else k
        wbuf[pl.ds(8 * kk, 8), :] = jnp.broadcast_to(row, (8, row.shape[-1]))


def _conv_fwd_part(p_ref, cw_ref, cb_ref, lg_ref, lb_ref, a_ref, y_ref, ubuf, wbuf, ush):
    tc, C = y_ref.shape
    off = HALO - (CONV_K - 1)
    i = pl.program_id(0)

    @pl.when(i == 0)
    def _():
        ubuf[0:HALO, :] = jnp.zeros((HALO, C), F32)
        _load_conv_w(cw_ref, wbuf, False)

    @pl.when(i > 0)
    def _():
        ubuf[0:HALO, :] = ubuf[tc:tc + HALO, :]

    ga = p_ref[:, 0:C].astype(F32)
    gb = p_ref[:, C:2 * C].astype(F32)
    ubuf[HALO:HALO + tc, :] = ga * _sigmoid(gb)
    _shift_copies(ubuf, ush)

    def rows_block(r, carry):
        r0 = pl.multiple_of(r * CONV_RB, CONV_RB)
        for lanes in _lane_parts(C):
            y_ref[pl.ds(r0, CONV_RB), lanes] = _conv_taps(wbuf, ubuf, ush, r0, off, CONV_RB, lanes) + cb_ref[:, lanes]
        return carry

    lax.fori_loop(0, tc // CONV_RB, rows_block, 0)
    gc = p_ref[:, 2 * C:3 * C].astype(F32)
    a_ref[...] = _ln_gate(y_ref[...], gc, lg_ref[...], lb_ref[...]).astype(BF16)


def _ret_fwd(proj, tb, L, ride=None):
    T = proj.shape[0]
    nS = T // L

    def body(p_ref, rope_ref, m_ref, qdm_ref, kdm_ref, mqs_ref, cd_ref, ar_ref, st_ref, state):
        _zero_at_start(state)
        _ret_fwd_part(p_ref, rope_ref, m_ref, qdm_ref, kdm_ref, mqs_ref, cd_ref, ar_ref, st_ref, state)

    return _pcall(
        body, name="ret_fwd",
        out_shape=(jax.ShapeDtypeStruct((T, VW), BF16), jax.ShapeDtypeStruct((nS, HEADS, 128, DV), F32)),
        grid=(nS,),
        in_specs=[pl.BlockSpec((L, PW), lambda s: (s, 0))] + _ret_specs(T, L),
        out_specs=(pl.BlockSpec((L, VW), lambda s: (s, 0)), pl.BlockSpec((None, HEADS, 128, DV), lambda s: (s, 0, 0, 0))),
        scratch_shapes=[pltpu.VMEM((HEADS, 128, DV), F32)],
        sem=("arbitrary",), ride=ride,
        args=(proj, tb["rope"], tb["mask"], tb["qdm"], tb["kdm"], tb["mqs"], tb["cd"]))


def _conv_fwd(proj, wcw, conv_b, ln_g, ln_b, l, ride=None):
    T = proj.shape[0]
    C = conv_b.shape[-1]
    Cc = wcw.shape[-1]
    tc = min(CONV_TILE, T)
    assert PW == 3 * C

    def body(p_ref, cw_ref, cb_ref, lg_ref, lb_ref, ac_ref, y_ref, ubuf, wbuf, ush):
        _conv_fwd_part(p_ref, cw_ref, cb_ref, lg_ref, lb_ref, ac_ref, y_ref, ubuf, wbuf, ush)

    vec = pl.BlockSpec((None, 1, C), lambda i: (l, 0, 0))
    tile = pl.BlockSpec((tc, C), lambda i: (i, 0))
    return _pcall(
        body, name="conv_fwd",
        out_shape=(jax.ShapeDtypeStruct((T, C), BF16), jax.ShapeDtypeStruct((T, C), F32)),
        grid=(T // tc,),
        in_specs=[pl.BlockSpec((tc, 3 * C), lambda i: (i, 1)),
                  pl.BlockSpec((4, None, CONV_K, Cc), lambda i: (0, l, 0, 0)), vec, vec, vec],
        out_specs=(tile, tile),
        scratch_shapes=[pltpu.VMEM((HALO + tc, C), F32), pltpu.VMEM((8 * CONV_K, C), F32),
                        pltpu.VMEM((7, HALO + tc - 8, C), F32)],
        sem=("arbitrary",), ride=ride, args=(proj, wcw, conv_b, ln_g, ln_b))


def _merge_fwd(x, proj, a_ret, a_conv, wsq, post_g, l, ride=None):
    T, D = x.shape
    R = wsq.shape[1] // 3
    tm = min(512, T)

    def body(x_ref, p_ref, ar_ref, ac_ref, wro_ref, wco_ref, wo_ref, g_ref, xn_ref, ro_ref, co_ref, ym_ref, z_ref):
        ro = _dot(ar_ref[...], wro_ref[...].reshape(4 * R, D))
        co = _dot(ac_ref[...], wco_ref[...].reshape(4 * R, D))
        ym = (_sigmoid(p_ref[:, 0:D].astype(F32)) * ro + _sigmoid(p_ref[:, D:2 * D].astype(F32)) * co).astype(BF16)
        z = _dot(ym, wo_ref[...].reshape(4 * R, D))
        ro_ref[...] = ro.astype(BF16)
        co_ref[...] = co.astype(BF16)
        ym_ref[...] = ym
        z_ref[...] = z.astype(BF16)
        xn_ref[...] = x_ref[...] + _rms(z, g_ref[...])

    tile = pl.BlockSpec((tm, D), lambda i: (i, 0))
    wspec = lambda m: pl.BlockSpec((4, R, D), lambda i: (0, m, 0))
    act = jax.ShapeDtypeStruct((T, D), BF16)
    return _pcall(
        body, name="merge_fwd",
        out_shape=(jax.ShapeDtypeStruct((T, D), F32), act, act, act, act),
        grid=(T // tm,),
        in_specs=[tile, pl.BlockSpec((tm, 2 * D), lambda i: (i, 3)), tile, tile,
                  wspec(0), wspec(1), wspec(2), pl.BlockSpec((None, 1, D), lambda i: (l, 0, 0))],
        out_specs=(tile, tile, tile, tile, tile),
        sem=("arbitrary",), ride=ride, args=(x, proj, a_ret, a_conv, wsq, wsq, wsq, post_g))


def _loss_fwd_bwd(y, target):
    T, D = y.shape
    tm = min(512, T)

    def body(y_ref, t_ref, dy_ref, ls_ref):
        @pl.when(pl.program_id(0) == 0)
        def _():
            ls_ref[...] = jnp.zeros_like(ls_ref)

        e = y_ref[...] - t_ref[...]
        dy_ref[...] = e * (1.0 / D)
        ls_ref[...] += jnp.sum((e * e).reshape(tm // 8, 8, D), axis=0) * (0.5 / D)

    tile = pl.BlockSpec((tm, D), lambda i: (i, 0))
    return pl.pallas_call(
        body, name="loss",
        out_shape=(jax.ShapeDtypeStruct((T, D), F32), jax.ShapeDtypeStruct((8, D), F32)),
        grid=(T // tm,),
        in_specs=[tile, tile],
        out_specs=(tile, pl.BlockSpec((8, D), lambda i: (0, 0))),
        compiler_params=_cp(("arbitrary",)),
    )(y, target)


def _merge_bwd(dxn, proj, a_ret, a_conv, ro, co, ym, z, wsq, post_g, l, ride=None):
    T, D = dxn.shape
    R = wsq.shape[1] // 3
    tm = min(512, T)
    n = T // tm

    def body(dx_ref, p_ref, ar_ref, ac_ref, ro_ref, co_ref, ym_ref, z_ref, wro_ref, wco_ref, wo_ref, g_ref,
             dp_ref, dar_ref, dac_ref, gsq_ref, dg_ref, acc, stage):
        i = pl.program_id(0)

        @pl.when(i == 0)
        def _():
            acc[...] = jnp.zeros_like(acc)
            dg_ref[...] = jnp.zeros_like(dg_ref)

        _, vjp = jax.vjp(_rms, z_ref[...].astype(F32), g_ref[...])
        dz, dg = vjp(dx_ref[...])
        dg_ref[0:1, :] += dg
        dzb = dz.astype(BF16)
        dym = _dot_nt(dzb, wo_ref[...].reshape(4 * R, D))
        acc[2] += _dot_tn(ym_ref[...], dzb)
        sr = _sigmoid(p_ref[:, 0:D].astype(F32))
        sc = _sigmoid(p_ref[:, D:2 * D].astype(F32))
        rov = ro_ref[...].astype(F32)
        cov = co_ref[...].astype(F32)
        dp_ref[:, 0:D] = (dym * rov * sr * (1.0 - sr)).astype(BF16)
        dp_ref[:, D:2 * D] = (dym * cov * sc * (1.0 - sc)).astype(BF16)
        dro = (dym * sr).astype(BF16)
        dco = (dym * sc).astype(BF16)
        dar_ref[...] = _dot_nt(dro, wro_ref[...].reshape(4 * R, D)).astype(BF16)
        dac_ref[...] = _dot_nt(dco, wco_ref[...].reshape(4 * R, D)).astype(BF16)
        acc[0] += _dot_tn(ar_ref[...], dro)
        acc[1] += _dot_tn(ac_ref[...], dco)

        @pl.when(i == n - 1)
        def _():
            for m in range(3):
                stage[...] = acc[m].astype(BF16).reshape(4, R, D)
                pltpu.sync_copy(stage, gsq_ref.at[:, pl.ds(m * R, R), :])

    tile = pl.BlockSpec((tm, D), lambda i: (i, 0))
    wspec = lambda m: pl.BlockSpec((4, R, D), lambda i: (0, m, 0), pipeline_mode=pl.Buffered(1))
    return _pcall(
        body, name="merge_bwd",
        out_shape=(jax.ShapeDtypeStruct(proj.shape, BF16), jax.ShapeDtypeStruct((T, D), BF16),
                   jax.ShapeDtypeStruct((T, D), BF16), jax.ShapeDtypeStruct(wsq.shape, BF16),
                   jax.ShapeDtypeStruct((8, D), F32)),
        grid=(n,),
        in_specs=[tile, pl.BlockSpec((tm, 2 * D), lambda i: (i, 3)), tile, tile, tile, tile, tile, tile,
                  wspec(0), wspec(1), wspec(2), pl.BlockSpec((None, 1, D), lambda i: (l, 0, 0))],
        out_specs=(pl.BlockSpec((tm, 2 * D), lambda i: (i, 3)), tile, tile, ANY, pl.BlockSpec((8, D), lambda i: (0, 0))),
        scratch_shapes=[pltpu.VMEM((3, 4 * R, D), F32), pltpu.VMEM((4, R, D), BF16)],
        sem=("arbitrary",), args=(dxn, proj, a_ret, a_conv, ro, co, ym, z, wsq, wsq, wsq, post_g), ride=ride)


def _conv_bwd_part(n, da_ref, y_ref, p_ref, ph_ref, cw_ref, lg_ref, lb_ref, dp_ref,
                   dcbuf, ubuf, dubuf, wbuf, dwacc, vacc, dsh, ush):
    tc, C = y_ref.shape
    off = HALO - (CONV_K - 1)
    nrb = tc // CONV_RB
    t = pl.program_id(0)
    i = n - 1 - t

    @pl.when(t == 0)
    def _():
        dcbuf[tc:tc + HALO, :] = jnp.zeros((HALO, C), F32)
        dwacc[...] = jnp.zeros_like(dwacc)
        vacc[...] = jnp.zeros_like(vacc)
        _load_conv_w(cw_ref, wbuf, True)

    @pl.when(t > 0)
    def _():
        dcbuf[tc:tc + HALO, :] = dcbuf[0:HALO, :]

    gc = p_ref[:, 2 * C:3 * C].astype(F32)
    _, vjp = jax.vjp(_ln_gate, y_ref[...], gc, lg_ref[...], lb_ref[...])
    dy, dgc, dlg, dlb = vjp(da_ref[...].astype(F32))
    dcbuf[0:tc, :] = dy
    dp_ref[:, 2 * C:3 * C] = dgc.astype(BF16)
    vacc[0:1, :] += jnp.sum(dy, axis=0, keepdims=True)
    vacc[1:2, :] += dlg
    vacc[2:3, :] += dlb

    ga = p_ref[:, 0:C].astype(F32)
    sb = _sigmoid(p_ref[:, C:2 * C].astype(F32))
    ubuf[HALO:HALO + tc, :] = ga * sb
    uh = ph_ref[:, 0:C].astype(F32) * _sigmoid(ph_ref[:, C:2 * C].astype(F32))
    ubuf[0:HALO, :] = jnp.where(i > 0, uh, 0.0)

    _shift_copies(dcbuf, dsh)
    _shift_copies(ubuf, ush)
    def du_block(r, carry):
        r0 = pl.multiple_of(r * CONV_RB, CONV_RB)
        for lanes in _lane_parts(C):
            dubuf[pl.ds(r0, CONV_RB), lanes] = _conv_taps(wbuf, dcbuf, dsh, r0, 0, CONV_RB, lanes)
        return carry

    lax.fori_loop(0, nrb, du_block, 0)
    du = dubuf[...]
    dp_ref[:, 0:C] = (du * sb).astype(BF16)
    dp_ref[:, C:2 * C] = (du * ga * sb * (1.0 - sb)).astype(BF16)

    def dw_block(r, carry):
        r0 = pl.multiple_of(r * CONV_RB, CONV_RB)
        for lanes in _lane_parts(C):
            dyb = dcbuf[pl.ds(r0, CONV_RB), lanes].reshape(CONV_RB // 8, 8, CONV_LANES)
            for k in range(CONV_K):
                dwacc[8 * k:8 * k + 8, lanes] += jnp.sum(dyb * _window(ubuf, ush, r0, off + k, CONV_RB, lanes), axis=0)
        return carry

    lax.fori_loop(0, nrb, dw_block, 0)


def _conv_bwd_final(n, sg_ref, dwacc, vacc):
    C = sg_ref.shape[-1]

    @pl.when(pl.program_id(0) == n - 1)
    def _():
        for k in range(CONV_K):
            sg_ref[pl.ds(k, 1), :] = jnp.sum(dwacc[8 * k:8 * k + 8, :], axis=0, keepdims=True)
        sg_ref[pl.ds(CONV_K, 1), :] = jnp.zeros((1, C), F32)
        sg_ref[ROW_CB:ROW_CB + 8, :] = jnp.zeros((8, C), F32)
        sg_ref[ROW_CB:ROW_CB + 3, :] = vacc[0:3, :]


def _ret_bwd_part(da_ref, p_ref, st_ref, rope_ref, m_ref, qdm_ref, kdm_ref, mqs_ref, cd_ref, dp_ref, gst):
    c, sl, sh = rope_ref[0], rope_ref[1], rope_ref[2]
    for j in range(HEADS // 2):
        rq = _rot(p_ref[:, 128 * j:128 * (j + 1)].astype(F32), c, sl, sh)
        rk = _rot(p_ref[:, QK + 128 * j:QK + 128 * (j + 1)].astype(F32), c, sl, sh)
        rkb = rk.astype(BF16)
        drq = jnp.zeros_like(rq)
        drk = jnp.zeros_like(rk)
        for e in range(2):
            h = 2 * j + e
            v = p_ref[:, 2 * QK + DV * h:2 * QK + DV * (h + 1)]
            g = p_ref[:, 2 * QK + VW + DV * h:2 * QK + VW + DV * (h + 1)].astype(F32)
            mqs = mqs_ref[h, 0:1, :]
            a = (rq * mqs).astype(BF16)
            aq = (rq * qdm_ref[h]).astype(BF16)
            kdv = (rk * kdm_ref[h]).astype(BF16)
            mk = m_ref[h]
            p = (_dot_nt(a, rkb) * mk).astype(BF16)
            stb = st_ref[h].astype(BF16)
            o = _dot(p, v) + _dot(aq, stb)
            _, vjp = jax.vjp(_gn_gate, o, g)
            do, dg = vjp(da_ref[:, DV * h:DV * (h + 1)].astype(F32))
            dob = do.astype(BF16)
            gs = gst[h]
            gsb = gs.astype(BF16)
            ds = (_dot_nt(dob, v) * mk).astype(BF16)
            drq = drq + _dot(ds, rkb) * mqs + _dot_nt(dob, stb) * qdm_ref[h]
            drk = drk + _dot_tn(ds, a) + _dot_nt(v, gsb) * kdm_ref[h]
            dv = _dot_tn(p, dob) + _dot(kdv, gsb)
            gst[h] = _dot_tn(aq, dob) + gs * cd_ref[h, 0:1, :]
            dp_ref[:, 2 * QK + DV * h:2 * QK + DV * (h + 1)] = dv.astype(BF16)
            dp_ref[:, 2 * QK + VW + DV * h:2 * QK + VW + DV * (h + 1)] = dg.astype(BF16)
        dp_ref[:, 128 * j:128 * (j + 1)] = _rot_t(drq, c, sl, sh).astype(BF16)
        dp_ref[:, QK + 128 * j:QK + 128 * (j + 1)] = _rot_t(drk, c, sl, sh).astype(BF16)


def _ret_bwd(dproj, da_ret, proj, states, tb, L, ride=None):
    T = proj.shape[0]
    nS = T // L

    def body(dpin_ref, dar_ref, p_ref, st_ref, rope_ref, m_ref, qdm_ref, kdm_ref, mqs_ref, cd_ref, dp_ref, gst):
        _zero_at_start(gst)
        _ret_bwd_part(dar_ref, p_ref, st_ref, rope_ref, m_ref, qdm_ref, kdm_ref, mqs_ref, cd_ref, dp_ref, gst)

    rev = lambda s: nS - 1 - s
    specs = _ret_specs(T, L)
    specs[0] = pl.BlockSpec((3, L, 128), lambda s: (0, rev(s), 0))
    ptile = pl.BlockSpec((L, PW), lambda s: (rev(s), 0))
    return _pcall(
        body, name="ret_bwd",
        out_shape=(jax.ShapeDtypeStruct(dproj.shape, BF16),),
        grid=(nS,),
        in_specs=[ANY, pl.BlockSpec((L, VW), lambda s: (rev(s), 0)), ptile,
                  pl.BlockSpec((None, HEADS, 128, DV), lambda s: (rev(s), 0, 0, 0))] + specs,
        out_specs=(ptile,),
        scratch_shapes=[pltpu.VMEM((HEADS, 128, DV), F32)],
        sem=("arbitrary",), aliases={0: 0}, ride=ride,
        args=(dproj, da_ret, proj, states, tb["rope"], tb["mask"], tb["qdm"], tb["kdm"], tb["mqs"], tb["cd"]))


def _conv_bwd(dproj, da_conv, y, proj, wcw, ln_g, ln_b, l, ride=None):
    T, C = y.shape
    Cc = wcw.shape[-1]
    tc = min(CONV_TILE, T)
    n = T // tc
    hb = tc // HALO

    def body(dpin_ref, dac_ref, y_ref, p_ref, ph_ref, cw_ref, lg_ref, lb_ref, dp_ref, sg_ref,
             dcbuf, ubuf, dubuf, wbuf, dwacc, vacc, dsh, ush):
        _conv_bwd_part(n, dac_ref, y_ref, p_ref, ph_ref, cw_ref, lg_ref, lb_ref, dp_ref,
                       dcbuf, ubuf, dubuf, wbuf, dwacc, vacc, dsh, ush)
        _conv_bwd_final(n, sg_ref, dwacc, vacc)

    rev = lambda t: n - 1 - t
    vec = pl.BlockSpec((None, 1, C), lambda t: (l, 0, 0))
    tile = pl.BlockSpec((tc, C), lambda t: (rev(t), 0))
    ptile = pl.BlockSpec((tc, 3 * C), lambda t: (rev(t), 1))
    halo = pl.BlockSpec((HALO, 3 * C), lambda t: (jnp.maximum(rev(t) * hb - 1, 0), 1))
    return _pcall(
        body, name="conv_bwd",
        out_shape=(jax.ShapeDtypeStruct(dproj.shape, BF16), jax.ShapeDtypeStruct((ROW_PRE, C), F32)),
        grid=(n,),
        in_specs=[ANY, tile, tile, ptile, halo, pl.BlockSpec((4, None, CONV_K, Cc), lambda t: (0, l, 0, 0)), vec, vec],
        out_specs=(ptile, pl.BlockSpec((ROW_PRE, C), lambda t: (0, 0))),
        scratch_shapes=[pltpu.VMEM((tc + HALO, C), F32), pltpu.VMEM((HALO + tc, C), F32), pltpu.VMEM((tc, C), F32),
                        pltpu.VMEM((8 * CONV_K, C), F32), pltpu.VMEM((8 * CONV_K, C), F32), pltpu.VMEM((8, C), F32),
                        pltpu.VMEM((7, HALO + tc - 8, C), F32), pltpu.VMEM((7, HALO + tc - 8, C), F32)],
        sem=("arbitrary",), aliases={0: 0}, ride=ride, args=(dproj, da_conv, y, proj, proj, wcw, ln_g, ln_b))


def _win_grad(h, dproj, W, ride=None):
    T, D = h.shape
    tk = min(2048, T)
    nk = T // tk

    def body(h_ref, dp_ref, g_ref, acc):
        k = pl.program_id(1)

        @pl.when(k == 0)
        def _():
            acc[...] = jnp.zeros_like(acc)

        acc[...] += _dot_tn(h_ref[...], dp_ref[...])

        @pl.when(k == nk - 1)
        def _():
            g_ref[...] = acc[...].astype(BF16)

    return _pcall(
        body, name="win_grad",
        out_shape=(jax.ShapeDtypeStruct((4, D, W), BF16),),
        grid=(4, nk),
        in_specs=[pl.BlockSpec((tk, D), lambda j, k: (k, 0)), pl.BlockSpec((tk, W), lambda j, k: (k, j))],
        out_specs=(pl.BlockSpec((None, D, W), lambda j, k: (j, 0, 0)),),
        scratch_shapes=[pltpu.VMEM((D, W), F32)],
        sem=("arbitrary", "arbitrary"), args=(h, dproj), ride=ride)


def _in_bwd(dxn, dproj, x, pre_g, win, l, ride=None):
    T, D = x.shape
    W = win.shape[-1]
    tm = min(512, T)

    def body(dxn_ref, dp_ref, x_ref, g_ref, w_ref, dx_ref, dg_ref):
        @pl.when(pl.program_id(0) == 0)
        def _():
            dg_ref[...] = jnp.zeros_like(dg_ref)

        dh = _dot_nt(dp_ref[:, 0:W], w_ref[0])
        for j in range(1, 4):
            dh = dh + _dot_nt(dp_ref[:, j * W:(j + 1) * W], w_ref[j])
        _, vjp = jax.vjp(_rms, x_ref[...], g_ref[...])
        dx, dg = vjp(dh)
        dx_ref[...] = dxn_ref[...] + dx
        dg_ref[0:1, :] += dg

    tile = pl.BlockSpec((tm, D), lambda i: (i, 0))
    return _pcall(
        body, name="in_bwd",
        out_shape=(jax.ShapeDtypeStruct((T, D), F32), jax.ShapeDtypeStruct((8, D), F32)),
        grid=(T // tm,),
        in_specs=[tile, pl.BlockSpec((tm, 4 * W), lambda i: (i, 0)), tile,
                  pl.BlockSpec((None, 1, D), lambda i: (l, 0, 0)),
                  pl.BlockSpec((4, D, W), lambda i: (0, 0, 0), pipeline_mode=pl.Buffered(1))],
        out_specs=(tile, pl.BlockSpec((8, D), lambda i: (0, 0))),
        sem=("arbitrary",), args=(dxn, dproj, x, pre_g, win), ride=ride)


def _sum_group(chip, t, u):
    _, A, B = t.shape
    tr = min(256, A)

    def body(k_ref, t_ref, u_ref, o_ref):
        o_ref[...] = ((t_ref[...].astype(F32) + u_ref[0].astype(F32)) + u_ref[1].astype(F32)) + u_ref[2].astype(F32)

    return pl.pallas_call(
        body, name="sum_group",
        out_shape=jax.ShapeDtypeStruct((A, B), F32),
        grid_spec=pltpu.PrefetchScalarGridSpec(
            num_scalar_prefetch=1, grid=(A // tr,),
            in_specs=[pl.BlockSpec((None, tr, B), lambda i, k: (k[0], i, 0)),
                      pl.BlockSpec((3, tr, B), lambda i, k: (0, i, 0))],
            out_specs=pl.BlockSpec((tr, B), lambda i, k: (i, 0))),
        compiler_params=_cp(("arbitrary",)),
    )(chip, t, u)


def _swap_rows(g):
    _, A, B = g.shape
    nh = A // 2

    def body(g_ref, r_ref, send_sems, recv_sems):
        x, y, c = _place()
        cp = _remote(g_ref.at[:, pl.ds((1 - c) * nh, nh)], r_ref, send_sems, recv_sems, 0, (x, y, 1 - c))
        cp.start()
        cp.wait()

    return pl.pallas_call(
        body, name="swap_rows",
        out_shape=jax.ShapeDtypeStruct((4, nh, B), g.dtype),
        in_specs=[ANY], out_specs=ANY,
        scratch_shapes=[pltpu.SemaphoreType.DMA((1,)), pltpu.SemaphoreType.DMA((1,))],
        compiler_params=pltpu.CompilerParams(has_side_effects=True),
    )(g)


def _add_rows(cidx, g, r):
    _, nh, B = r.shape
    tr = min(256, nh)
    nb = nh // tr

    def body(c_ref, g_ref, r_ref, o_ref):
        o_ref[...] = (g_ref[...].astype(F32) + r_ref[...].astype(F32)).astype(BF16)

    blk = (None, tr, B)
    return pl.pallas_call(
        body, name="add_rows",
        out_shape=jax.ShapeDtypeStruct(r.shape, BF16),
        grid_spec=pltpu.PrefetchScalarGridSpec(
            num_scalar_prefetch=1, grid=(4, nb),
            in_specs=[pl.BlockSpec(blk, lambda k, i, c: (k, c[0] * nb + i, 0)),
                      pl.BlockSpec(blk, lambda k, i, c: (k, i, 0))],
            out_specs=pl.BlockSpec(blk, lambda k, i, c: (k, i, 0))),
        compiler_params=_cp(("arbitrary", "arbitrary")),
    )(cidx, g, r)


def _sum_group_half(chip, cidx, t, u):
    _, nh, B = t.shape
    tr = min(256, nh)
    nb = nh // tr

    def body(k_ref, c_ref, t_ref, u_ref, o_ref):
        mine = (pl.program_id(0) // nb) == c_ref[0]

        @pl.when(mine)
        def _():
            o_ref[...] = ((t_ref[...].astype(F32) + u_ref[0].astype(F32)) + u_ref[1].astype(F32)) + u_ref[2].astype(F32)

        @pl.when(jnp.logical_not(mine))
        def _():
            o_ref[...] = jnp.zeros_like(o_ref)

    own = lambda i, c: jnp.clip(i - c[0] * nb, 0, nb - 1)
    return pl.pallas_call(
        body, name="sum_group_half",
        out_shape=jax.ShapeDtypeStruct((2 * nh, B), F32),
        grid_spec=pltpu.PrefetchScalarGridSpec(
            num_scalar_prefetch=2, grid=(2 * nb,),
            in_specs=[pl.BlockSpec((None, tr, B), lambda i, k, c: (k[0], own(i, c), 0)),
                      pl.BlockSpec((3, tr, B), lambda i, k, c: (0, own(i, c), 0))],
            out_specs=pl.BlockSpec((tr, B), lambda i, k, c: (i, 0))),
        compiler_params=_cp(("arbitrary",)),
    )(chip, cidx, t, u)


def _adam_math(w, g, m, v):
    c1 = 1.0 / (1.0 - ADAM_B1 ** ADAM_STEP)
    c2 = 1.0 / (1.0 - ADAM_B2 ** ADAM_STEP)
    nm = ADAM_B1 * m + (1.0 - ADAM_B1) * g
    nv = ADAM_B2 * v + (1.0 - ADAM_B2) * (g * g)
    return -ADAM_LR * ((nm * c1) / (jnp.sqrt(nv * c2) + ADAM_EPS) + ADAM_WD * w), nm, nv


def _adamw_layer(prev, w, m, v, sa, sb, l):
    NL, A, B = w.shape
    tr = A
    while tr * B * 4 > ADAM_BLOCK_BYTES and tr % 16 == 0:
        tr //= 2

    def body(p0, p1, p2, p3, w_ref, m_ref, v_ref, sa_ref, sb_ref, g_ref, d_ref, nm_ref, nv_ref):
        g = sa_ref[...] + sb_ref[...]
        g_ref[...] = g
        d_ref[...], nm_ref[...], nv_ref[...] = _adam_math(w_ref[...], g, m_ref[...], v_ref[...])

    lay = pl.BlockSpec((None, tr, B), lambda i: (l, i, 0))
    src = pl.BlockSpec((tr, B), lambda i: (i, 0))
    full = jax.ShapeDtypeStruct((NL, A, B), F32)
    if prev is None:
        prev = tuple(lax.empty((NL, A, B), F32) for _ in range(4))
    return tuple(pl.pallas_call(
        body, name="adamw_layer",
        out_shape=(full,) * 4, grid=(A // tr,),
        in_specs=[ANY] * 4 + [lay, lay, lay, src, src], out_specs=(lay,) * 4,
        input_output_aliases={0: 0, 1: 1, 2: 2, 3: 3},
        compiler_params=_cp(("arbitrary",)),
    )(*prev, w, m, v, sa, sb))


def _adamw_square(prev, wmv, sa, sb, l):
    NL, R, D = wmv[0][0].shape
    n = len(wmv)

    def body(*refs):
        ins, outs = refs[4 * n:4 * n + 3 * n + 2], refs[4 * n + 3 * n + 2:]
        sa_ref, sb_ref = ins[3 * n], ins[3 * n + 1]
        g = sa_ref[...] + sb_ref[...]
        for p in range(n):
            @pl.when(pl.program_id(0) == p)
            def _(p=p):
                w_ref, m_ref, v_ref = ins[3 * p:3 * p + 3]
                g_ref, d_ref, nm_ref, nv_ref = outs[4 * p:4 * p + 4]
                g_ref[...] = g
                d_ref[...], nm_ref[...], nv_ref[...] = _adam_math(w_ref[...], g, m_ref[...], v_ref[...])

    lay = pl.BlockSpec((None, R, D), lambda i: (l, 0, 0))
    src = pl.BlockSpec((R, D), lambda i: (i, 0))
    full = jax.ShapeDtypeStruct((NL, R, D), F32)
    if prev is None:
        prev = tuple(lax.empty((NL, R, D), F32) for _ in range(4 * n))
    flat = [a for t in wmv for a in t]
    outs = pl.pallas_call(
        body, name="adamw_square",
        out_shape=(full,) * (4 * n), grid=(n,),
        in_specs=[ANY] * (4 * n) + [lay] * (3 * n) + [src, src], out_specs=(lay,) * (4 * n),
        input_output_aliases={i: i for i in range(4 * n)},
        compiler_params=_cp(("arbitrary",)),
    )(*prev, *flat, sa, sb)
    return tuple(outs)


def _adamw(w, g, m, v):
    shape = w.shape
    cols = shape[-1]
    rows = int(np.prod(shape[:-1]))

    def body(w_ref, g_ref, m_ref, v_ref, d_ref, nm_ref, nv_ref):
        d_ref[...], nm_ref[...], nv_ref[...] = _adam_math(w_ref[...], g_ref[...], m_ref[...], v_ref[...])

    tile = pl.BlockSpec((rows, cols), lambda i: (0, 0))
    out = jax.ShapeDtypeStruct((rows, cols), F32)
    res = pl.pallas_call(
        body, name="adamw",
        out_shape=(out, out, out), grid=(1,),
        in_specs=[tile] * 4, out_specs=(tile,) * 3,
        compiler_params=_cp(("arbitrary",)),
    )(*[a.reshape(rows, cols) for a in (w, g, m, v)])
    return tuple(a.reshape(shape) for a in res)


def _tail_exchange(small, s_in, s_sq):
    def body(s_ref, a_ref, b_ref, o_ref, oa_ref, ob_ref, send_sems, recv_sems, local_sem):
        x, y, c = _place()
        me = 4 * x + 2 * y + c
        sibling = (x, y, 1 - c)
        mine = pltpu.make_async_copy(s_ref, o_ref.at[me], local_sem)
        mine.start()
        swaps = [_remote(a_ref, oa_ref, send_sems, recv_sems, 7, sibling), _remote(b_ref, ob_ref, send_sems, recv_sems, 8, sibling)]
        sends = [_remote(s_ref, o_ref.at[me], send_sems, recv_sems, r, peer) for r, peer in enumerate(_peers(x, y, c))]
        for cp in swaps + sends:
            cp.start()
        for r, peer in enumerate(_peers(x, y, c)):
            theirs = o_ref.at[4 * peer[0] + 2 * peer[1] + peer[2]]
            _remote(theirs, theirs, send_sems, recv_sems, r, peer).wait_recv()
        for cp in sends:
            cp.wait_send()
        for cp in swaps:
            cp.wait()
        mine.wait()

    return pl.pallas_call(
        body, name="tail_exchange",
        out_shape=(jax.ShapeDtypeStruct((8,) + small.shape, small.dtype),
                   jax.ShapeDtypeStruct(s_in.shape, s_in.dtype), jax.ShapeDtypeStruct(s_sq.shape, s_sq.dtype)),
        in_specs=[ANY] * 3, out_specs=(ANY,) * 3,
        scratch_shapes=[pltpu.SemaphoreType.DMA((9,)), pltpu.SemaphoreType.DMA((9,)), pltpu.SemaphoreType.DMA],
        compiler_params=pltpu.CompilerParams(has_side_effects=True),
    )(small, s_in, s_sq)


def _sum_devices(gs):
    NL = len(gs)
    _, R, D = gs[0].shape

    def body(*refs):
        o_ref = refs[NL]
        for l in range(NL):
            acc = refs[l][0]
            for k in range(1, 8):
                acc = acc + refs[l][k]
            o_ref[l] = acc

    return pl.pallas_call(
        body, name="sum_devices",
        out_shape=jax.ShapeDtypeStruct((NL, R, D), F32),
        grid=(1,),
        in_specs=[pl.BlockSpec((8, R, D), lambda i: (0, 0, 0))] * NL,
        out_specs=pl.BlockSpec((NL, R, D), lambda i: (0, 0, 0)),
        compiler_params=_cp(("arbitrary",)),
    )(*gs)


def kernel(x, pre_norm_g, w_in, w_ret_out, conv_w, conv_b, conv_ln_g, conv_ln_b, w_conv_out, w_o, post_norm_g, loss_target, m_pre_norm_g, m_w_in, m_w_ret_out, m_conv_w, m_conv_b, m_conv_ln_g, m_conv_ln_b, m_w_conv_out, m_w_o, m_post_norm_g, v_pre_norm_g, v_w_in, v_w_ret_out, v_conv_w, v_conv_b, v_conv_ln_g, v_conv_ln_b, v_w_conv_out, v_w_o, v_post_norm_g):
    NL, D, W = w_in.shape
    Cc = conv_w.shape[-1]
    T = x.shape[1]
    L = min(RET_BLOCK, T)
    tb = _tables(T, L)
    ax, ay, ac = _place()
    chip = (2 * ax + ay).astype(jnp.int32).reshape(1)
    cidx = ac.astype(jnp.int32).reshape(1)
    pre_g, cb, lg, lb, post_g = (a.reshape(NL, 1, D) for a in (pre_norm_g, conv_b, conv_ln_g, conv_ln_b, post_norm_g))

    win = [_cast_win(chip, w_in, l) for l in range(NL)]
    wsq = [_cast_wsq(chip, w_ret_out, w_conv_out, w_o, l) for l in range(NL)]
    win[0], wcw = _gather_first(win[0], _place_cw(chip, conv_w))

    saved = []
    xl = x[0]
    for l in range(NL):
        more = l + 1 < NL
        ride = _Ride()
        if more:
            _ride_gather_ici(ride, "win", win[l + 1], (0, 5, 8))
        (proj, h), got = _fwd_in(xl, pre_g, win[l], l, ride=ride)
        if more:
            win[l + 1] = got["win"]
        ride = _Ride()
        if l == 0:
            _ride_gather_ici(ride, "wsq0", wsq[0])
        (a_ret, states), got = _ret_fwd(proj, tb, L, ride=ride)
        if l == 0:
            wsq[0] = got["wsq0"]
        ride = _Ride()
        if more:
            _ride_gather_ici(ride, "win", win[l + 1], (5, 8, 8))
            _ride_gather_ici(ride, "wsq", wsq[l + 1])
        if l == 0:
            _ride_gather_pass(ride, "wsq0", wsq[0])
        (a_conv, y), got = _conv_fwd(proj, wcw, cb, lg, lb, l, ride=ride)
        if more:
            win[l + 1], wsq[l + 1] = got["win"], got["wsq"]
        if l == 0:
            wsq[0] = got["wsq0"]
        ride = _Ride()
        if more:
            _ride_gather_pass(ride, "win", win[l + 1])
            _ride_gather_pass(ride, "wsq", wsq[l + 1])
        (xn, ro, co, ym, z), got = _merge_fwd(xl, proj, a_ret, a_conv, wsq[l], post_g, l, ride=ride)
        if more:
            win[l + 1], wsq[l + 1] = got["win"], got["wsq"]
        saved.append((xl, proj, h, a_ret, states, a_conv, y, ro, co, ym, z))
        xl = xn
    dx, lsum = _loss_fwd_bwd(xl, loss_target[0])

    gin, gsq, uin, usq = [None] * NL, [None] * NL, [None] * NL, [None] * NL
    s_in, s_sq, o_in, o_sq = [None] * NL, [None] * NL, [None] * NL, [None] * NL
    small, gs = [None] * NL, [None] * NL
    for l in reversed(range(NL)):
        xin, proj, h, a_ret, states, a_conv, y, ro, co, ym, z = saved[l]
        (dproj, da_ret, da_conv, gsq[l], dpost), _ = _merge_bwd(dx, proj, a_ret, a_conv, ro, co, ym, z, wsq[l], post_g, l)
        ride = _Ride()
        if l + 1 < NL:
            _ride_exchange(ride, "gin", gin[l + 1], "uin", uin[l + 1],
                           [(2, WHOLE), (0, SECOND_HALF), (1, SECOND_HALF)])
        (dproj, sg), got = _conv_bwd(dproj, da_conv, y, proj, wcw, lg, lb, l, ride=ride)
        if l + 1 < NL:
            uin[l + 1] = got["uin"]
        ride = _Ride()
        _ride_exchange(ride, "gsq", gsq[l], "usq", None, [(0, WHOLE), (1, WHOLE), (2, WHOLE)])
        if l + 1 < NL:
            _ride_gather_all(ride, "small", small[l + 1], "gs")
        (dproj,), got = _ret_bwd(dproj, da_ret, proj, states, tb, L, ride=ride)
        usq[l] = got["usq"]
        if l + 1 < NL:
            gs[l + 1] = got["gs"]
        (gin[l],), _ = _win_grad(h, dproj, W)
        ride = _Ride()
        if l > 0:
            _ride_exchange(ride, "gin", gin[l], "uin", None, [(0, FIRST_HALF), (1, FIRST_HALF)])
        else:
            gin[0] = _add_rows(cidx, gin[0], _swap_rows(gin[0]))
            _ride_exchange(ride, "gin", gin[0], "uin", None, [(0, WHOLE), (1, WHOLE), (2, WHOLE)])
        if l + 1 < NL:
            s_in[l + 1] = _sum_group(chip, gin[l + 1], uin[l + 1])
            s_sq[l + 1] = _sum_group(chip, gsq[l + 1], usq[l + 1])
            _ride_swap(ride, "s_in", s_in[l + 1], "o_in")
            _ride_swap(ride, "s_sq", s_sq[l + 1], "o_sq")
        (dx, dpre), got = _in_bwd(dx, dproj, xin, pre_g, win[l], l, ride=ride)
        uin[l] = got["uin"]
        if l + 1 < NL:
            o_in[l + 1], o_sq[l + 1] = got["o_in"], got["o_sq"]
        small[l] = jnp.concatenate([sg, dpre, dpost, lsum if l == NL - 1 else jnp.zeros_like(lsum)], axis=0)
    grad_x = dx

    sq_names = ("w_ret_out", "w_conv_out", "w_o")
    sq_wmv = [(w_ret_out, m_w_ret_out, v_w_ret_out), (w_conv_out, m_w_conv_out, v_w_conv_out), (w_o, m_w_o, v_w_o)]
    s_in[0] = _sum_group_half(chip, cidx, gin[0], uin[0])
    s_sq[0] = _sum_group(chip, gsq[0], usq[0])
    gs[0], o_in[0], o_sq[0] = _tail_exchange(small[0], s_in[0], s_sq[0])
    big_in, square = None, None
    for l in reversed(range(NL)):
        big_in = _adamw_layer(big_in, w_in, m_w_in, v_w_in, s_in[l], o_in[l], l)
        square = _adamw_square(square, sq_wmv, s_sq[l], o_sq[l], l)
    big = {"w_in": big_in}
    for p, n in enumerate(sq_names):
        big[n] = square[4 * p:4 * p + 4]

    gsm = _sum_devices(gs)
    loss = jnp.sum(gsm[NL - 1, ROW_LOSS:ROW_LOSS + 8])
    grads = {
        "pre_norm_g": gsm[:, ROW_PRE], "conv_w": lax.dynamic_slice_in_dim(gsm[:, 0:CONV_K], chip[0] * Cc, Cc, axis=2),
        "conv_b": gsm[:, ROW_CB], "conv_ln_g": gsm[:, ROW_LG], "conv_ln_b": gsm[:, ROW_LB], "post_norm_g": gsm[:, ROW_POST],
    }
    weights = dict(pre_norm_g=pre_norm_g, conv_w=conv_w, conv_b=conv_b, conv_ln_g=conv_ln_g, conv_ln_b=conv_ln_b,
                   post_norm_g=post_norm_g)
    m1 = dict(pre_norm_g=m_pre_norm_g, conv_w=m_conv_w, conv_b=m_conv_b, conv_ln_g=m_conv_ln_g, conv_ln_b=m_conv_ln_b,
              post_norm_g=m_post_norm_g)
    m2 = dict(pre_norm_g=v_pre_norm_g, conv_w=v_conv_w, conv_b=v_conv_b, conv_ln_g=v_conv_ln_g, conv_ln_b=v_conv_ln_b,
              post_norm_g=v_post_norm_g)
    res = {n: (grads[n],) + _adamw(weights[n], grads[n], m1[n], m2[n]) for n in grads}
    res.update(big)
    order = ["pre_norm_g", "w_in", "w_ret_out", "conv_w", "conv_b", "conv_ln_g", "conv_ln_b", "w_conv_out", "w_o", "post_norm_g"]
    return (loss, grad_x[None], *[res[n][0] for n in order], *[res[n][1] for n in order],
            *[res[n][2] for n in order], *[res[n][3] for n in order])
```

```python
import numpy as np
import jax
import jax.numpy as jnp
from jax import lax
from jax.experimental import pallas as pl
from jax.experimental.pallas import tpu as pltpu

F32 = jnp.float32
BF16 = jnp.bfloat16

HEADS = 8
DK = 64
DV = 128
CONV_K = 31
CHUNK = 64
ROPE_BASE = 10000.0
EPS = 1e-6
HALO = 32
CONV_RB = 64
CONV_LANES = 512
CONV_UNROLL = 1
CONV_TILE = 256
RET_BLOCK = 512

ADAM_LR = 0.001
ADAM_B1 = 0.9
ADAM_B2 = 0.999
ADAM_EPS = 1e-08
ADAM_WD = 0.01
ADAM_STEP = 10
ADAM_BLOCK_BYTES = 2 * 1024 * 1024

VMEM_LIMIT = 56 * 1024 * 1024
MESH_T = pl.DeviceIdType.MESH
ANY = pl.BlockSpec(memory_space=pl.ANY)

ROW_CB, ROW_LG, ROW_LB = 32, 33, 34
ROW_PRE, ROW_POST, ROW_LOSS = 40, 48, 56


def _cp(sem=None, **kw):
    return pltpu.CompilerParams(dimension_semantics=sem, vmem_limit_bytes=VMEM_LIMIT, **kw)


def _dot(a, b):
    return jnp.dot(a, b, preferred_element_type=F32)


def _dot_nt(a, b):
    return lax.dot_general(a, b, (((1,), (1,)), ((), ())), preferred_element_type=F32)


def _dot_tn(a, b):
    return lax.dot_general(a, b, (((0,), (0,)), ((), ())), preferred_element_type=F32)


def _sigmoid(x):
    return jax.nn.sigmoid(x)


def _silu(x):
    return x * _sigmoid(x)


def _rms(x, g):
    return x * lax.rsqrt(jnp.mean(x * x, axis=-1, keepdims=True) + EPS) * g


def _gn_gate(o, g):
    mu = jnp.mean(o, axis=-1, keepdims=True)
    d = o - mu
    var = jnp.mean(d * d, axis=-1, keepdims=True)
    return d * lax.rsqrt(var + EPS) * _silu(g)


def _ln_gate(y, gc, lg, lb):
    mu = jnp.mean(y, axis=-1, keepdims=True)
    d = y - mu
    var = jnp.mean(d * d, axis=-1, keepdims=True)
    return _silu(d * lax.rsqrt(var + EPS) * lg + lb) * _silu(gc)


def _tables(T, L):
    lane = np.arange(128)
    d = lane % DK
    half = DK // 2
    inv = (ROPE_BASE ** (-(np.arange(half, dtype=np.float32)) / half)).astype(np.float32)
    ang = (np.arange(T, dtype=np.float32)[:, None] * inv[None, :]).astype(np.float64)
    angl = ang[:, d % half]
    cos = np.cos(angl)
    sin = np.sin(angl)
    lo = (d < half)[None, :]
    rope = np.stack([cos, np.where(lo, -sin, 0.0), np.where(lo, 0.0, sin)]).astype(np.float32)

    hh = np.arange(HEADS, dtype=np.float64)
    log_g = np.log1p(-np.exp2(-5.0 - hh))
    n = np.arange(L, dtype=np.float64)
    cn = np.arange(L) // CHUNK
    allowed = (cn[None, :] <= cn[:, None])
    dist = np.abs(n[:, None] - n[None, :])
    mask = np.exp(log_g[:, None, None] * dist[None]) * allowed[None]
    mq = ((lane[None, :] // DK) == (np.arange(HEADS)[:, None] % 2)).astype(np.float64)
    qd = np.exp(log_g[:, None] * n[None, :])
    kd = np.exp(log_g[:, None] * (L - n[None, :]))
    qdm = qd[:, :, None] * mq[:, None, :] * (DK ** -0.5)
    kdm = kd[:, :, None] * mq[:, None, :]
    mqs = np.broadcast_to((mq * (DK ** -0.5))[:, None, :], (HEADS, 8, 128))
    cd = np.broadcast_to(np.exp(log_g * L)[:, None, None], (HEADS, 8, 128))
    f = lambda a: jnp.asarray(np.ascontiguousarray(a), dtype=F32)
    return dict(rope=f(rope), mask=f(mask), qdm=f(qdm), kdm=f(kdm), mqs=f(mqs), cd=f(cd))


def _rot(b, c, sl, sh):
    return b * c + pltpu.roll(b, 96, axis=1) * sl + pltpu.roll(b, 32, axis=1) * sh


def _rot_t(d, c, sl, sh):
    return d * c + pltpu.roll(d * sl, 32, axis=1) + pltpu.roll(d * sh, 96, axis=1)


def _place():
    return lax.axis_index("x"), lax.axis_index("y"), lax.axis_index("c")


def _other_chips(x, y):
    return [(1 - x, y), (x, 1 - y), (1 - x, 1 - y)]


def _remote(src, dst, send_sems, recv_sems, k, to):
    return pltpu.make_async_remote_copy(src_ref=src, dst_ref=dst, send_sem=send_sems.at[k], recv_sem=recv_sems.at[k],
                                        device_id=to, device_id_type=MESH_T)


class _Ride:
    def __init__(self):
        self.arrays, self.kinds, self.names = [], [], []
        self.fresh = []
        self.ops = []

    def read(self, name, a):
        self.names.append(name)
        self.arrays.append(a)
        self.kinds.append("in")

    def inout(self, name, a):
        self.names.append(name)
        self.arrays.append(a)
        self.kinds.append("inout")

    def land(self, name, shape, dtype):
        self.fresh.append((name, jax.ShapeDtypeStruct(shape, dtype)))

    def op(self, n_sems, start, finish):
        self.ops.append((n_sems, start, finish))


def _pcall(body, *, name, grid, in_specs, out_specs, out_shape, args, scratch_shapes=(), sem, aliases=None, ride=None):
    if ride is None or not ride.ops:
        outs = pl.pallas_call(body, name=name, grid=grid, in_specs=list(in_specs), out_specs=tuple(out_specs),
                              out_shape=tuple(out_shape), scratch_shapes=list(scratch_shapes),
                              input_output_aliases=dict(aliases or {}), compiler_params=_cp(sem))(*args)
        return outs, {}

    ni, no, nr = len(args), len(out_shape), len(ride.arrays)
    inout = [i for i, k in enumerate(ride.kinds) if k == "inout"]
    r_out_shapes = [jax.ShapeDtypeStruct(ride.arrays[i].shape, ride.arrays[i].dtype) for i in inout] + [s for _, s in ride.fresh]
    r_out_names = [ride.names[i] for i in inout] + [n for n, _ in ride.fresh]
    nro = len(r_out_shapes)
    n_sems = sum(n for n, _, _ in ride.ops)
    n_scr = len(scratch_shapes)
    nd = len(grid)

    def wrapped(*refs):
        ins, rin = refs[:ni], refs[ni:ni + nr]
        outs, rout = refs[ni + nr:ni + nr + no], refs[ni + nr + no:ni + nr + no + nro]
        scr = refs[ni + nr + no + nro:ni + nr + no + nro + n_scr]
        send_sems, recv_sems = refs[-2], refs[-1]
        view = {nm: r for nm, r, k in zip(ride.names, rin, ride.kinds) if k == "in"}
        view.update(dict(zip(r_out_names, rout)))
        first = pl.program_id(0) == 0
        last = pl.program_id(0) == grid[0] - 1
        for d in range(1, nd):
            first = first & (pl.program_id(d) == 0)
            last = last & (pl.program_id(d) == grid[d] - 1)

        @pl.when(first)
        def _():
            base = 0
            for n, start, _ in ride.ops:
                start(view, send_sems, recv_sems, base)
                base += n

        body(*ins, *outs, *scr)

        @pl.when(last)
        def _():
            base = 0
            for n, _, finish in ride.ops:
                finish(view, send_sems, recv_sems, base)
                base += n

    res = pl.pallas_call(
        wrapped, name=name, grid=grid,
        in_specs=list(in_specs) + [ANY] * nr, out_specs=tuple(out_specs) + (ANY,) * nro,
        out_shape=tuple(out_shape) + tuple(r_out_shapes),
        scratch_shapes=list(scratch_shapes) + [pltpu.SemaphoreType.DMA((n_sems,)), pltpu.SemaphoreType.DMA((n_sems,))],
        input_output_aliases={**dict(aliases or {}), **{ni + i: no + j for j, i in enumerate(inout)}},
        compiler_params=_cp(sem),
    )(*args, *ride.arrays)
    return res[:no], dict(zip(r_out_names, res[no:]))


def _half(ref, chip_idx, cc, part=(0, 1, 1)):
    n = ref.shape[1] // 2
    lo, hi, k = part
    return ref.at[chip_idx, pl.ds(cc * n + lo * n // k, (hi - lo) * n // k)]


def _ride_gather_ici(ride, name, a, part=(0, 1, 1)):
    ride.inout(name, a)

    def start(view, ss, rs, b):
        x, y, c = _place()
        mine = _half(view[name], 2 * x + y, c, part)
        for j, (cx, cy) in enumerate(_other_chips(x, y)):
            _remote(mine, mine, ss, rs, b + j, (cx, cy, c)).start()

    def finish(view, ss, rs, b):
        x, y, c = _place()
        mine = _half(view[name], 2 * x + y, c, part)
        for j, (cx, cy) in enumerate(_other_chips(x, y)):
            theirs = _half(view[name], 2 * cx + cy, c, part)
            _remote(theirs, theirs, ss, rs, b + j, (cx, cy, c)).wait_recv()
        for j, (cx, cy) in enumerate(_other_chips(x, y)):
            _remote(mine, mine, ss, rs, b + j, (cx, cy, c)).wait_send()

    ride.op(3, start, finish)


def _ride_gather_pass(ride, name, a):
    ride.inout(name, a)

    def start(view, ss, rs, b):
        x, y, c = _place()
        for j, (cx, cy) in enumerate(_other_chips(x, y)):
            blk = _half(view[name], 2 * cx + cy, c)
            _remote(blk, blk, ss, rs, b + j, (x, y, 1 - c)).start()

    def finish(view, ss, rs, b):
        x, y, c = _place()
        for j, (cx, cy) in enumerate(_other_chips(x, y)):
            theirs = _half(view[name], 2 * cx + cy, 1 - c)
            _remote(theirs, theirs, ss, rs, b + j, (x, y, 1 - c)).wait_recv()
        for j, (cx, cy) in enumerate(_other_chips(x, y)):
            blk = _half(view[name], 2 * cx + cy, c)
            _remote(blk, blk, ss, rs, b + j, (x, y, 1 - c)).wait_send()

    ride.op(3, start, finish)


def _ride_exchange(ride, src_name, src, dst_name, dst, plan):
    ride.read(src_name, src)
    if dst is None:
        ride.land(dst_name, (3,) + src.shape[1:], src.dtype)
    else:
        ride.inout(dst_name, dst)
    A = src.shape[1]

    def copy(view, ss, rs, sem, j, part, x, y, c):
        lo, hi, k = part
        rows = pl.ds(lo * A // k, (hi - lo) * A // k)
        cx, cy = _other_chips(x, y)[j]
        return _remote(view[src_name].at[2 * cx + cy, rows], view[dst_name].at[j, rows], ss, rs, sem, (cx, cy, c))

    def start(view, ss, rs, b):
        x, y, c = _place()
        for i, (j, part) in enumerate(plan):
            copy(view, ss, rs, b + i, j, part, x, y, c).start()

    def finish(view, ss, rs, b):
        x, y, c = _place()
        for i, (j, part) in enumerate(plan):
            copy(view, ss, rs, b + i, j, part, x, y, c).wait()

    ride.op(len(plan), start, finish)


WHOLE, FIRST_HALF, SECOND_HALF = (0, 1, 1), (0, 1, 2), (1, 2, 2)


def _ride_swap(ride, src_name, src, dst_name):
    ride.read(src_name, src)
    ride.land(dst_name, src.shape, src.dtype)

    def start(view, ss, rs, b):
        x, y, c = _place()
        _remote(view[src_name], view[dst_name], ss, rs, b, (x, y, 1 - c)).start()

    def finish(view, ss, rs, b):
        x, y, c = _place()
        _remote(view[src_name], view[dst_name], ss, rs, b, (x, y, 1 - c)).wait()

    ride.op(1, start, finish)


def _peers(x, y, c):
    flip = lambda v, b: 1 - v if b else v
    return [(flip(x, r & 4), flip(y, r & 2), flip(c, r & 1)) for r in range(1, 8)]


def _ride_gather_all(ride, src_name, src, dst_name):
    ride.read(src_name, src)
    ride.land(dst_name, (8,) + src.shape, src.dtype)

    def start(view, ss, rs, b):
        x, y, c = _place()
        me = 4 * x + 2 * y + c
        pltpu.make_async_copy(view[src_name], view[dst_name].at[me], ss.at[b + 7]).start()
        for r, peer in enumerate(_peers(x, y, c)):
            _remote(view[src_name], view[dst_name].at[me], ss, rs, b + r, peer).start()

    def finish(view, ss, rs, b):
        x, y, c = _place()
        me = 4 * x + 2 * y + c
        for r, peer in enumerate(_peers(x, y, c)):
            theirs = view[dst_name].at[4 * peer[0] + 2 * peer[1] + peer[2]]
            _remote(theirs, theirs, ss, rs, b + r, peer).wait_recv()
        for r, peer in enumerate(_peers(x, y, c)):
            _remote(view[src_name], view[dst_name].at[me], ss, rs, b + r, peer).wait_send()
        pltpu.make_async_copy(view[src_name], view[dst_name].at[me], ss.at[b + 7]).wait()

    ride.op(8, start, finish)


def _cast_win(chip, w_in, l):
    _, D, W = w_in.shape
    tr = min(256, D)

    def body(chip_ref, w_ref, o_ref):
        o_ref[...] = w_ref[...].astype(BF16)

    return pl.pallas_call(
        body, name="cast_win",
        out_shape=jax.ShapeDtypeStruct((4, D, W), BF16),
        grid_spec=pltpu.PrefetchScalarGridSpec(
            num_scalar_prefetch=1, grid=(D // tr,),
            in_specs=[pl.BlockSpec((None, tr, W), lambda r, c: (l, r, 0))],
            out_specs=pl.BlockSpec((None, tr, W), lambda r, c: (c[0], r, 0))),
        compiler_params=_cp(("arbitrary",)),
    )(chip, w_in)


def _cast_wsq(chip, w_ro, w_co, w_o, l):
    _, R, D = w_ro.shape

    def body(chip_ref, a_ref, b_ref, c_ref, o_ref):
        o_ref[0:R, :] = a_ref[...].astype(BF16)
        o_ref[R:2 * R, :] = b_ref[...].astype(BF16)
        o_ref[2 * R:3 * R, :] = c_ref[...].astype(BF16)

    spec = pl.BlockSpec((None, R, D), lambda i, c: (l, 0, 0))
    return pl.pallas_call(
        body, name="cast_wsq",
        out_shape=jax.ShapeDtypeStruct((4, 3 * R, D), BF16),
        grid_spec=pltpu.PrefetchScalarGridSpec(
            num_scalar_prefetch=1, grid=(1,),
            in_specs=[spec, spec, spec],
            out_specs=pl.BlockSpec((None, 3 * R, D), lambda i, c: (c[0], 0, 0))),
        compiler_params=_cp(("arbitrary",)),
    )(chip, w_ro, w_co, w_o)


def _place_cw(chip, conv_w):
    NL, K, Cc = conv_w.shape

    def body(chip_ref, w_ref, o_ref):
        o_ref[...] = w_ref[...]

    return pl.pallas_call(
        body, name="place_cw",
        out_shape=jax.ShapeDtypeStruct((4, NL, K, Cc), F32),
        grid_spec=pltpu.PrefetchScalarGridSpec(
            num_scalar_prefetch=1, grid=(1,),
            in_specs=[pl.BlockSpec((NL, K, Cc), lambda i, c: (0, 0, 0))],
            out_specs=pl.BlockSpec((None, NL, K, Cc), lambda i, c: (c[0], 0, 0, 0))),
        compiler_params=_cp(("arbitrary",)),
    )(chip, conv_w)


def _gather_first(win0, wcw):
    n_arr = 2

    def body(a0, a1, o0, o1, send_sems, recv_sems):
        x, y, c = _place()
        sibling = (x, y, 1 - c)
        chips = _other_chips(x, y)
        outs = (o0, o1)

        def copy(k, a, cx, cy, cc, to):
            blk = _half(outs[a], 2 * cx + cy, cc)
            return _remote(blk, blk, send_sems, recv_sems, k, to)

        first = [copy(3 * a + j, a, x, y, c, (*chip, c)) for a in range(n_arr) for j, chip in enumerate(chips)]
        for cp in first:
            cp.start()
        passed = [copy(3 * n_arr + 3 * a + j, a, *chip, c, sibling) for a in range(n_arr) for j, chip in enumerate(chips)]
        for a in range(n_arr):
            for j, chip in enumerate(chips):
                copy(3 * a + j, a, *chip, c, sibling).wait_recv()
                passed[3 * a + j].start()
        for a in range(n_arr):
            for j, chip in enumerate(chips):
                copy(3 * n_arr + 3 * a + j, a, *chip, 1 - c, sibling).wait_recv()
        for cp in first + passed:
            cp.wait_send()

    ins = (win0, wcw)
    return pl.pallas_call(
        body, name="gather_first",
        out_shape=tuple(jax.ShapeDtypeStruct(a.shape, a.dtype) for a in ins),
        in_specs=[ANY] * n_arr, out_specs=(ANY,) * n_arr,
        scratch_shapes=[pltpu.SemaphoreType.DMA((6 * n_arr,)), pltpu.SemaphoreType.DMA((6 * n_arr,))],
        input_output_aliases={0: 0, 1: 1},
        compiler_params=pltpu.CompilerParams(has_side_effects=True),
    )(*ins)


def _fwd_in(x, pre_g, win, l, ride=None):
    T, D = x.shape
    W = win.shape[-1]
    tm = min(512, T)

    def body(x_ref, g_ref, w_ref, p_ref, h_ref):
        hb = _rms(x_ref[...], g_ref[...]).astype(BF16)
        h_ref[...] = hb
        for j in range(4):
            p_ref[:, j * W:(j + 1) * W] = _dot(hb, w_ref[j]).astype(BF16)

    return _pcall(
        body, name="fwd_in",
        out_shape=(jax.ShapeDtypeStruct((T, 4 * W), BF16), jax.ShapeDtypeStruct((T, D), BF16)),
        grid=(T // tm,),
        in_specs=[pl.BlockSpec((tm, D), lambda i: (i, 0)),
                  pl.BlockSpec((None, 1, D), lambda i: (l, 0, 0)),
                  pl.BlockSpec((4, D, W), lambda i: (0, 0, 0), pipeline_mode=pl.Buffered(1))],
        out_specs=(pl.BlockSpec((tm, 4 * W), lambda i: (i, 0)),
                   pl.BlockSpec((tm, D), lambda i: (i, 0))),
        sem=("arbitrary",), args=(x, pre_g, win), ride=ride)


def _ret_specs(T, L):
    rope = pl.BlockSpec((3, L, 128), lambda s: (0, s, 0))
    mask = pl.BlockSpec((HEADS, L, L), lambda s: (0, 0, 0), pipeline_mode=pl.Buffered(1))
    qdm = pl.BlockSpec((HEADS, L, 128), lambda s: (0, 0, 0), pipeline_mode=pl.Buffered(1))
    small = pl.BlockSpec((HEADS, 8, 128), lambda s: (0, 0, 0))
    return [rope, mask, qdm, qdm, small, small]


QK = HEADS * DK
VW = HEADS * DV
PW = 2 * QK + 2 * VW


def _zero_at_start(ref):
    @pl.when(pl.program_id(0) == 0)
    def _():
        ref[...] = jnp.zeros_like(ref)


def _ret_fwd_part(p_ref, rope_ref, m_ref, qdm_ref, kdm_ref, mqs_ref, cd_ref, a_ref, st_ref, state):
    c, sl, sh = rope_ref[0], rope_ref[1], rope_ref[2]
    for j in range(HEADS // 2):
        rq = _rot(p_ref[:, 128 * j:128 * (j + 1)].astype(F32), c, sl, sh)
        rk = _rot(p_ref[:, QK + 128 * j:QK + 128 * (j + 1)].astype(F32), c, sl, sh)
        rkb = rk.astype(BF16)
        for e in range(2):
            h = 2 * j + e
            v = p_ref[:, 2 * QK + DV * h:2 * QK + DV * (h + 1)]
            g = p_ref[:, 2 * QK + VW + DV * h:2 * QK + VW + DV * (h + 1)].astype(F32)
            a = (rq * mqs_ref[h, 0:1, :]).astype(BF16)
            p = (_dot_nt(a, rkb) * m_ref[h]).astype(BF16)
            st = state[h]
            st_ref[h] = st
            o = _dot(p, v) + _dot((rq * qdm_ref[h]).astype(BF16), st.astype(BF16))
            state[h] = st * cd_ref[h, 0:1, :] + _dot_tn((rk * kdm_ref[h]).astype(BF16), v)
            a_ref[:, DV * h:DV * (h + 1)] = _gn_gate(o, g).astype(BF16)


def _shift_copies(src, sh):
    rows = sh.shape[1]
    for b in range(1, 8):
        sh[b - 1, :, :] = src[pl.ds(b, rows), :]


def _window(src, sh, r0, const, rows, lanes):
    b = const % 8
    at = pl.ds(pl.multiple_of(r0 + (const - b), 8), rows)
    w = src[at, lanes] if b == 0 else sh[b - 1, at, lanes]
    return w.reshape(rows // 8, 8, w.shape[-1])


def _conv_taps(wbuf, src, sh, r0, const, rows, lanes):
    groups = rows // 8
    accs = [None] * groups
    for k in range(CONV_K):
        w8 = wbuf[pl.ds(8 * k, 8), lanes]
        win = _window(src, sh, r0, const + k, rows, lanes)
        for g in range(groups):
            term = w8 * win[g]
            accs[g] = term if k == 0 else accs[g] + term
    return jnp.concatenate(accs, axis=0)


def _lane_parts(C):
    return [pl.ds(j * CONV_LANES, CONV_LANES) for j in range(C // CONV_LANES)]


def _load_conv_w(cw_ref, wbuf, flip):
    for k in range(CONV_K):
        row = jnp.concatenate([cw_ref[c, pl.ds(k, 1), :] for c in range(4)], axis=-1)
        kk = CONV_K - 1 - k if flip else k
        wbuf[pl.ds(8 * kk, 8), :] = jnp.broadcast_to(row, (8, row.shape[-1]))


def _conv_fwd_part(p_ref, cw_ref, cb_ref, lg_ref, lb_ref, a_ref, y_ref, ubuf, wbuf, ush):
    tc, C = y_ref.shape
    off = HALO - (CONV_K - 1)
    i = pl.program_id(0)

    @pl.when(i == 0)
    def _():
        ubuf[0:HALO, :] = jnp.zeros((HALO, C), F32)
        _load_conv_w(cw_ref, wbuf, False)

    @pl.when(i > 0)
    def _():
        ubuf[0:HALO, :] = ubuf[tc:tc + HALO, :]

    ga = p_ref[:, 0:C].astype(F32)
    gb = p_ref[:, C:2 * C].astype(F32)
    ubuf[HALO:HALO + tc, :] = ga * _sigmoid(gb)
    _shift_copies(ubuf, ush)

    def rows_block(r, carry):
        r0 = pl.multiple_of(r * CONV_RB, CONV_RB)
        for lanes in _lane_parts(C):
            y_ref[pl.ds(r0, CONV_RB), lanes] = _conv_taps(wbuf, ubuf, ush, r0, off, CONV_RB, lanes) + cb_ref[:, lanes]
        return carry

    lax.fori_loop(0, tc // CONV_RB, rows_block, 0, unroll=CONV_UNROLL)
    gc = p_ref[:, 2 * C:3 * C].astype(F32)
    a_ref[...] = _ln_gate(y_ref[...], gc, lg_ref[...], lb_ref[...]).astype(BF16)


def _ret_fwd(proj, tb, L, ride=None):
    T = proj.shape[0]
    nS = T // L

    def body(p_ref, rope_ref, m_ref, qdm_ref, kdm_ref, mqs_ref, cd_ref, ar_ref, st_ref, state):
        _zero_at_start(state)
        _ret_fwd_part(p_ref, rope_ref, m_ref, qdm_ref, kdm_ref, mqs_ref, cd_ref, ar_ref, st_ref, state)

    return _pcall(
        body, name="ret_fwd",
        out_shape=(jax.ShapeDtypeStruct((T, VW), BF16), jax.ShapeDtypeStruct((nS, HEADS, 128, DV), F32)),
        grid=(nS,),
        in_specs=[pl.BlockSpec((L, PW), lambda s: (s, 0))] + _ret_specs(T, L),
        out_specs=(pl.BlockSpec((L, VW), lambda s: (s, 0)), pl.BlockSpec((None, HEADS, 128, DV), lambda s: (s, 0, 0, 0))),
        scratch_shapes=[pltpu.VMEM((HEADS, 128, DV), F32)],
        sem=("arbitrary",), ride=ride,
        args=(proj, tb["rope"], tb["mask"], tb["qdm"], tb["kdm"], tb["mqs"], tb["cd"]))


def _conv_fwd(proj, wcw, conv_b, ln_g, ln_b, l, ride=None):
    T = proj.shape[0]
    C = conv_b.shape[-1]
    Cc = wcw.shape[-1]
    tc = min(CONV_TILE, T)
    assert PW == 3 * C

    def body(p_ref, cw_ref, cb_ref, lg_ref, lb_ref, ac_ref, y_ref, ubuf, wbuf, ush):
        _conv_fwd_part(p_ref, cw_ref, cb_ref, lg_ref, lb_ref, ac_ref, y_ref, ubuf, wbuf, ush)

    vec = pl.BlockSpec((None, 1, C), lambda i: (l, 0, 0))
    tile = pl.BlockSpec((tc, C), lambda i: (i, 0))
    return _pcall(
        body, name="conv_fwd",
        out_shape=(jax.ShapeDtypeStruct((T, C), BF16), jax.ShapeDtypeStruct((T, C), F32)),
        grid=(T // tc,),
        in_specs=[pl.BlockSpec((tc, 3 * C), lambda i: (i, 1)),
                  pl.BlockSpec((4, None, CONV_K, Cc), lambda i: (0, l, 0, 0)), vec, vec, vec],
        out_specs=(tile, tile),
        scratch_shapes=[pltpu.VMEM((HALO + tc, C), F32), pltpu.VMEM((8 * CONV_K, C), F32),
                        pltpu.VMEM((7, HALO + tc - 8, C), F32)],
        sem=("arbitrary",), ride=ride, args=(proj, wcw, conv_b, ln_g, ln_b))


def _merge_fwd(x, proj, a_ret, a_conv, wsq, post_g, l, ride=None):
    T, D = x.shape
    R = wsq.shape[1] // 3
    tm = min(512, T)

    def body(x_ref, p_ref, ar_ref, ac_ref, wro_ref, wco_ref, wo_ref, g_ref, xn_ref, ro_ref, co_ref, ym_ref, z_ref):
        ro = _dot(ar_ref[...], wro_ref[...].reshape(4 * R, D))
        co = _dot(ac_ref[...], wco_ref[...].reshape(4 * R, D))
        ym = (_sigmoid(p_ref[:, 0:D].astype(F32)) * ro + _sigmoid(p_ref[:, D:2 * D].astype(F32)) * co).astype(BF16)
        z = _dot(ym, wo_ref[...].reshape(4 * R, D))
        ro_ref[...] = ro.astype(BF16)
        co_ref[...] = co.astype(BF16)
        ym_ref[...] = ym
        z_ref[...] = z.astype(BF16)
        xn_ref[...] = x_ref[...] + _rms(z, g_ref[...])

    tile = pl.BlockSpec((tm, D), lambda i: (i, 0))
    wspec = lambda m: pl.BlockSpec((4, R, D), lambda i: (0, m, 0))
    act = jax.ShapeDtypeStruct((T, D), BF16)
    return _pcall(
        body, name="merge_fwd",
        out_shape=(jax.ShapeDtypeStruct((T, D), F32), act, act, act, act),
        grid=(T // tm,),
        in_specs=[tile, pl.BlockSpec((tm, 2 * D), lambda i: (i, 3)), tile, tile,
                  wspec(0), wspec(1), wspec(2), pl.BlockSpec((None, 1, D), lambda i: (l, 0, 0))],
        out_specs=(tile, tile, tile, tile, tile),
        sem=("arbitrary",), ride=ride, args=(x, proj, a_ret, a_conv, wsq, wsq, wsq, post_g))


def _loss_fwd_bwd(y, target):
    T, D = y.shape
    tm = min(512, T)

    def body(y_ref, t_ref, dy_ref, ls_ref):
        @pl.when(pl.program_id(0) == 0)
        def _():
            ls_ref[...] = jnp.zeros_like(ls_ref)

        e = y_ref[...] - t_ref[...]
        dy_ref[...] = e * (1.0 / D)
        ls_ref[...] += jnp.sum((e * e).reshape(tm // 8, 8, D), axis=0) * (0.5 / D)

    tile = pl.BlockSpec((tm, D), lambda i: (i, 0))
    return pl.pallas_call(
        body, name="loss",
        out_shape=(jax.ShapeDtypeStruct((T, D), F32), jax.ShapeDtypeStruct((8, D), F32)),
        grid=(T // tm,),
        in_specs=[tile, tile],
        out_specs=(tile, pl.BlockSpec((8, D), lambda i: (0, 0))),
        compiler_params=_cp(("arbitrary",)),
    )(y, target)


def _merge_bwd(dxn, proj, a_ret, a_conv, ro, co, ym, z, wsq, post_g, l, ride=None):
    T, D = dxn.shape
    R = wsq.shape[1] // 3
    tm = min(512, T)
    n = T // tm

    def body(dx_ref, p_ref, ar_ref, ac_ref, ro_ref, co_ref, ym_ref, z_ref, wro_ref, wco_ref, wo_ref, g_ref,
             dp_ref, dar_ref, dac_ref, gsq_ref, dg_ref, acc, stage):
        i = pl.program_id(0)

        @pl.when(i == 0)
        def _():
            acc[...] = jnp.zeros_like(acc)
            dg_ref[...] = jnp.zeros_like(dg_ref)

        _, vjp = jax.vjp(_rms, z_ref[...].astype(F32), g_ref[...])
        dz, dg = vjp(dx_ref[...])
        dg_ref[0:1, :] += dg
        dzb = dz.astype(BF16)
        dym = _dot_nt(dzb, wo_ref[...].reshape(4 * R, D))
        acc[2] += _dot_tn(ym_ref[...], dzb)
        sr = _sigmoid(p_ref[:, 0:D].astype(F32))
        sc = _sigmoid(p_ref[:, D:2 * D].astype(F32))
        rov = ro_ref[...].astype(F32)
        cov = co_ref[...].astype(F32)
        dp_ref[:, 0:D] = (dym * rov * sr * (1.0 - sr)).astype(BF16)
        dp_ref[:, D:2 * D] = (dym * cov * sc * (1.0 - sc)).astype(BF16)
        dro = (dym * sr).astype(BF16)
        dco = (dym * sc).astype(BF16)
        dar_ref[...] = _dot_nt(dro, wro_ref[...].reshape(4 * R, D)).astype(BF16)
        dac_ref[...] = _dot_nt(dco, wco_ref[...].reshape(4 * R, D)).astype(BF16)
        acc[0] += _dot_tn(ar_ref[...], dro)
        acc[1] += _dot_tn(ac_ref[...], dco)

        @pl.when(i == n - 1)
        def _():
            for m in range(3):
                stage[...] = acc[m].astype(BF16).reshape(4, R, D)
                pltpu.sync_copy(stage, gsq_ref.at[:, pl.ds(m * R, R), :])

    tile = pl.BlockSpec((tm, D), lambda i: (i, 0))
    wspec = lambda m: pl.BlockSpec((4, R, D), lambda i: (0, m, 0), pipeline_mode=pl.Buffered(1))
    return _pcall(
        body, name="merge_bwd",
        out_shape=(jax.ShapeDtypeStruct(proj.shape, BF16), jax.ShapeDtypeStruct((T, D), BF16),
                   jax.ShapeDtypeStruct((T, D), BF16), jax.ShapeDtypeStruct(wsq.shape, BF16),
                   jax.ShapeDtypeStruct((8, D), F32)),
        grid=(n,),
        in_specs=[tile, pl.BlockSpec((tm, 2 * D), lambda i: (i, 3)), tile, tile, tile, tile, tile, tile,
                  wspec(0), wspec(1), wspec(2), pl.BlockSpec((None, 1, D), lambda i: (l, 0, 0))],
        out_specs=(pl.BlockSpec((tm, 2 * D), lambda i: (i, 3)), tile, tile, ANY, pl.BlockSpec((8, D), lambda i: (0, 0))),
        scratch_shapes=[pltpu.VMEM((3, 4 * R, D), F32), pltpu.VMEM((4, R, D), BF16)],
        sem=("arbitrary",), args=(dxn, proj, a_ret, a_conv, ro, co, ym, z, wsq, wsq, wsq, post_g), ride=ride)


def _conv_bwd_part(n, da_ref, y_ref, p_ref, ph_ref, cw_ref, lg_ref, lb_ref, dp_ref,
                   dcbuf, ubuf, dubuf, wbuf, dwacc, vacc, dsh, ush):
    tc, C = y_ref.shape
    off = HALO - (CONV_K - 1)
    nrb = tc // CONV_RB
    t = pl.program_id(0)
    i = n - 1 - t

    @pl.when(t == 0)
    def _():
        dcbuf[tc:tc + HALO, :] = jnp.zeros((HALO, C), F32)
        dwacc[...] = jnp.zeros_like(dwacc)
        vacc[...] = jnp.zeros_like(vacc)
        _load_conv_w(cw_ref, wbuf, True)

    @pl.when(t > 0)
    def _():
        dcbuf[tc:tc + HALO, :] = dcbuf[0:HALO, :]

    gc = p_ref[:, 2 * C:3 * C].astype(F32)
    _, vjp = jax.vjp(_ln_gate, y_ref[...], gc, lg_ref[...], lb_ref[...])
    dy, dgc, dlg, dlb = vjp(da_ref[...].astype(F32))
    dcbuf[0:tc, :] = dy
    dp_ref[:, 2 * C:3 * C] = dgc.astype(BF16)
    vacc[0:1, :] += jnp.sum(dy, axis=0, keepdims=True)
    vacc[1:2, :] += dlg
    vacc[2:3, :] += dlb

    ga = p_ref[:, 0:C].astype(F32)
    sb = _sigmoid(p_ref[:, C:2 * C].astype(F32))
    ubuf[HALO:HALO + tc, :] = ga * sb
    uh = ph_ref[:, 0:C].astype(F32) * _sigmoid(ph_ref[:, C:2 * C].astype(F32))
    ubuf[0:HALO, :] = jnp.where(i > 0, uh, 0.0)

    _shift_copies(dcbuf, dsh)
    _shift_copies(ubuf, ush)
    def du_block(r, carry):
        r0 = pl.multiple_of(r * CONV_RB, CONV_RB)
        for lanes in _lane_parts(C):
            dubuf[pl.ds(r0, CONV_RB), lanes] = _conv_taps(wbuf, dcbuf, dsh, r0, 0, CONV_RB, lanes)
        return carry

    lax.fori_loop(0, nrb, du_block, 0, unroll=CONV_UNROLL)
    du = dubuf[...]
    dp_ref[:, 0:C] = (du * sb).astype(BF16)
    dp_ref[:, C:2 * C] = (du * ga * sb * (1.0 - sb)).astype(BF16)

    def dw_block(r, carry):
        r0 = pl.multiple_of(r * CONV_RB, CONV_RB)
        for lanes in _lane_parts(C):
            dyb = dcbuf[pl.ds(r0, CONV_RB), lanes].reshape(CONV_RB // 8, 8, CONV_LANES)
            for k in range(CONV_K):
                dwacc[8 * k:8 * k + 8, lanes] += jnp.sum(dyb * _window(ubuf, ush, r0, off + k, CONV_RB, lanes), axis=0)
        return carry

    lax.fori_loop(0, nrb, dw_block, 0, unroll=CONV_UNROLL)


def _conv_bwd_final(n, sg_ref, dwacc, vacc):
    C = sg_ref.shape[-1]

    @pl.when(pl.program_id(0) == n - 1)
    def _():
        for k in range(CONV_K):
            sg_ref[pl.ds(k, 1), :] = jnp.sum(dwacc[8 * k:8 * k + 8, :], axis=0, keepdims=True)
        sg_ref[pl.ds(CONV_K, 1), :] = jnp.zeros((1, C), F32)
        sg_ref[ROW_CB:ROW_CB + 8, :] = jnp.zeros((8, C), F32)
        sg_ref[ROW_CB:ROW_CB + 3, :] = vacc[0:3, :]


def _ret_bwd_part(da_ref, p_ref, st_ref, rope_ref, m_ref, qdm_ref, kdm_ref, mqs_ref, cd_ref, dp_ref, gst):
    c, sl, sh = rope_ref[0], rope_ref[1], rope_ref[2]
    for j in range(HEADS // 2):
        rq = _rot(p_ref[:, 128 * j:128 * (j + 1)].astype(F32), c, sl, sh)
        rk = _rot(p_ref[:, QK + 128 * j:QK + 128 * (j + 1)].astype(F32), c, sl, sh)
        rkb = rk.astype(BF16)
        drq = jnp.zeros_like(rq)
        drk = jnp.zeros_like(rk)
        for e in range(2):
            h = 2 * j + e
            v = p_ref[:, 2 * QK + DV * h:2 * QK + DV * (h + 1)]
            g = p_ref[:, 2 * QK + VW + DV * h:2 * QK + VW + DV * (h + 1)].astype(F32)
            mqs = mqs_ref[h, 0:1, :]
            a = (rq * mqs).astype(BF16)
            aq = (rq * qdm_ref[h]).astype(BF16)
            kdv = (rk * kdm_ref[h]).astype(BF16)
            mk = m_ref[h]
            p = (_dot_nt(a, rkb) * mk).astype(BF16)
            stb = st_ref[h].astype(BF16)
            o = _dot(p, v) + _dot(aq, stb)
            _, vjp = jax.vjp(_gn_gate, o, g)
            do, dg = vjp(da_ref[:, DV * h:DV * (h + 1)].astype(F32))
            dob = do.astype(BF16)
            gs = gst[h]
            gsb = gs.astype(BF16)
            ds = (_dot_nt(dob, v) * mk).astype(BF16)
            drq = drq + _dot(ds, rkb) * mqs + _dot_nt(dob, stb) * qdm_ref[h]
            drk = drk + _dot_tn(ds, a) + _dot_nt(v, gsb) * kdm_ref[h]
            dv = _dot_tn(p, dob) + _dot(kdv, gsb)
            gst[h] = _dot_tn(aq, dob) + gs * cd_ref[h, 0:1, :]
            dp_ref[:, 2 * QK + DV * h:2 * QK + DV * (h + 1)] = dv.astype(BF16)
            dp_ref[:, 2 * QK + VW + DV * h:2 * QK + VW + DV * (h + 1)] = dg.astype(BF16)
        dp_ref[:, 128 * j:128 * (j + 1)] = _rot_t(drq, c, sl, sh).astype(BF16)
        dp_ref[:, QK + 128 * j:QK + 128 * (j + 1)] = _rot_t(drk, c, sl, sh).astype(BF16)


def _ret_bwd(dproj, da_ret, proj, states, tb, L, ride=None):
    T = proj.shape[0]
    nS = T // L

    def body(dpin_ref, dar_ref, p_ref, st_ref, rope_ref, m_ref, qdm_ref, kdm_ref, mqs_ref, cd_ref, dp_ref, gst):
        _zero_at_start(gst)
        _ret_bwd_part(dar_ref, p_ref, st_ref, rope_ref, m_ref, qdm_ref, kdm_ref, mqs_ref, cd_ref, dp_ref, gst)

    rev = lambda s: nS - 1 - s
    specs = _ret_specs(T, L)
    specs[0] = pl.BlockSpec((3, L, 128), lambda s: (0, rev(s), 0))
    ptile = pl.BlockSpec((L, PW), lambda s: (rev(s), 0))
    return _pcall(
        body, name="ret_bwd",
        out_shape=(jax.ShapeDtypeStruct(dproj.shape, BF16),),
        grid=(nS,),
        in_specs=[ANY, pl.BlockSpec((L, VW), lambda s: (rev(s), 0)), ptile,
                  pl.BlockSpec((None, HEADS, 128, DV), lambda s: (rev(s), 0, 0, 0))] + specs,
        out_specs=(ptile,),
        scratch_shapes=[pltpu.VMEM((HEADS, 128, DV), F32)],
        sem=("arbitrary",), aliases={0: 0}, ride=ride,
        args=(dproj, da_ret, proj, states, tb["rope"], tb["mask"], tb["qdm"], tb["kdm"], tb["mqs"], tb["cd"]))


def _conv_bwd(dproj, da_conv, y, proj, wcw, ln_g, ln_b, l, ride=None):
    T, C = y.shape
    Cc = wcw.shape[-1]
    tc = min(CONV_TILE, T)
    n = T // tc
    hb = tc // HALO

    def body(dpin_ref, dac_ref, y_ref, p_ref, ph_ref, cw_ref, lg_ref, lb_ref, dp_ref, sg_ref,
             dcbuf, ubuf, dubuf, wbuf, dwacc, vacc, dsh, ush):
        _conv_bwd_part(n, dac_ref, y_ref, p_ref, ph_ref, cw_ref, lg_ref, lb_ref, dp_ref,
                       dcbuf, ubuf, dubuf, wbuf, dwacc, vacc, dsh, ush)
        _conv_bwd_final(n, sg_ref, dwacc, vacc)

    rev = lambda t: n - 1 - t
    vec = pl.BlockSpec((None, 1, C), lambda t: (l, 0, 0))
    tile = pl.BlockSpec((tc, C), lambda t: (rev(t), 0))
    ptile = pl.BlockSpec((tc, 3 * C), lambda t: (rev(t), 1))
    halo = pl.BlockSpec((HALO, 3 * C), lambda t: (jnp.maximum(rev(t) * hb - 1, 0), 1))
    return _pcall(
        body, name="conv_bwd",
        out_shape=(jax.ShapeDtypeStruct(dproj.shape, BF16), jax.ShapeDtypeStruct((ROW_PRE, C), F32)),
        grid=(n,),
        in_specs=[ANY, tile, tile, ptile, halo, pl.BlockSpec((4, None, CONV_K, Cc), lambda t: (0, l, 0, 0)), vec, vec],
        out_specs=(ptile, pl.BlockSpec((ROW_PRE, C), lambda t: (0, 0))),
        scratch_shapes=[pltpu.VMEM((tc + HALO, C), F32), pltpu.VMEM((HALO + tc, C), F32), pltpu.VMEM((tc, C), F32),
                        pltpu.VMEM((8 * CONV_K, C), F32), pltpu.VMEM((8 * CONV_K, C), F32), pltpu.VMEM((8, C), F32),
                        pltpu.VMEM((7, HALO + tc - 8, C), F32), pltpu.VMEM((7, HALO + tc - 8, C), F32)],
        sem=("arbitrary",), aliases={0: 0}, ride=ride, args=(dproj, da_conv, y, proj, proj, wcw, ln_g, ln_b))


def _win_grad(h, dproj, W, ride=None):
    T, D = h.shape
    tk = min(2048, T)
    nk = T // tk

    def body(h_ref, dp_ref, g_ref, acc):
        k = pl.program_id(1)

        @pl.when(k == 0)
        def _():
            acc[...] = jnp.zeros_like(acc)

        acc[...] += _dot_tn(h_ref[...], dp_ref[...])

        @pl.when(k == nk - 1)
        def _():
            g_ref[...] = acc[...].astype(BF16)

    return _pcall(
        body, name="win_grad",
        out_shape=(jax.ShapeDtypeStruct((4, D, W), BF16),),
        grid=(4, nk),
        in_specs=[pl.BlockSpec((tk, D), lambda j, k: (k, 0)), pl.BlockSpec((tk, W), lambda j, k: (k, j))],
        out_specs=(pl.BlockSpec((None, D, W), lambda j, k: (j, 0, 0)),),
        scratch_shapes=[pltpu.VMEM((D, W), F32)],
        sem=("arbitrary", "arbitrary"), args=(h, dproj), ride=ride)


def _in_bwd(dxn, dproj, x, pre_g, win, l, ride=None):
    T, D = x.shape
    W = win.shape[-1]
    tm = min(512, T)

    def body(dxn_ref, dp_ref, x_ref, g_ref, w_ref, dx_ref, dg_ref):
        @pl.when(pl.program_id(0) == 0)
        def _():
            dg_ref[...] = jnp.zeros_like(dg_ref)

        dh = _dot_nt(dp_ref[:, 0:W], w_ref[0])
        for j in range(1, 4):
            dh = dh + _dot_nt(dp_ref[:, j * W:(j + 1) * W], w_ref[j])
        _, vjp = jax.vjp(_rms, x_ref[...], g_ref[...])
        dx, dg = vjp(dh)
        dx_ref[...] = dxn_ref[...] + dx
        dg_ref[0:1, :] += dg

    tile = pl.BlockSpec((tm, D), lambda i: (i, 0))
    return _pcall(
        body, name="in_bwd",
        out_shape=(jax.ShapeDtypeStruct((T, D), F32), jax.ShapeDtypeStruct((8, D), F32)),
        grid=(T // tm,),
        in_specs=[tile, pl.BlockSpec((tm, 4 * W), lambda i: (i, 0)), tile,
                  pl.BlockSpec((None, 1, D), lambda i: (l, 0, 0)),
                  pl.BlockSpec((4, D, W), lambda i: (0, 0, 0), pipeline_mode=pl.Buffered(1))],
        out_specs=(tile, pl.BlockSpec((8, D), lambda i: (0, 0))),
        sem=("arbitrary",), args=(dxn, dproj, x, pre_g, win), ride=ride)


def _sum_group(chip, t, u):
    _, A, B = t.shape
    tr = min(256, A)

    def body(k_ref, t_ref, u_ref, o_ref):
        o_ref[...] = ((t_ref[...].astype(F32) + u_ref[0].astype(F32)) + u_ref[1].astype(F32)) + u_ref[2].astype(F32)

    return pl.pallas_call(
        body, name="sum_group",
        out_shape=jax.ShapeDtypeStruct((A, B), F32),
        grid_spec=pltpu.PrefetchScalarGridSpec(
            num_scalar_prefetch=1, grid=(A // tr,),
            in_specs=[pl.BlockSpec((None, tr, B), lambda i, k: (k[0], i, 0)),
                      pl.BlockSpec((3, tr, B), lambda i, k: (0, i, 0))],
            out_specs=pl.BlockSpec((tr, B), lambda i, k: (i, 0))),
        compiler_params=_cp(("arbitrary",)),
    )(chip, t, u)


def _swap_rows(g):
    _, A, B = g.shape
    nh = A // 2

    def body(g_ref, r_ref, send_sems, recv_sems):
        x, y, c = _place()
        cp = _remote(g_ref.at[:, pl.ds((1 - c) * nh, nh)], r_ref, send_sems, recv_sems, 0, (x, y, 1 - c))
        cp.start()
        cp.wait()

    return pl.pallas_call(
        body, name="swap_rows",
        out_shape=jax.ShapeDtypeStruct((4, nh, B), g.dtype),
        in_specs=[ANY], out_specs=ANY,
        scratch_shapes=[pltpu.SemaphoreType.DMA((1,)), pltpu.SemaphoreType.DMA((1,))],
        compiler_params=pltpu.CompilerParams(has_side_effects=True),
    )(g)


def _add_rows(cidx, g, r):
    _, nh, B = r.shape
    tr = min(256, nh)
    nb = nh // tr

    def body(c_ref, g_ref, r_ref, o_ref):
        o_ref[...] = (g_ref[...].astype(F32) + r_ref[...].astype(F32)).astype(BF16)

    blk = (None, tr, B)
    return pl.pallas_call(
        body, name="add_rows",
        out_shape=jax.ShapeDtypeStruct(r.shape, BF16),
        grid_spec=pltpu.PrefetchScalarGridSpec(
            num_scalar_prefetch=1, grid=(4, nb),
            in_specs=[pl.BlockSpec(blk, lambda k, i, c: (k, c[0] * nb + i, 0)),
                      pl.BlockSpec(blk, lambda k, i, c: (k, i, 0))],
            out_specs=pl.BlockSpec(blk, lambda k, i, c: (k, i, 0))),
        compiler_params=_cp(("arbitrary", "arbitrary")),
    )(cidx, g, r)


def _sum_group_half(chip, cidx, t, u):
    _, nh, B = t.shape
    tr = min(256, nh)
    nb = nh // tr

    def body(k_ref, c_ref, t_ref, u_ref, o_ref):
        mine = (pl.program_id(0) // nb) == c_ref[0]

        @pl.when(mine)
        def _():
            o_ref[...] = ((t_ref[...].astype(F32) + u_ref[0].astype(F32)) + u_ref[1].astype(F32)) + u_ref[2].astype(F32)

        @pl.when(jnp.logical_not(mine))
        def _():
            o_ref[...] = jnp.zeros_like(o_ref)

    own = lambda i, c: jnp.clip(i - c[0] * nb, 0, nb - 1)
    return pl.pallas_call(
        body, name="sum_group_half",
        out_shape=jax.ShapeDtypeStruct((2 * nh, B), F32),
        grid_spec=pltpu.PrefetchScalarGridSpec(
            num_scalar_prefetch=2, grid=(2 * nb,),
            in_specs=[pl.BlockSpec((None, tr, B), lambda i, k, c: (k[0], own(i, c), 0)),
                      pl.BlockSpec((3, tr, B), lambda i, k, c: (0, own(i, c), 0))],
            out_specs=pl.BlockSpec((tr, B), lambda i, k, c: (i, 0))),
        compiler_params=_cp(("arbitrary",)),
    )(chip, cidx, t, u)


def _adam_math(w, g, m, v):
    c1 = 1.0 / (1.0 - ADAM_B1 ** ADAM_STEP)
    c2 = 1.0 / (1.0 - ADAM_B2 ** ADAM_STEP)
    nm = ADAM_B1 * m + (1.0 - ADAM_B1) * g
    nv = ADAM_B2 * v + (1.0 - ADAM_B2) * (g * g)
    return -ADAM_LR * ((nm * c1) / (jnp.sqrt(nv * c2) + ADAM_EPS) + ADAM_WD * w), nm, nv


def _adamw_layer(prev, w, m, v, sa, sb, l):
    NL, A, B = w.shape
    tr = A
    while tr * B * 4 > ADAM_BLOCK_BYTES and tr % 16 == 0:
        tr //= 2

    def body(p0, p1, p2, p3, w_ref, m_ref, v_ref, sa_ref, sb_ref, g_ref, d_ref, nm_ref, nv_ref):
        g = sa_ref[...] + sb_ref[...]
        g_ref[...] = g
        d_ref[...], nm_ref[...], nv_ref[...] = _adam_math(w_ref[...], g, m_ref[...], v_ref[...])

    lay = pl.BlockSpec((None, tr, B), lambda i: (l, i, 0))
    src = pl.BlockSpec((tr, B), lambda i: (i, 0))
    full = jax.ShapeDtypeStruct((NL, A, B), F32)
    if prev is None:
        prev = tuple(lax.empty((NL, A, B), F32) for _ in range(4))
    return tuple(pl.pallas_call(
        body, name="adamw_layer",
        out_shape=(full,) * 4, grid=(A // tr,),
        in_specs=[ANY] * 4 + [lay, lay, lay, src, src], out_specs=(lay,) * 4,
        input_output_aliases={0: 0, 1: 1, 2: 2, 3: 3},
        compiler_params=_cp(("arbitrary",)),
    )(*prev, w, m, v, sa, sb))


def _adamw_square(prev, wmv, sa, sb, l):
    NL, R, D = wmv[0][0].shape
    n = len(wmv)

    def body(*refs):
        ins, outs = refs[4 * n:4 * n + 3 * n + 2], refs[4 * n + 3 * n + 2:]
        sa_ref, sb_ref = ins[3 * n], ins[3 * n + 1]
        g = sa_ref[...] + sb_ref[...]
        for p in range(n):
            @pl.when(pl.program_id(0) == p)
            def _(p=p):
                w_ref, m_ref, v_ref = ins[3 * p:3 * p + 3]
                g_ref, d_ref, nm_ref, nv_ref = outs[4 * p:4 * p + 4]
                g_ref[...] = g
                d_ref[...], nm_ref[...], nv_ref[...] = _adam_math(w_ref[...], g, m_ref[...], v_ref[...])

    lay = pl.BlockSpec((None, R, D), lambda i: (l, 0, 0))
    src = pl.BlockSpec((R, D), lambda i: (i, 0))
    full = jax.ShapeDtypeStruct((NL, R, D), F32)
    if prev is None:
        prev = tuple(lax.empty((NL, R, D), F32) for _ in range(4 * n))
    flat = [a for t in wmv for a in t]
    outs = pl.pallas_call(
        body, name="adamw_square",
        out_shape=(full,) * (4 * n), grid=(n,),
        in_specs=[ANY] * (4 * n) + [lay] * (3 * n) + [src, src], out_specs=(lay,) * (4 * n),
        input_output_aliases={i: i for i in range(4 * n)},
        compiler_params=_cp(("arbitrary",)),
    )(*prev, *flat, sa, sb)
    return tuple(outs)


def _adamw(w, g, m, v):
    shape = w.shape
    cols = shape[-1]
    rows = int(np.prod(shape[:-1]))

    def body(w_ref, g_ref, m_ref, v_ref, d_ref, nm_ref, nv_ref):
        d_ref[...], nm_ref[...], nv_ref[...] = _adam_math(w_ref[...], g_ref[...], m_ref[...], v_ref[...])

    tile = pl.BlockSpec((rows, cols), lambda i: (0, 0))
    out = jax.ShapeDtypeStruct((rows, cols), F32)
    res = pl.pallas_call(
        body, name="adamw",
        out_shape=(out, out, out), grid=(1,),
        in_specs=[tile] * 4, out_specs=(tile,) * 3,
        compiler_params=_cp(("arbitrary",)),
    )(*[a.reshape(rows, cols) for a in (w, g, m, v)])
    return tuple(a.reshape(shape) for a in res)


def _tail_exchange(small, s_in, s_sq):
    def body(s_ref, a_ref, b_ref, o_ref, oa_ref, ob_ref, send_sems, recv_sems, local_sem):
        x, y, c = _place()
        me = 4 * x + 2 * y + c
        sibling = (x, y, 1 - c)
        mine = pltpu.make_async_copy(s_ref, o_ref.at[me], local_sem)
        mine.start()
        swaps = [_remote(a_ref, oa_ref, send_sems, recv_sems, 7, sibling), _remote(b_ref, ob_ref, send_sems, recv_sems, 8, sibling)]
        sends = [_remote(s_ref, o_ref.at[me], send_sems, recv_sems, r, peer) for r, peer in enumerate(_peers(x, y, c))]
        for cp in swaps + sends:
            cp.start()
        for r, peer in enumerate(_peers(x, y, c)):
            theirs = o_ref.at[4 * peer[0] + 2 * peer[1] + peer[2]]
            _remote(theirs, theirs, send_sems, recv_sems, r, peer).wait_recv()
        for cp in sends:
            cp.wait_send()
        for cp in swaps:
            cp.wait()
        mine.wait()

    return pl.pallas_call(
        body, name="tail_exchange",
        out_shape=(jax.ShapeDtypeStruct((8,) + small.shape, small.dtype),
                   jax.ShapeDtypeStruct(s_in.shape, s_in.dtype), jax.ShapeDtypeStruct(s_sq.shape, s_sq.dtype)),
        in_specs=[ANY] * 3, out_specs=(ANY,) * 3,
        scratch_shapes=[pltpu.SemaphoreType.DMA((9,)), pltpu.SemaphoreType.DMA((9,)), pltpu.SemaphoreType.DMA],
        compiler_params=pltpu.CompilerParams(has_side_effects=True),
    )(small, s_in, s_sq)


def _sum_devices(gs):
    NL = len(gs)
    _, R, D = gs[0].shape

    def body(*refs):
        o_ref = refs[NL]
        for l in range(NL):
            acc = refs[l][0]
            for k in range(1, 8):
                acc = acc + refs[l][k]
            o_ref[l] = acc

    return pl.pallas_call(
        body, name="sum_devices",
        out_shape=jax.ShapeDtypeStruct((NL, R, D), F32),
        grid=(1,),
        in_specs=[pl.BlockSpec((8, R, D), lambda i: (0, 0, 0))] * NL,
        out_specs=pl.BlockSpec((NL, R, D), lambda i: (0, 0, 0)),
        compiler_params=_cp(("arbitrary",)),
    )(*gs)


def kernel(x, pre_norm_g, w_in, w_ret_out, conv_w, conv_b, conv_ln_g, conv_ln_b, w_conv_out, w_o, post_norm_g, loss_target, m_pre_norm_g, m_w_in, m_w_ret_out, m_conv_w, m_conv_b, m_conv_ln_g, m_conv_ln_b, m_w_conv_out, m_w_o, m_post_norm_g, v_pre_norm_g, v_w_in, v_w_ret_out, v_conv_w, v_conv_b, v_conv_ln_g, v_conv_ln_b, v_w_conv_out, v_w_o, v_post_norm_g):
    NL, D, W = w_in.shape
    Cc = conv_w.shape[-1]
    T = x.shape[1]
    L = min(RET_BLOCK, T)
    tb = _tables(T, L)
    ax, ay, ac = _place()
    chip = (2 * ax + ay).astype(jnp.int32).reshape(1)
    cidx = ac.astype(jnp.int32).reshape(1)
    pre_g, cb, lg, lb, post_g = (a.reshape(NL, 1, D) for a in (pre_norm_g, conv_b, conv_ln_g, conv_ln_b, post_norm_g))

    win = [_cast_win(chip, w_in, l) for l in range(NL)]
    wsq = [_cast_wsq(chip, w_ret_out, w_conv_out, w_o, l) for l in range(NL)]
    win[0], wcw = _gather_first(win[0], _place_cw(chip, conv_w))

    saved = []
    xl = x[0]
    for l in range(NL):
        more = l + 1 < NL
        ride = _Ride()
        if more:
            _ride_gather_ici(ride, "win", win[l + 1], (0, 5, 8))
        (proj, h), got = _fwd_in(xl, pre_g, win[l], l, ride=ride)
        if more:
            win[l + 1] = got["win"]
        ride = _Ride()
        if l == 0:
            _ride_gather_ici(ride, "wsq0", wsq[0])
        (a_ret, states), got = _ret_fwd(proj, tb, L, ride=ride)
        if l == 0:
            wsq[0] = got["wsq0"]
        ride = _Ride()
        if more:
            _ride_gather_ici(ride, "win", win[l + 1], (5, 8, 8))
            _ride_gather_ici(ride, "wsq", wsq[l + 1])
        if l == 0:
            _ride_gather_pass(ride, "wsq0", wsq[0])
        (a_conv, y), got = _conv_fwd(proj, wcw, cb, lg, lb, l, ride=ride)
        if more:
            win[l + 1], wsq[l + 1] = got["win"], got["wsq"]
        if l == 0:
            wsq[0] = got["wsq0"]
        ride = _Ride()
        if more:
            _ride_gather_pass(ride, "win", win[l + 1])
            _ride_gather_pass(ride, "wsq", wsq[l + 1])
        (xn, ro, co, ym, z), got = _merge_fwd(xl, proj, a_ret, a_conv, wsq[l], post_g, l, ride=ride)
        if more:
            win[l + 1], wsq[l + 1] = got["win"], got["wsq"]
        saved.append((xl, proj, h, a_ret, states, a_conv, y, ro, co, ym, z))
        xl = xn
    dx, lsum = _loss_fwd_bwd(xl, loss_target[0])

    gin, gsq, uin, usq = [None] * NL, [None] * NL, [None] * NL, [None] * NL
    s_in, s_sq, o_in, o_sq = [None] * NL, [None] * NL, [None] * NL, [None] * NL
    small, gs = [None] * NL, [None] * NL
    for l in reversed(range(NL)):
        xin, proj, h, a_ret, states, a_conv, y, ro, co, ym, z = saved[l]
        (dproj, da_ret, da_conv, gsq[l], dpost), _ = _merge_bwd(dx, proj, a_ret, a_conv, ro, co, ym, z, wsq[l], post_g, l)
        ride = _Ride()
        if l + 1 < NL:
            _ride_exchange(ride, "gin", gin[l + 1], "uin", uin[l + 1],
                           [(2, WHOLE), (0, SECOND_HALF), (1, SECOND_HALF)])
        (dproj, sg), got = _conv_bwd(dproj, da_conv, y, proj, wcw, lg, lb, l, ride=ride)
        if l + 1 < NL:
            uin[l + 1] = got["uin"]
        ride = _Ride()
        _ride_exchange(ride, "gsq", gsq[l], "usq", None, [(0, WHOLE), (1, WHOLE), (2, WHOLE)])
        if l + 1 < NL:
            _ride_gather_all(ride, "small", small[l + 1], "gs")
        (dproj,), got = _ret_bwd(dproj, da_ret, proj, states, tb, L, ride=ride)
        usq[l] = got["usq"]
        if l + 1 < NL:
            gs[l + 1] = got["gs"]
        (gin[l],), _ = _win_grad(h, dproj, W)
        ride = _Ride()
        if l > 0:
            _ride_exchange(ride, "gin", gin[l], "uin", None, [(0, FIRST_HALF), (1, FIRST_HALF)])
        else:
            gin[0] = _add_rows(cidx, gin[0], _swap_rows(gin[0]))
            _ride_exchange(ride, "gin", gin[0], "uin", None, [(0, WHOLE), (1, WHOLE), (2, WHOLE)])
        if l + 1 < NL:
            s_in[l + 1] = _sum_group(chip, gin[l + 1], uin[l + 1])
            s_sq[l + 1] = _sum_group(chip, gsq[l + 1], usq[l + 1])
            _ride_swap(ride, "s_in", s_in[l + 1], "o_in")
            _ride_swap(ride, "s_sq", s_sq[l + 1], "o_sq")
        (dx, dpre), got = _in_bwd(dx, dproj, xin, pre_g, win[l], l, ride=ride)
        uin[l] = got["uin"]
        if l + 1 < NL:
            o_in[l + 1], o_sq[l + 1] = got["o_in"], got["o_sq"]
        small[l] = jnp.concatenate([sg, dpre, dpost, lsum if l == NL - 1 else jnp.zeros_like(lsum)], axis=0)
    grad_x = dx

    sq_names = ("w_ret_out", "w_conv_out", "w_o")
    sq_wmv = [(w_ret_out, m_w_ret_out, v_w_ret_out), (w_conv_out, m_w_conv_out, v_w_conv_out), (w_o, m_w_o, v_w_o)]
    s_in[0] = _sum_group_half(chip, cidx, gin[0], uin[0])
    s_sq[0] = _sum_group(chip, gsq[0], usq[0])
    gs[0], o_in[0], o_sq[0] = _tail_exchange(small[0], s_in[0], s_sq[0])
    big_in, square = None, None
    for l in reversed(range(NL)):
        big_in = _adamw_layer(big_in, w_in, m_w_in, v_w_in, s_in[l], o_in[l], l)
        square = _adamw_square(square, sq_wmv, s_sq[l], o_sq[l], l)
    big = {"w_in": big_in}
    for p, n in enumerate(sq_names):
        big[n] = square[4 * p:4 * p + 4]

    gsm = _sum_devices(gs)
    loss = jnp.sum(gsm[NL - 1, ROW_LOSS:ROW_LOSS + 8])
    grads = {
        "pre_norm_g": gsm[:, ROW_PRE], "conv_w": lax.dynamic_slice_in_dim(gsm[:, 0:CONV_K], chip[0] * Cc, Cc, axis=2),
        "conv_b": gsm[:, ROW_CB], "conv_ln_g": gsm[:, ROW_LG], "conv_ln_b": gsm[:, ROW_LB], "post_norm_g": gsm[:, ROW_POST],
    }
    weights = dict(pre_norm_g=pre_norm_g, conv_w=conv_w, conv_b=conv_b, conv_ln_g=conv_ln_g, conv_ln_b=conv_ln_b,
                   post_norm_g=post_norm_g)
    m1 = dict(pre_norm_g=m_pre_norm_g, conv_w=m_conv_w, conv_b=m_conv_b, conv_ln_g=m_conv_ln_g, conv_ln_b=m_conv_ln_b,
              post_norm_g=m_post_norm_g)
    m2 = dict(pre_norm_g=v_pre_norm_g, conv_w=v_conv_w, conv_b=v_conv_b, conv_ln_g=v_conv_ln_g, conv_ln_b=v_conv_ln_b,
              post_norm_g=v_post_norm_g)
    res = {n: (grads[n],) + _adamw(weights[n], grads[n], m1[n], m2[n]) for n in grads}
    res.update(big)
    order = ["pre_norm_g", "w_in", "w_ret_out", "conv_w", "conv_b", "conv_ln_g", "conv_ln_b", "w_conv_out", "w_o", "post_norm_g"]
    return (loss, grad_x[None], *[res[n][0] for n in order], *[res[n][1] for n in order],
            *[res[n][2] for n in order], *[res[n][3] for n in order])
```
